```python
import math
import jax
import jax.numpy as jnp
from jax import lax
import numpy as np

D_MODEL = 1024
BATCH = 16
SEQ = 256
DEPTH = 4
DEC_BATCH = 4
DEC_SEQ = 4096
PAST_LEN = 512

GRID_W = 64
ROPE_BASE = 10000.0
Q_BLOCK = 128
EPS = 1e-6
N_GROUPS = 4
D_MIX = D_MODEL
BRANCH = D_MIX // N_GROUPS
MLA_HEADS = 4
MLA_NOPE = 64
MLA_ROPE = 32
MLA_V = BRANCH // MLA_HEADS
MLA_Q_LORA = D_MODEL // 4
MLA_KV_LORA = D_MODEL // 8
MLA_SCALE = (MLA_NOPE + MLA_ROPE) ** -0.5
DIFF_HEADS = 4
DIFF_HD = BRANCH // (2 * DIFF_HEADS)
DIFF_SCALE = DIFF_HD ** -0.5
HGRN_HEADS = 4
HGRN_DK = BRANCH // HGRN_HEADS
HGRN_DV = BRANCH // HGRN_HEADS
HGRN_CHUNK = 64
HY_CH = BRANCH
HY_ORDER = 2
HY_EMB = 33
HY_BANDS = (HY_EMB - 1) // 2
HY_FH = 64
HY_DECAY_TARGET = 0.01
HY_FAST_DECAY = 0.3
HY_SLOW_DECAY = 1.5
IN_SPLITS = (MLA_Q_LORA, MLA_KV_LORA, MLA_ROPE, BRANCH,
             BRANCH, BRANCH, BRANCH, BRANCH,
             HGRN_HEADS * HGRN_DK, HGRN_HEADS * HGRN_DK, HGRN_HEADS * HGRN_DK, HGRN_HEADS * HGRN_DV, BRANCH,
             3 * HY_CH, BRANCH)
IN_COLS = sum(IN_SPLITS)

kernel_name = 'hybrid_diffusion_trunk_step'


def rms_norm(x, g):
    xf = x.astype(jnp.float32)
    y = xf * lax.rsqrt(jnp.mean(xf * xf, axis=-1, keepdims=True) + EPS)
    return (y * g.astype(jnp.float32)).astype(x.dtype)


def _rotate(x, ang):
    n = x.shape[-1] // 2
    shape = (x.shape[1],) + (1,) * (x.ndim - 3) + (n,)
    cos = jnp.cos(ang).reshape(shape).astype(x.dtype)
    sin = jnp.sin(ang).reshape(shape).astype(x.dtype)
    x1, x2 = x[..., :n], x[..., n:]
    return jnp.concatenate([x1 * cos - x2 * sin, x2 * cos + x1 * sin], axis=-1)


def rope2d(x, pos):
    half = x.shape[-1] // 2
    inv = ROPE_BASE ** (-jnp.arange(0, half, 2, dtype=jnp.float32) / half)
    row, col = pos
    return jnp.concatenate([_rotate(x[..., :half], row[:, None] * inv),
                            _rotate(x[..., half:], col[:, None] * inv)], axis=-1)


def blockwise(fn, q):
    B, L = q.shape[:2]
    qb = q.reshape((B, L // Q_BLOCK, Q_BLOCK) + q.shape[2:]).swapaxes(0, 1)
    out = lax.map(fn, qb)
    return out.swapaxes(0, 1).reshape((B, L) + out.shape[3:])


def attend(q, k, v, scale):
    def block(qb):
        s = jnp.einsum('bqhd,bkhd->bhqk', qb, k).astype(jnp.float32) * scale
        p = jax.nn.softmax(s, axis=-1).astype(v.dtype)
        return jnp.einsum('bhqk,bkhe->bqhe', p, v)
    return blockwise(block, q)


def diff_attend(q, k, v, lam):
    def block(qb):
        s = jnp.einsum('bqhjd,bkhjd->bhjqk', qb, k).astype(jnp.float32) * DIFF_SCALE
        p = jax.nn.softmax(s, axis=-1)
        w = (p[:, :, 0] - lam * p[:, :, 1]).astype(v.dtype)
        return jnp.einsum('bhqk,bkhe->bqhe', w, v)
    return blockwise(block, q)


def mla_expand(ckv, krope_h, w_ukv, k_nope_g):
    kv = (ckv @ w_ukv).reshape(ckv.shape[:2] + (MLA_HEADS, MLA_NOPE + MLA_V))
    k_nope = rms_norm(kv[..., :MLA_NOPE], k_nope_g)
    k = jnp.concatenate([k_nope, jnp.broadcast_to(krope_h, k_nope.shape[:-1] + (MLA_ROPE,))], axis=-1)
    return k, kv[..., MLA_NOPE:]


def hgrn_scan(q, k, v, logf, s0):
    B, L, H, _ = q.shape
    nc = L // HGRN_CHUNK

    def chunks(t):
        return t.reshape((B, nc, HGRN_CHUNK) + t.shape[2:]).swapaxes(0, 1)

    mask = jnp.tril(jnp.ones((HGRN_CHUNK, HGRN_CHUNK), bool))[None, :, :, None, None]

    def step(S, inp):
        qc, kc, vc, gc = inp
        b = jnp.cumsum(gc, axis=1)
        diff = jnp.where(mask, b[:, :, None] - b[:, None, :], 0.0)
        decay = jnp.where(mask, jnp.exp(diff), 0.0)
        a = jnp.einsum('bthd,bshd,btshd->bhts', qc, kc, decay)
        o = jnp.einsum('bhts,bshe->bthe', a, vc) + jnp.einsum('bthd,bhde->bthe', qc * jnp.exp(b), S)
        b_last = b[:, -1]
        S = jnp.exp(b_last)[..., None] * S + jnp.einsum('bshd,bshe->bhde', kc * jnp.exp(b_last[:, None] - b), vc)
        return S, o

    S, o = lax.scan(step, s0, (chunks(q), chunks(k), chunks(v), chunks(logf)))
    return o.swapaxes(0, 1).reshape(B, L, H, v.shape[-1]), S


def hgrn_mixer(q_raw, zf, zb, i_raw, lb_f, lb_b, s0, out_g):
    B, L, _ = q_raw.shape

    def heads(t):
        return t.astype(jnp.float32).reshape(B, L, HGRN_HEADS, -1)

    def gates(z, lb):
        z = z.astype(jnp.float32)
        f = lb + (1.0 - lb) * jax.nn.sigmoid(z)
        k = (1.0 - lb) * jax.nn.sigmoid(-z)
        return heads(jnp.log(f)), heads(k)

    q, v = heads(q_raw), heads(i_raw)
    gf, kf = gates(zf, lb_f)
    gb, kb = gates(zb, lb_b)
    s0 = s0.astype(jnp.float32)
    o_f, s_f = hgrn_scan(q, kf, v, gf, s0[:, 0])
    flip = lambda t: jnp.flip(t, axis=1)
    o_b, s_b = hgrn_scan(flip(q), flip(kb), flip(v), flip(gb), s0[:, 1])
    o = rms_norm(o_f + flip(o_b), out_g).reshape(B, L, BRANCH).astype(q_raw.dtype)
    return o, jnp.stack([s_f, s_b], axis=1).astype(q_raw.dtype)


def hyena_filters(L, w1, b1, w2, b2, w3, freq):
    t = jnp.linspace(0.0, 1.0, L, dtype=jnp.float32)[:, None]
    w = 2.0 * math.pi * jnp.arange(L, dtype=jnp.float32) / L
    f = jnp.linspace(1e-4, HY_BANDS - 1, HY_BANDS, dtype=jnp.float32)
    ang = w[:, None] * f[None, :]
    feats = jnp.concatenate([t, jnp.cos(ang), -jnp.sin(ang)], axis=-1)
    h = jnp.sin(freq[0] * (feats @ w1 + b1))
    h = jnp.sin(freq[1] * (h @ w2 + b2))
    h = (h @ w3).astype(jnp.float32).reshape(L, HY_ORDER, 2, HY_CH)
    min_decay = math.log(HY_DECAY_TARGET) / HY_SLOW_DECAY
    max_decay = math.log(HY_DECAY_TARGET) / HY_FAST_DECAY
    deltas = jnp.linspace(min_decay, max_decay, HY_CH, dtype=jnp.float32)
    window = jnp.exp(-t * jnp.abs(deltas))
    return h * window[:, None, None, :]


def long_conv(z, hf, hb):
    L = z.shape[1]
    filt = jnp.concatenate([hf, hb[::-1]], axis=0)
    zf = jnp.fft.rfft(z, n=2 * L, axis=1)
    ff = jnp.fft.rfft(filt, n=2 * L, axis=0)
    return jnp.fft.irfft(zf * ff[None], n=2 * L, axis=1)[:, :L]


def hyena_mixer(u, conv_w, conv_b, filt, bias):
    dtype = u.dtype
    up = jnp.pad(u, ((0, 0), (1, 1), (0, 0)))
    u = up[:, :-2] * conv_w[0] + up[:, 1:-1] * conv_w[1] + up[:, 2:] * conv_w[2] + conv_b
    v, x1, x2 = jnp.split(u.astype(jnp.float32), 3, axis=-1)
    z = v
    for n, xg in enumerate((x1, x2)):
        z = xg * (long_conv(z, filt[:, n, 0], filt[:, n, 1]) + z * bias[n])
    return z.astype(dtype)


def trunk_layer(x, cvec, l, W, ctx, pos):
    B, L, _ = x.shape
    mod = (jax.nn.silu(cvec) @ W['w_mod'][l] + W['b_mod'][l])[:, None, :]
    shift, scale, gate = jnp.split(mod, 3, axis=-1)
    h = rms_norm(x, W['norm_g'][l]) * (1 + scale) + shift
    split_at = np.cumsum(IN_SPLITS)[:-1].tolist()
    (cq, ckv_raw, krope_raw, g_a, dq, dk, dv, g_b,
     hq, hzf, hzb, hi, g_c, hu, g_d) = jnp.split(h @ W['w_in'][l], split_at, axis=-1)

    q = (rms_norm(cq, W['mla_q_norm_g'][l]) @ W['mla_w_uq'][l]).reshape(B, L, MLA_HEADS, MLA_NOPE + MLA_ROPE)
    q_nope = rms_norm(q[..., :MLA_NOPE], W['mla_nope_g'][l, 0])
    q_rope = rms_norm(q[..., MLA_NOPE:], W['mla_rope_g'][l, 0])
    ckv = rms_norm(ckv_raw, W['mla_kv_norm_g'][l])
    krope = rms_norm(krope_raw, W['mla_rope_g'][l, 1])
    krope_h = krope[:, :, None, :]
    if pos is not None:
        q_rope = rope2d(q_rope, pos)
        krope_h = rope2d(krope_h, pos)
    k_a, v_a = mla_expand(ckv, krope_h, W['mla_w_ukv'][l], W['mla_nope_g'][l, 1])
    if ctx is not None:
        k_c, v_c = mla_expand(ctx[0], ctx[1][:, :, None, :], W['mla_w_ukv'][l], W['mla_nope_g'][l, 1])
        k_a = jnp.concatenate([k_a, k_c], axis=1)
        v_a = jnp.concatenate([v_a, v_c], axis=1)
    out_a = attend(jnp.concatenate([q_nope, q_rope], axis=-1), k_a, v_a, MLA_SCALE).reshape(B, L, BRANCH)

    qd = rms_norm(dq.reshape(B, L, DIFF_HEADS, 2, DIFF_HD), W['diff_qk_g'][l, 0])
    kd = rms_norm(dk.reshape(B, L, DIFF_HEADS, 2, DIFF_HD), W['diff_qk_g'][l, 1])
    vd = dv.reshape(B, L, DIFF_HEADS, 2 * DIFF_HD)
    k_b, v_b = kd, vd
    if pos is not None:
        qd = rope2d(qd, pos)
        k_b = rope2d(kd, pos)
    if ctx is not None:
        k_b = jnp.concatenate([k_b, ctx[2]], axis=1)
        v_b = jnp.concatenate([v_b, ctx[3]], axis=1)
    lam_init = 0.8 - 0.6 * math.exp(-0.3 * l)
    lp = W['diff_lambda'][l].astype(jnp.float32)
    lam = jnp.exp(jnp.sum(lp[0] * lp[1])) - jnp.exp(jnp.sum(lp[2] * lp[3])) + lam_init
    out_b = rms_norm(diff_attend(qd, k_b, v_b, lam), W['diff_subln_g'][l]) * (1.0 - lam_init)
    out_b = out_b.reshape(B, L, BRANCH)

    s0 = jnp.zeros((B, 2, HGRN_HEADS, HGRN_DK, HGRN_DV), x.dtype) if ctx is None else ctx[4]
    out_c, states = hgrn_mixer(hq, hzf, hzb, hi, W['hgrn_lb'][0, l], W['hgrn_lb'][1, l], s0, W['hgrn_out_g'][l])

    filt = hyena_filters(L, W['hy_w1'][l], W['hy_b1'][l], W['hy_w2'][l], W['hy_b2'][l], W['hy_w3'][l], W['hy_sin_freq'][l])
    out_d = hyena_mixer(hu, W['hy_conv_w'][l], W['hy_conv_b'][l], filt, W['hy_bias'][l])

    y = jnp.concatenate([out_a, out_b, out_c, out_d], axis=-1) * jax.nn.silu(jnp.concatenate([g_a, g_b, g_c, g_d], axis=-1))
    x = x + gate * (y @ W['w_out'][l])
    if ctx is None:
        return x, (ckv, krope, kd, vd, states)
    return x


def setup_inputs(seed: int = 0) -> dict:
    key = jax.random.key(seed)
    ks = iter(jax.random.split(key, 48))

    def nrm(shape, s=1.0):
        return s * jax.random.normal(next(ks), shape, jnp.float32)

    def gain(shape):
        return 1.0 + 0.02 * nrm(shape)

    return {
        'x_prompt': nrm((BATCH, SEQ, D_MODEL)),
        'x_sample': nrm((DEC_BATCH, DEC_SEQ, D_MODEL)),
        'cache_mla_ckv': nrm((DEC_BATCH, DEPTH, PAST_LEN, MLA_KV_LORA)),
        'cache_mla_krope': nrm((DEC_BATCH, DEPTH, PAST_LEN, MLA_ROPE)),
        'cache_diff_k': nrm((DEC_BATCH, DEPTH, PAST_LEN, DIFF_HEADS, 2, DIFF_HD)),
        'cache_diff_v': nrm((DEC_BATCH, DEPTH, PAST_LEN, DIFF_HEADS, 2 * DIFF_HD)),
        'state_hgrn': nrm((DEC_BATCH, DEPTH, 2, HGRN_HEADS, HGRN_DK, HGRN_DV), 0.5),
        'c': nrm((DEC_BATCH, D_MODEL)),
        'c_ctx': nrm((D_MODEL,)),
        'norm_g': gain((DEPTH, D_MODEL)),
        'w_mod': nrm((DEPTH, D_MODEL, 3 * D_MODEL), 0.5 * D_MODEL ** -0.5),
        'b_mod': nrm((DEPTH, 3 * D_MODEL), 0.02),
        'w_in': nrm((DEPTH, D_MODEL, IN_COLS), D_MODEL ** -0.5),
        'w_out': nrm((DEPTH, D_MIX, D_MODEL), D_MIX ** -0.5),
        'mla_q_norm_g': gain((DEPTH, MLA_Q_LORA)),
        'mla_w_uq': nrm((DEPTH, MLA_Q_LORA, MLA_HEADS * (MLA_NOPE + MLA_ROPE)), MLA_Q_LORA ** -0.5),
        'mla_kv_norm_g': gain((DEPTH, MLA_KV_LORA)),
        'mla_w_ukv': nrm((DEPTH, MLA_KV_LORA, MLA_HEADS * (MLA_NOPE + MLA_V)), MLA_KV_LORA ** -0.5),
        'mla_nope_g': gain((DEPTH, 2, MLA_NOPE)),
        'mla_rope_g': gain((DEPTH, 2, MLA_ROPE)),
        'diff_qk_g': gain((DEPTH, 2, DIFF_HD)),
        'diff_lambda': nrm((DEPTH, 4, DIFF_HD), 0.1),
        'diff_subln_g': gain((DEPTH, 2 * DIFF_HD)),
        'hgrn_lb_logits': nrm((2, DEPTH, HGRN_HEADS * HGRN_DK), 0.5),
        'hgrn_out_g': gain((DEPTH, HGRN_DV)),
        'hy_conv_w': nrm((DEPTH, 3, 3 * HY_CH), 3 ** -0.5),
        'hy_conv_b': nrm((DEPTH, 3 * HY_CH), 0.02),
        'hy_w1': nrm((DEPTH, HY_EMB, HY_FH), HY_EMB ** -0.5),
        'hy_b1': nrm((DEPTH, HY_FH), 0.1),
        'hy_w2': nrm((DEPTH, HY_FH, HY_FH), HY_FH ** -0.5),
        'hy_b2': nrm((DEPTH, HY_FH), 0.1),
        'hy_w3': nrm((DEPTH, HY_FH, HY_ORDER * 2 * HY_CH), 0.003),
        'hy_sin_freq': gain((DEPTH, 2, HY_FH)),
        'hy_bias': nrm((DEPTH, HY_ORDER, HY_CH)),
    }


def reference(x_prompt, x_sample, cache_mla_ckv, cache_mla_krope, cache_diff_k, cache_diff_v, state_hgrn,
              c, c_ctx, norm_g, w_mod, b_mod, w_in, w_out, mla_q_norm_g, mla_w_uq, mla_kv_norm_g, mla_w_ukv,
              mla_nope_g, mla_rope_g, diff_qk_g, diff_lambda, diff_subln_g, hgrn_lb_logits, hgrn_out_g,
              hy_conv_w, hy_conv_b, hy_w1, hy_b1, hy_w2, hy_b2, hy_w3, hy_sin_freq, hy_bias):
    p = jax.nn.softmax(hgrn_lb_logits.astype(jnp.float32), axis=1)
    hgrn_lb = jnp.cumsum(p, axis=1) - p[:, :1]
    W = dict(norm_g=norm_g, w_mod=w_mod, b_mod=b_mod, w_in=w_in, w_out=w_out,
             mla_q_norm_g=mla_q_norm_g, mla_w_uq=mla_w_uq, mla_kv_norm_g=mla_kv_norm_g, mla_w_ukv=mla_w_ukv,
             mla_nope_g=mla_nope_g, mla_rope_g=mla_rope_g, diff_qk_g=diff_qk_g, diff_lambda=diff_lambda,
             diff_subln_g=diff_subln_g, hgrn_lb=hgrn_lb, hgrn_out_g=hgrn_out_g, hy_conv_w=hy_conv_w,
             hy_conv_b=hy_conv_b, hy_w1=hy_w1, hy_b1=hy_b1, hy_w2=hy_w2, hy_b2=hy_b2, hy_w3=hy_w3,
             hy_sin_freq=hy_sin_freq, hy_bias=hy_bias)

    y_prompt = x_prompt
    ctx_c = c_ctx[None, :]
    per_layer = []
    for l in range(DEPTH):
        y_prompt, new = trunk_layer(y_prompt, ctx_c, l, W, None, None)
        per_layer.append(new)
    new_mla_ckv = jnp.stack([s[0] for s in per_layer], axis=1)
    new_mla_krope = jnp.stack([s[1] for s in per_layer], axis=1)
    new_diff_k = jnp.stack([s[2] for s in per_layer], axis=1)
    new_diff_v = jnp.stack([s[3] for s in per_layer], axis=1)
    new_hgrn_state = jnp.stack([s[4] for s in per_layer], axis=1)

    L = x_sample.shape[1]
    rows = L // GRID_W
    row = jnp.repeat(jnp.arange(rows, dtype=jnp.float32), GRID_W)
    col = (jnp.arange(rows * GRID_W) % GRID_W).astype(jnp.float32)
    y_sample = x_sample
    for l in range(DEPTH):
        ctx = (cache_mla_ckv[:, l], cache_mla_krope[:, l], cache_diff_k[:, l], cache_diff_v[:, l], state_hgrn[:, l])
        y_sample = trunk_layer(y_sample, c, l, W, ctx, (row, col))

    return (y_prompt, y_sample, new_mla_ckv, new_mla_krope, new_diff_k, new_diff_v, new_hgrn_state)
```

```python
import functools
import math

import numpy as np
import jax
import jax.numpy as jnp
from jax import lax
from jax.experimental import pallas as pl
from jax.experimental.pallas import tpu as pltpu

F32 = jnp.float32
BF16 = jnp.bfloat16

D_MODEL = 1024
DEPTH = 4
GRID_W = 64
ROPE_BASE = 10000.0
EPS = 1e-6
BRANCH = 256
MLA_HEADS = 4
MLA_NOPE = 64
MLA_ROPE = 32
MLA_V = 64
MLA_Q_LORA = 256
MLA_KV_LORA = 128
MLA_SCALE = (MLA_NOPE + MLA_ROPE) ** -0.5
DIFF_HEADS = 4
DIFF_HD = 32
DIFF_SCALE = DIFF_HD ** -0.5
HGRN_HEADS = 4
HGRN_DK = 64
HGRN_CHUNK = 128
HGRN_LEVELS = 7
HY_CH = 256
HY_ORDER = 2
HY_EMB = 33
HY_BANDS = 16
HY_FH = 64
HY_DECAY_TARGET = 0.01
HY_FAST_DECAY = 0.3
HY_SLOW_DECAY = 1.5
IN_COLS = 4000

LANE = 128
VMEM_LIMIT = 52 * 1024 * 1024

P_CQ, P_CKV, P_KR, P_DQ, P_DK, P_DV = 0, 256, 384, 512, 768, 1024
P_HQ, P_HZF, P_HZB, P_HI, P_HU, P_GATE = 1280, 1536, 1792, 2048, 2304, 3072
P_COLS = 4096
KR_OFF = 64


def _in_col_perm():
    src = np.full((P_COLS,), IN_COLS, np.int32)

    def put(dst, lo, n):
        src[dst:dst + n] = np.arange(lo, lo + n)

    put(P_CQ, 0, 256)
    put(P_CKV, 256, 128)
    put(P_KR + KR_OFF, 384, 32)
    put(P_GATE, 416, 256)
    put(P_DQ, 672, 256)
    put(P_DK, 928, 256)
    put(P_DV, 1184, 256)
    put(P_GATE + 256, 1440, 256)
    put(P_HQ, 1696, 256)
    put(P_HZF, 1952, 256)
    put(P_HZB, 2208, 256)
    put(P_HI, 2464, 256)
    put(P_GATE + 512, 2720, 256)
    put(P_HU, 2976, 768)
    put(P_GATE + 768, 3744, 256)
    return src


def _cparams(sem):
    return pltpu.CompilerParams(dimension_semantics=sem, vmem_limit_bytes=VMEM_LIMIT)


def _bdot(a, b):
    return jnp.dot(a.astype(BF16), b.astype(BF16), preferred_element_type=F32)


def _nt(a, b):
    return lax.dot_general(a.astype(BF16), b.astype(BF16), (((1,), (1,)), ((), ())), preferred_element_type=F32)


def _split2(a):
    hi = a.astype(BF16)
    lo = (a - hi.astype(F32)).astype(BF16)
    return hi, lo


def _dot3(a, b):
    ah, al = _split2(a)
    bh, bl = _split2(b)
    d = functools.partial(jnp.dot, preferred_element_type=F32)
    return d(ah, bh) + d(ah, bl) + d(al, bh)


def _segsum(v, seg):
    hi, lo = _split2(v)
    d = functools.partial(jnp.dot, preferred_element_type=F32)
    return d(hi, seg) + d(lo, seg)


def _rms(x, g):
    return x * lax.rsqrt(jnp.mean(x * x, axis=-1, keepdims=True) + EPS) * g


def _swap8(x):
    w = x.shape[-1]
    lane = lax.broadcasted_iota(jnp.int32, x.shape, x.ndim - 1)
    up = pltpu.roll(x, w - 8, x.ndim - 1)
    dn = pltpu.roll(x, 8, x.ndim - 1)
    return jnp.where((lane & 15) < 8, up, dn)


def _tile_lanes(x, n):
    return x if n == 1 else jnp.concatenate([x] * n, axis=-1)


def _mod_kernel(c_ref, w_ref, b_ref, o_ref):
    c = c_ref[...]
    o_ref[0] = _dot3(c * jax.nn.sigmoid(c), w_ref[0]) + b_ref[0]


def _mod_all(cvecs, w_mod, b_mod):
    nt = 3
    return pl.pallas_call(
        _mod_kernel,
        grid=(DEPTH, nt),
        in_specs=[pl.BlockSpec((8, D_MODEL), lambda l, j: (0, 0)),
                  pl.BlockSpec((1, D_MODEL, D_MODEL), lambda l, j: (l, 0, j)),
                  pl.BlockSpec((1, 1, D_MODEL), lambda l, j: (l, 0, j))],
        out_specs=pl.BlockSpec((1, 8, D_MODEL), lambda l, j: (l, 0, j)),
        out_shape=jax.ShapeDtypeStruct((DEPTH, 8, 3 * D_MODEL), F32),
        compiler_params=_cparams(("arbitrary", "arbitrary")),
        name="mod",
    )(cvecs, w_mod, b_mod.reshape(DEPTH, 1, 3 * D_MODEL))


def _lb_kernel(x_ref, o_ref):
    x = x_ref[...]
    rows = [x[l:l + 1, :] for l in range(DEPTH)]
    m = functools.reduce(jnp.maximum, rows)
    e = [jnp.exp(r - m) for r in rows]
    tot = functools.reduce(lambda a, b: a + b, e)
    acc = jnp.zeros_like(tot)
    o_ref[0:1, :] = acc
    for l in range(1, DEPTH):
        acc = acc + e[l] / tot
        o_ref[l:l + 1, :] = acc


def _hgrn_lb(logits):
    flat = logits.transpose(1, 0, 2).reshape(DEPTH, 2 * BRANCH)
    lb = pl.pallas_call(
        _lb_kernel,
        out_shape=jax.ShapeDtypeStruct(flat.shape, F32),
        name="hgrn_lb",
    )(flat)
    return lb.reshape(DEPTH, 2, BRANCH).transpose(1, 0, 2)


def _inproj_kernel(x_ref, mod_ref, g_ref, w_ref, p_ref):
    h = _rms(x_ref[...], g_ref[...]) * (1.0 + mod_ref[0, 1:2, :]) + mod_ref[0, 0:1, :]
    p_ref[...] = jnp.dot(h.astype(BF16), w_ref[...], preferred_element_type=F32)


def _inproj(x2, mod, norm_g, w_in_p, seq_len):
    n = x2.shape[0]
    tm = 256
    per_batch = mod.shape[0] > 1
    tiles_per_seq = seq_len // tm
    mod_idx = (lambda i: (i // tiles_per_seq, 0, 0)) if per_batch else (lambda i: (0, 0, 0))
    return pl.pallas_call(
        _inproj_kernel,
        grid=(n // tm,),
        in_specs=[pl.BlockSpec((tm, D_MODEL), lambda i: (i, 0)),
                  pl.BlockSpec((1, 3, D_MODEL), mod_idx),
                  pl.BlockSpec((1, D_MODEL), lambda i: (0, 0)),
                  pl.BlockSpec((D_MODEL, P_COLS), lambda i: (0, 0))],
        out_specs=pl.BlockSpec((tm, P_COLS), lambda i: (i, 0)),
        out_shape=jax.ShapeDtypeStruct((n, P_COLS), F32),
        compiler_params=_cparams(("parallel",)),
        name="inproj",
    )(x2, mod, norm_g.reshape(1, D_MODEL), w_in_p)


def _mla_seg():
    sid = np.zeros((512,), np.int32)
    cnt = np.ones((512,), np.float32)
    for h in range(MLA_HEADS):
        b = 128 * h
        sid[b:b + 64] = 3 * h
        sid[b + 64:b + 96] = 3 * h + 1
        sid[b + 96:b + 128] = 3 * h + 2
        cnt[b:b + 64] = 1.0 / 64
        cnt[b + 64:b + 128] = 1.0 / 32
    seg = (sid[:, None] == sid[None, :]).astype(np.float32)
    return jnp.asarray(seg, BF16), jnp.asarray(cnt.reshape(1, 512))


def _mla_q_kernel(rope, cq_ref, ckv_ref, kr_ref, qng_ref, wuq_ref, kvg_ref, gq_ref, gkr_ref, seg_ref, cnt_ref,
                  *rest):
    if rope:
        cos_ref, sin_ref, q_ref, ckvn_ref, krp_ref = rest
    else:
        q_ref, ckvn_ref, krp_ref = rest
    cqn = _rms(cq_ref[...], qng_ref[...])
    q = _bdot(cqn, wuq_ref[...])
    ss = _segsum(q * q, seg_ref[...]) * cnt_ref[...]
    qn = q * lax.rsqrt(ss + EPS) * gq_ref[...]
    ckvn_ref[...] = _rms(ckv_ref[...], kvg_ref[...])
    kr = kr_ref[...]
    krn = kr * lax.rsqrt(jnp.sum(kr * kr, axis=-1, keepdims=True) * (1.0 / MLA_ROPE) + EPS) * gkr_ref[...]
    if rope:
        cos, sin = cos_ref[...], sin_ref[...]
        qn = qn * _tile_lanes(cos, MLA_HEADS) + _swap8(qn) * _tile_lanes(sin, MLA_HEADS)
        krn = krn * cos + _swap8(krn) * sin
    q_ref[...] = (qn * MLA_SCALE).astype(BF16)
    krp_ref[...] = krn


def _mla_q(p, lw, consts, seq_len, rope_tabs):
    n = p.shape[0]
    tm = 256
    rope = rope_tabs is not None
    full = lambda shape: pl.BlockSpec(shape, lambda i: (0,) * len(shape))
    in_specs = [pl.BlockSpec((tm, 256), lambda i: (i, P_CQ // 256)),
                pl.BlockSpec((tm, 128), lambda i: (i, P_CKV // 128)),
                pl.BlockSpec((tm, 128), lambda i: (i, P_KR // 128)),
                full((1, 256)), full((256, 512)), full((1, 128)), full((1, 512)), full((1, 128)),
                full((512, 512)), full((1, 512))]
    args = [p, p, p, lw["qn_g"], lw["w_uq"], lw["kvn_g"], lw["gq"], lw["gkr"], consts["seg512"], consts["cnt512"]]
    if rope:
        tps = seq_len // tm
        in_specs += [pl.BlockSpec((tm, 128), lambda i: (i % tps, 0))] * 2
        args += [rope_tabs["cos_mla"], rope_tabs["sin_mla"]]
    return pl.pallas_call(
        functools.partial(_mla_q_kernel, rope),
        grid=(n // tm,),
        in_specs=in_specs,
        out_specs=[pl.BlockSpec((tm, 512), lambda i: (i, 0)),
                   pl.BlockSpec((tm, 128), lambda i: (i, 0)),
                   pl.BlockSpec((tm, 128), lambda i: (i, 0))],
        out_shape=[jax.ShapeDtypeStruct((n, 512), BF16),
                   jax.ShapeDtypeStruct((n, 128), F32),
                   jax.ShapeDtypeStruct((n, 128), F32)],
        compiler_params=_cparams(("parallel",)),
        name="mla_q",
    )(*args)


def _mla_kv_kernel(ckvn_ref, krp_ref, wuk_ref, wuv_ref, gk_ref, seg_ref, cnt_ref, k_ref, v_ref):
    c = ckvn_ref[...].astype(BF16)
    kn = jnp.dot(c, wuk_ref[...], preferred_element_type=F32)
    ss = _segsum(kn * kn, seg_ref[...]) * cnt_ref[...]
    k = kn * lax.rsqrt(ss + EPS) * gk_ref[...] + _tile_lanes(krp_ref[...], MLA_HEADS)
    k_ref[...] = k.astype(BF16)
    v_ref[...] = jnp.dot(c, wuv_ref[...], preferred_element_type=F32).astype(BF16)


def _mla_kv(ckvn, krp, lw, consts):
    n = ckvn.shape[0]
    tm = 512
    full = lambda shape: pl.BlockSpec(shape, lambda i: (0,) * len(shape))
    return pl.pallas_call(
        _mla_kv_kernel,
        grid=(n // tm,),
        in_specs=[pl.BlockSpec((tm, 128), lambda i: (i, 0)), pl.BlockSpec((tm, 128), lambda i: (i, 0)),
                  full((128, 512)), full((128, 256)), full((1, 512)), full((512, 512)), full((1, 512))],
        out_specs=[pl.BlockSpec((tm, 512), lambda i: (i, 0)), pl.BlockSpec((tm, 256), lambda i: (i, 0))],
        out_shape=[jax.ShapeDtypeStruct((n, 512), BF16), jax.ShapeDtypeStruct((n, 256), BF16)],
        compiler_params=_cparams(("parallel",)),
        name="mla_kv",
    )(ckvn, krp, lw["w_uk"], lw["w_uv"], lw["gk"], consts["seg512"], consts["cnt512"])


def _mla_attn_kernel(q_ref, k_ref, vt_ref, o_ref):
    s = _nt(k_ref[0], q_ref[...])
    m = jnp.max(s, axis=0, keepdims=True)
    e = jnp.exp(s - m)
    l = jnp.sum(e, axis=0, keepdims=True)
    o = jnp.dot(vt_ref[0], e.astype(BF16), preferred_element_type=F32)
    o_ref[0] = o / l


def _mla_attn(q, k, vt, batch, seq_len):
    lk = k.shape[1]
    tq = 256
    nq = seq_len // tq
    return pl.pallas_call(
        _mla_attn_kernel,
        grid=(batch, MLA_HEADS, nq),
        in_specs=[pl.BlockSpec((tq, 128), lambda b, h, i: (b * nq + i, h)),
                  pl.BlockSpec((1, lk, 128), lambda b, h, i: (b, 0, h)),
                  pl.BlockSpec((1, MLA_V, lk), lambda b, h, i: (b, h, 0))],
        out_specs=pl.BlockSpec((1, MLA_V, tq), lambda b, h, i: (b, h, i)),
        out_shape=jax.ShapeDtypeStruct((batch, BRANCH, seq_len), F32),
        compiler_params=_cparams(("parallel", "parallel", "arbitrary")),
        name="mla_attn",
    )(q, k, vt)


def _seg_const(width, seg):
    sid = np.arange(width) // seg
    return jnp.asarray((sid[:, None] == sid[None, :]).astype(np.float32), BF16)


def _diff_prep_kernel(rope, dq_ref, dk_ref, gq_ref, gk_ref, seg_ref, *rest):
    if rope:
        cos_ref, sin_ref, q_ref, k_ref, kf_ref = rest
    else:
        q_ref, k_ref, kf_ref = rest
    seg = seg_ref[...]

    def norm(x, g):
        ss = _segsum(x * x, seg) * (1.0 / DIFF_HD)
        return x * lax.rsqrt(ss + EPS) * g

    q = norm(dq_ref[...], gq_ref[...])
    k = norm(dk_ref[...], gk_ref[...])
    kf_ref[...] = k
    if rope:
        cos, sin = cos_ref[...], sin_ref[...]
        q = q * cos + _swap8(q) * sin
        k = k * cos + _swap8(k) * sin
    q_ref[...] = (q * DIFF_SCALE).astype(BF16)
    k_ref[...] = k.astype(BF16)


def _diff_prep(p, lw, consts, seq_len, rope_tabs):
    n = p.shape[0]
    tm = 256
    rope = rope_tabs is not None
    full = lambda shape: pl.BlockSpec(shape, lambda i: (0,) * len(shape))
    in_specs = [pl.BlockSpec((tm, 256), lambda i: (i, P_DQ // 256)),
                pl.BlockSpec((tm, 256), lambda i: (i, P_DK // 256)),
                full((1, 256)), full((1, 256)), full((256, 256))]
    args = [p, p, lw["dgq"], lw["dgk"], consts["seg32"]]
    if rope:
        tps = seq_len // tm
        in_specs += [pl.BlockSpec((tm, 256), lambda i: (i % tps, 0))] * 2
        args += [rope_tabs["cos_diff"], rope_tabs["sin_diff"]]
    blk = pl.BlockSpec((tm, 256), lambda i: (i, 0))
    return pl.pallas_call(
        functools.partial(_diff_prep_kernel, rope),
        grid=(n // tm,),
        in_specs=in_specs,
        out_specs=[blk, blk, blk],
        out_shape=[jax.ShapeDtypeStruct((n, 256), BF16), jax.ShapeDtypeStruct((n, 256), BF16),
                   jax.ShapeDtypeStruct((n, 256), F32)],
        compiler_params=_cparams(("parallel",)),
        name="diff_prep",
    )(*args)


def _diff_attn_kernel(lam_init, q_ref, k_ref, vt_ref, lp_ref, g_ref, o_ref):
    h = pl.program_id(1)
    q = q_ref[...]
    k = k_ref[0]
    vt = vt_ref[0]
    lane = lax.broadcasted_iota(jnp.int32, q.shape, 1)
    base = (h % 2) * 64
    zero = jnp.zeros_like(q)

    def one_map(j):
        lo = base + 32 * j
        qj = jnp.where((lane >= lo) & (lane < lo + 32), q, zero)
        s = _nt(k, qj)
        e = jnp.exp(s - jnp.max(s, axis=0, keepdims=True))
        l = jnp.sum(e, axis=0, keepdims=True)
        return jnp.dot(vt, e.astype(BF16), preferred_element_type=F32) / l

    lp = lp_ref[...]
    lam = (jnp.exp(jnp.sum(lp[0:1] * lp[1:2], axis=1, keepdims=True))
           - jnp.exp(jnp.sum(lp[2:3] * lp[3:4], axis=1, keepdims=True)) + lam_init)
    o = one_map(0) - lam * one_map(1)
    ms = jnp.mean(o * o, axis=0, keepdims=True)
    o_ref[0] = o * lax.rsqrt(ms + EPS) * g_ref[...] * (1.0 - lam_init)


def _diff_attn(q, k, vt, lp, g_col, lam_init, batch, seq_len):
    lk = k.shape[1]
    tq = 256
    nq = seq_len // tq
    return pl.pallas_call(
        functools.partial(_diff_attn_kernel, lam_init),
        grid=(batch, DIFF_HEADS, nq),
        in_specs=[pl.BlockSpec((tq, 128), lambda b, h, i: (b * nq + i, h // 2)),
                  pl.BlockSpec((1, lk, 128), lambda b, h, i: (b, 0, h // 2)),
                  pl.BlockSpec((1, 64, lk), lambda b, h, i: (b, h, 0)),
                  pl.BlockSpec((4, DIFF_HD), lambda b, h, i: (0, 0)),
                  pl.BlockSpec((64, 1), lambda b, h, i: (0, 0))],
        out_specs=pl.BlockSpec((1, 64, tq), lambda b, h, i: (b, h, i)),
        out_shape=jax.ShapeDtypeStruct((batch, BRANCH, seq_len), F32),
        compiler_params=_cparams(("parallel", "parallel", "arbitrary")),
        name="diff_attn",
    )(q, k, vt, lp, g_col)


def _hgrn_consts():
    c = HGRN_CHUNK
    t = np.arange(c)
    low = (t[None, :] <= t[:, None]).astype(np.float32)
    blocks = []
    for j in range(HGRN_LEVELS):
        m = 1 << j
        rho = (t // (2 * m)) * (2 * m) + m - 1
        blocks.append(low - (t[None, :] <= rho[:, None]).astype(np.float32))
    blocks.append(low)
    fwd = np.concatenate(blocks, axis=0)
    bwd = np.concatenate([b[::-1, ::-1] for b in blocks], axis=0)
    return jnp.asarray(np.stack([fwd, bwd]), BF16)


def _hgrn_kernel(nc, q_ref, z_ref, v_ref, lb_ref, dd_ref, s0_ref, o_ref, sout_ref, st_ref):
    c = HGRN_CHUNK
    d = pl.program_id(1)
    ci = pl.program_id(2)

    @pl.when(ci == 0)
    def _():
        st_ref[...] = s0_ref[0, 0]

    lb = lb_ref[0]
    z = z_ref[...]
    q = q_ref[...]
    v = v_ref[...]
    f = lb + (1.0 - lb) * jax.nn.sigmoid(z)
    g = jnp.log(f)
    kk = (1.0 - lb) * jax.nn.sigmoid(-z)

    gh, gl = _split2(g)
    dd = dd_ref[0]
    u = jnp.dot(dd, gh, preferred_element_type=F32) + jnp.dot(dd, gl, preferred_element_type=F32)

    row = lax.broadcasted_iota(jnp.int32, (c, 1), 0)
    row_dir = jnp.where(d == 0, row, c - 1 - row)
    lane = lax.broadcasted_iota(jnp.int32, (1, BRANCH), 1)
    head_masks = [(lane >= HGRN_DK * h) & (lane < HGRN_DK * (h + 1)) for h in range(HGRN_HEADS)]
    t_idx = lax.broadcasted_iota(jnp.int32, (c, HGRN_HEADS * c), 0)
    s_idx = lax.broadcasted_iota(jnp.int32, (c, HGRN_HEADS * c), 1) & (c - 1)

    def stack_heads(x):
        xb = x.astype(BF16)
        zero = jnp.zeros_like(xb)
        return jnp.concatenate([jnp.where(hm, xb, zero) for hm in head_masks], axis=0)

    a = jnp.where(t_idx == s_idx, _nt(q, stack_heads(kk)), 0.0)
    for j in range(HGRN_LEVELS):
        e = jnp.exp(-jnp.abs(u[j * c:(j + 1) * c]))
        right = ((row_dir >> j) & 1) == 1
        qt = jnp.where(right, q * e, 0.0)
        kt = jnp.where(right, 0.0, kk * e)
        same = (t_idx >> (j + 1)) == (s_idx >> (j + 1))
        a = a + jnp.where(same, _nt(qt, stack_heads(kt)), 0.0)

    b = u[HGRN_LEVELS * c:]
    b_tot = jnp.where(d == 0, b[c - 1:c], b[0:1])
    st = st_ref[...]
    o = jnp.dot(a.astype(BF16), stack_heads(v), preferred_element_type=F32)
    o = o + _nt(q * jnp.exp(b), st)
    o_ref[0] = o

    kd = kk * jnp.exp(b_tot - b)
    upd = lax.dot_general(v.astype(BF16), kd.astype(BF16), (((0,), (0,)), ((), ())), preferred_element_type=F32)
    r2 = lax.broadcasted_iota(jnp.int32, (BRANCH, BRANCH), 0) // HGRN_DK
    c2 = lax.broadcasted_iota(jnp.int32, (BRANCH, BRANCH), 1) // HGRN_DK
    st_new = st * jnp.exp(b_tot) + jnp.where(r2 == c2, upd, 0.0)
    st_ref[...] = st_new

    @pl.when(ci == nc - 1)
    def _():
        sout_ref[0, 0] = st_new


def _hgrn(p, lb_l, dd, st0, batch, seq_len):
    n = p.shape[0]
    c = HGRN_CHUNK
    nc = seq_len // c

    def rows(b, d, i):
        return b * nc + i + d * (nc - 1 - 2 * i)

    return pl.pallas_call(
        functools.partial(_hgrn_kernel, nc),
        grid=(batch, 2, nc),
        in_specs=[pl.BlockSpec((c, 256), lambda b, d, i: (rows(b, d, i), P_HQ // 256)),
                  pl.BlockSpec((c, 256), lambda b, d, i: (rows(b, d, i), P_HZF // 256 + d)),
                  pl.BlockSpec((c, 256), lambda b, d, i: (rows(b, d, i), P_HI // 256)),
                  pl.BlockSpec((1, 1, 256), lambda b, d, i: (d, 0, 0)),
                  pl.BlockSpec((1, (HGRN_LEVELS + 1) * c, c), lambda b, d, i: (d, 0, 0)),
                  pl.BlockSpec((1, 1, 256, 256), lambda b, d, i: (b, d, 0, 0))],
        out_specs=[pl.BlockSpec((1, c, 256), lambda b, d, i: (d, rows(b, d, i), 0)),
                   pl.BlockSpec((1, 1, 256, 256), lambda b, d, i: (b, d, 0, 0))],
        out_shape=[jax.ShapeDtypeStruct((2, n, 256), F32),
                   jax.ShapeDtypeStruct((batch, 2, 256, 256), F32)],
        scratch_shapes=[pltpu.VMEM((256, 256), F32)],
        compiler_params=_cparams(("parallel", "parallel", "arbitrary")),
        name="hgrn",
    )(p, p, p, lb_l, dd, st0)


def _hy_conv3_kernel(prev_ref, cur_ref, nxt_ref, w_ref, b_ref, v_ref, x1_ref, x2_ref):
    w = w_ref[...]
    u = prev_ref[...] * w[0:1] + cur_ref[...] * w[1:2] + nxt_ref[...] * w[2:3] + b_ref[...]
    v_ref[...] = u[:, 0:256]
    x1_ref[...] = u[:, 256:512]
    x2_ref[...] = u[:, 512:768]


def _hy_conv3(prev, cur, nxt, w, b):
    n = cur.shape[0]
    tm = 512
    blk = pl.BlockSpec((tm, 768), lambda i: (i, 0))
    oblk = pl.BlockSpec((tm, 256), lambda i: (i, 0))
    return pl.pallas_call(
        _hy_conv3_kernel,
        grid=(n // tm,),
        in_specs=[blk, blk, blk, pl.BlockSpec((3, 768), lambda i: (0, 0)), pl.BlockSpec((1, 768), lambda i: (0, 0))],
        out_specs=[oblk, oblk, oblk],
        out_shape=[jax.ShapeDtypeStruct((n, 256), F32)] * 3,
        compiler_params=_cparams(("parallel",)),
        name="hy_conv3",
    )(prev, cur, nxt, w, b)


def _hy_filter_kernel(feat_ref, w1_ref, b1_ref, w2_ref, b2_ref, w3_ref, fr_ref, win_ref, o_ref):
    fr = fr_ref[...]
    h = jnp.sin(fr[0:1] * (_dot3(feat_ref[...], w1_ref[...]) + b1_ref[...]))
    h = jnp.sin(fr[1:2] * (_dot3(h, w2_ref[...]) + b2_ref[...]))
    o_ref[...] = _dot3(h, w3_ref[...]) * _tile_lanes(win_ref[...], 2 * HY_ORDER)


def _hy_filter(feats, window, w1p, b1, w2, b2, w3, freq):
    ln = feats.shape[0]
    tm = min(512, ln)
    full = lambda shape: pl.BlockSpec(shape, lambda i: (0,) * len(shape))
    return pl.pallas_call(
        _hy_filter_kernel,
        grid=(ln // tm,),
        in_specs=[pl.BlockSpec((tm, LANE), lambda i: (i, 0)),
                  full((LANE, HY_FH)), full((1, HY_FH)), full((HY_FH, HY_FH)), full((1, HY_FH)),
                  full((HY_FH, 2 * HY_ORDER * HY_CH)), full((2, HY_FH)),
                  pl.BlockSpec((tm, HY_CH), lambda i: (i, 0))],
        out_specs=pl.BlockSpec((tm, 2 * HY_ORDER * HY_CH), lambda i: (i, 0)),
        out_shape=jax.ShapeDtypeStruct((ln, 2 * HY_ORDER * HY_CH), F32),
        compiler_params=_cparams(("parallel",)),
        name="hy_filter",
    )(feats, w1p, b1, w2, b2, w3, freq, window)


def _dft_fwd_kernel(nb, with_filter, f_ref, z_ref, *rest):
    if with_filter:
        h_ref, o_ref = rest
    else:
        (o_ref,) = rest
    f = f_ref[...]
    tb = f.shape[0] // 2
    for b in range(nb):
        s = jnp.dot(f, z_ref[b], preferred_element_type=F32)
        if with_filter:
            zr, zi = s[:tb], s[tb:]
            hr, hi = h_ref[0, 0], h_ref[0, 1]
            o_ref[b, 0, 0] = (zr * hr - zi * hi).astype(o_ref.dtype)
            o_ref[b, 0, 1] = (zr * hi + zi * hr).astype(o_ref.dtype)
        else:
            o_ref[b, 0, 0] = s[:tb].astype(o_ref.dtype)
            o_ref[b, 0, 1] = s[tb:].astype(o_ref.dtype)


def _dft_fwd(fmat, z, spec, tb, out_dtype):
    nb, kdim, ch = z.shape
    nt = fmat.shape[0] // (2 * tb)
    in_specs = [pl.BlockSpec((2 * tb, kdim), lambda j: (j, 0)),
                pl.BlockSpec((nb, kdim, ch), lambda j: (0, 0, 0))]
    args = [fmat, z]
    if spec is not None:
        in_specs.append(pl.BlockSpec((1, 2, tb, ch), lambda j: (j, 0, 0, 0)))
        args.append(spec)
    return pl.pallas_call(
        functools.partial(_dft_fwd_kernel, nb, spec is not None),
        grid=(nt,),
        in_specs=in_specs,
        out_specs=pl.BlockSpec((nb, 1, 2, tb, ch), lambda j: (0, j, 0, 0, 0)),
        out_shape=jax.ShapeDtypeStruct((nb, nt, 2, tb, ch), out_dtype),
        compiler_params=_cparams(("parallel",)),
        name="dft_fwd",
    )(*args)


def _dft_inv_kernel(inv_ref, y_ref, x_ref, z_ref, bias_ref, o_ref):
    conv = jnp.dot(inv_ref[...], y_ref[0], preferred_element_type=F32)
    z = z_ref[0]
    o_ref[0] = x_ref[0] * (conv + z * bias_ref[...])


def _dft_inv(inv, yspec, xg, z, bias):
    nb, ln, ch = z.shape
    kdim = inv.shape[1]
    tt = min(256, ln)
    blk = pl.BlockSpec((1, tt, ch), lambda b, i: (b, i, 0))
    return pl.pallas_call(
        _dft_inv_kernel,
        grid=(nb, ln // tt),
        in_specs=[pl.BlockSpec((tt, kdim), lambda b, i: (i, 0)),
                  pl.BlockSpec((1, kdim, ch), lambda b, i: (b, 0, 0)),
                  blk, blk, pl.BlockSpec((1, ch), lambda b, i: (0, 0))],
        out_specs=blk,
        out_shape=jax.ShapeDtypeStruct((nb, ln, ch), F32),
        compiler_params=_cparams(("parallel", "arbitrary")),
        name="dft_inv",
    )(inv, yspec, xg, z, bias)


def _dft_tables(ln, tb):
    n = 2 * ln
    nbins = -(-(ln + 1) // tb) * tb
    nt = nbins // tb
    k = jnp.arange(nbins, dtype=jnp.int32)
    n1 = jnp.arange(n // 64, dtype=jnp.int32)
    n0 = jnp.arange(64, dtype=jnp.int32)
    a1 = (2.0 * math.pi / n) * ((k[:, None] * (64 * n1)[None, :]) % n).astype(F32)
    a0 = (2.0 * math.pi / n) * ((k[:, None] * n0[None, :]) % n).astype(F32)
    c1, s1 = jnp.cos(a1)[:, :, None], jnp.sin(a1)[:, :, None]
    c0, s0 = jnp.cos(a0)[:, None, :], jnp.sin(a0)[:, None, :]
    valid = (k <= ln).astype(F32)[:, None]
    cosm = (c1 * c0 - s1 * s0).reshape(nbins, n) * valid
    sinm = (s1 * c0 + c1 * s0).reshape(nbins, n) * valid
    fwd = jnp.stack([cosm.reshape(nt, tb, n), -sinm.reshape(nt, tb, n)], axis=1)
    wk = jnp.where((k == 0) | (k == ln), 1.0 / n, 2.0 / n).astype(F32)
    inv = (fwd[..., :ln] * wk.reshape(nt, 1, tb, 1)).reshape(nt * 2 * tb, ln).T
    return fwd.reshape(nt * 2 * tb, n).astype(BF16), inv.astype(BF16)


def _hy_static(ln):
    t = jnp.linspace(0.0, 1.0, ln, dtype=F32)[:, None]
    w = 2.0 * math.pi * jnp.arange(ln, dtype=F32) / ln
    f = jnp.linspace(1e-4, HY_BANDS - 1, HY_BANDS, dtype=F32)
    ang = w[:, None] * f[None, :]
    feats = jnp.concatenate([t, jnp.cos(ang), -jnp.sin(ang)], axis=-1)
    feats = jnp.pad(feats, ((0, 0), (0, LANE - HY_EMB)))
    min_decay = math.log(HY_DECAY_TARGET) / HY_SLOW_DECAY
    max_decay = math.log(HY_DECAY_TARGET) / HY_FAST_DECAY
    deltas = jnp.linspace(min_decay, max_decay, HY_CH, dtype=F32)
    window = jnp.exp(-t * jnp.abs(deltas))
    return feats, window


def _outproj_kernel(x_ref, mod_ref, oa_ref, ob_ref, of_ref, obk_ref, od_ref, gate_ref, hg_ref, seg_ref, w_ref,
                    y_ref):
    gt = gate_ref[...]
    sg = gt * jax.nn.sigmoid(gt)
    oc = of_ref[0] + obk_ref[0]
    ss = _segsum(oc * oc, seg_ref[...]) * (1.0 / HGRN_DK)
    oc = oc * lax.rsqrt(ss + EPS) * hg_ref[...]
    acc = _bdot(oa_ref[...] * sg[:, 0:256], w_ref[0:256, :])
    acc += _bdot(ob_ref[...] * sg[:, 256:512], w_ref[256:512, :])
    acc += _bdot(oc * sg[:, 512:768], w_ref[512:768, :])
    acc += _bdot(od_ref[...] * sg[:, 768:1024], w_ref[768:1024, :])
    y_ref[...] = x_ref[...] + mod_ref[0, 2:3, :] * acc


def _outproj(x2, mod, out_a, out_b, o_hgrn, out_d, p, hg, seg64, w_out, seq_len):
    n = x2.shape[0]
    tm = 256
    per_batch = mod.shape[0] > 1
    tps = seq_len // tm
    mod_idx = (lambda i: (i // tps, 0, 0)) if per_batch else (lambda i: (0, 0, 0))
    b256 = pl.BlockSpec((tm, 256), lambda i: (i, 0))
    return pl.pallas_call(
        _outproj_kernel,
        grid=(n // tm,),
        in_specs=[pl.BlockSpec((tm, D_MODEL), lambda i: (i, 0)),
                  pl.BlockSpec((1, 3, D_MODEL), mod_idx),
                  b256, b256,
                  pl.BlockSpec((1, tm, 256), lambda i: (0, i, 0)),
                  pl.BlockSpec((1, tm, 256), lambda i: (1, i, 0)),
                  b256,
                  pl.BlockSpec((tm, 1024), lambda i: (i, P_GATE // 1024)),
                  pl.BlockSpec((1, 256), lambda i: (0, 0)),
                  pl.BlockSpec((256, 256), lambda i: (0, 0)),
                  pl.BlockSpec((D_MODEL, D_MODEL), lambda i: (0, 0))],
        out_specs=pl.BlockSpec((tm, D_MODEL), lambda i: (i, 0)),
        out_shape=jax.ShapeDtypeStruct((n, D_MODEL), F32),
        compiler_params=_cparams(("parallel",)),
        name="outproj",
    )(x2, mod, out_a, out_b, o_hgrn, o_hgrn, out_d, p, hg, seg64, w_out)


def _layer(x2, mod, lw, consts, batch, seq_len, ctx, rope_tabs, hy):
    n = batch * seq_len
    p = _inproj(x2, mod, lw["norm_g"], lw["w_in"], seq_len)

    q_a, ckvn, krp = _mla_q(p, lw, consts, seq_len, rope_tabs)
    ckv3 = ckvn.reshape(batch, seq_len, MLA_KV_LORA)
    krp3 = krp.reshape(batch, seq_len, LANE)
    if ctx is not None:
        cache_kr = jnp.pad(ctx[1], ((0, 0), (0, 0), (KR_OFF, LANE - KR_OFF - MLA_ROPE)))
        ckv_all = jnp.concatenate([ckv3, ctx[0]], axis=1)
        kr_all = jnp.concatenate([krp3, cache_kr], axis=1)
    else:
        ckv_all, kr_all = ckv3, krp3
    lk = ckv_all.shape[1]
    k_a, v_a = _mla_kv(ckv_all.reshape(batch * lk, MLA_KV_LORA), kr_all.reshape(batch * lk, LANE), lw, consts)
    vt_a = v_a.reshape(batch, lk, BRANCH).transpose(0, 2, 1)
    ot_a = _mla_attn(q_a, k_a.reshape(batch, lk, 512), vt_a, batch, seq_len)
    out_a = ot_a.transpose(0, 2, 1).reshape(n, BRANCH)

    q_b, k_b, kd = _diff_prep(p, lw, consts, seq_len, rope_tabs)
    dv = p[:, P_DV:P_DV + BRANCH]
    k_b3 = k_b.reshape(batch, seq_len, BRANCH)
    v_b3 = dv.astype(BF16).reshape(batch, seq_len, BRANCH)
    if ctx is not None:
        k_b3 = jnp.concatenate([k_b3, ctx[2].reshape(batch, -1, BRANCH).astype(BF16)], axis=1)
        v_b3 = jnp.concatenate([v_b3, ctx[3].reshape(batch, -1, BRANCH).astype(BF16)], axis=1)
    ot_b = _diff_attn(q_b, k_b3, v_b3.transpose(0, 2, 1), lw["diff_lambda"], lw["subln_col"], lw["lam_init"],
                      batch, seq_len)
    out_b = ot_b.transpose(0, 2, 1).reshape(n, BRANCH)

    if ctx is not None:
        s0 = ctx[4]
    else:
        s0 = jnp.zeros((batch, 2, HGRN_HEADS, HGRN_DK, HGRN_DK), F32)
    eye = jnp.eye(HGRN_HEADS, dtype=F32)
    st0 = jnp.einsum("bdhke,hg->bdhegk", s0, eye).reshape(batch, 2, BRANCH, BRANCH)
    o_hgrn, st_out = _hgrn(p, lw["hgrn_lb"], consts["hgrn_dd"], st0, batch, seq_len)
    st5 = st_out.reshape(batch, 2, HGRN_HEADS, HGRN_DK, HGRN_HEADS, HGRN_DK)
    states = jnp.stack([st5[:, :, h, :, h, :] for h in range(HGRN_HEADS)], axis=2).swapaxes(-1, -2)

    hu = p[:, P_HU:P_HU + 3 * HY_CH].reshape(batch, seq_len, 3 * HY_CH)
    hu_pad = jnp.pad(hu, ((0, 0), (1, 1), (0, 0)))
    v_d, x1, x2g = _hy_conv3(hu_pad[:, :-2].reshape(n, -1), hu.reshape(n, -1), hu_pad[:, 2:].reshape(n, -1),
                             lw["hy_conv_w"], lw["hy_conv_b"])
    filt = _hy_filter(hy["feats"], hy["window"], lw["hy_w1"], lw["hy_b1"], lw["hy_w2"], lw["hy_b2"], lw["hy_w3"],
                      lw["hy_freq"])
    taps = []
    for o in range(HY_ORDER):
        hf = filt[:, (2 * o) * HY_CH:(2 * o + 1) * HY_CH]
        hb = filt[:, (2 * o + 1) * HY_CH:(2 * o + 2) * HY_CH]
        taps.append(jnp.concatenate([hf, hb[::-1]], axis=0))
    taps = jnp.concatenate(taps, axis=1).astype(BF16)[None]
    tb = hy["tb"]
    spec = _dft_fwd(hy["fwd_full"], taps, None, tb, F32)[0]
    z = v_d.reshape(batch, seq_len, HY_CH)
    for o, xg in enumerate((x1, x2g)):
        yspec = _dft_fwd(hy["fwd_half"], z.astype(BF16), spec[..., o * HY_CH:(o + 1) * HY_CH], tb, BF16)
        z = _dft_inv(hy["inv"], yspec.reshape(batch, -1, HY_CH), xg.reshape(batch, seq_len, HY_CH), z,
                     lw["hy_bias"][o:o + 1])
    out_d = z.reshape(n, HY_CH)

    y = _outproj(x2, mod, out_a, out_b, o_hgrn, out_d, p, lw["hgrn_out_g"], consts["seg64"], lw["w_out"], seq_len)
    new = None
    if ctx is None:
        new = (ckv3, krp3[:, :, KR_OFF:KR_OFF + MLA_ROPE],
               kd.reshape(batch, seq_len, DIFF_HEADS, 2, DIFF_HD),
               dv.reshape(batch, seq_len, DIFF_HEADS, 2 * DIFF_HD), states)
    return y, new


def _rope_tables(seq_len):
    half = MLA_ROPE // 2
    inv = ROPE_BASE ** (-jnp.arange(0, half, 2, dtype=F32) / half)
    rows = seq_len // GRID_W
    row = jnp.repeat(jnp.arange(rows, dtype=F32), GRID_W)
    col = (jnp.arange(rows * GRID_W) % GRID_W).astype(F32)
    ar, ac = row[:, None] * inv, col[:, None] * inv
    cos32 = jnp.concatenate([jnp.cos(ar), jnp.cos(ar), jnp.cos(ac), jnp.cos(ac)], axis=-1)
    sin32 = jnp.concatenate([-jnp.sin(ar), jnp.sin(ar), -jnp.sin(ac), jnp.sin(ac)], axis=-1)
    pad = ((0, 0), (KR_OFF, LANE - KR_OFF - MLA_ROPE))
    return dict(cos_mla=jnp.pad(cos32, pad, constant_values=1.0), sin_mla=jnp.pad(sin32, pad),
                cos_diff=jnp.tile(cos32, (1, 2 * DIFF_HEADS)), sin_diff=jnp.tile(sin32, (1, 2 * DIFF_HEADS)))


def _hy_tables(seq_len):
    tb = 256
    fwd_full, inv = _dft_tables(seq_len, tb)
    feats, window = _hy_static(seq_len)
    return dict(tb=tb, fwd_full=fwd_full, fwd_half=fwd_full[:, :seq_len], inv=inv, feats=feats, window=window)


def _layer_weights(l, w_in_p, lb, W):
    def head_pad(w, width, per):
        k = w.shape[0]
        w = w.reshape(k, MLA_HEADS, per)[:, :, :width]
        return jnp.pad(w, ((0, 0), (0, 0), (0, LANE - width))).reshape(k, MLA_HEADS * LANE)

    w_ukv = W["mla_w_ukv"][l].reshape(MLA_KV_LORA, MLA_HEADS, MLA_NOPE + MLA_V)
    nope_g, rope_g = W["mla_nope_g"][l], W["mla_rope_g"][l]
    zeros32 = jnp.zeros((MLA_ROPE,), F32)
    zeros64 = jnp.zeros((MLA_NOPE,), F32)
    gq = jnp.tile(jnp.concatenate([nope_g[0], rope_g[0], zeros32]), MLA_HEADS).reshape(1, 512)
    gk = jnp.tile(jnp.concatenate([nope_g[1], zeros64]), MLA_HEADS).reshape(1, 512)
    gkr = jnp.concatenate([zeros64, rope_g[1], zeros32]).reshape(1, LANE)
    return dict(
        norm_g=W["norm_g"][l], w_in=w_in_p[l], w_out=W["w_out"][l].astype(BF16),
        qn_g=W["mla_q_norm_g"][l].reshape(1, -1),
        w_uq=head_pad(W["mla_w_uq"][l], MLA_NOPE + MLA_ROPE, MLA_NOPE + MLA_ROPE).astype(BF16),
        kvn_g=W["mla_kv_norm_g"][l].reshape(1, -1),
        w_uk=jnp.pad(w_ukv[:, :, :MLA_NOPE], ((0, 0), (0, 0), (0, LANE - MLA_NOPE))).reshape(MLA_KV_LORA, 512)
        .astype(BF16),
        w_uv=w_ukv[:, :, MLA_NOPE:].reshape(MLA_KV_LORA, BRANCH).astype(BF16),
        gq=gq, gk=gk, gkr=gkr,
        dgq=jnp.tile(W["diff_qk_g"][l, 0], 2 * DIFF_HEADS).reshape(1, BRANCH),
        dgk=jnp.tile(W["diff_qk_g"][l, 1], 2 * DIFF_HEADS).reshape(1, BRANCH),
        diff_lambda=W["diff_lambda"][l], subln_col=W["diff_subln_g"][l].reshape(2 * DIFF_HD, 1),
        lam_init=0.8 - 0.6 * math.exp(-0.3 * l),
        hgrn_lb=lb[:, l].reshape(2, 1, BRANCH),
        hgrn_out_g=jnp.tile(W["hgrn_out_g"][l], HGRN_HEADS).reshape(1, BRANCH),
        hy_conv_w=W["hy_conv_w"][l], hy_conv_b=W["hy_conv_b"][l].reshape(1, -1),
        hy_w1=jnp.pad(W["hy_w1"][l], ((0, LANE - HY_EMB), (0, 0))), hy_b1=W["hy_b1"][l].reshape(1, -1),
        hy_w2=W["hy_w2"][l], hy_b2=W["hy_b2"][l].reshape(1, -1), hy_w3=W["hy_w3"][l],
        hy_freq=W["hy_sin_freq"][l], hy_bias=W["hy_bias"][l],
    )


def kernel(x_prompt, x_sample, cache_mla_ckv, cache_mla_krope, cache_diff_k, cache_diff_v, state_hgrn, c, c_ctx,
           norm_g, w_mod, b_mod, w_in, w_out, mla_q_norm_g, mla_w_uq, mla_kv_norm_g, mla_w_ukv, mla_nope_g,
           mla_rope_g, diff_qk_g, diff_lambda, diff_subln_g, hgrn_lb_logits, hgrn_out_g, hy_conv_w, hy_conv_b,
           hy_w1, hy_b1, hy_w2, hy_b2, hy_w3, hy_sin_freq, hy_bias):
    W = dict(norm_g=norm_g, w_out=w_out, mla_q_norm_g=mla_q_norm_g, mla_w_uq=mla_w_uq,
             mla_kv_norm_g=mla_kv_norm_g, mla_w_ukv=mla_w_ukv, mla_nope_g=mla_nope_g, mla_rope_g=mla_rope_g,
             diff_qk_g=diff_qk_g, diff_lambda=diff_lambda, diff_subln_g=diff_subln_g, hgrn_out_g=hgrn_out_g,
             hy_conv_w=hy_conv_w, hy_conv_b=hy_conv_b, hy_w1=hy_w1, hy_b1=hy_b1, hy_w2=hy_w2, hy_b2=hy_b2,
             hy_w3=hy_w3, hy_sin_freq=hy_sin_freq, hy_bias=hy_bias)
    bp, lp, _ = x_prompt.shape
    bs, ls, _ = x_sample.shape

    w_in_z = jnp.pad(w_in, ((0, 0), (0, 0), (0, 1)))
    w_in_p = jnp.take(w_in_z, jnp.asarray(_in_col_perm()), axis=2).astype(BF16)
    cvecs = jnp.concatenate([c_ctx[None, :], c, jnp.zeros((8 - 1 - bs, D_MODEL), F32)], axis=0)
    mods = _mod_all(cvecs, w_mod, b_mod)
    lb = _hgrn_lb(hgrn_lb_logits)
    seg512, cnt512 = _mla_seg()
    consts = dict(seg512=seg512, cnt512=cnt512, seg32=_seg_const(BRANCH, DIFF_HD), seg64=_seg_const(BRANCH, HGRN_DK),
                  hgrn_dd=_hgrn_consts())
    lws = [_layer_weights(l, w_in_p, lb, W) for l in range(DEPTH)]

    hy_p = _hy_tables(lp)
    y = x_prompt.reshape(bp * lp, D_MODEL)
    per_layer = []
    for l in range(DEPTH):
        mod = mods[l, 0:1].reshape(1, 3, D_MODEL)
        y, new = _layer(y, mod, lws[l], consts, bp, lp, None, None, hy_p)
        per_layer.append(new)
    y_prompt = y.reshape(bp, lp, D_MODEL)
    news = [jnp.stack([s[i] for s in per_layer], axis=1) for i in range(5)]

    hy_s = _hy_tables(ls)
    rope_tabs = _rope_tables(ls)
    y = x_sample.reshape(bs * ls, D_MODEL)
    for l in range(DEPTH):
        mod = mods[l, 1:1 + bs].reshape(bs, 3, D_MODEL)
        ctx = (cache_mla_ckv[:, l], cache_mla_krope[:, l], cache_diff_k[:, l], cache_diff_v[:, l], state_hgrn[:, l])
        y, _ = _layer(y, mod, lws[l], consts, bs, ls, ctx, rope_tabs, hy_s)
    y_sample = y.reshape(bs, ls, D_MODEL)

    return (y_prompt, y_sample, news[0], news[1], news[2], news[3], news[4])
```

```python
import functools
import math

import numpy as np
import jax
import jax.numpy as jnp
from jax import lax
from jax.experimental import pallas as pl
from jax.experimental.pallas import tpu as pltpu

F32 = jnp.float32
BF16 = jnp.bfloat16

D_MODEL = 1024
DEPTH = 4
GRID_W = 64
ROPE_BASE = 10000.0
EPS = 1e-6
BRANCH = 256
MLA_HEADS = 4
MLA_NOPE = 64
MLA_ROPE = 32
MLA_V = 64
MLA_Q_LORA = 256
MLA_KV_LORA = 128
MLA_SCALE = (MLA_NOPE + MLA_ROPE) ** -0.5
DIFF_HEADS = 4
DIFF_HD = 32
DIFF_SCALE = DIFF_HD ** -0.5
HGRN_HEADS = 4
HGRN_DK = 64
HGRN_CHUNK = 128
HGRN_LEVELS = 7
HY_CH = 256
HY_ORDER = 2
HY_EMB = 33
HY_BANDS = 16
HY_FH = 64
HY_DECAY_TARGET = 0.01
HY_FAST_DECAY = 0.3
HY_SLOW_DECAY = 1.5
IN_COLS = 4000

LANE = 128
LOG2E = math.log2(math.e)
VT_ROWS = 80
MLA_SUB = 128
DIFF_SUB = 256
VMEM_LIMIT = 52 * 1024 * 1024

P_CQ, P_CKV, P_KR, P_DQ, P_DK, P_DV = 0, 256, 384, 512, 768, 1024
P_HQ, P_HZF, P_HZB, P_HI, P_HU, P_GATE = 1280, 1536, 1792, 2048, 2304, 3072
P_COLS = 4096
KR_OFF = 64


def _in_col_perm():
    src = np.full((P_COLS,), IN_COLS, np.int32)

    def put(dst, lo, n):
        src[dst:dst + n] = np.arange(lo, lo + n)

    put(P_CQ, 0, 256)
    put(P_CKV, 256, 128)
    put(P_KR + KR_OFF, 384, 32)
    put(P_GATE, 416, 256)
    put(P_DQ, 672, 256)
    put(P_DK, 928, 256)
    put(P_DV, 1184, 256)
    put(P_GATE + 256, 1440, 256)
    put(P_HQ, 1696, 256)
    put(P_HZF, 1952, 256)
    put(P_HZB, 2208, 256)
    put(P_HI, 2464, 256)
    put(P_GATE + 512, 2720, 256)
    put(P_HU, 2976, 768)
    put(P_GATE + 768, 3744, 256)
    return src


def _cparams(sem):
    return pltpu.CompilerParams(dimension_semantics=sem, vmem_limit_bytes=VMEM_LIMIT)


def _bdot(a, b):
    return jnp.dot(a.astype(BF16), b.astype(BF16), preferred_element_type=F32)


def _nt(a, b):
    return lax.dot_general(a.astype(BF16), b.astype(BF16), (((1,), (1,)), ((), ())), preferred_element_type=F32)


def _split2(a):
    hi = a.astype(BF16)
    lo = (a - hi.astype(F32)).astype(BF16)
    return hi, lo


def _dot3(a, b):
    ah, al = _split2(a)
    bh, bl = _split2(b)
    d = functools.partial(jnp.dot, preferred_element_type=F32)
    return d(ah, bh) + d(ah, bl) + d(al, bh)


def _segsum(v, seg):
    hi, lo = _split2(v)
    d = functools.partial(jnp.dot, preferred_element_type=F32)
    return d(hi, seg) + d(lo, seg)


def _rms(x, g):
    return x * lax.rsqrt(jnp.mean(x * x, axis=-1, keepdims=True) + EPS) * g


def _swap8(x):
    w = x.shape[-1]
    lane = lax.broadcasted_iota(jnp.int32, x.shape, x.ndim - 1)
    up = pltpu.roll(x, w - 8, x.ndim - 1)
    dn = pltpu.roll(x, 8, x.ndim - 1)
    return jnp.where((lane & 15) < 8, up, dn)


def _tile_lanes(x, n):
    return x if n == 1 else jnp.concatenate([x] * n, axis=-1)


def _mod_kernel(c_ref, w_ref, b_ref, o_ref):
    c = c_ref[...]
    o_ref[0] = _dot3(c * jax.nn.sigmoid(c), w_ref[0]) + b_ref[0]


def _mod_all(cvecs, w_mod, b_mod):
    nt = 3
    return pl.pallas_call(
        _mod_kernel,
        grid=(DEPTH, nt),
        in_specs=[pl.BlockSpec((8, D_MODEL), lambda l, j: (0, 0)),
                  pl.BlockSpec((1, D_MODEL, D_MODEL), lambda l, j: (l, 0, j)),
                  pl.BlockSpec((1, 1, D_MODEL), lambda l, j: (l, 0, j))],
        out_specs=pl.BlockSpec((1, 8, D_MODEL), lambda l, j: (l, 0, j)),
        out_shape=jax.ShapeDtypeStruct((DEPTH, 8, 3 * D_MODEL), F32),
        compiler_params=_cparams(("arbitrary", "arbitrary")),
        name="mod",
    )(cvecs, w_mod, b_mod.reshape(DEPTH, 1, 3 * D_MODEL))


def _lb_kernel(x_ref, o_ref):
    x = x_ref[...]
    rows = [x[l:l + 1, :] for l in range(DEPTH)]
    m = functools.reduce(jnp.maximum, rows)
    e = [jnp.exp(r - m) for r in rows]
    tot = functools.reduce(lambda a, b: a + b, e)
    acc = jnp.zeros_like(tot)
    o_ref[0:1, :] = acc
    for l in range(1, DEPTH):
        acc = acc + e[l] / tot
        o_ref[l:l + 1, :] = acc


def _hgrn_lb(logits):
    flat = logits.transpose(1, 0, 2).reshape(DEPTH, 2 * BRANCH)
    lb = pl.pallas_call(
        _lb_kernel,
        out_shape=jax.ShapeDtypeStruct(flat.shape, F32),
        name="hgrn_lb",
    )(flat)
    return lb.reshape(DEPTH, 2, BRANCH).transpose(1, 0, 2)


def _inproj_kernel(x_ref, mod_ref, g_ref, w_ref, p_ref):
    h = _rms(x_ref[...], g_ref[...]) * (1.0 + mod_ref[0, 1:2, :]) + mod_ref[0, 0:1, :]
    p_ref[...] = jnp.dot(h.astype(BF16), w_ref[...], preferred_element_type=F32)


def _inproj(x2, mod, norm_g, w_in_p, seq_len):
    n = x2.shape[0]
    tm = 256
    per_batch = mod.shape[0] > 1
    tiles_per_seq = seq_len // tm
    mod_idx = (lambda i: (i // tiles_per_seq, 0, 0)) if per_batch else (lambda i: (0, 0, 0))
    return pl.pallas_call(
        _inproj_kernel,
        grid=(n // tm,),
        in_specs=[pl.BlockSpec((tm, D_MODEL), lambda i: (i, 0)),
                  pl.BlockSpec((1, 3, D_MODEL), mod_idx),
                  pl.BlockSpec((1, D_MODEL), lambda i: (0, 0)),
                  pl.BlockSpec((D_MODEL, P_COLS), lambda i: (0, 0))],
        out_specs=pl.BlockSpec((tm, P_COLS), lambda i: (i, 0)),
        out_shape=jax.ShapeDtypeStruct((n, P_COLS), F32),
        compiler_params=_cparams(("parallel",)),
        name="inproj",
    )(x2, mod, norm_g.reshape(1, D_MODEL), w_in_p)


def _mla_seg():
    sid = np.zeros((512,), np.int32)
    cnt = np.ones((512,), np.float32)
    for h in range(MLA_HEADS):
        b = 128 * h
        sid[b:b + 64] = 3 * h
        sid[b + 64:b + 96] = 3 * h + 1
        sid[b + 96:b + 128] = 3 * h + 2
        cnt[b:b + 64] = 1.0 / 64
        cnt[b + 64:b + 128] = 1.0 / 32
    seg = (sid[:, None] == sid[None, :]).astype(np.float32)
    return jnp.asarray(seg, BF16), jnp.asarray(cnt.reshape(1, 512))


def _mla_q_kernel(rope, cq_ref, ckv_ref, kr_ref, qng_ref, wuq_ref, kvg_ref, gq_ref, gkr_ref, seg_ref, cnt_ref,
                  *rest):
    if rope:
        cos_ref, sin_ref, q_ref, ckvn_ref, krp_ref = rest
    else:
        q_ref, ckvn_ref, krp_ref = rest
    cqn = _rms(cq_ref[...], qng_ref[...])
    q = _bdot(cqn, wuq_ref[...])
    ss = _segsum(q * q, seg_ref[...]) * cnt_ref[...]
    qn = q * lax.rsqrt(ss + EPS) * gq_ref[...]
    ckvn_ref[...] = _rms(ckv_ref[...], kvg_ref[...])
    kr = kr_ref[...]
    krn = kr * lax.rsqrt(jnp.sum(kr * kr, axis=-1, keepdims=True) * (1.0 / MLA_ROPE) + EPS) * gkr_ref[...]
    if rope:
        cos, sin = cos_ref[...], sin_ref[...]
        qn = qn * _tile_lanes(cos, MLA_HEADS) + _swap8(qn) * _tile_lanes(sin, MLA_HEADS)
        krn = krn * cos + _swap8(krn) * sin
    q_ref[...] = (qn * (MLA_SCALE * LOG2E)).astype(BF16)
    krp_ref[...] = krn


def _mla_q(p, lw, consts, seq_len, rope_tabs):
    n = p.shape[0]
    tm = 256
    rope = rope_tabs is not None
    full = lambda shape: pl.BlockSpec(shape, lambda i: (0,) * len(shape))
    in_specs = [pl.BlockSpec((tm, 256), lambda i: (i, P_CQ // 256)),
                pl.BlockSpec((tm, 128), lambda i: (i, P_CKV // 128)),
                pl.BlockSpec((tm, 128), lambda i: (i, P_KR // 128)),
                full((1, 256)), full((256, 512)), full((1, 128)), full((1, 512)), full((1, 128)),
                full((512, 512)), full((1, 512))]
    args = [p, p, p, lw["qn_g"], lw["w_uq"], lw["kvn_g"], lw["gq"], lw["gkr"], consts["seg512"], consts["cnt512"]]
    if rope:
        tps = seq_len // tm
        in_specs += [pl.BlockSpec((tm, 128), lambda i: (i % tps, 0))] * 2
        args += [rope_tabs["cos_mla"], rope_tabs["sin_mla"]]
    return pl.pallas_call(
        functools.partial(_mla_q_kernel, rope),
        grid=(n // tm,),
        in_specs=in_specs,
        out_specs=[pl.BlockSpec((tm, 512), lambda i: (i, 0)),
                   pl.BlockSpec((tm, 128), lambda i: (i, 0)),
                   pl.BlockSpec((tm, 128), lambda i: (i, 0))],
        out_shape=[jax.ShapeDtypeStruct((n, 512), BF16),
                   jax.ShapeDtypeStruct((n, 128), F32),
                   jax.ShapeDtypeStruct((n, 128), F32)],
        compiler_params=_cparams(("parallel",)),
        name="mla_q",
    )(*args)


def _mla_kv_kernel(ckvn_ref, krp_ref, wuk_ref, wuv_ref, gk_ref, seg_ref, cnt_ref, k_ref, v_ref):
    c = ckvn_ref[...].astype(BF16)
    kn = jnp.dot(c, wuk_ref[...], preferred_element_type=F32)
    ss = _segsum(kn * kn, seg_ref[...]) * cnt_ref[...]
    k = kn * lax.rsqrt(ss + EPS) * gk_ref[...] + _tile_lanes(krp_ref[...], MLA_HEADS)
    k_ref[...] = k.astype(BF16)
    v_ref[...] = jnp.dot(c, wuv_ref[...], preferred_element_type=F32).astype(BF16)


def _mla_kv(ckvn, krp, lw, consts):
    n = ckvn.shape[0]
    tm = 512
    full = lambda shape: pl.BlockSpec(shape, lambda i: (0,) * len(shape))
    return pl.pallas_call(
        _mla_kv_kernel,
        grid=(n // tm,),
        in_specs=[pl.BlockSpec((tm, 128), lambda i: (i, 0)), pl.BlockSpec((tm, 128), lambda i: (i, 0)),
                  full((128, 512)), full((128, 256)), full((1, 512)), full((512, 512)), full((1, 512))],
        out_specs=[pl.BlockSpec((tm, 512), lambda i: (i, 0)), pl.BlockSpec((tm, 256), lambda i: (i, 0))],
        out_shape=[jax.ShapeDtypeStruct((n, 512), BF16), jax.ShapeDtypeStruct((n, 256), BF16)],
        compiler_params=_cparams(("parallel",)),
        name="mla_kv",
    )(ckvn, krp, lw["w_uk"], lw["w_uv"], lw["gk"], consts["seg512"], consts["cnt512"])


def _softmax_pv(qs, k_ref, vt_ref, key_chunk, sub_rows):
    lk = k_ref.shape[1]
    nch = lk // key_chunk
    nq = len(qs)
    sub = min(sub_rows, key_chunk)
    nsub = key_chunk // sub

    def scores(c, u):
        lo = c * key_chunk + u * sub
        ks = k_ref[0, lo:lo + sub, :]
        return [_nt(ks, q) for q in qs]

    m = [None] * nq
    acc = [None] * nq
    s_next = [scores(0, u) for u in range(nsub)]
    for c in range(nch):
        s_cur = s_next
        s_next = []
        m_new = []
        for j in range(nq):
            mc = functools.reduce(jnp.maximum, [s_cur[u][j] for u in range(nsub)])
            mc = jnp.max(mc, axis=0, keepdims=True)
            m_new.append(mc if c == 0 else jnp.maximum(m[j], mc))
        p = [[] for _ in range(nq)]
        for u in range(nsub):
            if c + 1 < nch:
                s_next.append(scores(c + 1, u))
            for j in range(nq):
                p[j].append(jnp.exp2(s_cur[u][j] - m_new[j]).astype(BF16))
        vs = vt_ref[0, 0, :, c * key_chunk:(c + 1) * key_chunk]
        for j in range(nq):
            pv = jnp.dot(vs, jnp.concatenate(p[j], axis=0), preferred_element_type=F32)
            acc[j] = pv if c == 0 else acc[j] * jnp.exp2(m[j] - m_new[j]) + pv
            m[j] = m_new[j]
    return acc


def _mla_attn_kernel(key_chunk, q_ref, k_ref, vt_ref, o_ref):
    (acc,) = _softmax_pv([q_ref[...]], k_ref, vt_ref, key_chunk, MLA_SUB)
    o_ref[0] = acc[0:MLA_V] / acc[MLA_V:MLA_V + 1]


def _key_chunk(lk):
    return 512 if lk % 512 == 0 else lk


def _vt_with_ones(v3):
    b, lk, _ = v3.shape
    vt = v3.reshape(b, lk, 4, 64).transpose(0, 2, 3, 1)
    extra = jnp.zeros((b, 4, VT_ROWS - 64, lk), BF16).at[:, :, 0, :].set(1.0)
    return jnp.concatenate([vt, extra], axis=2)


def _mla_attn(q, k, vt, batch, seq_len):
    lk = k.shape[1]
    tq = 256
    nq = seq_len // tq
    return pl.pallas_call(
        functools.partial(_mla_attn_kernel, _key_chunk(lk)),
        grid=(batch, MLA_HEADS, nq),
        in_specs=[pl.BlockSpec((tq, 128), lambda b, h, i: (b * nq + i, h)),
                  pl.BlockSpec((1, lk, 128), lambda b, h, i: (b, 0, h)),
                  pl.BlockSpec((1, 1, VT_ROWS, lk), lambda b, h, i: (b, h, 0, 0))],
        out_specs=pl.BlockSpec((1, MLA_V, tq), lambda b, h, i: (b, h, i)),
        out_shape=jax.ShapeDtypeStruct((batch, BRANCH, seq_len), F32),
        compiler_params=_cparams(("parallel", "parallel", "arbitrary")),
        name="mla_attn",
    )(q, k, vt)


def _seg_const(width, seg):
    sid = np.arange(width) // seg
    return jnp.asarray((sid[:, None] == sid[None, :]).astype(np.float32), BF16)


def _diff_prep_kernel(rope, dq_ref, dk_ref, gq_ref, gk_ref, seg_ref, *rest):
    if rope:
        cos_ref, sin_ref, q_ref, k_ref, kf_ref = rest
    else:
        q_ref, k_ref, kf_ref = rest
    seg = seg_ref[...]

    def norm(x, g):
        ss = _segsum(x * x, seg) * (1.0 / DIFF_HD)
        return x * lax.rsqrt(ss + EPS) * g

    q = norm(dq_ref[...], gq_ref[...])
    k = norm(dk_ref[...], gk_ref[...])
    kf_ref[...] = k
    if rope:
        cos, sin = cos_ref[...], sin_ref[...]
        q = q * cos + _swap8(q) * sin
        k = k * cos + _swap8(k) * sin
    q_ref[...] = (q * (DIFF_SCALE * LOG2E)).astype(BF16)
    k_ref[...] = k.astype(BF16)


def _diff_prep(p, lw, consts, seq_len, rope_tabs):
    n = p.shape[0]
    tm = 256
    rope = rope_tabs is not None
    full = lambda shape: pl.BlockSpec(shape, lambda i: (0,) * len(shape))
    in_specs = [pl.BlockSpec((tm, 256), lambda i: (i, P_DQ // 256)),
                pl.BlockSpec((tm, 256), lambda i: (i, P_DK // 256)),
                full((1, 256)), full((1, 256)), full((256, 256))]
    args = [p, p, lw["dgq"], lw["dgk"], consts["seg32"]]
    if rope:
        tps = seq_len // tm
        in_specs += [pl.BlockSpec((tm, 256), lambda i: (i % tps, 0))] * 2
        args += [rope_tabs["cos_diff"], rope_tabs["sin_diff"]]
    blk = pl.BlockSpec((tm, 256), lambda i: (i, 0))
    return pl.pallas_call(
        functools.partial(_diff_prep_kernel, rope),
        grid=(n // tm,),
        in_specs=in_specs,
        out_specs=[blk, blk, blk],
        out_shape=[jax.ShapeDtypeStruct((n, 256), BF16), jax.ShapeDtypeStruct((n, 256), BF16),
                   jax.ShapeDtypeStruct((n, 256), F32)],
        compiler_params=_cparams(("parallel",)),
        name="diff_prep",
    )(*args)


def _diff_attn_kernel(lam_init, key_chunk, q_ref, k_ref, vt_ref, lp_ref, g_ref, o_ref):
    h = pl.program_id(1)
    q = q_ref[...]
    lane = lax.broadcasted_iota(jnp.int32, q.shape, 1)
    base = (h % 2) * 64
    zero = jnp.zeros_like(q)

    def map_query(j):
        lo = base + 32 * j
        return jnp.where((lane >= lo) & (lane < lo + 32), q, zero)

    acc0, acc1 = _softmax_pv([map_query(0), map_query(1)], k_ref, vt_ref, key_chunk, DIFF_SUB)
    lp = lp_ref[...]
    lam = (jnp.exp(jnp.sum(lp[0:1] * lp[1:2], axis=1, keepdims=True))
           - jnp.exp(jnp.sum(lp[2:3] * lp[3:4], axis=1, keepdims=True)) + lam_init)
    o = acc0[0:64] / acc0[64:65] - lam * (acc1[0:64] / acc1[64:65])
    ms = jnp.mean(o * o, axis=0, keepdims=True)
    o_ref[0] = o * lax.rsqrt(ms + EPS) * g_ref[...] * (1.0 - lam_init)


def _diff_attn(q, k, vt, lp, g_col, lam_init, batch, seq_len):
    lk = k.shape[1]
    tq = 256
    nq = seq_len // tq
    return pl.pallas_call(
        functools.partial(_diff_attn_kernel, lam_init, _key_chunk(lk)),
        grid=(batch, DIFF_HEADS, nq),
        in_specs=[pl.BlockSpec((tq, 128), lambda b, h, i: (b * nq + i, h // 2)),
                  pl.BlockSpec((1, lk, 128), lambda b, h, i: (b, 0, h // 2)),
                  pl.BlockSpec((1, 1, VT_ROWS, lk), lambda b, h, i: (b, h, 0, 0)),
                  pl.BlockSpec((4, DIFF_HD), lambda b, h, i: (0, 0)),
                  pl.BlockSpec((64, 1), lambda b, h, i: (0, 0))],
        out_specs=pl.BlockSpec((1, 64, tq), lambda b, h, i: (b, h, i)),
        out_shape=jax.ShapeDtypeStruct((batch, BRANCH, seq_len), F32),
        compiler_params=_cparams(("parallel", "parallel", "arbitrary")),
        name="diff_attn",
    )(q, k, vt, lp, g_col)


def _hgrn_consts():
    c = HGRN_CHUNK
    t = np.arange(c)
    low = (t[None, :] <= t[:, None]).astype(np.float32)
    blocks = []
    for j in range(HGRN_LEVELS):
        m = 1 << j
        rho = (t // (2 * m)) * (2 * m) + m - 1
        blocks.append(low - (t[None, :] <= rho[:, None]).astype(np.float32))
    blocks.append(low)
    fwd = np.concatenate(blocks, axis=0)
    bwd = np.concatenate([b[::-1, ::-1] for b in blocks], axis=0)
    return jnp.asarray(np.stack([fwd, bwd]), BF16)


def _hgrn_kernel(nc, q_ref, z_ref, v_ref, lb_ref, dd_ref, s0_ref, o_ref, sout_ref, st_ref):
    c = HGRN_CHUNK
    d = pl.program_id(1)
    ci = pl.program_id(2)

    @pl.when(ci == 0)
    def _():
        st_ref[...] = s0_ref[0, 0]

    lb = lb_ref[0]
    z = z_ref[...]
    q = q_ref[...]
    v = v_ref[...]
    f = lb + (1.0 - lb) * jax.nn.sigmoid(z)
    g = jnp.log(f)
    kk = (1.0 - lb) * jax.nn.sigmoid(-z)

    gh, gl = _split2(g)
    dd = dd_ref[0]
    u = jnp.dot(dd, gh, preferred_element_type=F32) + jnp.dot(dd, gl, preferred_element_type=F32)

    row = lax.broadcasted_iota(jnp.int32, (c, 1), 0)
    row_dir = jnp.where(d == 0, row, c - 1 - row)
    lane = lax.broadcasted_iota(jnp.int32, (1, BRANCH), 1)
    head_masks = [(lane >= HGRN_DK * h) & (lane < HGRN_DK * (h + 1)) for h in range(HGRN_HEADS)]
    t_idx = lax.broadcasted_iota(jnp.int32, (c, HGRN_HEADS * c), 0)
    s_idx = lax.broadcasted_iota(jnp.int32, (c, HGRN_HEADS * c), 1) & (c - 1)

    def stack_heads(x):
        xb = x.astype(BF16)
        zero = jnp.zeros_like(xb)
        return jnp.concatenate([jnp.where(hm, xb, zero) for hm in head_masks], axis=0)

    a = jnp.where(t_idx == s_idx, _nt(q, stack_heads(kk)), 0.0)
    for j in range(HGRN_LEVELS):
        e = jnp.exp(-jnp.abs(u[j * c:(j + 1) * c]))
        right = ((row_dir >> j) & 1) == 1
        qt = jnp.where(right, q * e, 0.0)
        kt = jnp.where(right, 0.0, kk * e)
        same = (t_idx >> (j + 1)) == (s_idx >> (j + 1))
        a = a + jnp.where(same, _nt(qt, stack_heads(kt)), 0.0)

    b = u[HGRN_LEVELS * c:]
    b_tot = jnp.where(d == 0, b[c - 1:c], b[0:1])
    st = st_ref[...]
    o = jnp.dot(a.astype(BF16), stack_heads(v), preferred_element_type=F32)
    o = o + _nt(q * jnp.exp(b), st)
    o_ref[0] = o

    kd = kk * jnp.exp(b_tot - b)
    upd = lax.dot_general(v.astype(BF16), kd.astype(BF16), (((0,), (0,)), ((), ())), preferred_element_type=F32)
    r2 = lax.broadcasted_iota(jnp.int32, (BRANCH, BRANCH), 0) // HGRN_DK
    c2 = lax.broadcasted_iota(jnp.int32, (BRANCH, BRANCH), 1) // HGRN_DK
    st_new = st * jnp.exp(b_tot) + jnp.where(r2 == c2, upd, 0.0)
    st_ref[...] = st_new

    @pl.when(ci == nc - 1)
    def _():
        sout_ref[0, 0] = st_new


def _hgrn(p, lb_l, dd, st0, batch, seq_len):
    n = p.shape[0]
    c = HGRN_CHUNK
    nc = seq_len // c

    def rows(b, d, i):
        return b * nc + i + d * (nc - 1 - 2 * i)

    return pl.pallas_call(
        functools.partial(_hgrn_kernel, nc),
        grid=(batch, 2, nc),
        in_specs=[pl.BlockSpec((c, 256), lambda b, d, i: (rows(b, d, i), P_HQ // 256)),
                  pl.BlockSpec((c, 256), lambda b, d, i: (rows(b, d, i), P_HZF // 256 + d)),
                  pl.BlockSpec((c, 256), lambda b, d, i: (rows(b, d, i), P_HI // 256)),
                  pl.BlockSpec((1, 1, 256), lambda b, d, i: (d, 0, 0)),
                  pl.BlockSpec((1, (HGRN_LEVELS + 1) * c, c), lambda b, d, i: (d, 0, 0)),
                  pl.BlockSpec((1, 1, 256, 256), lambda b, d, i: (b, d, 0, 0))],
        out_specs=[pl.BlockSpec((1, c, 256), lambda b, d, i: (d, rows(b, d, i), 0)),
                   pl.BlockSpec((1, 1, 256, 256), lambda b, d, i: (b, d, 0, 0))],
        out_shape=[jax.ShapeDtypeStruct((2, n, 256), F32),
                   jax.ShapeDtypeStruct((batch, 2, 256, 256), F32)],
        scratch_shapes=[pltpu.VMEM((256, 256), F32)],
        compiler_params=_cparams(("parallel", "parallel", "arbitrary")),
        name="hgrn",
    )(p, p, p, lb_l, dd, st0)


def _hy_conv3_kernel(prev_ref, cur_ref, nxt_ref, w_ref, b_ref, v_ref, x1_ref, x2_ref):
    w = w_ref[...]
    u = prev_ref[...] * w[0:1] + cur_ref[...] * w[1:2] + nxt_ref[...] * w[2:3] + b_ref[...]
    v_ref[...] = u[:, 0:256]
    x1_ref[...] = u[:, 256:512]
    x2_ref[...] = u[:, 512:768]


def _hy_conv3(prev, cur, nxt, w, b):
    n = cur.shape[0]
    tm = 512
    blk = pl.BlockSpec((tm, 768), lambda i: (i, 0))
    oblk = pl.BlockSpec((tm, 256), lambda i: (i, 0))
    return pl.pallas_call(
        _hy_conv3_kernel,
        grid=(n // tm,),
        in_specs=[blk, blk, blk, pl.BlockSpec((3, 768), lambda i: (0, 0)), pl.BlockSpec((1, 768), lambda i: (0, 0))],
        out_specs=[oblk, oblk, oblk],
        out_shape=[jax.ShapeDtypeStruct((n, 256), F32)] * 3,
        compiler_params=_cparams(("parallel",)),
        name="hy_conv3",
    )(prev, cur, nxt, w, b)


def _hy_filter_kernel(feat_ref, w1_ref, b1_ref, w2_ref, b2_ref, w3_ref, fr_ref, win_ref, o_ref):
    fr = fr_ref[...]
    h = jnp.sin(fr[0:1] * (_dot3(feat_ref[...], w1_ref[...]) + b1_ref[...]))
    h = jnp.sin(fr[1:2] * (_dot3(h, w2_ref[...]) + b2_ref[...]))
    o_ref[...] = _dot3(h, w3_ref[...]) * _tile_lanes(win_ref[...], 2 * HY_ORDER)


def _hy_filter(feats, window, w1p, b1, w2, b2, w3, freq):
    ln = feats.shape[0]
    tm = min(512, ln)
    full = lambda shape: pl.BlockSpec(shape, lambda i: (0,) * len(shape))
    return pl.pallas_call(
        _hy_filter_kernel,
        grid=(ln // tm,),
        in_specs=[pl.BlockSpec((tm, LANE), lambda i: (i, 0)),
                  full((LANE, HY_FH)), full((1, HY_FH)), full((HY_FH, HY_FH)), full((1, HY_FH)),
                  full((HY_FH, 2 * HY_ORDER * HY_CH)), full((2, HY_FH)),
                  pl.BlockSpec((tm, HY_CH), lambda i: (i, 0))],
        out_specs=pl.BlockSpec((tm, 2 * HY_ORDER * HY_CH), lambda i: (i, 0)),
        out_shape=jax.ShapeDtypeStruct((ln, 2 * HY_ORDER * HY_CH), F32),
        compiler_params=_cparams(("parallel",)),
        name="hy_filter",
    )(feats, w1p, b1, w2, b2, w3, freq, window)


def _dft_fwd_kernel(nb, with_filter, f_ref, z_ref, *rest):
    if with_filter:
        h_ref, o_ref = rest
    else:
        (o_ref,) = rest
    f = f_ref[...]
    tb = f.shape[0] // 2
    for b in range(nb):
        s = jnp.dot(f, z_ref[b], preferred_element_type=F32)
        if with_filter:
            zr, zi = s[:tb], s[tb:]
            hr, hi = h_ref[0, 0], h_ref[0, 1]
            o_ref[b, 0, 0] = (zr * hr - zi * hi).astype(o_ref.dtype)
            o_ref[b, 0, 1] = (zr * hi + zi * hr).astype(o_ref.dtype)
        else:
            o_ref[b, 0, 0] = s[:tb].astype(o_ref.dtype)
            o_ref[b, 0, 1] = s[tb:].astype(o_ref.dtype)


def _dft_fwd(fmat, z, spec, tb, out_dtype):
    nb, kdim, ch = z.shape
    nt = fmat.shape[0] // (2 * tb)
    in_specs = [pl.BlockSpec((2 * tb, kdim), lambda j: (j, 0)),
                pl.BlockSpec((nb, kdim, ch), lambda j: (0, 0, 0))]
    args = [fmat, z]
    if spec is not None:
        in_specs.append(pl.BlockSpec((1, 2, tb, ch), lambda j: (j, 0, 0, 0)))
        args.append(spec)
    return pl.pallas_call(
        functools.partial(_dft_fwd_kernel, nb, spec is not None),
        grid=(nt,),
        in_specs=in_specs,
        out_specs=pl.BlockSpec((nb, 1, 2, tb, ch), lambda j: (0, j, 0, 0, 0)),
        out_shape=jax.ShapeDtypeStruct((nb, nt, 2, tb, ch), out_dtype),
        compiler_params=_cparams(("parallel",)),
        name="dft_fwd",
    )(*args)


def _dft_inv_kernel(inv_ref, y_ref, x_ref, z_ref, bias_ref, o_ref):
    conv = jnp.dot(inv_ref[...], y_ref[0], preferred_element_type=F32)
    z = z_ref[0]
    o_ref[0] = x_ref[0] * (conv + z * bias_ref[...])


def _dft_inv(inv, yspec, xg, z, bias):
    nb, ln, ch = z.shape
    kdim = inv.shape[1]
    tt = min(256, ln)
    blk = pl.BlockSpec((1, tt, ch), lambda b, i: (b, i, 0))
    return pl.pallas_call(
        _dft_inv_kernel,
        grid=(nb, ln // tt),
        in_specs=[pl.BlockSpec((tt, kdim), lambda b, i: (i, 0)),
                  pl.BlockSpec((1, kdim, ch), lambda b, i: (b, 0, 0)),
                  blk, blk, pl.BlockSpec((1, ch), lambda b, i: (0, 0))],
        out_specs=blk,
        out_shape=jax.ShapeDtypeStruct((nb, ln, ch), F32),
        compiler_params=_cparams(("parallel", "arbitrary")),
        name="dft_inv",
    )(inv, yspec, xg, z, bias)


def _dft_tables(ln, tb):
    n = 2 * ln
    nbins = -(-(ln + 1) // tb) * tb
    nt = nbins // tb
    k = jnp.arange(nbins, dtype=jnp.int32)
    n1 = jnp.arange(n // 64, dtype=jnp.int32)
    n0 = jnp.arange(64, dtype=jnp.int32)
    a1 = (2.0 * math.pi / n) * ((k[:, None] * (64 * n1)[None, :]) % n).astype(F32)
    a0 = (2.0 * math.pi / n) * ((k[:, None] * n0[None, :]) % n).astype(F32)
    c1, s1 = jnp.cos(a1)[:, :, None], jnp.sin(a1)[:, :, None]
    c0, s0 = jnp.cos(a0)[:, None, :], jnp.sin(a0)[:, None, :]
    valid = (k <= ln).astype(F32)[:, None]
    cosm = (c1 * c0 - s1 * s0).reshape(nbins, n) * valid
    sinm = (s1 * c0 + c1 * s0).reshape(nbins, n) * valid
    fwd = jnp.stack([cosm.reshape(nt, tb, n), -sinm.reshape(nt, tb, n)], axis=1)
    wk = jnp.where((k == 0) | (k == ln), 1.0 / n, 2.0 / n).astype(F32)
    inv = (fwd[..., :ln] * wk.reshape(nt, 1, tb, 1)).reshape(nt * 2 * tb, ln).T
    return fwd.reshape(nt * 2 * tb, n).astype(BF16), inv.astype(BF16)


def _hy_static(ln):
    t = jnp.linspace(0.0, 1.0, ln, dtype=F32)[:, None]
    w = 2.0 * math.pi * jnp.arange(ln, dtype=F32) / ln
    f = jnp.linspace(1e-4, HY_BANDS - 1, HY_BANDS, dtype=F32)
    ang = w[:, None] * f[None, :]
    feats = jnp.concatenate([t, jnp.cos(ang), -jnp.sin(ang)], axis=-1)
    feats = jnp.pad(feats, ((0, 0), (0, LANE - HY_EMB)))
    min_decay = math.log(HY_DECAY_TARGET) / HY_SLOW_DECAY
    max_decay = math.log(HY_DECAY_TARGET) / HY_FAST_DECAY
    deltas = jnp.linspace(min_decay, max_decay, HY_CH, dtype=F32)
    window = jnp.exp(-t * jnp.abs(deltas))
    return feats, window


def _outproj_kernel(x_ref, mod_ref, oa_ref, ob_ref, of_ref, obk_ref, od_ref, gate_ref, hg_ref, seg_ref, w_ref,
                    y_ref):
    gt = gate_ref[...]
    sg = gt * jax.nn.sigmoid(gt)
    oc = of_ref[0] + obk_ref[0]
    ss = _segsum(oc * oc, seg_ref[...]) * (1.0 / HGRN_DK)
    oc = oc * lax.rsqrt(ss + EPS) * hg_ref[...]
    acc = _bdot(oa_ref[...] * sg[:, 0:256], w_ref[0:256, :])
    acc += _bdot(ob_ref[...] * sg[:, 256:512], w_ref[256:512, :])
    acc += _bdot(oc * sg[:, 512:768], w_ref[512:768, :])
    acc += _bdot(od_ref[...] * sg[:, 768:1024], w_ref[768:1024, :])
    y_ref[...] = x_ref[...] + mod_ref[0, 2:3, :] * acc


def _outproj(x2, mod, out_a, out_b, o_hgrn, out_d, p, hg, seg64, w_out, seq_len):
    n = x2.shape[0]
    tm = 256
    per_batch = mod.shape[0] > 1
    tps = seq_len // tm
    mod_idx = (lambda i: (i // tps, 0, 0)) if per_batch else (lambda i: (0, 0, 0))
    b256 = pl.BlockSpec((tm, 256), lambda i: (i, 0))
    return pl.pallas_call(
        _outproj_kernel,
        grid=(n // tm,),
        in_specs=[pl.BlockSpec((tm, D_MODEL), lambda i: (i, 0)),
                  pl.BlockSpec((1, 3, D_MODEL), mod_idx),
                  b256, b256,
                  pl.BlockSpec((1, tm, 256), lambda i: (0, i, 0)),
                  pl.BlockSpec((1, tm, 256), lambda i: (1, i, 0)),
                  b256,
                  pl.BlockSpec((tm, 1024), lambda i: (i, P_GATE // 1024)),
                  pl.BlockSpec((1, 256), lambda i: (0, 0)),
                  pl.BlockSpec((256, 256), lambda i: (0, 0)),
                  pl.BlockSpec((D_MODEL, D_MODEL), lambda i: (0, 0))],
        out_specs=pl.BlockSpec((tm, D_MODEL), lambda i: (i, 0)),
        out_shape=jax.ShapeDtypeStruct((n, D_MODEL), F32),
        compiler_params=_cparams(("parallel",)),
        name="outproj",
    )(x2, mod, out_a, out_b, o_hgrn, o_hgrn, out_d, p, hg, seg64, w_out)


def _layer(x2, mod, lw, consts, batch, seq_len, ctx, rope_tabs, hy):
    n = batch * seq_len
    p = _inproj(x2, mod, lw["norm_g"], lw["w_in"], seq_len)

    q_a, ckvn, krp = _mla_q(p, lw, consts, seq_len, rope_tabs)
    ckv3 = ckvn.reshape(batch, seq_len, MLA_KV_LORA)
    krp3 = krp.reshape(batch, seq_len, LANE)
    if ctx is not None:
        cache_kr = jnp.pad(ctx[1], ((0, 0), (0, 0), (KR_OFF, LANE - KR_OFF - MLA_ROPE)))
        ckv_all = jnp.concatenate([ckv3, ctx[0]], axis=1)
        kr_all = jnp.concatenate([krp3, cache_kr], axis=1)
    else:
        ckv_all, kr_all = ckv3, krp3
    lk = ckv_all.shape[1]
    k_a, v_a = _mla_kv(ckv_all.reshape(batch * lk, MLA_KV_LORA), kr_all.reshape(batch * lk, LANE), lw, consts)
    ot_a = _mla_attn(q_a, k_a.reshape(batch, lk, 512), _vt_with_ones(v_a.reshape(batch, lk, BRANCH)), batch, seq_len)
    out_a = ot_a.transpose(0, 2, 1).reshape(n, BRANCH)

    q_b, k_b, kd = _diff_prep(p, lw, consts, seq_len, rope_tabs)
    dv = p[:, P_DV:P_DV + BRANCH]
    k_b3 = k_b.reshape(batch, seq_len, BRANCH)
    v_b3 = dv.astype(BF16).reshape(batch, seq_len, BRANCH)
    if ctx is not None:
        k_b3 = jnp.concatenate([k_b3, ctx[2].reshape(batch, -1, BRANCH).astype(BF16)], axis=1)
        v_b3 = jnp.concatenate([v_b3, ctx[3].reshape(batch, -1, BRANCH).astype(BF16)], axis=1)
    ot_b = _diff_attn(q_b, k_b3, _vt_with_ones(v_b3), lw["diff_lambda"], lw["subln_col"], lw["lam_init"],
                      batch, seq_len)
    out_b = ot_b.transpose(0, 2, 1).reshape(n, BRANCH)

    if ctx is not None:
        s0 = ctx[4]
    else:
        s0 = jnp.zeros((batch, 2, HGRN_HEADS, HGRN_DK, HGRN_DK), F32)
    eye = jnp.eye(HGRN_HEADS, dtype=F32)
    st0 = jnp.einsum("bdhke,hg->bdhegk", s0, eye).reshape(batch, 2, BRANCH, BRANCH)
    o_hgrn, st_out = _hgrn(p, lw["hgrn_lb"], consts["hgrn_dd"], st0, batch, seq_len)
    st5 = st_out.reshape(batch, 2, HGRN_HEADS, HGRN_DK, HGRN_HEADS, HGRN_DK)
    states = jnp.stack([st5[:, :, h, :, h, :] for h in range(HGRN_HEADS)], axis=2).swapaxes(-1, -2)

    hu = p[:, P_HU:P_HU + 3 * HY_CH].reshape(batch, seq_len, 3 * HY_CH)
    hu_pad = jnp.pad(hu, ((0, 0), (1, 1), (0, 0)))
    v_d, x1, x2g = _hy_conv3(hu_pad[:, :-2].reshape(n, -1), hu.reshape(n, -1), hu_pad[:, 2:].reshape(n, -1),
                             lw["hy_conv_w"], lw["hy_conv_b"])
    filt = _hy_filter(hy["feats"], hy["window"], lw["hy_w1"], lw["hy_b1"], lw["hy_w2"], lw["hy_b2"], lw["hy_w3"],
                      lw["hy_freq"])
    taps = []
    for o in range(HY_ORDER):
        hf = filt[:, (2 * o) * HY_CH:(2 * o + 1) * HY_CH]
        hb = filt[:, (2 * o + 1) * HY_CH:(2 * o + 2) * HY_CH]
        taps.append(jnp.concatenate([hf, hb[::-1]], axis=0))
    taps = jnp.concatenate(taps, axis=1).astype(BF16)[None]
    tb = hy["tb"]
    spec = _dft_fwd(hy["fwd_full"], taps, None, tb, F32)[0]
    z = v_d.reshape(batch, seq_len, HY_CH)
    for o, xg in enumerate((x1, x2g)):
        yspec = _dft_fwd(hy["fwd_half"], z.astype(BF16), spec[..., o * HY_CH:(o + 1) * HY_CH], tb, BF16)
        z = _dft_inv(hy["inv"], yspec.reshape(batch, -1, HY_CH), xg.reshape(batch, seq_len, HY_CH), z,
                     lw["hy_bias"][o:o + 1])
    out_d = z.reshape(n, HY_CH)

    y = _outproj(x2, mod, out_a, out_b, o_hgrn, out_d, p, lw["hgrn_out_g"], consts["seg64"], lw["w_out"], seq_len)
    new = None
    if ctx is None:
        new = (ckv3, krp3[:, :, KR_OFF:KR_OFF + MLA_ROPE],
               kd.reshape(batch, seq_len, DIFF_HEADS, 2, DIFF_HD),
               dv.reshape(batch, seq_len, DIFF_HEADS, 2 * DIFF_HD), states)
    return y, new


def _rope_tables(seq_len):
    half = MLA_ROPE // 2
    inv = ROPE_BASE ** (-jnp.arange(0, half, 2, dtype=F32) / half)
    rows = seq_len // GRID_W
    row = jnp.repeat(jnp.arange(rows, dtype=F32), GRID_W)
    col = (jnp.arange(rows * GRID_W) % GRID_W).astype(F32)
    ar, ac = row[:, None] * inv, col[:, None] * inv
    cos32 = jnp.concatenate([jnp.cos(ar), jnp.cos(ar), jnp.cos(ac), jnp.cos(ac)], axis=-1)
    sin32 = jnp.concatenate([-jnp.sin(ar), jnp.sin(ar), -jnp.sin(ac), jnp.sin(ac)], axis=-1)
    pad = ((0, 0), (KR_OFF, LANE - KR_OFF - MLA_ROPE))
    return dict(cos_mla=jnp.pad(cos32, pad, constant_values=1.0), sin_mla=jnp.pad(sin32, pad),
                cos_diff=jnp.tile(cos32, (1, 2 * DIFF_HEADS)), sin_diff=jnp.tile(sin32, (1, 2 * DIFF_HEADS)))


def _hy_tables(seq_len):
    tb = 256
    fwd_full, inv = _dft_tables(seq_len, tb)
    feats, window = _hy_static(seq_len)
    return dict(tb=tb, fwd_full=fwd_full, fwd_half=fwd_full[:, :seq_len], inv=inv, feats=feats, window=window)


def _layer_weights(l, w_in_p, lb, W):
    def head_pad(w, width, per):
        k = w.shape[0]
        w = w.reshape(k, MLA_HEADS, per)[:, :, :width]
        return jnp.pad(w, ((0, 0), (0, 0), (0, LANE - width))).reshape(k, MLA_HEADS * LANE)

    w_ukv = W["mla_w_ukv"][l].reshape(MLA_KV_LORA, MLA_HEADS, MLA_NOPE + MLA_V)
    nope_g, rope_g = W["mla_nope_g"][l], W["mla_rope_g"][l]
    zeros32 = jnp.zeros((MLA_ROPE,), F32)
    zeros64 = jnp.zeros((MLA_NOPE,), F32)
    gq = jnp.tile(jnp.concatenate([nope_g[0], rope_g[0], zeros32]), MLA_HEADS).reshape(1, 512)
    gk = jnp.tile(jnp.concatenate([nope_g[1], zeros64]), MLA_HEADS).reshape(1, 512)
    gkr = jnp.concatenate([zeros64, rope_g[1], zeros32]).reshape(1, LANE)
    return dict(
        norm_g=W["norm_g"][l], w_in=w_in_p[l], w_out=W["w_out"][l].astype(BF16),
        qn_g=W["mla_q_norm_g"][l].reshape(1, -1),
        w_uq=head_pad(W["mla_w_uq"][l], MLA_NOPE + MLA_ROPE, MLA_NOPE + MLA_ROPE).astype(BF16),
        kvn_g=W["mla_kv_norm_g"][l].reshape(1, -1),
        w_uk=jnp.pad(w_ukv[:, :, :MLA_NOPE], ((0, 0), (0, 0), (0, LANE - MLA_NOPE))).reshape(MLA_KV_LORA, 512)
        .astype(BF16),
        w_uv=w_ukv[:, :, MLA_NOPE:].reshape(MLA_KV_LORA, BRANCH).astype(BF16),
        gq=gq, gk=gk, gkr=gkr,
        dgq=jnp.tile(W["diff_qk_g"][l, 0], 2 * DIFF_HEADS).reshape(1, BRANCH),
        dgk=jnp.tile(W["diff_qk_g"][l, 1], 2 * DIFF_HEADS).reshape(1, BRANCH),
        diff_lambda=W["diff_lambda"][l], subln_col=W["diff_subln_g"][l].reshape(2 * DIFF_HD, 1),
        lam_init=0.8 - 0.6 * math.exp(-0.3 * l),
        hgrn_lb=lb[:, l].reshape(2, 1, BRANCH),
        hgrn_out_g=jnp.tile(W["hgrn_out_g"][l], HGRN_HEADS).reshape(1, BRANCH),
        hy_conv_w=W["hy_conv_w"][l], hy_conv_b=W["hy_conv_b"][l].reshape(1, -1),
        hy_w1=jnp.pad(W["hy_w1"][l], ((0, LANE - HY_EMB), (0, 0))), hy_b1=W["hy_b1"][l].reshape(1, -1),
        hy_w2=W["hy_w2"][l], hy_b2=W["hy_b2"][l].reshape(1, -1), hy_w3=W["hy_w3"][l],
        hy_freq=W["hy_sin_freq"][l], hy_bias=W["hy_bias"][l],
    )


def kernel(x_prompt, x_sample, cache_mla_ckv, cache_mla_krope, cache_diff_k, cache_diff_v, state_hgrn, c, c_ctx,
           norm_g, w_mod, b_mod, w_in, w_out, mla_q_norm_g, mla_w_uq, mla_kv_norm_g, mla_w_ukv, mla_nope_g,
           mla_rope_g, diff_qk_g, diff_lambda, diff_subln_g, hgrn_lb_logits, hgrn_out_g, hy_conv_w, hy_conv_b,
           hy_w1, hy_b1, hy_w2, hy_b2, hy_w3, hy_sin_freq, hy_bias):
    W = dict(norm_g=norm_g, w_out=w_out, mla_q_norm_g=mla_q_norm_g, mla_w_uq=mla_w_uq,
             mla_kv_norm_g=mla_kv_norm_g, mla_w_ukv=mla_w_ukv, mla_nope_g=mla_nope_g, mla_rope_g=mla_rope_g,
             diff_qk_g=diff_qk_g, diff_lambda=diff_lambda, diff_subln_g=diff_subln_g, hgrn_out_g=hgrn_out_g,
             hy_conv_w=hy_conv_w, hy_conv_b=hy_conv_b, hy_w1=hy_w1, hy_b1=hy_b1, hy_w2=hy_w2, hy_b2=hy_b2,
             hy_w3=hy_w3, hy_sin_freq=hy_sin_freq, hy_bias=hy_bias)
    bp, lp, _ = x_prompt.shape
    bs, ls, _ = x_sample.shape

    w_in_z = jnp.pad(w_in, ((0, 0), (0, 0), (0, 1)))
    w_in_p = jnp.take(w_in_z, jnp.asarray(_in_col_perm()), axis=2).astype(BF16)
    cvecs = jnp.concatenate([c_ctx[None, :], c, jnp.zeros((8 - 1 - bs, D_MODEL), F32)], axis=0)
    mods = _mod_all(cvecs, w_mod, b_mod)
    lb = _hgrn_lb(hgrn_lb_logits)
    seg512, cnt512 = _mla_seg()
    consts = dict(seg512=seg512, cnt512=cnt512, seg32=_seg_const(BRANCH, DIFF_HD), seg64=_seg_const(BRANCH, HGRN_DK),
                  hgrn_dd=_hgrn_consts())
    lws = [_layer_weights(l, w_in_p, lb, W) for l in range(DEPTH)]

    hy_p = _hy_tables(lp)
    y = x_prompt.reshape(bp * lp, D_MODEL)
    per_layer = []
    for l in range(DEPTH):
        mod = mods[l, 0:1].reshape(1, 3, D_MODEL)
        y, new = _layer(y, mod, lws[l], consts, bp, lp, None, None, hy_p)
        per_layer.append(new)
    y_prompt = y.reshape(bp, lp, D_MODEL)
    news = [jnp.stack([s[i] for s in per_layer], axis=1) for i in range(5)]

    hy_s = _hy_tables(ls)
    rope_tabs = _rope_tables(ls)
    y = x_sample.reshape(bs * ls, D_MODEL)
    for l in range(DEPTH):
        mod = mods[l, 1:1 + bs].reshape(bs, 3, D_MODEL)
        ctx = (cache_mla_ckv[:, l], cache_mla_krope[:, l], cache_diff_k[:, l], cache_diff_v[:, l], state_hgrn[:, l])
        y, _ = _layer(y, mod, lws[l], consts, bs, ls, ctx, rope_tabs, hy_s)
    y_sample = y.reshape(bs, ls, D_MODEL)

    return (y_prompt, y_sample, news[0], news[1], news[2], news[3], news[4])
```

```python
import functools
import math

import numpy as np
import jax
import jax.numpy as jnp
from jax import lax
from jax.experimental import pallas as pl
from jax.experimental.pallas import tpu as pltpu

F32 = jnp.float32
BF16 = jnp.bfloat16

D_MODEL = 1024
DEPTH = 4
GRID_W = 64
ROPE_BASE = 10000.0
EPS = 1e-6
BRANCH = 256
MLA_HEADS = 4
MLA_NOPE = 64
MLA_ROPE = 32
MLA_V = 64
MLA_Q_LORA = 256
MLA_KV_LORA = 128
MLA_SCALE = (MLA_NOPE + MLA_ROPE) ** -0.5
DIFF_HEADS = 4
DIFF_HD = 32
DIFF_SCALE = DIFF_HD ** -0.5
HGRN_HEADS = 4
HGRN_DK = 64
HGRN_CHUNK = 128
HGRN_LEVELS = 7
HY_CH = 256
HY_ORDER = 2
HY_EMB = 33
HY_BANDS = 16
HY_FH = 64
HY_DECAY_TARGET = 0.01
HY_FAST_DECAY = 0.3
HY_SLOW_DECAY = 1.5
IN_COLS = 4000

LANE = 128
LOG2E = math.log2(math.e)
VT_ROWS = 80
FFT_N2 = 128
FFT_RT = 32
MLA_SUB = 128
DIFF_SUB = 256
VMEM_LIMIT = 52 * 1024 * 1024

P_CQ, P_CKV, P_KR, P_DQ, P_DK, P_DV = 0, 256, 384, 512, 768, 1024
P_HQ, P_HZF, P_HZB, P_HI, P_HU, P_GATE = 1280, 1536, 1792, 2048, 2304, 3072
P_COLS = 4096
KR_OFF = 64


def _in_col_perm():
    src = np.full((P_COLS,), IN_COLS, np.int32)

    def put(dst, lo, n):
        src[dst:dst + n] = np.arange(lo, lo + n)

    put(P_CQ, 0, 256)
    put(P_CKV, 256, 128)
    put(P_KR + KR_OFF, 384, 32)
    put(P_GATE, 416, 256)
    put(P_DQ, 672, 256)
    put(P_DK, 928, 256)
    put(P_DV, 1184, 256)
    put(P_GATE + 256, 1440, 256)
    put(P_HQ, 1696, 256)
    put(P_HZF, 1952, 256)
    put(P_HZB, 2208, 256)
    put(P_HI, 2464, 256)
    put(P_GATE + 512, 2720, 256)
    put(P_HU, 2976, 768)
    put(P_GATE + 768, 3744, 256)
    return src


def _cparams(sem):
    return pltpu.CompilerParams(dimension_semantics=sem, vmem_limit_bytes=VMEM_LIMIT)


def _bdot(a, b):
    return jnp.dot(a.astype(BF16), b.astype(BF16), preferred_element_type=F32)


def _nt(a, b):
    return lax.dot_general(a.astype(BF16), b.astype(BF16), (((1,), (1,)), ((), ())), preferred_element_type=F32)


def _split2(a):
    hi = a.astype(BF16)
    lo = (a - hi.astype(F32)).astype(BF16)
    return hi, lo


def _dot3(a, b):
    ah, al = _split2(a)
    bh, bl = _split2(b)
    d = functools.partial(jnp.dot, preferred_element_type=F32)
    return d(ah, bh) + d(ah, bl) + d(al, bh)


def _segsum(v, seg):
    hi, lo = _split2(v)
    d = functools.partial(jnp.dot, preferred_element_type=F32)
    return d(hi, seg) + d(lo, seg)


def _rms(x, g):
    return x * lax.rsqrt(jnp.mean(x * x, axis=-1, keepdims=True) + EPS) * g


def _swap8(x):
    w = x.shape[-1]
    lane = lax.broadcasted_iota(jnp.int32, x.shape, x.ndim - 1)
    up = pltpu.roll(x, w - 8, x.ndim - 1)
    dn = pltpu.roll(x, 8, x.ndim - 1)
    return jnp.where((lane & 15) < 8, up, dn)


def _tile_lanes(x, n):
    return x if n == 1 else jnp.concatenate([x] * n, axis=-1)


def _mod_kernel(c_ref, w_ref, b_ref, o_ref):
    c = c_ref[...]
    o_ref[0] = _dot3(c * jax.nn.sigmoid(c), w_ref[0]) + b_ref[0]


def _mod_all(cvecs, w_mod, b_mod):
    nt = 3
    return pl.pallas_call(
        _mod_kernel,
        grid=(DEPTH, nt),
        in_specs=[pl.BlockSpec((8, D_MODEL), lambda l, j: (0, 0)),
                  pl.BlockSpec((1, D_MODEL, D_MODEL), lambda l, j: (l, 0, j)),
                  pl.BlockSpec((1, 1, D_MODEL), lambda l, j: (l, 0, j))],
        out_specs=pl.BlockSpec((1, 8, D_MODEL), lambda l, j: (l, 0, j)),
        out_shape=jax.ShapeDtypeStruct((DEPTH, 8, 3 * D_MODEL), F32),
        compiler_params=_cparams(("arbitrary", "arbitrary")),
        name="mod",
    )(cvecs, w_mod, b_mod.reshape(DEPTH, 1, 3 * D_MODEL))


def _lb_kernel(x_ref, o_ref):
    x = x_ref[...]
    rows = [x[l:l + 1, :] for l in range(DEPTH)]
    m = functools.reduce(jnp.maximum, rows)
    e = [jnp.exp(r - m) for r in rows]
    tot = functools.reduce(lambda a, b: a + b, e)
    acc = jnp.zeros_like(tot)
    o_ref[0:1, :] = acc
    for l in range(1, DEPTH):
        acc = acc + e[l] / tot
        o_ref[l:l + 1, :] = acc


def _hgrn_lb(logits):
    flat = logits.transpose(1, 0, 2).reshape(DEPTH, 2 * BRANCH)
    lb = pl.pallas_call(
        _lb_kernel,
        out_shape=jax.ShapeDtypeStruct(flat.shape, F32),
        name="hgrn_lb",
    )(flat)
    return lb.reshape(DEPTH, 2, BRANCH).transpose(1, 0, 2)


def _inproj_kernel(x_ref, mod_ref, g_ref, w_ref, p_ref):
    h = _rms(x_ref[...], g_ref[...]) * (1.0 + mod_ref[0, 1:2, :]) + mod_ref[0, 0:1, :]
    p_ref[...] = jnp.dot(h.astype(BF16), w_ref[...], preferred_element_type=F32)


def _inproj(x2, mod, norm_g, w_in_p, seq_len):
    n = x2.shape[0]
    tm = 256
    per_batch = mod.shape[0] > 1
    tiles_per_seq = seq_len // tm
    mod_idx = (lambda i: (i // tiles_per_seq, 0, 0)) if per_batch else (lambda i: (0, 0, 0))
    return pl.pallas_call(
        _inproj_kernel,
        grid=(n // tm,),
        in_specs=[pl.BlockSpec((tm, D_MODEL), lambda i: (i, 0)),
                  pl.BlockSpec((1, 3, D_MODEL), mod_idx),
                  pl.BlockSpec((1, D_MODEL), lambda i: (0, 0)),
                  pl.BlockSpec((D_MODEL, P_COLS), lambda i: (0, 0))],
        out_specs=pl.BlockSpec((tm, P_COLS), lambda i: (i, 0)),
        out_shape=jax.ShapeDtypeStruct((n, P_COLS), F32),
        compiler_params=_cparams(("parallel",)),
        name="inproj",
    )(x2, mod, norm_g.reshape(1, D_MODEL), w_in_p)


def _mla_seg():
    sid = np.zeros((512,), np.int32)
    cnt = np.ones((512,), np.float32)
    for h in range(MLA_HEADS):
        b = 128 * h
        sid[b:b + 64] = 3 * h
        sid[b + 64:b + 96] = 3 * h + 1
        sid[b + 96:b + 128] = 3 * h + 2
        cnt[b:b + 64] = 1.0 / 64
        cnt[b + 64:b + 128] = 1.0 / 32
    seg = (sid[:, None] == sid[None, :]).astype(np.float32)
    return jnp.asarray(seg, BF16), jnp.asarray(cnt.reshape(1, 512))


def _mla_q_kernel(rope, cq_ref, ckv_ref, kr_ref, qng_ref, wuq_ref, kvg_ref, gq_ref, gkr_ref, seg_ref, cnt_ref,
                  *rest):
    if rope:
        cos_ref, sin_ref, q_ref, ckvn_ref, krp_ref = rest
    else:
        q_ref, ckvn_ref, krp_ref = rest
    cqn = _rms(cq_ref[...], qng_ref[...])
    q = _bdot(cqn, wuq_ref[...])
    ss = _segsum(q * q, seg_ref[...]) * cnt_ref[...]
    qn = q * lax.rsqrt(ss + EPS) * gq_ref[...]
    ckvn_ref[...] = _rms(ckv_ref[...], kvg_ref[...])
    kr = kr_ref[...]
    krn = kr * lax.rsqrt(jnp.sum(kr * kr, axis=-1, keepdims=True) * (1.0 / MLA_ROPE) + EPS) * gkr_ref[...]
    if rope:
        cos, sin = cos_ref[...], sin_ref[...]
        qn = qn * _tile_lanes(cos, MLA_HEADS) + _swap8(qn) * _tile_lanes(sin, MLA_HEADS)
        krn = krn * cos + _swap8(krn) * sin
    q_ref[...] = (qn * (MLA_SCALE * LOG2E)).astype(BF16)
    krp_ref[...] = krn


def _mla_q(p, lw, consts, seq_len, rope_tabs):
    n = p.shape[0]
    tm = 256
    rope = rope_tabs is not None
    full = lambda shape: pl.BlockSpec(shape, lambda i: (0,) * len(shape))
    in_specs = [pl.BlockSpec((tm, 256), lambda i: (i, P_CQ // 256)),
                pl.BlockSpec((tm, 128), lambda i: (i, P_CKV // 128)),
                pl.BlockSpec((tm, 128), lambda i: (i, P_KR // 128)),
                full((1, 256)), full((256, 512)), full((1, 128)), full((1, 512)), full((1, 128)),
                full((512, 512)), full((1, 512))]
    args = [p, p, p, lw["qn_g"], lw["w_uq"], lw["kvn_g"], lw["gq"], lw["gkr"], consts["seg512"], consts["cnt512"]]
    if rope:
        tps = seq_len // tm
        in_specs += [pl.BlockSpec((tm, 128), lambda i: (i % tps, 0))] * 2
        args += [rope_tabs["cos_mla"], rope_tabs["sin_mla"]]
    return pl.pallas_call(
        functools.partial(_mla_q_kernel, rope),
        grid=(n // tm,),
        in_specs=in_specs,
        out_specs=[pl.BlockSpec((tm, 512), lambda i: (i, 0)),
                   pl.BlockSpec((tm, 128), lambda i: (i, 0)),
                   pl.BlockSpec((tm, 128), lambda i: (i, 0))],
        out_shape=[jax.ShapeDtypeStruct((n, 512), BF16),
                   jax.ShapeDtypeStruct((n, 128), F32),
                   jax.ShapeDtypeStruct((n, 128), F32)],
        compiler_params=_cparams(("parallel",)),
        name="mla_q",
    )(*args)


def _mla_kv_kernel(ckvn_ref, krp_ref, wuk_ref, wuv_ref, gk_ref, seg_ref, cnt_ref, k_ref, v_ref):
    c = ckvn_ref[...].astype(BF16)
    kn = jnp.dot(c, wuk_ref[...], preferred_element_type=F32)
    ss = _segsum(kn * kn, seg_ref[...]) * cnt_ref[...]
    k = kn * lax.rsqrt(ss + EPS) * gk_ref[...] + _tile_lanes(krp_ref[...], MLA_HEADS)
    k_ref[...] = k.astype(BF16)
    v_ref[...] = jnp.dot(c, wuv_ref[...], preferred_element_type=F32).astype(BF16)


def _mla_kv(ckvn, krp, lw, consts):
    n = ckvn.shape[0]
    tm = 512
    full = lambda shape: pl.BlockSpec(shape, lambda i: (0,) * len(shape))
    return pl.pallas_call(
        _mla_kv_kernel,
        grid=(n // tm,),
        in_specs=[pl.BlockSpec((tm, 128), lambda i: (i, 0)), pl.BlockSpec((tm, 128), lambda i: (i, 0)),
                  full((128, 512)), full((128, 256)), full((1, 512)), full((512, 512)), full((1, 512))],
        out_specs=[pl.BlockSpec((tm, 512), lambda i: (i, 0)), pl.BlockSpec((tm, 256), lambda i: (i, 0))],
        out_shape=[jax.ShapeDtypeStruct((n, 512), BF16), jax.ShapeDtypeStruct((n, 256), BF16)],
        compiler_params=_cparams(("parallel",)),
        name="mla_kv",
    )(ckvn, krp, lw["w_uk"], lw["w_uv"], lw["gk"], consts["seg512"], consts["cnt512"])


def _softmax_pv(qs, k_ref, vt_ref, key_chunk, sub_rows):
    lk = k_ref.shape[1]
    nch = lk // key_chunk
    nq = len(qs)
    sub = min(sub_rows, key_chunk)
    nsub = key_chunk // sub

    def scores(c, u):
        lo = c * key_chunk + u * sub
        ks = k_ref[0, lo:lo + sub, :]
        return [_nt(ks, q) for q in qs]

    m = [None] * nq
    acc = [None] * nq
    s_next = [scores(0, u) for u in range(nsub)]
    for c in range(nch):
        s_cur = s_next
        s_next = []
        m_new = []
        for j in range(nq):
            mc = functools.reduce(jnp.maximum, [s_cur[u][j] for u in range(nsub)])
            mc = jnp.max(mc, axis=0, keepdims=True)
            m_new.append(mc if c == 0 else jnp.maximum(m[j], mc))
        p = [[] for _ in range(nq)]
        for u in range(nsub):
            if c + 1 < nch:
                s_next.append(scores(c + 1, u))
            for j in range(nq):
                p[j].append(jnp.exp2(s_cur[u][j] - m_new[j]).astype(BF16))
        vs = vt_ref[0, 0, :, c * key_chunk:(c + 1) * key_chunk]
        for j in range(nq):
            pv = jnp.dot(vs, jnp.concatenate(p[j], axis=0), preferred_element_type=F32)
            acc[j] = pv if c == 0 else acc[j] * jnp.exp2(m[j] - m_new[j]) + pv
            m[j] = m_new[j]
    return acc


def _mla_attn_kernel(key_chunk, q_ref, k_ref, vt_ref, o_ref):
    (acc,) = _softmax_pv([q_ref[...]], k_ref, vt_ref, key_chunk, MLA_SUB)
    o_ref[0] = acc[0:MLA_V] / acc[MLA_V:MLA_V + 1]


def _key_chunk(lk):
    return 512 if lk % 512 == 0 else lk


def _vt_with_ones(v3):
    b, lk, _ = v3.shape
    vt = v3.reshape(b, lk, 4, 64).transpose(0, 2, 3, 1)
    extra = jnp.zeros((b, 4, VT_ROWS - 64, lk), BF16).at[:, :, 0, :].set(1.0)
    return jnp.concatenate([vt, extra], axis=2)


def _mla_attn(q, k, vt, batch, seq_len):
    lk = k.shape[1]
    tq = 256
    nq = seq_len // tq
    return pl.pallas_call(
        functools.partial(_mla_attn_kernel, _key_chunk(lk)),
        grid=(batch, MLA_HEADS, nq),
        in_specs=[pl.BlockSpec((tq, 128), lambda b, h, i: (b * nq + i, h)),
                  pl.BlockSpec((1, lk, 128), lambda b, h, i: (b, 0, h)),
                  pl.BlockSpec((1, 1, VT_ROWS, lk), lambda b, h, i: (b, h, 0, 0))],
        out_specs=pl.BlockSpec((1, MLA_V, tq), lambda b, h, i: (b, h, i)),
        out_shape=jax.ShapeDtypeStruct((batch, BRANCH, seq_len), F32),
        compiler_params=_cparams(("parallel", "parallel", "arbitrary")),
        name="mla_attn",
    )(q, k, vt)


def _seg_const(width, seg):
    sid = np.arange(width) // seg
    return jnp.asarray((sid[:, None] == sid[None, :]).astype(np.float32), BF16)


def _diff_prep_kernel(rope, dq_ref, dk_ref, gq_ref, gk_ref, seg_ref, *rest):
    if rope:
        cos_ref, sin_ref, q_ref, k_ref, kf_ref = rest
    else:
        q_ref, k_ref, kf_ref = rest
    seg = seg_ref[...]

    def norm(x, g):
        ss = _segsum(x * x, seg) * (1.0 / DIFF_HD)
        return x * lax.rsqrt(ss + EPS) * g

    q = norm(dq_ref[...], gq_ref[...])
    k = norm(dk_ref[...], gk_ref[...])
    kf_ref[...] = k
    if rope:
        cos, sin = cos_ref[...], sin_ref[...]
        q = q * cos + _swap8(q) * sin
        k = k * cos + _swap8(k) * sin
    q_ref[...] = (q * (DIFF_SCALE * LOG2E)).astype(BF16)
    k_ref[...] = k.astype(BF16)


def _diff_prep(p, lw, consts, seq_len, rope_tabs):
    n = p.shape[0]
    tm = 256
    rope = rope_tabs is not None
    full = lambda shape: pl.BlockSpec(shape, lambda i: (0,) * len(shape))
    in_specs = [pl.BlockSpec((tm, 256), lambda i: (i, P_DQ // 256)),
                pl.BlockSpec((tm, 256), lambda i: (i, P_DK // 256)),
                full((1, 256)), full((1, 256)), full((256, 256))]
    args = [p, p, lw["dgq"], lw["dgk"], consts["seg32"]]
    if rope:
        tps = seq_len // tm
        in_specs += [pl.BlockSpec((tm, 256), lambda i: (i % tps, 0))] * 2
        args += [rope_tabs["cos_diff"], rope_tabs["sin_diff"]]
    blk = pl.BlockSpec((tm, 256), lambda i: (i, 0))
    return pl.pallas_call(
        functools.partial(_diff_prep_kernel, rope),
        grid=(n // tm,),
        in_specs=in_specs,
        out_specs=[blk, blk, blk],
        out_shape=[jax.ShapeDtypeStruct((n, 256), BF16), jax.ShapeDtypeStruct((n, 256), BF16),
                   jax.ShapeDtypeStruct((n, 256), F32)],
        compiler_params=_cparams(("parallel",)),
        name="diff_prep",
    )(*args)


def _diff_attn_kernel(lam_init, key_chunk, q_ref, k_ref, vt_ref, lp_ref, g_ref, o_ref):
    h = pl.program_id(1)
    q = q_ref[...]
    lane = lax.broadcasted_iota(jnp.int32, q.shape, 1)
    base = (h % 2) * 64
    zero = jnp.zeros_like(q)

    def map_query(j):
        lo = base + 32 * j
        return jnp.where((lane >= lo) & (lane < lo + 32), q, zero)

    acc0, acc1 = _softmax_pv([map_query(0), map_query(1)], k_ref, vt_ref, key_chunk, DIFF_SUB)
    lp = lp_ref[...]
    lam = (jnp.exp(jnp.sum(lp[0:1] * lp[1:2], axis=1, keepdims=True))
           - jnp.exp(jnp.sum(lp[2:3] * lp[3:4], axis=1, keepdims=True)) + lam_init)
    o = acc0[0:64] / acc0[64:65] - lam * (acc1[0:64] / acc1[64:65])
    ms = jnp.mean(o * o, axis=0, keepdims=True)
    o_ref[0] = o * lax.rsqrt(ms + EPS) * g_ref[...] * (1.0 - lam_init)


def _diff_attn(q, k, vt, lp, g_col, lam_init, batch, seq_len):
    lk = k.shape[1]
    tq = 256
    nq = seq_len // tq
    return pl.pallas_call(
        functools.partial(_diff_attn_kernel, lam_init, _key_chunk(lk)),
        grid=(batch, DIFF_HEADS, nq),
        in_specs=[pl.BlockSpec((tq, 128), lambda b, h, i: (b * nq + i, h // 2)),
                  pl.BlockSpec((1, lk, 128), lambda b, h, i: (b, 0, h // 2)),
                  pl.BlockSpec((1, 1, VT_ROWS, lk), lambda b, h, i: (b, h, 0, 0)),
                  pl.BlockSpec((4, DIFF_HD), lambda b, h, i: (0, 0)),
                  pl.BlockSpec((64, 1), lambda b, h, i: (0, 0))],
        out_specs=pl.BlockSpec((1, 64, tq), lambda b, h, i: (b, h, i)),
        out_shape=jax.ShapeDtypeStruct((batch, BRANCH, seq_len), F32),
        compiler_params=_cparams(("parallel", "parallel", "arbitrary")),
        name="diff_attn",
    )(q, k, vt, lp, g_col)


def _hgrn_consts():
    c = HGRN_CHUNK
    t = np.arange(c)
    low = (t[None, :] <= t[:, None]).astype(np.float32)
    blocks = []
    for j in range(HGRN_LEVELS):
        m = 1 << j
        rho = (t // (2 * m)) * (2 * m) + m - 1
        blocks.append(low - (t[None, :] <= rho[:, None]).astype(np.float32))
    blocks.append(low)
    fwd = np.concatenate(blocks, axis=0)
    bwd = np.concatenate([b[::-1, ::-1] for b in blocks], axis=0)
    return jnp.asarray(np.stack([fwd, bwd]), BF16)


def _hgrn_kernel(nc, q_ref, z_ref, v_ref, lb_ref, dd_ref, s0_ref, o_ref, sout_ref, st_ref):
    c = HGRN_CHUNK
    d = pl.program_id(1)
    ci = pl.program_id(2)

    @pl.when(ci == 0)
    def _():
        st_ref[...] = s0_ref[0, 0]

    lb = lb_ref[0]
    z = z_ref[...]
    q = q_ref[...]
    v = v_ref[...]
    f = lb + (1.0 - lb) * jax.nn.sigmoid(z)
    g = jnp.log(f)
    kk = (1.0 - lb) * jax.nn.sigmoid(-z)

    gh, gl = _split2(g)
    dd = dd_ref[0]
    u = jnp.dot(dd, gh, preferred_element_type=F32) + jnp.dot(dd, gl, preferred_element_type=F32)

    row = lax.broadcasted_iota(jnp.int32, (c, 1), 0)
    row_dir = jnp.where(d == 0, row, c - 1 - row)
    lane = lax.broadcasted_iota(jnp.int32, (1, BRANCH), 1)
    head_masks = [(lane >= HGRN_DK * h) & (lane < HGRN_DK * (h + 1)) for h in range(HGRN_HEADS)]
    t_idx = lax.broadcasted_iota(jnp.int32, (c, HGRN_HEADS * c), 0)
    s_idx = lax.broadcasted_iota(jnp.int32, (c, HGRN_HEADS * c), 1) & (c - 1)

    def stack_heads(x):
        xb = x.astype(BF16)
        zero = jnp.zeros_like(xb)
        return jnp.concatenate([jnp.where(hm, xb, zero) for hm in head_masks], axis=0)

    a = jnp.where(t_idx == s_idx, _nt(q, stack_heads(kk)), 0.0)
    for j in range(HGRN_LEVELS):
        e = jnp.exp(-jnp.abs(u[j * c:(j + 1) * c]))
        right = ((row_dir >> j) & 1) == 1
        qt = jnp.where(right, q * e, 0.0)
        kt = jnp.where(right, 0.0, kk * e)
        same = (t_idx >> (j + 1)) == (s_idx >> (j + 1))
        a = a + jnp.where(same, _nt(qt, stack_heads(kt)), 0.0)

    b = u[HGRN_LEVELS * c:]
    b_tot = jnp.where(d == 0, b[c - 1:c], b[0:1])
    st = st_ref[...]
    o = jnp.dot(a.astype(BF16), stack_heads(v), preferred_element_type=F32)
    o = o + _nt(q * jnp.exp(b), st)
    o_ref[0] = o

    kd = kk * jnp.exp(b_tot - b)
    upd = lax.dot_general(v.astype(BF16), kd.astype(BF16), (((0,), (0,)), ((), ())), preferred_element_type=F32)
    r2 = lax.broadcasted_iota(jnp.int32, (BRANCH, BRANCH), 0) // HGRN_DK
    c2 = lax.broadcasted_iota(jnp.int32, (BRANCH, BRANCH), 1) // HGRN_DK
    st_new = st * jnp.exp(b_tot) + jnp.where(r2 == c2, upd, 0.0)
    st_ref[...] = st_new

    @pl.when(ci == nc - 1)
    def _():
        sout_ref[0, 0] = st_new


def _hgrn(p, lb_l, dd, st0, batch, seq_len):
    n = p.shape[0]
    c = HGRN_CHUNK
    nc = seq_len // c

    def rows(b, d, i):
        return b * nc + i + d * (nc - 1 - 2 * i)

    return pl.pallas_call(
        functools.partial(_hgrn_kernel, nc),
        grid=(batch, 2, nc),
        in_specs=[pl.BlockSpec((c, 256), lambda b, d, i: (rows(b, d, i), P_HQ // 256)),
                  pl.BlockSpec((c, 256), lambda b, d, i: (rows(b, d, i), P_HZF // 256 + d)),
                  pl.BlockSpec((c, 256), lambda b, d, i: (rows(b, d, i), P_HI // 256)),
                  pl.BlockSpec((1, 1, 256), lambda b, d, i: (d, 0, 0)),
                  pl.BlockSpec((1, (HGRN_LEVELS + 1) * c, c), lambda b, d, i: (d, 0, 0)),
                  pl.BlockSpec((1, 1, 256, 256), lambda b, d, i: (b, d, 0, 0))],
        out_specs=[pl.BlockSpec((1, c, 256), lambda b, d, i: (d, rows(b, d, i), 0)),
                   pl.BlockSpec((1, 1, 256, 256), lambda b, d, i: (b, d, 0, 0))],
        out_shape=[jax.ShapeDtypeStruct((2, n, 256), F32),
                   jax.ShapeDtypeStruct((batch, 2, 256, 256), F32)],
        scratch_shapes=[pltpu.VMEM((256, 256), F32)],
        compiler_params=_cparams(("parallel", "parallel", "arbitrary")),
        name="hgrn",
    )(p, p, p, lb_l, dd, st0)


def _hy_conv3_kernel(prev_ref, cur_ref, nxt_ref, w_ref, b_ref, v_ref, x1_ref, x2_ref):
    w = w_ref[...]
    u = prev_ref[...] * w[0:1] + cur_ref[...] * w[1:2] + nxt_ref[...] * w[2:3] + b_ref[...]
    v_ref[...] = u[:, 0:256]
    x1_ref[...] = u[:, 256:512]
    x2_ref[...] = u[:, 512:768]


def _hy_conv3(prev, cur, nxt, w, b):
    n = cur.shape[0]
    tm = 512
    blk = pl.BlockSpec((tm, 768), lambda i: (i, 0))
    oblk = pl.BlockSpec((tm, 256), lambda i: (i, 0))
    return pl.pallas_call(
        _hy_conv3_kernel,
        grid=(n // tm,),
        in_specs=[blk, blk, blk, pl.BlockSpec((3, 768), lambda i: (0, 0)), pl.BlockSpec((1, 768), lambda i: (0, 0))],
        out_specs=[oblk, oblk, oblk],
        out_shape=[jax.ShapeDtypeStruct((n, 256), F32)] * 3,
        compiler_params=_cparams(("parallel",)),
        name="hy_conv3",
    )(prev, cur, nxt, w, b)


def _hy_filter_kernel(feat_ref, w1_ref, b1_ref, w2_ref, b2_ref, w3_ref, fr_ref, win_ref, o_ref):
    fr = fr_ref[...]
    h = jnp.sin(fr[0:1] * (_dot3(feat_ref[...], w1_ref[...]) + b1_ref[...]))
    h = jnp.sin(fr[1:2] * (_dot3(h, w2_ref[...]) + b2_ref[...]))
    o_ref[...] = _dot3(h, w3_ref[...]) * _tile_lanes(win_ref[...], 2 * HY_ORDER)


def _hy_filter(feats, window, w1p, b1, w2, b2, w3, freq):
    ln = feats.shape[0]
    tm = min(512, ln)
    full = lambda shape: pl.BlockSpec(shape, lambda i: (0,) * len(shape))
    return pl.pallas_call(
        _hy_filter_kernel,
        grid=(ln // tm,),
        in_specs=[pl.BlockSpec((tm, LANE), lambda i: (i, 0)),
                  full((LANE, HY_FH)), full((1, HY_FH)), full((HY_FH, HY_FH)), full((1, HY_FH)),
                  full((HY_FH, 2 * HY_ORDER * HY_CH)), full((2, HY_FH)),
                  pl.BlockSpec((tm, HY_CH), lambda i: (i, 0))],
        out_specs=pl.BlockSpec((tm, 2 * HY_ORDER * HY_CH), lambda i: (i, 0)),
        out_shape=jax.ShapeDtypeStruct((ln, 2 * HY_ORDER * HY_CH), F32),
        compiler_params=_cparams(("parallel",)),
        name="hy_filter",
    )(feats, w1p, b1, w2, b2, w3, freq, window)


def _fft_a_kernel(fh_ref, fl_ref, x_ref, o_ref):
    x = x_ref[0]
    xh, xl = _split2(x)
    fh, fl = fh_ref[...], fl_ref[...]
    e = functools.partial(jnp.einsum, "kn,nrc->krc", preferred_element_type=F32)
    o_ref[0] = (e(fh, xh) + e(fl, xh) + e(fh, xl)).astype(o_ref.dtype)


def _fft_a(fa, x4, out_dtype):
    nb, n1, n2, ch = x4.shape
    r = fa.shape[0]
    rt = FFT_RT
    fh, fl = _split2(fa)
    return pl.pallas_call(
        _fft_a_kernel,
        grid=(nb, n2 // rt),
        in_specs=[pl.BlockSpec((r, n1), lambda b, i: (0, 0)), pl.BlockSpec((r, n1), lambda b, i: (0, 0)),
                  pl.BlockSpec((1, n1, rt, ch), lambda b, i: (b, 0, i, 0))],
        out_specs=pl.BlockSpec((1, r, rt, ch), lambda b, i: (b, 0, i, 0)),
        out_shape=jax.ShapeDtypeStruct((nb, r, n2, ch), out_dtype),
        compiler_params=_cparams(("parallel", "parallel")),
        name="fft_a",
    )(fh, fl, x4)


def _fft_b_kernel(with_inverse, mf_ref, *rest):
    if with_inverse:
        mi_ref, a_ref, h_ref, o_ref = rest
    else:
        a_ref, o_ref = rest
    half = FFT_N2
    a = jnp.concatenate([a_ref[0, 0, 0], a_ref[0, 1, 0]], axis=0)
    x = jnp.dot(mf_ref[0], a.astype(BF16), preferred_element_type=F32)
    if with_inverse:
        xr, xi = x[:half], x[half:]
        hr, hi = h_ref[0, 0], h_ref[1, 0]
        y = jnp.concatenate([xr * hr - xi * hi, xr * hi + xi * hr], axis=0)
        x = jnp.dot(mi_ref[0], y.astype(BF16), preferred_element_type=F32)
    o_ref[0, 0, 0] = x[:half]
    o_ref[0, 1, 0] = x[half:]


def _fft_b(mf, mi, a5, spec, order):
    nb, _, k1n, n2, ch = a5.shape
    mat = pl.BlockSpec((1, 2 * n2, 2 * n2), lambda b, k: (k, 0, 0))
    blk = pl.BlockSpec((1, 2, 1, n2, ch), lambda b, k: (b, 0, k, 0, 0))
    if spec is None:
        in_specs, args = [mat, blk], [mf, a5]
    else:
        in_specs = [mat, mat, blk, pl.BlockSpec((2, 1, n2, ch), lambda b, k: (0, k, 0, order))]
        args = [mf, mi, a5, spec]
    return pl.pallas_call(
        functools.partial(_fft_b_kernel, spec is not None),
        grid=(nb, k1n),
        in_specs=in_specs,
        out_specs=blk,
        out_shape=jax.ShapeDtypeStruct(a5.shape, F32),
        compiler_params=_cparams(("parallel", "parallel")),
        name="fft_b",
    )(*args)


def _fft_a_inv_kernel(gh_ref, gl_ref, p_ref, x_ref, z_ref, bias_ref, o_ref):
    ph, pl_ = _split2(p_ref[0])
    gh, gl = gh_ref[...], gl_ref[...]
    e = functools.partial(jnp.einsum, "nk,krc->nrc", preferred_element_type=F32)
    conv = e(gh, ph) + e(gl, ph) + e(gh, pl_)
    o_ref[0] = x_ref[0] * (conv + z_ref[0] * bias_ref[...])


def _fft_a_inv(g, p4, xg4, z4, bias):
    nb, n1, n2, ch = z4.shape
    r = g.shape[1]
    rt = FFT_RT
    gh, gl = _split2(g)
    blk = pl.BlockSpec((1, n1, rt, ch), lambda b, i: (b, 0, i, 0))
    return pl.pallas_call(
        _fft_a_inv_kernel,
        grid=(nb, n2 // rt),
        in_specs=[pl.BlockSpec((n1, r), lambda b, i: (0, 0)), pl.BlockSpec((n1, r), lambda b, i: (0, 0)),
                  pl.BlockSpec((1, r, rt, ch), lambda b, i: (b, 0, i, 0)), blk, blk,
                  pl.BlockSpec((1, 1, ch), lambda b, i: (0, 0, 0))],
        out_specs=blk,
        out_shape=jax.ShapeDtypeStruct(z4.shape, F32),
        compiler_params=_cparams(("parallel", "parallel")),
        name="fft_a_inv",
    )(gh, gl, p4, xg4, z4, bias.reshape(1, 1, ch))


def _fft_tables(ln):
    n = 2 * ln
    n1t = n // FFT_N2
    k1n = n1t // 2 + 1
    kk = np.arange(k1n)

    def stage_a(n1_in):
        ang = 2.0 * np.pi * ((kk[:, None] * np.arange(n1_in)[None, :]) % n1t) / n1t
        return jnp.asarray(np.concatenate([np.cos(ang), -np.sin(ang)], axis=0), F32)

    n1o = n1t // 2
    ang = 2.0 * np.pi * ((np.arange(n1o)[:, None] * kk[None, :]) % n1t) / n1t
    edge = (kk == 0) | (kk == n1t // 2)
    ck = np.where(edge, 1.0, 2.0) / n
    g = jnp.asarray(np.concatenate([ck * np.cos(ang), -ck * np.where(edge, 0.0, np.sin(ang))], axis=1), F32)

    k1 = jnp.arange(k1n, dtype=jnp.int32)[:, None, None]
    k2 = jnp.arange(FFT_N2, dtype=jnp.int32)[None, :, None]
    n2 = jnp.arange(FFT_N2, dtype=jnp.int32)[None, None, :]
    th = (2.0 * math.pi / n) * ((n2 * (k1 + n1t * k2)) % n).astype(F32)
    c, s = jnp.cos(th), jnp.sin(th)
    mf = jnp.concatenate([jnp.concatenate([c, s], axis=2), jnp.concatenate([-s, c], axis=2)], axis=1)
    return dict(fa_half=stage_a(n1o), fa_full=stage_a(n1t), g=g, mf=mf.astype(BF16),
                mi=mf.transpose(0, 2, 1).astype(BF16), k1n=k1n, n1o=n1o, n1t=n1t)


def _hy_static(ln):
    t = jnp.linspace(0.0, 1.0, ln, dtype=F32)[:, None]
    w = 2.0 * math.pi * jnp.arange(ln, dtype=F32) / ln
    f = jnp.linspace(1e-4, HY_BANDS - 1, HY_BANDS, dtype=F32)
    ang = w[:, None] * f[None, :]
    feats = jnp.concatenate([t, jnp.cos(ang), -jnp.sin(ang)], axis=-1)
    feats = jnp.pad(feats, ((0, 0), (0, LANE - HY_EMB)))
    min_decay = math.log(HY_DECAY_TARGET) / HY_SLOW_DECAY
    max_decay = math.log(HY_DECAY_TARGET) / HY_FAST_DECAY
    deltas = jnp.linspace(min_decay, max_decay, HY_CH, dtype=F32)
    window = jnp.exp(-t * jnp.abs(deltas))
    return feats, window


def _outproj_kernel(x_ref, mod_ref, oa_ref, ob_ref, of_ref, obk_ref, od_ref, gate_ref, hg_ref, seg_ref, w_ref,
                    y_ref):
    gt = gate_ref[...]
    sg = gt * jax.nn.sigmoid(gt)
    oc = of_ref[0] + obk_ref[0]
    ss = _segsum(oc * oc, seg_ref[...]) * (1.0 / HGRN_DK)
    oc = oc * lax.rsqrt(ss + EPS) * hg_ref[...]
    acc = _bdot(oa_ref[...] * sg[:, 0:256], w_ref[0:256, :])
    acc += _bdot(ob_ref[...] * sg[:, 256:512], w_ref[256:512, :])
    acc += _bdot(oc * sg[:, 512:768], w_ref[512:768, :])
    acc += _bdot(od_ref[...] * sg[:, 768:1024], w_ref[768:1024, :])
    y_ref[...] = x_ref[...] + mod_ref[0, 2:3, :] * acc


def _outproj(x2, mod, out_a, out_b, o_hgrn, out_d, p, hg, seg64, w_out, seq_len):
    n = x2.shape[0]
    tm = 256
    per_batch = mod.shape[0] > 1
    tps = seq_len // tm
    mod_idx = (lambda i: (i // tps, 0, 0)) if per_batch else (lambda i: (0, 0, 0))
    b256 = pl.BlockSpec((tm, 256), lambda i: (i, 0))
    return pl.pallas_call(
        _outproj_kernel,
        grid=(n // tm,),
        in_specs=[pl.BlockSpec((tm, D_MODEL), lambda i: (i, 0)),
                  pl.BlockSpec((1, 3, D_MODEL), mod_idx),
                  b256, b256,
                  pl.BlockSpec((1, tm, 256), lambda i: (0, i, 0)),
                  pl.BlockSpec((1, tm, 256), lambda i: (1, i, 0)),
                  b256,
                  pl.BlockSpec((tm, 1024), lambda i: (i, P_GATE // 1024)),
                  pl.BlockSpec((1, 256), lambda i: (0, 0)),
                  pl.BlockSpec((256, 256), lambda i: (0, 0)),
                  pl.BlockSpec((D_MODEL, D_MODEL), lambda i: (0, 0))],
        out_specs=pl.BlockSpec((tm, D_MODEL), lambda i: (i, 0)),
        out_shape=jax.ShapeDtypeStruct((n, D_MODEL), F32),
        compiler_params=_cparams(("parallel",)),
        name="outproj",
    )(x2, mod, out_a, out_b, o_hgrn, o_hgrn, out_d, p, hg, seg64, w_out)


def _layer(x2, mod, lw, consts, batch, seq_len, ctx, rope_tabs, hy):
    n = batch * seq_len
    p = _inproj(x2, mod, lw["norm_g"], lw["w_in"], seq_len)

    q_a, ckvn, krp = _mla_q(p, lw, consts, seq_len, rope_tabs)
    ckv3 = ckvn.reshape(batch, seq_len, MLA_KV_LORA)
    krp3 = krp.reshape(batch, seq_len, LANE)
    if ctx is not None:
        cache_kr = jnp.pad(ctx[1], ((0, 0), (0, 0), (KR_OFF, LANE - KR_OFF - MLA_ROPE)))
        ckv_all = jnp.concatenate([ckv3, ctx[0]], axis=1)
        kr_all = jnp.concatenate([krp3, cache_kr], axis=1)
    else:
        ckv_all, kr_all = ckv3, krp3
    lk = ckv_all.shape[1]
    k_a, v_a = _mla_kv(ckv_all.reshape(batch * lk, MLA_KV_LORA), kr_all.reshape(batch * lk, LANE), lw, consts)
    ot_a = _mla_attn(q_a, k_a.reshape(batch, lk, 512), _vt_with_ones(v_a.reshape(batch, lk, BRANCH)), batch, seq_len)
    out_a = ot_a.transpose(0, 2, 1).reshape(n, BRANCH)

    q_b, k_b, kd = _diff_prep(p, lw, consts, seq_len, rope_tabs)
    dv = p[:, P_DV:P_DV + BRANCH]
    k_b3 = k_b.reshape(batch, seq_len, BRANCH)
    v_b3 = dv.astype(BF16).reshape(batch, seq_len, BRANCH)
    if ctx is not None:
        k_b3 = jnp.concatenate([k_b3, ctx[2].reshape(batch, -1, BRANCH).astype(BF16)], axis=1)
        v_b3 = jnp.concatenate([v_b3, ctx[3].reshape(batch, -1, BRANCH).astype(BF16)], axis=1)
    ot_b = _diff_attn(q_b, k_b3, _vt_with_ones(v_b3), lw["diff_lambda"], lw["subln_col"], lw["lam_init"],
                      batch, seq_len)
    out_b = ot_b.transpose(0, 2, 1).reshape(n, BRANCH)

    if ctx is not None:
        s0 = ctx[4]
    else:
        s0 = jnp.zeros((batch, 2, HGRN_HEADS, HGRN_DK, HGRN_DK), F32)
    eye = jnp.eye(HGRN_HEADS, dtype=F32)
    st0 = jnp.einsum("bdhke,hg->bdhegk", s0, eye).reshape(batch, 2, BRANCH, BRANCH)
    o_hgrn, st_out = _hgrn(p, lw["hgrn_lb"], consts["hgrn_dd"], st0, batch, seq_len)
    st5 = st_out.reshape(batch, 2, HGRN_HEADS, HGRN_DK, HGRN_HEADS, HGRN_DK)
    states = jnp.stack([st5[:, :, h, :, h, :] for h in range(HGRN_HEADS)], axis=2).swapaxes(-1, -2)

    hu = p[:, P_HU:P_HU + 3 * HY_CH].reshape(batch, seq_len, 3 * HY_CH)
    hu_pad = jnp.pad(hu, ((0, 0), (1, 1), (0, 0)))
    v_d, x1, x2g = _hy_conv3(hu_pad[:, :-2].reshape(n, -1), hu.reshape(n, -1), hu_pad[:, 2:].reshape(n, -1),
                             lw["hy_conv_w"], lw["hy_conv_b"])
    filt = _hy_filter(hy["feats"], hy["window"], lw["hy_w1"], lw["hy_b1"], lw["hy_w2"], lw["hy_b2"], lw["hy_w3"],
                      lw["hy_freq"])
    taps = []
    for o in range(HY_ORDER):
        hf = filt[:, (2 * o) * HY_CH:(2 * o + 1) * HY_CH]
        hb = filt[:, (2 * o + 1) * HY_CH:(2 * o + 2) * HY_CH]
        taps.append(jnp.concatenate([hf, hb[::-1]], axis=0))
    taps = jnp.concatenate(taps, axis=1)
    k1n, n1o, n1t = hy["k1n"], hy["n1o"], hy["n1t"]
    ta = _fft_a(hy["fa_full"], taps.reshape(1, n1t, FFT_N2, HY_ORDER * HY_CH), BF16)
    spec = _fft_b(hy["mf"], None, ta.reshape(1, 2, k1n, FFT_N2, HY_ORDER * HY_CH), None, 0)[0]
    z4 = v_d.reshape(batch, n1o, FFT_N2, HY_CH)
    for o, xg in enumerate((x1, x2g)):
        a = _fft_a(hy["fa_half"], z4, BF16).reshape(batch, 2, k1n, FFT_N2, HY_CH)
        pk = _fft_b(hy["mf"], hy["mi"], a, spec, o).reshape(batch, 2 * k1n, FFT_N2, HY_CH)
        z4 = _fft_a_inv(hy["g"], pk, xg.reshape(batch, n1o, FFT_N2, HY_CH), z4, lw["hy_bias"][o:o + 1])
    out_d = z4.reshape(n, HY_CH)

    y = _outproj(x2, mod, out_a, out_b, o_hgrn, out_d, p, lw["hgrn_out_g"], consts["seg64"], lw["w_out"], seq_len)
    new = None
    if ctx is None:
        new = (ckv3, krp3[:, :, KR_OFF:KR_OFF + MLA_ROPE],
               kd.reshape(batch, seq_len, DIFF_HEADS, 2, DIFF_HD),
               dv.reshape(batch, seq_len, DIFF_HEADS, 2 * DIFF_HD), states)
    return y, new


def _rope_tables(seq_len):
    half = MLA_ROPE // 2
    inv = ROPE_BASE ** (-jnp.arange(0, half, 2, dtype=F32) / half)
    rows = seq_len // GRID_W
    row = jnp.repeat(jnp.arange(rows, dtype=F32), GRID_W)
    col = (jnp.arange(rows * GRID_W) % GRID_W).astype(F32)
    ar, ac = row[:, None] * inv, col[:, None] * inv
    cos32 = jnp.concatenate([jnp.cos(ar), jnp.cos(ar), jnp.cos(ac), jnp.cos(ac)], axis=-1)
    sin32 = jnp.concatenate([-jnp.sin(ar), jnp.sin(ar), -jnp.sin(ac), jnp.sin(ac)], axis=-1)
    pad = ((0, 0), (KR_OFF, LANE - KR_OFF - MLA_ROPE))
    return dict(cos_mla=jnp.pad(cos32, pad, constant_values=1.0), sin_mla=jnp.pad(sin32, pad),
                cos_diff=jnp.tile(cos32, (1, 2 * DIFF_HEADS)), sin_diff=jnp.tile(sin32, (1, 2 * DIFF_HEADS)))


def _hy_tables(seq_len):
    feats, window = _hy_static(seq_len)
    return dict(feats=feats, window=window, **_fft_tables(seq_len))


def _layer_weights(l, w_in_p, lb, W):
    def head_pad(w, width, per):
        k = w.shape[0]
        w = w.reshape(k, MLA_HEADS, per)[:, :, :width]
        return jnp.pad(w, ((0, 0), (0, 0), (0, LANE - width))).reshape(k, MLA_HEADS * LANE)

    w_ukv = W["mla_w_ukv"][l].reshape(MLA_KV_LORA, MLA_HEADS, MLA_NOPE + MLA_V)
    nope_g, rope_g = W["mla_nope_g"][l], W["mla_rope_g"][l]
    zeros32 = jnp.zeros((MLA_ROPE,), F32)
    zeros64 = jnp.zeros((MLA_NOPE,), F32)
    gq = jnp.tile(jnp.concatenate([nope_g[0], rope_g[0], zeros32]), MLA_HEADS).reshape(1, 512)
    gk = jnp.tile(jnp.concatenate([nope_g[1], zeros64]), MLA_HEADS).reshape(1, 512)
    gkr = jnp.concatenate([zeros64, rope_g[1], zeros32]).reshape(1, LANE)
    return dict(
        norm_g=W["norm_g"][l], w_in=w_in_p[l], w_out=W["w_out"][l].astype(BF16),
        qn_g=W["mla_q_norm_g"][l].reshape(1, -1),
        w_uq=head_pad(W["mla_w_uq"][l], MLA_NOPE + MLA_ROPE, MLA_NOPE + MLA_ROPE).astype(BF16),
        kvn_g=W["mla_kv_norm_g"][l].reshape(1, -1),
        w_uk=jnp.pad(w_ukv[:, :, :MLA_NOPE], ((0, 0), (0, 0), (0, LANE - MLA_NOPE))).reshape(MLA_KV_LORA, 512)
        .astype(BF16),
        w_uv=w_ukv[:, :, MLA_NOPE:].reshape(MLA_KV_LORA, BRANCH).astype(BF16),
        gq=gq, gk=gk, gkr=gkr,
        dgq=jnp.tile(W["diff_qk_g"][l, 0], 2 * DIFF_HEADS).reshape(1, BRANCH),
        dgk=jnp.tile(W["diff_qk_g"][l, 1], 2 * DIFF_HEADS).reshape(1, BRANCH),
        diff_lambda=W["diff_lambda"][l], subln_col=W["diff_subln_g"][l].reshape(2 * DIFF_HD, 1),
        lam_init=0.8 - 0.6 * math.exp(-0.3 * l),
        hgrn_lb=lb[:, l].reshape(2, 1, BRANCH),
        hgrn_out_g=jnp.tile(W["hgrn_out_g"][l], HGRN_HEADS).reshape(1, BRANCH),
        hy_conv_w=W["hy_conv_w"][l], hy_conv_b=W["hy_conv_b"][l].reshape(1, -1),
        hy_w1=jnp.pad(W["hy_w1"][l], ((0, LANE - HY_EMB), (0, 0))), hy_b1=W["hy_b1"][l].reshape(1, -1),
        hy_w2=W["hy_w2"][l], hy_b2=W["hy_b2"][l].reshape(1, -1), hy_w3=W["hy_w3"][l],
        hy_freq=W["hy_sin_freq"][l], hy_bias=W["hy_bias"][l],
    )


def kernel(x_prompt, x_sample, cache_mla_ckv, cache_mla_krope, cache_diff_k, cache_diff_v, state_hgrn, c, c_ctx,
           norm_g, w_mod, b_mod, w_in, w_out, mla_q_norm_g, mla_w_uq, mla_kv_norm_g, mla_w_ukv, mla_nope_g,
           mla_rope_g, diff_qk_g, diff_lambda, diff_subln_g, hgrn_lb_logits, hgrn_out_g, hy_conv_w, hy_conv_b,
           hy_w1, hy_b1, hy_w2, hy_b2, hy_w3, hy_sin_freq, hy_bias):
    W = dict(norm_g=norm_g, w_out=w_out, mla_q_norm_g=mla_q_norm_g, mla_w_uq=mla_w_uq,
             mla_kv_norm_g=mla_kv_norm_g, mla_w_ukv=mla_w_ukv, mla_nope_g=mla_nope_g, mla_rope_g=mla_rope_g,
             diff_qk_g=diff_qk_g, diff_lambda=diff_lambda, diff_subln_g=diff_subln_g, hgrn_out_g=hgrn_out_g,
             hy_conv_w=hy_conv_w, hy_conv_b=hy_conv_b, hy_w1=hy_w1, hy_b1=hy_b1, hy_w2=hy_w2, hy_b2=hy_b2,
             hy_w3=hy_w3, hy_sin_freq=hy_sin_freq, hy_bias=hy_bias)
    bp, lp, _ = x_prompt.shape
    bs, ls, _ = x_sample.shape

    w_in_z = jnp.pad(w_in, ((0, 0), (0, 0), (0, 1)))
    w_in_p = jnp.take(w_in_z, jnp.asarray(_in_col_perm()), axis=2).astype(BF16)
    cvecs = jnp.concatenate([c_ctx[None, :], c, jnp.zeros((8 - 1 - bs, D_MODEL), F32)], axis=0)
    mods = _mod_all(cvecs, w_mod, b_mod)
    lb = _hgrn_lb(hgrn_lb_logits)
    seg512, cnt512 = _mla_seg()
    consts = dict(seg512=seg512, cnt512=cnt512, seg32=_seg_const(BRANCH, DIFF_HD), seg64=_seg_const(BRANCH, HGRN_DK),
                  hgrn_dd=_hgrn_consts())
    lws = [_layer_weights(l, w_in_p, lb, W) for l in range(DEPTH)]

    hy_p = _hy_tables(lp)
    y = x_prompt.reshape(bp * lp, D_MODEL)
    per_layer = []
    for l in range(DEPTH):
        mod = mods[l, 0:1].reshape(1, 3, D_MODEL)
        y, new = _layer(y, mod, lws[l], consts, bp, lp, None, None, hy_p)
        per_layer.append(new)
    y_prompt = y.reshape(bp, lp, D_MODEL)
    news = [jnp.stack([s[i] for s in per_layer], axis=1) for i in range(5)]

    hy_s = _hy_tables(ls)
    rope_tabs = _rope_tables(ls)
    y = x_sample.reshape(bs * ls, D_MODEL)
    for l in range(DEPTH):
        mod = mods[l, 1:1 + bs].reshape(bs, 3, D_MODEL)
        ctx = (cache_mla_ckv[:, l], cache_mla_krope[:, l], cache_diff_k[:, l], cache_diff_v[:, l], state_hgrn[:, l])
        y, _ = _layer(y, mod, lws[l], consts, bs, ls, ctx, rope_tabs, hy_s)
    y_sample = y.reshape(bs, ls, D_MODEL)

    return (y_prompt, y_sample, news[0], news[1], news[2], news[3], news[4])
```

```python
import functools
import math

import numpy as np
import jax
import jax.numpy as jnp
from jax import lax
from jax.experimental import pallas as pl
from jax.experimental.pallas import tpu as pltpu

F32 = jnp.float32
BF16 = jnp.bfloat16

D_MODEL = 1024
DEPTH = 4
GRID_W = 64
ROPE_BASE = 10000.0
EPS = 1e-6
BRANCH = 256
MLA_HEADS = 4
MLA_NOPE = 64
MLA_ROPE = 32
MLA_V = 64
MLA_Q_LORA = 256
MLA_KV_LORA = 128
MLA_SCALE = (MLA_NOPE + MLA_ROPE) ** -0.5
DIFF_HEADS = 4
DIFF_HD = 32
DIFF_SCALE = DIFF_HD ** -0.5
HGRN_HEADS = 4
HGRN_DK = 64
HGRN_CHUNK = 128
HGRN_LEVELS = 7
HGRN_MM_LEVELS = 3
HY_CH = 256
HY_ORDER = 2
HY_EMB = 33
HY_BANDS = 16
HY_FH = 64
HY_DECAY_TARGET = 0.01
HY_FAST_DECAY = 0.3
HY_SLOW_DECAY = 1.5
IN_COLS = 4000

LANE = 128
LOG2E = math.log2(math.e)
VT_ROWS = 80
FFT_N2 = 128
FFT_BLOCK_BYTES = 2 * 1024 * 1024
MLA_SUB = 128
DIFF_SUB = 256
VMEM_LIMIT = 52 * 1024 * 1024

P_CQ, P_CKV, P_KR, P_DQ, P_DK, P_DV = 0, 256, 384, 512, 768, 1024
P_HQ, P_HZF, P_HZB, P_HI, P_HU, P_GATE = 1280, 1536, 1792, 2048, 2304, 3072
P_COLS = 4096
KR_OFF = 64


def _in_col_perm():
    src = np.full((P_COLS,), IN_COLS, np.int32)

    def put(dst, lo, n):
        src[dst:dst + n] = np.arange(lo, lo + n)

    put(P_CQ, 0, 256)
    put(P_CKV, 256, 128)
    put(P_KR + KR_OFF, 384, 32)
    put(P_GATE, 416, 256)
    put(P_DQ, 672, 256)
    put(P_DK, 928, 256)
    put(P_DV, 1184, 256)
    put(P_GATE + 256, 1440, 256)
    put(P_HQ, 1696, 256)
    put(P_HZF, 1952, 256)
    put(P_HZB, 2208, 256)
    put(P_HI, 2464, 256)
    put(P_GATE + 512, 2720, 256)
    put(P_HU, 2976, 768)
    put(P_GATE + 768, 3744, 256)
    return src


def _cparams(sem):
    return pltpu.CompilerParams(dimension_semantics=sem, vmem_limit_bytes=VMEM_LIMIT)


def _bdot(a, b):
    return jnp.dot(a.astype(BF16), b.astype(BF16), preferred_element_type=F32)


def _nt(a, b):
    return lax.dot_general(a.astype(BF16), b.astype(BF16), (((1,), (1,)), ((), ())), preferred_element_type=F32)


def _split2(a):
    hi = a.astype(BF16)
    lo = (a - hi.astype(F32)).astype(BF16)
    return hi, lo


def _dot3(a, b):
    ah, al = _split2(a)
    bh, bl = _split2(b)
    d = functools.partial(jnp.dot, preferred_element_type=F32)
    return d(ah, bh) + d(ah, bl) + d(al, bh)


def _segsum(v, seg):
    hi, lo = _split2(v)
    d = functools.partial(jnp.dot, preferred_element_type=F32)
    return d(hi, seg) + d(lo, seg)


def _rms(x, g):
    return x * lax.rsqrt(jnp.mean(x * x, axis=-1, keepdims=True) + EPS) * g


def _swap8(x):
    w = x.shape[-1]
    lane = lax.broadcasted_iota(jnp.int32, x.shape, x.ndim - 1)
    up = pltpu.roll(x, w - 8, x.ndim - 1)
    dn = pltpu.roll(x, 8, x.ndim - 1)
    return jnp.where((lane & 15) < 8, up, dn)


def _tile_lanes(x, n):
    return x if n == 1 else jnp.concatenate([x] * n, axis=-1)


def _mod_kernel(c_ref, w_ref, b_ref, o_ref):
    c = c_ref[...]
    o_ref[0] = _dot3(c * jax.nn.sigmoid(c), w_ref[0]) + b_ref[0]


def _mod_all(cvecs, w_mod, b_mod):
    nt = 3
    return pl.pallas_call(
        _mod_kernel,
        grid=(DEPTH, nt),
        in_specs=[pl.BlockSpec((8, D_MODEL), lambda l, j: (0, 0)),
                  pl.BlockSpec((1, D_MODEL, D_MODEL), lambda l, j: (l, 0, j)),
                  pl.BlockSpec((1, 1, D_MODEL), lambda l, j: (l, 0, j))],
        out_specs=pl.BlockSpec((1, 8, D_MODEL), lambda l, j: (l, 0, j)),
        out_shape=jax.ShapeDtypeStruct((DEPTH, 8, 3 * D_MODEL), F32),
        compiler_params=_cparams(("arbitrary", "arbitrary")),
        name="mod",
    )(cvecs, w_mod, b_mod.reshape(DEPTH, 1, 3 * D_MODEL))


def _lb_kernel(x_ref, o_ref):
    x = x_ref[...]
    rows = [x[l:l + 1, :] for l in range(DEPTH)]
    m = functools.reduce(jnp.maximum, rows)
    e = [jnp.exp(r - m) for r in rows]
    tot = functools.reduce(lambda a, b: a + b, e)
    acc = jnp.zeros_like(tot)
    o_ref[0:1, :] = acc
    for l in range(1, DEPTH):
        acc = acc + e[l] / tot
        o_ref[l:l + 1, :] = acc


def _hgrn_lb(logits):
    flat = logits.transpose(1, 0, 2).reshape(DEPTH, 2 * BRANCH)
    lb = pl.pallas_call(
        _lb_kernel,
        out_shape=jax.ShapeDtypeStruct(flat.shape, F32),
        name="hgrn_lb",
    )(flat)
    return lb.reshape(DEPTH, 2, BRANCH).transpose(1, 0, 2)


def _inproj_kernel(x_ref, mod_ref, g_ref, w_ref, p_ref):
    h = _rms(x_ref[...], g_ref[...]) * (1.0 + mod_ref[0, 1:2, :]) + mod_ref[0, 0:1, :]
    p_ref[...] = jnp.dot(h.astype(BF16), w_ref[...], preferred_element_type=F32)


def _inproj(x2, mod, norm_g, w_in_p, seq_len):
    n = x2.shape[0]
    tm = 256
    per_batch = mod.shape[0] > 1
    tiles_per_seq = seq_len // tm
    mod_idx = (lambda i: (i // tiles_per_seq, 0, 0)) if per_batch else (lambda i: (0, 0, 0))
    return pl.pallas_call(
        _inproj_kernel,
        grid=(n // tm,),
        in_specs=[pl.BlockSpec((tm, D_MODEL), lambda i: (i, 0)),
                  pl.BlockSpec((1, 3, D_MODEL), mod_idx),
                  pl.BlockSpec((1, D_MODEL), lambda i: (0, 0)),
                  pl.BlockSpec((D_MODEL, P_COLS), lambda i: (0, 0))],
        out_specs=pl.BlockSpec((tm, P_COLS), lambda i: (i, 0)),
        out_shape=jax.ShapeDtypeStruct((n, P_COLS), F32),
        compiler_params=_cparams(("parallel",)),
        name="inproj",
    )(x2, mod, norm_g.reshape(1, D_MODEL), w_in_p)


def _mla_seg():
    sid = np.zeros((512,), np.int32)
    cnt = np.ones((512,), np.float32)
    for h in range(MLA_HEADS):
        b = 128 * h
        sid[b:b + 64] = 3 * h
        sid[b + 64:b + 96] = 3 * h + 1
        sid[b + 96:b + 128] = 3 * h + 2
        cnt[b:b + 64] = 1.0 / 64
        cnt[b + 64:b + 128] = 1.0 / 32
    seg = (sid[:, None] == sid[None, :]).astype(np.float32)
    return jnp.asarray(seg, BF16), jnp.asarray(cnt.reshape(1, 512))


def _mla_q_kernel(rope, cq_ref, ckv_ref, kr_ref, qng_ref, wuq_ref, kvg_ref, gq_ref, gkr_ref, seg_ref, cnt_ref,
                  *rest):
    if rope:
        cos_ref, sin_ref, q_ref, ckvn_ref, krp_ref = rest
    else:
        q_ref, ckvn_ref, krp_ref = rest
    cqn = _rms(cq_ref[...], qng_ref[...])
    q = _bdot(cqn, wuq_ref[...])
    ss = _segsum(q * q, seg_ref[...]) * cnt_ref[...]
    qn = q * lax.rsqrt(ss + EPS) * gq_ref[...]
    ckvn_ref[...] = _rms(ckv_ref[...], kvg_ref[...])
    kr = kr_ref[...]
    krn = kr * lax.rsqrt(jnp.sum(kr * kr, axis=-1, keepdims=True) * (1.0 / MLA_ROPE) + EPS) * gkr_ref[...]
    if rope:
        cos, sin = cos_ref[...], sin_ref[...]
        qn = qn * _tile_lanes(cos, MLA_HEADS) + _swap8(qn) * _tile_lanes(sin, MLA_HEADS)
        krn = krn * cos + _swap8(krn) * sin
    q_ref[...] = (qn * (MLA_SCALE * LOG2E)).astype(BF16)
    krp_ref[...] = krn


def _mla_q(p, lw, consts, seq_len, rope_tabs):
    n = p.shape[0]
    tm = 256
    rope = rope_tabs is not None
    full = lambda shape: pl.BlockSpec(shape, lambda i: (0,) * len(shape))
    in_specs = [pl.BlockSpec((tm, 256), lambda i: (i, P_CQ // 256)),
                pl.BlockSpec((tm, 128), lambda i: (i, P_CKV // 128)),
                pl.BlockSpec((tm, 128), lambda i: (i, P_KR // 128)),
                full((1, 256)), full((256, 512)), full((1, 128)), full((1, 512)), full((1, 128)),
                full((512, 512)), full((1, 512))]
    args = [p, p, p, lw["qn_g"], lw["w_uq"], lw["kvn_g"], lw["gq"], lw["gkr"], consts["seg512"], consts["cnt512"]]
    if rope:
        tps = seq_len // tm
        in_specs += [pl.BlockSpec((tm, 128), lambda i: (i % tps, 0))] * 2
        args += [rope_tabs["cos_mla"], rope_tabs["sin_mla"]]
    return pl.pallas_call(
        functools.partial(_mla_q_kernel, rope),
        grid=(n // tm,),
        in_specs=in_specs,
        out_specs=[pl.BlockSpec((tm, 512), lambda i: (i, 0)),
                   pl.BlockSpec((tm, 128), lambda i: (i, 0)),
                   pl.BlockSpec((tm, 128), lambda i: (i, 0))],
        out_shape=[jax.ShapeDtypeStruct((n, 512), BF16),
                   jax.ShapeDtypeStruct((n, 128), F32),
                   jax.ShapeDtypeStruct((n, 128), F32)],
        compiler_params=_cparams(("parallel",)),
        name="mla_q",
    )(*args)


def _mla_kv_kernel(ckvn_ref, krp_ref, wuk_ref, wuv_ref, gk_ref, seg_ref, cnt_ref, k_ref, v_ref):
    c = ckvn_ref[...].astype(BF16)
    kn = jnp.dot(c, wuk_ref[...], preferred_element_type=F32)
    ss = _segsum(kn * kn, seg_ref[...]) * cnt_ref[...]
    k = kn * lax.rsqrt(ss + EPS) * gk_ref[...] + _tile_lanes(krp_ref[...], MLA_HEADS)
    k_ref[...] = k.astype(BF16)
    v_ref[...] = jnp.dot(c, wuv_ref[...], preferred_element_type=F32).astype(BF16)


def _mla_kv(ckvn, krp, lw, consts):
    n = ckvn.shape[0]
    tm = 512
    full = lambda shape: pl.BlockSpec(shape, lambda i: (0,) * len(shape))
    return pl.pallas_call(
        _mla_kv_kernel,
        grid=(n // tm,),
        in_specs=[pl.BlockSpec((tm, 128), lambda i: (i, 0)), pl.BlockSpec((tm, 128), lambda i: (i, 0)),
                  full((128, 512)), full((128, 256)), full((1, 512)), full((512, 512)), full((1, 512))],
        out_specs=[pl.BlockSpec((tm, 512), lambda i: (i, 0)), pl.BlockSpec((tm, 256), lambda i: (i, 0))],
        out_shape=[jax.ShapeDtypeStruct((n, 512), BF16), jax.ShapeDtypeStruct((n, 256), BF16)],
        compiler_params=_cparams(("parallel",)),
        name="mla_kv",
    )(ckvn, krp, lw["w_uk"], lw["w_uv"], lw["gk"], consts["seg512"], consts["cnt512"])


def _softmax_pv(qs, k_ref, vt_ref, key_chunk, sub_rows):
    lk = k_ref.shape[1]
    nch = lk // key_chunk
    nq = len(qs)
    sub = min(sub_rows, key_chunk)
    nsub = key_chunk // sub

    def scores(c, u):
        lo = c * key_chunk + u * sub
        ks = k_ref[0, lo:lo + sub, :]
        return [_nt(ks, q) for q in qs]

    m = [None] * nq
    acc = [None] * nq
    s_next = [scores(0, u) for u in range(nsub)]
    for c in range(nch):
        s_cur = s_next
        s_next = []
        m_new = []
        for j in range(nq):
            mc = functools.reduce(jnp.maximum, [s_cur[u][j] for u in range(nsub)])
            mc = jnp.max(mc, axis=0, keepdims=True)
            m_new.append(mc if c == 0 else jnp.maximum(m[j], mc))
        p = [[] for _ in range(nq)]
        for u in range(nsub):
            if c + 1 < nch:
                s_next.append(scores(c + 1, u))
            for j in range(nq):
                p[j].append(jnp.exp2(s_cur[u][j] - m_new[j]).astype(BF16))
        vs = vt_ref[0, 0, :, c * key_chunk:(c + 1) * key_chunk]
        for j in range(nq):
            pv = jnp.dot(vs, jnp.concatenate(p[j], axis=0), preferred_element_type=F32)
            acc[j] = pv if c == 0 else acc[j] * jnp.exp2(m[j] - m_new[j]) + pv
            m[j] = m_new[j]
    return acc


def _mla_attn_kernel(key_chunk, q_ref, k_ref, vt_ref, o_ref):
    (acc,) = _softmax_pv([q_ref[...]], k_ref, vt_ref, key_chunk, MLA_SUB)
    o_ref[0] = acc[0:MLA_V] / acc[MLA_V:MLA_V + 1]


def _key_chunk(lk):
    return 512 if lk % 512 == 0 else lk


def _vt_with_ones(v3):
    b, lk, _ = v3.shape
    vt = v3.reshape(b, lk, 4, 64).transpose(0, 2, 3, 1)
    extra = jnp.zeros((b, 4, VT_ROWS - 64, lk), BF16).at[:, :, 0, :].set(1.0)
    return jnp.concatenate([vt, extra], axis=2)


def _mla_attn(q, k, vt, batch, seq_len):
    lk = k.shape[1]
    tq = 256
    nq = seq_len // tq
    return pl.pallas_call(
        functools.partial(_mla_attn_kernel, _key_chunk(lk)),
        grid=(batch, MLA_HEADS, nq),
        in_specs=[pl.BlockSpec((tq, 128), lambda b, h, i: (b * nq + i, h)),
                  pl.BlockSpec((1, lk, 128), lambda b, h, i: (b, 0, h)),
                  pl.BlockSpec((1, 1, VT_ROWS, lk), lambda b, h, i: (b, h, 0, 0))],
        out_specs=pl.BlockSpec((1, MLA_V, tq), lambda b, h, i: (b, h, i)),
        out_shape=jax.ShapeDtypeStruct((batch, BRANCH, seq_len), F32),
        compiler_params=_cparams(("parallel", "parallel", "arbitrary")),
        name="mla_attn",
    )(q, k, vt)


def _seg_const(width, seg):
    sid = np.arange(width) // seg
    return jnp.asarray((sid[:, None] == sid[None, :]).astype(np.float32), BF16)


def _diff_prep_kernel(rope, dq_ref, dk_ref, gq_ref, gk_ref, seg_ref, *rest):
    if rope:
        cos_ref, sin_ref, q_ref, k_ref, kf_ref = rest
    else:
        q_ref, k_ref, kf_ref = rest
    seg = seg_ref[...]

    def norm(x, g):
        ss = _segsum(x * x, seg) * (1.0 / DIFF_HD)
        return x * lax.rsqrt(ss + EPS) * g

    q = norm(dq_ref[...], gq_ref[...])
    k = norm(dk_ref[...], gk_ref[...])
    kf_ref[...] = k
    if rope:
        cos, sin = cos_ref[...], sin_ref[...]
        q = q * cos + _swap8(q) * sin
        k = k * cos + _swap8(k) * sin
    q_ref[...] = (q * (DIFF_SCALE * LOG2E)).astype(BF16)
    k_ref[...] = k.astype(BF16)


def _diff_prep(p, lw, consts, seq_len, rope_tabs):
    n = p.shape[0]
    tm = 256
    rope = rope_tabs is not None
    full = lambda shape: pl.BlockSpec(shape, lambda i: (0,) * len(shape))
    in_specs = [pl.BlockSpec((tm, 256), lambda i: (i, P_DQ // 256)),
                pl.BlockSpec((tm, 256), lambda i: (i, P_DK // 256)),
                full((1, 256)), full((1, 256)), full((256, 256))]
    args = [p, p, lw["dgq"], lw["dgk"], consts["seg32"]]
    if rope:
        tps = seq_len // tm
        in_specs += [pl.BlockSpec((tm, 256), lambda i: (i % tps, 0))] * 2
        args += [rope_tabs["cos_diff"], rope_tabs["sin_diff"]]
    blk = pl.BlockSpec((tm, 256), lambda i: (i, 0))
    return pl.pallas_call(
        functools.partial(_diff_prep_kernel, rope),
        grid=(n // tm,),
        in_specs=in_specs,
        out_specs=[blk, blk, blk],
        out_shape=[jax.ShapeDtypeStruct((n, 256), BF16), jax.ShapeDtypeStruct((n, 256), BF16),
                   jax.ShapeDtypeStruct((n, 256), F32)],
        compiler_params=_cparams(("parallel",)),
        name="diff_prep",
    )(*args)


def _diff_attn_kernel(lam_init, key_chunk, q_ref, k_ref, vt_ref, lp_ref, g_ref, o_ref):
    h = pl.program_id(1)
    q = q_ref[...]
    lane = lax.broadcasted_iota(jnp.int32, q.shape, 1)
    base = (h % 2) * 64
    zero = jnp.zeros_like(q)

    def map_query(j):
        lo = base + 32 * j
        return jnp.where((lane >= lo) & (lane < lo + 32), q, zero)

    acc0, acc1 = _softmax_pv([map_query(0), map_query(1)], k_ref, vt_ref, key_chunk, DIFF_SUB)
    lp = lp_ref[...]
    lam = (jnp.exp(jnp.sum(lp[0:1] * lp[1:2], axis=1, keepdims=True))
           - jnp.exp(jnp.sum(lp[2:3] * lp[3:4], axis=1, keepdims=True)) + lam_init)
    o = acc0[0:64] / acc0[64:65] - lam * (acc1[0:64] / acc1[64:65])
    ms = jnp.mean(o * o, axis=0, keepdims=True)
    o_ref[0] = o * lax.rsqrt(ms + EPS) * g_ref[...] * (1.0 - lam_init)


def _diff_attn(q, k, vt, lp, g_col, lam_init, batch, seq_len):
    lk = k.shape[1]
    tq = 256
    nq = seq_len // tq
    return pl.pallas_call(
        functools.partial(_diff_attn_kernel, lam_init, _key_chunk(lk)),
        grid=(batch, DIFF_HEADS, nq),
        in_specs=[pl.BlockSpec((tq, 128), lambda b, h, i: (b * nq + i, h // 2)),
                  pl.BlockSpec((1, lk, 128), lambda b, h, i: (b, 0, h // 2)),
                  pl.BlockSpec((1, 1, VT_ROWS, lk), lambda b, h, i: (b, h, 0, 0)),
                  pl.BlockSpec((4, DIFF_HD), lambda b, h, i: (0, 0)),
                  pl.BlockSpec((64, 1), lambda b, h, i: (0, 0))],
        out_specs=pl.BlockSpec((1, 64, tq), lambda b, h, i: (b, h, i)),
        out_shape=jax.ShapeDtypeStruct((batch, BRANCH, seq_len), F32),
        compiler_params=_cparams(("parallel", "parallel", "arbitrary")),
        name="diff_attn",
    )(q, k, vt, lp, g_col)


def _hgrn_consts():
    c = HGRN_CHUNK
    t = np.arange(c)
    low = (t[None, :] <= t[:, None]).astype(np.float32)
    blocks = []
    for j in range(HGRN_MM_LEVELS):
        m = 1 << j
        rho = (t // (2 * m)) * (2 * m) + m - 1
        sign = np.where((t // m) % 2 == 1, 1.0, -1.0)[:, None]
        blocks.append(sign * (low - (t[None, :] <= rho[:, None]).astype(np.float32)))
    blocks.append(low)
    fwd = np.concatenate(blocks, axis=0)
    bwd = np.concatenate([b[::-1, ::-1] for b in blocks], axis=0)
    return jnp.asarray(np.stack([fwd, bwd]), BF16)


def _hgrn_kernel(nc, qf_ref, zf_ref, vf_ref, qb_ref, zb_ref, vb_ref, lb_ref, dd_ref, s0_ref,
                 of_ref, ob_ref, sout_ref, st_ref):
    c = HGRN_CHUNK
    ci = pl.program_id(1)
    dirs = (0, 1)

    @pl.when(ci == 0)
    def _():
        st_ref[...] = s0_ref[0]

    lane = lax.broadcasted_iota(jnp.int32, (1, BRANCH), 1)
    head_masks = [(lane >= HGRN_DK * h) & (lane < HGRN_DK * (h + 1)) for h in range(HGRN_HEADS)]
    t_idx = lax.broadcasted_iota(jnp.int32, (c, HGRN_HEADS * c), 0)
    s_idx = lax.broadcasted_iota(jnp.int32, (c, HGRN_HEADS * c), 1) & (c - 1)
    pair_xor = t_idx ^ s_idx
    row = lax.broadcasted_iota(jnp.int32, (c, 1), 0)
    row_dir = (row, c - 1 - row)

    def stack_heads(x):
        xb = x.astype(BF16)
        zero = jnp.zeros_like(xb)
        return jnp.concatenate([jnp.where(hm, xb, zero) for hm in head_masks], axis=0)

    q = (qf_ref[...], qb_ref[...])
    v = (vf_ref[...], vb_ref[...])
    z = (zf_ref[...], zb_ref[...])
    lb = (lb_ref[0], lb_ref[1])
    g = [jnp.log(lb[d] + (1.0 - lb[d]) * jax.nn.sigmoid(z[d])) for d in dirs]
    kk = [(1.0 - lb[d]) * jax.nn.sigmoid(-z[d]) for d in dirs]
    sums = []
    for d in dirs:
        gh, gl = _split2(g[d])
        sums.append(jnp.dot(dd_ref[d], gh, preferred_element_type=F32)
                    + jnp.dot(dd_ref[d], gl, preferred_element_type=F32))
    b = [sums[d][HGRN_MM_LEVELS * c:] for d in dirs]
    b_tot = (b[0][c - 1:c], b[1][0:1])

    def neg_abs_decay(d, j, right):
        if j < HGRN_MM_LEVELS:
            return sums[d][j * c:(j + 1) * c]
        m = 1 << j
        off = m - 1 if d == 0 else m
        ref = jnp.concatenate([jnp.broadcast_to(b[d][g0 + off:g0 + off + 1], (2 * m, BRANCH))
                               for g0 in range(0, c, 2 * m)], axis=0)
        return jnp.where(right, b[d] - ref, ref - b[d])

    a = [None, None]
    for j in reversed(range(HGRN_LEVELS)):
        same_group = pair_xor < (2 << j)
        for d in dirs:
            right = ((row_dir[d] >> j) & 1) == 1
            e = jnp.exp(neg_abs_decay(d, j, right))
            qt = jnp.where(right, q[d] * e, 0.0)
            kt = jnp.where(right, 0.0, kk[d] * e)
            lvl = _nt(qt, stack_heads(kt))
            a[d] = lvl if a[d] is None else jnp.where(same_group, lvl, a[d])
    diagonal = pair_xor == 0
    for d in dirs:
        a[d] = jnp.where(diagonal, _nt(q[d], stack_heads(kk[d])), a[d])

    outs = (of_ref, ob_ref)
    for d in dirs:
        o = jnp.dot(a[d].astype(BF16), stack_heads(v[d]), preferred_element_type=F32)
        outs[d][...] = o + _nt(q[d] * jnp.exp(b[d]), st_ref[d])

    r2 = lax.broadcasted_iota(jnp.int32, (BRANCH, BRANCH), 0) // HGRN_DK
    c2 = lax.broadcasted_iota(jnp.int32, (BRANCH, BRANCH), 1) // HGRN_DK
    for d in dirs:
        kd = kk[d] * jnp.exp(b_tot[d] - b[d])
        upd = lax.dot_general(v[d].astype(BF16), kd.astype(BF16), (((0,), (0,)), ((), ())),
                              preferred_element_type=F32)
        st_new = st_ref[d] * jnp.exp(b_tot[d]) + jnp.where(r2 == c2, upd, 0.0)
        st_ref[d] = st_new

        @pl.when(ci == nc - 1)
        def _():
            sout_ref[0, d] = st_new


def _hgrn(p, lb_l, dd, st0, batch, seq_len):
    n = p.shape[0]
    c = HGRN_CHUNK
    nc = seq_len // c
    fwd = lambda col: pl.BlockSpec((c, 256), lambda b, i: (b * nc + i, col))
    bwd = lambda col: pl.BlockSpec((c, 256), lambda b, i: (b * nc + nc - 1 - i, col))
    whole = lambda shape: pl.BlockSpec(shape, lambda b, i: (0,) * len(shape))
    return pl.pallas_call(
        functools.partial(_hgrn_kernel, nc),
        grid=(batch, nc),
        in_specs=[fwd(P_HQ // 256), fwd(P_HZF // 256), fwd(P_HI // 256),
                  bwd(P_HQ // 256), bwd(P_HZB // 256), bwd(P_HI // 256),
                  whole((2, 1, 256)), whole((2, (HGRN_MM_LEVELS + 1) * c, c)),
                  pl.BlockSpec((1, 2, 256, 256), lambda b, i: (b, 0, 0, 0))],
        out_specs=[pl.BlockSpec((c, 256), lambda b, i: (b * nc + i, 0)),
                   pl.BlockSpec((c, 256), lambda b, i: (b * nc + nc - 1 - i, 0)),
                   pl.BlockSpec((1, 2, 256, 256), lambda b, i: (b, 0, 0, 0))],
        out_shape=[jax.ShapeDtypeStruct((n, 256), F32), jax.ShapeDtypeStruct((n, 256), F32),
                   jax.ShapeDtypeStruct((batch, 2, 256, 256), F32)],
        scratch_shapes=[pltpu.VMEM((2, 256, 256), F32)],
        compiler_params=_cparams(("parallel", "arbitrary")),
        name="hgrn",
    )(p, p, p, p, p, p, lb_l, dd, st0)


def _hy_conv3_kernel(prev_ref, cur_ref, nxt_ref, w_ref, b_ref, v_ref, x1_ref, x2_ref):
    w = w_ref[...]
    u = prev_ref[...] * w[0:1] + cur_ref[...] * w[1:2] + nxt_ref[...] * w[2:3] + b_ref[...]
    v_ref[...] = u[:, 0:256]
    x1_ref[...] = u[:, 256:512]
    x2_ref[...] = u[:, 512:768]


def _hy_conv3(prev, cur, nxt, w, b):
    n = cur.shape[0]
    tm = 512
    blk = pl.BlockSpec((tm, 768), lambda i: (i, 0))
    oblk = pl.BlockSpec((tm, 256), lambda i: (i, 0))
    return pl.pallas_call(
        _hy_conv3_kernel,
        grid=(n // tm,),
        in_specs=[blk, blk, blk, pl.BlockSpec((3, 768), lambda i: (0, 0)), pl.BlockSpec((1, 768), lambda i: (0, 0))],
        out_specs=[oblk, oblk, oblk],
        out_shape=[jax.ShapeDtypeStruct((n, 256), F32)] * 3,
        compiler_params=_cparams(("parallel",)),
        name="hy_conv3",
    )(prev, cur, nxt, w, b)


def _hy_filter_kernel(feat_ref, w1_ref, b1_ref, w2_ref, b2_ref, w3_ref, fr_ref, win_ref, o_ref):
    fr = fr_ref[...]
    h = jnp.sin(fr[0:1] * (_dot3(feat_ref[...], w1_ref[...]) + b1_ref[...]))
    h = jnp.sin(fr[1:2] * (_dot3(h, w2_ref[...]) + b2_ref[...]))
    o_ref[...] = _dot3(h, w3_ref[...]) * _tile_lanes(win_ref[...], 2 * HY_ORDER)


def _hy_filter(feats, window, w1p, b1, w2, b2, w3, freq):
    ln = feats.shape[0]
    tm = min(512, ln)
    full = lambda shape: pl.BlockSpec(shape, lambda i: (0,) * len(shape))
    return pl.pallas_call(
        _hy_filter_kernel,
        grid=(ln // tm,),
        in_specs=[pl.BlockSpec((tm, LANE), lambda i: (i, 0)),
                  full((LANE, HY_FH)), full((1, HY_FH)), full((HY_FH, HY_FH)), full((1, HY_FH)),
                  full((HY_FH, 2 * HY_ORDER * HY_CH)), full((2, HY_FH)),
                  pl.BlockSpec((tm, HY_CH), lambda i: (i, 0))],
        out_specs=pl.BlockSpec((tm, 2 * HY_ORDER * HY_CH), lambda i: (i, 0)),
        out_shape=jax.ShapeDtypeStruct((ln, 2 * HY_ORDER * HY_CH), F32),
        compiler_params=_cparams(("parallel",)),
        name="hy_filter",
    )(feats, w1p, b1, w2, b2, w3, freq, window)


def _fft_blocking(nb, n1, n2, ch):
    per_batch = n1 * n2 * ch * 4
    if per_batch <= FFT_BLOCK_BYTES:
        bb = max(1, min(nb, FFT_BLOCK_BYTES // per_batch))
        while nb % bb:
            bb -= 1
        return bb, n2
    rt = n2
    while n1 * rt * ch * 4 > FFT_BLOCK_BYTES and rt > 8:
        rt //= 2
    return 1, rt


def _fft_a_kernel(fh_ref, fl_ref, x_ref, o_ref):
    fh, fl = fh_ref[...], fl_ref[...]
    e = functools.partial(jnp.einsum, "kn,nrc->krc", preferred_element_type=F32)
    for b in range(x_ref.shape[0]):
        xh, xl = _split2(x_ref[b])
        o_ref[b] = (e(fh, xh) + e(fl, xh) + e(fh, xl)).astype(o_ref.dtype)


def _fft_a(fa, x4, out_dtype):
    nb, n1, n2, ch = x4.shape
    r = fa.shape[0]
    bb, rt = _fft_blocking(nb, n1, n2, ch)
    fh, fl = _split2(fa)
    return pl.pallas_call(
        _fft_a_kernel,
        grid=(nb // bb, n2 // rt),
        in_specs=[pl.BlockSpec((r, n1), lambda b, i: (0, 0)), pl.BlockSpec((r, n1), lambda b, i: (0, 0)),
                  pl.BlockSpec((bb, n1, rt, ch), lambda b, i: (b, 0, i, 0))],
        out_specs=pl.BlockSpec((bb, r, rt, ch), lambda b, i: (b, 0, i, 0)),
        out_shape=jax.ShapeDtypeStruct((nb, r, n2, ch), out_dtype),
        compiler_params=_cparams(("parallel", "parallel")),
        name="fft_a",
    )(fh, fl, x4)


def _fft_b_kernel(with_inverse, mf_ref, *rest):
    if with_inverse:
        mi_ref, a_ref, h_ref, o_ref = rest
    else:
        a_ref, o_ref = rest
    half = FFT_N2
    for b in range(a_ref.shape[0]):
        a = jnp.concatenate([a_ref[b, 0, 0], a_ref[b, 1, 0]], axis=0)
        x = jnp.dot(mf_ref[0], a.astype(BF16), preferred_element_type=F32)
        if with_inverse:
            xr, xi = x[:half], x[half:]
            hr, hi = h_ref[0, 0], h_ref[1, 0]
            y = jnp.concatenate([xr * hr - xi * hi, xr * hi + xi * hr], axis=0)
            x = jnp.dot(mi_ref[0], y.astype(BF16), preferred_element_type=F32)
        o_ref[b, 0, 0] = x[:half]
        o_ref[b, 1, 0] = x[half:]


def _fft_b(mf, mi, a5, spec, order):
    nb, _, k1n, n2, ch = a5.shape
    mat = pl.BlockSpec((1, 2 * n2, 2 * n2), lambda k: (k, 0, 0))
    blk = pl.BlockSpec((nb, 2, 1, n2, ch), lambda k: (0, 0, k, 0, 0))
    if spec is None:
        in_specs, args = [mat, blk], [mf, a5]
    else:
        in_specs = [mat, mat, blk, pl.BlockSpec((2, 1, n2, ch), lambda k: (0, k, 0, order))]
        args = [mf, mi, a5, spec]
    return pl.pallas_call(
        functools.partial(_fft_b_kernel, spec is not None),
        grid=(k1n,),
        in_specs=in_specs,
        out_specs=blk,
        out_shape=jax.ShapeDtypeStruct(a5.shape, F32),
        compiler_params=_cparams(("parallel",)),
        name="fft_b",
    )(*args)


def _fft_a_inv_kernel(gh_ref, gl_ref, p_ref, x_ref, z_ref, bias_ref, o_ref):
    gh, gl = gh_ref[...], gl_ref[...]
    e = functools.partial(jnp.einsum, "nk,krc->nrc", preferred_element_type=F32)
    for b in range(p_ref.shape[0]):
        ph, pl_ = _split2(p_ref[b])
        conv = e(gh, ph) + e(gl, ph) + e(gh, pl_)
        o_ref[b] = x_ref[b] * (conv + z_ref[b] * bias_ref[...])


def _fft_a_inv(g, p4, xg4, z4, bias):
    nb, n1, n2, ch = z4.shape
    r = g.shape[1]
    bb, rt = _fft_blocking(nb, n1, n2, ch)
    gh, gl = _split2(g)
    blk = pl.BlockSpec((bb, n1, rt, ch), lambda b, i: (b, 0, i, 0))
    return pl.pallas_call(
        _fft_a_inv_kernel,
        grid=(nb // bb, n2 // rt),
        in_specs=[pl.BlockSpec((n1, r), lambda b, i: (0, 0)), pl.BlockSpec((n1, r), lambda b, i: (0, 0)),
                  pl.BlockSpec((bb, r, rt, ch), lambda b, i: (b, 0, i, 0)), blk, blk,
                  pl.BlockSpec((1, 1, ch), lambda b, i: (0, 0, 0))],
        out_specs=blk,
        out_shape=jax.ShapeDtypeStruct(z4.shape, F32),
        compiler_params=_cparams(("parallel", "parallel")),
        name="fft_a_inv",
    )(gh, gl, p4, xg4, z4, bias.reshape(1, 1, ch))


def _fft_tables(ln):
    n = 2 * ln
    n1t = n // FFT_N2
    k1n = n1t // 2 + 1
    kk = np.arange(k1n)

    def stage_a(n1_in):
        ang = 2.0 * np.pi * ((kk[:, None] * np.arange(n1_in)[None, :]) % n1t) / n1t
        return jnp.asarray(np.concatenate([np.cos(ang), -np.sin(ang)], axis=0), F32)

    n1o = n1t // 2
    ang = 2.0 * np.pi * ((np.arange(n1o)[:, None] * kk[None, :]) % n1t) / n1t
    edge = (kk == 0) | (kk == n1t // 2)
    ck = np.where(edge, 1.0, 2.0) / n
    g = jnp.asarray(np.concatenate([ck * np.cos(ang), -ck * np.where(edge, 0.0, np.sin(ang))], axis=1), F32)

    k1 = jnp.arange(k1n, dtype=jnp.int32)[:, None, None]
    k2 = jnp.arange(FFT_N2, dtype=jnp.int32)[None, :, None]
    n2 = jnp.arange(FFT_N2, dtype=jnp.int32)[None, None, :]
    th = (2.0 * math.pi / n) * ((n2 * (k1 + n1t * k2)) % n).astype(F32)
    c, s = jnp.cos(th), jnp.sin(th)
    mf = jnp.concatenate([jnp.concatenate([c, s], axis=2), jnp.concatenate([-s, c], axis=2)], axis=1)
    return dict(fa_half=stage_a(n1o), fa_full=stage_a(n1t), g=g, mf=mf.astype(BF16),
                mi=mf.transpose(0, 2, 1).astype(BF16), k1n=k1n, n1o=n1o, n1t=n1t)


def _hy_static(ln):
    t = jnp.linspace(0.0, 1.0, ln, dtype=F32)[:, None]
    w = 2.0 * math.pi * jnp.arange(ln, dtype=F32) / ln
    f = jnp.linspace(1e-4, HY_BANDS - 1, HY_BANDS, dtype=F32)
    ang = w[:, None] * f[None, :]
    feats = jnp.concatenate([t, jnp.cos(ang), -jnp.sin(ang)], axis=-1)
    feats = jnp.pad(feats, ((0, 0), (0, LANE - HY_EMB)))
    min_decay = math.log(HY_DECAY_TARGET) / HY_SLOW_DECAY
    max_decay = math.log(HY_DECAY_TARGET) / HY_FAST_DECAY
    deltas = jnp.linspace(min_decay, max_decay, HY_CH, dtype=F32)
    window = jnp.exp(-t * jnp.abs(deltas))
    return feats, window


def _outproj_kernel(x_ref, mod_ref, oa_ref, ob_ref, of_ref, obk_ref, od_ref, gate_ref, hg_ref, seg_ref, w_ref,
                    y_ref):
    gt = gate_ref[...]
    sg = gt * jax.nn.sigmoid(gt)
    oc = of_ref[...] + obk_ref[...]
    ss = _segsum(oc * oc, seg_ref[...]) * (1.0 / HGRN_DK)
    oc = oc * lax.rsqrt(ss + EPS) * hg_ref[...]
    acc = _bdot(oa_ref[0].T * sg[:, 0:256], w_ref[0:256, :])
    acc += _bdot(ob_ref[0].T * sg[:, 256:512], w_ref[256:512, :])
    acc += _bdot(oc * sg[:, 512:768], w_ref[512:768, :])
    acc += _bdot(od_ref[...] * sg[:, 768:1024], w_ref[768:1024, :])
    y_ref[...] = x_ref[...] + mod_ref[0, 2:3, :] * acc


def _outproj(x2, mod, ot_a, ot_b, o_f, o_b, out_d, p, hg, seg64, w_out, seq_len):
    n = x2.shape[0]
    tm = 256
    per_batch = mod.shape[0] > 1
    tps = seq_len // tm
    mod_idx = (lambda i: (i // tps, 0, 0)) if per_batch else (lambda i: (0, 0, 0))
    b256 = pl.BlockSpec((tm, 256), lambda i: (i, 0))
    bt = pl.BlockSpec((1, 256, tm), lambda i: (i // tps, 0, i % tps))
    return pl.pallas_call(
        _outproj_kernel,
        grid=(n // tm,),
        in_specs=[pl.BlockSpec((tm, D_MODEL), lambda i: (i, 0)),
                  pl.BlockSpec((1, 3, D_MODEL), mod_idx),
                  bt, bt, b256, b256, b256,
                  pl.BlockSpec((tm, 1024), lambda i: (i, P_GATE // 1024)),
                  pl.BlockSpec((1, 256), lambda i: (0, 0)),
                  pl.BlockSpec((256, 256), lambda i: (0, 0)),
                  pl.BlockSpec((D_MODEL, D_MODEL), lambda i: (0, 0))],
        out_specs=pl.BlockSpec((tm, D_MODEL), lambda i: (i, 0)),
        out_shape=jax.ShapeDtypeStruct((n, D_MODEL), F32),
        compiler_params=_cparams(("parallel",)),
        name="outproj",
    )(x2, mod, ot_a, ot_b, o_f, o_b, out_d, p, hg, seg64, w_out)


def _layer(x2, mod, lw, consts, batch, seq_len, ctx, rope_tabs, hy):
    n = batch * seq_len
    p = _inproj(x2, mod, lw["norm_g"], lw["w_in"], seq_len)

    q_a, ckvn, krp = _mla_q(p, lw, consts, seq_len, rope_tabs)
    ckv3 = ckvn.reshape(batch, seq_len, MLA_KV_LORA)
    krp3 = krp.reshape(batch, seq_len, LANE)
    if ctx is not None:
        cache_kr = jnp.pad(ctx[1], ((0, 0), (0, 0), (KR_OFF, LANE - KR_OFF - MLA_ROPE)))
        ckv_all = jnp.concatenate([ckv3, ctx[0]], axis=1)
        kr_all = jnp.concatenate([krp3, cache_kr], axis=1)
    else:
        ckv_all, kr_all = ckv3, krp3
    lk = ckv_all.shape[1]
    k_a, v_a = _mla_kv(ckv_all.reshape(batch * lk, MLA_KV_LORA), kr_all.reshape(batch * lk, LANE), lw, consts)
    ot_a = _mla_attn(q_a, k_a.reshape(batch, lk, 512), _vt_with_ones(v_a.reshape(batch, lk, BRANCH)), batch, seq_len)

    q_b, k_b, kd = _diff_prep(p, lw, consts, seq_len, rope_tabs)
    dv = p[:, P_DV:P_DV + BRANCH]
    k_b3 = k_b.reshape(batch, seq_len, BRANCH)
    v_b3 = dv.astype(BF16).reshape(batch, seq_len, BRANCH)
    if ctx is not None:
        k_b3 = jnp.concatenate([k_b3, ctx[2].reshape(batch, -1, BRANCH).astype(BF16)], axis=1)
        v_b3 = jnp.concatenate([v_b3, ctx[3].reshape(batch, -1, BRANCH).astype(BF16)], axis=1)
    ot_b = _diff_attn(q_b, k_b3, _vt_with_ones(v_b3), lw["diff_lambda"], lw["subln_col"], lw["lam_init"],
                      batch, seq_len)

    if ctx is not None:
        s0 = ctx[4]
    else:
        s0 = jnp.zeros((batch, 2, HGRN_HEADS, HGRN_DK, HGRN_DK), F32)
    eye = jnp.eye(HGRN_HEADS, dtype=F32)
    st0 = jnp.einsum("bdhke,hg->bdhegk", s0, eye).reshape(batch, 2, BRANCH, BRANCH)
    o_f, o_b, st_out = _hgrn(p, lw["hgrn_lb"], consts["hgrn_dd"], st0, batch, seq_len)
    st5 = st_out.reshape(batch, 2, HGRN_HEADS, HGRN_DK, HGRN_HEADS, HGRN_DK)
    states = jnp.stack([st5[:, :, h, :, h, :] for h in range(HGRN_HEADS)], axis=2).swapaxes(-1, -2)

    hu = p[:, P_HU:P_HU + 3 * HY_CH].reshape(batch, seq_len, 3 * HY_CH)
    hu_pad = jnp.pad(hu, ((0, 0), (1, 1), (0, 0)))
    v_d, x1, x2g = _hy_conv3(hu_pad[:, :-2].reshape(n, -1), hu.reshape(n, -1), hu_pad[:, 2:].reshape(n, -1),
                             lw["hy_conv_w"], lw["hy_conv_b"])
    filt = _hy_filter(hy["feats"], hy["window"], lw["hy_w1"], lw["hy_b1"], lw["hy_w2"], lw["hy_b2"], lw["hy_w3"],
                      lw["hy_freq"])
    taps = []
    for o in range(HY_ORDER):
        hf = filt[:, (2 * o) * HY_CH:(2 * o + 1) * HY_CH]
        hb = filt[:, (2 * o + 1) * HY_CH:(2 * o + 2) * HY_CH]
        taps.append(jnp.concatenate([hf, hb[::-1]], axis=0))
    taps = jnp.concatenate(taps, axis=1)
    k1n, n1o, n1t = hy["k1n"], hy["n1o"], hy["n1t"]
    ta = _fft_a(hy["fa_full"], taps.reshape(1, n1t, FFT_N2, HY_ORDER * HY_CH), BF16)
    spec = _fft_b(hy["mf"], None, ta.reshape(1, 2, k1n, FFT_N2, HY_ORDER * HY_CH), None, 0)[0]
    z4 = v_d.reshape(batch, n1o, FFT_N2, HY_CH)
    for o, xg in enumerate((x1, x2g)):
        a = _fft_a(hy["fa_half"], z4, BF16).reshape(batch, 2, k1n, FFT_N2, HY_CH)
        pk = _fft_b(hy["mf"], hy["mi"], a, spec, o).reshape(batch, 2 * k1n, FFT_N2, HY_CH)
        z4 = _fft_a_inv(hy["g"], pk, xg.reshape(batch, n1o, FFT_N2, HY_CH), z4, lw["hy_bias"][o:o + 1])
    out_d = z4.reshape(n, HY_CH)

    y = _outproj(x2, mod, ot_a, ot_b, o_f, o_b, out_d, p, lw["hgrn_out_g"], consts["seg64"], lw["w_out"], seq_len)
    new = None
    if ctx is None:
        new = (ckv3, krp3[:, :, KR_OFF:KR_OFF + MLA_ROPE],
               kd.reshape(batch, seq_len, DIFF_HEADS, 2, DIFF_HD),
               dv.reshape(batch, seq_len, DIFF_HEADS, 2 * DIFF_HD), states)
    return y, new


def _rope_tables(seq_len):
    half = MLA_ROPE // 2
    inv = ROPE_BASE ** (-jnp.arange(0, half, 2, dtype=F32) / half)
    rows = seq_len // GRID_W
    row = jnp.repeat(jnp.arange(rows, dtype=F32), GRID_W)
    col = (jnp.arange(rows * GRID_W) % GRID_W).astype(F32)
    ar, ac = row[:, None] * inv, col[:, None] * inv
    cos32 = jnp.concatenate([jnp.cos(ar), jnp.cos(ar), jnp.cos(ac), jnp.cos(ac)], axis=-1)
    sin32 = jnp.concatenate([-jnp.sin(ar), jnp.sin(ar), -jnp.sin(ac), jnp.sin(ac)], axis=-1)
    pad = ((0, 0), (KR_OFF, LANE - KR_OFF - MLA_ROPE))
    return dict(cos_mla=jnp.pad(cos32, pad, constant_values=1.0), sin_mla=jnp.pad(sin32, pad),
                cos_diff=jnp.tile(cos32, (1, 2 * DIFF_HEADS)), sin_diff=jnp.tile(sin32, (1, 2 * DIFF_HEADS)))


def _hy_tables(seq_len):
    feats, window = _hy_static(seq_len)
    return dict(feats=feats, window=window, **_fft_tables(seq_len))


def _layer_weights(l, w_in_p, lb, W):
    def head_pad(w, width, per):
        k = w.shape[0]
        w = w.reshape(k, MLA_HEADS, per)[:, :, :width]
        return jnp.pad(w, ((0, 0), (0, 0), (0, LANE - width))).reshape(k, MLA_HEADS * LANE)

    w_ukv = W["mla_w_ukv"][l].reshape(MLA_KV_LORA, MLA_HEADS, MLA_NOPE + MLA_V)
    nope_g, rope_g = W["mla_nope_g"][l], W["mla_rope_g"][l]
    zeros32 = jnp.zeros((MLA_ROPE,), F32)
    zeros64 = jnp.zeros((MLA_NOPE,), F32)
    gq = jnp.tile(jnp.concatenate([nope_g[0], rope_g[0], zeros32]), MLA_HEADS).reshape(1, 512)
    gk = jnp.tile(jnp.concatenate([nope_g[1], zeros64]), MLA_HEADS).reshape(1, 512)
    gkr = jnp.concatenate([zeros64, rope_g[1], zeros32]).reshape(1, LANE)
    return dict(
        norm_g=W["norm_g"][l], w_in=w_in_p[l], w_out=W["w_out"][l].astype(BF16),
        qn_g=W["mla_q_norm_g"][l].reshape(1, -1),
        w_uq=head_pad(W["mla_w_uq"][l], MLA_NOPE + MLA_ROPE, MLA_NOPE + MLA_ROPE).astype(BF16),
        kvn_g=W["mla_kv_norm_g"][l].reshape(1, -1),
        w_uk=jnp.pad(w_ukv[:, :, :MLA_NOPE], ((0, 0), (0, 0), (0, LANE - MLA_NOPE))).reshape(MLA_KV_LORA, 512)
        .astype(BF16),
        w_uv=w_ukv[:, :, MLA_NOPE:].reshape(MLA_KV_LORA, BRANCH).astype(BF16),
        gq=gq, gk=gk, gkr=gkr,
        dgq=jnp.tile(W["diff_qk_g"][l, 0], 2 * DIFF_HEADS).reshape(1, BRANCH),
        dgk=jnp.tile(W["diff_qk_g"][l, 1], 2 * DIFF_HEADS).reshape(1, BRANCH),
        diff_lambda=W["diff_lambda"][l], subln_col=W["diff_subln_g"][l].reshape(2 * DIFF_HD, 1),
        lam_init=0.8 - 0.6 * math.exp(-0.3 * l),
        hgrn_lb=lb[:, l].reshape(2, 1, BRANCH),
        hgrn_out_g=jnp.tile(W["hgrn_out_g"][l], HGRN_HEADS).reshape(1, BRANCH),
        hy_conv_w=W["hy_conv_w"][l], hy_conv_b=W["hy_conv_b"][l].reshape(1, -1),
        hy_w1=jnp.pad(W["hy_w1"][l], ((0, LANE - HY_EMB), (0, 0))), hy_b1=W["hy_b1"][l].reshape(1, -1),
        hy_w2=W["hy_w2"][l], hy_b2=W["hy_b2"][l].reshape(1, -1), hy_w3=W["hy_w3"][l],
        hy_freq=W["hy_sin_freq"][l], hy_bias=W["hy_bias"][l],
    )


def kernel(x_prompt, x_sample, cache_mla_ckv, cache_mla_krope, cache_diff_k, cache_diff_v, state_hgrn, c, c_ctx,
           norm_g, w_mod, b_mod, w_in, w_out, mla_q_norm_g, mla_w_uq, mla_kv_norm_g, mla_w_ukv, mla_nope_g,
           mla_rope_g, diff_qk_g, diff_lambda, diff_subln_g, hgrn_lb_logits, hgrn_out_g, hy_conv_w, hy_conv_b,
           hy_w1, hy_b1, hy_w2, hy_b2, hy_w3, hy_sin_freq, hy_bias):
    W = dict(norm_g=norm_g, w_out=w_out, mla_q_norm_g=mla_q_norm_g, mla_w_uq=mla_w_uq,
             mla_kv_norm_g=mla_kv_norm_g, mla_w_ukv=mla_w_ukv, mla_nope_g=mla_nope_g, mla_rope_g=mla_rope_g,
             diff_qk_g=diff_qk_g, diff_lambda=diff_lambda, diff_subln_g=diff_subln_g, hgrn_out_g=hgrn_out_g,
             hy_conv_w=hy_conv_w, hy_conv_b=hy_conv_b, hy_w1=hy_w1, hy_b1=hy_b1, hy_w2=hy_w2, hy_b2=hy_b2,
             hy_w3=hy_w3, hy_sin_freq=hy_sin_freq, hy_bias=hy_bias)
    bp, lp, _ = x_prompt.shape
    bs, ls, _ = x_sample.shape

    w_in_z = jnp.pad(w_in, ((0, 0), (0, 0), (0, 1)))
    w_in_p = jnp.take(w_in_z, jnp.asarray(_in_col_perm()), axis=2).astype(BF16)
    cvecs = jnp.concatenate([c_ctx[None, :], c, jnp.zeros((8 - 1 - bs, D_MODEL), F32)], axis=0)
    mods = _mod_all(cvecs, w_mod, b_mod)
    lb = _hgrn_lb(hgrn_lb_logits)
    seg512, cnt512 = _mla_seg()
    consts = dict(seg512=seg512, cnt512=cnt512, seg32=_seg_const(BRANCH, DIFF_HD), seg64=_seg_const(BRANCH, HGRN_DK),
                  hgrn_dd=_hgrn_consts())
    lws = [_layer_weights(l, w_in_p, lb, W) for l in range(DEPTH)]

    hy_p = _hy_tables(lp)
    y = x_prompt.reshape(bp * lp, D_MODEL)
    per_layer = []
    for l in range(DEPTH):
        mod = mods[l, 0:1].reshape(1, 3, D_MODEL)
        y, new = _layer(y, mod, lws[l], consts, bp, lp, None, None, hy_p)
        per_layer.append(new)
    y_prompt = y.reshape(bp, lp, D_MODEL)
    news = [jnp.stack([s[i] for s in per_layer], axis=1) for i in range(5)]

    hy_s = _hy_tables(ls)
    rope_tabs = _rope_tables(ls)
    y = x_sample.reshape(bs * ls, D_MODEL)
    for l in range(DEPTH):
        mod = mods[l, 1:1 + bs].reshape(bs, 3, D_MODEL)
        ctx = (cache_mla_ckv[:, l], cache_mla_krope[:, l], cache_diff_k[:, l], cache_diff_v[:, l], state_hgrn[:, l])
        y, _ = _layer(y, mod, lws[l], consts, bs, ls, ctx, rope_tabs, hy_s)
    y_sample = y.reshape(bs, ls, D_MODEL)

    return (y_prompt, y_sample, news[0], news[1], news[2], news[3], news[4])
```

```python
import functools
import math

import numpy as np
import jax
import jax.numpy as jnp
from jax import lax
from jax.experimental import pallas as pl
from jax.experimental.pallas import tpu as pltpu

F32 = jnp.float32
BF16 = jnp.bfloat16

D_MODEL = 1024
DEPTH = 4
GRID_W = 64
ROPE_BASE = 10000.0
EPS = 1e-6
BRANCH = 256
MLA_HEADS = 4
MLA_NOPE = 64
MLA_ROPE = 32
MLA_V = 64
MLA_Q_LORA = 256
MLA_KV_LORA = 128
MLA_SCALE = (MLA_NOPE + MLA_ROPE) ** -0.5
DIFF_HEADS = 4
DIFF_HD = 32
DIFF_SCALE = DIFF_HD ** -0.5
HGRN_HEADS = 4
HGRN_DK = 64
HGRN_CHUNK = 128
HGRN_LEVELS = 7
HGRN_MM_LEVELS = 3
HY_CH = 256
HY_ORDER = 2
HY_EMB = 33
HY_BANDS = 16
HY_FH = 64
HY_DECAY_TARGET = 0.01
HY_FAST_DECAY = 0.3
HY_SLOW_DECAY = 1.5
IN_COLS = 4000

LANE = 128
LOG2E = math.log2(math.e)
VT_ROWS = 80
FFT_N2 = 128
FFT_BLOCK_BYTES = 2 * 1024 * 1024
MLA_AHEAD = 8
DIFF_AHEAD = 3
MLA_SUB = 256
DIFF_SUB = 512
VMEM_LIMIT = 52 * 1024 * 1024

P_CQ, P_CKV, P_KR, P_DQ, P_DK, P_DV = 0, 256, 384, 512, 768, 1024
P_HQ, P_HZF, P_HZB, P_HI, P_HU, P_GATE = 1280, 1536, 1792, 2048, 2304, 3072
P_COLS = 4096
KR_OFF = 64


def _in_col_perm():
    src = np.full((P_COLS,), IN_COLS, np.int32)

    def put(dst, lo, n):
        src[dst:dst + n] = np.arange(lo, lo + n)

    put(P_CQ, 0, 256)
    put(P_CKV, 256, 128)
    put(P_KR + KR_OFF, 384, 32)
    put(P_GATE, 416, 256)
    put(P_DQ, 672, 256)
    put(P_DK, 928, 256)
    put(P_DV, 1184, 256)
    put(P_GATE + 256, 1440, 256)
    put(P_HQ, 1696, 256)
    put(P_HZF, 1952, 256)
    put(P_HZB, 2208, 256)
    put(P_HI, 2464, 256)
    put(P_GATE + 512, 2720, 256)
    put(P_HU, 2976, 768)
    put(P_GATE + 768, 3744, 256)
    return src


def _reorder_in_cols(w):
    src = _in_col_perm()
    pieces, lo = [], 0
    while lo < P_COLS:
        hi = lo + 1
        if src[lo] == IN_COLS:
            while hi < P_COLS and src[hi] == IN_COLS:
                hi += 1
            pieces.append(jnp.zeros(w.shape[:-1] + (hi - lo,), w.dtype))
        else:
            while hi < P_COLS and src[hi] == src[hi - 1] + 1:
                hi += 1
            pieces.append(w[..., int(src[lo]):int(src[lo]) + hi - lo])
        lo = hi
    return jnp.concatenate(pieces, axis=-1)


def _cparams(sem):
    return pltpu.CompilerParams(dimension_semantics=sem, vmem_limit_bytes=VMEM_LIMIT)


def _bdot(a, b):
    return jnp.dot(a.astype(BF16), b.astype(BF16), preferred_element_type=F32)


def _nt(a, b):
    return lax.dot_general(a.astype(BF16), b.astype(BF16), (((1,), (1,)), ((), ())), preferred_element_type=F32)


def _split2(a):
    hi = a.astype(BF16)
    lo = (a - hi.astype(F32)).astype(BF16)
    return hi, lo


def _dot3(a, b):
    ah, al = _split2(a)
    bh, bl = _split2(b)
    d = functools.partial(jnp.dot, preferred_element_type=F32)
    return d(ah, bh) + d(ah, bl) + d(al, bh)


def _segsum(v, seg):
    hi, lo = _split2(v)
    d = functools.partial(jnp.dot, preferred_element_type=F32)
    return d(hi, seg) + d(lo, seg)


def _rms(x, g):
    return x * lax.rsqrt(jnp.mean(x * x, axis=-1, keepdims=True) + EPS) * g


def _swap8(x):
    w = x.shape[-1]
    lane = lax.broadcasted_iota(jnp.int32, x.shape, x.ndim - 1)
    up = pltpu.roll(x, w - 8, x.ndim - 1)
    dn = pltpu.roll(x, 8, x.ndim - 1)
    return jnp.where((lane & 15) < 8, up, dn)


def _tile_lanes(x, n):
    return x if n == 1 else jnp.concatenate([x] * n, axis=-1)


def _mod_kernel(c_ref, w_ref, b_ref, o_ref):
    c = c_ref[...]
    o_ref[0] = _dot3(c * jax.nn.sigmoid(c), w_ref[0]) + b_ref[0]


def _mod_all(cvecs, w_mod, b_mod):
    nt = 3
    return pl.pallas_call(
        _mod_kernel,
        grid=(DEPTH, nt),
        in_specs=[pl.BlockSpec((8, D_MODEL), lambda l, j: (0, 0)),
                  pl.BlockSpec((1, D_MODEL, D_MODEL), lambda l, j: (l, 0, j)),
                  pl.BlockSpec((1, 1, D_MODEL), lambda l, j: (l, 0, j))],
        out_specs=pl.BlockSpec((1, 8, D_MODEL), lambda l, j: (l, 0, j)),
        out_shape=jax.ShapeDtypeStruct((DEPTH, 8, 3 * D_MODEL), F32),
        compiler_params=_cparams(("arbitrary", "arbitrary")),
        name="mod",
    )(cvecs, w_mod, b_mod.reshape(DEPTH, 1, 3 * D_MODEL))


def _lb_kernel(x_ref, o_ref):
    x = x_ref[...]
    rows = [x[l:l + 1, :] for l in range(DEPTH)]
    m = functools.reduce(jnp.maximum, rows)
    e = [jnp.exp(r - m) for r in rows]
    tot = functools.reduce(lambda a, b: a + b, e)
    acc = jnp.zeros_like(tot)
    o_ref[0:1, :] = acc
    for l in range(1, DEPTH):
        acc = acc + e[l] / tot
        o_ref[l:l + 1, :] = acc


def _hgrn_lb(logits):
    flat = logits.transpose(1, 0, 2).reshape(DEPTH, 2 * BRANCH)
    lb = pl.pallas_call(
        _lb_kernel,
        out_shape=jax.ShapeDtypeStruct(flat.shape, F32),
        name="hgrn_lb",
    )(flat)
    return lb.reshape(DEPTH, 2, BRANCH).transpose(1, 0, 2)


def _inproj_kernel(x_ref, mod_ref, g_ref, w_ref, p_ref):
    h = _rms(x_ref[...], g_ref[...]) * (1.0 + mod_ref[0, 1:2, :]) + mod_ref[0, 0:1, :]
    p_ref[...] = jnp.dot(h.astype(BF16), w_ref[...], preferred_element_type=F32)


def _inproj(x2, mod, norm_g, w_in_p, seq_len):
    n = x2.shape[0]
    tm = 256
    per_batch = mod.shape[0] > 1
    tiles_per_seq = seq_len // tm
    mod_idx = (lambda i: (i // tiles_per_seq, 0, 0)) if per_batch else (lambda i: (0, 0, 0))
    return pl.pallas_call(
        _inproj_kernel,
        grid=(n // tm,),
        in_specs=[pl.BlockSpec((tm, D_MODEL), lambda i: (i, 0)),
                  pl.BlockSpec((1, 3, D_MODEL), mod_idx),
                  pl.BlockSpec((1, D_MODEL), lambda i: (0, 0)),
                  pl.BlockSpec((D_MODEL, P_COLS), lambda i: (0, 0))],
        out_specs=pl.BlockSpec((tm, P_COLS), lambda i: (i, 0)),
        out_shape=jax.ShapeDtypeStruct((n, P_COLS), F32),
        compiler_params=_cparams(("parallel",)),
        name="inproj",
    )(x2, mod, norm_g.reshape(1, D_MODEL), w_in_p)


def _mla_seg():
    sid = np.zeros((512,), np.int32)
    cnt = np.ones((512,), np.float32)
    for h in range(MLA_HEADS):
        b = 128 * h
        sid[b:b + 64] = 3 * h
        sid[b + 64:b + 96] = 3 * h + 1
        sid[b + 96:b + 128] = 3 * h + 2
        cnt[b:b + 64] = 1.0 / 64
        cnt[b + 64:b + 128] = 1.0 / 32
    seg = (sid[:, None] == sid[None, :]).astype(np.float32)
    return jnp.asarray(seg, BF16), jnp.asarray(cnt.reshape(1, 512))


def _mla_q_kernel(rope, cq_ref, ckv_ref, kr_ref, qng_ref, wuq_ref, kvg_ref, gq_ref, gkr_ref, seg_ref, cnt_ref,
                  *rest):
    if rope:
        cos_ref, sin_ref, q_ref, ckvn_ref, krp_ref = rest
    else:
        q_ref, ckvn_ref, krp_ref = rest
    cqn = _rms(cq_ref[...], qng_ref[...])
    q = _bdot(cqn, wuq_ref[...])
    ss = _segsum(q * q, seg_ref[...]) * cnt_ref[...]
    qn = q * lax.rsqrt(ss + EPS) * gq_ref[...]
    ckvn_ref[...] = _rms(ckv_ref[...], kvg_ref[...])
    kr = kr_ref[...]
    krn = kr * lax.rsqrt(jnp.sum(kr * kr, axis=-1, keepdims=True) * (1.0 / MLA_ROPE) + EPS) * gkr_ref[...]
    if rope:
        cos, sin = cos_ref[...], sin_ref[...]
        qn = qn * _tile_lanes(cos, MLA_HEADS) + _swap8(qn) * _tile_lanes(sin, MLA_HEADS)
        krn = krn * cos + _swap8(krn) * sin
    q_ref[...] = (qn * (MLA_SCALE * LOG2E)).astype(BF16)
    krp_ref[...] = krn


def _mla_q(p, lw, consts, seq_len, rope_tabs):
    n = p.shape[0]
    tm = 256
    rope = rope_tabs is not None
    full = lambda shape: pl.BlockSpec(shape, lambda i: (0,) * len(shape))
    in_specs = [pl.BlockSpec((tm, 256), lambda i: (i, P_CQ // 256)),
                pl.BlockSpec((tm, 128), lambda i: (i, P_CKV // 128)),
                pl.BlockSpec((tm, 128), lambda i: (i, P_KR // 128)),
                full((1, 256)), full((256, 512)), full((1, 128)), full((1, 512)), full((1, 128)),
                full((512, 512)), full((1, 512))]
    args = [p, p, p, lw["qn_g"], lw["w_uq"], lw["kvn_g"], lw["gq"], lw["gkr"], consts["seg512"], consts["cnt512"]]
    if rope:
        tps = seq_len // tm
        in_specs += [pl.BlockSpec((tm, 128), lambda i: (i % tps, 0))] * 2
        args += [rope_tabs["cos_mla"], rope_tabs["sin_mla"]]
    return pl.pallas_call(
        functools.partial(_mla_q_kernel, rope),
        grid=(n // tm,),
        in_specs=in_specs,
        out_specs=[pl.BlockSpec((tm, 512), lambda i: (i, 0)),
                   pl.BlockSpec((tm, 128), lambda i: (i, 0)),
                   pl.BlockSpec((tm, 128), lambda i: (i, 0))],
        out_shape=[jax.ShapeDtypeStruct((n, 512), BF16),
                   jax.ShapeDtypeStruct((n, 128), F32),
                   jax.ShapeDtypeStruct((n, 128), F32)],
        compiler_params=_cparams(("parallel",)),
        name="mla_q",
    )(*args)


def _mla_kv_kernel(ckvn_ref, krp_ref, wuk_ref, wuv_ref, gk_ref, seg_ref, cnt_ref, k_ref, v_ref):
    c = ckvn_ref[...].astype(BF16)
    kn = jnp.dot(c, wuk_ref[...], preferred_element_type=F32)
    ss = _segsum(kn * kn, seg_ref[...]) * cnt_ref[...]
    k = kn * lax.rsqrt(ss + EPS) * gk_ref[...] + _tile_lanes(krp_ref[...], MLA_HEADS)
    k_ref[...] = k.astype(BF16)
    v_ref[...] = jnp.dot(c, wuv_ref[...], preferred_element_type=F32).astype(BF16)


def _mla_kv(ckvn, krp, lw, consts):
    n = ckvn.shape[0]
    tm = 512
    full = lambda shape: pl.BlockSpec(shape, lambda i: (0,) * len(shape))
    return pl.pallas_call(
        _mla_kv_kernel,
        grid=(n // tm,),
        in_specs=[pl.BlockSpec((tm, 128), lambda i: (i, 0)), pl.BlockSpec((tm, 128), lambda i: (i, 0)),
                  full((128, 512)), full((128, 256)), full((1, 512)), full((512, 512)), full((1, 512))],
        out_specs=[pl.BlockSpec((tm, 512), lambda i: (i, 0)), pl.BlockSpec((tm, 256), lambda i: (i, 0))],
        out_shape=[jax.ShapeDtypeStruct((n, 512), BF16), jax.ShapeDtypeStruct((n, 256), BF16)],
        compiler_params=_cparams(("parallel",)),
        name="mla_kv",
    )(ckvn, krp, lw["w_uk"], lw["w_uv"], lw["gk"], consts["seg512"], consts["cnt512"])


def _softmax_pv(qs, k_ref, vt_ref, key_chunk, sub_rows, n_ahead):
    lk = k_ref.shape[1]
    nch = lk // key_chunk
    nq = len(qs)
    sub = min(sub_rows, key_chunk)
    nsub = key_chunk // sub

    def scores(c, u):
        lo = c * key_chunk + u * sub
        ks = k_ref[0, lo:lo + sub, :]
        return [_nt(ks, q) for q in qs]

    def chunk_max(s_chunk, j):
        mc = functools.reduce(jnp.maximum, [s_chunk[u][j] for u in range(nsub)])
        return jnp.max(mc, axis=0, keepdims=True)

    s_buf = {c: [scores(c, u) for u in range(nsub)] for c in range(min(n_ahead, nch))}
    m = [None] * nq
    acc = [None] * nq
    m_new = [chunk_max(s_buf[0], j) for j in range(nq)]
    for c in range(nch):
        s_cur = s_buf.pop(c)
        ahead = c + n_ahead
        if ahead < nch:
            s_buf[ahead] = []
        pv = [None] * nq
        for u in range(nsub):
            if ahead < nch:
                s_buf[ahead].append(scores(ahead, u))
            lo = c * key_chunk + u * sub
            vs = vt_ref[0, 0, :, lo:lo + sub]
            for j in range(nq):
                part = jnp.dot(vs, jnp.exp2(s_cur[u][j] - m_new[j]).astype(BF16), preferred_element_type=F32)
                pv[j] = part if pv[j] is None else pv[j] + part
        for j in range(nq):
            acc[j] = pv[j] if c == 0 else acc[j] * jnp.exp2(m[j] - m_new[j]) + pv[j]
            m[j] = m_new[j]
        if c + 1 < nch:
            m_new = [jnp.maximum(m[j], chunk_max(s_buf[c + 1], j)) for j in range(nq)]
    return acc


def _mla_attn_kernel(key_chunk, q_ref, k_ref, vt_ref, o_ref):
    (acc,) = _softmax_pv([q_ref[...]], k_ref, vt_ref, key_chunk, MLA_SUB, MLA_AHEAD)
    o_ref[0] = acc[0:MLA_V] / acc[MLA_V:MLA_V + 1]


def _key_chunk(lk):
    return 512 if lk % 512 == 0 else lk


def _vt_with_ones(v3):
    b, lk, _ = v3.shape
    vt = v3.reshape(b, lk, 4, 64).transpose(0, 2, 3, 1)
    extra = jnp.zeros((b, 4, VT_ROWS - 64, lk), BF16).at[:, :, 0, :].set(1.0)
    return jnp.concatenate([vt, extra], axis=2)


def _mla_attn(q, k, vt, batch, seq_len):
    lk = k.shape[1]
    tq = 256
    nq = seq_len // tq
    return pl.pallas_call(
        functools.partial(_mla_attn_kernel, _key_chunk(lk)),
        grid=(batch, MLA_HEADS, nq),
        in_specs=[pl.BlockSpec((tq, 128), lambda b, h, i: (b * nq + i, h)),
                  pl.BlockSpec((1, lk, 128), lambda b, h, i: (b, 0, h)),
                  pl.BlockSpec((1, 1, VT_ROWS, lk), lambda b, h, i: (b, h, 0, 0))],
        out_specs=pl.BlockSpec((1, MLA_V, tq), lambda b, h, i: (b, h, i)),
        out_shape=jax.ShapeDtypeStruct((batch, BRANCH, seq_len), F32),
        compiler_params=_cparams(("parallel", "parallel", "arbitrary")),
        name="mla_attn",
    )(q, k, vt)


def _seg_const(width, seg):
    sid = np.arange(width) // seg
    return jnp.asarray((sid[:, None] == sid[None, :]).astype(np.float32), BF16)


def _diff_prep_kernel(rope, dq_ref, dk_ref, gq_ref, gk_ref, seg_ref, *rest):
    if rope:
        cos_ref, sin_ref, q_ref, k_ref, kf_ref = rest
    else:
        q_ref, k_ref, kf_ref = rest
    seg = seg_ref[...]

    def norm(x, g):
        ss = _segsum(x * x, seg) * (1.0 / DIFF_HD)
        return x * lax.rsqrt(ss + EPS) * g

    q = norm(dq_ref[...], gq_ref[...])
    k = norm(dk_ref[...], gk_ref[...])
    kf_ref[...] = k
    if rope:
        cos, sin = cos_ref[...], sin_ref[...]
        q = q * cos + _swap8(q) * sin
        k = k * cos + _swap8(k) * sin
    q_ref[...] = (q * (DIFF_SCALE * LOG2E)).astype(BF16)
    k_ref[...] = k.astype(BF16)


def _diff_prep(p, lw, consts, seq_len, rope_tabs):
    n = p.shape[0]
    tm = 256
    rope = rope_tabs is not None
    full = lambda shape: pl.BlockSpec(shape, lambda i: (0,) * len(shape))
    in_specs = [pl.BlockSpec((tm, 256), lambda i: (i, P_DQ // 256)),
                pl.BlockSpec((tm, 256), lambda i: (i, P_DK // 256)),
                full((1, 256)), full((1, 256)), full((256, 256))]
    args = [p, p, lw["dgq"], lw["dgk"], consts["seg32"]]
    if rope:
        tps = seq_len // tm
        in_specs += [pl.BlockSpec((tm, 256), lambda i: (i % tps, 0))] * 2
        args += [rope_tabs["cos_diff"], rope_tabs["sin_diff"]]
    blk = pl.BlockSpec((tm, 256), lambda i: (i, 0))
    return pl.pallas_call(
        functools.partial(_diff_prep_kernel, rope),
        grid=(n // tm,),
        in_specs=in_specs,
        out_specs=[blk, blk, blk],
        out_shape=[jax.ShapeDtypeStruct((n, 256), BF16), jax.ShapeDtypeStruct((n, 256), BF16),
                   jax.ShapeDtypeStruct((n, 256), F32)],
        compiler_params=_cparams(("parallel",)),
        name="diff_prep",
    )(*args)


def _diff_attn_kernel(lam_init, key_chunk, q_ref, k_ref, vt_ref, lp_ref, g_ref, o_ref):
    h = pl.program_id(1)
    q = q_ref[...]
    lane = lax.broadcasted_iota(jnp.int32, q.shape, 1)
    base = (h % 2) * 64
    zero = jnp.zeros_like(q)

    def map_query(j):
        lo = base + 32 * j
        return jnp.where((lane >= lo) & (lane < lo + 32), q, zero)

    acc0, acc1 = _softmax_pv([map_query(0), map_query(1)], k_ref, vt_ref, key_chunk, DIFF_SUB, DIFF_AHEAD)
    lp = lp_ref[...]
    lam = (jnp.exp(jnp.sum(lp[0:1] * lp[1:2], axis=1, keepdims=True))
           - jnp.exp(jnp.sum(lp[2:3] * lp[3:4], axis=1, keepdims=True)) + lam_init)
    o = acc0[0:64] / acc0[64:65] - lam * (acc1[0:64] / acc1[64:65])
    ms = jnp.mean(o * o, axis=0, keepdims=True)
    o_ref[0] = o * lax.rsqrt(ms + EPS) * g_ref[...] * (1.0 - lam_init)


def _diff_attn(q, k, vt, lp, g_col, lam_init, batch, seq_len):
    lk = k.shape[1]
    tq = 256
    nq = seq_len // tq
    return pl.pallas_call(
        functools.partial(_diff_attn_kernel, lam_init, _key_chunk(lk)),
        grid=(batch, DIFF_HEADS, nq),
        in_specs=[pl.BlockSpec((tq, 128), lambda b, h, i: (b * nq + i, h // 2)),
                  pl.BlockSpec((1, lk, 128), lambda b, h, i: (b, 0, h // 2)),
                  pl.BlockSpec((1, 1, VT_ROWS, lk), lambda b, h, i: (b, h, 0, 0)),
                  pl.BlockSpec((4, DIFF_HD), lambda b, h, i: (0, 0)),
                  pl.BlockSpec((64, 1), lambda b, h, i: (0, 0))],
        out_specs=pl.BlockSpec((1, 64, tq), lambda b, h, i: (b, h, i)),
        out_shape=jax.ShapeDtypeStruct((batch, BRANCH, seq_len), F32),
        compiler_params=_cparams(("parallel", "parallel", "arbitrary")),
        name="diff_attn",
    )(q, k, vt, lp, g_col)


def _hgrn_consts():
    c = HGRN_CHUNK
    t = np.arange(c)
    low = (t[None, :] <= t[:, None]).astype(np.float32)
    blocks = []
    for j in range(HGRN_MM_LEVELS):
        m = 1 << j
        rho = (t // (2 * m)) * (2 * m) + m - 1
        sign = np.where((t // m) % 2 == 1, 1.0, -1.0)[:, None]
        blocks.append(sign * (low - (t[None, :] <= rho[:, None]).astype(np.float32)))
    blocks.append(low)
    fwd = np.concatenate(blocks, axis=0)
    bwd = np.concatenate([b[::-1, ::-1] for b in blocks], axis=0)
    return jnp.asarray(np.stack([fwd, bwd]), BF16)


def _hgrn_kernel(nc, qf_ref, zf_ref, vf_ref, qb_ref, zb_ref, vb_ref, lb_ref, dd_ref, s0_ref,
                 of_ref, ob_ref, sout_ref, st_ref):
    c = HGRN_CHUNK
    ci = pl.program_id(1)
    dirs = (0, 1)

    @pl.when(ci == 0)
    def _():
        st_ref[...] = s0_ref[0]

    lane = lax.broadcasted_iota(jnp.int32, (1, BRANCH), 1)
    head_masks = [(lane >= HGRN_DK * h) & (lane < HGRN_DK * (h + 1)) for h in range(HGRN_HEADS)]
    t_idx = lax.broadcasted_iota(jnp.int32, (c, HGRN_HEADS * c), 0)
    s_idx = lax.broadcasted_iota(jnp.int32, (c, HGRN_HEADS * c), 1) & (c - 1)
    pair_xor = t_idx ^ s_idx
    row = lax.broadcasted_iota(jnp.int32, (c, 1), 0)
    row_dir = (row, c - 1 - row)

    def stack_heads(x):
        xb = x.astype(BF16)
        zero = jnp.zeros_like(xb)
        return jnp.concatenate([jnp.where(hm, xb, zero) for hm in head_masks], axis=0)

    q = (qf_ref[...], qb_ref[...])
    v = (vf_ref[...], vb_ref[...])
    z = (zf_ref[...], zb_ref[...])
    lb = (lb_ref[0], lb_ref[1])
    g = [jnp.log(lb[d] + (1.0 - lb[d]) * jax.nn.sigmoid(z[d])) for d in dirs]
    kk = [(1.0 - lb[d]) * jax.nn.sigmoid(-z[d]) for d in dirs]
    sums = []
    for d in dirs:
        gh, gl = _split2(g[d])
        sums.append(jnp.dot(dd_ref[d], gh, preferred_element_type=F32)
                    + jnp.dot(dd_ref[d], gl, preferred_element_type=F32))
    b = [sums[d][HGRN_MM_LEVELS * c:] for d in dirs]
    b_tot = (b[0][c - 1:c], b[1][0:1])

    def neg_abs_decay(d, j, right):
        if j < HGRN_MM_LEVELS:
            return sums[d][j * c:(j + 1) * c]
        m = 1 << j
        off = m - 1 if d == 0 else m
        ref = jnp.concatenate([jnp.broadcast_to(b[d][g0 + off:g0 + off + 1], (2 * m, BRANCH))
                               for g0 in range(0, c, 2 * m)], axis=0)
        return jnp.where(right, b[d] - ref, ref - b[d])

    a = [None, None]
    for j in reversed(range(HGRN_LEVELS)):
        same_group = pair_xor < (2 << j)
        for d in dirs:
            right = ((row_dir[d] >> j) & 1) == 1
            e = jnp.exp(neg_abs_decay(d, j, right))
            qt = jnp.where(right, q[d] * e, 0.0)
            kt = jnp.where(right, 0.0, kk[d] * e)
            lvl = _nt(qt, stack_heads(kt))
            a[d] = lvl if a[d] is None else jnp.where(same_group, lvl, a[d])
    diagonal = pair_xor == 0
    for d in dirs:
        a[d] = jnp.where(diagonal, _nt(q[d], stack_heads(kk[d])), a[d])

    outs = (of_ref, ob_ref)
    for d in dirs:
        o = jnp.dot(a[d].astype(BF16), stack_heads(v[d]), preferred_element_type=F32)
        outs[d][...] = o + _nt(q[d] * jnp.exp(b[d]), st_ref[d])

    r2 = lax.broadcasted_iota(jnp.int32, (BRANCH, BRANCH), 0) // HGRN_DK
    c2 = lax.broadcasted_iota(jnp.int32, (BRANCH, BRANCH), 1) // HGRN_DK
    for d in dirs:
        kd = kk[d] * jnp.exp(b_tot[d] - b[d])
        upd = lax.dot_general(v[d].astype(BF16), kd.astype(BF16), (((0,), (0,)), ((), ())),
                              preferred_element_type=F32)
        st_new = st_ref[d] * jnp.exp(b_tot[d]) + jnp.where(r2 == c2, upd, 0.0)
        st_ref[d] = st_new

        @pl.when(ci == nc - 1)
        def _():
            sout_ref[0, d] = st_new


def _hgrn(p, lb_l, dd, st0, batch, seq_len):
    n = p.shape[0]
    c = HGRN_CHUNK
    nc = seq_len // c
    fwd = lambda col: pl.BlockSpec((c, 256), lambda b, i: (b * nc + i, col))
    bwd = lambda col: pl.BlockSpec((c, 256), lambda b, i: (b * nc + nc - 1 - i, col))
    whole = lambda shape: pl.BlockSpec(shape, lambda b, i: (0,) * len(shape))
    return pl.pallas_call(
        functools.partial(_hgrn_kernel, nc),
        grid=(batch, nc),
        in_specs=[fwd(P_HQ // 256), fwd(P_HZF // 256), fwd(P_HI // 256),
                  bwd(P_HQ // 256), bwd(P_HZB // 256), bwd(P_HI // 256),
                  whole((2, 1, 256)), whole((2, (HGRN_MM_LEVELS + 1) * c, c)),
                  pl.BlockSpec((1, 2, 256, 256), lambda b, i: (b, 0, 0, 0))],
        out_specs=[pl.BlockSpec((c, 256), lambda b, i: (b * nc + i, 0)),
                   pl.BlockSpec((c, 256), lambda b, i: (b * nc + nc - 1 - i, 0)),
                   pl.BlockSpec((1, 2, 256, 256), lambda b, i: (b, 0, 0, 0))],
        out_shape=[jax.ShapeDtypeStruct((n, 256), F32), jax.ShapeDtypeStruct((n, 256), F32),
                   jax.ShapeDtypeStruct((batch, 2, 256, 256), F32)],
        scratch_shapes=[pltpu.VMEM((2, 256, 256), F32)],
        compiler_params=_cparams(("parallel", "arbitrary")),
        name="hgrn",
    )(p, p, p, p, p, p, lb_l, dd, st0)


def _hy_conv3_kernel(tiles_per_seq, above_ref, cur_ref, below_ref, w_ref, b_ref, v_ref, x1_ref, x2_ref):
    i = pl.program_id(0)
    cur = cur_ref[...]
    tm = cur.shape[0]
    first = (i % tiles_per_seq) == 0
    last = (i % tiles_per_seq) == tiles_per_seq - 1
    above = jnp.where(first, 0.0, above_ref[7:8, :])
    below = jnp.where(last, 0.0, below_ref[0:1, :])
    row = lax.broadcasted_iota(jnp.int32, (tm, 1), 0)
    prev = jnp.where(row == 0, above, pltpu.roll(cur, 1, 0))
    nxt = jnp.where(row == tm - 1, below, pltpu.roll(cur, tm - 1, 0))
    w = w_ref[...]
    u = prev * w[0:1] + cur * w[1:2] + nxt * w[2:3] + b_ref[...]
    v_ref[...] = u[:, 0:256]
    x1_ref[...] = u[:, 256:512]
    x2_ref[...] = u[:, 512:768]


def _hy_conv3(p, w, b, seq_len):
    n = p.shape[0]
    tm = min(512, seq_len)
    nt = n // tm
    g = tm // 8
    col = P_HU // 768
    oblk = pl.BlockSpec((tm, 256), lambda i: (i, 0))
    return pl.pallas_call(
        functools.partial(_hy_conv3_kernel, seq_len // tm),
        grid=(nt,),
        in_specs=[pl.BlockSpec((8, 768), lambda i: (jnp.maximum(i * g - 1, 0), col)),
                  pl.BlockSpec((tm, 768), lambda i: (i, col)),
                  pl.BlockSpec((8, 768), lambda i: (jnp.minimum((i + 1) * g, nt * g - 1), col)),
                  pl.BlockSpec((3, 768), lambda i: (0, 0)), pl.BlockSpec((1, 768), lambda i: (0, 0))],
        out_specs=[oblk, oblk, oblk],
        out_shape=[jax.ShapeDtypeStruct((n, 256), F32)] * 3,
        compiler_params=_cparams(("parallel",)),
        name="hy_conv3",
    )(p, p, p, w, b)


def _hy_filter_kernel(feat_ref, w1_ref, b1_ref, w2_ref, b2_ref, w3_ref, fr_ref, win_ref, o_ref):
    fr = fr_ref[...]
    h = jnp.sin(fr[0:1] * (_dot3(feat_ref[...], w1_ref[...]) + b1_ref[...]))
    h = jnp.sin(fr[1:2] * (_dot3(h, w2_ref[...]) + b2_ref[...]))
    o_ref[...] = _dot3(h, w3_ref[0]) * _tile_lanes(win_ref[...], HY_ORDER)


def _hy_filter(feats2, window2, w1p, b1, w2, b2, w3d, freq):
    l2 = feats2.shape[0]
    ln = l2 // 2
    tm = min(512, ln)
    full = lambda shape: pl.BlockSpec(shape, lambda i: (0,) * len(shape))
    return pl.pallas_call(
        _hy_filter_kernel,
        grid=(l2 // tm,),
        in_specs=[pl.BlockSpec((tm, LANE), lambda i: (i, 0)),
                  full((LANE, HY_FH)), full((1, HY_FH)), full((HY_FH, HY_FH)), full((1, HY_FH)),
                  pl.BlockSpec((1, HY_FH, HY_ORDER * HY_CH), lambda i: (i // (ln // tm), 0, 0)), full((2, HY_FH)),
                  pl.BlockSpec((tm, HY_CH), lambda i: (i, 0))],
        out_specs=pl.BlockSpec((tm, HY_ORDER * HY_CH), lambda i: (i, 0)),
        out_shape=jax.ShapeDtypeStruct((l2, HY_ORDER * HY_CH), F32),
        compiler_params=_cparams(("parallel",)),
        name="hy_filter",
    )(feats2, w1p, b1, w2, b2, w3d, freq, window2)


def _fft_blocking(nb, n1, n2, ch):
    per_batch = n1 * n2 * ch * 4
    if per_batch <= FFT_BLOCK_BYTES:
        bb = max(1, min(nb, FFT_BLOCK_BYTES // per_batch))
        while nb % bb:
            bb -= 1
        return bb, n2
    rt = n2
    while n1 * rt * ch * 4 > FFT_BLOCK_BYTES and rt > 8:
        rt //= 2
    return 1, rt


def _fft_a_kernel(fh_ref, fl_ref, x_ref, o_ref):
    fh, fl = fh_ref[...], fl_ref[...]
    e = functools.partial(jnp.einsum, "kn,nrc->krc", preferred_element_type=F32)
    for b in range(x_ref.shape[0]):
        xh, xl = _split2(x_ref[b])
        o_ref[b] = (e(fh, xh) + e(fl, xh) + e(fh, xl)).astype(o_ref.dtype)


def _fft_a(fa, x4, out_dtype):
    nb, n1, n2, ch = x4.shape
    r = fa.shape[0]
    bb, rt = _fft_blocking(nb, n1, n2, ch)
    fh, fl = _split2(fa)
    return pl.pallas_call(
        _fft_a_kernel,
        grid=(nb // bb, n2 // rt),
        in_specs=[pl.BlockSpec((r, n1), lambda b, i: (0, 0)), pl.BlockSpec((r, n1), lambda b, i: (0, 0)),
                  pl.BlockSpec((bb, n1, rt, ch), lambda b, i: (b, 0, i, 0))],
        out_specs=pl.BlockSpec((bb, r, rt, ch), lambda b, i: (b, 0, i, 0)),
        out_shape=jax.ShapeDtypeStruct((nb, r, n2, ch), out_dtype),
        compiler_params=_cparams(("parallel", "parallel")),
        name="fft_a",
    )(fh, fl, x4)


def _fft_b_kernel(with_inverse, mf_ref, *rest):
    if with_inverse:
        mi_ref, a_ref, h_ref, o_ref = rest
    else:
        a_ref, o_ref = rest
    half = FFT_N2
    for b in range(a_ref.shape[0]):
        a = jnp.concatenate([a_ref[b, 0, 0], a_ref[b, 1, 0]], axis=0)
        x = jnp.dot(mf_ref[0], a.astype(BF16), preferred_element_type=F32)
        if with_inverse:
            xr, xi = x[:half], x[half:]
            hr, hi = h_ref[0, 0], h_ref[1, 0]
            y = jnp.concatenate([xr * hr - xi * hi, xr * hi + xi * hr], axis=0)
            x = jnp.dot(mi_ref[0], y.astype(BF16), preferred_element_type=F32)
        o_ref[b, 0, 0] = x[:half]
        o_ref[b, 1, 0] = x[half:]


def _fft_b(mf, mi, a5, spec, order):
    nb, _, k1n, n2, ch = a5.shape
    mat = pl.BlockSpec((1, 2 * n2, 2 * n2), lambda k: (k, 0, 0))
    blk = pl.BlockSpec((nb, 2, 1, n2, ch), lambda k: (0, 0, k, 0, 0))
    if spec is None:
        in_specs, args = [mat, blk], [mf, a5]
    else:
        in_specs = [mat, mat, blk, pl.BlockSpec((2, 1, n2, ch), lambda k: (0, k, 0, order))]
        args = [mf, mi, a5, spec]
    return pl.pallas_call(
        functools.partial(_fft_b_kernel, spec is not None),
        grid=(k1n,),
        in_specs=in_specs,
        out_specs=blk,
        out_shape=jax.ShapeDtypeStruct(a5.shape, F32),
        compiler_params=_cparams(("parallel",)),
        name="fft_b",
    )(*args)


def _fft_a_inv_kernel(gh_ref, gl_ref, p_ref, x_ref, z_ref, bias_ref, o_ref):
    gh, gl = gh_ref[...], gl_ref[...]
    e = functools.partial(jnp.einsum, "nk,krc->nrc", preferred_element_type=F32)
    for b in range(p_ref.shape[0]):
        ph, pl_ = _split2(p_ref[b])
        conv = e(gh, ph) + e(gl, ph) + e(gh, pl_)
        o_ref[b] = x_ref[b] * (conv + z_ref[b] * bias_ref[...])


def _fft_a_inv(g, p4, xg4, z4, bias):
    nb, n1, n2, ch = z4.shape
    r = g.shape[1]
    bb, rt = _fft_blocking(nb, n1, n2, ch)
    gh, gl = _split2(g)
    blk = pl.BlockSpec((bb, n1, rt, ch), lambda b, i: (b, 0, i, 0))
    return pl.pallas_call(
        _fft_a_inv_kernel,
        grid=(nb // bb, n2 // rt),
        in_specs=[pl.BlockSpec((n1, r), lambda b, i: (0, 0)), pl.BlockSpec((n1, r), lambda b, i: (0, 0)),
                  pl.BlockSpec((bb, r, rt, ch), lambda b, i: (b, 0, i, 0)), blk, blk,
                  pl.BlockSpec((1, 1, ch), lambda b, i: (0, 0, 0))],
        out_specs=blk,
        out_shape=jax.ShapeDtypeStruct(z4.shape, F32),
        compiler_params=_cparams(("parallel", "parallel")),
        name="fft_a_inv",
    )(gh, gl, p4, xg4, z4, bias.reshape(1, 1, ch))


def _fft_tables(ln):
    n = 2 * ln
    n1t = n // FFT_N2
    k1n = n1t // 2 + 1
    kk = np.arange(k1n)

    def stage_a(n1_in):
        ang = 2.0 * np.pi * ((kk[:, None] * np.arange(n1_in)[None, :]) % n1t) / n1t
        return jnp.asarray(np.concatenate([np.cos(ang), -np.sin(ang)], axis=0), F32)

    n1o = n1t // 2
    ang = 2.0 * np.pi * ((np.arange(n1o)[:, None] * kk[None, :]) % n1t) / n1t
    edge = (kk == 0) | (kk == n1t // 2)
    ck = np.where(edge, 1.0, 2.0) / n
    g = jnp.asarray(np.concatenate([ck * np.cos(ang), -ck * np.where(edge, 0.0, np.sin(ang))], axis=1), F32)

    k1 = jnp.arange(k1n, dtype=jnp.int32)[:, None, None]
    k2 = jnp.arange(FFT_N2, dtype=jnp.int32)[None, :, None]
    n2 = jnp.arange(FFT_N2, dtype=jnp.int32)[None, None, :]
    th = (2.0 * math.pi / n) * ((n2 * (k1 + n1t * k2)) % n).astype(F32)
    c, s = jnp.cos(th), jnp.sin(th)
    mf = jnp.concatenate([jnp.concatenate([c, s], axis=2), jnp.concatenate([-s, c], axis=2)], axis=1)
    return dict(fa_half=stage_a(n1o), fa_full=stage_a(n1t), g=g, mf=mf.astype(BF16),
                mi=mf.transpose(0, 2, 1).astype(BF16), k1n=k1n, n1o=n1o, n1t=n1t)


def _hy_static(ln):
    t = jnp.linspace(0.0, 1.0, ln, dtype=F32)[:, None]
    w = 2.0 * math.pi * jnp.arange(ln, dtype=F32) / ln
    f = jnp.linspace(1e-4, HY_BANDS - 1, HY_BANDS, dtype=F32)
    ang = w[:, None] * f[None, :]
    feats = jnp.concatenate([t, jnp.cos(ang), -jnp.sin(ang)], axis=-1)
    feats = jnp.pad(feats, ((0, 0), (0, LANE - HY_EMB)))
    min_decay = math.log(HY_DECAY_TARGET) / HY_SLOW_DECAY
    max_decay = math.log(HY_DECAY_TARGET) / HY_FAST_DECAY
    deltas = jnp.linspace(min_decay, max_decay, HY_CH, dtype=F32)
    window = jnp.exp(-t * jnp.abs(deltas))
    feats = jnp.concatenate([feats, feats[::-1]], axis=0)
    window = jnp.concatenate([window, window[::-1]], axis=0)
    return feats, window


def _outproj_kernel(x_ref, mod_ref, oa_ref, ob_ref, of_ref, obk_ref, od_ref, gate_ref, hg_ref, seg_ref, w_ref,
                    y_ref):
    gt = gate_ref[...]
    sg = gt * jax.nn.sigmoid(gt)
    oc = of_ref[...] + obk_ref[...]
    ss = _segsum(oc * oc, seg_ref[...]) * (1.0 / HGRN_DK)
    oc = oc * lax.rsqrt(ss + EPS) * hg_ref[...]
    acc = _bdot(oa_ref[0].T * sg[:, 0:256], w_ref[0:256, :])
    acc += _bdot(ob_ref[0].T * sg[:, 256:512], w_ref[256:512, :])
    acc += _bdot(oc * sg[:, 512:768], w_ref[512:768, :])
    acc += _bdot(od_ref[...] * sg[:, 768:1024], w_ref[768:1024, :])
    y_ref[...] = x_ref[...] + mod_ref[0, 2:3, :] * acc


def _outproj(x2, mod, ot_a, ot_b, o_f, o_b, out_d, p, hg, seg64, w_out, seq_len):
    n = x2.shape[0]
    tm = 256
    per_batch = mod.shape[0] > 1
    tps = seq_len // tm
    mod_idx = (lambda i: (i // tps, 0, 0)) if per_batch else (lambda i: (0, 0, 0))
    b256 = pl.BlockSpec((tm, 256), lambda i: (i, 0))
    bt = pl.BlockSpec((1, 256, tm), lambda i: (i // tps, 0, i % tps))
    return pl.pallas_call(
        _outproj_kernel,
        grid=(n // tm,),
        in_specs=[pl.BlockSpec((tm, D_MODEL), lambda i: (i, 0)),
                  pl.BlockSpec((1, 3, D_MODEL), mod_idx),
                  bt, bt, b256, b256, b256,
                  pl.BlockSpec((tm, 1024), lambda i: (i, P_GATE // 1024)),
                  pl.BlockSpec((1, 256), lambda i: (0, 0)),
                  pl.BlockSpec((256, 256), lambda i: (0, 0)),
                  pl.BlockSpec((D_MODEL, D_MODEL), lambda i: (0, 0))],
        out_specs=pl.BlockSpec((tm, D_MODEL), lambda i: (i, 0)),
        out_shape=jax.ShapeDtypeStruct((n, D_MODEL), F32),
        compiler_params=_cparams(("parallel",)),
        name="outproj",
    )(x2, mod, ot_a, ot_b, o_f, o_b, out_d, p, hg, seg64, w_out)


def _layer(x2, mod, lw, consts, batch, seq_len, ctx, rope_tabs, hy):
    n = batch * seq_len
    p = _inproj(x2, mod, lw["norm_g"], lw["w_in"], seq_len)

    q_a, ckvn, krp = _mla_q(p, lw, consts, seq_len, rope_tabs)
    ckv3 = ckvn.reshape(batch, seq_len, MLA_KV_LORA)
    krp3 = krp.reshape(batch, seq_len, LANE)
    if ctx is not None:
        cache_kr = jnp.pad(ctx[1], ((0, 0), (0, 0), (KR_OFF, LANE - KR_OFF - MLA_ROPE)))
        ckv_all = jnp.concatenate([ckv3, ctx[0]], axis=1)
        kr_all = jnp.concatenate([krp3, cache_kr], axis=1)
    else:
        ckv_all, kr_all = ckv3, krp3
    lk = ckv_all.shape[1]
    k_a, v_a = _mla_kv(ckv_all.reshape(batch * lk, MLA_KV_LORA), kr_all.reshape(batch * lk, LANE), lw, consts)
    ot_a = _mla_attn(q_a, k_a.reshape(batch, lk, 512), _vt_with_ones(v_a.reshape(batch, lk, BRANCH)), batch, seq_len)

    q_b, k_b, kd = _diff_prep(p, lw, consts, seq_len, rope_tabs)
    dv = p[:, P_DV:P_DV + BRANCH]
    k_b3 = k_b.reshape(batch, seq_len, BRANCH)
    v_b3 = dv.astype(BF16).reshape(batch, seq_len, BRANCH)
    if ctx is not None:
        k_b3 = jnp.concatenate([k_b3, ctx[2].reshape(batch, -1, BRANCH).astype(BF16)], axis=1)
        v_b3 = jnp.concatenate([v_b3, ctx[3].reshape(batch, -1, BRANCH).astype(BF16)], axis=1)
    ot_b = _diff_attn(q_b, k_b3, _vt_with_ones(v_b3), lw["diff_lambda"], lw["subln_col"], lw["lam_init"],
                      batch, seq_len)

    if ctx is not None:
        s0 = ctx[4]
    else:
        s0 = jnp.zeros((batch, 2, HGRN_HEADS, HGRN_DK, HGRN_DK), F32)
    eye = jnp.eye(HGRN_HEADS, dtype=F32)
    st0 = jnp.einsum("bdhke,hg->bdhegk", s0, eye).reshape(batch, 2, BRANCH, BRANCH)
    o_f, o_b, st_out = _hgrn(p, lw["hgrn_lb"], consts["hgrn_dd"], st0, batch, seq_len)
    st5 = st_out.reshape(batch, 2, HGRN_HEADS, HGRN_DK, HGRN_HEADS, HGRN_DK)
    states = jnp.stack([st5[:, :, h, :, h, :] for h in range(HGRN_HEADS)], axis=2).swapaxes(-1, -2)

    v_d, x1, x2g = _hy_conv3(p, lw["hy_conv_w"], lw["hy_conv_b"], seq_len)
    taps = _hy_filter(hy["feats"], hy["window"], lw["hy_w1"], lw["hy_b1"], lw["hy_w2"], lw["hy_b2"], lw["hy_w3"],
                      lw["hy_freq"])
    k1n, n1o, n1t = hy["k1n"], hy["n1o"], hy["n1t"]
    ta = _fft_a(hy["fa_full"], taps.reshape(1, n1t, FFT_N2, HY_ORDER * HY_CH), BF16)
    spec = _fft_b(hy["mf"], None, ta.reshape(1, 2, k1n, FFT_N2, HY_ORDER * HY_CH), None, 0)[0]
    z4 = v_d.reshape(batch, n1o, FFT_N2, HY_CH)
    for o, xg in enumerate((x1, x2g)):
        a = _fft_a(hy["fa_half"], z4, BF16).reshape(batch, 2, k1n, FFT_N2, HY_CH)
        pk = _fft_b(hy["mf"], hy["mi"], a, spec, o).reshape(batch, 2 * k1n, FFT_N2, HY_CH)
        z4 = _fft_a_inv(hy["g"], pk, xg.reshape(batch, n1o, FFT_N2, HY_CH), z4, lw["hy_bias"][o:o + 1])
    out_d = z4.reshape(n, HY_CH)

    y = _outproj(x2, mod, ot_a, ot_b, o_f, o_b, out_d, p, lw["hgrn_out_g"], consts["seg64"], lw["w_out"], seq_len)
    new = None
    if ctx is None:
        new = (ckv3, krp3[:, :, KR_OFF:KR_OFF + MLA_ROPE],
               kd.reshape(batch, seq_len, DIFF_HEADS, 2, DIFF_HD),
               dv.reshape(batch, seq_len, DIFF_HEADS, 2 * DIFF_HD), states)
    return y, new


def _rope_tables(seq_len):
    half = MLA_ROPE // 2
    inv = ROPE_BASE ** (-jnp.arange(0, half, 2, dtype=F32) / half)
    rows = seq_len // GRID_W
    row = jnp.repeat(jnp.arange(rows, dtype=F32), GRID_W)
    col = (jnp.arange(rows * GRID_W) % GRID_W).astype(F32)
    ar, ac = row[:, None] * inv, col[:, None] * inv
    cos32 = jnp.concatenate([jnp.cos(ar), jnp.cos(ar), jnp.cos(ac), jnp.cos(ac)], axis=-1)
    sin32 = jnp.concatenate([-jnp.sin(ar), jnp.sin(ar), -jnp.sin(ac), jnp.sin(ac)], axis=-1)
    pad = ((0, 0), (KR_OFF, LANE - KR_OFF - MLA_ROPE))
    return dict(cos_mla=jnp.pad(cos32, pad, constant_values=1.0), sin_mla=jnp.pad(sin32, pad),
                cos_diff=jnp.tile(cos32, (1, 2 * DIFF_HEADS)), sin_diff=jnp.tile(sin32, (1, 2 * DIFF_HEADS)))


def _hy_tables(seq_len):
    feats, window = _hy_static(seq_len)
    return dict(feats=feats, window=window, **_fft_tables(seq_len))


def _layer_weights(l, w_in_p, lb, W):
    def head_pad(w, width, per):
        k = w.shape[0]
        w = w.reshape(k, MLA_HEADS, per)[:, :, :width]
        return jnp.pad(w, ((0, 0), (0, 0), (0, LANE - width))).reshape(k, MLA_HEADS * LANE)

    w_ukv = W["mla_w_ukv"][l].reshape(MLA_KV_LORA, MLA_HEADS, MLA_NOPE + MLA_V)
    nope_g, rope_g = W["mla_nope_g"][l], W["mla_rope_g"][l]
    zeros32 = jnp.zeros((MLA_ROPE,), F32)
    zeros64 = jnp.zeros((MLA_NOPE,), F32)
    gq = jnp.tile(jnp.concatenate([nope_g[0], rope_g[0], zeros32]), MLA_HEADS).reshape(1, 512)
    gk = jnp.tile(jnp.concatenate([nope_g[1], zeros64]), MLA_HEADS).reshape(1, 512)
    gkr = jnp.concatenate([zeros64, rope_g[1], zeros32]).reshape(1, LANE)
    return dict(
        norm_g=W["norm_g"][l], w_in=w_in_p[l], w_out=W["w_out"][l].astype(BF16),
        qn_g=W["mla_q_norm_g"][l].reshape(1, -1),
        w_uq=head_pad(W["mla_w_uq"][l], MLA_NOPE + MLA_ROPE, MLA_NOPE + MLA_ROPE).astype(BF16),
        kvn_g=W["mla_kv_norm_g"][l].reshape(1, -1),
        w_uk=jnp.pad(w_ukv[:, :, :MLA_NOPE], ((0, 0), (0, 0), (0, LANE - MLA_NOPE))).reshape(MLA_KV_LORA, 512)
        .astype(BF16),
        w_uv=w_ukv[:, :, MLA_NOPE:].reshape(MLA_KV_LORA, BRANCH).astype(BF16),
        gq=gq, gk=gk, gkr=gkr,
        dgq=jnp.tile(W["diff_qk_g"][l, 0], 2 * DIFF_HEADS).reshape(1, BRANCH),
        dgk=jnp.tile(W["diff_qk_g"][l, 1], 2 * DIFF_HEADS).reshape(1, BRANCH),
        diff_lambda=W["diff_lambda"][l], subln_col=W["diff_subln_g"][l].reshape(2 * DIFF_HD, 1),
        lam_init=0.8 - 0.6 * math.exp(-0.3 * l),
        hgrn_lb=lb[:, l].reshape(2, 1, BRANCH),
        hgrn_out_g=jnp.tile(W["hgrn_out_g"][l], HGRN_HEADS).reshape(1, BRANCH),
        hy_conv_w=W["hy_conv_w"][l], hy_conv_b=W["hy_conv_b"][l].reshape(1, -1),
        hy_w1=jnp.pad(W["hy_w1"][l], ((0, LANE - HY_EMB), (0, 0))), hy_b1=W["hy_b1"][l].reshape(1, -1),
        hy_w2=W["hy_w2"][l], hy_b2=W["hy_b2"][l].reshape(1, -1),
        hy_w3=W["hy_w3"][l].reshape(HY_FH, HY_ORDER, 2, HY_CH).transpose(2, 0, 1, 3)
        .reshape(2, HY_FH, HY_ORDER * HY_CH),
        hy_freq=W["hy_sin_freq"][l], hy_bias=W["hy_bias"][l],
    )


def kernel(x_prompt, x_sample, cache_mla_ckv, cache_mla_krope, cache_diff_k, cache_diff_v, state_hgrn, c, c_ctx,
           norm_g, w_mod, b_mod, w_in, w_out, mla_q_norm_g, mla_w_uq, mla_kv_norm_g, mla_w_ukv, mla_nope_g,
           mla_rope_g, diff_qk_g, diff_lambda, diff_subln_g, hgrn_lb_logits, hgrn_out_g, hy_conv_w, hy_conv_b,
           hy_w1, hy_b1, hy_w2, hy_b2, hy_w3, hy_sin_freq, hy_bias):
    W = dict(norm_g=norm_g, w_out=w_out, mla_q_norm_g=mla_q_norm_g, mla_w_uq=mla_w_uq,
             mla_kv_norm_g=mla_kv_norm_g, mla_w_ukv=mla_w_ukv, mla_nope_g=mla_nope_g, mla_rope_g=mla_rope_g,
             diff_qk_g=diff_qk_g, diff_lambda=diff_lambda, diff_subln_g=diff_subln_g, hgrn_out_g=hgrn_out_g,
             hy_conv_w=hy_conv_w, hy_conv_b=hy_conv_b, hy_w1=hy_w1, hy_b1=hy_b1, hy_w2=hy_w2, hy_b2=hy_b2,
             hy_w3=hy_w3, hy_sin_freq=hy_sin_freq, hy_bias=hy_bias)
    bp, lp, _ = x_prompt.shape
    bs, ls, _ = x_sample.shape

    w_in_p = _reorder_in_cols(w_in.astype(BF16))
    cvecs = jnp.concatenate([c_ctx[None, :], c, jnp.zeros((8 - 1 - bs, D_MODEL), F32)], axis=0)
    mods = _mod_all(cvecs, w_mod, b_mod)
    lb = _hgrn_lb(hgrn_lb_logits)
    seg512, cnt512 = _mla_seg()
    consts = dict(seg512=seg512, cnt512=cnt512, seg32=_seg_const(BRANCH, DIFF_HD), seg64=_seg_const(BRANCH, HGRN_DK),
                  hgrn_dd=_hgrn_consts())
    lws = [_layer_weights(l, w_in_p, lb, W) for l in range(DEPTH)]

    hy_p = _hy_tables(lp)
    y = x_prompt.reshape(bp * lp, D_MODEL)
    per_layer = []
    for l in range(DEPTH):
        mod = mods[l, 0:1].reshape(1, 3, D_MODEL)
        y, new = _layer(y, mod, lws[l], consts, bp, lp, None, None, hy_p)
        per_layer.append(new)
    y_prompt = y.reshape(bp, lp, D_MODEL)
    news = [jnp.stack([s[i] for s in per_layer], axis=1) for i in range(5)]

    hy_s = _hy_tables(ls)
    rope_tabs = _rope_tables(ls)
    y = x_sample.reshape(bs * ls, D_MODEL)
    for l in range(DEPTH):
        mod = mods[l, 1:1 + bs].reshape(bs, 3, D_MODEL)
        ctx = (cache_mla_ckv[:, l], cache_mla_krope[:, l], cache_diff_k[:, l], cache_diff_v[:, l], state_hgrn[:, l])
        y, _ = _layer(y, mod, lws[l], consts, bs, ls, ctx, rope_tabs, hy_s)
    y_sample = y.reshape(bs, ls, D_MODEL)

    return (y_prompt, y_sample, news[0], news[1], news[2], news[3], news[4])
```

```python
import functools
import math

import numpy as np
import jax
import jax.numpy as jnp
from jax import lax
from jax.experimental import pallas as pl
from jax.experimental.pallas import tpu as pltpu

F32 = jnp.float32
BF16 = jnp.bfloat16

D_MODEL = 1024
DEPTH = 4
GRID_W = 64
ROPE_BASE = 10000.0
EPS = 1e-6
BRANCH = 256
MLA_HEADS = 4
MLA_NOPE = 64
MLA_ROPE = 32
MLA_V = 64
MLA_Q_LORA = 256
MLA_KV_LORA = 128
MLA_SCALE = (MLA_NOPE + MLA_ROPE) ** -0.5
DIFF_HEADS = 4
DIFF_HD = 32
DIFF_SCALE = DIFF_HD ** -0.5
HGRN_HEADS = 4
HGRN_DK = 64
HGRN_CHUNK = 128
HGRN_LEVELS = 7
HGRN_MM_LEVELS = 3
HY_CH = 256
HY_ORDER = 2
HY_EMB = 33
HY_BANDS = 16
HY_FH = 64
HY_DECAY_TARGET = 0.01
HY_FAST_DECAY = 0.3
HY_SLOW_DECAY = 1.5
IN_COLS = 4000

LANE = 128
LOG2E = math.log2(math.e)
VT_ROWS = 80
FFT_N2 = 128
FFT_BLOCK_BYTES = 2 * 1024 * 1024
MLA_AHEAD = 8
DIFF_AHEAD = 3
MLA_SUB = 256
DIFF_SUB = 512
VMEM_LIMIT = 52 * 1024 * 1024

P_CQ, P_CKV, P_KR, P_DQ, P_DK, P_DV = 0, 256, 384, 512, 768, 1024
P_HQ, P_HZF, P_HZB, P_HI, P_HU, P_GATE = 1280, 1536, 1792, 2048, 2304, 3072
P_COLS = 4096
KR_OFF = 64


def _in_col_perm():
    src = np.full((P_COLS,), IN_COLS, np.int32)

    def put(dst, lo, n):
        src[dst:dst + n] = np.arange(lo, lo + n)

    put(P_CQ, 0, 256)
    put(P_CKV, 256, 128)
    put(P_KR + KR_OFF, 384, 32)
    put(P_GATE, 416, 256)
    put(P_DQ, 672, 256)
    put(P_DK, 928, 256)
    put(P_DV, 1184, 256)
    put(P_GATE + 256, 1440, 256)
    put(P_HQ, 1696, 256)
    put(P_HZF, 1952, 256)
    put(P_HZB, 2208, 256)
    put(P_HI, 2464, 256)
    put(P_GATE + 512, 2720, 256)
    put(P_HU, 2976, 768)
    put(P_GATE + 768, 3744, 256)
    return src


def _reorder_in_cols(w):
    src = _in_col_perm()
    pieces, lo = [], 0
    while lo < P_COLS:
        hi = lo + 1
        if src[lo] == IN_COLS:
            while hi < P_COLS and src[hi] == IN_COLS:
                hi += 1
            pieces.append(jnp.zeros(w.shape[:-1] + (hi - lo,), w.dtype))
        else:
            while hi < P_COLS and src[hi] == src[hi - 1] + 1:
                hi += 1
            pieces.append(w[..., int(src[lo]):int(src[lo]) + hi - lo])
        lo = hi
    return jnp.concatenate(pieces, axis=-1)


def _cparams(sem):
    return pltpu.CompilerParams(dimension_semantics=sem, vmem_limit_bytes=VMEM_LIMIT)


def _bdot(a, b):
    return jnp.dot(a.astype(BF16), b.astype(BF16), preferred_element_type=F32)


def _nt(a, b):
    return lax.dot_general(a.astype(BF16), b.astype(BF16), (((1,), (1,)), ((), ())), preferred_element_type=F32)


def _split2(a):
    hi = a.astype(BF16)
    lo = (a - hi.astype(F32)).astype(BF16)
    return hi, lo


def _dot3(a, b):
    ah, al = _split2(a)
    bh, bl = _split2(b)
    d = functools.partial(jnp.dot, preferred_element_type=F32)
    return d(ah, bh) + d(ah, bl) + d(al, bh)


def _segsum(v, seg):
    hi, lo = _split2(v)
    d = functools.partial(jnp.dot, preferred_element_type=F32)
    return d(hi, seg) + d(lo, seg)


def _rms(x, g):
    return x * lax.rsqrt(jnp.mean(x * x, axis=-1, keepdims=True) + EPS) * g


def _swap8(x):
    w = x.shape[-1]
    lane = lax.broadcasted_iota(jnp.int32, x.shape, x.ndim - 1)
    up = pltpu.roll(x, w - 8, x.ndim - 1)
    dn = pltpu.roll(x, 8, x.ndim - 1)
    return jnp.where((lane & 15) < 8, up, dn)


def _tile_lanes(x, n):
    return x if n == 1 else jnp.concatenate([x] * n, axis=-1)


def _mod_kernel(c_ref, w_ref, b_ref, o_ref):
    c = c_ref[...]
    o_ref[0] = _dot3(c * jax.nn.sigmoid(c), w_ref[0]) + b_ref[0]


def _mod_all(cvecs, w_mod, b_mod):
    nt = 3
    return pl.pallas_call(
        _mod_kernel,
        grid=(DEPTH, nt),
        in_specs=[pl.BlockSpec((8, D_MODEL), lambda l, j: (0, 0)),
                  pl.BlockSpec((1, D_MODEL, D_MODEL), lambda l, j: (l, 0, j)),
                  pl.BlockSpec((1, 1, D_MODEL), lambda l, j: (l, 0, j))],
        out_specs=pl.BlockSpec((1, 8, D_MODEL), lambda l, j: (l, 0, j)),
        out_shape=jax.ShapeDtypeStruct((DEPTH, 8, 3 * D_MODEL), F32),
        compiler_params=_cparams(("arbitrary", "arbitrary")),
        name="mod",
    )(cvecs, w_mod, b_mod.reshape(DEPTH, 1, 3 * D_MODEL))


def _lb_kernel(x_ref, o_ref):
    x = x_ref[...]
    rows = [x[l:l + 1, :] for l in range(DEPTH)]
    m = functools.reduce(jnp.maximum, rows)
    e = [jnp.exp(r - m) for r in rows]
    tot = functools.reduce(lambda a, b: a + b, e)
    acc = jnp.zeros_like(tot)
    o_ref[0:1, :] = acc
    for l in range(1, DEPTH):
        acc = acc + e[l] / tot
        o_ref[l:l + 1, :] = acc


def _hgrn_lb(logits):
    flat = logits.transpose(1, 0, 2).reshape(DEPTH, 2 * BRANCH)
    lb = pl.pallas_call(
        _lb_kernel,
        out_shape=jax.ShapeDtypeStruct(flat.shape, F32),
        name="hgrn_lb",
    )(flat)
    return lb.reshape(DEPTH, 2, BRANCH).transpose(1, 0, 2)


def _inproj_kernel(x_ref, mod_ref, g_ref, w_ref, p_ref):
    h = _rms(x_ref[...], g_ref[...]) * (1.0 + mod_ref[0, 1:2, :]) + mod_ref[0, 0:1, :]
    p_ref[...] = jnp.dot(h.astype(BF16), w_ref[...], preferred_element_type=F32)


def _inproj(x2, mod, norm_g, w_in_p, seq_len):
    n = x2.shape[0]
    tm = 256
    per_batch = mod.shape[0] > 1
    tiles_per_seq = seq_len // tm
    mod_idx = (lambda i: (i // tiles_per_seq, 0, 0)) if per_batch else (lambda i: (0, 0, 0))
    return pl.pallas_call(
        _inproj_kernel,
        grid=(n // tm,),
        in_specs=[pl.BlockSpec((tm, D_MODEL), lambda i: (i, 0)),
                  pl.BlockSpec((1, 3, D_MODEL), mod_idx),
                  pl.BlockSpec((1, D_MODEL), lambda i: (0, 0)),
                  pl.BlockSpec((D_MODEL, P_COLS), lambda i: (0, 0))],
        out_specs=pl.BlockSpec((tm, P_COLS), lambda i: (i, 0)),
        out_shape=jax.ShapeDtypeStruct((n, P_COLS), F32),
        compiler_params=_cparams(("parallel",)),
        name="inproj",
    )(x2, mod, norm_g.reshape(1, D_MODEL), w_in_p)


def _mla_seg():
    sid = np.zeros((512,), np.int32)
    cnt = np.ones((512,), np.float32)
    for h in range(MLA_HEADS):
        b = 128 * h
        sid[b:b + 64] = 3 * h
        sid[b + 64:b + 96] = 3 * h + 1
        sid[b + 96:b + 128] = 3 * h + 2
        cnt[b:b + 64] = 1.0 / 64
        cnt[b + 64:b + 128] = 1.0 / 32
    seg = (sid[:, None] == sid[None, :]).astype(np.float32)
    return jnp.asarray(seg, BF16), jnp.asarray(cnt.reshape(1, 512))


def _mla_q_kernel(rope, cq_ref, ckv_ref, kr_ref, qng_ref, wuq_ref, kvg_ref, gq_ref, gkr_ref, seg_ref, cnt_ref,
                  *rest):
    if rope:
        cos_ref, sin_ref, q_ref, ckvn_ref, krp_ref = rest
    else:
        q_ref, ckvn_ref, krp_ref = rest
    cqn = _rms(cq_ref[...], qng_ref[...])
    q = _bdot(cqn, wuq_ref[...])
    ss = _segsum(q * q, seg_ref[...]) * cnt_ref[...]
    qn = q * lax.rsqrt(ss + EPS) * gq_ref[...]
    ckvn_ref[...] = _rms(ckv_ref[...], kvg_ref[...])
    kr = kr_ref[...]
    krn = kr * lax.rsqrt(jnp.sum(kr * kr, axis=-1, keepdims=True) * (1.0 / MLA_ROPE) + EPS) * gkr_ref[...]
    if rope:
        cos, sin = cos_ref[...], sin_ref[...]
        qn = qn * _tile_lanes(cos, MLA_HEADS) + _swap8(qn) * _tile_lanes(sin, MLA_HEADS)
        krn = krn * cos + _swap8(krn) * sin
    q_ref[...] = (qn * (MLA_SCALE * LOG2E)).astype(BF16)
    krp_ref[...] = krn


def _mla_q(p, lw, consts, seq_len, rope_tabs):
    n = p.shape[0]
    tm = 256
    rope = rope_tabs is not None
    full = lambda shape: pl.BlockSpec(shape, lambda i: (0,) * len(shape))
    in_specs = [pl.BlockSpec((tm, 256), lambda i: (i, P_CQ // 256)),
                pl.BlockSpec((tm, 128), lambda i: (i, P_CKV // 128)),
                pl.BlockSpec((tm, 128), lambda i: (i, P_KR // 128)),
                full((1, 256)), full((256, 512)), full((1, 128)), full((1, 512)), full((1, 128)),
                full((512, 512)), full((1, 512))]
    args = [p, p, p, lw["qn_g"], lw["w_uq"], lw["kvn_g"], lw["gq"], lw["gkr"], consts["seg512"], consts["cnt512"]]
    if rope:
        tps = seq_len // tm
        in_specs += [pl.BlockSpec((tm, 128), lambda i: (i % tps, 0))] * 2
        args += [rope_tabs["cos_mla"], rope_tabs["sin_mla"]]
    return pl.pallas_call(
        functools.partial(_mla_q_kernel, rope),
        grid=(n // tm,),
        in_specs=in_specs,
        out_specs=[pl.BlockSpec((tm, 512), lambda i: (i, 0)),
                   pl.BlockSpec((tm, 128), lambda i: (i, 0)),
                   pl.BlockSpec((tm, 128), lambda i: (i, 0))],
        out_shape=[jax.ShapeDtypeStruct((n, 512), BF16),
                   jax.ShapeDtypeStruct((n, 128), F32),
                   jax.ShapeDtypeStruct((n, 128), F32)],
        compiler_params=_cparams(("parallel",)),
        name="mla_q",
    )(*args)


def _mla_kv_kernel(ckvn_ref, krp_ref, wuk_ref, wuv_ref, gk_ref, seg_ref, cnt_ref, k_ref, v_ref):
    c = ckvn_ref[...].astype(BF16)
    kn = jnp.dot(c, wuk_ref[...], preferred_element_type=F32)
    ss = _segsum(kn * kn, seg_ref[...]) * cnt_ref[...]
    k = kn * lax.rsqrt(ss + EPS) * gk_ref[...] + _tile_lanes(krp_ref[...], MLA_HEADS)
    k_ref[...] = k.astype(BF16)
    v_ref[...] = jnp.dot(c, wuv_ref[...], preferred_element_type=F32).astype(BF16)


def _mla_kv(ckvn, krp, lw, consts):
    n = ckvn.shape[0]
    tm = 512
    full = lambda shape: pl.BlockSpec(shape, lambda i: (0,) * len(shape))
    return pl.pallas_call(
        _mla_kv_kernel,
        grid=(n // tm,),
        in_specs=[pl.BlockSpec((tm, 128), lambda i: (i, 0)), pl.BlockSpec((tm, 128), lambda i: (i, 0)),
                  full((128, 512)), full((128, 256)), full((1, 512)), full((512, 512)), full((1, 512))],
        out_specs=[pl.BlockSpec((tm, 512), lambda i: (i, 0)), pl.BlockSpec((tm, 256), lambda i: (i, 0))],
        out_shape=[jax.ShapeDtypeStruct((n, 512), BF16), jax.ShapeDtypeStruct((n, 256), BF16)],
        compiler_params=_cparams(("parallel",)),
        name="mla_kv",
    )(ckvn, krp, lw["w_uk"], lw["w_uv"], lw["gk"], consts["seg512"], consts["cnt512"])


def _softmax_pv(qs, k_ref, vt_ref, key_chunk, sub_rows, n_ahead):
    lk = k_ref.shape[1]
    nch = lk // key_chunk
    nq = len(qs)
    sub = min(sub_rows, key_chunk)
    nsub = key_chunk // sub

    def scores(c, u):
        lo = c * key_chunk + u * sub
        ks = k_ref[0, lo:lo + sub, :]
        return [_nt(ks, q) for q in qs]

    def chunk_max(s_chunk, j):
        mc = functools.reduce(jnp.maximum, [s_chunk[u][j] for u in range(nsub)])
        return jnp.max(mc, axis=0, keepdims=True)

    s_buf = {c: [scores(c, u) for u in range(nsub)] for c in range(min(n_ahead, nch))}
    m = [None] * nq
    acc = [None] * nq
    m_new = [chunk_max(s_buf[0], j) for j in range(nq)]
    for c in range(nch):
        s_cur = s_buf.pop(c)
        ahead = c + n_ahead
        if ahead < nch:
            s_buf[ahead] = []
        pv = [None] * nq
        for u in range(nsub):
            if ahead < nch:
                s_buf[ahead].append(scores(ahead, u))
            lo = c * key_chunk + u * sub
            vs = vt_ref[0, 0, :, lo:lo + sub]
            for j in range(nq):
                part = jnp.dot(vs, jnp.exp2(s_cur[u][j] - m_new[j]).astype(BF16), preferred_element_type=F32)
                pv[j] = part if pv[j] is None else pv[j] + part
        for j in range(nq):
            acc[j] = pv[j] if c == 0 else acc[j] * jnp.exp2(m[j] - m_new[j]) + pv[j]
            m[j] = m_new[j]
        if c + 1 < nch:
            m_new = [jnp.maximum(m[j], chunk_max(s_buf[c + 1], j)) for j in range(nq)]
    return acc


def _mla_attn_kernel(key_chunk, q_ref, k_ref, vt_ref, o_ref):
    (acc,) = _softmax_pv([q_ref[...]], k_ref, vt_ref, key_chunk, MLA_SUB, MLA_AHEAD)
    o_ref[0] = acc[0:MLA_V] / acc[MLA_V:MLA_V + 1]


def _key_chunk(lk):
    return 512 if lk % 512 == 0 else lk


def _vt_kernel(v_ref, o_ref):
    vt = v_ref[0].astype(F32).T
    tm = vt.shape[1]
    row = lax.broadcasted_iota(jnp.int32, (VT_ROWS - 64, tm), 0)
    extra = jnp.where(row == 0, 1.0, 0.0).astype(BF16)
    for h in range(4):
        o_ref[0, h, 0:64, :] = vt[64 * h:64 * (h + 1)].astype(BF16)
        o_ref[0, h, 64:VT_ROWS, :] = extra


def _vt_with_ones(v3):
    b, lk, width = v3.shape
    tm = 512 if lk % 512 == 0 else lk
    return pl.pallas_call(
        _vt_kernel,
        grid=(b, lk // tm),
        in_specs=[pl.BlockSpec((1, tm, width), lambda i, j: (i, j, 0))],
        out_specs=pl.BlockSpec((1, 4, VT_ROWS, tm), lambda i, j: (i, 0, 0, j)),
        out_shape=jax.ShapeDtypeStruct((b, 4, VT_ROWS, lk), BF16),
        compiler_params=_cparams(("parallel", "parallel")),
        name="vt_ones",
    )(v3)


def _mla_attn(q, k, vt, batch, seq_len):
    lk = k.shape[1]
    tq = 256
    nq = seq_len // tq
    return pl.pallas_call(
        functools.partial(_mla_attn_kernel, _key_chunk(lk)),
        grid=(batch, MLA_HEADS, nq),
        in_specs=[pl.BlockSpec((tq, 128), lambda b, h, i: (b * nq + i, h)),
                  pl.BlockSpec((1, lk, 128), lambda b, h, i: (b, 0, h)),
                  pl.BlockSpec((1, 1, VT_ROWS, lk), lambda b, h, i: (b, h, 0, 0))],
        out_specs=pl.BlockSpec((1, MLA_V, tq), lambda b, h, i: (b, h, i)),
        out_shape=jax.ShapeDtypeStruct((batch, BRANCH, seq_len), F32),
        compiler_params=_cparams(("parallel", "parallel", "arbitrary")),
        name="mla_attn",
    )(q, k, vt)


def _seg_const(width, seg):
    sid = np.arange(width) // seg
    return jnp.asarray((sid[:, None] == sid[None, :]).astype(np.float32), BF16)


def _diff_prep_kernel(rope, dq_ref, dk_ref, gq_ref, gk_ref, seg_ref, *rest):
    if rope:
        cos_ref, sin_ref, q_ref, k_ref, kf_ref = rest
    else:
        q_ref, k_ref, kf_ref = rest
    seg = seg_ref[...]

    def norm(x, g):
        ss = _segsum(x * x, seg) * (1.0 / DIFF_HD)
        return x * lax.rsqrt(ss + EPS) * g

    q = norm(dq_ref[...], gq_ref[...])
    k = norm(dk_ref[...], gk_ref[...])
    kf_ref[...] = k
    if rope:
        cos, sin = cos_ref[...], sin_ref[...]
        q = q * cos + _swap8(q) * sin
        k = k * cos + _swap8(k) * sin
    q_ref[...] = (q * (DIFF_SCALE * LOG2E)).astype(BF16)
    k_ref[...] = k.astype(BF16)


def _diff_prep(p, lw, consts, seq_len, rope_tabs):
    n = p.shape[0]
    tm = 256
    rope = rope_tabs is not None
    full = lambda shape: pl.BlockSpec(shape, lambda i: (0,) * len(shape))
    in_specs = [pl.BlockSpec((tm, 256), lambda i: (i, P_DQ // 256)),
                pl.BlockSpec((tm, 256), lambda i: (i, P_DK // 256)),
                full((1, 256)), full((1, 256)), full((256, 256))]
    args = [p, p, lw["dgq"], lw["dgk"], consts["seg32"]]
    if rope:
        tps = seq_len // tm
        in_specs += [pl.BlockSpec((tm, 256), lambda i: (i % tps, 0))] * 2
        args += [rope_tabs["cos_diff"], rope_tabs["sin_diff"]]
    blk = pl.BlockSpec((tm, 256), lambda i: (i, 0))
    return pl.pallas_call(
        functools.partial(_diff_prep_kernel, rope),
        grid=(n // tm,),
        in_specs=in_specs,
        out_specs=[blk, blk, blk],
        out_shape=[jax.ShapeDtypeStruct((n, 256), BF16), jax.ShapeDtypeStruct((n, 256), BF16),
                   jax.ShapeDtypeStruct((n, 256), F32)],
        compiler_params=_cparams(("parallel",)),
        name="diff_prep",
    )(*args)


def _diff_attn_kernel(lam_init, key_chunk, q_ref, k_ref, vt_ref, lp_ref, g_ref, o_ref):
    h = pl.program_id(1)
    q = q_ref[...]
    lane = lax.broadcasted_iota(jnp.int32, q.shape, 1)
    base = (h % 2) * 64
    zero = jnp.zeros_like(q)

    def map_query(j):
        lo = base + 32 * j
        return jnp.where((lane >= lo) & (lane < lo + 32), q, zero)

    acc0, acc1 = _softmax_pv([map_query(0), map_query(1)], k_ref, vt_ref, key_chunk, DIFF_SUB, DIFF_AHEAD)
    lp = lp_ref[...]
    lam = (jnp.exp(jnp.sum(lp[0:1] * lp[1:2], axis=1, keepdims=True))
           - jnp.exp(jnp.sum(lp[2:3] * lp[3:4], axis=1, keepdims=True)) + lam_init)
    o = acc0[0:64] / acc0[64:65] - lam * (acc1[0:64] / acc1[64:65])
    ms = jnp.mean(o * o, axis=0, keepdims=True)
    o_ref[0] = o * lax.rsqrt(ms + EPS) * g_ref[...] * (1.0 - lam_init)


def _diff_attn(q, k, vt, lp, g_col, lam_init, batch, seq_len):
    lk = k.shape[1]
    tq = 256
    nq = seq_len // tq
    return pl.pallas_call(
        functools.partial(_diff_attn_kernel, lam_init, _key_chunk(lk)),
        grid=(batch, DIFF_HEADS, nq),
        in_specs=[pl.BlockSpec((tq, 128), lambda b, h, i: (b * nq + i, h // 2)),
                  pl.BlockSpec((1, lk, 128), lambda b, h, i: (b, 0, h // 2)),
                  pl.BlockSpec((1, 1, VT_ROWS, lk), lambda b, h, i: (b, h, 0, 0)),
                  pl.BlockSpec((4, DIFF_HD), lambda b, h, i: (0, 0)),
                  pl.BlockSpec((64, 1), lambda b, h, i: (0, 0))],
        out_specs=pl.BlockSpec((1, 64, tq), lambda b, h, i: (b, h, i)),
        out_shape=jax.ShapeDtypeStruct((batch, BRANCH, seq_len), F32),
        compiler_params=_cparams(("parallel", "parallel", "arbitrary")),
        name="diff_attn",
    )(q, k, vt, lp, g_col)


def _hgrn_consts():
    c = HGRN_CHUNK
    t = np.arange(c)
    low = (t[None, :] <= t[:, None]).astype(np.float32)
    blocks = []
    for j in range(HGRN_MM_LEVELS):
        m = 1 << j
        rho = (t // (2 * m)) * (2 * m) + m - 1
        sign = np.where((t // m) % 2 == 1, 1.0, -1.0)[:, None]
        blocks.append(sign * (low - (t[None, :] <= rho[:, None]).astype(np.float32)))
    blocks.append(low)
    fwd = np.concatenate(blocks, axis=0)
    bwd = np.concatenate([b[::-1, ::-1] for b in blocks], axis=0)
    return jnp.asarray(np.stack([fwd, bwd]), BF16)


def _hgrn_kernel(nc, qf_ref, zf_ref, vf_ref, qb_ref, zb_ref, vb_ref, lb_ref, dd_ref, s0_ref,
                 of_ref, ob_ref, sout_ref, st_ref):
    c = HGRN_CHUNK
    ci = pl.program_id(1)
    dirs = (0, 1)

    @pl.when(ci == 0)
    def _():
        st_ref[...] = s0_ref[0]

    lane = lax.broadcasted_iota(jnp.int32, (1, BRANCH), 1)
    head_masks = [(lane >= HGRN_DK * h) & (lane < HGRN_DK * (h + 1)) for h in range(HGRN_HEADS)]
    t_idx = lax.broadcasted_iota(jnp.int32, (c, HGRN_HEADS * c), 0)
    s_idx = lax.broadcasted_iota(jnp.int32, (c, HGRN_HEADS * c), 1) & (c - 1)
    pair_xor = t_idx ^ s_idx
    row = lax.broadcasted_iota(jnp.int32, (c, 1), 0)
    row_dir = (row, c - 1 - row)

    def stack_heads(x):
        xb = x.astype(BF16)
        zero = jnp.zeros_like(xb)
        return jnp.concatenate([jnp.where(hm, xb, zero) for hm in head_masks], axis=0)

    q = (qf_ref[...], qb_ref[...])
    v = (vf_ref[...], vb_ref[...])
    z = (zf_ref[...], zb_ref[...])
    lb = (lb_ref[0], lb_ref[1])
    g = [jnp.log(lb[d] + (1.0 - lb[d]) * jax.nn.sigmoid(z[d])) for d in dirs]
    kk = [(1.0 - lb[d]) * jax.nn.sigmoid(-z[d]) for d in dirs]
    sums = []
    for d in dirs:
        gh, gl = _split2(g[d])
        sums.append(jnp.dot(dd_ref[d], gh, preferred_element_type=F32)
                    + jnp.dot(dd_ref[d], gl, preferred_element_type=F32))
    b = [sums[d][HGRN_MM_LEVELS * c:] for d in dirs]
    b_tot = (b[0][c - 1:c], b[1][0:1])

    def neg_abs_decay(d, j, right):
        if j < HGRN_MM_LEVELS:
            return sums[d][j * c:(j + 1) * c]
        m = 1 << j
        off = m - 1 if d == 0 else m
        ref = jnp.concatenate([jnp.broadcast_to(b[d][g0 + off:g0 + off + 1], (2 * m, BRANCH))
                               for g0 in range(0, c, 2 * m)], axis=0)
        return jnp.where(right, b[d] - ref, ref - b[d])

    a = [None, None]
    for j in reversed(range(HGRN_LEVELS)):
        same_group = pair_xor < (2 << j)
        for d in dirs:
            right = ((row_dir[d] >> j) & 1) == 1
            e = jnp.exp(neg_abs_decay(d, j, right))
            qt = jnp.where(right, q[d] * e, 0.0)
            kt = jnp.where(right, 0.0, kk[d] * e)
            lvl = _nt(qt, stack_heads(kt))
            a[d] = lvl if a[d] is None else jnp.where(same_group, lvl, a[d])
    diagonal = pair_xor == 0
    for d in dirs:
        a[d] = jnp.where(diagonal, _nt(q[d], stack_heads(kk[d])), a[d])

    outs = (of_ref, ob_ref)
    for d in dirs:
        o = jnp.dot(a[d].astype(BF16), stack_heads(v[d]), preferred_element_type=F32)
        outs[d][...] = o + _nt(q[d] * jnp.exp(b[d]), st_ref[d])

    r2 = lax.broadcasted_iota(jnp.int32, (BRANCH, BRANCH), 0) // HGRN_DK
    c2 = lax.broadcasted_iota(jnp.int32, (BRANCH, BRANCH), 1) // HGRN_DK
    for d in dirs:
        kd = kk[d] * jnp.exp(b_tot[d] - b[d])
        upd = lax.dot_general(v[d].astype(BF16), kd.astype(BF16), (((0,), (0,)), ((), ())),
                              preferred_element_type=F32)
        st_new = st_ref[d] * jnp.exp(b_tot[d]) + jnp.where(r2 == c2, upd, 0.0)
        st_ref[d] = st_new

        @pl.when(ci == nc - 1)
        def _():
            sout_ref[0, d] = st_new


def _hgrn(p, lb_l, dd, st0, batch, seq_len):
    n = p.shape[0]
    c = HGRN_CHUNK
    nc = seq_len // c
    fwd = lambda col: pl.BlockSpec((c, 256), lambda b, i: (b * nc + i, col))
    bwd = lambda col: pl.BlockSpec((c, 256), lambda b, i: (b * nc + nc - 1 - i, col))
    whole = lambda shape: pl.BlockSpec(shape, lambda b, i: (0,) * len(shape))
    return pl.pallas_call(
        functools.partial(_hgrn_kernel, nc),
        grid=(batch, nc),
        in_specs=[fwd(P_HQ // 256), fwd(P_HZF // 256), fwd(P_HI // 256),
                  bwd(P_HQ // 256), bwd(P_HZB // 256), bwd(P_HI // 256),
                  whole((2, 1, 256)), whole((2, (HGRN_MM_LEVELS + 1) * c, c)),
                  pl.BlockSpec((1, 2, 256, 256), lambda b, i: (b, 0, 0, 0))],
        out_specs=[pl.BlockSpec((c, 256), lambda b, i: (b * nc + i, 0)),
                   pl.BlockSpec((c, 256), lambda b, i: (b * nc + nc - 1 - i, 0)),
                   pl.BlockSpec((1, 2, 256, 256), lambda b, i: (b, 0, 0, 0))],
        out_shape=[jax.ShapeDtypeStruct((n, 256), F32), jax.ShapeDtypeStruct((n, 256), F32),
                   jax.ShapeDtypeStruct((batch, 2, 256, 256), F32)],
        scratch_shapes=[pltpu.VMEM((2, 256, 256), F32)],
        compiler_params=_cparams(("parallel", "arbitrary")),
        name="hgrn",
    )(p, p, p, p, p, p, lb_l, dd, st0)


def _hy_conv3_kernel(tiles_per_seq, above_ref, cur_ref, below_ref, w_ref, b_ref, v_ref, x1_ref, x2_ref):
    i = pl.program_id(0)
    cur = cur_ref[...]
    tm = cur.shape[0]
    first = (i % tiles_per_seq) == 0
    last = (i % tiles_per_seq) == tiles_per_seq - 1
    above = jnp.where(first, 0.0, above_ref[7:8, :])
    below = jnp.where(last, 0.0, below_ref[0:1, :])
    row = lax.broadcasted_iota(jnp.int32, (tm, 1), 0)
    prev = jnp.where(row == 0, above, pltpu.roll(cur, 1, 0))
    nxt = jnp.where(row == tm - 1, below, pltpu.roll(cur, tm - 1, 0))
    w = w_ref[...]
    u = prev * w[0:1] + cur * w[1:2] + nxt * w[2:3] + b_ref[...]
    v_ref[...] = u[:, 0:256]
    x1_ref[...] = u[:, 256:512]
    x2_ref[...] = u[:, 512:768]


def _hy_conv3(p, w, b, seq_len):
    n = p.shape[0]
    tm = min(512, seq_len)
    nt = n // tm
    g = tm // 8
    col = P_HU // 768
    oblk = pl.BlockSpec((tm, 256), lambda i: (i, 0))
    return pl.pallas_call(
        functools.partial(_hy_conv3_kernel, seq_len // tm),
        grid=(nt,),
        in_specs=[pl.BlockSpec((8, 768), lambda i: (jnp.maximum(i * g - 1, 0), col)),
                  pl.BlockSpec((tm, 768), lambda i: (i, col)),
                  pl.BlockSpec((8, 768), lambda i: (jnp.minimum((i + 1) * g, nt * g - 1), col)),
                  pl.BlockSpec((3, 768), lambda i: (0, 0)), pl.BlockSpec((1, 768), lambda i: (0, 0))],
        out_specs=[oblk, oblk, oblk],
        out_shape=[jax.ShapeDtypeStruct((n, 256), F32)] * 3,
        compiler_params=_cparams(("parallel",)),
        name="hy_conv3",
    )(p, p, p, w, b)


def _hy_filter_kernel(feat_ref, w1_ref, b1_ref, w2_ref, b2_ref, w3_ref, fr_ref, win_ref, o_ref):
    fr = fr_ref[...]
    h = jnp.sin(fr[0:1] * (_dot3(feat_ref[...], w1_ref[...]) + b1_ref[...]))
    h = jnp.sin(fr[1:2] * (_dot3(h, w2_ref[...]) + b2_ref[...]))
    o_ref[...] = _dot3(h, w3_ref[0]) * _tile_lanes(win_ref[...], HY_ORDER)


def _hy_filter(feats2, window2, w1p, b1, w2, b2, w3d, freq):
    l2 = feats2.shape[0]
    ln = l2 // 2
    tm = min(512, ln)
    full = lambda shape: pl.BlockSpec(shape, lambda i: (0,) * len(shape))
    return pl.pallas_call(
        _hy_filter_kernel,
        grid=(l2 // tm,),
        in_specs=[pl.BlockSpec((tm, LANE), lambda i: (i, 0)),
                  full((LANE, HY_FH)), full((1, HY_FH)), full((HY_FH, HY_FH)), full((1, HY_FH)),
                  pl.BlockSpec((1, HY_FH, HY_ORDER * HY_CH), lambda i: (i // (ln // tm), 0, 0)), full((2, HY_FH)),
                  pl.BlockSpec((tm, HY_CH), lambda i: (i, 0))],
        out_specs=pl.BlockSpec((tm, HY_ORDER * HY_CH), lambda i: (i, 0)),
        out_shape=jax.ShapeDtypeStruct((l2, HY_ORDER * HY_CH), F32),
        compiler_params=_cparams(("parallel",)),
        name="hy_filter",
    )(feats2, w1p, b1, w2, b2, w3d, freq, window2)


def _fft_blocking(nb, n1, n2, ch):
    per_batch = n1 * n2 * ch * 4
    if per_batch <= FFT_BLOCK_BYTES:
        bb = max(1, min(nb, FFT_BLOCK_BYTES // per_batch))
        while nb % bb:
            bb -= 1
        return bb, n2
    rt = n2
    while n1 * rt * ch * 4 > FFT_BLOCK_BYTES and rt > 8:
        rt //= 2
    return 1, rt


def _fft_a_kernel(f_ref, x_ref, o_ref):
    f = f_ref[...]
    for b in range(x_ref.shape[0]):
        x = x_ref[b].astype(BF16)
        o_ref[b] = jnp.einsum("kn,nrc->krc", f, x, preferred_element_type=F32).astype(o_ref.dtype)


def _fft_a(fa, x4, out_dtype):
    nb, n1, n2, ch = x4.shape
    r = fa.shape[0]
    bb, rt = _fft_blocking(nb, n1, n2, ch)
    return pl.pallas_call(
        _fft_a_kernel,
        grid=(nb // bb, n2 // rt),
        in_specs=[pl.BlockSpec((r, n1), lambda b, i: (0, 0)),
                  pl.BlockSpec((bb, n1, rt, ch), lambda b, i: (b, 0, i, 0))],
        out_specs=pl.BlockSpec((bb, r, rt, ch), lambda b, i: (b, 0, i, 0)),
        out_shape=jax.ShapeDtypeStruct((nb, r, n2, ch), out_dtype),
        compiler_params=_cparams(("parallel", "parallel")),
        name="fft_a",
    )(fa, x4)


def _fft_b_kernel(with_inverse, mf_ref, *rest):
    if with_inverse:
        mi_ref, a_ref, h_ref, o_ref = rest
    else:
        a_ref, o_ref = rest
    half = FFT_N2
    for b in range(a_ref.shape[0]):
        a = jnp.concatenate([a_ref[b, 0, 0], a_ref[b, 1, 0]], axis=0)
        x = jnp.dot(mf_ref[0], a.astype(BF16), preferred_element_type=F32)
        if with_inverse:
            xr, xi = x[:half], x[half:]
            hr, hi = h_ref[0, 0], h_ref[1, 0]
            y = jnp.concatenate([xr * hr - xi * hi, xr * hi + xi * hr], axis=0)
            x = jnp.dot(mi_ref[0], y.astype(BF16), preferred_element_type=F32)
        o_ref[b, 0, 0] = x[:half].astype(o_ref.dtype)
        o_ref[b, 1, 0] = x[half:].astype(o_ref.dtype)


def _fft_b(mf, mi, a5, spec, order):
    nb, _, k1n, n2, ch = a5.shape
    mat = pl.BlockSpec((1, 2 * n2, 2 * n2), lambda k: (k, 0, 0))
    blk = pl.BlockSpec((nb, 2, 1, n2, ch), lambda k: (0, 0, k, 0, 0))
    if spec is None:
        in_specs, args = [mat, blk], [mf, a5]
    else:
        in_specs = [mat, mat, blk, pl.BlockSpec((2, 1, n2, ch), lambda k: (0, k, 0, order))]
        args = [mf, mi, a5, spec]
    return pl.pallas_call(
        functools.partial(_fft_b_kernel, spec is not None),
        grid=(k1n,),
        in_specs=in_specs,
        out_specs=blk,
        out_shape=jax.ShapeDtypeStruct(a5.shape, F32 if spec is None else BF16),
        compiler_params=_cparams(("parallel",)),
        name="fft_b",
    )(*args)


def _fft_a_inv_kernel(g_ref, p_ref, x_ref, z_ref, bias_ref, o_ref):
    g = g_ref[...]
    for b in range(p_ref.shape[0]):
        conv = jnp.einsum("nk,krc->nrc", g, p_ref[b], preferred_element_type=F32)
        o_ref[b] = x_ref[b] * (conv + z_ref[b] * bias_ref[...])


def _fft_a_inv(g, p4, xg4, z4, bias):
    nb, n1, n2, ch = z4.shape
    r = g.shape[1]
    bb, rt = _fft_blocking(nb, n1, n2, ch)
    blk = pl.BlockSpec((bb, n1, rt, ch), lambda b, i: (b, 0, i, 0))
    return pl.pallas_call(
        _fft_a_inv_kernel,
        grid=(nb // bb, n2 // rt),
        in_specs=[pl.BlockSpec((n1, r), lambda b, i: (0, 0)),
                  pl.BlockSpec((bb, r, rt, ch), lambda b, i: (b, 0, i, 0)), blk, blk,
                  pl.BlockSpec((1, 1, ch), lambda b, i: (0, 0, 0))],
        out_specs=blk,
        out_shape=jax.ShapeDtypeStruct(z4.shape, F32),
        compiler_params=_cparams(("parallel", "parallel")),
        name="fft_a_inv",
    )(g, p4, xg4, z4, bias.reshape(1, 1, ch))


def _fft_tables(ln):
    n = 2 * ln
    n1t = n // FFT_N2
    k1n = n1t // 2 + 1
    kk = np.arange(k1n)

    def stage_a(n1_in):
        ang = 2.0 * np.pi * ((kk[:, None] * np.arange(n1_in)[None, :]) % n1t) / n1t
        return jnp.asarray(np.concatenate([np.cos(ang), -np.sin(ang)], axis=0), BF16)

    n1o = n1t // 2
    ang = 2.0 * np.pi * ((np.arange(n1o)[:, None] * kk[None, :]) % n1t) / n1t
    edge = (kk == 0) | (kk == n1t // 2)
    ck = np.where(edge, 1.0, 2.0) / n
    g = jnp.asarray(np.concatenate([ck * np.cos(ang), -ck * np.where(edge, 0.0, np.sin(ang))], axis=1), BF16)

    k1 = jnp.arange(k1n, dtype=jnp.int32)[:, None, None]
    k2 = jnp.arange(FFT_N2, dtype=jnp.int32)[None, :, None]
    n2 = jnp.arange(FFT_N2, dtype=jnp.int32)[None, None, :]
    th = (2.0 * math.pi / n) * ((n2 * (k1 + n1t * k2)) % n).astype(F32)
    c, s = jnp.cos(th), jnp.sin(th)
    mf = jnp.concatenate([jnp.concatenate([c, s], axis=2), jnp.concatenate([-s, c], axis=2)], axis=1)
    return dict(fa_half=stage_a(n1o), fa_full=stage_a(n1t), g=g, mf=mf.astype(BF16),
                mi=mf.transpose(0, 2, 1).astype(BF16), k1n=k1n, n1o=n1o, n1t=n1t)


def _hy_static(ln):
    t = jnp.linspace(0.0, 1.0, ln, dtype=F32)[:, None]
    w = 2.0 * math.pi * jnp.arange(ln, dtype=F32) / ln
    f = jnp.linspace(1e-4, HY_BANDS - 1, HY_BANDS, dtype=F32)
    ang = w[:, None] * f[None, :]
    feats = jnp.concatenate([t, jnp.cos(ang), -jnp.sin(ang)], axis=-1)
    feats = jnp.pad(feats, ((0, 0), (0, LANE - HY_EMB)))
    min_decay = math.log(HY_DECAY_TARGET) / HY_SLOW_DECAY
    max_decay = math.log(HY_DECAY_TARGET) / HY_FAST_DECAY
    deltas = jnp.linspace(min_decay, max_decay, HY_CH, dtype=F32)
    window = jnp.exp(-t * jnp.abs(deltas))
    feats = jnp.concatenate([feats, feats[::-1]], axis=0)
    window = jnp.concatenate([window, window[::-1]], axis=0)
    return feats, window


def _outproj_kernel(x_ref, mod_ref, oa_ref, ob_ref, of_ref, obk_ref, od_ref, gate_ref, hg_ref, seg_ref, w_ref,
                    y_ref):
    gt = gate_ref[...]
    sg = gt * jax.nn.sigmoid(gt)
    oc = of_ref[...] + obk_ref[...]
    ss = _segsum(oc * oc, seg_ref[...]) * (1.0 / HGRN_DK)
    oc = oc * lax.rsqrt(ss + EPS) * hg_ref[...]
    acc = _bdot(oa_ref[0].T * sg[:, 0:256], w_ref[0:256, :])
    acc += _bdot(ob_ref[0].T * sg[:, 256:512], w_ref[256:512, :])
    acc += _bdot(oc * sg[:, 512:768], w_ref[512:768, :])
    acc += _bdot(od_ref[...] * sg[:, 768:1024], w_ref[768:1024, :])
    y_ref[...] = x_ref[...] + mod_ref[0, 2:3, :] * acc


def _outproj(x2, mod, ot_a, ot_b, o_f, o_b, out_d, p, hg, seg64, w_out, seq_len):
    n = x2.shape[0]
    tm = 256
    per_batch = mod.shape[0] > 1
    tps = seq_len // tm
    mod_idx = (lambda i: (i // tps, 0, 0)) if per_batch else (lambda i: (0, 0, 0))
    b256 = pl.BlockSpec((tm, 256), lambda i: (i, 0))
    bt = pl.BlockSpec((1, 256, tm), lambda i: (i // tps, 0, i % tps))
    return pl.pallas_call(
        _outproj_kernel,
        grid=(n // tm,),
        in_specs=[pl.BlockSpec((tm, D_MODEL), lambda i: (i, 0)),
                  pl.BlockSpec((1, 3, D_MODEL), mod_idx),
                  bt, bt, b256, b256, b256,
                  pl.BlockSpec((tm, 1024), lambda i: (i, P_GATE // 1024)),
                  pl.BlockSpec((1, 256), lambda i: (0, 0)),
                  pl.BlockSpec((256, 256), lambda i: (0, 0)),
                  pl.BlockSpec((D_MODEL, D_MODEL), lambda i: (0, 0))],
        out_specs=pl.BlockSpec((tm, D_MODEL), lambda i: (i, 0)),
        out_shape=jax.ShapeDtypeStruct((n, D_MODEL), F32),
        compiler_params=_cparams(("parallel",)),
        name="outproj",
    )(x2, mod, ot_a, ot_b, o_f, o_b, out_d, p, hg, seg64, w_out)


def _layer(x2, mod, lw, consts, batch, seq_len, ctx, rope_tabs, hy):
    n = batch * seq_len
    p = _inproj(x2, mod, lw["norm_g"], lw["w_in"], seq_len)

    q_a, ckvn, krp = _mla_q(p, lw, consts, seq_len, rope_tabs)
    ckv3 = ckvn.reshape(batch, seq_len, MLA_KV_LORA)
    krp3 = krp.reshape(batch, seq_len, LANE)
    if ctx is not None:
        cache_kr = jnp.pad(ctx[1], ((0, 0), (0, 0), (KR_OFF, LANE - KR_OFF - MLA_ROPE)))
        ckv_all = jnp.concatenate([ckv3, ctx[0]], axis=1)
        kr_all = jnp.concatenate([krp3, cache_kr], axis=1)
    else:
        ckv_all, kr_all = ckv3, krp3
    lk = ckv_all.shape[1]
    k_a, v_a = _mla_kv(ckv_all.reshape(batch * lk, MLA_KV_LORA), kr_all.reshape(batch * lk, LANE), lw, consts)
    ot_a = _mla_attn(q_a, k_a.reshape(batch, lk, 512), _vt_with_ones(v_a.reshape(batch, lk, BRANCH)), batch, seq_len)

    q_b, k_b, kd = _diff_prep(p, lw, consts, seq_len, rope_tabs)
    dv = p[:, P_DV:P_DV + BRANCH]
    k_b3 = k_b.reshape(batch, seq_len, BRANCH)
    v_b3 = dv.astype(BF16).reshape(batch, seq_len, BRANCH)
    if ctx is not None:
        k_b3 = jnp.concatenate([k_b3, ctx[2].reshape(batch, -1, BRANCH).astype(BF16)], axis=1)
        v_b3 = jnp.concatenate([v_b3, ctx[3].reshape(batch, -1, BRANCH).astype(BF16)], axis=1)
    ot_b = _diff_attn(q_b, k_b3, _vt_with_ones(v_b3), lw["diff_lambda"], lw["subln_col"], lw["lam_init"],
                      batch, seq_len)

    if ctx is not None:
        s0 = ctx[4]
    else:
        s0 = jnp.zeros((batch, 2, HGRN_HEADS, HGRN_DK, HGRN_DK), F32)
    eye = jnp.eye(HGRN_HEADS, dtype=F32)
    st0 = jnp.einsum("bdhke,hg->bdhegk", s0, eye).reshape(batch, 2, BRANCH, BRANCH)
    o_f, o_b, st_out = _hgrn(p, lw["hgrn_lb"], consts["hgrn_dd"], st0, batch, seq_len)
    st5 = st_out.reshape(batch, 2, HGRN_HEADS, HGRN_DK, HGRN_HEADS, HGRN_DK)
    states = jnp.stack([st5[:, :, h, :, h, :] for h in range(HGRN_HEADS)], axis=2).swapaxes(-1, -2)

    v_d, x1, x2g = _hy_conv3(p, lw["hy_conv_w"], lw["hy_conv_b"], seq_len)
    taps = _hy_filter(hy["feats"], hy["window"], lw["hy_w1"], lw["hy_b1"], lw["hy_w2"], lw["hy_b2"], lw["hy_w3"],
                      lw["hy_freq"])
    k1n, n1o, n1t = hy["k1n"], hy["n1o"], hy["n1t"]
    ta = _fft_a(hy["fa_full"], taps.reshape(1, n1t, FFT_N2, HY_ORDER * HY_CH), BF16)
    spec = _fft_b(hy["mf"], None, ta.reshape(1, 2, k1n, FFT_N2, HY_ORDER * HY_CH), None, 0)[0]
    z4 = v_d.reshape(batch, n1o, FFT_N2, HY_CH)
    for o, xg in enumerate((x1, x2g)):
        a = _fft_a(hy["fa_half"], z4, BF16).reshape(batch, 2, k1n, FFT_N2, HY_CH)
        pk = _fft_b(hy["mf"], hy["mi"], a, spec, o).reshape(batch, 2 * k1n, FFT_N2, HY_CH)
        z4 = _fft_a_inv(hy["g"], pk, xg.reshape(batch, n1o, FFT_N2, HY_CH), z4, lw["hy_bias"][o:o + 1])
    out_d = z4.reshape(n, HY_CH)

    y = _outproj(x2, mod, ot_a, ot_b, o_f, o_b, out_d, p, lw["hgrn_out_g"], consts["seg64"], lw["w_out"], seq_len)
    new = None
    if ctx is None:
        new = (ckv3, krp3[:, :, KR_OFF:KR_OFF + MLA_ROPE],
               kd.reshape(batch, seq_len, DIFF_HEADS, 2, DIFF_HD),
               dv.reshape(batch, seq_len, DIFF_HEADS, 2 * DIFF_HD), states)
    return y, new


def _rope_tables(seq_len):
    half = MLA_ROPE // 2
    inv = ROPE_BASE ** (-jnp.arange(0, half, 2, dtype=F32) / half)
    rows = seq_len // GRID_W
    row = jnp.repeat(jnp.arange(rows, dtype=F32), GRID_W)
    col = (jnp.arange(rows * GRID_W) % GRID_W).astype(F32)
    ar, ac = row[:, None] * inv, col[:, None] * inv
    cos32 = jnp.concatenate([jnp.cos(ar), jnp.cos(ar), jnp.cos(ac), jnp.cos(ac)], axis=-1)
    sin32 = jnp.concatenate([-jnp.sin(ar), jnp.sin(ar), -jnp.sin(ac), jnp.sin(ac)], axis=-1)
    pad = ((0, 0), (KR_OFF, LANE - KR_OFF - MLA_ROPE))
    return dict(cos_mla=jnp.pad(cos32, pad, constant_values=1.0), sin_mla=jnp.pad(sin32, pad),
                cos_diff=jnp.tile(cos32, (1, 2 * DIFF_HEADS)), sin_diff=jnp.tile(sin32, (1, 2 * DIFF_HEADS)))


def _hy_tables(seq_len):
    feats, window = _hy_static(seq_len)
    return dict(feats=feats, window=window, **_fft_tables(seq_len))


def _layer_weights(l, w_in_p, lb, W):
    def head_pad(w, width, per):
        k = w.shape[0]
        w = w.reshape(k, MLA_HEADS, per)[:, :, :width]
        return jnp.pad(w, ((0, 0), (0, 0), (0, LANE - width))).reshape(k, MLA_HEADS * LANE)

    w_ukv = W["mla_w_ukv"][l].reshape(MLA_KV_LORA, MLA_HEADS, MLA_NOPE + MLA_V)
    nope_g, rope_g = W["mla_nope_g"][l], W["mla_rope_g"][l]
    zeros32 = jnp.zeros((MLA_ROPE,), F32)
    zeros64 = jnp.zeros((MLA_NOPE,), F32)
    gq = jnp.tile(jnp.concatenate([nope_g[0], rope_g[0], zeros32]), MLA_HEADS).reshape(1, 512)
    gk = jnp.tile(jnp.concatenate([nope_g[1], zeros64]), MLA_HEADS).reshape(1, 512)
    gkr = jnp.concatenate([zeros64, rope_g[1], zeros32]).reshape(1, LANE)
    return dict(
        norm_g=W["norm_g"][l], w_in=w_in_p[l], w_out=W["w_out"][l].astype(BF16),
        qn_g=W["mla_q_norm_g"][l].reshape(1, -1),
        w_uq=head_pad(W["mla_w_uq"][l], MLA_NOPE + MLA_ROPE, MLA_NOPE + MLA_ROPE).astype(BF16),
        kvn_g=W["mla_kv_norm_g"][l].reshape(1, -1),
        w_uk=jnp.pad(w_ukv[:, :, :MLA_NOPE], ((0, 0), (0, 0), (0, LANE - MLA_NOPE))).reshape(MLA_KV_LORA, 512)
        .astype(BF16),
        w_uv=w_ukv[:, :, MLA_NOPE:].reshape(MLA_KV_LORA, BRANCH).astype(BF16),
        gq=gq, gk=gk, gkr=gkr,
        dgq=jnp.tile(W["diff_qk_g"][l, 0], 2 * DIFF_HEADS).reshape(1, BRANCH),
        dgk=jnp.tile(W["diff_qk_g"][l, 1], 2 * DIFF_HEADS).reshape(1, BRANCH),
        diff_lambda=W["diff_lambda"][l], subln_col=W["diff_subln_g"][l].reshape(2 * DIFF_HD, 1),
        lam_init=0.8 - 0.6 * math.exp(-0.3 * l),
        hgrn_lb=lb[:, l].reshape(2, 1, BRANCH),
        hgrn_out_g=jnp.tile(W["hgrn_out_g"][l], HGRN_HEADS).reshape(1, BRANCH),
        hy_conv_w=W["hy_conv_w"][l], hy_conv_b=W["hy_conv_b"][l].reshape(1, -1),
        hy_w1=jnp.pad(W["hy_w1"][l], ((0, LANE - HY_EMB), (0, 0))), hy_b1=W["hy_b1"][l].reshape(1, -1),
        hy_w2=W["hy_w2"][l], hy_b2=W["hy_b2"][l].reshape(1, -1),
        hy_w3=W["hy_w3"][l].reshape(HY_FH, HY_ORDER, 2, HY_CH).transpose(2, 0, 1, 3)
        .reshape(2, HY_FH, HY_ORDER * HY_CH),
        hy_freq=W["hy_sin_freq"][l], hy_bias=W["hy_bias"][l],
    )


def kernel(x_prompt, x_sample, cache_mla_ckv, cache_mla_krope, cache_diff_k, cache_diff_v, state_hgrn, c, c_ctx,
           norm_g, w_mod, b_mod, w_in, w_out, mla_q_norm_g, mla_w_uq, mla_kv_norm_g, mla_w_ukv, mla_nope_g,
           mla_rope_g, diff_qk_g, diff_lambda, diff_subln_g, hgrn_lb_logits, hgrn_out_g, hy_conv_w, hy_conv_b,
           hy_w1, hy_b1, hy_w2, hy_b2, hy_w3, hy_sin_freq, hy_bias):
    W = dict(norm_g=norm_g, w_out=w_out, mla_q_norm_g=mla_q_norm_g, mla_w_uq=mla_w_uq,
             mla_kv_norm_g=mla_kv_norm_g, mla_w_ukv=mla_w_ukv, mla_nope_g=mla_nope_g, mla_rope_g=mla_rope_g,
             diff_qk_g=diff_qk_g, diff_lambda=diff_lambda, diff_subln_g=diff_subln_g, hgrn_out_g=hgrn_out_g,
             hy_conv_w=hy_conv_w, hy_conv_b=hy_conv_b, hy_w1=hy_w1, hy_b1=hy_b1, hy_w2=hy_w2, hy_b2=hy_b2,
             hy_w3=hy_w3, hy_sin_freq=hy_sin_freq, hy_bias=hy_bias)
    bp, lp, _ = x_prompt.shape
    bs, ls, _ = x_sample.shape

    w_in_p = _reorder_in_cols(w_in.astype(BF16))
    cvecs = jnp.concatenate([c_ctx[None, :], c, jnp.zeros((8 - 1 - bs, D_MODEL), F32)], axis=0)
    mods = _mod_all(cvecs, w_mod, b_mod)
    lb = _hgrn_lb(hgrn_lb_logits)
    seg512, cnt512 = _mla_seg()
    consts = dict(seg512=seg512, cnt512=cnt512, seg32=_seg_const(BRANCH, DIFF_HD), seg64=_seg_const(BRANCH, HGRN_DK),
                  hgrn_dd=_hgrn_consts())
    lws = [_layer_weights(l, w_in_p, lb, W) for l in range(DEPTH)]

    hy_p = _hy_tables(lp)
    y = x_prompt.reshape(bp * lp, D_MODEL)
    per_layer = []
    for l in range(DEPTH):
        mod = mods[l, 0:1].reshape(1, 3, D_MODEL)
        y, new = _layer(y, mod, lws[l], consts, bp, lp, None, None, hy_p)
        per_layer.append(new)
    y_prompt = y.reshape(bp, lp, D_MODEL)
    news = [jnp.stack([s[i] for s in per_layer], axis=1) for i in range(5)]

    hy_s = _hy_tables(ls)
    rope_tabs = _rope_tables(ls)
    y = x_sample.reshape(bs * ls, D_MODEL)
    for l in range(DEPTH):
        mod = mods[l, 1:1 + bs].reshape(bs, 3, D_MODEL)
        ctx = (cache_mla_ckv[:, l], cache_mla_krope[:, l], cache_diff_k[:, l], cache_diff_v[:, l], state_hgrn[:, l])
        y, _ = _layer(y, mod, lws[l], consts, bs, ls, ctx, rope_tabs, hy_s)
    y_sample = y.reshape(bs, ls, D_MODEL)

    return (y_prompt, y_sample, news[0], news[1], news[2], news[3], news[4])
```

```python
import functools
import math

import numpy as np
import jax
import jax.numpy as jnp
from jax import lax
from jax.experimental import pallas as pl
from jax.experimental.pallas import tpu as pltpu

F32 = jnp.float32
BF16 = jnp.bfloat16

D_MODEL = 1024
DEPTH = 4
GRID_W = 64
ROPE_BASE = 10000.0
EPS = 1e-6
BRANCH = 256
MLA_HEADS = 4
MLA_NOPE = 64
MLA_ROPE = 32
MLA_V = 64
MLA_Q_LORA = 256
MLA_KV_LORA = 128
MLA_SCALE = (MLA_NOPE + MLA_ROPE) ** -0.5
DIFF_HEADS = 4
DIFF_HD = 32
DIFF_SCALE = DIFF_HD ** -0.5
HGRN_HEADS = 4
HGRN_DK = 64
HGRN_CHUNK = 128
HGRN_LEVELS = 7
HGRN_ROWS = 2
HGRN_MM_LEVELS = 3
HY_CH = 256
HY_ORDER = 2
HY_EMB = 33
HY_BANDS = 16
HY_FH = 64
HY_DECAY_TARGET = 0.01
HY_FAST_DECAY = 0.3
HY_SLOW_DECAY = 1.5
IN_COLS = 4000

LANE = 128
LOG2E = math.log2(math.e)
VT_ROWS = 80
FFT_N2 = 128
FFT_BLOCK_BYTES = 2 * 1024 * 1024
MLA_AHEAD = 8
MLA_CHUNK = 512
DIFF_AHEAD = 6
DIFF_CHUNK = 256
MLA_SUB = 256
DIFF_SUB = 512
VMEM_LIMIT = 52 * 1024 * 1024

P_CQ, P_CKV, P_KR, P_DQ, P_DK, P_DV = 0, 256, 384, 512, 768, 1024
P_HQ, P_HZF, P_HZB, P_HI, P_HU, P_GATE = 1280, 1536, 1792, 2048, 2304, 3072
P_COLS = 4096
KR_OFF = 64


def _in_col_perm():
    src = np.full((P_COLS,), IN_COLS, np.int32)

    def put(dst, lo, n):
        src[dst:dst + n] = np.arange(lo, lo + n)

    put(P_CQ, 0, 256)
    put(P_CKV, 256, 128)
    put(P_KR + KR_OFF, 384, 32)
    put(P_GATE, 416, 256)
    put(P_DQ, 672, 256)
    put(P_DK, 928, 256)
    put(P_DV, 1184, 256)
    put(P_GATE + 256, 1440, 256)
    put(P_HQ, 1696, 256)
    put(P_HZF, 1952, 256)
    put(P_HZB, 2208, 256)
    put(P_HI, 2464, 256)
    put(P_GATE + 512, 2720, 256)
    put(P_HU, 2976, 768)
    put(P_GATE + 768, 3744, 256)
    return src


def _reorder_in_cols(w):
    src = _in_col_perm()
    pieces, lo = [], 0
    while lo < P_COLS:
        hi = lo + 1
        if src[lo] == IN_COLS:
            while hi < P_COLS and src[hi] == IN_COLS:
                hi += 1
            pieces.append(jnp.zeros(w.shape[:-1] + (hi - lo,), w.dtype))
        else:
            while hi < P_COLS and src[hi] == src[hi - 1] + 1:
                hi += 1
            pieces.append(w[..., int(src[lo]):int(src[lo]) + hi - lo])
        lo = hi
    return jnp.concatenate(pieces, axis=-1)


def _cparams(sem):
    return pltpu.CompilerParams(dimension_semantics=sem, vmem_limit_bytes=VMEM_LIMIT)


def _bdot(a, b):
    return jnp.dot(a.astype(BF16), b.astype(BF16), preferred_element_type=F32)


def _nt(a, b):
    return lax.dot_general(a.astype(BF16), b.astype(BF16), (((1,), (1,)), ((), ())), preferred_element_type=F32)


def _split2(a):
    hi = a.astype(BF16)
    lo = (a - hi.astype(F32)).astype(BF16)
    return hi, lo


def _dot3(a, b):
    ah, al = _split2(a)
    bh, bl = _split2(b)
    d = functools.partial(jnp.dot, preferred_element_type=F32)
    return d(ah, bh) + d(ah, bl) + d(al, bh)


def _segsum(v, seg):
    return jnp.dot(v.astype(BF16), seg, preferred_element_type=F32)


def _rms(x, g):
    return x * lax.rsqrt(jnp.mean(x * x, axis=-1, keepdims=True) + EPS) * g


def _swap8(x):
    w = x.shape[-1]
    lane = lax.broadcasted_iota(jnp.int32, x.shape, x.ndim - 1)
    up = pltpu.roll(x, w - 8, x.ndim - 1)
    dn = pltpu.roll(x, 8, x.ndim - 1)
    return jnp.where((lane & 15) < 8, up, dn)


def _tile_lanes(x, n):
    return x if n == 1 else jnp.concatenate([x] * n, axis=-1)


def _mod_kernel(c_ref, w_ref, b_ref, o_ref):
    c = c_ref[...]
    o_ref[0] = _dot3(c * jax.nn.sigmoid(c), w_ref[0]) + b_ref[0]


def _mod_all(cvecs, w_mod, b_mod):
    nt = 3
    return pl.pallas_call(
        _mod_kernel,
        grid=(DEPTH, nt),
        in_specs=[pl.BlockSpec((8, D_MODEL), lambda l, j: (0, 0)),
                  pl.BlockSpec((1, D_MODEL, D_MODEL), lambda l, j: (l, 0, j)),
                  pl.BlockSpec((1, 1, D_MODEL), lambda l, j: (l, 0, j))],
        out_specs=pl.BlockSpec((1, 8, D_MODEL), lambda l, j: (l, 0, j)),
        out_shape=jax.ShapeDtypeStruct((DEPTH, 8, 3 * D_MODEL), F32),
        compiler_params=_cparams(("arbitrary", "arbitrary")),
        name="mod",
    )(cvecs, w_mod, b_mod.reshape(DEPTH, 1, 3 * D_MODEL))


def _lb_kernel(x_ref, o_ref):
    x = x_ref[...]
    rows = [x[l:l + 1, :] for l in range(DEPTH)]
    m = functools.reduce(jnp.maximum, rows)
    e = [jnp.exp(r - m) for r in rows]
    tot = functools.reduce(lambda a, b: a + b, e)
    acc = jnp.zeros_like(tot)
    o_ref[0:1, :] = acc
    for l in range(1, DEPTH):
        acc = acc + e[l] / tot
        o_ref[l:l + 1, :] = acc


def _hgrn_lb(logits):
    flat = logits.transpose(1, 0, 2).reshape(DEPTH, 2 * BRANCH)
    lb = pl.pallas_call(
        _lb_kernel,
        out_shape=jax.ShapeDtypeStruct(flat.shape, F32),
        name="hgrn_lb",
    )(flat)
    return lb.reshape(DEPTH, 2, BRANCH).transpose(1, 0, 2)


def _inproj_kernel(x_ref, mod_ref, g_ref, w_ref, p_ref):
    h = _rms(x_ref[...], g_ref[...]) * (1.0 + mod_ref[0, 1:2, :]) + mod_ref[0, 0:1, :]
    p_ref[...] = jnp.dot(h.astype(BF16), w_ref[...], preferred_element_type=F32)


def _inproj(x2, mod, norm_g, w_in_p, seq_len):
    n = x2.shape[0]
    tm = 256
    per_batch = mod.shape[0] > 1
    tiles_per_seq = seq_len // tm
    mod_idx = (lambda i: (i // tiles_per_seq, 0, 0)) if per_batch else (lambda i: (0, 0, 0))
    return pl.pallas_call(
        _inproj_kernel,
        grid=(n // tm,),
        in_specs=[pl.BlockSpec((tm, D_MODEL), lambda i: (i, 0)),
                  pl.BlockSpec((1, 3, D_MODEL), mod_idx),
                  pl.BlockSpec((1, D_MODEL), lambda i: (0, 0)),
                  pl.BlockSpec((D_MODEL, P_COLS), lambda i: (0, 0))],
        out_specs=pl.BlockSpec((tm, P_COLS), lambda i: (i, 0)),
        out_shape=jax.ShapeDtypeStruct((n, P_COLS), F32),
        compiler_params=_cparams(("parallel",)),
        name="inproj",
    )(x2, mod, norm_g.reshape(1, D_MODEL), w_in_p)


def _mla_seg():
    sid = np.zeros((512,), np.int32)
    cnt = np.ones((512,), np.float32)
    for h in range(MLA_HEADS):
        b = 128 * h
        sid[b:b + 64] = 3 * h
        sid[b + 64:b + 96] = 3 * h + 1
        sid[b + 96:b + 128] = 3 * h + 2
        cnt[b:b + 64] = 1.0 / 64
        cnt[b + 64:b + 128] = 1.0 / 32
    seg = (sid[:, None] == sid[None, :]).astype(np.float32)
    return jnp.asarray(seg, BF16), jnp.asarray(cnt.reshape(1, 512))


def _mla_q_kernel(rope, cq_ref, ckv_ref, kr_ref, qng_ref, wuq_ref, kvg_ref, gq_ref, gkr_ref, seg_ref, cnt_ref,
                  *rest):
    if rope:
        cos_ref, sin_ref, q_ref, ckvn_ref, krp_ref = rest
    else:
        q_ref, ckvn_ref, krp_ref = rest
    cqn = _rms(cq_ref[...], qng_ref[...])
    q = _bdot(cqn, wuq_ref[...])
    ss = _segsum(q * q, seg_ref[...]) * cnt_ref[...]
    qn = q * lax.rsqrt(ss + EPS) * gq_ref[...]
    ckvn_ref[...] = _rms(ckv_ref[...], kvg_ref[...])
    kr = kr_ref[...]
    krn = kr * lax.rsqrt(jnp.sum(kr * kr, axis=-1, keepdims=True) * (1.0 / MLA_ROPE) + EPS) * gkr_ref[...]
    if rope:
        cos, sin = cos_ref[...], sin_ref[...]
        qn = qn * _tile_lanes(cos, MLA_HEADS) + _swap8(qn) * _tile_lanes(sin, MLA_HEADS)
        krn = krn * cos + _swap8(krn) * sin
    q_ref[...] = (qn * (MLA_SCALE * LOG2E)).astype(BF16)
    krp_ref[...] = krn


def _mla_q(p, lw, consts, seq_len, rope_tabs):
    n = p.shape[0]
    tm = 256
    rope = rope_tabs is not None
    full = lambda shape: pl.BlockSpec(shape, lambda i: (0,) * len(shape))
    in_specs = [pl.BlockSpec((tm, 256), lambda i: (i, P_CQ // 256)),
                pl.BlockSpec((tm, 128), lambda i: (i, P_CKV // 128)),
                pl.BlockSpec((tm, 128), lambda i: (i, P_KR // 128)),
                full((1, 256)), full((256, 512)), full((1, 128)), full((1, 512)), full((1, 128)),
                full((512, 512)), full((1, 512))]
    args = [p, p, p, lw["qn_g"], lw["w_uq"], lw["kvn_g"], lw["gq"], lw["gkr"], consts["seg512"], consts["cnt512"]]
    if rope:
        tps = seq_len // tm
        in_specs += [pl.BlockSpec((tm, 128), lambda i: (i % tps, 0))] * 2
        args += [rope_tabs["cos_mla"], rope_tabs["sin_mla"]]
    return pl.pallas_call(
        functools.partial(_mla_q_kernel, rope),
        grid=(n // tm,),
        in_specs=in_specs,
        out_specs=[pl.BlockSpec((tm, 512), lambda i: (i, 0)),
                   pl.BlockSpec((tm, 128), lambda i: (i, 0)),
                   pl.BlockSpec((tm, 128), lambda i: (i, 0))],
        out_shape=[jax.ShapeDtypeStruct((n, 512), BF16),
                   jax.ShapeDtypeStruct((n, 128), F32),
                   jax.ShapeDtypeStruct((n, 128), F32)],
        compiler_params=_cparams(("parallel",)),
        name="mla_q",
    )(*args)


def _mla_kv_kernel(ckvn_ref, krp_ref, wuk_ref, wuv_ref, gk_ref, seg_ref, cnt_ref, k_ref, v_ref):
    c = ckvn_ref[...].astype(BF16)
    kn = jnp.dot(c, wuk_ref[...], preferred_element_type=F32)
    ss = _segsum(kn * kn, seg_ref[...]) * cnt_ref[...]
    k = kn * lax.rsqrt(ss + EPS) * gk_ref[...] + _tile_lanes(krp_ref[...], MLA_HEADS)
    k_ref[...] = k.astype(BF16)
    v_ref[...] = jnp.dot(c, wuv_ref[...], preferred_element_type=F32).astype(BF16)


def _mla_kv(ckvn, krp, lw, consts):
    n = ckvn.shape[0]
    tm = 512
    full = lambda shape: pl.BlockSpec(shape, lambda i: (0,) * len(shape))
    return pl.pallas_call(
        _mla_kv_kernel,
        grid=(n // tm,),
        in_specs=[pl.BlockSpec((tm, 128), lambda i: (i, 0)), pl.BlockSpec((tm, 128), lambda i: (i, 0)),
                  full((128, 512)), full((128, 256)), full((1, 512)), full((512, 512)), full((1, 512))],
        out_specs=[pl.BlockSpec((tm, 512), lambda i: (i, 0)), pl.BlockSpec((tm, 256), lambda i: (i, 0))],
        out_shape=[jax.ShapeDtypeStruct((n, 512), BF16), jax.ShapeDtypeStruct((n, 256), BF16)],
        compiler_params=_cparams(("parallel",)),
        name="mla_kv",
    )(ckvn, krp, lw["w_uk"], lw["w_uv"], lw["gk"], consts["seg512"], consts["cnt512"])


def _softmax_pv(qs, k_ref, vt_ref, key_chunk, sub_rows, n_ahead):
    lk = k_ref.shape[1]
    nch = lk // key_chunk
    nq = len(qs)
    sub = min(sub_rows, key_chunk)
    nsub = key_chunk // sub

    def scores(c, u):
        lo = c * key_chunk + u * sub
        ks = k_ref[0, lo:lo + sub, :]
        return [_nt(ks, q) for q in qs]

    def chunk_max(s_chunk, j):
        mc = functools.reduce(jnp.maximum, [s_chunk[u][j] for u in range(nsub)])
        return jnp.max(mc, axis=0, keepdims=True)

    s_buf = {c: [scores(c, u) for u in range(nsub)] for c in range(min(n_ahead, nch))}
    m = [None] * nq
    acc = [None] * nq
    m_new = [chunk_max(s_buf[0], j) for j in range(nq)]
    for c in range(nch):
        s_cur = s_buf.pop(c)
        ahead = c + n_ahead
        if ahead < nch:
            s_buf[ahead] = []
        pv = [None] * nq
        for u in range(nsub):
            if ahead < nch:
                s_buf[ahead].append(scores(ahead, u))
            lo = c * key_chunk + u * sub
            vs = vt_ref[0, 0, :, lo:lo + sub]
            for j in range(nq):
                part = jnp.dot(vs, jnp.exp2(s_cur[u][j] - m_new[j]).astype(BF16), preferred_element_type=F32)
                pv[j] = part if pv[j] is None else pv[j] + part
        for j in range(nq):
            acc[j] = pv[j] if c == 0 else acc[j] * jnp.exp2(m[j] - m_new[j]) + pv[j]
            m[j] = m_new[j]
        if c + 1 < nch:
            m_new = [jnp.maximum(m[j], chunk_max(s_buf[c + 1], j)) for j in range(nq)]
    return acc


def _mla_attn_kernel(key_chunk, q_ref, k_ref, vt_ref, o_ref):
    (acc,) = _softmax_pv([q_ref[...]], k_ref, vt_ref, key_chunk, MLA_SUB, MLA_AHEAD)
    o_ref[0] = acc[0:MLA_V] / acc[MLA_V:MLA_V + 1]


def _key_chunk(lk, rows=512):
    return rows if lk % rows == 0 else lk


def _vt_kernel(v_ref, o_ref):
    vt = v_ref[0].astype(F32).T
    tm = vt.shape[1]
    row = lax.broadcasted_iota(jnp.int32, (VT_ROWS - 64, tm), 0)
    extra = jnp.where(row == 0, 1.0, 0.0).astype(BF16)
    for h in range(4):
        o_ref[0, h, 0:64, :] = vt[64 * h:64 * (h + 1)].astype(BF16)
        o_ref[0, h, 64:VT_ROWS, :] = extra


def _vt_with_ones(v3):
    b, lk, width = v3.shape
    tm = 512 if lk % 512 == 0 else lk
    return pl.pallas_call(
        _vt_kernel,
        grid=(b, lk // tm),
        in_specs=[pl.BlockSpec((1, tm, width), lambda i, j: (i, j, 0))],
        out_specs=pl.BlockSpec((1, 4, VT_ROWS, tm), lambda i, j: (i, 0, 0, j)),
        out_shape=jax.ShapeDtypeStruct((b, 4, VT_ROWS, lk), BF16),
        compiler_params=_cparams(("parallel", "parallel")),
        name="vt_ones",
    )(v3)


def _mla_attn(q, k, vt, batch, seq_len):
    lk = k.shape[1]
    tq = 256
    nq = seq_len // tq
    return pl.pallas_call(
        functools.partial(_mla_attn_kernel, _key_chunk(lk, MLA_CHUNK)),
        grid=(batch, MLA_HEADS, nq),
        in_specs=[pl.BlockSpec((tq, 128), lambda b, h, i: (b * nq + i, h)),
                  pl.BlockSpec((1, lk, 128), lambda b, h, i: (b, 0, h)),
                  pl.BlockSpec((1, 1, VT_ROWS, lk), lambda b, h, i: (b, h, 0, 0))],
        out_specs=pl.BlockSpec((1, MLA_V, tq), lambda b, h, i: (b, h, i)),
        out_shape=jax.ShapeDtypeStruct((batch, BRANCH, seq_len), F32),
        compiler_params=_cparams(("parallel", "parallel", "arbitrary")),
        name="mla_attn",
    )(q, k, vt)


def _seg_const(width, seg):
    sid = np.arange(width) // seg
    return jnp.asarray((sid[:, None] == sid[None, :]).astype(np.float32), BF16)


def _diff_prep_kernel(rope, dq_ref, dk_ref, gq_ref, gk_ref, seg_ref, *rest):
    if rope:
        cos_ref, sin_ref, q_ref, k_ref, kf_ref = rest
    else:
        q_ref, k_ref, kf_ref = rest
    seg = seg_ref[...]

    def norm(x, g):
        ss = _segsum(x * x, seg) * (1.0 / DIFF_HD)
        return x * lax.rsqrt(ss + EPS) * g

    q = norm(dq_ref[...], gq_ref[...])
    k = norm(dk_ref[...], gk_ref[...])
    kf_ref[...] = k
    if rope:
        cos, sin = cos_ref[...], sin_ref[...]
        q = q * cos + _swap8(q) * sin
        k = k * cos + _swap8(k) * sin
    q_ref[...] = (q * (DIFF_SCALE * LOG2E)).astype(BF16)
    k_ref[...] = k.astype(BF16)


def _diff_prep(p, lw, consts, seq_len, rope_tabs):
    n = p.shape[0]
    tm = 256
    rope = rope_tabs is not None
    full = lambda shape: pl.BlockSpec(shape, lambda i: (0,) * len(shape))
    in_specs = [pl.BlockSpec((tm, 256), lambda i: (i, P_DQ // 256)),
                pl.BlockSpec((tm, 256), lambda i: (i, P_DK // 256)),
                full((1, 256)), full((1, 256)), full((256, 256))]
    args = [p, p, lw["dgq"], lw["dgk"], consts["seg32"]]
    if rope:
        tps = seq_len // tm
        in_specs += [pl.BlockSpec((tm, 256), lambda i: (i % tps, 0))] * 2
        args += [rope_tabs["cos_diff"], rope_tabs["sin_diff"]]
    blk = pl.BlockSpec((tm, 256), lambda i: (i, 0))
    return pl.pallas_call(
        functools.partial(_diff_prep_kernel, rope),
        grid=(n // tm,),
        in_specs=in_specs,
        out_specs=[blk, blk, blk],
        out_shape=[jax.ShapeDtypeStruct((n, 256), BF16), jax.ShapeDtypeStruct((n, 256), BF16),
                   jax.ShapeDtypeStruct((n, 256), F32)],
        compiler_params=_cparams(("parallel",)),
        name="diff_prep",
    )(*args)


def _diff_attn_kernel(lam_init, key_chunk, q_ref, k_ref, vt_ref, lp_ref, g_ref, o_ref):
    h = pl.program_id(1)
    q = q_ref[...]
    lane = lax.broadcasted_iota(jnp.int32, q.shape, 1)
    base = (h % 2) * 64
    zero = jnp.zeros_like(q)

    def map_query(j):
        lo = base + 32 * j
        return jnp.where((lane >= lo) & (lane < lo + 32), q, zero)

    acc0, acc1 = _softmax_pv([map_query(0), map_query(1)], k_ref, vt_ref, key_chunk, DIFF_SUB, DIFF_AHEAD)
    lp = lp_ref[...]
    lam = (jnp.exp(jnp.sum(lp[0:1] * lp[1:2], axis=1, keepdims=True))
           - jnp.exp(jnp.sum(lp[2:3] * lp[3:4], axis=1, keepdims=True)) + lam_init)
    o = acc0[0:64] / acc0[64:65] - lam * (acc1[0:64] / acc1[64:65])
    ms = jnp.mean(o * o, axis=0, keepdims=True)
    o_ref[0] = o * lax.rsqrt(ms + EPS) * g_ref[...] * (1.0 - lam_init)


def _diff_attn(q, k, vt, lp, g_col, lam_init, batch, seq_len):
    lk = k.shape[1]
    tq = 256
    nq = seq_len // tq
    return pl.pallas_call(
        functools.partial(_diff_attn_kernel, lam_init, _key_chunk(lk, DIFF_CHUNK)),
        grid=(batch, DIFF_HEADS, nq),
        in_specs=[pl.BlockSpec((tq, 128), lambda b, h, i: (b * nq + i, h // 2)),
                  pl.BlockSpec((1, lk, 128), lambda b, h, i: (b, 0, h // 2)),
                  pl.BlockSpec((1, 1, VT_ROWS, lk), lambda b, h, i: (b, h, 0, 0)),
                  pl.BlockSpec((4, DIFF_HD), lambda b, h, i: (0, 0)),
                  pl.BlockSpec((64, 1), lambda b, h, i: (0, 0))],
        out_specs=pl.BlockSpec((1, 64, tq), lambda b, h, i: (b, h, i)),
        out_shape=jax.ShapeDtypeStruct((batch, BRANCH, seq_len), F32),
        compiler_params=_cparams(("parallel", "parallel", "arbitrary")),
        name="diff_attn",
    )(q, k, vt, lp, g_col)


def _hgrn_consts():
    c = HGRN_CHUNK
    t = np.arange(c)
    low = (t[None, :] <= t[:, None]).astype(np.float32)
    blocks = []
    for j in range(HGRN_MM_LEVELS):
        m = 1 << j
        rho = (t // (2 * m)) * (2 * m) + m - 1
        sign = np.where((t // m) % 2 == 1, 1.0, -1.0)[:, None]
        blocks.append(sign * (low - (t[None, :] <= rho[:, None]).astype(np.float32)))
    blocks.append(low)
    fwd = np.concatenate(blocks, axis=0)
    bwd = np.concatenate([b[::-1, ::-1] for b in blocks], axis=0)
    right = np.stack([(t // (1 << j)) % 2 for j in range(HGRN_LEVELS)]).astype(np.float32)
    right = np.stack([right, right[:, ::-1]])
    right = np.broadcast_to(right[..., None], right.shape + (BRANCH,))
    return jnp.asarray(np.stack([fwd, bwd]), BF16), jnp.asarray(right, F32)


def _hgrn_kernel(nc, nb, qf_ref, zf_ref, vf_ref, qb_ref, zb_ref, vb_ref, lb_ref, dd_ref, rm_ref, s0_ref,
                 of_ref, ob_ref, sout_ref, st_ref):
    c = HGRN_CHUNK
    ci = pl.program_id(1)
    chains = [(bi, d) for bi in range(nb) for d in (0, 1)]
    ids = range(len(chains))

    @pl.when(ci == 0)
    def _():
        st_ref[...] = s0_ref[...]

    lane = lax.broadcasted_iota(jnp.int32, (1, BRANCH), 1)
    head_masks = [(lane >= HGRN_DK * h) & (lane < HGRN_DK * (h + 1)) for h in range(HGRN_HEADS)]
    t_idx = lax.broadcasted_iota(jnp.int32, (c, HGRN_HEADS * c), 0)
    s_idx = lax.broadcasted_iota(jnp.int32, (c, HGRN_HEADS * c), 1) & (c - 1)
    pair_xor = t_idx ^ s_idx

    def stack_heads(x):
        xb = x.astype(BF16)
        zero = jnp.zeros_like(xb)
        return jnp.concatenate([jnp.where(hm, xb, zero) for hm in head_masks], axis=0)

    q_refs, z_refs, v_refs = (qf_ref, qb_ref), (zf_ref, zb_ref), (vf_ref, vb_ref)
    q = [q_refs[d][bi] for bi, d in chains]
    v = [v_refs[d][bi] for bi, d in chains]
    z = [z_refs[d][bi] for bi, d in chains]
    lb = [lb_ref[d] for _, d in chains]
    g = [jnp.log(lb[i] + (1.0 - lb[i]) * jax.nn.sigmoid(z[i])) for i in ids]
    kk = [(1.0 - lb[i]) * jax.nn.sigmoid(-z[i]) for i in ids]
    sums = []
    for i in ids:
        gh, gl = _split2(g[i])
        dd = dd_ref[chains[i][1]]
        sums.append(jnp.dot(dd, gh, preferred_element_type=F32) + jnp.dot(dd, gl, preferred_element_type=F32))
    b = [sums[i][HGRN_MM_LEVELS * c:] for i in ids]
    b_tot = [b[i][c - 1:c] if chains[i][1] == 0 else b[i][0:1] for i in ids]

    def neg_abs_decay(i, j):
        if j < HGRN_MM_LEVELS:
            return sums[i][j * c:(j + 1) * c]
        m = 1 << j
        off = m - 1 if chains[i][1] == 0 else m
        ref = jnp.concatenate([jnp.broadcast_to(b[i][g0 + off:g0 + off + 1], (2 * m, BRANCH))
                               for g0 in range(0, c, 2 * m)], axis=0)
        return -jnp.abs(b[i] - ref)

    a = [None] * len(chains)
    for j in reversed(range(HGRN_LEVELS)):
        same_group = pair_xor < (2 << j)
        for i in ids:
            e = jnp.exp(neg_abs_decay(i, j))
            eq = e * rm_ref[chains[i][1], j]
            qt = q[i] * eq
            kt = kk[i] * (e - eq)
            lvl = _nt(qt, stack_heads(kt))
            a[i] = lvl if a[i] is None else jnp.where(same_group, lvl, a[i])
    diagonal = pair_xor == 0
    for i in ids:
        a[i] = jnp.where(diagonal, _nt(q[i], stack_heads(kk[i])), a[i])

    outs = (of_ref, ob_ref)
    for i in ids:
        bi, d = chains[i]
        o = jnp.dot(a[i].astype(BF16), stack_heads(v[i]), preferred_element_type=F32)
        outs[d][bi] = o + _nt(q[i] * jnp.exp(b[i]), st_ref[bi, d])

    r2 = lax.broadcasted_iota(jnp.int32, (BRANCH, BRANCH), 0) // HGRN_DK
    c2 = lax.broadcasted_iota(jnp.int32, (BRANCH, BRANCH), 1) // HGRN_DK
    for i in ids:
        bi, d = chains[i]
        kd = kk[i] * jnp.exp(b_tot[i] - b[i])
        upd = lax.dot_general(v[i].astype(BF16), kd.astype(BF16), (((0,), (0,)), ((), ())),
                              preferred_element_type=F32)
        st_new = st_ref[bi, d] * jnp.exp(b_tot[i]) + jnp.where(r2 == c2, upd, 0.0)
        st_ref[bi, d] = st_new

        @pl.when(ci == nc - 1)
        def _(bi=bi, d=d, st_new=st_new):
            sout_ref[bi, d] = st_new


def _hgrn(p, lb_l, dd, rm, st0, batch, seq_len):
    n = p.shape[0]
    c = HGRN_CHUNK
    nc = seq_len // c
    nb = HGRN_ROWS
    p3 = p.reshape(batch, seq_len, P_COLS)
    fwd = lambda col: pl.BlockSpec((nb, c, 256), lambda b, i: (b, i, col))
    bwd = lambda col: pl.BlockSpec((nb, c, 256), lambda b, i: (b, nc - 1 - i, col))
    whole = lambda shape: pl.BlockSpec(shape, lambda b, i: (0,) * len(shape))
    state = pl.BlockSpec((nb, 2, 256, 256), lambda b, i: (b, 0, 0, 0))
    o_f, o_b, st = pl.pallas_call(
        functools.partial(_hgrn_kernel, nc, nb),
        grid=(batch // nb, nc),
        in_specs=[fwd(P_HQ // 256), fwd(P_HZF // 256), fwd(P_HI // 256),
                  bwd(P_HQ // 256), bwd(P_HZB // 256), bwd(P_HI // 256),
                  whole((2, 1, 256)), whole((2, (HGRN_MM_LEVELS + 1) * c, c)),
                  whole((2, HGRN_LEVELS, c, 256)), state],
        out_specs=[pl.BlockSpec((nb, c, 256), lambda b, i: (b, i, 0)),
                   pl.BlockSpec((nb, c, 256), lambda b, i: (b, nc - 1 - i, 0)), state],
        out_shape=[jax.ShapeDtypeStruct((batch, seq_len, 256), F32),
                   jax.ShapeDtypeStruct((batch, seq_len, 256), F32),
                   jax.ShapeDtypeStruct((batch, 2, 256, 256), F32)],
        scratch_shapes=[pltpu.VMEM((nb, 2, 256, 256), F32)],
        compiler_params=_cparams(("parallel", "arbitrary")),
        name="hgrn",
    )(p3, p3, p3, p3, p3, p3, lb_l, dd, rm, st0)
    return o_f.reshape(n, 256), o_b.reshape(n, 256), st


def _hy_conv3_kernel(tiles_per_seq, above_ref, cur_ref, below_ref, w_ref, b_ref, v_ref, x1_ref, x2_ref):
    i = pl.program_id(0)
    cur = cur_ref[...]
    tm = cur.shape[0]
    first = (i % tiles_per_seq) == 0
    last = (i % tiles_per_seq) == tiles_per_seq - 1
    above = jnp.where(first, 0.0, above_ref[7:8, :])
    below = jnp.where(last, 0.0, below_ref[0:1, :])
    row = lax.broadcasted_iota(jnp.int32, (tm, 1), 0)
    prev = jnp.where(row == 0, above, pltpu.roll(cur, 1, 0))
    nxt = jnp.where(row == tm - 1, below, pltpu.roll(cur, tm - 1, 0))
    w = w_ref[...]
    u = prev * w[0:1] + cur * w[1:2] + nxt * w[2:3] + b_ref[...]
    v_ref[...] = u[:, 0:256]
    x1_ref[...] = u[:, 256:512]
    x2_ref[...] = u[:, 512:768]


def _hy_conv3(p, w, b, seq_len):
    n = p.shape[0]
    tm = min(512, seq_len)
    nt = n // tm
    g = tm // 8
    col = P_HU // 768
    oblk = pl.BlockSpec((tm, 256), lambda i: (i, 0))
    return pl.pallas_call(
        functools.partial(_hy_conv3_kernel, seq_len // tm),
        grid=(nt,),
        in_specs=[pl.BlockSpec((8, 768), lambda i: (jnp.maximum(i * g - 1, 0), col)),
                  pl.BlockSpec((tm, 768), lambda i: (i, col)),
                  pl.BlockSpec((8, 768), lambda i: (jnp.minimum((i + 1) * g, nt * g - 1), col)),
                  pl.BlockSpec((3, 768), lambda i: (0, 0)), pl.BlockSpec((1, 768), lambda i: (0, 0))],
        out_specs=[oblk, oblk, oblk],
        out_shape=[jax.ShapeDtypeStruct((n, 256), F32)] * 3,
        compiler_params=_cparams(("parallel",)),
        name="hy_conv3",
    )(p, p, p, w, b)


def _hy_filter_kernel(feat_ref, w1_ref, b1_ref, w2_ref, b2_ref, w3_ref, fr_ref, win_ref, o_ref):
    fr = fr_ref[...]
    h = jnp.sin(fr[0:1] * (_dot3(feat_ref[...], w1_ref[...]) + b1_ref[...]))
    h = jnp.sin(fr[1:2] * (_dot3(h, w2_ref[...]) + b2_ref[...]))
    o_ref[...] = _dot3(h, w3_ref[0]) * _tile_lanes(win_ref[...], HY_ORDER)


def _hy_filter(feats2, window2, w1p, b1, w2, b2, w3d, freq):
    l2 = feats2.shape[0]
    ln = l2 // 2
    tm = min(512, ln)
    full = lambda shape: pl.BlockSpec(shape, lambda i: (0,) * len(shape))
    return pl.pallas_call(
        _hy_filter_kernel,
        grid=(l2 // tm,),
        in_specs=[pl.BlockSpec((tm, LANE), lambda i: (i, 0)),
                  full((LANE, HY_FH)), full((1, HY_FH)), full((HY_FH, HY_FH)), full((1, HY_FH)),
                  pl.BlockSpec((1, HY_FH, HY_ORDER * HY_CH), lambda i: (i // (ln // tm), 0, 0)), full((2, HY_FH)),
                  pl.BlockSpec((tm, HY_CH), lambda i: (i, 0))],
        out_specs=pl.BlockSpec((tm, HY_ORDER * HY_CH), lambda i: (i, 0)),
        out_shape=jax.ShapeDtypeStruct((l2, HY_ORDER * HY_CH), F32),
        compiler_params=_cparams(("parallel",)),
        name="hy_filter",
    )(feats2, w1p, b1, w2, b2, w3d, freq, window2)


def _fft_blocking(nb, n1, n2, ch):
    per_batch = n1 * n2 * ch * 4
    if per_batch <= FFT_BLOCK_BYTES:
        bb = max(1, min(nb, FFT_BLOCK_BYTES // per_batch))
        while nb % bb:
            bb -= 1
        return bb, n2
    rt = n2
    while n1 * rt * ch * 4 > FFT_BLOCK_BYTES and rt > 8:
        rt //= 2
    return 1, rt


def _fft_a_kernel(f_ref, x_ref, o_ref):
    f = f_ref[...]
    for b in range(x_ref.shape[0]):
        x = x_ref[b].astype(BF16)
        o_ref[b] = jnp.einsum("kn,nrc->krc", f, x, preferred_element_type=F32).astype(o_ref.dtype)


def _fft_a(fa, x4, out_dtype):
    nb, n1, n2, ch = x4.shape
    r = fa.shape[0]
    bb, rt = _fft_blocking(nb, n1, n2, ch)
    return pl.pallas_call(
        _fft_a_kernel,
        grid=(nb // bb, n2 // rt),
        in_specs=[pl.BlockSpec((r, n1), lambda b, i: (0, 0)),
                  pl.BlockSpec((bb, n1, rt, ch), lambda b, i: (b, 0, i, 0))],
        out_specs=pl.BlockSpec((bb, r, rt, ch), lambda b, i: (b, 0, i, 0)),
        out_shape=jax.ShapeDtypeStruct((nb, r, n2, ch), out_dtype),
        compiler_params=_cparams(("parallel", "parallel")),
        name="fft_a",
    )(fa, x4)


def _fft_b_kernel(with_inverse, mf_ref, *rest):
    if with_inverse:
        mi_ref, a_ref, h_ref, o_ref = rest
    else:
        a_ref, o_ref = rest
    half = FFT_N2
    for b in range(a_ref.shape[0]):
        a = jnp.concatenate([a_ref[b, 0, 0], a_ref[b, 1, 0]], axis=0)
        x = jnp.dot(mf_ref[0], a.astype(BF16), preferred_element_type=F32)
        if with_inverse:
            xr, xi = x[:half], x[half:]
            hr, hi = h_ref[0, 0], h_ref[1, 0]
            y = jnp.concatenate([xr * hr - xi * hi, xr * hi + xi * hr], axis=0)
            x = jnp.dot(mi_ref[0], y.astype(BF16), preferred_element_type=F32)
        o_ref[b, 0, 0] = x[:half].astype(o_ref.dtype)
        o_ref[b, 1, 0] = x[half:].astype(o_ref.dtype)


def _fft_b(mf, mi, a5, spec, order):
    nb, _, k1n, n2, ch = a5.shape
    mat = pl.BlockSpec((1, 2 * n2, 2 * n2), lambda k: (k, 0, 0))
    blk = pl.BlockSpec((nb, 2, 1, n2, ch), lambda k: (0, 0, k, 0, 0))
    if spec is None:
        in_specs, args = [mat, blk], [mf, a5]
    else:
        in_specs = [mat, mat, blk, pl.BlockSpec((2, 1, n2, ch), lambda k: (0, k, 0, order))]
        args = [mf, mi, a5, spec]
    return pl.pallas_call(
        functools.partial(_fft_b_kernel, spec is not None),
        grid=(k1n,),
        in_specs=in_specs,
        out_specs=blk,
        out_shape=jax.ShapeDtypeStruct(a5.shape, F32 if spec is None else BF16),
        compiler_params=_cparams(("parallel",)),
        name="fft_b",
    )(*args)


def _fft_a_inv_kernel(g_ref, p_ref, x_ref, z_ref, bias_ref, o_ref):
    g = g_ref[...]
    for b in range(p_ref.shape[0]):
        conv = jnp.einsum("nk,krc->nrc", g, p_ref[b], preferred_element_type=F32)
        o_ref[b] = x_ref[b] * (conv + z_ref[b] * bias_ref[...])


def _fft_a_inv(g, p4, xg4, z4, bias):
    nb, n1, n2, ch = z4.shape
    r = g.shape[1]
    bb, rt = _fft_blocking(nb, n1, n2, ch)
    blk = pl.BlockSpec((bb, n1, rt, ch), lambda b, i: (b, 0, i, 0))
    return pl.pallas_call(
        _fft_a_inv_kernel,
        grid=(nb // bb, n2 // rt),
        in_specs=[pl.BlockSpec((n1, r), lambda b, i: (0, 0)),
                  pl.BlockSpec((bb, r, rt, ch), lambda b, i: (b, 0, i, 0)), blk, blk,
                  pl.BlockSpec((1, 1, ch), lambda b, i: (0, 0, 0))],
        out_specs=blk,
        out_shape=jax.ShapeDtypeStruct(z4.shape, F32),
        compiler_params=_cparams(("parallel", "parallel")),
        name="fft_a_inv",
    )(g, p4, xg4, z4, bias.reshape(1, 1, ch))


def _fft_tables(ln):
    n = 2 * ln
    n1t = n // FFT_N2
    k1n = n1t // 2 + 1
    kk = np.arange(k1n)

    def stage_a(n1_in):
        ang = 2.0 * np.pi * ((kk[:, None] * np.arange(n1_in)[None, :]) % n1t) / n1t
        return jnp.asarray(np.concatenate([np.cos(ang), -np.sin(ang)], axis=0), BF16)

    n1o = n1t // 2
    ang = 2.0 * np.pi * ((np.arange(n1o)[:, None] * kk[None, :]) % n1t) / n1t
    edge = (kk == 0) | (kk == n1t // 2)
    ck = np.where(edge, 1.0, 2.0) / n
    g = jnp.asarray(np.concatenate([ck * np.cos(ang), -ck * np.where(edge, 0.0, np.sin(ang))], axis=1), BF16)

    k1 = jnp.arange(k1n, dtype=jnp.int32)[:, None, None]
    k2 = jnp.arange(FFT_N2, dtype=jnp.int32)[None, :, None]
    n2 = jnp.arange(FFT_N2, dtype=jnp.int32)[None, None, :]
    th = (2.0 * math.pi / n) * ((n2 * (k1 + n1t * k2)) % n).astype(F32)
    c, s = jnp.cos(th), jnp.sin(th)
    mf = jnp.concatenate([jnp.concatenate([c, s], axis=2), jnp.concatenate([-s, c], axis=2)], axis=1)
    return dict(fa_half=stage_a(n1o), fa_full=stage_a(n1t), g=g, mf=mf.astype(BF16),
                mi=mf.transpose(0, 2, 1).astype(BF16), k1n=k1n, n1o=n1o, n1t=n1t)


def _hy_static(ln):
    t = jnp.linspace(0.0, 1.0, ln, dtype=F32)[:, None]
    w = 2.0 * math.pi * jnp.arange(ln, dtype=F32) / ln
    f = jnp.linspace(1e-4, HY_BANDS - 1, HY_BANDS, dtype=F32)
    ang = w[:, None] * f[None, :]
    feats = jnp.concatenate([t, jnp.cos(ang), -jnp.sin(ang)], axis=-1)
    feats = jnp.pad(feats, ((0, 0), (0, LANE - HY_EMB)))
    min_decay = math.log(HY_DECAY_TARGET) / HY_SLOW_DECAY
    max_decay = math.log(HY_DECAY_TARGET) / HY_FAST_DECAY
    deltas = jnp.linspace(min_decay, max_decay, HY_CH, dtype=F32)
    window = jnp.exp(-t * jnp.abs(deltas))
    feats = jnp.concatenate([feats, feats[::-1]], axis=0)
    window = jnp.concatenate([window, window[::-1]], axis=0)
    return feats, window


def _outproj_kernel(x_ref, mod_ref, oa_ref, ob_ref, of_ref, obk_ref, od_ref, gate_ref, hg_ref, seg_ref, w_ref,
                    y_ref):
    gt = gate_ref[...]
    sg = gt * jax.nn.sigmoid(gt)
    oc = of_ref[...] + obk_ref[...]
    ss = _segsum(oc * oc, seg_ref[...]) * (1.0 / HGRN_DK)
    oc = oc * lax.rsqrt(ss + EPS) * hg_ref[...]
    acc = _bdot(oa_ref[0].T * sg[:, 0:256], w_ref[0:256, :])
    acc += _bdot(ob_ref[0].T * sg[:, 256:512], w_ref[256:512, :])
    acc += _bdot(oc * sg[:, 512:768], w_ref[512:768, :])
    acc += _bdot(od_ref[...] * sg[:, 768:1024], w_ref[768:1024, :])
    y_ref[...] = x_ref[...] + mod_ref[0, 2:3, :] * acc


def _outproj(x2, mod, ot_a, ot_b, o_f, o_b, out_d, p, hg, seg64, w_out, seq_len):
    n = x2.shape[0]
    tm = 256
    per_batch = mod.shape[0] > 1
    tps = seq_len // tm
    mod_idx = (lambda i: (i // tps, 0, 0)) if per_batch else (lambda i: (0, 0, 0))
    b256 = pl.BlockSpec((tm, 256), lambda i: (i, 0))
    bt = pl.BlockSpec((1, 256, tm), lambda i: (i // tps, 0, i % tps))
    return pl.pallas_call(
        _outproj_kernel,
        grid=(n // tm,),
        in_specs=[pl.BlockSpec((tm, D_MODEL), lambda i: (i, 0)),
                  pl.BlockSpec((1, 3, D_MODEL), mod_idx),
                  bt, bt, b256, b256, b256,
                  pl.BlockSpec((tm, 1024), lambda i: (i, P_GATE // 1024)),
                  pl.BlockSpec((1, 256), lambda i: (0, 0)),
                  pl.BlockSpec((256, 256), lambda i: (0, 0)),
                  pl.BlockSpec((D_MODEL, D_MODEL), lambda i: (0, 0))],
        out_specs=pl.BlockSpec((tm, D_MODEL), lambda i: (i, 0)),
        out_shape=jax.ShapeDtypeStruct((n, D_MODEL), F32),
        compiler_params=_cparams(("parallel",)),
        name="outproj",
    )(x2, mod, ot_a, ot_b, o_f, o_b, out_d, p, hg, seg64, w_out)


def _layer(x2, mod, lw, consts, batch, seq_len, ctx, rope_tabs, hy):
    n = batch * seq_len
    p = _inproj(x2, mod, lw["norm_g"], lw["w_in"], seq_len)

    q_a, ckvn, krp = _mla_q(p, lw, consts, seq_len, rope_tabs)
    ckv3 = ckvn.reshape(batch, seq_len, MLA_KV_LORA)
    krp3 = krp.reshape(batch, seq_len, LANE)
    if ctx is not None:
        cache_kr = jnp.pad(ctx[1], ((0, 0), (0, 0), (KR_OFF, LANE - KR_OFF - MLA_ROPE)))
        ckv_all = jnp.concatenate([ckv3, ctx[0]], axis=1)
        kr_all = jnp.concatenate([krp3, cache_kr], axis=1)
    else:
        ckv_all, kr_all = ckv3, krp3
    lk = ckv_all.shape[1]
    k_a, v_a = _mla_kv(ckv_all.reshape(batch * lk, MLA_KV_LORA), kr_all.reshape(batch * lk, LANE), lw, consts)
    ot_a = _mla_attn(q_a, k_a.reshape(batch, lk, 512), _vt_with_ones(v_a.reshape(batch, lk, BRANCH)), batch, seq_len)

    q_b, k_b, kd = _diff_prep(p, lw, consts, seq_len, rope_tabs)
    dv = p[:, P_DV:P_DV + BRANCH]
    k_b3 = k_b.reshape(batch, seq_len, BRANCH)
    v_b3 = dv.astype(BF16).reshape(batch, seq_len, BRANCH)
    if ctx is not None:
        k_b3 = jnp.concatenate([k_b3, ctx[2].reshape(batch, -1, BRANCH).astype(BF16)], axis=1)
        v_b3 = jnp.concatenate([v_b3, ctx[3].reshape(batch, -1, BRANCH).astype(BF16)], axis=1)
    ot_b = _diff_attn(q_b, k_b3, _vt_with_ones(v_b3), lw["diff_lambda"], lw["subln_col"], lw["lam_init"],
                      batch, seq_len)

    if ctx is not None:
        s0 = ctx[4]
    else:
        s0 = jnp.zeros((batch, 2, HGRN_HEADS, HGRN_DK, HGRN_DK), F32)
    eye = jnp.eye(HGRN_HEADS, dtype=F32)
    st0 = jnp.einsum("bdhke,hg->bdhegk", s0, eye).reshape(batch, 2, BRANCH, BRANCH)
    o_f, o_b, st_out = _hgrn(p, lw["hgrn_lb"], consts["hgrn_dd"], consts["hgrn_right"], st0, batch, seq_len)
    st5 = st_out.reshape(batch, 2, HGRN_HEADS, HGRN_DK, HGRN_HEADS, HGRN_DK)
    states = jnp.stack([st5[:, :, h, :, h, :] for h in range(HGRN_HEADS)], axis=2).swapaxes(-1, -2)

    v_d, x1, x2g = _hy_conv3(p, lw["hy_conv_w"], lw["hy_conv_b"], seq_len)
    taps = _hy_filter(hy["feats"], hy["window"], lw["hy_w1"], lw["hy_b1"], lw["hy_w2"], lw["hy_b2"], lw["hy_w3"],
                      lw["hy_freq"])
    k1n, n1o, n1t = hy["k1n"], hy["n1o"], hy["n1t"]
    ta = _fft_a(hy["fa_full"], taps.reshape(1, n1t, FFT_N2, HY_ORDER * HY_CH), BF16)
    spec = _fft_b(hy["mf"], None, ta.reshape(1, 2, k1n, FFT_N2, HY_ORDER * HY_CH), None, 0)[0]
    z4 = v_d.reshape(batch, n1o, FFT_N2, HY_CH)
    for o, xg in enumerate((x1, x2g)):
        a = _fft_a(hy["fa_half"], z4, BF16).reshape(batch, 2, k1n, FFT_N2, HY_CH)
        pk = _fft_b(hy["mf"], hy["mi"], a, spec, o).reshape(batch, 2 * k1n, FFT_N2, HY_CH)
        z4 = _fft_a_inv(hy["g"], pk, xg.reshape(batch, n1o, FFT_N2, HY_CH), z4, lw["hy_bias"][o:o + 1])
    out_d = z4.reshape(n, HY_CH)

    y = _outproj(x2, mod, ot_a, ot_b, o_f, o_b, out_d, p, lw["hgrn_out_g"], consts["seg64"], lw["w_out"], seq_len)
    new = None
    if ctx is None:
        new = (ckv3, krp3[:, :, KR_OFF:KR_OFF + MLA_ROPE],
               kd.reshape(batch, seq_len, DIFF_HEADS, 2, DIFF_HD),
               dv.reshape(batch, seq_len, DIFF_HEADS, 2 * DIFF_HD), states)
    return y, new


def _rope_tables(seq_len):
    half = MLA_ROPE // 2
    inv = ROPE_BASE ** (-jnp.arange(0, half, 2, dtype=F32) / half)
    rows = seq_len // GRID_W
    row = jnp.repeat(jnp.arange(rows, dtype=F32), GRID_W)
    col = (jnp.arange(rows * GRID_W) % GRID_W).astype(F32)
    ar, ac = row[:, None] * inv, col[:, None] * inv
    cos32 = jnp.concatenate([jnp.cos(ar), jnp.cos(ar), jnp.cos(ac), jnp.cos(ac)], axis=-1)
    sin32 = jnp.concatenate([-jnp.sin(ar), jnp.sin(ar), -jnp.sin(ac), jnp.sin(ac)], axis=-1)
    pad = ((0, 0), (KR_OFF, LANE - KR_OFF - MLA_ROPE))
    return dict(cos_mla=jnp.pad(cos32, pad, constant_values=1.0), sin_mla=jnp.pad(sin32, pad),
                cos_diff=jnp.tile(cos32, (1, 2 * DIFF_HEADS)), sin_diff=jnp.tile(sin32, (1, 2 * DIFF_HEADS)))


def _hy_tables(seq_len):
    feats, window = _hy_static(seq_len)
    return dict(feats=feats, window=window, **_fft_tables(seq_len))


def _layer_weights(l, w_in_p, lb, W):
    def head_pad(w, width, per):
        k = w.shape[0]
        w = w.reshape(k, MLA_HEADS, per)[:, :, :width]
        return jnp.pad(w, ((0, 0), (0, 0), (0, LANE - width))).reshape(k, MLA_HEADS * LANE)

    w_ukv = W["mla_w_ukv"][l].reshape(MLA_KV_LORA, MLA_HEADS, MLA_NOPE + MLA_V)
    nope_g, rope_g = W["mla_nope_g"][l], W["mla_rope_g"][l]
    zeros32 = jnp.zeros((MLA_ROPE,), F32)
    zeros64 = jnp.zeros((MLA_NOPE,), F32)
    gq = jnp.tile(jnp.concatenate([nope_g[0], rope_g[0], zeros32]), MLA_HEADS).reshape(1, 512)
    gk = jnp.tile(jnp.concatenate([nope_g[1], zeros64]), MLA_HEADS).reshape(1, 512)
    gkr = jnp.concatenate([zeros64, rope_g[1], zeros32]).reshape(1, LANE)
    return dict(
        norm_g=W["norm_g"][l], w_in=w_in_p[l], w_out=W["w_out"][l].astype(BF16),
        qn_g=W["mla_q_norm_g"][l].reshape(1, -1),
        w_uq=head_pad(W["mla_w_uq"][l], MLA_NOPE + MLA_ROPE, MLA_NOPE + MLA_ROPE).astype(BF16),
        kvn_g=W["mla_kv_norm_g"][l].reshape(1, -1),
        w_uk=jnp.pad(w_ukv[:, :, :MLA_NOPE], ((0, 0), (0, 0), (0, LANE - MLA_NOPE))).reshape(MLA_KV_LORA, 512)
        .astype(BF16),
        w_uv=w_ukv[:, :, MLA_NOPE:].reshape(MLA_KV_LORA, BRANCH).astype(BF16),
        gq=gq, gk=gk, gkr=gkr,
        dgq=jnp.tile(W["diff_qk_g"][l, 0], 2 * DIFF_HEADS).reshape(1, BRANCH),
        dgk=jnp.tile(W["diff_qk_g"][l, 1], 2 * DIFF_HEADS).reshape(1, BRANCH),
        diff_lambda=W["diff_lambda"][l], subln_col=W["diff_subln_g"][l].reshape(2 * DIFF_HD, 1),
        lam_init=0.8 - 0.6 * math.exp(-0.3 * l),
        hgrn_lb=lb[:, l].reshape(2, 1, BRANCH),
        hgrn_out_g=jnp.tile(W["hgrn_out_g"][l], HGRN_HEADS).reshape(1, BRANCH),
        hy_conv_w=W["hy_conv_w"][l], hy_conv_b=W["hy_conv_b"][l].reshape(1, -1),
        hy_w1=jnp.pad(W["hy_w1"][l], ((0, LANE - HY_EMB), (0, 0))), hy_b1=W["hy_b1"][l].reshape(1, -1),
        hy_w2=W["hy_w2"][l], hy_b2=W["hy_b2"][l].reshape(1, -1),
        hy_w3=W["hy_w3"][l].reshape(HY_FH, HY_ORDER, 2, HY_CH).transpose(2, 0, 1, 3)
        .reshape(2, HY_FH, HY_ORDER * HY_CH),
        hy_freq=W["hy_sin_freq"][l], hy_bias=W["hy_bias"][l],
    )


def kernel(x_prompt, x_sample, cache_mla_ckv, cache_mla_krope, cache_diff_k, cache_diff_v, state_hgrn, c, c_ctx,
           norm_g, w_mod, b_mod, w_in, w_out, mla_q_norm_g, mla_w_uq, mla_kv_norm_g, mla_w_ukv, mla_nope_g,
           mla_rope_g, diff_qk_g, diff_lambda, diff_subln_g, hgrn_lb_logits, hgrn_out_g, hy_conv_w, hy_conv_b,
           hy_w1, hy_b1, hy_w2, hy_b2, hy_w3, hy_sin_freq, hy_bias):
    W = dict(norm_g=norm_g, w_out=w_out, mla_q_norm_g=mla_q_norm_g, mla_w_uq=mla_w_uq,
             mla_kv_norm_g=mla_kv_norm_g, mla_w_ukv=mla_w_ukv, mla_nope_g=mla_nope_g, mla_rope_g=mla_rope_g,
             diff_qk_g=diff_qk_g, diff_lambda=diff_lambda, diff_subln_g=diff_subln_g, hgrn_out_g=hgrn_out_g,
             hy_conv_w=hy_conv_w, hy_conv_b=hy_conv_b, hy_w1=hy_w1, hy_b1=hy_b1, hy_w2=hy_w2, hy_b2=hy_b2,
             hy_w3=hy_w3, hy_sin_freq=hy_sin_freq, hy_bias=hy_bias)
    bp, lp, _ = x_prompt.shape
    bs, ls, _ = x_sample.shape

    w_in_p = _reorder_in_cols(w_in.astype(BF16))
    cvecs = jnp.concatenate([c_ctx[None, :], c, jnp.zeros((8 - 1 - bs, D_MODEL), F32)], axis=0)
    mods = _mod_all(cvecs, w_mod, b_mod)
    lb = _hgrn_lb(hgrn_lb_logits)
    seg512, cnt512 = _mla_seg()
    hgrn_dd, hgrn_right = _hgrn_consts()
    consts = dict(seg512=seg512, cnt512=cnt512, seg32=_seg_const(BRANCH, DIFF_HD), seg64=_seg_const(BRANCH, HGRN_DK),
                  hgrn_dd=hgrn_dd, hgrn_right=hgrn_right)
    lws = [_layer_weights(l, w_in_p, lb, W) for l in range(DEPTH)]

    hy_p = _hy_tables(lp)
    y = x_prompt.reshape(bp * lp, D_MODEL)
    per_layer = []
    for l in range(DEPTH):
        mod = mods[l, 0:1].reshape(1, 3, D_MODEL)
        y, new = _layer(y, mod, lws[l], consts, bp, lp, None, None, hy_p)
        per_layer.append(new)
    y_prompt = y.reshape(bp, lp, D_MODEL)
    news = [jnp.stack([s[i] for s in per_layer], axis=1) for i in range(5)]

    hy_s = _hy_tables(ls)
    rope_tabs = _rope_tables(ls)
    y = x_sample.reshape(bs * ls, D_MODEL)
    for l in range(DEPTH):
        mod = mods[l, 1:1 + bs].reshape(bs, 3, D_MODEL)
        ctx = (cache_mla_ckv[:, l], cache_mla_krope[:, l], cache_diff_k[:, l], cache_diff_v[:, l], state_hgrn[:, l])
        y, _ = _layer(y, mod, lws[l], consts, bs, ls, ctx, rope_tabs, hy_s)
    y_sample = y.reshape(bs, ls, D_MODEL)

    return (y_prompt, y_sample, news[0], news[1], news[2], news[3], news[4])
```

```python
import functools
import math

import numpy as np
import jax
import jax.numpy as jnp
from jax import lax
from jax.experimental import pallas as pl
from jax.experimental.pallas import tpu as pltpu

F32 = jnp.float32
BF16 = jnp.bfloat16

D_MODEL = 1024
DEPTH = 4
GRID_W = 64
ROPE_BASE = 10000.0
EPS = 1e-6
BRANCH = 256
MLA_HEADS = 4
MLA_NOPE = 64
MLA_ROPE = 32
MLA_V = 64
MLA_Q_LORA = 256
MLA_KV_LORA = 128
MLA_SCALE = (MLA_NOPE + MLA_ROPE) ** -0.5
DIFF_HEADS = 4
DIFF_HD = 32
DIFF_SCALE = DIFF_HD ** -0.5
HGRN_HEADS = 4
HGRN_DK = 64
HGRN_CHUNK = 128
HGRN_LEVELS = 7
HGRN_ROWS = 2
HGRN_MM_LEVELS = 3
HY_CH = 256
HY_ORDER = 2
HY_EMB = 33
HY_BANDS = 16
HY_FH = 64
HY_DECAY_TARGET = 0.01
HY_FAST_DECAY = 0.3
HY_SLOW_DECAY = 1.5
IN_COLS = 4000

LANE = 128
LOG2E = math.log2(math.e)
VT_ROWS = 80
FFT_N2 = 128
FFT_BLOCK_BYTES = 2 * 1024 * 1024
MLA_AHEAD = 8
MLA_CHUNK = 512
DIFF_AHEAD = 6
DIFF_CHUNK = 256
MLA_SUB = 256
DIFF_SUB = 512
VMEM_LIMIT = 52 * 1024 * 1024

P_CQ, P_CKV, P_KR, P_DQ, P_DK, P_DV = 0, 256, 384, 512, 768, 1024
P_HQ, P_HZF, P_HZB, P_HI, P_HU, P_GATE = 1280, 1536, 1792, 2048, 2304, 3072
P_COLS = 4096
KR_OFF = 64


def _in_col_perm():
    src = np.full((P_COLS,), IN_COLS, np.int32)

    def put(dst, lo, n):
        src[dst:dst + n] = np.arange(lo, lo + n)

    put(P_CQ, 0, 256)
    put(P_CKV, 256, 128)
    put(P_KR + KR_OFF, 384, 32)
    put(P_GATE, 416, 256)
    put(P_DQ, 672, 256)
    put(P_DK, 928, 256)
    put(P_DV, 1184, 256)
    put(P_GATE + 256, 1440, 256)
    put(P_HQ, 1696, 256)
    put(P_HZF, 1952, 256)
    put(P_HZB, 2208, 256)
    put(P_HI, 2464, 256)
    put(P_GATE + 512, 2720, 256)
    put(P_HU, 2976, 768)
    put(P_GATE + 768, 3744, 256)
    return src


def _reorder_in_cols(w):
    src = _in_col_perm()
    pieces, lo = [], 0
    while lo < P_COLS:
        hi = lo + 1
        if src[lo] == IN_COLS:
            while hi < P_COLS and src[hi] == IN_COLS:
                hi += 1
            pieces.append(jnp.zeros(w.shape[:-1] + (hi - lo,), w.dtype))
        else:
            while hi < P_COLS and src[hi] == src[hi - 1] + 1:
                hi += 1
            pieces.append(w[..., int(src[lo]):int(src[lo]) + hi - lo])
        lo = hi
    return jnp.concatenate(pieces, axis=-1)


def _cparams(sem):
    return pltpu.CompilerParams(dimension_semantics=sem, vmem_limit_bytes=VMEM_LIMIT)


def _bdot(a, b):
    return jnp.dot(a.astype(BF16), b.astype(BF16), preferred_element_type=F32)


def _nt(a, b):
    return lax.dot_general(a.astype(BF16), b.astype(BF16), (((1,), (1,)), ((), ())), preferred_element_type=F32)


def _split2(a):
    hi = a.astype(BF16)
    lo = (a - hi.astype(F32)).astype(BF16)
    return hi, lo


def _dot3(a, b):
    ah, al = _split2(a)
    bh, bl = _split2(b)
    d = functools.partial(jnp.dot, preferred_element_type=F32)
    return d(ah, bh) + d(ah, bl) + d(al, bh)


def _segsum(v, seg):
    return jnp.dot(v.astype(BF16), seg, preferred_element_type=F32)


def _rms(x, g):
    return x * lax.rsqrt(jnp.mean(x * x, axis=-1, keepdims=True) + EPS) * g


def _swap8(x):
    w = x.shape[-1]
    lane = lax.broadcasted_iota(jnp.int32, x.shape, x.ndim - 1)
    up = pltpu.roll(x, w - 8, x.ndim - 1)
    dn = pltpu.roll(x, 8, x.ndim - 1)
    return jnp.where((lane & 15) < 8, up, dn)


def _tile_lanes(x, n):
    return x if n == 1 else jnp.concatenate([x] * n, axis=-1)


def _mod_kernel(c_ref, w_ref, b_ref, o_ref):
    c = c_ref[...]
    o_ref[0] = _dot3(c * jax.nn.sigmoid(c), w_ref[0]) + b_ref[0]


def _mod_all(cvecs, w_mod, b_mod):
    nt = 3
    return pl.pallas_call(
        _mod_kernel,
        grid=(DEPTH, nt),
        in_specs=[pl.BlockSpec((8, D_MODEL), lambda l, j: (0, 0)),
                  pl.BlockSpec((1, D_MODEL, D_MODEL), lambda l, j: (l, 0, j)),
                  pl.BlockSpec((1, 1, D_MODEL), lambda l, j: (l, 0, j))],
        out_specs=pl.BlockSpec((1, 8, D_MODEL), lambda l, j: (l, 0, j)),
        out_shape=jax.ShapeDtypeStruct((DEPTH, 8, 3 * D_MODEL), F32),
        compiler_params=_cparams(("arbitrary", "arbitrary")),
        name="mod",
    )(cvecs, w_mod, b_mod.reshape(DEPTH, 1, 3 * D_MODEL))


def _lb_kernel(x_ref, o_ref):
    x = x_ref[...]
    rows = [x[l:l + 1, :] for l in range(DEPTH)]
    m = functools.reduce(jnp.maximum, rows)
    e = [jnp.exp(r - m) for r in rows]
    tot = functools.reduce(lambda a, b: a + b, e)
    acc = jnp.zeros_like(tot)
    o_ref[0:1, :] = acc
    for l in range(1, DEPTH):
        acc = acc + e[l] / tot
        o_ref[l:l + 1, :] = acc


def _hgrn_lb(logits):
    flat = logits.transpose(1, 0, 2).reshape(DEPTH, 2 * BRANCH)
    lb = pl.pallas_call(
        _lb_kernel,
        out_shape=jax.ShapeDtypeStruct(flat.shape, F32),
        name="hgrn_lb",
    )(flat)
    return lb.reshape(DEPTH, 2, BRANCH).transpose(1, 0, 2)


def _inproj_kernel(x_ref, mod_ref, g_ref, w_ref, p_ref):
    h = _rms(x_ref[...], g_ref[...]) * (1.0 + mod_ref[0, 1:2, :]) + mod_ref[0, 0:1, :]
    p_ref[...] = jnp.dot(h.astype(BF16), w_ref[...], preferred_element_type=F32)


def _inproj(x2, mod, norm_g, w_in_p, seq_len):
    n = x2.shape[0]
    tm = 256
    per_batch = mod.shape[0] > 1
    tiles_per_seq = seq_len // tm
    mod_idx = (lambda i: (i // tiles_per_seq, 0, 0)) if per_batch else (lambda i: (0, 0, 0))
    return pl.pallas_call(
        _inproj_kernel,
        grid=(n // tm,),
        in_specs=[pl.BlockSpec((tm, D_MODEL), lambda i: (i, 0)),
                  pl.BlockSpec((1, 3, D_MODEL), mod_idx),
                  pl.BlockSpec((1, D_MODEL), lambda i: (0, 0)),
                  pl.BlockSpec((D_MODEL, P_COLS), lambda i: (0, 0))],
        out_specs=pl.BlockSpec((tm, P_COLS), lambda i: (i, 0)),
        out_shape=jax.ShapeDtypeStruct((n, P_COLS), F32),
        compiler_params=_cparams(("parallel",)),
        name="inproj",
    )(x2, mod, norm_g.reshape(1, D_MODEL), w_in_p)


def _mla_seg():
    sid = np.zeros((512,), np.int32)
    cnt = np.ones((512,), np.float32)
    for h in range(MLA_HEADS):
        b = 128 * h
        sid[b:b + 64] = 3 * h
        sid[b + 64:b + 96] = 3 * h + 1
        sid[b + 96:b + 128] = 3 * h + 2
        cnt[b:b + 64] = 1.0 / 64
        cnt[b + 64:b + 128] = 1.0 / 32
    seg = (sid[:, None] == sid[None, :]).astype(np.float32)
    return jnp.asarray(seg, BF16), jnp.asarray(cnt.reshape(1, 512))


def _mla_q_kernel(rope, cq_ref, ckv_ref, kr_ref, qng_ref, wuq_ref, kvg_ref, gq_ref, gkr_ref, seg_ref, cnt_ref,
                  *rest):
    if rope:
        cos_ref, sin_ref, q_ref, ckvn_ref, krp_ref = rest
    else:
        q_ref, ckvn_ref, krp_ref = rest
    cqn = _rms(cq_ref[...], qng_ref[...])
    q = _bdot(cqn, wuq_ref[...])
    ss = _segsum(q * q, seg_ref[...]) * cnt_ref[...]
    qn = q * lax.rsqrt(ss + EPS) * gq_ref[...]
    ckvn_ref[...] = _rms(ckv_ref[...], kvg_ref[...])
    kr = kr_ref[...]
    krn = kr * lax.rsqrt(jnp.sum(kr * kr, axis=-1, keepdims=True) * (1.0 / MLA_ROPE) + EPS) * gkr_ref[...]
    if rope:
        cos, sin = cos_ref[...], sin_ref[...]
        qn = qn * _tile_lanes(cos, MLA_HEADS) + _swap8(qn) * _tile_lanes(sin, MLA_HEADS)
        krn = krn * cos + _swap8(krn) * sin
    q_ref[...] = (qn * (MLA_SCALE * LOG2E)).astype(BF16)
    krp_ref[...] = krn


def _mla_q(p, lw, consts, seq_len, rope_tabs):
    n = p.shape[0]
    tm = min(512, seq_len)
    rope = rope_tabs is not None
    full = lambda shape: pl.BlockSpec(shape, lambda i: (0,) * len(shape))
    in_specs = [pl.BlockSpec((tm, 256), lambda i: (i, P_CQ // 256)),
                pl.BlockSpec((tm, 128), lambda i: (i, P_CKV // 128)),
                pl.BlockSpec((tm, 128), lambda i: (i, P_KR // 128)),
                full((1, 256)), full((256, 512)), full((1, 128)), full((1, 512)), full((1, 128)),
                full((512, 512)), full((1, 512))]
    args = [p, p, p, lw["qn_g"], lw["w_uq"], lw["kvn_g"], lw["gq"], lw["gkr"], consts["seg512"], consts["cnt512"]]
    if rope:
        tps = seq_len // tm
        in_specs += [pl.BlockSpec((tm, 128), lambda i: (i % tps, 0))] * 2
        args += [rope_tabs["cos_mla"], rope_tabs["sin_mla"]]
    return pl.pallas_call(
        functools.partial(_mla_q_kernel, rope),
        grid=(n // tm,),
        in_specs=in_specs,
        out_specs=[pl.BlockSpec((tm, 512), lambda i: (i, 0)),
                   pl.BlockSpec((tm, 128), lambda i: (i, 0)),
                   pl.BlockSpec((tm, 128), lambda i: (i, 0))],
        out_shape=[jax.ShapeDtypeStruct((n, 512), BF16),
                   jax.ShapeDtypeStruct((n, 128), F32),
                   jax.ShapeDtypeStruct((n, 128), F32)],
        compiler_params=_cparams(("parallel",)),
        name="mla_q",
    )(*args)


def _mla_kv_kernel(ckvn_ref, krp_ref, wuk_ref, wuv_ref, gk_ref, seg_ref, cnt_ref, k_ref, v_ref):
    c = ckvn_ref[...].astype(BF16)
    kn = jnp.dot(c, wuk_ref[...], preferred_element_type=F32)
    ss = _segsum(kn * kn, seg_ref[...]) * cnt_ref[...]
    k = kn * lax.rsqrt(ss + EPS) * gk_ref[...] + _tile_lanes(krp_ref[...], MLA_HEADS)
    k_ref[...] = k.astype(BF16)
    v_ref[...] = jnp.dot(c, wuv_ref[...], preferred_element_type=F32).astype(BF16)


def _mla_kv(ckvn, krp, lw, consts):
    n = ckvn.shape[0]
    tm = 512
    full = lambda shape: pl.BlockSpec(shape, lambda i: (0,) * len(shape))
    return pl.pallas_call(
        _mla_kv_kernel,
        grid=(n // tm,),
        in_specs=[pl.BlockSpec((tm, 128), lambda i: (i, 0)), pl.BlockSpec((tm, 128), lambda i: (i, 0)),
                  full((128, 512)), full((128, 256)), full((1, 512)), full((512, 512)), full((1, 512))],
        out_specs=[pl.BlockSpec((tm, 512), lambda i: (i, 0)), pl.BlockSpec((tm, 256), lambda i: (i, 0))],
        out_shape=[jax.ShapeDtypeStruct((n, 512), BF16), jax.ShapeDtypeStruct((n, 256), BF16)],
        compiler_params=_cparams(("parallel",)),
        name="mla_kv",
    )(ckvn, krp, lw["w_uk"], lw["w_uv"], lw["gk"], consts["seg512"], consts["cnt512"])


def _softmax_pv(qs, k_ref, vt_ref, key_chunk, sub_rows, n_ahead):
    lk = k_ref.shape[1]
    nch = lk // key_chunk
    nq = len(qs)
    sub = min(sub_rows, key_chunk)
    nsub = key_chunk // sub

    def scores(c, u):
        lo = c * key_chunk + u * sub
        ks = k_ref[0, lo:lo + sub, :]
        return [_nt(ks, q) for q in qs]

    def chunk_max(s_chunk, j):
        mc = functools.reduce(jnp.maximum, [s_chunk[u][j] for u in range(nsub)])
        return jnp.max(mc, axis=0, keepdims=True)

    s_buf = {c: [scores(c, u) for u in range(nsub)] for c in range(min(n_ahead, nch))}
    m = [None] * nq
    acc = [None] * nq
    m_new = [chunk_max(s_buf[0], j) for j in range(nq)]
    for c in range(nch):
        s_cur = s_buf.pop(c)
        ahead = c + n_ahead
        if ahead < nch:
            s_buf[ahead] = []
        pv = [None] * nq
        for u in range(nsub):
            if ahead < nch:
                s_buf[ahead].append(scores(ahead, u))
            lo = c * key_chunk + u * sub
            vs = vt_ref[0, 0, :, lo:lo + sub]
            for j in range(nq):
                part = jnp.dot(vs, jnp.exp2(s_cur[u][j] - m_new[j]).astype(BF16), preferred_element_type=F32)
                pv[j] = part if pv[j] is None else pv[j] + part
        for j in range(nq):
            acc[j] = pv[j] if c == 0 else acc[j] * jnp.exp2(m[j] - m_new[j]) + pv[j]
            m[j] = m_new[j]
        if c + 1 < nch:
            m_new = [jnp.maximum(m[j], chunk_max(s_buf[c + 1], j)) for j in range(nq)]
    return acc


def _mla_attn_kernel(key_chunk, q_ref, k_ref, vt_ref, o_ref):
    (acc,) = _softmax_pv([q_ref[...]], k_ref, vt_ref, key_chunk, MLA_SUB, MLA_AHEAD)
    o_ref[0] = acc[0:MLA_V] / acc[MLA_V:MLA_V + 1]


def _key_chunk(lk, rows=512):
    return rows if lk % rows == 0 else lk


def _vt_kernel(v_ref, o_ref):
    vt = v_ref[0].astype(F32).T
    tm = vt.shape[1]
    row = lax.broadcasted_iota(jnp.int32, (VT_ROWS - 64, tm), 0)
    extra = jnp.where(row == 0, 1.0, 0.0).astype(BF16)
    for h in range(4):
        o_ref[0, h, 0:64, :] = vt[64 * h:64 * (h + 1)].astype(BF16)
        o_ref[0, h, 64:VT_ROWS, :] = extra


def _vt_with_ones(v3):
    b, lk, width = v3.shape
    tm = 1536 if lk % 1536 == 0 else lk
    return pl.pallas_call(
        _vt_kernel,
        grid=(b, lk // tm),
        in_specs=[pl.BlockSpec((1, tm, width), lambda i, j: (i, j, 0))],
        out_specs=pl.BlockSpec((1, 4, VT_ROWS, tm), lambda i, j: (i, 0, 0, j)),
        out_shape=jax.ShapeDtypeStruct((b, 4, VT_ROWS, lk), BF16),
        compiler_params=_cparams(("parallel", "parallel")),
        name="vt_ones",
    )(v3)


def _mla_attn(q, k, vt, batch, seq_len):
    lk = k.shape[1]
    tq = 256
    nq = seq_len // tq
    return pl.pallas_call(
        functools.partial(_mla_attn_kernel, _key_chunk(lk, MLA_CHUNK)),
        grid=(batch, MLA_HEADS, nq),
        in_specs=[pl.BlockSpec((tq, 128), lambda b, h, i: (b * nq + i, h)),
                  pl.BlockSpec((1, lk, 128), lambda b, h, i: (b, 0, h)),
                  pl.BlockSpec((1, 1, VT_ROWS, lk), lambda b, h, i: (b, h, 0, 0))],
        out_specs=pl.BlockSpec((1, MLA_V, tq), lambda b, h, i: (b, h, i)),
        out_shape=jax.ShapeDtypeStruct((batch, BRANCH, seq_len), F32),
        compiler_params=_cparams(("parallel", "parallel", "arbitrary")),
        name="mla_attn",
    )(q, k, vt)


def _seg_const(width, seg):
    sid = np.arange(width) // seg
    return jnp.asarray((sid[:, None] == sid[None, :]).astype(np.float32), BF16)


def _diff_prep_kernel(rope, dq_ref, dk_ref, gq_ref, gk_ref, seg_ref, *rest):
    if rope:
        cos_ref, sin_ref, q_ref, k_ref, kf_ref = rest
    else:
        q_ref, k_ref, kf_ref = rest
    seg = seg_ref[...]

    def norm(x, g):
        ss = _segsum(x * x, seg) * (1.0 / DIFF_HD)
        return x * lax.rsqrt(ss + EPS) * g

    q = norm(dq_ref[...], gq_ref[...])
    k = norm(dk_ref[...], gk_ref[...])
    kf_ref[...] = k
    if rope:
        cos, sin = cos_ref[...], sin_ref[...]
        q = q * cos + _swap8(q) * sin
        k = k * cos + _swap8(k) * sin
    q_ref[...] = (q * (DIFF_SCALE * LOG2E)).astype(BF16)
    k_ref[...] = k.astype(BF16)


def _diff_prep(p, lw, consts, seq_len, rope_tabs):
    n = p.shape[0]
    tm = min(512, seq_len)
    rope = rope_tabs is not None
    full = lambda shape: pl.BlockSpec(shape, lambda i: (0,) * len(shape))
    in_specs = [pl.BlockSpec((tm, 256), lambda i: (i, P_DQ // 256)),
                pl.BlockSpec((tm, 256), lambda i: (i, P_DK // 256)),
                full((1, 256)), full((1, 256)), full((256, 256))]
    args = [p, p, lw["dgq"], lw["dgk"], consts["seg32"]]
    if rope:
        tps = seq_len // tm
        in_specs += [pl.BlockSpec((tm, 256), lambda i: (i % tps, 0))] * 2
        args += [rope_tabs["cos_diff"], rope_tabs["sin_diff"]]
    blk = pl.BlockSpec((tm, 256), lambda i: (i, 0))
    return pl.pallas_call(
        functools.partial(_diff_prep_kernel, rope),
        grid=(n // tm,),
        in_specs=in_specs,
        out_specs=[blk, blk, blk],
        out_shape=[jax.ShapeDtypeStruct((n, 256), BF16), jax.ShapeDtypeStruct((n, 256), BF16),
                   jax.ShapeDtypeStruct((n, 256), F32)],
        compiler_params=_cparams(("parallel",)),
        name="diff_prep",
    )(*args)


def _diff_attn_kernel(lam_init, key_chunk, q_ref, k_ref, vt_ref, lp_ref, g_ref, o_ref):
    h = pl.program_id(1)
    q = q_ref[...]
    lane = lax.broadcasted_iota(jnp.int32, q.shape, 1)
    base = (h % 2) * 64
    zero = jnp.zeros_like(q)

    def map_query(j):
        lo = base + 32 * j
        return jnp.where((lane >= lo) & (lane < lo + 32), q, zero)

    acc0, acc1 = _softmax_pv([map_query(0), map_query(1)], k_ref, vt_ref, key_chunk, DIFF_SUB, DIFF_AHEAD)
    lp = lp_ref[...]
    lam = (jnp.exp(jnp.sum(lp[0:1] * lp[1:2], axis=1, keepdims=True))
           - jnp.exp(jnp.sum(lp[2:3] * lp[3:4], axis=1, keepdims=True)) + lam_init)
    o = acc0[0:64] / acc0[64:65] - lam * (acc1[0:64] / acc1[64:65])
    ms = jnp.mean(o * o, axis=0, keepdims=True)
    o_ref[0] = o * lax.rsqrt(ms + EPS) * g_ref[...] * (1.0 - lam_init)


def _diff_attn(q, k, vt, lp, g_col, lam_init, batch, seq_len):
    lk = k.shape[1]
    tq = 256
    nq = seq_len // tq
    return pl.pallas_call(
        functools.partial(_diff_attn_kernel, lam_init, _key_chunk(lk, DIFF_CHUNK)),
        grid=(batch, DIFF_HEADS, nq),
        in_specs=[pl.BlockSpec((tq, 128), lambda b, h, i: (b * nq + i, h // 2)),
                  pl.BlockSpec((1, lk, 128), lambda b, h, i: (b, 0, h // 2)),
                  pl.BlockSpec((1, 1, VT_ROWS, lk), lambda b, h, i: (b, h, 0, 0)),
                  pl.BlockSpec((4, DIFF_HD), lambda b, h, i: (0, 0)),
                  pl.BlockSpec((64, 1), lambda b, h, i: (0, 0))],
        out_specs=pl.BlockSpec((1, 64, tq), lambda b, h, i: (b, h, i)),
        out_shape=jax.ShapeDtypeStruct((batch, BRANCH, seq_len), F32),
        compiler_params=_cparams(("parallel", "parallel", "arbitrary")),
        name="diff_attn",
    )(q, k, vt, lp, g_col)


def _hgrn_consts():
    c = HGRN_CHUNK
    t = np.arange(c)
    low = (t[None, :] <= t[:, None]).astype(np.float32)
    blocks = []
    for j in range(HGRN_MM_LEVELS):
        m = 1 << j
        rho = (t // (2 * m)) * (2 * m) + m - 1
        sign = np.where((t // m) % 2 == 1, 1.0, -1.0)[:, None]
        blocks.append(sign * (low - (t[None, :] <= rho[:, None]).astype(np.float32)))
    blocks.append(low)
    fwd = np.concatenate(blocks, axis=0)
    bwd = np.concatenate([b[::-1, ::-1] for b in blocks], axis=0)
    right = np.stack([(t // (1 << j)) % 2 for j in range(HGRN_LEVELS)]).astype(np.float32)
    right = np.stack([right, right[:, ::-1]])
    right = np.broadcast_to(right[..., None], right.shape + (BRANCH,))
    return jnp.asarray(np.stack([fwd, bwd]), BF16), jnp.asarray(right, F32)


def _hgrn_kernel(nc, nb, qf_ref, zf_ref, vf_ref, qb_ref, zb_ref, vb_ref, lb_ref, dd_ref, rm_ref, s0_ref,
                 of_ref, ob_ref, sout_ref, st_ref):
    c = HGRN_CHUNK
    ci = pl.program_id(1)
    chains = [(bi, d) for bi in range(nb) for d in (0, 1)]
    ids = range(len(chains))

    @pl.when(ci == 0)
    def _():
        st_ref[...] = s0_ref[...]

    lane = lax.broadcasted_iota(jnp.int32, (1, BRANCH), 1)
    head_masks = [(lane >= HGRN_DK * h) & (lane < HGRN_DK * (h + 1)) for h in range(HGRN_HEADS)]
    t_idx = lax.broadcasted_iota(jnp.int32, (c, HGRN_HEADS * c), 0)
    s_idx = lax.broadcasted_iota(jnp.int32, (c, HGRN_HEADS * c), 1) & (c - 1)
    pair_xor = t_idx ^ s_idx

    def stack_heads(x):
        xb = x.astype(BF16)
        zero = jnp.zeros_like(xb)
        return jnp.concatenate([jnp.where(hm, xb, zero) for hm in head_masks], axis=0)

    q_refs, z_refs, v_refs = (qf_ref, qb_ref), (zf_ref, zb_ref), (vf_ref, vb_ref)
    q = [q_refs[d][bi] for bi, d in chains]
    v = [v_refs[d][bi] for bi, d in chains]
    z = [z_refs[d][bi] for bi, d in chains]
    lb = [lb_ref[d] for _, d in chains]
    g = [jnp.log(lb[i] + (1.0 - lb[i]) * jax.nn.sigmoid(z[i])) for i in ids]
    kk = [(1.0 - lb[i]) * jax.nn.sigmoid(-z[i]) for i in ids]
    sums = []
    for i in ids:
        gh, gl = _split2(g[i])
        dd = dd_ref[chains[i][1]]
        sums.append(jnp.dot(dd, gh, preferred_element_type=F32) + jnp.dot(dd, gl, preferred_element_type=F32))
    b = [sums[i][HGRN_MM_LEVELS * c:] for i in ids]
    b_tot = [b[i][c - 1:c] if chains[i][1] == 0 else b[i][0:1] for i in ids]

    def neg_abs_decay(i, j):
        if j < HGRN_MM_LEVELS:
            return sums[i][j * c:(j + 1) * c]
        m = 1 << j
        off = m - 1 if chains[i][1] == 0 else m
        ref = jnp.concatenate([jnp.broadcast_to(b[i][g0 + off:g0 + off + 1], (2 * m, BRANCH))
                               for g0 in range(0, c, 2 * m)], axis=0)
        return -jnp.abs(b[i] - ref)

    a = [None] * len(chains)
    for j in reversed(range(HGRN_LEVELS)):
        same_group = pair_xor < (2 << j)
        for i in ids:
            e = jnp.exp(neg_abs_decay(i, j))
            eq = e * rm_ref[chains[i][1], j]
            qt = q[i] * eq
            kt = kk[i] * (e - eq)
            lvl = _nt(qt, stack_heads(kt))
            a[i] = lvl if a[i] is None else jnp.where(same_group, lvl, a[i])
    diagonal = pair_xor == 0
    for i in ids:
        a[i] = jnp.where(diagonal, _nt(q[i], stack_heads(kk[i])), a[i])

    outs = (of_ref, ob_ref)
    for i in ids:
        bi, d = chains[i]
        o = jnp.dot(a[i].astype(BF16), stack_heads(v[i]), preferred_element_type=F32)
        outs[d][bi] = o + _nt(q[i] * jnp.exp(b[i]), st_ref[bi, d])

    r2 = lax.broadcasted_iota(jnp.int32, (BRANCH, BRANCH), 0) // HGRN_DK
    c2 = lax.broadcasted_iota(jnp.int32, (BRANCH, BRANCH), 1) // HGRN_DK
    for i in ids:
        bi, d = chains[i]
        kd = kk[i] * jnp.exp(b_tot[i] - b[i])
        upd = lax.dot_general(v[i].astype(BF16), kd.astype(BF16), (((0,), (0,)), ((), ())),
                              preferred_element_type=F32)
        st_new = st_ref[bi, d] * jnp.exp(b_tot[i]) + jnp.where(r2 == c2, upd, 0.0)
        st_ref[bi, d] = st_new

        @pl.when(ci == nc - 1)
        def _(bi=bi, d=d, st_new=st_new):
            sout_ref[bi, d] = st_new


def _hgrn(p, lb_l, dd, rm, st0, batch, seq_len):
    n = p.shape[0]
    c = HGRN_CHUNK
    nc = seq_len // c
    nb = HGRN_ROWS if batch % HGRN_ROWS == 0 else 1
    p3 = p.reshape(batch, seq_len, P_COLS)
    fwd = lambda col: pl.BlockSpec((nb, c, 256), lambda b, i: (b, i, col))
    bwd = lambda col: pl.BlockSpec((nb, c, 256), lambda b, i: (b, nc - 1 - i, col))
    whole = lambda shape: pl.BlockSpec(shape, lambda b, i: (0,) * len(shape))
    state = pl.BlockSpec((nb, 2, 256, 256), lambda b, i: (b, 0, 0, 0))
    o_f, o_b, st = pl.pallas_call(
        functools.partial(_hgrn_kernel, nc, nb),
        grid=(batch // nb, nc),
        in_specs=[fwd(P_HQ // 256), fwd(P_HZF // 256), fwd(P_HI // 256),
                  bwd(P_HQ // 256), bwd(P_HZB // 256), bwd(P_HI // 256),
                  whole((2, 1, 256)), whole((2, (HGRN_MM_LEVELS + 1) * c, c)),
                  whole((2, HGRN_LEVELS, c, 256)), state],
        out_specs=[pl.BlockSpec((nb, c, 256), lambda b, i: (b, i, 0)),
                   pl.BlockSpec((nb, c, 256), lambda b, i: (b, nc - 1 - i, 0)), state],
        out_shape=[jax.ShapeDtypeStruct((batch, seq_len, 256), F32),
                   jax.ShapeDtypeStruct((batch, seq_len, 256), F32),
                   jax.ShapeDtypeStruct((batch, 2, 256, 256), F32)],
        scratch_shapes=[pltpu.VMEM((nb, 2, 256, 256), F32)],
        compiler_params=_cparams(("parallel", "arbitrary")),
        name="hgrn",
    )(p3, p3, p3, p3, p3, p3, lb_l, dd, rm, st0)
    return o_f.reshape(n, 256), o_b.reshape(n, 256), st


def _hy_conv3_kernel(tiles_per_seq, above_ref, cur_ref, below_ref, w_ref, b_ref, v_ref, x1_ref, x2_ref):
    i = pl.program_id(0)
    cur = cur_ref[...]
    tm = cur.shape[0]
    first = (i % tiles_per_seq) == 0
    last = (i % tiles_per_seq) == tiles_per_seq - 1
    above = jnp.where(first, 0.0, above_ref[7:8, :])
    below = jnp.where(last, 0.0, below_ref[0:1, :])
    row = lax.broadcasted_iota(jnp.int32, (tm, 1), 0)
    prev = jnp.where(row == 0, above, pltpu.roll(cur, 1, 0))
    nxt = jnp.where(row == tm - 1, below, pltpu.roll(cur, tm - 1, 0))
    w = w_ref[...]
    u = prev * w[0:1] + cur * w[1:2] + nxt * w[2:3] + b_ref[...]
    v_ref[...] = u[:, 0:256]
    x1_ref[...] = u[:, 256:512]
    x2_ref[...] = u[:, 512:768]


def _hy_conv3(p, w, b, seq_len):
    n = p.shape[0]
    tm = min(512, seq_len)
    nt = n // tm
    g = tm // 8
    col = P_HU // 768
    oblk = pl.BlockSpec((tm, 256), lambda i: (i, 0))
    return pl.pallas_call(
        functools.partial(_hy_conv3_kernel, seq_len // tm),
        grid=(nt,),
        in_specs=[pl.BlockSpec((8, 768), lambda i: (jnp.maximum(i * g - 1, 0), col)),
                  pl.BlockSpec((tm, 768), lambda i: (i, col)),
                  pl.BlockSpec((8, 768), lambda i: (jnp.minimum((i + 1) * g, nt * g - 1), col)),
                  pl.BlockSpec((3, 768), lambda i: (0, 0)), pl.BlockSpec((1, 768), lambda i: (0, 0))],
        out_specs=[oblk, oblk, oblk],
        out_shape=[jax.ShapeDtypeStruct((n, 256), F32)] * 3,
        compiler_params=_cparams(("parallel",)),
        name="hy_conv3",
    )(p, p, p, w, b)


def _hy_filter_kernel(feat_ref, w1_ref, b1_ref, w2_ref, b2_ref, w3_ref, fr_ref, win_ref, o_ref):
    fr = fr_ref[...]
    h = jnp.sin(fr[0:1] * (_dot3(feat_ref[...], w1_ref[...]) + b1_ref[...]))
    h = jnp.sin(fr[1:2] * (_dot3(h, w2_ref[...]) + b2_ref[...]))
    o_ref[...] = _dot3(h, w3_ref[0]) * _tile_lanes(win_ref[...], HY_ORDER)


def _hy_filter(feats2, window2, w1p, b1, w2, b2, w3d, freq):
    l2 = feats2.shape[0]
    ln = l2 // 2
    tm = min(512, ln)
    full = lambda shape: pl.BlockSpec(shape, lambda i: (0,) * len(shape))
    return pl.pallas_call(
        _hy_filter_kernel,
        grid=(l2 // tm,),
        in_specs=[pl.BlockSpec((tm, LANE), lambda i: (i, 0)),
                  full((LANE, HY_FH)), full((1, HY_FH)), full((HY_FH, HY_FH)), full((1, HY_FH)),
                  pl.BlockSpec((1, HY_FH, HY_ORDER * HY_CH), lambda i: (i // (ln // tm), 0, 0)), full((2, HY_FH)),
                  pl.BlockSpec((tm, HY_CH), lambda i: (i, 0))],
        out_specs=pl.BlockSpec((tm, HY_ORDER * HY_CH), lambda i: (i, 0)),
        out_shape=jax.ShapeDtypeStruct((l2, HY_ORDER * HY_CH), F32),
        compiler_params=_cparams(("parallel",)),
        name="hy_filter",
    )(feats2, w1p, b1, w2, b2, w3d, freq, window2)


def _fft_blocking(nb, n1, n2, ch):
    per_batch = n1 * n2 * ch * 4
    if per_batch <= FFT_BLOCK_BYTES:
        bb = max(1, min(nb, FFT_BLOCK_BYTES // per_batch))
        while nb % bb:
            bb -= 1
        return bb, n2
    rt = n2
    while n1 * rt * ch * 4 > FFT_BLOCK_BYTES and rt > 8:
        rt //= 2
    return 1, rt


def _fft_a_kernel(f_ref, x_ref, o_ref):
    f = f_ref[...]
    for b in range(x_ref.shape[0]):
        x = x_ref[b].astype(BF16)
        o_ref[b] = jnp.einsum("kn,nrc->krc", f, x, preferred_element_type=F32).astype(o_ref.dtype)


def _fft_a(fa, x4, out_dtype):
    nb, n1, n2, ch = x4.shape
    r = fa.shape[0]
    bb, rt = _fft_blocking(nb, n1, n2, ch)
    return pl.pallas_call(
        _fft_a_kernel,
        grid=(nb // bb, n2 // rt),
        in_specs=[pl.BlockSpec((r, n1), lambda b, i: (0, 0)),
                  pl.BlockSpec((bb, n1, rt, ch), lambda b, i: (b, 0, i, 0))],
        out_specs=pl.BlockSpec((bb, r, rt, ch), lambda b, i: (b, 0, i, 0)),
        out_shape=jax.ShapeDtypeStruct((nb, r, n2, ch), out_dtype),
        compiler_params=_cparams(("parallel", "parallel")),
        name="fft_a",
    )(fa, x4)


def _fft_b_kernel(with_inverse, mf_ref, *rest):
    if with_inverse:
        mi_ref, a_ref, h_ref, o_ref = rest
    else:
        a_ref, o_ref = rest
    half = FFT_N2
    for kk in range(a_ref.shape[2]):
        for b in range(a_ref.shape[0]):
            a = jnp.concatenate([a_ref[b, 0, kk], a_ref[b, 1, kk]], axis=0)
            x = jnp.dot(mf_ref[kk], a.astype(BF16), preferred_element_type=F32)
            if with_inverse:
                xr, xi = x[:half], x[half:]
                hr, hi = h_ref[0, kk], h_ref[1, kk]
                y = jnp.concatenate([xr * hr - xi * hi, xr * hi + xi * hr], axis=0)
                x = jnp.dot(mi_ref[kk], y.astype(BF16), preferred_element_type=F32)
            o_ref[b, 0, kk] = x[:half].astype(o_ref.dtype)
            o_ref[b, 1, kk] = x[half:].astype(o_ref.dtype)


def _fft_b(mf, mi, a5, spec, order):
    nb, _, k1n, n2, ch = a5.shape
    ks = 3 if (k1n % 3 == 0 and nb * ch <= 1024) else 1
    mat = pl.BlockSpec((ks, 2 * n2, 2 * n2), lambda k: (k, 0, 0))
    blk = pl.BlockSpec((nb, 2, ks, n2, ch), lambda k: (0, 0, k, 0, 0))
    if spec is None:
        in_specs, args = [mat, blk], [mf, a5]
    else:
        in_specs = [mat, mat, blk, pl.BlockSpec((2, ks, n2, ch), lambda k: (0, k, 0, order))]
        args = [mf, mi, a5, spec]
    return pl.pallas_call(
        functools.partial(_fft_b_kernel, spec is not None),
        grid=(k1n // ks,),
        in_specs=in_specs,
        out_specs=blk,
        out_shape=jax.ShapeDtypeStruct(a5.shape, F32 if spec is None else BF16),
        compiler_params=_cparams(("parallel",)),
        name="fft_b",
    )(*args)


def _fft_a_inv_kernel(g_ref, p_ref, x_ref, z_ref, bias_ref, o_ref):
    g = g_ref[...]
    for b in range(p_ref.shape[0]):
        conv = jnp.einsum("nk,krc->nrc", g, p_ref[b], preferred_element_type=F32)
        o_ref[b] = x_ref[b] * (conv + z_ref[b] * bias_ref[...])


def _fft_a_inv(g, p4, xg4, z4, bias):
    nb, n1, n2, ch = z4.shape
    r = g.shape[1]
    bb, rt = _fft_blocking(nb, n1, n2, ch)
    blk = pl.BlockSpec((bb, n1, rt, ch), lambda b, i: (b, 0, i, 0))
    return pl.pallas_call(
        _fft_a_inv_kernel,
        grid=(nb // bb, n2 // rt),
        in_specs=[pl.BlockSpec((n1, r), lambda b, i: (0, 0)),
                  pl.BlockSpec((bb, r, rt, ch), lambda b, i: (b, 0, i, 0)), blk, blk,
                  pl.BlockSpec((1, 1, ch), lambda b, i: (0, 0, 0))],
        out_specs=blk,
        out_shape=jax.ShapeDtypeStruct(z4.shape, F32),
        compiler_params=_cparams(("parallel", "parallel")),
        name="fft_a_inv",
    )(g, p4, xg4, z4, bias.reshape(1, 1, ch))


def _fft_tables(ln):
    n = 2 * ln
    n1t = n // FFT_N2
    k1n = n1t // 2 + 1
    kk = np.arange(k1n)

    def stage_a(n1_in):
        ang = 2.0 * np.pi * ((kk[:, None] * np.arange(n1_in)[None, :]) % n1t) / n1t
        return jnp.asarray(np.concatenate([np.cos(ang), -np.sin(ang)], axis=0), BF16)

    n1o = n1t // 2
    ang = 2.0 * np.pi * ((np.arange(n1o)[:, None] * kk[None, :]) % n1t) / n1t
    edge = (kk == 0) | (kk == n1t // 2)
    ck = np.where(edge, 1.0, 2.0) / n
    g = jnp.asarray(np.concatenate([ck * np.cos(ang), -ck * np.where(edge, 0.0, np.sin(ang))], axis=1), BF16)

    k1 = jnp.arange(k1n, dtype=jnp.int32)[:, None, None]
    k2 = jnp.arange(FFT_N2, dtype=jnp.int32)[None, :, None]
    n2 = jnp.arange(FFT_N2, dtype=jnp.int32)[None, None, :]
    th = (2.0 * math.pi / n) * ((n2 * (k1 + n1t * k2)) % n).astype(F32)
    c, s = jnp.cos(th), jnp.sin(th)
    mf = jnp.concatenate([jnp.concatenate([c, s], axis=2), jnp.concatenate([-s, c], axis=2)], axis=1)
    return dict(fa_half=stage_a(n1o), fa_full=stage_a(n1t), g=g, mf=mf.astype(BF16),
                mi=mf.transpose(0, 2, 1).astype(BF16), k1n=k1n, n1o=n1o, n1t=n1t)


def _hy_static(ln):
    t = jnp.linspace(0.0, 1.0, ln, dtype=F32)[:, None]
    w = 2.0 * math.pi * jnp.arange(ln, dtype=F32) / ln
    f = jnp.linspace(1e-4, HY_BANDS - 1, HY_BANDS, dtype=F32)
    ang = w[:, None] * f[None, :]
    feats = jnp.concatenate([t, jnp.cos(ang), -jnp.sin(ang)], axis=-1)
    feats = jnp.pad(feats, ((0, 0), (0, LANE - HY_EMB)))
    min_decay = math.log(HY_DECAY_TARGET) / HY_SLOW_DECAY
    max_decay = math.log(HY_DECAY_TARGET) / HY_FAST_DECAY
    deltas = jnp.linspace(min_decay, max_decay, HY_CH, dtype=F32)
    window = jnp.exp(-t * jnp.abs(deltas))
    feats = jnp.concatenate([feats, feats[::-1]], axis=0)
    window = jnp.concatenate([window, window[::-1]], axis=0)
    return feats, window


def _outproj_kernel(x_ref, mod_ref, oa_ref, ob_ref, of_ref, obk_ref, od_ref, gate_ref, hg_ref, seg_ref, w_ref,
                    y_ref):
    gt = gate_ref[...]
    sg = gt * jax.nn.sigmoid(gt)
    oc = of_ref[...] + obk_ref[...]
    ss = _segsum(oc * oc, seg_ref[...]) * (1.0 / HGRN_DK)
    oc = oc * lax.rsqrt(ss + EPS) * hg_ref[...]
    acc = _bdot(oa_ref[0].T * sg[:, 0:256], w_ref[0:256, :])
    acc += _bdot(ob_ref[0].T * sg[:, 256:512], w_ref[256:512, :])
    acc += _bdot(oc * sg[:, 512:768], w_ref[512:768, :])
    acc += _bdot(od_ref[...] * sg[:, 768:1024], w_ref[768:1024, :])
    y_ref[...] = x_ref[...] + mod_ref[0, 2:3, :] * acc


def _outproj(x2, mod, ot_a, ot_b, o_f, o_b, out_d, p, hg, seg64, w_out, seq_len):
    n = x2.shape[0]
    tm = 256
    per_batch = mod.shape[0] > 1
    tps = seq_len // tm
    mod_idx = (lambda i: (i // tps, 0, 0)) if per_batch else (lambda i: (0, 0, 0))
    b256 = pl.BlockSpec((tm, 256), lambda i: (i, 0))
    bt = pl.BlockSpec((1, 256, tm), lambda i: (i // tps, 0, i % tps))
    return pl.pallas_call(
        _outproj_kernel,
        grid=(n // tm,),
        in_specs=[pl.BlockSpec((tm, D_MODEL), lambda i: (i, 0)),
                  pl.BlockSpec((1, 3, D_MODEL), mod_idx),
                  bt, bt, b256, b256, b256,
                  pl.BlockSpec((tm, 1024), lambda i: (i, P_GATE // 1024)),
                  pl.BlockSpec((1, 256), lambda i: (0, 0)),
                  pl.BlockSpec((256, 256), lambda i: (0, 0)),
                  pl.BlockSpec((D_MODEL, D_MODEL), lambda i: (0, 0))],
        out_specs=pl.BlockSpec((tm, D_MODEL), lambda i: (i, 0)),
        out_shape=jax.ShapeDtypeStruct((n, D_MODEL), F32),
        compiler_params=_cparams(("parallel",)),
        name="outproj",
    )(x2, mod, ot_a, ot_b, o_f, o_b, out_d, p, hg, seg64, w_out)


def _layer(x2, mod, lw, consts, batch, seq_len, ctx, rope_tabs, hy):
    n = batch * seq_len
    p = _inproj(x2, mod, lw["norm_g"], lw["w_in"], seq_len)

    q_a, ckvn, krp = _mla_q(p, lw, consts, seq_len, rope_tabs)
    ckv3 = ckvn.reshape(batch, seq_len, MLA_KV_LORA)
    krp3 = krp.reshape(batch, seq_len, LANE)
    if ctx is not None:
        cache_kr = jnp.pad(ctx[1], ((0, 0), (0, 0), (KR_OFF, LANE - KR_OFF - MLA_ROPE)))
        ckv_all = jnp.concatenate([ckv3, ctx[0]], axis=1)
        kr_all = jnp.concatenate([krp3, cache_kr], axis=1)
    else:
        ckv_all, kr_all = ckv3, krp3
    lk = ckv_all.shape[1]
    k_a, v_a = _mla_kv(ckv_all.reshape(batch * lk, MLA_KV_LORA), kr_all.reshape(batch * lk, LANE), lw, consts)
    ot_a = _mla_attn(q_a, k_a.reshape(batch, lk, 512), _vt_with_ones(v_a.reshape(batch, lk, BRANCH)), batch, seq_len)

    q_b, k_b, kd = _diff_prep(p, lw, consts, seq_len, rope_tabs)
    dv = p[:, P_DV:P_DV + BRANCH]
    k_b3 = k_b.reshape(batch, seq_len, BRANCH)
    v_b3 = dv.astype(BF16).reshape(batch, seq_len, BRANCH)
    if ctx is not None:
        k_b3 = jnp.concatenate([k_b3, ctx[2].reshape(batch, -1, BRANCH).astype(BF16)], axis=1)
        v_b3 = jnp.concatenate([v_b3, ctx[3].reshape(batch, -1, BRANCH).astype(BF16)], axis=1)
    ot_b = _diff_attn(q_b, k_b3, _vt_with_ones(v_b3), lw["diff_lambda"], lw["subln_col"], lw["lam_init"],
                      batch, seq_len)

    if ctx is not None:
        s0 = ctx[4]
    else:
        s0 = jnp.zeros((batch, 2, HGRN_HEADS, HGRN_DK, HGRN_DK), F32)
    eye = jnp.eye(HGRN_HEADS, dtype=F32)
    st0 = jnp.einsum("bdhke,hg->bdhegk", s0, eye).reshape(batch, 2, BRANCH, BRANCH)
    o_f, o_b, st_out = _hgrn(p, lw["hgrn_lb"], consts["hgrn_dd"], consts["hgrn_right"], st0, batch, seq_len)
    st5 = st_out.reshape(batch, 2, HGRN_HEADS, HGRN_DK, HGRN_HEADS, HGRN_DK)
    states = jnp.stack([st5[:, :, h, :, h, :] for h in range(HGRN_HEADS)], axis=2).swapaxes(-1, -2)

    v_d, x1, x2g = _hy_conv3(p, lw["hy_conv_w"], lw["hy_conv_b"], seq_len)
    taps = _hy_filter(hy["feats"], hy["window"], lw["hy_w1"], lw["hy_b1"], lw["hy_w2"], lw["hy_b2"], lw["hy_w3"],
                      lw["hy_freq"])
    k1n, n1o, n1t = hy["k1n"], hy["n1o"], hy["n1t"]
    ta = _fft_a(hy["fa_full"], taps.reshape(1, n1t, FFT_N2, HY_ORDER * HY_CH), BF16)
    spec = _fft_b(hy["mf"], None, ta.reshape(1, 2, k1n, FFT_N2, HY_ORDER * HY_CH), None, 0)[0]
    z4 = v_d.reshape(batch, n1o, FFT_N2, HY_CH)
    for o, xg in enumerate((x1, x2g)):
        a = _fft_a(hy["fa_half"], z4, BF16).reshape(batch, 2, k1n, FFT_N2, HY_CH)
        pk = _fft_b(hy["mf"], hy["mi"], a, spec, o).reshape(batch, 2 * k1n, FFT_N2, HY_CH)
        z4 = _fft_a_inv(hy["g"], pk, xg.reshape(batch, n1o, FFT_N2, HY_CH), z4, lw["hy_bias"][o:o + 1])
    out_d = z4.reshape(n, HY_CH)

    y = _outproj(x2, mod, ot_a, ot_b, o_f, o_b, out_d, p, lw["hgrn_out_g"], consts["seg64"], lw["w_out"], seq_len)
    new = None
    if ctx is None:
        new = (ckv3, krp3[:, :, KR_OFF:KR_OFF + MLA_ROPE],
               kd.reshape(batch, seq_len, DIFF_HEADS, 2, DIFF_HD),
               dv.reshape(batch, seq_len, DIFF_HEADS, 2 * DIFF_HD), states)
    return y, new


def _rope_tables(seq_len):
    half = MLA_ROPE // 2
    inv = ROPE_BASE ** (-jnp.arange(0, half, 2, dtype=F32) / half)
    rows = seq_len // GRID_W
    row = jnp.repeat(jnp.arange(rows, dtype=F32), GRID_W)
    col = (jnp.arange(rows * GRID_W) % GRID_W).astype(F32)
    ar, ac = row[:, None] * inv, col[:, None] * inv
    cos32 = jnp.concatenate([jnp.cos(ar), jnp.cos(ar), jnp.cos(ac), jnp.cos(ac)], axis=-1)
    sin32 = jnp.concatenate([-jnp.sin(ar), jnp.sin(ar), -jnp.sin(ac), jnp.sin(ac)], axis=-1)
    pad = ((0, 0), (KR_OFF, LANE - KR_OFF - MLA_ROPE))
    return dict(cos_mla=jnp.pad(cos32, pad, constant_values=1.0), sin_mla=jnp.pad(sin32, pad),
                cos_diff=jnp.tile(cos32, (1, 2 * DIFF_HEADS)), sin_diff=jnp.tile(sin32, (1, 2 * DIFF_HEADS)))


def _hy_tables(seq_len):
    feats, window = _hy_static(seq_len)
    return dict(feats=feats, window=window, **_fft_tables(seq_len))


def _layer_weights(l, w_in_p, lb, W):
    def head_pad(w, width, per):
        k = w.shape[0]
        w = w.reshape(k, MLA_HEADS, per)[:, :, :width]
        return jnp.pad(w, ((0, 0), (0, 0), (0, LANE - width))).reshape(k, MLA_HEADS * LANE)

    w_ukv = W["mla_w_ukv"][l].reshape(MLA_KV_LORA, MLA_HEADS, MLA_NOPE + MLA_V)
    nope_g, rope_g = W["mla_nope_g"][l], W["mla_rope_g"][l]
    zeros32 = jnp.zeros((MLA_ROPE,), F32)
    zeros64 = jnp.zeros((MLA_NOPE,), F32)
    gq = jnp.tile(jnp.concatenate([nope_g[0], rope_g[0], zeros32]), MLA_HEADS).reshape(1, 512)
    gk = jnp.tile(jnp.concatenate([nope_g[1], zeros64]), MLA_HEADS).reshape(1, 512)
    gkr = jnp.concatenate([zeros64, rope_g[1], zeros32]).reshape(1, LANE)
    return dict(
        norm_g=W["norm_g"][l], w_in=w_in_p[l], w_out=W["w_out"][l].astype(BF16),
        qn_g=W["mla_q_norm_g"][l].reshape(1, -1),
        w_uq=head_pad(W["mla_w_uq"][l], MLA_NOPE + MLA_ROPE, MLA_NOPE + MLA_ROPE).astype(BF16),
        kvn_g=W["mla_kv_norm_g"][l].reshape(1, -1),
        w_uk=jnp.pad(w_ukv[:, :, :MLA_NOPE], ((0, 0), (0, 0), (0, LANE - MLA_NOPE))).reshape(MLA_KV_LORA, 512)
        .astype(BF16),
        w_uv=w_ukv[:, :, MLA_NOPE:].reshape(MLA_KV_LORA, BRANCH).astype(BF16),
        gq=gq, gk=gk, gkr=gkr,
        dgq=jnp.tile(W["diff_qk_g"][l, 0], 2 * DIFF_HEADS).reshape(1, BRANCH),
        dgk=jnp.tile(W["diff_qk_g"][l, 1], 2 * DIFF_HEADS).reshape(1, BRANCH),
        diff_lambda=W["diff_lambda"][l], subln_col=W["diff_subln_g"][l].reshape(2 * DIFF_HD, 1),
        lam_init=0.8 - 0.6 * math.exp(-0.3 * l),
        hgrn_lb=lb[:, l].reshape(2, 1, BRANCH),
        hgrn_out_g=jnp.tile(W["hgrn_out_g"][l], HGRN_HEADS).reshape(1, BRANCH),
        hy_conv_w=W["hy_conv_w"][l], hy_conv_b=W["hy_conv_b"][l].reshape(1, -1),
        hy_w1=jnp.pad(W["hy_w1"][l], ((0, LANE - HY_EMB), (0, 0))), hy_b1=W["hy_b1"][l].reshape(1, -1),
        hy_w2=W["hy_w2"][l], hy_b2=W["hy_b2"][l].reshape(1, -1),
        hy_w3=W["hy_w3"][l].reshape(HY_FH, HY_ORDER, 2, HY_CH).transpose(2, 0, 1, 3)
        .reshape(2, HY_FH, HY_ORDER * HY_CH),
        hy_freq=W["hy_sin_freq"][l], hy_bias=W["hy_bias"][l],
    )


def kernel(x_prompt, x_sample, cache_mla_ckv, cache_mla_krope, cache_diff_k, cache_diff_v, state_hgrn, c, c_ctx,
           norm_g, w_mod, b_mod, w_in, w_out, mla_q_norm_g, mla_w_uq, mla_kv_norm_g, mla_w_ukv, mla_nope_g,
           mla_rope_g, diff_qk_g, diff_lambda, diff_subln_g, hgrn_lb_logits, hgrn_out_g, hy_conv_w, hy_conv_b,
           hy_w1, hy_b1, hy_w2, hy_b2, hy_w3, hy_sin_freq, hy_bias):
    W = dict(norm_g=norm_g, w_out=w_out, mla_q_norm_g=mla_q_norm_g, mla_w_uq=mla_w_uq,
             mla_kv_norm_g=mla_kv_norm_g, mla_w_ukv=mla_w_ukv, mla_nope_g=mla_nope_g, mla_rope_g=mla_rope_g,
             diff_qk_g=diff_qk_g, diff_lambda=diff_lambda, diff_subln_g=diff_subln_g, hgrn_out_g=hgrn_out_g,
             hy_conv_w=hy_conv_w, hy_conv_b=hy_conv_b, hy_w1=hy_w1, hy_b1=hy_b1, hy_w2=hy_w2, hy_b2=hy_b2,
             hy_w3=hy_w3, hy_sin_freq=hy_sin_freq, hy_bias=hy_bias)
    bp, lp, _ = x_prompt.shape
    bs, ls, _ = x_sample.shape

    w_in_p = _reorder_in_cols(w_in.astype(BF16))
    cvecs = jnp.concatenate([c_ctx[None, :], c, jnp.zeros((8 - 1 - bs, D_MODEL), F32)], axis=0)
    mods = _mod_all(cvecs, w_mod, b_mod)
    lb = _hgrn_lb(hgrn_lb_logits)
    seg512, cnt512 = _mla_seg()
    hgrn_dd, hgrn_right = _hgrn_consts()
    consts = dict(seg512=seg512, cnt512=cnt512, seg32=_seg_const(BRANCH, DIFF_HD), seg64=_seg_const(BRANCH, HGRN_DK),
                  hgrn_dd=hgrn_dd, hgrn_right=hgrn_right)
    lws = [_layer_weights(l, w_in_p, lb, W) for l in range(DEPTH)]

    hy_p = _hy_tables(lp)
    y = x_prompt.reshape(bp * lp, D_MODEL)
    per_layer = []
    for l in range(DEPTH):
        mod = mods[l, 0:1].reshape(1, 3, D_MODEL)
        y, new = _layer(y, mod, lws[l], consts, bp, lp, None, None, hy_p)
        per_layer.append(new)
    y_prompt = y.reshape(bp, lp, D_MODEL)
    news = [jnp.stack([s[i] for s in per_layer], axis=1) for i in range(5)]

    hy_s = _hy_tables(ls)
    rope_tabs = _rope_tables(ls)
    y = x_sample.reshape(bs * ls, D_MODEL)
    for l in range(DEPTH):
        mod = mods[l, 1:1 + bs].reshape(bs, 3, D_MODEL)
        ctx = (cache_mla_ckv[:, l], cache_mla_krope[:, l], cache_diff_k[:, l], cache_diff_v[:, l], state_hgrn[:, l])
        y, _ = _layer(y, mod, lws[l], consts, bs, ls, ctx, rope_tabs, hy_s)
    y_sample = y.reshape(bs, ls, D_MODEL)

    return (y_prompt, y_sample, news[0], news[1], news[2], news[3], news[4])
```

```python
import functools
import math

import numpy as np
import jax
import jax.numpy as jnp
from jax import lax
from jax.experimental import pallas as pl
from jax.experimental.pallas import tpu as pltpu

F32 = jnp.float32
BF16 = jnp.bfloat16

D_MODEL = 1024
DEPTH = 4
GRID_W = 64
ROPE_BASE = 10000.0
EPS = 1e-6
BRANCH = 256
MLA_HEADS = 4
MLA_NOPE = 64
MLA_ROPE = 32
MLA_V = 64
MLA_Q_LORA = 256
MLA_KV_LORA = 128
MLA_SCALE = (MLA_NOPE + MLA_ROPE) ** -0.5
DIFF_HEADS = 4
DIFF_HD = 32
DIFF_SCALE = DIFF_HD ** -0.5
HGRN_HEADS = 4
HGRN_DK = 64
HGRN_CHUNK = 128
HGRN_LEVELS = 7
HGRN_ROWS = 2
HGRN_MM_LEVELS = 3
HY_CH = 256
HY_ORDER = 2
HY_EMB = 33
HY_BANDS = 16
HY_FH = 64
HY_DECAY_TARGET = 0.01
HY_FAST_DECAY = 0.3
HY_SLOW_DECAY = 1.5
IN_COLS = 4000

LANE = 128
LOG2E = math.log2(math.e)
VT_ROWS = 80
FFT_N2 = 128
FFT_BLOCK_BYTES = 2 * 1024 * 1024
MLA_AHEAD = 8
MLA_CHUNK = 512
DIFF_AHEAD = 6
DIFF_CHUNK = 256
MLA_SUB = 256
DIFF_SUB = 512
VMEM_LIMIT = 52 * 1024 * 1024

P_CQ, P_CKV, P_KR, P_DQ, P_DK, P_DV = 0, 256, 384, 512, 768, 1024
P_HQ, P_HZF, P_HZB, P_HI, P_HU, P_GATE = 1280, 1536, 1792, 2048, 2304, 3072
P_COLS = 4096
KR_OFF = 64


def _in_col_perm():
    src = np.full((P_COLS,), IN_COLS, np.int32)

    def put(dst, lo, n):
        src[dst:dst + n] = np.arange(lo, lo + n)

    put(P_CQ, 0, 256)
    put(P_CKV, 256, 128)
    put(P_KR + KR_OFF, 384, 32)
    put(P_GATE, 416, 256)
    put(P_DQ, 672, 256)
    put(P_DK, 928, 256)
    put(P_DV, 1184, 256)
    put(P_GATE + 256, 1440, 256)
    put(P_HQ, 1696, 256)
    put(P_HZF, 1952, 256)
    put(P_HZB, 2208, 256)
    put(P_HI, 2464, 256)
    put(P_GATE + 512, 2720, 256)
    put(P_HU, 2976, 768)
    put(P_GATE + 768, 3744, 256)
    return src


def _reorder_in_cols(w):
    src = _in_col_perm()
    pieces, lo = [], 0
    while lo < P_COLS:
        hi = lo + 1
        if src[lo] == IN_COLS:
            while hi < P_COLS and src[hi] == IN_COLS:
                hi += 1
            pieces.append(jnp.zeros(w.shape[:-1] + (hi - lo,), w.dtype))
        else:
            while hi < P_COLS and src[hi] == src[hi - 1] + 1:
                hi += 1
            pieces.append(w[..., int(src[lo]):int(src[lo]) + hi - lo])
        lo = hi
    return jnp.concatenate(pieces, axis=-1)


def _cparams(sem):
    return pltpu.CompilerParams(dimension_semantics=sem, vmem_limit_bytes=VMEM_LIMIT)


def _bdot(a, b):
    return jnp.dot(a.astype(BF16), b.astype(BF16), preferred_element_type=F32)


def _nt(a, b):
    return lax.dot_general(a.astype(BF16), b.astype(BF16), (((1,), (1,)), ((), ())), preferred_element_type=F32)


def _split2(a):
    hi = a.astype(BF16)
    lo = (a - hi.astype(F32)).astype(BF16)
    return hi, lo


def _dot3(a, b):
    ah, al = _split2(a)
    bh, bl = _split2(b)
    d = functools.partial(jnp.dot, preferred_element_type=F32)
    return d(ah, bh) + d(ah, bl) + d(al, bh)


def _segsum(v, seg):
    return jnp.dot(v.astype(BF16), seg, preferred_element_type=F32)


def _rms(x, g):
    return x * lax.rsqrt(jnp.mean(x * x, axis=-1, keepdims=True) + EPS) * g


def _swap8(x):
    w = x.shape[-1]
    lane = lax.broadcasted_iota(jnp.int32, x.shape, x.ndim - 1)
    up = pltpu.roll(x, w - 8, x.ndim - 1)
    dn = pltpu.roll(x, 8, x.ndim - 1)
    return jnp.where((lane & 15) < 8, up, dn)


def _tile_lanes(x, n):
    return x if n == 1 else jnp.concatenate([x] * n, axis=-1)


def _mod_kernel(c_ref, w_ref, b_ref, o_ref):
    c = c_ref[...]
    o_ref[0] = _dot3(c * jax.nn.sigmoid(c), w_ref[0]) + b_ref[0]


def _mod_all(cvecs, w_mod, b_mod):
    nt = 3
    return pl.pallas_call(
        _mod_kernel,
        grid=(DEPTH, nt),
        in_specs=[pl.BlockSpec((8, D_MODEL), lambda l, j: (0, 0)),
                  pl.BlockSpec((1, D_MODEL, D_MODEL), lambda l, j: (l, 0, j)),
                  pl.BlockSpec((1, 1, D_MODEL), lambda l, j: (l, 0, j))],
        out_specs=pl.BlockSpec((1, 8, D_MODEL), lambda l, j: (l, 0, j)),
        out_shape=jax.ShapeDtypeStruct((DEPTH, 8, 3 * D_MODEL), F32),
        compiler_params=_cparams(("arbitrary", "arbitrary")),
        name="mod",
    )(cvecs, w_mod, b_mod.reshape(DEPTH, 1, 3 * D_MODEL))


def _lb_kernel(x_ref, o_ref):
    x = x_ref[...]
    rows = [x[l:l + 1, :] for l in range(DEPTH)]
    m = functools.reduce(jnp.maximum, rows)
    e = [jnp.exp(r - m) for r in rows]
    tot = functools.reduce(lambda a, b: a + b, e)
    acc = jnp.zeros_like(tot)
    o_ref[0:1, :] = acc
    for l in range(1, DEPTH):
        acc = acc + e[l] / tot
        o_ref[l:l + 1, :] = acc


def _hgrn_lb(logits):
    flat = logits.transpose(1, 0, 2).reshape(DEPTH, 2 * BRANCH)
    lb = pl.pallas_call(
        _lb_kernel,
        out_shape=jax.ShapeDtypeStruct(flat.shape, F32),
        name="hgrn_lb",
    )(flat)
    return lb.reshape(DEPTH, 2, BRANCH).transpose(1, 0, 2)


def _inproj_kernel(x_ref, mod_ref, g_ref, w_ref, p_ref):
    h = _rms(x_ref[...], g_ref[...]) * (1.0 + mod_ref[0, 1:2, :]) + mod_ref[0, 0:1, :]
    p_ref[...] = jnp.dot(h.astype(BF16), w_ref[...], preferred_element_type=F32)


def _inproj(x2, mod, norm_g, w_in_p, seq_len):
    n = x2.shape[0]
    tm = 256
    per_batch = mod.shape[0] > 1
    tiles_per_seq = seq_len // tm
    mod_idx = (lambda i: (i // tiles_per_seq, 0, 0)) if per_batch else (lambda i: (0, 0, 0))
    return pl.pallas_call(
        _inproj_kernel,
        grid=(n // tm,),
        in_specs=[pl.BlockSpec((tm, D_MODEL), lambda i: (i, 0)),
                  pl.BlockSpec((1, 3, D_MODEL), mod_idx),
                  pl.BlockSpec((1, D_MODEL), lambda i: (0, 0)),
                  pl.BlockSpec((D_MODEL, P_COLS), lambda i: (0, 0))],
        out_specs=pl.BlockSpec((tm, P_COLS), lambda i: (i, 0)),
        out_shape=jax.ShapeDtypeStruct((n, P_COLS), F32),
        compiler_params=_cparams(("parallel",)),
        name="inproj",
    )(x2, mod, norm_g.reshape(1, D_MODEL), w_in_p)


def _mla_seg():
    sid = np.zeros((512,), np.int32)
    cnt = np.ones((512,), np.float32)
    for h in range(MLA_HEADS):
        b = 128 * h
        sid[b:b + 64] = 3 * h
        sid[b + 64:b + 96] = 3 * h + 1
        sid[b + 96:b + 128] = 3 * h + 2
        cnt[b:b + 64] = 1.0 / 64
        cnt[b + 64:b + 128] = 1.0 / 32
    seg = (sid[:, None] == sid[None, :]).astype(np.float32)
    return jnp.asarray(seg, BF16), jnp.asarray(cnt.reshape(1, 512))


def _mla_q_kernel(rope, cq_ref, ckv_ref, kr_ref, qng_ref, wuq_ref, kvg_ref, gq_ref, gkr_ref, seg_ref, cnt_ref,
                  *rest):
    if rope:
        cos_ref, sin_ref, q_ref, ckvn_ref, krp_ref = rest
    else:
        q_ref, ckvn_ref, krp_ref = rest
    cqn = _rms(cq_ref[...], qng_ref[...])
    q = _bdot(cqn, wuq_ref[...])
    ss = _segsum(q * q, seg_ref[...]) * cnt_ref[...]
    qn = q * lax.rsqrt(ss + EPS) * gq_ref[...]
    ckvn_ref[...] = _rms(ckv_ref[...], kvg_ref[...])
    kr = kr_ref[...]
    krn = kr * lax.rsqrt(jnp.sum(kr * kr, axis=-1, keepdims=True) * (1.0 / MLA_ROPE) + EPS) * gkr_ref[...]
    if rope:
        cos, sin = cos_ref[...], sin_ref[...]
        qn = qn * _tile_lanes(cos, MLA_HEADS) + _swap8(qn) * _tile_lanes(sin, MLA_HEADS)
        krn = krn * cos + _swap8(krn) * sin
    q_ref[...] = (qn * (MLA_SCALE * LOG2E)).astype(BF16)
    krp_ref[...] = krn


def _mla_q(p, lw, consts, seq_len, rope_tabs):
    n = p.shape[0]
    tm = min(512, seq_len)
    rope = rope_tabs is not None
    full = lambda shape: pl.BlockSpec(shape, lambda i: (0,) * len(shape))
    in_specs = [pl.BlockSpec((tm, 256), lambda i: (i, P_CQ // 256)),
                pl.BlockSpec((tm, 128), lambda i: (i, P_CKV // 128)),
                pl.BlockSpec((tm, 128), lambda i: (i, P_KR // 128)),
                full((1, 256)), full((256, 512)), full((1, 128)), full((1, 512)), full((1, 128)),
                full((512, 512)), full((1, 512))]
    args = [p, p, p, lw["qn_g"], lw["w_uq"], lw["kvn_g"], lw["gq"], lw["gkr"], consts["seg512"], consts["cnt512"]]
    if rope:
        tps = seq_len // tm
        in_specs += [pl.BlockSpec((tm, 128), lambda i: (i % tps, 0))] * 2
        args += [rope_tabs["cos_mla"], rope_tabs["sin_mla"]]
    return pl.pallas_call(
        functools.partial(_mla_q_kernel, rope),
        grid=(n // tm,),
        in_specs=in_specs,
        out_specs=[pl.BlockSpec((tm, 512), lambda i: (i, 0)),
                   pl.BlockSpec((tm, 128), lambda i: (i, 0)),
                   pl.BlockSpec((tm, 128), lambda i: (i, 0))],
        out_shape=[jax.ShapeDtypeStruct((n, 512), BF16),
                   jax.ShapeDtypeStruct((n, 128), F32),
                   jax.ShapeDtypeStruct((n, 128), F32)],
        compiler_params=_cparams(("parallel",)),
        name="mla_q",
    )(*args)


def _store_vt(o_ref, vt):
    tm = vt.shape[1]
    row = lax.broadcasted_iota(jnp.int32, (VT_ROWS - 64, tm), 0)
    extra = jnp.where(row == 0, 1.0, 0.0).astype(BF16)
    for h in range(4):
        o_ref[0, h, 0:64, :] = vt[64 * h:64 * (h + 1)].astype(BF16)
        o_ref[0, h, 64:VT_ROWS, :] = extra


def _mla_kv_kernel(ckvn_ref, krp_ref, wuk_ref, wuv_ref, gk_ref, seg_ref, cnt_ref, k_ref, vt_ref):
    c = ckvn_ref[...].astype(BF16)
    kn = jnp.dot(c, wuk_ref[...], preferred_element_type=F32)
    ss = _segsum(kn * kn, seg_ref[...]) * cnt_ref[...]
    k = kn * lax.rsqrt(ss + EPS) * gk_ref[...] + _tile_lanes(krp_ref[...], MLA_HEADS)
    k_ref[...] = k.astype(BF16)
    _store_vt(vt_ref, _nt(wuv_ref[...], c))


def _mla_kv(ckvn, krp, lw, consts, batch):
    n = ckvn.shape[0]
    lseg = n // batch
    tm = min(512, lseg)
    tpb = lseg // tm
    full = lambda shape: pl.BlockSpec(shape, lambda i: (0,) * len(shape))
    return pl.pallas_call(
        _mla_kv_kernel,
        grid=(n // tm,),
        in_specs=[pl.BlockSpec((tm, 128), lambda i: (i, 0)), pl.BlockSpec((tm, 128), lambda i: (i, 0)),
                  full((128, 512)), full((256, 128)), full((1, 512)), full((512, 512)), full((1, 512))],
        out_specs=[pl.BlockSpec((tm, 512), lambda i: (i, 0)),
                   pl.BlockSpec((1, 4, VT_ROWS, tm), lambda i: (i // tpb, 0, 0, i % tpb))],
        out_shape=[jax.ShapeDtypeStruct((n, 512), BF16), jax.ShapeDtypeStruct((batch, 4, VT_ROWS, lseg), BF16)],
        compiler_params=_cparams(("parallel",)),
        name="mla_kv",
    )(ckvn, krp, lw["w_uk"], lw["w_uv"], lw["gk"], consts["seg512"], consts["cnt512"])


def _softmax_pv(qs, k_refs, vt_refs, key_chunk, sub_rows, n_ahead):
    where = [(i, lo) for i, r in enumerate(k_refs) for lo in range(0, r.shape[1], key_chunk)]
    nch = len(where)
    nq = len(qs)
    sub = min(sub_rows, key_chunk)
    nsub = key_chunk // sub

    def scores(c, u):
        seg, lo = where[c]
        ks = k_refs[seg][0, lo + u * sub:lo + (u + 1) * sub, :]
        return [_nt(ks, q) for q in qs]

    def chunk_max(s_chunk, j):
        mc = functools.reduce(jnp.maximum, [s_chunk[u][j] for u in range(nsub)])
        return jnp.max(mc, axis=0, keepdims=True)

    s_buf = {c: [scores(c, u) for u in range(nsub)] for c in range(min(n_ahead, nch))}
    m = [None] * nq
    acc = [None] * nq
    m_new = [chunk_max(s_buf[0], j) for j in range(nq)]
    for c in range(nch):
        s_cur = s_buf.pop(c)
        ahead = c + n_ahead
        if ahead < nch:
            s_buf[ahead] = []
        pv = [None] * nq
        for u in range(nsub):
            if ahead < nch:
                s_buf[ahead].append(scores(ahead, u))
            seg, lo = where[c]
            vs = vt_refs[seg][0, 0, :, lo + u * sub:lo + (u + 1) * sub]
            for j in range(nq):
                part = jnp.dot(vs, jnp.exp2(s_cur[u][j] - m_new[j]).astype(BF16), preferred_element_type=F32)
                pv[j] = part if pv[j] is None else pv[j] + part
        for j in range(nq):
            acc[j] = pv[j] if c == 0 else acc[j] * jnp.exp2(m[j] - m_new[j]) + pv[j]
            m[j] = m_new[j]
        if c + 1 < nch:
            m_new = [jnp.maximum(m[j], chunk_max(s_buf[c + 1], j)) for j in range(nq)]
    return acc


def _mla_attn_kernel(key_chunk, nseg, q_ref, *refs):
    k_refs, vt_refs, o_ref = refs[:nseg], refs[nseg:2 * nseg], refs[2 * nseg]
    (acc,) = _softmax_pv([q_ref[...]], k_refs, vt_refs, key_chunk, MLA_SUB, MLA_AHEAD)
    o_ref[0] = acc[0:MLA_V] / acc[MLA_V:MLA_V + 1]


def _key_chunk(lk, rows=512):
    return rows if lk % rows == 0 else lk


def _vt_kernel(v_ref, o_ref):
    _store_vt(o_ref, v_ref[0].astype(F32).T)


def _vt_with_ones(v3):
    b, lk, width = v3.shape
    tm = 1536 if lk % 1536 == 0 else lk
    return pl.pallas_call(
        _vt_kernel,
        grid=(b, lk // tm),
        in_specs=[pl.BlockSpec((1, tm, width), lambda i, j: (i, j, 0))],
        out_specs=pl.BlockSpec((1, 4, VT_ROWS, tm), lambda i, j: (i, 0, 0, j)),
        out_shape=jax.ShapeDtypeStruct((b, 4, VT_ROWS, lk), BF16),
        compiler_params=_cparams(("parallel", "parallel")),
        name="vt_ones",
    )(v3)


def _mla_attn(q, ks, vts, batch, seq_len):
    tq = 256
    nq = seq_len // tq
    chunk = _key_chunk(min(k.shape[1] for k in ks), MLA_CHUNK)
    k_specs = [pl.BlockSpec((1, k.shape[1], 128), lambda b, h, i: (b, 0, h)) for k in ks]
    v_specs = [pl.BlockSpec((1, 1, VT_ROWS, v.shape[3]), lambda b, h, i: (b, h, 0, 0)) for v in vts]
    return pl.pallas_call(
        functools.partial(_mla_attn_kernel, chunk, len(ks)),
        grid=(batch, MLA_HEADS, nq),
        in_specs=[pl.BlockSpec((tq, 128), lambda b, h, i: (b * nq + i, h))] + k_specs + v_specs,
        out_specs=pl.BlockSpec((1, MLA_V, tq), lambda b, h, i: (b, h, i)),
        out_shape=jax.ShapeDtypeStruct((batch, BRANCH, seq_len), F32),
        compiler_params=_cparams(("parallel", "parallel", "arbitrary")),
        name="mla_attn",
    )(q, *ks, *vts)


def _seg_const(width, seg):
    sid = np.arange(width) // seg
    return jnp.asarray((sid[:, None] == sid[None, :]).astype(np.float32), BF16)


def _diff_prep_kernel(rope, dq_ref, dk_ref, dv_ref, gq_ref, gk_ref, seg_ref, *rest):
    if rope:
        cos_ref, sin_ref, q_ref, k_ref, kf_ref, vt_ref = rest
    else:
        q_ref, k_ref, kf_ref, vt_ref = rest
    seg = seg_ref[...]
    _store_vt(vt_ref, dv_ref[...].T)

    def norm(x, g):
        ss = _segsum(x * x, seg) * (1.0 / DIFF_HD)
        return x * lax.rsqrt(ss + EPS) * g

    q = norm(dq_ref[...], gq_ref[...])
    k = norm(dk_ref[...], gk_ref[...])
    kf_ref[...] = k
    if rope:
        cos, sin = cos_ref[...], sin_ref[...]
        q = q * cos + _swap8(q) * sin
        k = k * cos + _swap8(k) * sin
    q_ref[...] = (q * (DIFF_SCALE * LOG2E)).astype(BF16)
    k_ref[...] = k.astype(BF16)


def _diff_prep(p, lw, consts, seq_len, rope_tabs):
    n = p.shape[0]
    tm = min(512, seq_len)
    rope = rope_tabs is not None
    full = lambda shape: pl.BlockSpec(shape, lambda i: (0,) * len(shape))
    in_specs = [pl.BlockSpec((tm, 256), lambda i: (i, P_DQ // 256)),
                pl.BlockSpec((tm, 256), lambda i: (i, P_DK // 256)),
                pl.BlockSpec((tm, 256), lambda i: (i, P_DV // 256)),
                full((1, 256)), full((1, 256)), full((256, 256))]
    args = [p, p, p, lw["dgq"], lw["dgk"], consts["seg32"]]
    tps = seq_len // tm
    if rope:
        in_specs += [pl.BlockSpec((tm, 256), lambda i: (i % tps, 0))] * 2
        args += [rope_tabs["cos_diff"], rope_tabs["sin_diff"]]
    blk = pl.BlockSpec((tm, 256), lambda i: (i, 0))
    return pl.pallas_call(
        functools.partial(_diff_prep_kernel, rope),
        grid=(n // tm,),
        in_specs=in_specs,
        out_specs=[blk, blk, blk, pl.BlockSpec((1, 4, VT_ROWS, tm), lambda i: (i // tps, 0, 0, i % tps))],
        out_shape=[jax.ShapeDtypeStruct((n, 256), BF16), jax.ShapeDtypeStruct((n, 256), BF16),
                   jax.ShapeDtypeStruct((n, 256), F32),
                   jax.ShapeDtypeStruct((n // seq_len, 4, VT_ROWS, seq_len), BF16)],
        compiler_params=_cparams(("parallel",)),
        name="diff_prep",
    )(*args)


def _diff_attn_kernel(lam_init, key_chunk, nseg, q_ref, *refs):
    k_refs, vt_refs = refs[:nseg], refs[nseg:2 * nseg]
    lp_ref, g_ref, o_ref = refs[2 * nseg:]
    h = pl.program_id(1)
    q = q_ref[...]
    lane = lax.broadcasted_iota(jnp.int32, q.shape, 1)
    base = (h % 2) * 64
    zero = jnp.zeros_like(q)

    def map_query(j):
        lo = base + 32 * j
        return jnp.where((lane >= lo) & (lane < lo + 32), q, zero)

    acc0, acc1 = _softmax_pv([map_query(0), map_query(1)], k_refs, vt_refs, key_chunk, DIFF_SUB, DIFF_AHEAD)
    lp = lp_ref[...]
    lam = (jnp.exp(jnp.sum(lp[0:1] * lp[1:2], axis=1, keepdims=True))
           - jnp.exp(jnp.sum(lp[2:3] * lp[3:4], axis=1, keepdims=True)) + lam_init)
    o = acc0[0:64] / acc0[64:65] - lam * (acc1[0:64] / acc1[64:65])
    ms = jnp.mean(o * o, axis=0, keepdims=True)
    o_ref[0] = o * lax.rsqrt(ms + EPS) * g_ref[...] * (1.0 - lam_init)


def _diff_attn(q, ks, vts, lp, g_col, lam_init, batch, seq_len):
    tq = 256
    nq = seq_len // tq
    chunk = _key_chunk(min(k.shape[1] for k in ks), DIFF_CHUNK)
    k_specs = [pl.BlockSpec((1, k.shape[1], 128), lambda b, h, i: (b, 0, h // 2)) for k in ks]
    v_specs = [pl.BlockSpec((1, 1, VT_ROWS, v.shape[3]), lambda b, h, i: (b, h, 0, 0)) for v in vts]
    return pl.pallas_call(
        functools.partial(_diff_attn_kernel, lam_init, chunk, len(ks)),
        grid=(batch, DIFF_HEADS, nq),
        in_specs=[pl.BlockSpec((tq, 128), lambda b, h, i: (b * nq + i, h // 2))] + k_specs + v_specs
        + [pl.BlockSpec((4, DIFF_HD), lambda b, h, i: (0, 0)), pl.BlockSpec((64, 1), lambda b, h, i: (0, 0))],
        out_specs=pl.BlockSpec((1, 64, tq), lambda b, h, i: (b, h, i)),
        out_shape=jax.ShapeDtypeStruct((batch, BRANCH, seq_len), F32),
        compiler_params=_cparams(("parallel", "parallel", "arbitrary")),
        name="diff_attn",
    )(q, *ks, *vts, lp, g_col)


def _hgrn_consts():
    c = HGRN_CHUNK
    t = np.arange(c)
    low = (t[None, :] <= t[:, None]).astype(np.float32)
    blocks = []
    for j in range(HGRN_MM_LEVELS):
        m = 1 << j
        rho = (t // (2 * m)) * (2 * m) + m - 1
        sign = np.where((t // m) % 2 == 1, 1.0, -1.0)[:, None]
        blocks.append(sign * (low - (t[None, :] <= rho[:, None]).astype(np.float32)))
    blocks.append(low)
    fwd = np.concatenate(blocks, axis=0)
    bwd = np.concatenate([b[::-1, ::-1] for b in blocks], axis=0)
    right = np.stack([(t // (1 << j)) % 2 for j in range(HGRN_LEVELS)]).astype(np.float32)
    right = np.stack([right, right[:, ::-1]])
    right = np.broadcast_to(right[..., None], right.shape + (BRANCH,))
    return jnp.asarray(np.stack([fwd, bwd]), BF16), jnp.asarray(right, F32)


def _hgrn_kernel(nc, nb, qf_ref, zf_ref, vf_ref, qb_ref, zb_ref, vb_ref, lb_ref, dd_ref, rm_ref, s0_ref,
                 of_ref, ob_ref, sout_ref, st_ref):
    c = HGRN_CHUNK
    ci = pl.program_id(1)
    chains = [(bi, d) for bi in range(nb) for d in (0, 1)]
    ids = range(len(chains))

    @pl.when(ci == 0)
    def _():
        st_ref[...] = s0_ref[...]

    lane = lax.broadcasted_iota(jnp.int32, (1, BRANCH), 1)
    head_masks = [(lane >= HGRN_DK * h) & (lane < HGRN_DK * (h + 1)) for h in range(HGRN_HEADS)]
    t_idx = lax.broadcasted_iota(jnp.int32, (c, HGRN_HEADS * c), 0)
    s_idx = lax.broadcasted_iota(jnp.int32, (c, HGRN_HEADS * c), 1) & (c - 1)
    pair_xor = t_idx ^ s_idx

    def stack_heads(x):
        xb = x.astype(BF16)
        zero = jnp.zeros_like(xb)
        return jnp.concatenate([jnp.where(hm, xb, zero) for hm in head_masks], axis=0)

    q_refs, z_refs, v_refs = (qf_ref, qb_ref), (zf_ref, zb_ref), (vf_ref, vb_ref)
    q = [q_refs[d][bi] for bi, d in chains]
    v = [v_refs[d][bi] for bi, d in chains]
    z = [z_refs[d][bi] for bi, d in chains]
    lb = [lb_ref[d] for _, d in chains]
    g = [jnp.log(lb[i] + (1.0 - lb[i]) * jax.nn.sigmoid(z[i])) for i in ids]
    kk = [(1.0 - lb[i]) * jax.nn.sigmoid(-z[i]) for i in ids]
    sums = []
    for i in ids:
        gh, gl = _split2(g[i])
        dd = dd_ref[chains[i][1]]
        sums.append(jnp.dot(dd, gh, preferred_element_type=F32) + jnp.dot(dd, gl, preferred_element_type=F32))
    b = [sums[i][HGRN_MM_LEVELS * c:] for i in ids]
    b_tot = [b[i][c - 1:c] if chains[i][1] == 0 else b[i][0:1] for i in ids]

    def neg_abs_decay(i, j):
        if j < HGRN_MM_LEVELS:
            return sums[i][j * c:(j + 1) * c]
        m = 1 << j
        off = m - 1 if chains[i][1] == 0 else m
        ref = jnp.concatenate([jnp.broadcast_to(b[i][g0 + off:g0 + off + 1], (2 * m, BRANCH))
                               for g0 in range(0, c, 2 * m)], axis=0)
        return -jnp.abs(b[i] - ref)

    a = [None] * len(chains)
    for j in reversed(range(HGRN_LEVELS)):
        same_group = pair_xor < (2 << j)
        for i in ids:
            e = jnp.exp(neg_abs_decay(i, j))
            eq = e * rm_ref[chains[i][1], j]
            qt = q[i] * eq
            kt = kk[i] * (e - eq)
            lvl = _nt(qt, stack_heads(kt))
            a[i] = lvl if a[i] is None else jnp.where(same_group, lvl, a[i])
    diagonal = pair_xor == 0
    for i in ids:
        a[i] = jnp.where(diagonal, _nt(q[i], stack_heads(kk[i])), a[i])

    outs = (of_ref, ob_ref)
    for i in ids:
        bi, d = chains[i]
        o = jnp.dot(a[i].astype(BF16), stack_heads(v[i]), preferred_element_type=F32)
        outs[d][bi] = o + _nt(q[i] * jnp.exp(b[i]), st_ref[bi, d])

    r2 = lax.broadcasted_iota(jnp.int32, (BRANCH, BRANCH), 0) // HGRN_DK
    c2 = lax.broadcasted_iota(jnp.int32, (BRANCH, BRANCH), 1) // HGRN_DK
    for i in ids:
        bi, d = chains[i]
        kd = kk[i] * jnp.exp(b_tot[i] - b[i])
        upd = lax.dot_general(v[i].astype(BF16), kd.astype(BF16), (((0,), (0,)), ((), ())),
                              preferred_element_type=F32)
        st_new = st_ref[bi, d] * jnp.exp(b_tot[i]) + jnp.where(r2 == c2, upd, 0.0)
        st_ref[bi, d] = st_new

        @pl.when(ci == nc - 1)
        def _(bi=bi, d=d, st_new=st_new):
            sout_ref[bi, d] = st_new


def _hgrn(p, lb_l, dd, rm, st0, batch, seq_len):
    n = p.shape[0]
    c = HGRN_CHUNK
    nc = seq_len // c
    nb = HGRN_ROWS if batch % HGRN_ROWS == 0 else 1
    p3 = p.reshape(batch, seq_len, P_COLS)
    fwd = lambda col: pl.BlockSpec((nb, c, 256), lambda b, i: (b, i, col))
    bwd = lambda col: pl.BlockSpec((nb, c, 256), lambda b, i: (b, nc - 1 - i, col))
    whole = lambda shape: pl.BlockSpec(shape, lambda b, i: (0,) * len(shape))
    state = pl.BlockSpec((nb, 2, 256, 256), lambda b, i: (b, 0, 0, 0))
    o_f, o_b, st = pl.pallas_call(
        functools.partial(_hgrn_kernel, nc, nb),
        grid=(batch // nb, nc),
        in_specs=[fwd(P_HQ // 256), fwd(P_HZF // 256), fwd(P_HI // 256),
                  bwd(P_HQ // 256), bwd(P_HZB // 256), bwd(P_HI // 256),
                  whole((2, 1, 256)), whole((2, (HGRN_MM_LEVELS + 1) * c, c)),
                  whole((2, HGRN_LEVELS, c, 256)), state],
        out_specs=[pl.BlockSpec((nb, c, 256), lambda b, i: (b, i, 0)),
                   pl.BlockSpec((nb, c, 256), lambda b, i: (b, nc - 1 - i, 0)), state],
        out_shape=[jax.ShapeDtypeStruct((batch, seq_len, 256), F32),
                   jax.ShapeDtypeStruct((batch, seq_len, 256), F32),
                   jax.ShapeDtypeStruct((batch, 2, 256, 256), F32)],
        scratch_shapes=[pltpu.VMEM((nb, 2, 256, 256), F32)],
        compiler_params=_cparams(("parallel", "arbitrary")),
        name="hgrn",
    )(p3, p3, p3, p3, p3, p3, lb_l, dd, rm, st0)
    return o_f.reshape(n, 256), o_b.reshape(n, 256), st


def _hy_conv3_kernel(tiles_per_seq, above_ref, cur_ref, below_ref, w_ref, b_ref, v_ref, x1_ref, x2_ref):
    i = pl.program_id(0)
    cur = cur_ref[...]
    tm = cur.shape[0]
    first = (i % tiles_per_seq) == 0
    last = (i % tiles_per_seq) == tiles_per_seq - 1
    above = jnp.where(first, 0.0, above_ref[7:8, :])
    below = jnp.where(last, 0.0, below_ref[0:1, :])
    row = lax.broadcasted_iota(jnp.int32, (tm, 1), 0)
    prev = jnp.where(row == 0, above, pltpu.roll(cur, 1, 0))
    nxt = jnp.where(row == tm - 1, below, pltpu.roll(cur, tm - 1, 0))
    w = w_ref[...]
    u = prev * w[0:1] + cur * w[1:2] + nxt * w[2:3] + b_ref[...]
    v_ref[...] = u[:, 0:256]
    x1_ref[...] = u[:, 256:512]
    x2_ref[...] = u[:, 512:768]


def _hy_conv3(p, w, b, seq_len):
    n = p.shape[0]
    tm = min(512, seq_len)
    nt = n // tm
    g = tm // 8
    col = P_HU // 768
    oblk = pl.BlockSpec((tm, 256), lambda i: (i, 0))
    return pl.pallas_call(
        functools.partial(_hy_conv3_kernel, seq_len // tm),
        grid=(nt,),
        in_specs=[pl.BlockSpec((8, 768), lambda i: (jnp.maximum(i * g - 1, 0), col)),
                  pl.BlockSpec((tm, 768), lambda i: (i, col)),
                  pl.BlockSpec((8, 768), lambda i: (jnp.minimum((i + 1) * g, nt * g - 1), col)),
                  pl.BlockSpec((3, 768), lambda i: (0, 0)), pl.BlockSpec((1, 768), lambda i: (0, 0))],
        out_specs=[oblk, oblk, oblk],
        out_shape=[jax.ShapeDtypeStruct((n, 256), F32)] * 3,
        compiler_params=_cparams(("parallel",)),
        name="hy_conv3",
    )(p, p, p, w, b)


def _hy_filter_kernel(feat_ref, w1_ref, b1_ref, w2_ref, b2_ref, w3_ref, fr_ref, win_ref, o_ref):
    fr = fr_ref[...]
    h = jnp.sin(fr[0:1] * (_dot3(feat_ref[...], w1_ref[...]) + b1_ref[...]))
    h = jnp.sin(fr[1:2] * (_dot3(h, w2_ref[...]) + b2_ref[...]))
    o_ref[...] = _dot3(h, w3_ref[0]) * _tile_lanes(win_ref[...], HY_ORDER)


def _hy_filter(feats2, window2, w1p, b1, w2, b2, w3d, freq):
    l2 = feats2.shape[0]
    ln = l2 // 2
    tm = min(512, ln)
    full = lambda shape: pl.BlockSpec(shape, lambda i: (0,) * len(shape))
    return pl.pallas_call(
        _hy_filter_kernel,
        grid=(l2 // tm,),
        in_specs=[pl.BlockSpec((tm, LANE), lambda i: (i, 0)),
                  full((LANE, HY_FH)), full((1, HY_FH)), full((HY_FH, HY_FH)), full((1, HY_FH)),
                  pl.BlockSpec((1, HY_FH, HY_ORDER * HY_CH), lambda i: (i // (ln // tm), 0, 0)), full((2, HY_FH)),
                  pl.BlockSpec((tm, HY_CH), lambda i: (i, 0))],
        out_specs=pl.BlockSpec((tm, HY_ORDER * HY_CH), lambda i: (i, 0)),
        out_shape=jax.ShapeDtypeStruct((l2, HY_ORDER * HY_CH), F32),
        compiler_params=_cparams(("parallel",)),
        name="hy_filter",
    )(feats2, w1p, b1, w2, b2, w3d, freq, window2)


def _fft_blocking(nb, n1, n2, ch):
    per_batch = n1 * n2 * ch * 4
    if per_batch <= FFT_BLOCK_BYTES:
        bb = max(1, min(nb, FFT_BLOCK_BYTES // per_batch))
        while nb % bb:
            bb -= 1
        return bb, n2
    rt = n2
    while n1 * rt * ch * 4 > FFT_BLOCK_BYTES and rt > 8:
        rt //= 2
    return 1, rt


def _fft_a_kernel(f_ref, x_ref, o_ref):
    f = f_ref[...]
    for b in range(x_ref.shape[0]):
        x = x_ref[b].astype(BF16)
        o_ref[b] = jnp.einsum("kn,nrc->krc", f, x, preferred_element_type=F32).astype(o_ref.dtype)


def _fft_a(fa, x4, out_dtype):
    nb, n1, n2, ch = x4.shape
    r = fa.shape[0]
    bb, rt = _fft_blocking(nb, n1, n2, ch)
    return pl.pallas_call(
        _fft_a_kernel,
        grid=(nb // bb, n2 // rt),
        in_specs=[pl.BlockSpec((r, n1), lambda b, i: (0, 0)),
                  pl.BlockSpec((bb, n1, rt, ch), lambda b, i: (b, 0, i, 0))],
        out_specs=pl.BlockSpec((bb, r, rt, ch), lambda b, i: (b, 0, i, 0)),
        out_shape=jax.ShapeDtypeStruct((nb, r, n2, ch), out_dtype),
        compiler_params=_cparams(("parallel", "parallel")),
        name="fft_a",
    )(fa, x4)


def _fft_b_kernel(with_inverse, mf_ref, *rest):
    if with_inverse:
        mi_ref, a_ref, h_ref, o_ref = rest
    else:
        a_ref, o_ref = rest
    half = FFT_N2
    for kk in range(a_ref.shape[2]):
        for b in range(a_ref.shape[0]):
            a = jnp.concatenate([a_ref[b, 0, kk], a_ref[b, 1, kk]], axis=0)
            x = jnp.dot(mf_ref[kk], a.astype(BF16), preferred_element_type=F32)
            if with_inverse:
                xr, xi = x[:half], x[half:]
                hr, hi = h_ref[0, kk], h_ref[1, kk]
                y = jnp.concatenate([xr * hr - xi * hi, xr * hi + xi * hr], axis=0)
                x = jnp.dot(mi_ref[kk], y.astype(BF16), preferred_element_type=F32)
            o_ref[b, 0, kk] = x[:half].astype(o_ref.dtype)
            o_ref[b, 1, kk] = x[half:].astype(o_ref.dtype)


def _fft_b(mf, mi, a5, spec, order):
    nb, _, k1n, n2, ch = a5.shape
    ks = 3 if (k1n % 3 == 0 and nb * ch <= 1024) else 1
    mat = pl.BlockSpec((ks, 2 * n2, 2 * n2), lambda k: (k, 0, 0))
    blk = pl.BlockSpec((nb, 2, ks, n2, ch), lambda k: (0, 0, k, 0, 0))
    if spec is None:
        in_specs, args = [mat, blk], [mf, a5]
    else:
        in_specs = [mat, mat, blk, pl.BlockSpec((2, ks, n2, ch), lambda k: (0, k, 0, order))]
        args = [mf, mi, a5, spec]
    return pl.pallas_call(
        functools.partial(_fft_b_kernel, spec is not None),
        grid=(k1n // ks,),
        in_specs=in_specs,
        out_specs=blk,
        out_shape=jax.ShapeDtypeStruct(a5.shape, F32 if spec is None else BF16),
        compiler_params=_cparams(("parallel",)),
        name="fft_b",
    )(*args)


def _fft_a_inv_kernel(g_ref, p_ref, x_ref, z_ref, bias_ref, o_ref):
    g = g_ref[...]
    for b in range(p_ref.shape[0]):
        conv = jnp.einsum("nk,krc->nrc", g, p_ref[b], preferred_element_type=F32)
        o_ref[b] = x_ref[b] * (conv + z_ref[b] * bias_ref[...])


def _fft_a_inv(g, p4, xg4, z4, bias):
    nb, n1, n2, ch = z4.shape
    r = g.shape[1]
    bb, rt = _fft_blocking(nb, n1, n2, ch)
    blk = pl.BlockSpec((bb, n1, rt, ch), lambda b, i: (b, 0, i, 0))
    return pl.pallas_call(
        _fft_a_inv_kernel,
        grid=(nb // bb, n2 // rt),
        in_specs=[pl.BlockSpec((n1, r), lambda b, i: (0, 0)),
                  pl.BlockSpec((bb, r, rt, ch), lambda b, i: (b, 0, i, 0)), blk, blk,
                  pl.BlockSpec((1, 1, ch), lambda b, i: (0, 0, 0))],
        out_specs=blk,
        out_shape=jax.ShapeDtypeStruct(z4.shape, F32),
        compiler_params=_cparams(("parallel", "parallel")),
        name="fft_a_inv",
    )(g, p4, xg4, z4, bias.reshape(1, 1, ch))


def _fft_tables(ln):
    n = 2 * ln
    n1t = n // FFT_N2
    k1n = n1t // 2 + 1
    kk = np.arange(k1n)

    def stage_a(n1_in):
        ang = 2.0 * np.pi * ((kk[:, None] * np.arange(n1_in)[None, :]) % n1t) / n1t
        return jnp.asarray(np.concatenate([np.cos(ang), -np.sin(ang)], axis=0), BF16)

    n1o = n1t // 2
    ang = 2.0 * np.pi * ((np.arange(n1o)[:, None] * kk[None, :]) % n1t) / n1t
    edge = (kk == 0) | (kk == n1t // 2)
    ck = np.where(edge, 1.0, 2.0) / n
    g = jnp.asarray(np.concatenate([ck * np.cos(ang), -ck * np.where(edge, 0.0, np.sin(ang))], axis=1), BF16)

    k1 = jnp.arange(k1n, dtype=jnp.int32)[:, None, None]
    k2 = jnp.arange(FFT_N2, dtype=jnp.int32)[None, :, None]
    n2 = jnp.arange(FFT_N2, dtype=jnp.int32)[None, None, :]
    th = (2.0 * math.pi / n) * ((n2 * (k1 + n1t * k2)) % n).astype(F32)
    c, s = jnp.cos(th), jnp.sin(th)
    mf = jnp.concatenate([jnp.concatenate([c, s], axis=2), jnp.concatenate([-s, c], axis=2)], axis=1)
    return dict(fa_half=stage_a(n1o), fa_full=stage_a(n1t), g=g, mf=mf.astype(BF16),
                mi=mf.transpose(0, 2, 1).astype(BF16), k1n=k1n, n1o=n1o, n1t=n1t)


def _hy_static(ln):
    t = jnp.linspace(0.0, 1.0, ln, dtype=F32)[:, None]
    w = 2.0 * math.pi * jnp.arange(ln, dtype=F32) / ln
    f = jnp.linspace(1e-4, HY_BANDS - 1, HY_BANDS, dtype=F32)
    ang = w[:, None] * f[None, :]
    feats = jnp.concatenate([t, jnp.cos(ang), -jnp.sin(ang)], axis=-1)
    feats = jnp.pad(feats, ((0, 0), (0, LANE - HY_EMB)))
    min_decay = math.log(HY_DECAY_TARGET) / HY_SLOW_DECAY
    max_decay = math.log(HY_DECAY_TARGET) / HY_FAST_DECAY
    deltas = jnp.linspace(min_decay, max_decay, HY_CH, dtype=F32)
    window = jnp.exp(-t * jnp.abs(deltas))
    feats = jnp.concatenate([feats, feats[::-1]], axis=0)
    window = jnp.concatenate([window, window[::-1]], axis=0)
    return feats, window


def _outproj_kernel(x_ref, mod_ref, oa_ref, ob_ref, of_ref, obk_ref, od_ref, gate_ref, hg_ref, seg_ref, w_ref,
                    y_ref):
    gt = gate_ref[...]
    sg = gt * jax.nn.sigmoid(gt)
    oc = of_ref[...] + obk_ref[...]
    ss = _segsum(oc * oc, seg_ref[...]) * (1.0 / HGRN_DK)
    oc = oc * lax.rsqrt(ss + EPS) * hg_ref[...]
    acc = _bdot(oa_ref[0].T * sg[:, 0:256], w_ref[0:256, :])
    acc += _bdot(ob_ref[0].T * sg[:, 256:512], w_ref[256:512, :])
    acc += _bdot(oc * sg[:, 512:768], w_ref[512:768, :])
    acc += _bdot(od_ref[...] * sg[:, 768:1024], w_ref[768:1024, :])
    y_ref[...] = x_ref[...] + mod_ref[0, 2:3, :] * acc


def _outproj(x2, mod, ot_a, ot_b, o_f, o_b, out_d, p, hg, seg64, w_out, seq_len):
    n = x2.shape[0]
    tm = 256
    per_batch = mod.shape[0] > 1
    tps = seq_len // tm
    mod_idx = (lambda i: (i // tps, 0, 0)) if per_batch else (lambda i: (0, 0, 0))
    b256 = pl.BlockSpec((tm, 256), lambda i: (i, 0))
    bt = pl.BlockSpec((1, 256, tm), lambda i: (i // tps, 0, i % tps))
    return pl.pallas_call(
        _outproj_kernel,
        grid=(n // tm,),
        in_specs=[pl.BlockSpec((tm, D_MODEL), lambda i: (i, 0)),
                  pl.BlockSpec((1, 3, D_MODEL), mod_idx),
                  bt, bt, b256, b256, b256,
                  pl.BlockSpec((tm, 1024), lambda i: (i, P_GATE // 1024)),
                  pl.BlockSpec((1, 256), lambda i: (0, 0)),
                  pl.BlockSpec((256, 256), lambda i: (0, 0)),
                  pl.BlockSpec((D_MODEL, D_MODEL), lambda i: (0, 0))],
        out_specs=pl.BlockSpec((tm, D_MODEL), lambda i: (i, 0)),
        out_shape=jax.ShapeDtypeStruct((n, D_MODEL), F32),
        compiler_params=_cparams(("parallel",)),
        name="outproj",
    )(x2, mod, ot_a, ot_b, o_f, o_b, out_d, p, hg, seg64, w_out)


def _layer(x2, mod, lw, consts, batch, seq_len, ctx, rope_tabs, hy):
    n = batch * seq_len
    p = _inproj(x2, mod, lw["norm_g"], lw["w_in"], seq_len)

    q_a, ckvn, krp = _mla_q(p, lw, consts, seq_len, rope_tabs)
    ckv3 = ckvn.reshape(batch, seq_len, MLA_KV_LORA)
    krp3 = krp.reshape(batch, seq_len, LANE)
    k_a, vt_a = _mla_kv(ckvn, krp, lw, consts, batch)
    ks_a, vts_a = [k_a.reshape(batch, seq_len, 512)], [vt_a]
    if ctx is not None:
        lc = ctx[0].shape[1]
        k_c, vt_c = _mla_kv(ctx[0].reshape(batch * lc, MLA_KV_LORA), ctx[1].reshape(batch * lc, LANE), lw, consts,
                            batch)
        ks_a.append(k_c.reshape(batch, lc, 512))
        vts_a.append(vt_c)
    ot_a = _mla_attn(q_a, ks_a, vts_a, batch, seq_len)

    q_b, k_b, kd, vt_b = _diff_prep(p, lw, consts, seq_len, rope_tabs)
    ks_b, vts_b = [k_b.reshape(batch, seq_len, BRANCH)], [vt_b]
    if ctx is not None:
        ks_b.append(ctx[2])
        vts_b.append(ctx[3])
    ot_b = _diff_attn(q_b, ks_b, vts_b, lw["diff_lambda"], lw["subln_col"], lw["lam_init"], batch, seq_len)

    if ctx is not None:
        s0 = ctx[4]
    else:
        s0 = jnp.zeros((batch, 2, HGRN_HEADS, HGRN_DK, HGRN_DK), F32)
    eye = jnp.eye(HGRN_HEADS, dtype=F32)
    st0 = jnp.einsum("bdhke,hg->bdhegk", s0, eye).reshape(batch, 2, BRANCH, BRANCH)
    o_f, o_b, st_out = _hgrn(p, lw["hgrn_lb"], consts["hgrn_dd"], consts["hgrn_right"], st0, batch, seq_len)
    st5 = st_out.reshape(batch, 2, HGRN_HEADS, HGRN_DK, HGRN_HEADS, HGRN_DK)
    states = jnp.stack([st5[:, :, h, :, h, :] for h in range(HGRN_HEADS)], axis=2).swapaxes(-1, -2)

    v_d, x1, x2g = _hy_conv3(p, lw["hy_conv_w"], lw["hy_conv_b"], seq_len)
    taps = _hy_filter(hy["feats"], hy["window"], lw["hy_w1"], lw["hy_b1"], lw["hy_w2"], lw["hy_b2"], lw["hy_w3"],
                      lw["hy_freq"])
    k1n, n1o, n1t = hy["k1n"], hy["n1o"], hy["n1t"]
    ta = _fft_a(hy["fa_full"], taps.reshape(1, n1t, FFT_N2, HY_ORDER * HY_CH), BF16)
    spec = _fft_b(hy["mf"], None, ta.reshape(1, 2, k1n, FFT_N2, HY_ORDER * HY_CH), None, 0)[0]
    z4 = v_d.reshape(batch, n1o, FFT_N2, HY_CH)
    for o, xg in enumerate((x1, x2g)):
        a = _fft_a(hy["fa_half"], z4, BF16).reshape(batch, 2, k1n, FFT_N2, HY_CH)
        pk = _fft_b(hy["mf"], hy["mi"], a, spec, o).reshape(batch, 2 * k1n, FFT_N2, HY_CH)
        z4 = _fft_a_inv(hy["g"], pk, xg.reshape(batch, n1o, FFT_N2, HY_CH), z4, lw["hy_bias"][o:o + 1])
    out_d = z4.reshape(n, HY_CH)

    y = _outproj(x2, mod, ot_a, ot_b, o_f, o_b, out_d, p, lw["hgrn_out_g"], consts["seg64"], lw["w_out"], seq_len)
    new = None
    if ctx is None:
        new = (ckv3, krp3[:, :, KR_OFF:KR_OFF + MLA_ROPE],
               kd.reshape(batch, seq_len, DIFF_HEADS, 2, DIFF_HD),
               p[:, P_DV:P_DV + BRANCH].reshape(batch, seq_len, DIFF_HEADS, 2 * DIFF_HD), states)
    return y, new


def _rope_tables(seq_len):
    half = MLA_ROPE // 2
    inv = ROPE_BASE ** (-jnp.arange(0, half, 2, dtype=F32) / half)
    rows = seq_len // GRID_W
    row = jnp.repeat(jnp.arange(rows, dtype=F32), GRID_W)
    col = (jnp.arange(rows * GRID_W) % GRID_W).astype(F32)
    ar, ac = row[:, None] * inv, col[:, None] * inv
    cos32 = jnp.concatenate([jnp.cos(ar), jnp.cos(ar), jnp.cos(ac), jnp.cos(ac)], axis=-1)
    sin32 = jnp.concatenate([-jnp.sin(ar), jnp.sin(ar), -jnp.sin(ac), jnp.sin(ac)], axis=-1)
    pad = ((0, 0), (KR_OFF, LANE - KR_OFF - MLA_ROPE))
    return dict(cos_mla=jnp.pad(cos32, pad, constant_values=1.0), sin_mla=jnp.pad(sin32, pad),
                cos_diff=jnp.tile(cos32, (1, 2 * DIFF_HEADS)), sin_diff=jnp.tile(sin32, (1, 2 * DIFF_HEADS)))


def _hy_tables(seq_len):
    feats, window = _hy_static(seq_len)
    return dict(feats=feats, window=window, **_fft_tables(seq_len))


def _layer_weights(l, w_in_p, lb, W):
    def head_pad(w, width, per):
        k = w.shape[0]
        w = w.reshape(k, MLA_HEADS, per)[:, :, :width]
        return jnp.pad(w, ((0, 0), (0, 0), (0, LANE - width))).reshape(k, MLA_HEADS * LANE)

    w_ukv = W["mla_w_ukv"][l].reshape(MLA_KV_LORA, MLA_HEADS, MLA_NOPE + MLA_V)
    nope_g, rope_g = W["mla_nope_g"][l], W["mla_rope_g"][l]
    zeros32 = jnp.zeros((MLA_ROPE,), F32)
    zeros64 = jnp.zeros((MLA_NOPE,), F32)
    gq = jnp.tile(jnp.concatenate([nope_g[0], rope_g[0], zeros32]), MLA_HEADS).reshape(1, 512)
    gk = jnp.tile(jnp.concatenate([nope_g[1], zeros64]), MLA_HEADS).reshape(1, 512)
    gkr = jnp.concatenate([zeros64, rope_g[1], zeros32]).reshape(1, LANE)
    return dict(
        norm_g=W["norm_g"][l], w_in=w_in_p[l], w_out=W["w_out"][l].astype(BF16),
        qn_g=W["mla_q_norm_g"][l].reshape(1, -1),
        w_uq=head_pad(W["mla_w_uq"][l], MLA_NOPE + MLA_ROPE, MLA_NOPE + MLA_ROPE).astype(BF16),
        kvn_g=W["mla_kv_norm_g"][l].reshape(1, -1),
        w_uk=jnp.pad(w_ukv[:, :, :MLA_NOPE], ((0, 0), (0, 0), (0, LANE - MLA_NOPE))).reshape(MLA_KV_LORA, 512)
        .astype(BF16),
        w_uv=w_ukv[:, :, MLA_NOPE:].reshape(MLA_KV_LORA, BRANCH).T.astype(BF16),
        gq=gq, gk=gk, gkr=gkr,
        dgq=jnp.tile(W["diff_qk_g"][l, 0], 2 * DIFF_HEADS).reshape(1, BRANCH),
        dgk=jnp.tile(W["diff_qk_g"][l, 1], 2 * DIFF_HEADS).reshape(1, BRANCH),
        diff_lambda=W["diff_lambda"][l], subln_col=W["diff_subln_g"][l].reshape(2 * DIFF_HD, 1),
        lam_init=0.8 - 0.6 * math.exp(-0.3 * l),
        hgrn_lb=lb[:, l].reshape(2, 1, BRANCH),
        hgrn_out_g=jnp.tile(W["hgrn_out_g"][l], HGRN_HEADS).reshape(1, BRANCH),
        hy_conv_w=W["hy_conv_w"][l], hy_conv_b=W["hy_conv_b"][l].reshape(1, -1),
        hy_w1=jnp.pad(W["hy_w1"][l], ((0, LANE - HY_EMB), (0, 0))), hy_b1=W["hy_b1"][l].reshape(1, -1),
        hy_w2=W["hy_w2"][l], hy_b2=W["hy_b2"][l].reshape(1, -1),
        hy_w3=W["hy_w3"][l].reshape(HY_FH, HY_ORDER, 2, HY_CH).transpose(2, 0, 1, 3)
        .reshape(2, HY_FH, HY_ORDER * HY_CH),
        hy_freq=W["hy_sin_freq"][l], hy_bias=W["hy_bias"][l],
    )


def kernel(x_prompt, x_sample, cache_mla_ckv, cache_mla_krope, cache_diff_k, cache_diff_v, state_hgrn, c, c_ctx,
           norm_g, w_mod, b_mod, w_in, w_out, mla_q_norm_g, mla_w_uq, mla_kv_norm_g, mla_w_ukv, mla_nope_g,
           mla_rope_g, diff_qk_g, diff_lambda, diff_subln_g, hgrn_lb_logits, hgrn_out_g, hy_conv_w, hy_conv_b,
           hy_w1, hy_b1, hy_w2, hy_b2, hy_w3, hy_sin_freq, hy_bias):
    W = dict(norm_g=norm_g, w_out=w_out, mla_q_norm_g=mla_q_norm_g, mla_w_uq=mla_w_uq,
             mla_kv_norm_g=mla_kv_norm_g, mla_w_ukv=mla_w_ukv, mla_nope_g=mla_nope_g, mla_rope_g=mla_rope_g,
             diff_qk_g=diff_qk_g, diff_lambda=diff_lambda, diff_subln_g=diff_subln_g, hgrn_out_g=hgrn_out_g,
             hy_conv_w=hy_conv_w, hy_conv_b=hy_conv_b, hy_w1=hy_w1, hy_b1=hy_b1, hy_w2=hy_w2, hy_b2=hy_b2,
             hy_w3=hy_w3, hy_sin_freq=hy_sin_freq, hy_bias=hy_bias)
    bp, lp, _ = x_prompt.shape
    bs, ls, _ = x_sample.shape

    w_in_p = _reorder_in_cols(w_in.astype(BF16))
    cvecs = jnp.concatenate([c_ctx[None, :], c, jnp.zeros((8 - 1 - bs, D_MODEL), F32)], axis=0)
    mods = _mod_all(cvecs, w_mod, b_mod)
    lb = _hgrn_lb(hgrn_lb_logits)
    seg512, cnt512 = _mla_seg()
    hgrn_dd, hgrn_right = _hgrn_consts()
    consts = dict(seg512=seg512, cnt512=cnt512, seg32=_seg_const(BRANCH, DIFF_HD), seg64=_seg_const(BRANCH, HGRN_DK),
                  hgrn_dd=hgrn_dd, hgrn_right=hgrn_right)
    lws = [_layer_weights(l, w_in_p, lb, W) for l in range(DEPTH)]

    hy_p = _hy_tables(lp)
    y = x_prompt.reshape(bp * lp, D_MODEL)
    per_layer = []
    for l in range(DEPTH):
        mod = mods[l, 0:1].reshape(1, 3, D_MODEL)
        y, new = _layer(y, mod, lws[l], consts, bp, lp, None, None, hy_p)
        per_layer.append(new)
    y_prompt = y.reshape(bp, lp, D_MODEL)
    news = [jnp.stack([s[i] for s in per_layer], axis=1) for i in range(5)]

    hy_s = _hy_tables(ls)
    rope_tabs = _rope_tables(ls)
    y = x_sample.reshape(bs * ls, D_MODEL)
    past = cache_mla_ckv.shape[2]
    cache_kr = jnp.pad(cache_mla_krope, ((0, 0), (0, 0), (0, 0), (KR_OFF, LANE - KR_OFF - MLA_ROPE)))
    cache_kb = cache_diff_k.reshape(bs, DEPTH, past, BRANCH).astype(BF16)
    cache_vtb = _vt_with_ones(cache_diff_v.reshape(bs * DEPTH, past, BRANCH).astype(BF16))
    cache_vtb = cache_vtb.reshape(bs, DEPTH, DIFF_HEADS, VT_ROWS, past)
    for l in range(DEPTH):
        mod = mods[l, 1:1 + bs].reshape(bs, 3, D_MODEL)
        ctx = (cache_mla_ckv[:, l], cache_kr[:, l], cache_kb[:, l], cache_vtb[:, l], state_hgrn[:, l])
        y, _ = _layer(y, mod, lws[l], consts, bs, ls, ctx, rope_tabs, hy_s)
    y_sample = y.reshape(bs, ls, D_MODEL)

    return (y_prompt, y_sample, news[0], news[1], news[2], news[3], news[4])
```

```python
import functools
import math

import numpy as np
import jax
import jax.numpy as jnp
from jax import lax
from jax.experimental import pallas as pl
from jax.experimental.pallas import tpu as pltpu

F32 = jnp.float32
BF16 = jnp.bfloat16

D_MODEL = 1024
DEPTH = 4
GRID_W = 64
ROPE_BASE = 10000.0
EPS = 1e-6
BRANCH = 256
MLA_HEADS = 4
MLA_NOPE = 64
MLA_ROPE = 32
MLA_V = 64
MLA_Q_LORA = 256
MLA_KV_LORA = 128
MLA_SCALE = (MLA_NOPE + MLA_ROPE) ** -0.5
DIFF_HEADS = 4
DIFF_HD = 32
DIFF_SCALE = DIFF_HD ** -0.5
HGRN_HEADS = 4
HGRN_DK = 64
HGRN_CHUNK = 128
HGRN_LEVELS = 7
HGRN_ROWS = 2
HGRN_MM_LEVELS = 3
HY_CH = 256
HY_ORDER = 2
HY_EMB = 33
HY_BANDS = 16
HY_FH = 64
HY_DECAY_TARGET = 0.01
HY_FAST_DECAY = 0.3
HY_SLOW_DECAY = 1.5
IN_COLS = 4000

LANE = 128
LOG2E = math.log2(math.e)
VT_ROWS = 80
FFT_N2 = 128
FFT_BLOCK_BYTES = 2 * 1024 * 1024
MLA_AHEAD = 8
MLA_CHUNK = 512
DIFF_AHEAD = 6
DIFF_CHUNK = 256
MLA_SUB = 256
DIFF_SUB = 512
VMEM_LIMIT = 52 * 1024 * 1024

P_CQ, P_CKV, P_KR, P_DQ, P_DK, P_DV = 0, 256, 384, 512, 768, 1024
P_HQ, P_HZF, P_HZB, P_HI, P_HU, P_GATE = 1280, 1536, 1792, 2048, 2304, 3072
P_COLS = 4096
KR_OFF = 64


def _in_col_perm():
    src = np.full((P_COLS,), IN_COLS, np.int32)

    def put(dst, lo, n):
        src[dst:dst + n] = np.arange(lo, lo + n)

    put(P_CQ, 0, 256)
    put(P_CKV, 256, 128)
    put(P_KR + KR_OFF, 384, 32)
    put(P_GATE, 416, 256)
    put(P_DQ, 672, 256)
    put(P_DK, 928, 256)
    put(P_DV, 1184, 256)
    put(P_GATE + 256, 1440, 256)
    put(P_HQ, 1696, 256)
    put(P_HZF, 1952, 256)
    put(P_HZB, 2208, 256)
    put(P_HI, 2464, 256)
    put(P_GATE + 512, 2720, 256)
    put(P_HU, 2976, 768)
    put(P_GATE + 768, 3744, 256)
    return src


def _reorder_in_cols(w):
    src = _in_col_perm()
    pieces, lo = [], 0
    while lo < P_COLS:
        hi = lo + 1
        if src[lo] == IN_COLS:
            while hi < P_COLS and src[hi] == IN_COLS:
                hi += 1
            pieces.append(jnp.zeros(w.shape[:-1] + (hi - lo,), w.dtype))
        else:
            while hi < P_COLS and src[hi] == src[hi - 1] + 1:
                hi += 1
            pieces.append(w[..., int(src[lo]):int(src[lo]) + hi - lo])
        lo = hi
    return jnp.concatenate(pieces, axis=-1)


def _cparams(sem):
    return pltpu.CompilerParams(dimension_semantics=sem, vmem_limit_bytes=VMEM_LIMIT)


def _bdot(a, b):
    return jnp.dot(a.astype(BF16), b.astype(BF16), preferred_element_type=F32)


def _nt(a, b):
    return lax.dot_general(a.astype(BF16), b.astype(BF16), (((1,), (1,)), ((), ())), preferred_element_type=F32)


def _split2(a):
    hi = a.astype(BF16)
    lo = (a - hi.astype(F32)).astype(BF16)
    return hi, lo


def _dot3(a, b):
    ah, al = _split2(a)
    bh, bl = _split2(b)
    d = functools.partial(jnp.dot, preferred_element_type=F32)
    return d(ah, bh) + d(ah, bl) + d(al, bh)


def _segsum(v, seg):
    return jnp.dot(v.astype(BF16), seg, preferred_element_type=F32)


def _rms(x, g):
    return x * lax.rsqrt(jnp.mean(x * x, axis=-1, keepdims=True) + EPS) * g


def _swap8(x):
    w = x.shape[-1]
    lane = lax.broadcasted_iota(jnp.int32, x.shape, x.ndim - 1)
    up = pltpu.roll(x, w - 8, x.ndim - 1)
    dn = pltpu.roll(x, 8, x.ndim - 1)
    return jnp.where((lane & 15) < 8, up, dn)


def _tile_lanes(x, n):
    return x if n == 1 else jnp.concatenate([x] * n, axis=-1)


def _mod_kernel(c_ref, w_ref, b_ref, o_ref):
    c = c_ref[...]
    o_ref[0] = _dot3(c * jax.nn.sigmoid(c), w_ref[0]) + b_ref[0]


def _mod_all(cvecs, w_mod, b_mod):
    nt = 3
    return pl.pallas_call(
        _mod_kernel,
        grid=(DEPTH, nt),
        in_specs=[pl.BlockSpec((8, D_MODEL), lambda l, j: (0, 0)),
                  pl.BlockSpec((1, D_MODEL, D_MODEL), lambda l, j: (l, 0, j)),
                  pl.BlockSpec((1, 1, D_MODEL), lambda l, j: (l, 0, j))],
        out_specs=pl.BlockSpec((1, 8, D_MODEL), lambda l, j: (l, 0, j)),
        out_shape=jax.ShapeDtypeStruct((DEPTH, 8, 3 * D_MODEL), F32),
        compiler_params=_cparams(("arbitrary", "arbitrary")),
        name="mod",
    )(cvecs, w_mod, b_mod.reshape(DEPTH, 1, 3 * D_MODEL))


def _lb_kernel(x_ref, o_ref):
    x = x_ref[...]
    rows = [x[l:l + 1, :] for l in range(DEPTH)]
    m = functools.reduce(jnp.maximum, rows)
    e = [jnp.exp(r - m) for r in rows]
    tot = functools.reduce(lambda a, b: a + b, e)
    acc = jnp.zeros_like(tot)
    o_ref[0:1, :] = acc
    for l in range(1, DEPTH):
        acc = acc + e[l] / tot
        o_ref[l:l + 1, :] = acc


def _hgrn_lb(logits):
    flat = logits.transpose(1, 0, 2).reshape(DEPTH, 2 * BRANCH)
    lb = pl.pallas_call(
        _lb_kernel,
        out_shape=jax.ShapeDtypeStruct(flat.shape, F32),
        name="hgrn_lb",
    )(flat)
    return lb.reshape(DEPTH, 2, BRANCH).transpose(1, 0, 2)


def _inproj_kernel(x_ref, mod_ref, g_ref, w_ref, p_ref):
    h = _rms(x_ref[...], g_ref[...]) * (1.0 + mod_ref[0, 1:2, :]) + mod_ref[0, 0:1, :]
    p_ref[...] = jnp.dot(h.astype(BF16), w_ref[...], preferred_element_type=F32)


def _inproj(x2, mod, norm_g, w_in_p, seq_len):
    n = x2.shape[0]
    tm = min(512, seq_len)
    per_batch = mod.shape[0] > 1
    tiles_per_seq = seq_len // tm
    mod_idx = (lambda i: (i // tiles_per_seq, 0, 0)) if per_batch else (lambda i: (0, 0, 0))
    return pl.pallas_call(
        _inproj_kernel,
        grid=(n // tm,),
        in_specs=[pl.BlockSpec((tm, D_MODEL), lambda i: (i, 0)),
                  pl.BlockSpec((1, 3, D_MODEL), mod_idx),
                  pl.BlockSpec((1, D_MODEL), lambda i: (0, 0)),
                  pl.BlockSpec((D_MODEL, P_COLS), lambda i: (0, 0))],
        out_specs=pl.BlockSpec((tm, P_COLS), lambda i: (i, 0)),
        out_shape=jax.ShapeDtypeStruct((n, P_COLS), F32),
        compiler_params=_cparams(("parallel",)),
        name="inproj",
    )(x2, mod, norm_g.reshape(1, D_MODEL), w_in_p)


def _mla_seg():
    sid = np.zeros((512,), np.int32)
    cnt = np.ones((512,), np.float32)
    for h in range(MLA_HEADS):
        b = 128 * h
        sid[b:b + 64] = 3 * h
        sid[b + 64:b + 96] = 3 * h + 1
        sid[b + 96:b + 128] = 3 * h + 2
        cnt[b:b + 64] = 1.0 / 64
        cnt[b + 64:b + 128] = 1.0 / 32
    seg = (sid[:, None] == sid[None, :]).astype(np.float32)
    return jnp.asarray(seg, BF16), jnp.asarray(cnt.reshape(1, 512))


def _mla_q_kernel(rope, cq_ref, ckv_ref, kr_ref, qng_ref, wuq_ref, kvg_ref, gq_ref, gkr_ref, seg_ref, cnt_ref,
                  *rest):
    if rope:
        cos_ref, sin_ref, q_ref, ckvn_ref, krp_ref = rest
    else:
        q_ref, ckvn_ref, krp_ref = rest
    cqn = _rms(cq_ref[...], qng_ref[...])
    q = _bdot(cqn, wuq_ref[...])
    ss = _segsum(q * q, seg_ref[...]) * cnt_ref[...]
    qn = q * lax.rsqrt(ss + EPS) * gq_ref[...]
    ckvn_ref[...] = _rms(ckv_ref[...], kvg_ref[...])
    kr = kr_ref[...]
    krn = kr * lax.rsqrt(jnp.sum(kr * kr, axis=-1, keepdims=True) * (1.0 / MLA_ROPE) + EPS) * gkr_ref[...]
    if rope:
        cos, sin = cos_ref[...], sin_ref[...]
        qn = qn * _tile_lanes(cos, MLA_HEADS) + _swap8(qn) * _tile_lanes(sin, MLA_HEADS)
        krn = krn * cos + _swap8(krn) * sin
    q_ref[...] = (qn * (MLA_SCALE * LOG2E)).astype(BF16)
    krp_ref[...] = krn


def _mla_q(p, lw, consts, seq_len, rope_tabs):
    n = p.shape[0]
    tm = min(512, seq_len)
    rope = rope_tabs is not None
    full = lambda shape: pl.BlockSpec(shape, lambda i: (0,) * len(shape))
    in_specs = [pl.BlockSpec((tm, 256), lambda i: (i, P_CQ // 256)),
                pl.BlockSpec((tm, 128), lambda i: (i, P_CKV // 128)),
                pl.BlockSpec((tm, 128), lambda i: (i, P_KR // 128)),
                full((1, 256)), full((256, 512)), full((1, 128)), full((1, 512)), full((1, 128)),
                full((512, 512)), full((1, 512))]
    args = [p, p, p, lw["qn_g"], lw["w_uq"], lw["kvn_g"], lw["gq"], lw["gkr"], consts["seg512"], consts["cnt512"]]
    if rope:
        tps = seq_len // tm
        in_specs += [pl.BlockSpec((tm, 128), lambda i: (i % tps, 0))] * 2
        args += [rope_tabs["cos_mla"], rope_tabs["sin_mla"]]
    return pl.pallas_call(
        functools.partial(_mla_q_kernel, rope),
        grid=(n // tm,),
        in_specs=in_specs,
        out_specs=[pl.BlockSpec((tm, 512), lambda i: (i, 0)),
                   pl.BlockSpec((tm, 128), lambda i: (i, 0)),
                   pl.BlockSpec((tm, 128), lambda i: (i, 0))],
        out_shape=[jax.ShapeDtypeStruct((n, 512), BF16),
                   jax.ShapeDtypeStruct((n, 128), F32),
                   jax.ShapeDtypeStruct((n, 128), F32)],
        compiler_params=_cparams(("parallel",)),
        name="mla_q",
    )(*args)


def _store_vt(o_ref, vt):
    tm = vt.shape[1]
    row = lax.broadcasted_iota(jnp.int32, (VT_ROWS - 64, tm), 0)
    extra = jnp.where(row == 0, 1.0, 0.0).astype(BF16)
    for h in range(4):
        o_ref[0, h, 0:64, :] = vt[64 * h:64 * (h + 1)].astype(BF16)
        o_ref[0, h, 64:VT_ROWS, :] = extra


def _mla_kv_kernel(ckvn_ref, krp_ref, wuk_ref, wuv_ref, gk_ref, seg_ref, cnt_ref, k_ref, vt_ref):
    c = ckvn_ref[...].astype(BF16)
    kn = jnp.dot(c, wuk_ref[...], preferred_element_type=F32)
    ss = _segsum(kn * kn, seg_ref[...]) * cnt_ref[...]
    k = kn * lax.rsqrt(ss + EPS) * gk_ref[...] + _tile_lanes(krp_ref[...], MLA_HEADS)
    k_ref[...] = k.astype(BF16)
    _store_vt(vt_ref, _nt(wuv_ref[...], c))


def _mla_kv(ckvn, krp, lw, consts, batch):
    n = ckvn.shape[0]
    lseg = n // batch
    tm = min(512, lseg)
    tpb = lseg // tm
    full = lambda shape: pl.BlockSpec(shape, lambda i: (0,) * len(shape))
    return pl.pallas_call(
        _mla_kv_kernel,
        grid=(n // tm,),
        in_specs=[pl.BlockSpec((tm, 128), lambda i: (i, 0)), pl.BlockSpec((tm, 128), lambda i: (i, 0)),
                  full((128, 512)), full((256, 128)), full((1, 512)), full((512, 512)), full((1, 512))],
        out_specs=[pl.BlockSpec((tm, 512), lambda i: (i, 0)),
                   pl.BlockSpec((1, 4, VT_ROWS, tm), lambda i: (i // tpb, 0, 0, i % tpb))],
        out_shape=[jax.ShapeDtypeStruct((n, 512), BF16), jax.ShapeDtypeStruct((batch, 4, VT_ROWS, lseg), BF16)],
        compiler_params=_cparams(("parallel",)),
        name="mla_kv",
    )(ckvn, krp, lw["w_uk"], lw["w_uv"], lw["gk"], consts["seg512"], consts["cnt512"])


def _softmax_pv(qs, k_refs, vt_refs, key_chunk, sub_rows, n_ahead, k_lane=0, v_head=0):
    where = [(i, lo) for i, r in enumerate(k_refs) for lo in range(0, r.shape[1], key_chunk)]
    nch = len(where)
    nq = len(qs)
    sub = min(sub_rows, key_chunk)
    nsub = key_chunk // sub

    def scores(c, u):
        seg, lo = where[c]
        ks = k_refs[seg][0, lo + u * sub:lo + (u + 1) * sub, k_lane:k_lane + LANE]
        return [_nt(ks, q) for q in qs]

    def chunk_max(s_chunk, j):
        mc = functools.reduce(jnp.maximum, [s_chunk[u][j] for u in range(nsub)])
        return jnp.max(mc, axis=0, keepdims=True)

    s_buf = {c: [scores(c, u) for u in range(nsub)] for c in range(min(n_ahead, nch))}
    m = [None] * nq
    acc = [None] * nq
    m_new = [chunk_max(s_buf[0], j) for j in range(nq)]
    for c in range(nch):
        s_cur = s_buf.pop(c)
        ahead = c + n_ahead
        if ahead < nch:
            s_buf[ahead] = []
        pv = [None] * nq
        for u in range(nsub):
            if ahead < nch:
                s_buf[ahead].append(scores(ahead, u))
            seg, lo = where[c]
            vs = vt_refs[seg][0, v_head, :, lo + u * sub:lo + (u + 1) * sub]
            for j in range(nq):
                part = jnp.dot(vs, jnp.exp2(s_cur[u][j] - m_new[j]).astype(BF16), preferred_element_type=F32)
                pv[j] = part if pv[j] is None else pv[j] + part
        for j in range(nq):
            acc[j] = pv[j] if c == 0 else acc[j] * jnp.exp2(m[j] - m_new[j]) + pv[j]
            m[j] = m_new[j]
        if c + 1 < nch:
            m_new = [jnp.maximum(m[j], chunk_max(s_buf[c + 1], j)) for j in range(nq)]
    return acc


def _mla_attn_kernel(key_chunk, nseg, heads, q_ref, *refs):
    k_refs, vt_refs, o_ref = refs[:nseg], refs[nseg:2 * nseg], refs[2 * nseg]
    for hh in range(heads):
        q = q_ref[:, LANE * hh:LANE * (hh + 1)]
        (acc,) = _softmax_pv([q], k_refs, vt_refs, key_chunk, MLA_SUB, MLA_AHEAD, LANE * hh, hh)
        o_ref[0, MLA_V * hh:MLA_V * (hh + 1)] = acc[0:MLA_V] / acc[MLA_V:MLA_V + 1]


def _key_chunk(lk, rows=512):
    return rows if lk % rows == 0 else lk


def _heads_per_step(ks):
    return 4 if sum(k.shape[1] for k in ks) <= 512 else 1


def _vt_kernel(v_ref, o_ref):
    _store_vt(o_ref, v_ref[0].astype(F32).T)


def _vt_with_ones(v3):
    b, lk, width = v3.shape
    tm = 1536 if lk % 1536 == 0 else lk
    return pl.pallas_call(
        _vt_kernel,
        grid=(b, lk // tm),
        in_specs=[pl.BlockSpec((1, tm, width), lambda i, j: (i, j, 0))],
        out_specs=pl.BlockSpec((1, 4, VT_ROWS, tm), lambda i, j: (i, 0, 0, j)),
        out_shape=jax.ShapeDtypeStruct((b, 4, VT_ROWS, lk), BF16),
        compiler_params=_cparams(("parallel", "parallel")),
        name="vt_ones",
    )(v3)


def _mla_attn(q, ks, vts, batch, seq_len):
    tq = 256
    nq = seq_len // tq
    chunk = _key_chunk(min(k.shape[1] for k in ks), MLA_CHUNK)
    hp = _heads_per_step(ks)
    k_specs = [pl.BlockSpec((1, k.shape[1], LANE * hp), lambda b, h, i: (b, 0, h)) for k in ks]
    v_specs = [pl.BlockSpec((1, hp, VT_ROWS, v.shape[3]), lambda b, h, i: (b, h, 0, 0)) for v in vts]
    return pl.pallas_call(
        functools.partial(_mla_attn_kernel, chunk, len(ks), hp),
        grid=(batch, MLA_HEADS // hp, nq),
        in_specs=[pl.BlockSpec((tq, LANE * hp), lambda b, h, i: (b * nq + i, h))] + k_specs + v_specs,
        out_specs=pl.BlockSpec((1, MLA_V * hp, tq), lambda b, h, i: (b, h, i)),
        out_shape=jax.ShapeDtypeStruct((batch, BRANCH, seq_len), F32),
        compiler_params=_cparams(("parallel", "parallel", "arbitrary")),
        name="mla_attn",
    )(q, *ks, *vts)


def _seg_const(width, seg):
    sid = np.arange(width) // seg
    return jnp.asarray((sid[:, None] == sid[None, :]).astype(np.float32), BF16)


def _diff_prep_kernel(rope, dq_ref, dk_ref, dv_ref, gq_ref, gk_ref, seg_ref, *rest):
    if rope:
        cos_ref, sin_ref, q_ref, k_ref, kf_ref, vt_ref = rest
    else:
        q_ref, k_ref, kf_ref, vt_ref = rest
    seg = seg_ref[...]
    _store_vt(vt_ref, dv_ref[...].T)

    def norm(x, g):
        ss = _segsum(x * x, seg) * (1.0 / DIFF_HD)
        return x * lax.rsqrt(ss + EPS) * g

    q = norm(dq_ref[...], gq_ref[...])
    k = norm(dk_ref[...], gk_ref[...])
    kf_ref[...] = k
    if rope:
        cos, sin = cos_ref[...], sin_ref[...]
        q = q * cos + _swap8(q) * sin
        k = k * cos + _swap8(k) * sin
    q_ref[...] = (q * (DIFF_SCALE * LOG2E)).astype(BF16)
    k_ref[...] = k.astype(BF16)


def _diff_prep(p, lw, consts, seq_len, rope_tabs):
    n = p.shape[0]
    tm = min(512, seq_len)
    rope = rope_tabs is not None
    full = lambda shape: pl.BlockSpec(shape, lambda i: (0,) * len(shape))
    in_specs = [pl.BlockSpec((tm, 256), lambda i: (i, P_DQ // 256)),
                pl.BlockSpec((tm, 256), lambda i: (i, P_DK // 256)),
                pl.BlockSpec((tm, 256), lambda i: (i, P_DV // 256)),
                full((1, 256)), full((1, 256)), full((256, 256))]
    args = [p, p, p, lw["dgq"], lw["dgk"], consts["seg32"]]
    tps = seq_len // tm
    if rope:
        in_specs += [pl.BlockSpec((tm, 256), lambda i: (i % tps, 0))] * 2
        args += [rope_tabs["cos_diff"], rope_tabs["sin_diff"]]
    blk = pl.BlockSpec((tm, 256), lambda i: (i, 0))
    return pl.pallas_call(
        functools.partial(_diff_prep_kernel, rope),
        grid=(n // tm,),
        in_specs=in_specs,
        out_specs=[blk, blk, blk, pl.BlockSpec((1, 4, VT_ROWS, tm), lambda i: (i // tps, 0, 0, i % tps))],
        out_shape=[jax.ShapeDtypeStruct((n, 256), BF16), jax.ShapeDtypeStruct((n, 256), BF16),
                   jax.ShapeDtypeStruct((n, 256), F32),
                   jax.ShapeDtypeStruct((n // seq_len, 4, VT_ROWS, seq_len), BF16)],
        compiler_params=_cparams(("parallel",)),
        name="diff_prep",
    )(*args)


def _diff_attn_kernel(lam_init, key_chunk, nseg, heads, q_ref, *refs):
    k_refs, vt_refs = refs[:nseg], refs[nseg:2 * nseg]
    lp_ref, g_ref, o_ref = refs[2 * nseg:]
    lp = lp_ref[...]
    lam = (jnp.exp(jnp.sum(lp[0:1] * lp[1:2], axis=1, keepdims=True))
           - jnp.exp(jnp.sum(lp[2:3] * lp[3:4], axis=1, keepdims=True)) + lam_init)
    for hh in range(heads):
        blk = 0 if heads == 1 else LANE * (hh // 2)
        base = (pl.program_id(1) % 2) * 64 if heads == 1 else (hh % 2) * 64
        q = q_ref[:, blk:blk + LANE]
        lane = lax.broadcasted_iota(jnp.int32, q.shape, 1)
        zero = jnp.zeros_like(q)

        def map_query(j):
            lo = base + 32 * j
            return jnp.where((lane >= lo) & (lane < lo + 32), q, zero)

        acc0, acc1 = _softmax_pv([map_query(0), map_query(1)], k_refs, vt_refs, key_chunk, DIFF_SUB, DIFF_AHEAD,
                                 blk, hh)
        o = acc0[0:64] / acc0[64:65] - lam * (acc1[0:64] / acc1[64:65])
        ms = jnp.mean(o * o, axis=0, keepdims=True)
        o_ref[0, 64 * hh:64 * (hh + 1)] = o * lax.rsqrt(ms + EPS) * g_ref[...] * (1.0 - lam_init)


def _diff_attn(q, ks, vts, lp, g_col, lam_init, batch, seq_len):
    tq = 256
    nq = seq_len // tq
    chunk = _key_chunk(min(k.shape[1] for k in ks), DIFF_CHUNK)
    hp = _heads_per_step(ks)
    lanes = LANE if hp == 1 else BRANCH
    k_specs = [pl.BlockSpec((1, k.shape[1], lanes), lambda b, h, i: (b, 0, h // 2)) for k in ks]
    v_specs = [pl.BlockSpec((1, hp, VT_ROWS, v.shape[3]), lambda b, h, i: (b, h, 0, 0)) for v in vts]
    return pl.pallas_call(
        functools.partial(_diff_attn_kernel, lam_init, chunk, len(ks), hp),
        grid=(batch, DIFF_HEADS // hp, nq),
        in_specs=[pl.BlockSpec((tq, lanes), lambda b, h, i: (b * nq + i, h // 2))] + k_specs + v_specs
        + [pl.BlockSpec((4, DIFF_HD), lambda b, h, i: (0, 0)), pl.BlockSpec((64, 1), lambda b, h, i: (0, 0))],
        out_specs=pl.BlockSpec((1, 64 * hp, tq), lambda b, h, i: (b, h, i)),
        out_shape=jax.ShapeDtypeStruct((batch, BRANCH, seq_len), F32),
        compiler_params=_cparams(("parallel", "parallel", "arbitrary")),
        name="diff_attn",
    )(q, *ks, *vts, lp, g_col)


def _hgrn_consts():
    c = HGRN_CHUNK
    t = np.arange(c)
    low = (t[None, :] <= t[:, None]).astype(np.float32)
    blocks = []
    for j in range(HGRN_MM_LEVELS):
        m = 1 << j
        rho = (t // (2 * m)) * (2 * m) + m - 1
        sign = np.where((t // m) % 2 == 1, 1.0, -1.0)[:, None]
        blocks.append(sign * (low - (t[None, :] <= rho[:, None]).astype(np.float32)))
    blocks.append(low)
    fwd = np.concatenate(blocks, axis=0)
    bwd = np.concatenate([b[::-1, ::-1] for b in blocks], axis=0)
    right = np.stack([(t // (1 << j)) % 2 for j in range(HGRN_LEVELS)]).astype(np.float32)
    right = np.stack([right, right[:, ::-1]])
    right = np.broadcast_to(right[..., None], right.shape + (BRANCH,))
    return jnp.asarray(np.stack([fwd, bwd]), BF16), jnp.asarray(right, F32)


def _hgrn_kernel(nc, nb, qf_ref, zf_ref, vf_ref, qb_ref, zb_ref, vb_ref, lb_ref, dd_ref, rm_ref, s0_ref,
                 of_ref, ob_ref, sout_ref, st_ref):
    c = HGRN_CHUNK
    ci = pl.program_id(1)
    chains = [(bi, d) for bi in range(nb) for d in (0, 1)]
    ids = range(len(chains))

    @pl.when(ci == 0)
    def _():
        st_ref[...] = s0_ref[...]

    lane = lax.broadcasted_iota(jnp.int32, (1, BRANCH), 1)
    head_masks = [(lane >= HGRN_DK * h) & (lane < HGRN_DK * (h + 1)) for h in range(HGRN_HEADS)]
    t_idx = lax.broadcasted_iota(jnp.int32, (c, HGRN_HEADS * c), 0)
    s_idx = lax.broadcasted_iota(jnp.int32, (c, HGRN_HEADS * c), 1) & (c - 1)
    pair_xor = t_idx ^ s_idx

    def stack_heads(x):
        xb = x.astype(BF16)
        zero = jnp.zeros_like(xb)
        return jnp.concatenate([jnp.where(hm, xb, zero) for hm in head_masks], axis=0)

    q_refs, z_refs, v_refs = (qf_ref, qb_ref), (zf_ref, zb_ref), (vf_ref, vb_ref)
    q = [q_refs[d][bi] for bi, d in chains]
    v = [v_refs[d][bi] for bi, d in chains]
    z = [z_refs[d][bi] for bi, d in chains]
    lb = [lb_ref[d] for _, d in chains]
    g = [jnp.log(lb[i] + (1.0 - lb[i]) * jax.nn.sigmoid(z[i])) for i in ids]
    kk = [(1.0 - lb[i]) * jax.nn.sigmoid(-z[i]) for i in ids]
    sums = []
    for i in ids:
        gh, gl = _split2(g[i])
        dd = dd_ref[chains[i][1]]
        sums.append(jnp.dot(dd, gh, preferred_element_type=F32) + jnp.dot(dd, gl, preferred_element_type=F32))
    b = [sums[i][HGRN_MM_LEVELS * c:] for i in ids]
    b_tot = [b[i][c - 1:c] if chains[i][1] == 0 else b[i][0:1] for i in ids]

    def neg_abs_decay(i, j):
        if j < HGRN_MM_LEVELS:
            return sums[i][j * c:(j + 1) * c]
        m = 1 << j
        off = m - 1 if chains[i][1] == 0 else m
        ref = jnp.concatenate([jnp.broadcast_to(b[i][g0 + off:g0 + off + 1], (2 * m, BRANCH))
                               for g0 in range(0, c, 2 * m)], axis=0)
        return -jnp.abs(b[i] - ref)

    a = [None] * len(chains)
    for j in reversed(range(HGRN_LEVELS)):
        same_group = pair_xor < (2 << j)
        for i in ids:
            e = jnp.exp(neg_abs_decay(i, j))
            eq = e * rm_ref[chains[i][1], j]
            qt = q[i] * eq
            kt = kk[i] * (e - eq)
            lvl = _nt(qt, stack_heads(kt))
            a[i] = lvl if a[i] is None else jnp.where(same_group, lvl, a[i])
    diagonal = pair_xor == 0
    for i in ids:
        a[i] = jnp.where(diagonal, _nt(q[i], stack_heads(kk[i])), a[i])

    outs = (of_ref, ob_ref)
    for i in ids:
        bi, d = chains[i]
        o = jnp.dot(a[i].astype(BF16), stack_heads(v[i]), preferred_element_type=F32)
        outs[d][bi] = o + _nt(q[i] * jnp.exp(b[i]), st_ref[bi, d])

    r2 = lax.broadcasted_iota(jnp.int32, (BRANCH, BRANCH), 0) // HGRN_DK
    c2 = lax.broadcasted_iota(jnp.int32, (BRANCH, BRANCH), 1) // HGRN_DK
    for i in ids:
        bi, d = chains[i]
        kd = kk[i] * jnp.exp(b_tot[i] - b[i])
        upd = lax.dot_general(v[i].astype(BF16), kd.astype(BF16), (((0,), (0,)), ((), ())),
                              preferred_element_type=F32)
        st_new = st_ref[bi, d] * jnp.exp(b_tot[i]) + jnp.where(r2 == c2, upd, 0.0)
        st_ref[bi, d] = st_new

        @pl.when(ci == nc - 1)
        def _(bi=bi, d=d, st_new=st_new):
            sout_ref[bi, d] = st_new


def _hgrn(p, lb_l, dd, rm, st0, batch, seq_len):
    n = p.shape[0]
    c = HGRN_CHUNK
    nc = seq_len // c
    nb = HGRN_ROWS if batch % HGRN_ROWS == 0 else 1
    p3 = p.reshape(batch, seq_len, P_COLS)
    fwd = lambda col: pl.BlockSpec((nb, c, 256), lambda b, i: (b, i, col))
    bwd = lambda col: pl.BlockSpec((nb, c, 256), lambda b, i: (b, nc - 1 - i, col))
    whole = lambda shape: pl.BlockSpec(shape, lambda b, i: (0,) * len(shape))
    state = pl.BlockSpec((nb, 2, 256, 256), lambda b, i: (b, 0, 0, 0))
    o_f, o_b, st = pl.pallas_call(
        functools.partial(_hgrn_kernel, nc, nb),
        grid=(batch // nb, nc),
        in_specs=[fwd(P_HQ // 256), fwd(P_HZF // 256), fwd(P_HI // 256),
                  bwd(P_HQ // 256), bwd(P_HZB // 256), bwd(P_HI // 256),
                  whole((2, 1, 256)), whole((2, (HGRN_MM_LEVELS + 1) * c, c)),
                  whole((2, HGRN_LEVELS, c, 256)), state],
        out_specs=[pl.BlockSpec((nb, c, 256), lambda b, i: (b, i, 0)),
                   pl.BlockSpec((nb, c, 256), lambda b, i: (b, nc - 1 - i, 0)), state],
        out_shape=[jax.ShapeDtypeStruct((batch, seq_len, 256), F32),
                   jax.ShapeDtypeStruct((batch, seq_len, 256), F32),
                   jax.ShapeDtypeStruct((batch, 2, 256, 256), F32)],
        scratch_shapes=[pltpu.VMEM((nb, 2, 256, 256), F32)],
        compiler_params=_cparams(("parallel", "arbitrary")),
        name="hgrn",
    )(p3, p3, p3, p3, p3, p3, lb_l, dd, rm, st0)
    return o_f.reshape(n, 256), o_b.reshape(n, 256), st


def _hy_conv3_kernel(tiles_per_seq, above_ref, cur_ref, below_ref, w_ref, b_ref, v_ref, x1_ref, x2_ref):
    i = pl.program_id(0)
    cur = cur_ref[...]
    tm = cur.shape[0]
    first = (i % tiles_per_seq) == 0
    last = (i % tiles_per_seq) == tiles_per_seq - 1
    above = jnp.where(first, 0.0, above_ref[7:8, :])
    below = jnp.where(last, 0.0, below_ref[0:1, :])
    row = lax.broadcasted_iota(jnp.int32, (tm, 1), 0)
    prev = jnp.where(row == 0, above, pltpu.roll(cur, 1, 0))
    nxt = jnp.where(row == tm - 1, below, pltpu.roll(cur, tm - 1, 0))
    w = w_ref[...]
    u = prev * w[0:1] + cur * w[1:2] + nxt * w[2:3] + b_ref[...]
    v_ref[...] = u[:, 0:256]
    x1_ref[...] = u[:, 256:512]
    x2_ref[...] = u[:, 512:768]


def _hy_conv3(p, w, b, seq_len):
    n = p.shape[0]
    tm = min(512, seq_len)
    nt = n // tm
    g = tm // 8
    col = P_HU // 768
    oblk = pl.BlockSpec((tm, 256), lambda i: (i, 0))
    return pl.pallas_call(
        functools.partial(_hy_conv3_kernel, seq_len // tm),
        grid=(nt,),
        in_specs=[pl.BlockSpec((8, 768), lambda i: (jnp.maximum(i * g - 1, 0), col)),
                  pl.BlockSpec((tm, 768), lambda i: (i, col)),
                  pl.BlockSpec((8, 768), lambda i: (jnp.minimum((i + 1) * g, nt * g - 1), col)),
                  pl.BlockSpec((3, 768), lambda i: (0, 0)), pl.BlockSpec((1, 768), lambda i: (0, 0))],
        out_specs=[oblk, oblk, oblk],
        out_shape=[jax.ShapeDtypeStruct((n, 256), F32)] * 3,
        compiler_params=_cparams(("parallel",)),
        name="hy_conv3",
    )(p, p, p, w, b)


def _hy_filter_kernel(feat_ref, w1_ref, b1_ref, w2_ref, b2_ref, w3_ref, fr_ref, win_ref, o_ref):
    fr = fr_ref[...]
    h = jnp.sin(fr[0:1] * (_dot3(feat_ref[...], w1_ref[...]) + b1_ref[...]))
    h = jnp.sin(fr[1:2] * (_dot3(h, w2_ref[...]) + b2_ref[...]))
    o_ref[...] = _dot3(h, w3_ref[0]) * _tile_lanes(win_ref[...], HY_ORDER)


def _hy_filter(feats2, window2, w1p, b1, w2, b2, w3d, freq):
    l2 = feats2.shape[0]
    ln = l2 // 2
    tm = min(512, ln)
    full = lambda shape: pl.BlockSpec(shape, lambda i: (0,) * len(shape))
    return pl.pallas_call(
        _hy_filter_kernel,
        grid=(l2 // tm,),
        in_specs=[pl.BlockSpec((tm, LANE), lambda i: (i, 0)),
                  full((LANE, HY_FH)), full((1, HY_FH)), full((HY_FH, HY_FH)), full((1, HY_FH)),
                  pl.BlockSpec((1, HY_FH, HY_ORDER * HY_CH), lambda i: (i // (ln // tm), 0, 0)), full((2, HY_FH)),
                  pl.BlockSpec((tm, HY_CH), lambda i: (i, 0))],
        out_specs=pl.BlockSpec((tm, HY_ORDER * HY_CH), lambda i: (i, 0)),
        out_shape=jax.ShapeDtypeStruct((l2, HY_ORDER * HY_CH), F32),
        compiler_params=_cparams(("parallel",)),
        name="hy_filter",
    )(feats2, w1p, b1, w2, b2, w3d, freq, window2)


def _fft_blocking(nb, n1, n2, ch):
    per_batch = n1 * n2 * ch * 4
    if per_batch <= FFT_BLOCK_BYTES:
        bb = max(1, min(nb, FFT_BLOCK_BYTES // per_batch))
        while nb % bb:
            bb -= 1
        return bb, n2
    rt = n2
    while n1 * rt * ch * 4 > FFT_BLOCK_BYTES and rt > 8:
        rt //= 2
    return 1, rt


def _fft_a_kernel(f_ref, x_ref, o_ref):
    f = f_ref[...]
    for b in range(x_ref.shape[0]):
        x = x_ref[b].astype(BF16)
        o_ref[b] = jnp.einsum("kn,nrc->krc", f, x, preferred_element_type=F32).astype(o_ref.dtype)


def _fft_a(fa, x4, out_dtype):
    nb, n1, n2, ch = x4.shape
    r = fa.shape[0]
    bb, rt = _fft_blocking(nb, n1, n2, ch)
    return pl.pallas_call(
        _fft_a_kernel,
        grid=(nb // bb, n2 // rt),
        in_specs=[pl.BlockSpec((r, n1), lambda b, i: (0, 0)),
                  pl.BlockSpec((bb, n1, rt, ch), lambda b, i: (b, 0, i, 0))],
        out_specs=pl.BlockSpec((bb, r, rt, ch), lambda b, i: (b, 0, i, 0)),
        out_shape=jax.ShapeDtypeStruct((nb, r, n2, ch), out_dtype),
        compiler_params=_cparams(("parallel", "parallel")),
        name="fft_a",
    )(fa, x4)


def _fft_b_kernel(with_inverse, mf_ref, *rest):
    if with_inverse:
        mi_ref, a_ref, h_ref, o_ref = rest
    else:
        a_ref, o_ref = rest
    half = FFT_N2
    for kk in range(a_ref.shape[2]):
        for b in range(a_ref.shape[0]):
            a = jnp.concatenate([a_ref[b, 0, kk], a_ref[b, 1, kk]], axis=0)
            x = jnp.dot(mf_ref[kk], a.astype(BF16), preferred_element_type=F32)
            if with_inverse:
                xr, xi = x[:half], x[half:]
                hr, hi = h_ref[0, kk], h_ref[1, kk]
                y = jnp.concatenate([xr * hr - xi * hi, xr * hi + xi * hr], axis=0)
                x = jnp.dot(mi_ref[kk], y.astype(BF16), preferred_element_type=F32)
            o_ref[b, 0, kk] = x[:half].astype(o_ref.dtype)
            o_ref[b, 1, kk] = x[half:].astype(o_ref.dtype)


def _fft_b(mf, mi, a5, spec, order):
    nb, _, k1n, n2, ch = a5.shape
    ks = 3 if (k1n % 3 == 0 and nb * ch <= 1024) else 1
    mat = pl.BlockSpec((ks, 2 * n2, 2 * n2), lambda k: (k, 0, 0))
    blk = pl.BlockSpec((nb, 2, ks, n2, ch), lambda k: (0, 0, k, 0, 0))
    if spec is None:
        in_specs, args = [mat, blk], [mf, a5]
    else:
        in_specs = [mat, mat, blk, pl.BlockSpec((2, ks, n2, ch), lambda k: (0, k, 0, order))]
        args = [mf, mi, a5, spec]
    return pl.pallas_call(
        functools.partial(_fft_b_kernel, spec is not None),
        grid=(k1n // ks,),
        in_specs=in_specs,
        out_specs=blk,
        out_shape=jax.ShapeDtypeStruct(a5.shape, F32 if spec is None else BF16),
        compiler_params=_cparams(("parallel",)),
        name="fft_b",
    )(*args)


def _fft_a_inv_kernel(g_ref, p_ref, x_ref, z_ref, bias_ref, o_ref):
    g = g_ref[...]
    for b in range(p_ref.shape[0]):
        conv = jnp.einsum("nk,krc->nrc", g, p_ref[b], preferred_element_type=F32)
        o_ref[b] = x_ref[b] * (conv + z_ref[b] * bias_ref[...])


def _fft_a_inv(g, p4, xg4, z4, bias):
    nb, n1, n2, ch = z4.shape
    r = g.shape[1]
    bb, rt = _fft_blocking(nb, n1, n2, ch)
    blk = pl.BlockSpec((bb, n1, rt, ch), lambda b, i: (b, 0, i, 0))
    return pl.pallas_call(
        _fft_a_inv_kernel,
        grid=(nb // bb, n2 // rt),
        in_specs=[pl.BlockSpec((n1, r), lambda b, i: (0, 0)),
                  pl.BlockSpec((bb, r, rt, ch), lambda b, i: (b, 0, i, 0)), blk, blk,
                  pl.BlockSpec((1, 1, ch), lambda b, i: (0, 0, 0))],
        out_specs=blk,
        out_shape=jax.ShapeDtypeStruct(z4.shape, F32),
        compiler_params=_cparams(("parallel", "parallel")),
        name="fft_a_inv",
    )(g, p4, xg4, z4, bias.reshape(1, 1, ch))


def _fft_tables(ln):
    n = 2 * ln
    n1t = n // FFT_N2
    k1n = n1t // 2 + 1
    kk = np.arange(k1n)

    def stage_a(n1_in):
        ang = 2.0 * np.pi * ((kk[:, None] * np.arange(n1_in)[None, :]) % n1t) / n1t
        return jnp.asarray(np.concatenate([np.cos(ang), -np.sin(ang)], axis=0), BF16)

    n1o = n1t // 2
    ang = 2.0 * np.pi * ((np.arange(n1o)[:, None] * kk[None, :]) % n1t) / n1t
    edge = (kk == 0) | (kk == n1t // 2)
    ck = np.where(edge, 1.0, 2.0) / n
    g = jnp.asarray(np.concatenate([ck * np.cos(ang), -ck * np.where(edge, 0.0, np.sin(ang))], axis=1), BF16)

    k1 = jnp.arange(k1n, dtype=jnp.int32)[:, None, None]
    k2 = jnp.arange(FFT_N2, dtype=jnp.int32)[None, :, None]
    n2 = jnp.arange(FFT_N2, dtype=jnp.int32)[None, None, :]
    th = (2.0 * math.pi / n) * ((n2 * (k1 + n1t * k2)) % n).astype(F32)
    c, s = jnp.cos(th), jnp.sin(th)
    mf = jnp.concatenate([jnp.concatenate([c, s], axis=2), jnp.concatenate([-s, c], axis=2)], axis=1)
    return dict(fa_half=stage_a(n1o), fa_full=stage_a(n1t), g=g, mf=mf.astype(BF16),
                mi=mf.transpose(0, 2, 1).astype(BF16), k1n=k1n, n1o=n1o, n1t=n1t)


def _hy_static(ln):
    t = jnp.linspace(0.0, 1.0, ln, dtype=F32)[:, None]
    w = 2.0 * math.pi * jnp.arange(ln, dtype=F32) / ln
    f = jnp.linspace(1e-4, HY_BANDS - 1, HY_BANDS, dtype=F32)
    ang = w[:, None] * f[None, :]
    feats = jnp.concatenate([t, jnp.cos(ang), -jnp.sin(ang)], axis=-1)
    feats = jnp.pad(feats, ((0, 0), (0, LANE - HY_EMB)))
    min_decay = math.log(HY_DECAY_TARGET) / HY_SLOW_DECAY
    max_decay = math.log(HY_DECAY_TARGET) / HY_FAST_DECAY
    deltas = jnp.linspace(min_decay, max_decay, HY_CH, dtype=F32)
    window = jnp.exp(-t * jnp.abs(deltas))
    feats = jnp.concatenate([feats, feats[::-1]], axis=0)
    window = jnp.concatenate([window, window[::-1]], axis=0)
    return feats, window


def _outproj_kernel(x_ref, mod_ref, oa_ref, ob_ref, of_ref, obk_ref, od_ref, gate_ref, hg_ref, seg_ref, w_ref,
                    y_ref):
    gt = gate_ref[...]
    sg = gt * jax.nn.sigmoid(gt)
    oc = of_ref[...] + obk_ref[...]
    ss = _segsum(oc * oc, seg_ref[...]) * (1.0 / HGRN_DK)
    oc = oc * lax.rsqrt(ss + EPS) * hg_ref[...]
    acc = _bdot(oa_ref[0].T * sg[:, 0:256], w_ref[0:256, :])
    acc += _bdot(ob_ref[0].T * sg[:, 256:512], w_ref[256:512, :])
    acc += _bdot(oc * sg[:, 512:768], w_ref[512:768, :])
    acc += _bdot(od_ref[...] * sg[:, 768:1024], w_ref[768:1024, :])
    y_ref[...] = x_ref[...] + mod_ref[0, 2:3, :] * acc


def _outproj(x2, mod, ot_a, ot_b, o_f, o_b, out_d, p, hg, seg64, w_out, seq_len):
    n = x2.shape[0]
    tm = min(512, seq_len)
    per_batch = mod.shape[0] > 1
    tps = seq_len // tm
    mod_idx = (lambda i: (i // tps, 0, 0)) if per_batch else (lambda i: (0, 0, 0))
    b256 = pl.BlockSpec((tm, 256), lambda i: (i, 0))
    bt = pl.BlockSpec((1, 256, tm), lambda i: (i // tps, 0, i % tps))
    return pl.pallas_call(
        _outproj_kernel,
        grid=(n // tm,),
        in_specs=[pl.BlockSpec((tm, D_MODEL), lambda i: (i, 0)),
                  pl.BlockSpec((1, 3, D_MODEL), mod_idx),
                  bt, bt, b256, b256, b256,
                  pl.BlockSpec((tm, 1024), lambda i: (i, P_GATE // 1024)),
                  pl.BlockSpec((1, 256), lambda i: (0, 0)),
                  pl.BlockSpec((256, 256), lambda i: (0, 0)),
                  pl.BlockSpec((D_MODEL, D_MODEL), lambda i: (0, 0))],
        out_specs=pl.BlockSpec((tm, D_MODEL), lambda i: (i, 0)),
        out_shape=jax.ShapeDtypeStruct((n, D_MODEL), F32),
        compiler_params=_cparams(("parallel",)),
        name="outproj",
    )(x2, mod, ot_a, ot_b, o_f, o_b, out_d, p, hg, seg64, w_out)


def _layer(x2, mod, lw, consts, batch, seq_len, ctx, rope_tabs, hy):
    n = batch * seq_len
    p = _inproj(x2, mod, lw["norm_g"], lw["w_in"], seq_len)

    q_a, ckvn, krp = _mla_q(p, lw, consts, seq_len, rope_tabs)
    ckv3 = ckvn.reshape(batch, seq_len, MLA_KV_LORA)
    krp3 = krp.reshape(batch, seq_len, LANE)
    k_a, vt_a = _mla_kv(ckvn, krp, lw, consts, batch)
    ks_a, vts_a = [k_a.reshape(batch, seq_len, 512)], [vt_a]
    if ctx is not None:
        lc = ctx[0].shape[1]
        k_c, vt_c = _mla_kv(ctx[0].reshape(batch * lc, MLA_KV_LORA), ctx[1].reshape(batch * lc, LANE), lw, consts,
                            batch)
        ks_a.append(k_c.reshape(batch, lc, 512))
        vts_a.append(vt_c)
    ot_a = _mla_attn(q_a, ks_a, vts_a, batch, seq_len)

    q_b, k_b, kd, vt_b = _diff_prep(p, lw, consts, seq_len, rope_tabs)
    ks_b, vts_b = [k_b.reshape(batch, seq_len, BRANCH)], [vt_b]
    if ctx is not None:
        ks_b.append(ctx[2])
        vts_b.append(ctx[3])
    ot_b = _diff_attn(q_b, ks_b, vts_b, lw["diff_lambda"], lw["subln_col"], lw["lam_init"], batch, seq_len)

    if ctx is not None:
        s0 = ctx[4]
    else:
        s0 = jnp.zeros((batch, 2, HGRN_HEADS, HGRN_DK, HGRN_DK), F32)
    eye = jnp.eye(HGRN_HEADS, dtype=F32)
    st0 = jnp.einsum("bdhke,hg->bdhegk", s0, eye).reshape(batch, 2, BRANCH, BRANCH)
    o_f, o_b, st_out = _hgrn(p, lw["hgrn_lb"], consts["hgrn_dd"], consts["hgrn_right"], st0, batch, seq_len)
    st5 = st_out.reshape(batch, 2, HGRN_HEADS, HGRN_DK, HGRN_HEADS, HGRN_DK)
    states = jnp.stack([st5[:, :, h, :, h, :] for h in range(HGRN_HEADS)], axis=2).swapaxes(-1, -2)

    v_d, x1, x2g = _hy_conv3(p, lw["hy_conv_w"], lw["hy_conv_b"], seq_len)
    taps = _hy_filter(hy["feats"], hy["window"], lw["hy_w1"], lw["hy_b1"], lw["hy_w2"], lw["hy_b2"], lw["hy_w3"],
                      lw["hy_freq"])
    k1n, n1o, n1t = hy["k1n"], hy["n1o"], hy["n1t"]
    ta = _fft_a(hy["fa_full"], taps.reshape(1, n1t, FFT_N2, HY_ORDER * HY_CH), BF16)
    spec = _fft_b(hy["mf"], None, ta.reshape(1, 2, k1n, FFT_N2, HY_ORDER * HY_CH), None, 0)[0]
    z4 = v_d.reshape(batch, n1o, FFT_N2, HY_CH)
    for o, xg in enumerate((x1, x2g)):
        a = _fft_a(hy["fa_half"], z4, BF16).reshape(batch, 2, k1n, FFT_N2, HY_CH)
        pk = _fft_b(hy["mf"], hy["mi"], a, spec, o).reshape(batch, 2 * k1n, FFT_N2, HY_CH)
        z4 = _fft_a_inv(hy["g"], pk, xg.reshape(batch, n1o, FFT_N2, HY_CH), z4, lw["hy_bias"][o:o + 1])
    out_d = z4.reshape(n, HY_CH)

    y = _outproj(x2, mod, ot_a, ot_b, o_f, o_b, out_d, p, lw["hgrn_out_g"], consts["seg64"], lw["w_out"], seq_len)
    new = None
    if ctx is None:
        new = (ckv3, krp3[:, :, KR_OFF:KR_OFF + MLA_ROPE],
               kd.reshape(batch, seq_len, DIFF_HEADS, 2, DIFF_HD),
               p[:, P_DV:P_DV + BRANCH].reshape(batch, seq_len, DIFF_HEADS, 2 * DIFF_HD), states)
    return y, new


def _rope_tables(seq_len):
    half = MLA_ROPE // 2
    inv = ROPE_BASE ** (-jnp.arange(0, half, 2, dtype=F32) / half)
    rows = seq_len // GRID_W
    row = jnp.repeat(jnp.arange(rows, dtype=F32), GRID_W)
    col = (jnp.arange(rows * GRID_W) % GRID_W).astype(F32)
    ar, ac = row[:, None] * inv, col[:, None] * inv
    cos32 = jnp.concatenate([jnp.cos(ar), jnp.cos(ar), jnp.cos(ac), jnp.cos(ac)], axis=-1)
    sin32 = jnp.concatenate([-jnp.sin(ar), jnp.sin(ar), -jnp.sin(ac), jnp.sin(ac)], axis=-1)
    pad = ((0, 0), (KR_OFF, LANE - KR_OFF - MLA_ROPE))
    return dict(cos_mla=jnp.pad(cos32, pad, constant_values=1.0), sin_mla=jnp.pad(sin32, pad),
                cos_diff=jnp.tile(cos32, (1, 2 * DIFF_HEADS)), sin_diff=jnp.tile(sin32, (1, 2 * DIFF_HEADS)))


def _hy_tables(seq_len):
    feats, window = _hy_static(seq_len)
    return dict(feats=feats, window=window, **_fft_tables(seq_len))


def _layer_weights(l, w_in_p, lb, W):
    def head_pad(w, width, per):
        k = w.shape[0]
        w = w.reshape(k, MLA_HEADS, per)[:, :, :width]
        return jnp.pad(w, ((0, 0), (0, 0), (0, LANE - width))).reshape(k, MLA_HEADS * LANE)

    w_ukv = W["mla_w_ukv"][l].reshape(MLA_KV_LORA, MLA_HEADS, MLA_NOPE + MLA_V)
    nope_g, rope_g = W["mla_nope_g"][l], W["mla_rope_g"][l]
    zeros32 = jnp.zeros((MLA_ROPE,), F32)
    zeros64 = jnp.zeros((MLA_NOPE,), F32)
    gq = jnp.tile(jnp.concatenate([nope_g[0], rope_g[0], zeros32]), MLA_HEADS).reshape(1, 512)
    gk = jnp.tile(jnp.concatenate([nope_g[1], zeros64]), MLA_HEADS).reshape(1, 512)
    gkr = jnp.concatenate([zeros64, rope_g[1], zeros32]).reshape(1, LANE)
    return dict(
        norm_g=W["norm_g"][l], w_in=w_in_p[l], w_out=W["w_out"][l].astype(BF16),
        qn_g=W["mla_q_norm_g"][l].reshape(1, -1),
        w_uq=head_pad(W["mla_w_uq"][l], MLA_NOPE + MLA_ROPE, MLA_NOPE + MLA_ROPE).astype(BF16),
        kvn_g=W["mla_kv_norm_g"][l].reshape(1, -1),
        w_uk=jnp.pad(w_ukv[:, :, :MLA_NOPE], ((0, 0), (0, 0), (0, LANE - MLA_NOPE))).reshape(MLA_KV_LORA, 512)
        .astype(BF16),
        w_uv=w_ukv[:, :, MLA_NOPE:].reshape(MLA_KV_LORA, BRANCH).T.astype(BF16),
        gq=gq, gk=gk, gkr=gkr,
        dgq=jnp.tile(W["diff_qk_g"][l, 0], 2 * DIFF_HEADS).reshape(1, BRANCH),
        dgk=jnp.tile(W["diff_qk_g"][l, 1], 2 * DIFF_HEADS).reshape(1, BRANCH),
        diff_lambda=W["diff_lambda"][l], subln_col=W["diff_subln_g"][l].reshape(2 * DIFF_HD, 1),
        lam_init=0.8 - 0.6 * math.exp(-0.3 * l),
        hgrn_lb=lb[:, l].reshape(2, 1, BRANCH),
        hgrn_out_g=jnp.tile(W["hgrn_out_g"][l], HGRN_HEADS).reshape(1, BRANCH),
        hy_conv_w=W["hy_conv_w"][l], hy_conv_b=W["hy_conv_b"][l].reshape(1, -1),
        hy_w1=jnp.pad(W["hy_w1"][l], ((0, LANE - HY_EMB), (0, 0))), hy_b1=W["hy_b1"][l].reshape(1, -1),
        hy_w2=W["hy_w2"][l], hy_b2=W["hy_b2"][l].reshape(1, -1),
        hy_w3=W["hy_w3"][l].reshape(HY_FH, HY_ORDER, 2, HY_CH).transpose(2, 0, 1, 3)
        .reshape(2, HY_FH, HY_ORDER * HY_CH),
        hy_freq=W["hy_sin_freq"][l], hy_bias=W["hy_bias"][l],
    )


def kernel(x_prompt, x_sample, cache_mla_ckv, cache_mla_krope, cache_diff_k, cache_diff_v, state_hgrn, c, c_ctx,
           norm_g, w_mod, b_mod, w_in, w_out, mla_q_norm_g, mla_w_uq, mla_kv_norm_g, mla_w_ukv, mla_nope_g,
           mla_rope_g, diff_qk_g, diff_lambda, diff_subln_g, hgrn_lb_logits, hgrn_out_g, hy_conv_w, hy_conv_b,
           hy_w1, hy_b1, hy_w2, hy_b2, hy_w3, hy_sin_freq, hy_bias):
    W = dict(norm_g=norm_g, w_out=w_out, mla_q_norm_g=mla_q_norm_g, mla_w_uq=mla_w_uq,
             mla_kv_norm_g=mla_kv_norm_g, mla_w_ukv=mla_w_ukv, mla_nope_g=mla_nope_g, mla_rope_g=mla_rope_g,
             diff_qk_g=diff_qk_g, diff_lambda=diff_lambda, diff_subln_g=diff_subln_g, hgrn_out_g=hgrn_out_g,
             hy_conv_w=hy_conv_w, hy_conv_b=hy_conv_b, hy_w1=hy_w1, hy_b1=hy_b1, hy_w2=hy_w2, hy_b2=hy_b2,
             hy_w3=hy_w3, hy_sin_freq=hy_sin_freq, hy_bias=hy_bias)
    bp, lp, _ = x_prompt.shape
    bs, ls, _ = x_sample.shape

    w_in_p = _reorder_in_cols(w_in.astype(BF16))
    cvecs = jnp.concatenate([c_ctx[None, :], c, jnp.zeros((8 - 1 - bs, D_MODEL), F32)], axis=0)
    mods = _mod_all(cvecs, w_mod, b_mod)
    lb = _hgrn_lb(hgrn_lb_logits)
    seg512, cnt512 = _mla_seg()
    hgrn_dd, hgrn_right = _hgrn_consts()
    consts = dict(seg512=seg512, cnt512=cnt512, seg32=_seg_const(BRANCH, DIFF_HD), seg64=_seg_const(BRANCH, HGRN_DK),
                  hgrn_dd=hgrn_dd, hgrn_right=hgrn_right)
    lws = [_layer_weights(l, w_in_p, lb, W) for l in range(DEPTH)]

    hy_p = _hy_tables(lp)
    y = x_prompt.reshape(bp * lp, D_MODEL)
    per_layer = []
    for l in range(DEPTH):
        mod = mods[l, 0:1].reshape(1, 3, D_MODEL)
        y, new = _layer(y, mod, lws[l], consts, bp, lp, None, None, hy_p)
        per_layer.append(new)
    y_prompt = y.reshape(bp, lp, D_MODEL)
    news = [jnp.stack([s[i] for s in per_layer], axis=1) for i in range(5)]

    hy_s = _hy_tables(ls)
    rope_tabs = _rope_tables(ls)
    y = x_sample.reshape(bs * ls, D_MODEL)
    past = cache_mla_ckv.shape[2]
    cache_kr = jnp.pad(cache_mla_krope, ((0, 0), (0, 0), (0, 0), (KR_OFF, LANE - KR_OFF - MLA_ROPE)))
    cache_kb = cache_diff_k.reshape(bs, DEPTH, past, BRANCH).astype(BF16)
    cache_vtb = _vt_with_ones(cache_diff_v.reshape(bs * DEPTH, past, BRANCH).astype(BF16))
    cache_vtb = cache_vtb.reshape(bs, DEPTH, DIFF_HEADS, VT_ROWS, past)
    for l in range(DEPTH):
        mod = mods[l, 1:1 + bs].reshape(bs, 3, D_MODEL)
        ctx = (cache_mla_ckv[:, l], cache_kr[:, l], cache_kb[:, l], cache_vtb[:, l], state_hgrn[:, l])
        y, _ = _layer(y, mod, lws[l], consts, bs, ls, ctx, rope_tabs, hy_s)
    y_sample = y.reshape(bs, ls, D_MODEL)

    return (y_prompt, y_sample, news[0], news[1], news[2], news[3], news[4])
```

```python
import functools
import math

import numpy as np
import jax
import jax.numpy as jnp
from jax import lax
from jax.experimental import pallas as pl
from jax.experimental.pallas import tpu as pltpu

F32 = jnp.float32
BF16 = jnp.bfloat16

D_MODEL = 1024
DEPTH = 4
GRID_W = 64
ROPE_BASE = 10000.0
EPS = 1e-6
BRANCH = 256
MLA_HEADS = 4
MLA_NOPE = 64
MLA_ROPE = 32
MLA_V = 64
MLA_Q_LORA = 256
MLA_KV_LORA = 128
MLA_SCALE = (MLA_NOPE + MLA_ROPE) ** -0.5
DIFF_HEADS = 4
DIFF_HD = 32
DIFF_SCALE = DIFF_HD ** -0.5
HGRN_HEADS = 4
HGRN_DK = 64
HGRN_CHUNK = 128
HGRN_LEVELS = 7
HGRN_ROWS = 2
HGRN_MM_LEVELS = 3
HY_CH = 256
HY_ORDER = 2
HY_EMB = 33
HY_BANDS = 16
HY_FH = 64
HY_DECAY_TARGET = 0.01
HY_FAST_DECAY = 0.3
HY_SLOW_DECAY = 1.5
IN_COLS = 4000

LANE = 128
LOG2E = math.log2(math.e)
VT_ROWS = 80
FFT_N2 = 128
FFT_BLOCK_BYTES = 4 * 1024 * 1024
MLA_AHEAD = 8
MLA_CHUNK = 512
DIFF_AHEAD = 6
DIFF_CHUNK = 256
MLA_SUB = 256
DIFF_SUB = 512
VMEM_LIMIT = 52 * 1024 * 1024

P_CQ, P_CKV, P_KR, P_DQ, P_DK, P_DV = 0, 256, 384, 512, 768, 1024
P_HQ, P_HZF, P_HZB, P_HI, P_HU, P_GATE = 1280, 1536, 1792, 2048, 2304, 3072
P_COLS = 4096
KR_OFF = 64


def _in_col_perm():
    src = np.full((P_COLS,), IN_COLS, np.int32)

    def put(dst, lo, n):
        src[dst:dst + n] = np.arange(lo, lo + n)

    put(P_CQ, 0, 256)
    put(P_CKV, 256, 128)
    put(P_KR + KR_OFF, 384, 32)
    put(P_GATE, 416, 256)
    put(P_DQ, 672, 256)
    put(P_DK, 928, 256)
    put(P_DV, 1184, 256)
    put(P_GATE + 256, 1440, 256)
    put(P_HQ, 1696, 256)
    put(P_HZF, 1952, 256)
    put(P_HZB, 2208, 256)
    put(P_HI, 2464, 256)
    put(P_GATE + 512, 2720, 256)
    put(P_HU, 2976, 768)
    put(P_GATE + 768, 3744, 256)
    return src


def _reorder_in_cols(w):
    src = _in_col_perm()
    pieces, lo = [], 0
    while lo < P_COLS:
        hi = lo + 1
        if src[lo] == IN_COLS:
            while hi < P_COLS and src[hi] == IN_COLS:
                hi += 1
            pieces.append(jnp.zeros(w.shape[:-1] + (hi - lo,), w.dtype))
        else:
            while hi < P_COLS and src[hi] == src[hi - 1] + 1:
                hi += 1
            pieces.append(w[..., int(src[lo]):int(src[lo]) + hi - lo])
        lo = hi
    return jnp.concatenate(pieces, axis=-1)


def _cparams(sem):
    return pltpu.CompilerParams(dimension_semantics=sem, vmem_limit_bytes=VMEM_LIMIT)


def _bdot(a, b):
    return jnp.dot(a.astype(BF16), b.astype(BF16), preferred_element_type=F32)


def _nt(a, b):
    return lax.dot_general(a.astype(BF16), b.astype(BF16), (((1,), (1,)), ((), ())), preferred_element_type=F32)


def _split2(a):
    hi = a.astype(BF16)
    lo = (a - hi.astype(F32)).astype(BF16)
    return hi, lo


def _dot3(a, b):
    ah, al = _split2(a)
    bh, bl = _split2(b)
    d = functools.partial(jnp.dot, preferred_element_type=F32)
    return d(ah, bh) + d(ah, bl) + d(al, bh)


def _segsum(v, seg):
    return jnp.dot(v.astype(BF16), seg, preferred_element_type=F32)


def _rms(x, g):
    return x * lax.rsqrt(jnp.mean(x * x, axis=-1, keepdims=True) + EPS) * g


def _swap8(x):
    w = x.shape[-1]
    lane = lax.broadcasted_iota(jnp.int32, x.shape, x.ndim - 1)
    up = pltpu.roll(x, w - 8, x.ndim - 1)
    dn = pltpu.roll(x, 8, x.ndim - 1)
    return jnp.where((lane & 15) < 8, up, dn)


def _tile_lanes(x, n):
    return x if n == 1 else jnp.concatenate([x] * n, axis=-1)


def _mod_kernel(c_ref, w_ref, b_ref, o_ref):
    c = c_ref[...]
    o_ref[0] = _dot3(c * jax.nn.sigmoid(c), w_ref[0]) + b_ref[0]


def _mod_all(cvecs, w_mod, b_mod):
    nt = 3
    return pl.pallas_call(
        _mod_kernel,
        grid=(DEPTH, nt),
        in_specs=[pl.BlockSpec((8, D_MODEL), lambda l, j: (0, 0)),
                  pl.BlockSpec((1, D_MODEL, D_MODEL), lambda l, j: (l, 0, j)),
                  pl.BlockSpec((1, 1, D_MODEL), lambda l, j: (l, 0, j))],
        out_specs=pl.BlockSpec((1, 8, D_MODEL), lambda l, j: (l, 0, j)),
        out_shape=jax.ShapeDtypeStruct((DEPTH, 8, 3 * D_MODEL), F32),
        compiler_params=_cparams(("arbitrary", "arbitrary")),
        name="mod",
    )(cvecs, w_mod, b_mod.reshape(DEPTH, 1, 3 * D_MODEL))


def _lb_kernel(x_ref, o_ref):
    x = x_ref[...]
    rows = [x[l:l + 1, :] for l in range(DEPTH)]
    m = functools.reduce(jnp.maximum, rows)
    e = [jnp.exp(r - m) for r in rows]
    tot = functools.reduce(lambda a, b: a + b, e)
    acc = jnp.zeros_like(tot)
    o_ref[0:1, :] = acc
    for l in range(1, DEPTH):
        acc = acc + e[l] / tot
        o_ref[l:l + 1, :] = acc


def _hgrn_lb(logits):
    flat = logits.transpose(1, 0, 2).reshape(DEPTH, 2 * BRANCH)
    lb = pl.pallas_call(
        _lb_kernel,
        out_shape=jax.ShapeDtypeStruct(flat.shape, F32),
        name="hgrn_lb",
    )(flat)
    return lb.reshape(DEPTH, 2, BRANCH).transpose(1, 0, 2)


def _inproj_kernel(x_ref, mod_ref, g_ref, w_ref, p_ref):
    h = _rms(x_ref[...], g_ref[...]) * (1.0 + mod_ref[0, 1:2, :]) + mod_ref[0, 0:1, :]
    p_ref[...] = jnp.dot(h.astype(BF16), w_ref[...], preferred_element_type=F32)


def _inproj(x2, mod, norm_g, w_in_p, seq_len):
    n = x2.shape[0]
    tm = min(512, seq_len)
    per_batch = mod.shape[0] > 1
    tiles_per_seq = seq_len // tm
    mod_idx = (lambda i: (i // tiles_per_seq, 0, 0)) if per_batch else (lambda i: (0, 0, 0))
    return pl.pallas_call(
        _inproj_kernel,
        grid=(n // tm,),
        in_specs=[pl.BlockSpec((tm, D_MODEL), lambda i: (i, 0)),
                  pl.BlockSpec((1, 3, D_MODEL), mod_idx),
                  pl.BlockSpec((1, D_MODEL), lambda i: (0, 0)),
                  pl.BlockSpec((D_MODEL, P_COLS), lambda i: (0, 0))],
        out_specs=pl.BlockSpec((tm, P_COLS), lambda i: (i, 0)),
        out_shape=jax.ShapeDtypeStruct((n, P_COLS), F32),
        compiler_params=_cparams(("parallel",)),
        name="inproj",
    )(x2, mod, norm_g.reshape(1, D_MODEL), w_in_p)


def _mla_seg():
    sid = np.zeros((512,), np.int32)
    cnt = np.ones((512,), np.float32)
    for h in range(MLA_HEADS):
        b = 128 * h
        sid[b:b + 64] = 3 * h
        sid[b + 64:b + 96] = 3 * h + 1
        sid[b + 96:b + 128] = 3 * h + 2
        cnt[b:b + 64] = 1.0 / 64
        cnt[b + 64:b + 128] = 1.0 / 32
    seg = (sid[:, None] == sid[None, :]).astype(np.float32)
    return jnp.asarray(seg, BF16), jnp.asarray(cnt.reshape(1, 512))


def _mla_q_kernel(rope, cq_ref, ckv_ref, kr_ref, qng_ref, wuq_ref, kvg_ref, gq_ref, gkr_ref, seg_ref, cnt_ref,
                  *rest):
    if rope:
        cos_ref, sin_ref, q_ref, ckvn_ref, krp_ref = rest
    else:
        q_ref, ckvn_ref, krp_ref = rest
    cqn = _rms(cq_ref[...], qng_ref[...])
    q = _bdot(cqn, wuq_ref[...])
    ss = _segsum(q * q, seg_ref[...]) * cnt_ref[...]
    qn = q * lax.rsqrt(ss + EPS) * gq_ref[...]
    ckvn_ref[...] = _rms(ckv_ref[...], kvg_ref[...])
    kr = kr_ref[...]
    krn = kr * lax.rsqrt(jnp.sum(kr * kr, axis=-1, keepdims=True) * (1.0 / MLA_ROPE) + EPS) * gkr_ref[...]
    if rope:
        cos, sin = cos_ref[...], sin_ref[...]
        qn = qn * _tile_lanes(cos, MLA_HEADS) + _swap8(qn) * _tile_lanes(sin, MLA_HEADS)
        krn = krn * cos + _swap8(krn) * sin
    q_ref[...] = (qn * (MLA_SCALE * LOG2E)).astype(BF16)
    krp_ref[...] = krn


def _mla_q(p, lw, consts, seq_len, rope_tabs):
    n = p.shape[0]
    tm = min(512, seq_len)
    rope = rope_tabs is not None
    full = lambda shape: pl.BlockSpec(shape, lambda i: (0,) * len(shape))
    in_specs = [pl.BlockSpec((tm, 256), lambda i: (i, P_CQ // 256)),
                pl.BlockSpec((tm, 128), lambda i: (i, P_CKV // 128)),
                pl.BlockSpec((tm, 128), lambda i: (i, P_KR // 128)),
                full((1, 256)), full((256, 512)), full((1, 128)), full((1, 512)), full((1, 128)),
                full((512, 512)), full((1, 512))]
    args = [p, p, p, lw["qn_g"], lw["w_uq"], lw["kvn_g"], lw["gq"], lw["gkr"], consts["seg512"], consts["cnt512"]]
    if rope:
        tps = seq_len // tm
        in_specs += [pl.BlockSpec((tm, 128), lambda i: (i % tps, 0))] * 2
        args += [rope_tabs["cos_mla"], rope_tabs["sin_mla"]]
    return pl.pallas_call(
        functools.partial(_mla_q_kernel, rope),
        grid=(n // tm,),
        in_specs=in_specs,
        out_specs=[pl.BlockSpec((tm, 512), lambda i: (i, 0)),
                   pl.BlockSpec((tm, 128), lambda i: (i, 0)),
                   pl.BlockSpec((tm, 128), lambda i: (i, 0))],
        out_shape=[jax.ShapeDtypeStruct((n, 512), BF16),
                   jax.ShapeDtypeStruct((n, 128), F32),
                   jax.ShapeDtypeStruct((n, 128), F32)],
        compiler_params=_cparams(("parallel",)),
        name="mla_q",
    )(*args)


def _store_vt(o_ref, vt):
    tm = vt.shape[1]
    row = lax.broadcasted_iota(jnp.int32, (VT_ROWS - 64, tm), 0)
    extra = jnp.where(row == 0, 1.0, 0.0).astype(BF16)
    for h in range(4):
        o_ref[0, h, 0:64, :] = vt[64 * h:64 * (h + 1)].astype(BF16)
        o_ref[0, h, 64:VT_ROWS, :] = extra


def _mla_kv_kernel(ckvn_ref, krp_ref, wuk_ref, wuv_ref, gk_ref, seg_ref, cnt_ref, k_ref, vt_ref):
    c = ckvn_ref[...].astype(BF16)
    kn = jnp.dot(c, wuk_ref[...], preferred_element_type=F32)
    ss = _segsum(kn * kn, seg_ref[...]) * cnt_ref[...]
    k = kn * lax.rsqrt(ss + EPS) * gk_ref[...] + _tile_lanes(krp_ref[...], MLA_HEADS)
    k_ref[...] = k.astype(BF16)
    _store_vt(vt_ref, _nt(wuv_ref[...], c))


def _mla_kv(ckvn, krp, lw, consts, batch):
    n = ckvn.shape[0]
    lseg = n // batch
    tm = min(512, lseg)
    tpb = lseg // tm
    full = lambda shape: pl.BlockSpec(shape, lambda i: (0,) * len(shape))
    return pl.pallas_call(
        _mla_kv_kernel,
        grid=(n // tm,),
        in_specs=[pl.BlockSpec((tm, 128), lambda i: (i, 0)), pl.BlockSpec((tm, 128), lambda i: (i, 0)),
                  full((128, 512)), full((256, 128)), full((1, 512)), full((512, 512)), full((1, 512))],
        out_specs=[pl.BlockSpec((tm, 512), lambda i: (i, 0)),
                   pl.BlockSpec((1, 4, VT_ROWS, tm), lambda i: (i // tpb, 0, 0, i % tpb))],
        out_shape=[jax.ShapeDtypeStruct((n, 512), BF16), jax.ShapeDtypeStruct((batch, 4, VT_ROWS, lseg), BF16)],
        compiler_params=_cparams(("parallel",)),
        name="mla_kv",
    )(ckvn, krp, lw["w_uk"], lw["w_uv"], lw["gk"], consts["seg512"], consts["cnt512"])


def _softmax_pv(qs, k_refs, vt_refs, key_chunk, sub_rows, n_ahead, k_lane=0, v_head=0):
    where = [(i, lo) for i, r in enumerate(k_refs) for lo in range(0, r.shape[1], key_chunk)]
    nch = len(where)
    nq = len(qs)
    sub = min(sub_rows, key_chunk)
    nsub = key_chunk // sub

    def scores(c, u):
        seg, lo = where[c]
        ks = k_refs[seg][0, lo + u * sub:lo + (u + 1) * sub, k_lane:k_lane + LANE]
        return [_nt(ks, q) for q in qs]

    def chunk_max(s_chunk, j):
        mc = functools.reduce(jnp.maximum, [s_chunk[u][j] for u in range(nsub)])
        return jnp.max(mc, axis=0, keepdims=True)

    s_buf = {c: [scores(c, u) for u in range(nsub)] for c in range(min(n_ahead, nch))}
    m = [None] * nq
    acc = [None] * nq
    m_new = [chunk_max(s_buf[0], j) for j in range(nq)]
    for c in range(nch):
        s_cur = s_buf.pop(c)
        ahead = c + n_ahead
        if ahead < nch:
            s_buf[ahead] = []
        pv = [None] * nq
        for u in range(nsub):
            if ahead < nch:
                s_buf[ahead].append(scores(ahead, u))
            seg, lo = where[c]
            vs = vt_refs[seg][0, v_head, :, lo + u * sub:lo + (u + 1) * sub]
            for j in range(nq):
                part = jnp.dot(vs, jnp.exp2(s_cur[u][j] - m_new[j]).astype(BF16), preferred_element_type=F32)
                pv[j] = part if pv[j] is None else pv[j] + part
        for j in range(nq):
            acc[j] = pv[j] if c == 0 else acc[j] * jnp.exp2(m[j] - m_new[j]) + pv[j]
            m[j] = m_new[j]
        if c + 1 < nch:
            m_new = [jnp.maximum(m[j], chunk_max(s_buf[c + 1], j)) for j in range(nq)]
    return acc


def _mla_attn_kernel(key_chunk, nseg, heads, q_ref, *refs):
    k_refs, vt_refs, o_ref = refs[:nseg], refs[nseg:2 * nseg], refs[2 * nseg]
    for hh in range(heads):
        q = q_ref[:, LANE * hh:LANE * (hh + 1)]
        (acc,) = _softmax_pv([q], k_refs, vt_refs, key_chunk, MLA_SUB, MLA_AHEAD, LANE * hh, hh)
        o_ref[0, MLA_V * hh:MLA_V * (hh + 1)] = acc[0:MLA_V] / acc[MLA_V:MLA_V + 1]


def _key_chunk(lk, rows=512):
    return rows if lk % rows == 0 else lk


def _heads_per_step(ks):
    return 4 if sum(k.shape[1] for k in ks) <= 512 else 1


def _vt_kernel(v_ref, o_ref):
    _store_vt(o_ref, v_ref[0].astype(F32).T)


def _vt_with_ones(v3):
    b, lk, width = v3.shape
    tm = 1536 if lk % 1536 == 0 else lk
    return pl.pallas_call(
        _vt_kernel,
        grid=(b, lk // tm),
        in_specs=[pl.BlockSpec((1, tm, width), lambda i, j: (i, j, 0))],
        out_specs=pl.BlockSpec((1, 4, VT_ROWS, tm), lambda i, j: (i, 0, 0, j)),
        out_shape=jax.ShapeDtypeStruct((b, 4, VT_ROWS, lk), BF16),
        compiler_params=_cparams(("parallel", "parallel")),
        name="vt_ones",
    )(v3)


def _mla_attn(q, ks, vts, batch, seq_len):
    tq = 256
    nq = seq_len // tq
    chunk = _key_chunk(min(k.shape[1] for k in ks), MLA_CHUNK)
    hp = _heads_per_step(ks)
    k_specs = [pl.BlockSpec((1, k.shape[1], LANE * hp), lambda b, h, i: (b, 0, h)) for k in ks]
    v_specs = [pl.BlockSpec((1, hp, VT_ROWS, v.shape[3]), lambda b, h, i: (b, h, 0, 0)) for v in vts]
    return pl.pallas_call(
        functools.partial(_mla_attn_kernel, chunk, len(ks), hp),
        grid=(batch, MLA_HEADS // hp, nq),
        in_specs=[pl.BlockSpec((tq, LANE * hp), lambda b, h, i: (b * nq + i, h))] + k_specs + v_specs,
        out_specs=pl.BlockSpec((1, MLA_V * hp, tq), lambda b, h, i: (b, h, i)),
        out_shape=jax.ShapeDtypeStruct((batch, BRANCH, seq_len), F32),
        compiler_params=_cparams(("parallel", "parallel", "arbitrary")),
        name="mla_attn",
    )(q, *ks, *vts)


def _seg_const(width, seg):
    sid = np.arange(width) // seg
    return jnp.asarray((sid[:, None] == sid[None, :]).astype(np.float32), BF16)


def _diff_prep_kernel(rope, dq_ref, dk_ref, dv_ref, gq_ref, gk_ref, seg_ref, *rest):
    if rope:
        cos_ref, sin_ref, q_ref, k_ref, kf_ref, vt_ref = rest
    else:
        q_ref, k_ref, kf_ref, vt_ref = rest
    seg = seg_ref[...]
    _store_vt(vt_ref, dv_ref[...].T)

    def norm(x, g):
        ss = _segsum(x * x, seg) * (1.0 / DIFF_HD)
        return x * lax.rsqrt(ss + EPS) * g

    q = norm(dq_ref[...], gq_ref[...])
    k = norm(dk_ref[...], gk_ref[...])
    kf_ref[...] = k
    if rope:
        cos, sin = cos_ref[...], sin_ref[...]
        q = q * cos + _swap8(q) * sin
        k = k * cos + _swap8(k) * sin
    q_ref[...] = (q * (DIFF_SCALE * LOG2E)).astype(BF16)
    k_ref[...] = k.astype(BF16)


def _diff_prep(p, lw, consts, seq_len, rope_tabs):
    n = p.shape[0]
    tm = min(512, seq_len)
    rope = rope_tabs is not None
    full = lambda shape: pl.BlockSpec(shape, lambda i: (0,) * len(shape))
    in_specs = [pl.BlockSpec((tm, 256), lambda i: (i, P_DQ // 256)),
                pl.BlockSpec((tm, 256), lambda i: (i, P_DK // 256)),
                pl.BlockSpec((tm, 256), lambda i: (i, P_DV // 256)),
                full((1, 256)), full((1, 256)), full((256, 256))]
    args = [p, p, p, lw["dgq"], lw["dgk"], consts["seg32"]]
    tps = seq_len // tm
    if rope:
        in_specs += [pl.BlockSpec((tm, 256), lambda i: (i % tps, 0))] * 2
        args += [rope_tabs["cos_diff"], rope_tabs["sin_diff"]]
    blk = pl.BlockSpec((tm, 256), lambda i: (i, 0))
    return pl.pallas_call(
        functools.partial(_diff_prep_kernel, rope),
        grid=(n // tm,),
        in_specs=in_specs,
        out_specs=[blk, blk, blk, pl.BlockSpec((1, 4, VT_ROWS, tm), lambda i: (i // tps, 0, 0, i % tps))],
        out_shape=[jax.ShapeDtypeStruct((n, 256), BF16), jax.ShapeDtypeStruct((n, 256), BF16),
                   jax.ShapeDtypeStruct((n, 256), F32),
                   jax.ShapeDtypeStruct((n // seq_len, 4, VT_ROWS, seq_len), BF16)],
        compiler_params=_cparams(("parallel",)),
        name="diff_prep",
    )(*args)


def _diff_attn_kernel(lam_init, key_chunk, nseg, heads, q_ref, *refs):
    k_refs, vt_refs = refs[:nseg], refs[nseg:2 * nseg]
    lp_ref, g_ref, o_ref = refs[2 * nseg:]
    lp = lp_ref[...]
    lam = (jnp.exp(jnp.sum(lp[0:1] * lp[1:2], axis=1, keepdims=True))
           - jnp.exp(jnp.sum(lp[2:3] * lp[3:4], axis=1, keepdims=True)) + lam_init)
    for hh in range(heads):
        blk = 0 if heads == 1 else LANE * (hh // 2)
        base = (pl.program_id(1) % 2) * 64 if heads == 1 else (hh % 2) * 64
        q = q_ref[:, blk:blk + LANE]
        lane = lax.broadcasted_iota(jnp.int32, q.shape, 1)
        zero = jnp.zeros_like(q)

        def map_query(j):
            lo = base + 32 * j
            return jnp.where((lane >= lo) & (lane < lo + 32), q, zero)

        acc0, acc1 = _softmax_pv([map_query(0), map_query(1)], k_refs, vt_refs, key_chunk, DIFF_SUB, DIFF_AHEAD,
                                 blk, hh)
        o = acc0[0:64] / acc0[64:65] - lam * (acc1[0:64] / acc1[64:65])
        ms = jnp.mean(o * o, axis=0, keepdims=True)
        o_ref[0, 64 * hh:64 * (hh + 1)] = o * lax.rsqrt(ms + EPS) * g_ref[...] * (1.0 - lam_init)


def _diff_attn(q, ks, vts, lp, g_col, lam_init, batch, seq_len):
    tq = 256
    nq = seq_len // tq
    chunk = _key_chunk(min(k.shape[1] for k in ks), DIFF_CHUNK)
    hp = _heads_per_step(ks)
    lanes = LANE if hp == 1 else BRANCH
    k_specs = [pl.BlockSpec((1, k.shape[1], lanes), lambda b, h, i: (b, 0, h // 2)) for k in ks]
    v_specs = [pl.BlockSpec((1, hp, VT_ROWS, v.shape[3]), lambda b, h, i: (b, h, 0, 0)) for v in vts]
    return pl.pallas_call(
        functools.partial(_diff_attn_kernel, lam_init, chunk, len(ks), hp),
        grid=(batch, DIFF_HEADS // hp, nq),
        in_specs=[pl.BlockSpec((tq, lanes), lambda b, h, i: (b * nq + i, h // 2))] + k_specs + v_specs
        + [pl.BlockSpec((4, DIFF_HD), lambda b, h, i: (0, 0)), pl.BlockSpec((64, 1), lambda b, h, i: (0, 0))],
        out_specs=pl.BlockSpec((1, 64 * hp, tq), lambda b, h, i: (b, h, i)),
        out_shape=jax.ShapeDtypeStruct((batch, BRANCH, seq_len), F32),
        compiler_params=_cparams(("parallel", "parallel", "arbitrary")),
        name="diff_attn",
    )(q, *ks, *vts, lp, g_col)


def _hgrn_consts():
    c = HGRN_CHUNK
    t = np.arange(c)
    low = (t[None, :] <= t[:, None]).astype(np.float32)
    blocks = []
    for j in range(HGRN_MM_LEVELS):
        m = 1 << j
        rho = (t // (2 * m)) * (2 * m) + m - 1
        sign = np.where((t // m) % 2 == 1, 1.0, -1.0)[:, None]
        blocks.append(sign * (low - (t[None, :] <= rho[:, None]).astype(np.float32)))
    blocks.append(low)
    fwd = np.concatenate(blocks, axis=0)
    bwd = np.concatenate([b[::-1, ::-1] for b in blocks], axis=0)
    right = np.stack([(t // (1 << j)) % 2 for j in range(HGRN_LEVELS)]).astype(np.float32)
    right = np.stack([right, right[:, ::-1]])
    right = np.broadcast_to(right[..., None], right.shape + (BRANCH,))
    return jnp.asarray(np.stack([fwd, bwd]), BF16), jnp.asarray(right, F32)


def _hgrn_kernel(nc, nb, qf_ref, zf_ref, vf_ref, qb_ref, zb_ref, vb_ref, lb_ref, dd_ref, rm_ref, s0_ref,
                 of_ref, ob_ref, sout_ref, st_ref):
    c = HGRN_CHUNK
    ci = pl.program_id(1)
    chains = [(bi, d) for bi in range(nb) for d in (0, 1)]
    ids = range(len(chains))

    @pl.when(ci == 0)
    def _():
        st_ref[...] = s0_ref[...]

    lane = lax.broadcasted_iota(jnp.int32, (1, BRANCH), 1)
    head_masks = [(lane >= HGRN_DK * h) & (lane < HGRN_DK * (h + 1)) for h in range(HGRN_HEADS)]
    t_idx = lax.broadcasted_iota(jnp.int32, (c, HGRN_HEADS * c), 0)
    s_idx = lax.broadcasted_iota(jnp.int32, (c, HGRN_HEADS * c), 1) & (c - 1)
    pair_xor = t_idx ^ s_idx

    def stack_heads(x):
        xb = x.astype(BF16)
        zero = jnp.zeros_like(xb)
        return jnp.concatenate([jnp.where(hm, xb, zero) for hm in head_masks], axis=0)

    q_refs, z_refs, v_refs = (qf_ref, qb_ref), (zf_ref, zb_ref), (vf_ref, vb_ref)
    q = [q_refs[d][bi] for bi, d in chains]
    v = [v_refs[d][bi] for bi, d in chains]
    z = [z_refs[d][bi] for bi, d in chains]
    lb = [lb_ref[d] for _, d in chains]
    g = [jnp.log(lb[i] + (1.0 - lb[i]) * jax.nn.sigmoid(z[i])) for i in ids]
    kk = [(1.0 - lb[i]) * jax.nn.sigmoid(-z[i]) for i in ids]
    sums = []
    for i in ids:
        gh, gl = _split2(g[i])
        dd = dd_ref[chains[i][1]]
        sums.append(jnp.dot(dd, gh, preferred_element_type=F32) + jnp.dot(dd, gl, preferred_element_type=F32))
    b = [sums[i][HGRN_MM_LEVELS * c:] for i in ids]
    b_tot = [b[i][c - 1:c] if chains[i][1] == 0 else b[i][0:1] for i in ids]

    def neg_abs_decay(i, j):
        if j < HGRN_MM_LEVELS:
            return sums[i][j * c:(j + 1) * c]
        m = 1 << j
        off = m - 1 if chains[i][1] == 0 else m
        ref = jnp.concatenate([jnp.broadcast_to(b[i][g0 + off:g0 + off + 1], (2 * m, BRANCH))
                               for g0 in range(0, c, 2 * m)], axis=0)
        return -jnp.abs(b[i] - ref)

    a = [None] * len(chains)
    for j in reversed(range(HGRN_LEVELS)):
        same_group = pair_xor < (2 << j)
        for i in ids:
            e = jnp.exp(neg_abs_decay(i, j))
            eq = e * rm_ref[chains[i][1], j]
            qt = q[i] * eq
            kt = kk[i] * (e - eq)
            lvl = _nt(qt, stack_heads(kt))
            a[i] = lvl if a[i] is None else jnp.where(same_group, lvl, a[i])
    diagonal = pair_xor == 0
    for i in ids:
        a[i] = jnp.where(diagonal, _nt(q[i], stack_heads(kk[i])), a[i])

    outs = (of_ref, ob_ref)
    for i in ids:
        bi, d = chains[i]
        o = jnp.dot(a[i].astype(BF16), stack_heads(v[i]), preferred_element_type=F32)
        outs[d][bi] = o + _nt(q[i] * jnp.exp(b[i]), st_ref[bi, d])

    r2 = lax.broadcasted_iota(jnp.int32, (BRANCH, BRANCH), 0) // HGRN_DK
    c2 = lax.broadcasted_iota(jnp.int32, (BRANCH, BRANCH), 1) // HGRN_DK
    for i in ids:
        bi, d = chains[i]
        kd = kk[i] * jnp.exp(b_tot[i] - b[i])
        upd = lax.dot_general(v[i].astype(BF16), kd.astype(BF16), (((0,), (0,)), ((), ())),
                              preferred_element_type=F32)
        st_new = st_ref[bi, d] * jnp.exp(b_tot[i]) + jnp.where(r2 == c2, upd, 0.0)
        st_ref[bi, d] = st_new

        @pl.when(ci == nc - 1)
        def _(bi=bi, d=d, st_new=st_new):
            sout_ref[bi, d] = st_new


def _hgrn(p, lb_l, dd, rm, st0, batch, seq_len):
    n = p.shape[0]
    c = HGRN_CHUNK
    nc = seq_len // c
    nb = HGRN_ROWS if batch % HGRN_ROWS == 0 else 1
    p3 = p.reshape(batch, seq_len, P_COLS)
    fwd = lambda col: pl.BlockSpec((nb, c, 256), lambda b, i: (b, i, col))
    bwd = lambda col: pl.BlockSpec((nb, c, 256), lambda b, i: (b, nc - 1 - i, col))
    whole = lambda shape: pl.BlockSpec(shape, lambda b, i: (0,) * len(shape))
    state = pl.BlockSpec((nb, 2, 256, 256), lambda b, i: (b, 0, 0, 0))
    o_f, o_b, st = pl.pallas_call(
        functools.partial(_hgrn_kernel, nc, nb),
        grid=(batch // nb, nc),
        in_specs=[fwd(P_HQ // 256), fwd(P_HZF // 256), fwd(P_HI // 256),
                  bwd(P_HQ // 256), bwd(P_HZB // 256), bwd(P_HI // 256),
                  whole((2, 1, 256)), whole((2, (HGRN_MM_LEVELS + 1) * c, c)),
                  whole((2, HGRN_LEVELS, c, 256)), state],
        out_specs=[pl.BlockSpec((nb, c, 256), lambda b, i: (b, i, 0)),
                   pl.BlockSpec((nb, c, 256), lambda b, i: (b, nc - 1 - i, 0)), state],
        out_shape=[jax.ShapeDtypeStruct((batch, seq_len, 256), F32),
                   jax.ShapeDtypeStruct((batch, seq_len, 256), F32),
                   jax.ShapeDtypeStruct((batch, 2, 256, 256), F32)],
        scratch_shapes=[pltpu.VMEM((nb, 2, 256, 256), F32)],
        compiler_params=_cparams(("parallel", "arbitrary")),
        name="hgrn",
    )(p3, p3, p3, p3, p3, p3, lb_l, dd, rm, st0)
    return o_f.reshape(n, 256), o_b.reshape(n, 256), st


def _hy_conv3_kernel(tiles_per_seq, above_ref, cur_ref, below_ref, w_ref, b_ref, v_ref, x1_ref, x2_ref):
    i = pl.program_id(0)
    cur = cur_ref[...]
    tm = cur.shape[0]
    first = (i % tiles_per_seq) == 0
    last = (i % tiles_per_seq) == tiles_per_seq - 1
    above = jnp.where(first, 0.0, above_ref[7:8, :])
    below = jnp.where(last, 0.0, below_ref[0:1, :])
    row = lax.broadcasted_iota(jnp.int32, (tm, 1), 0)
    prev = jnp.where(row == 0, above, pltpu.roll(cur, 1, 0))
    nxt = jnp.where(row == tm - 1, below, pltpu.roll(cur, tm - 1, 0))
    w = w_ref[...]
    u = prev * w[0:1] + cur * w[1:2] + nxt * w[2:3] + b_ref[...]
    v_ref[...] = u[:, 0:256]
    x1_ref[...] = u[:, 256:512]
    x2_ref[...] = u[:, 512:768]


def _hy_conv3(p, w, b, seq_len):
    n = p.shape[0]
    tm = min(512, seq_len)
    nt = n // tm
    g = tm // 8
    col = P_HU // 768
    oblk = pl.BlockSpec((tm, 256), lambda i: (i, 0))
    return pl.pallas_call(
        functools.partial(_hy_conv3_kernel, seq_len // tm),
        grid=(nt,),
        in_specs=[pl.BlockSpec((8, 768), lambda i: (jnp.maximum(i * g - 1, 0), col)),
                  pl.BlockSpec((tm, 768), lambda i: (i, col)),
                  pl.BlockSpec((8, 768), lambda i: (jnp.minimum((i + 1) * g, nt * g - 1), col)),
                  pl.BlockSpec((3, 768), lambda i: (0, 0)), pl.BlockSpec((1, 768), lambda i: (0, 0))],
        out_specs=[oblk, oblk, oblk],
        out_shape=[jax.ShapeDtypeStruct((n, 256), F32)] * 3,
        compiler_params=_cparams(("parallel",)),
        name="hy_conv3",
    )(p, p, p, w, b)


def _hy_filter_kernel(feat_ref, w1_ref, b1_ref, w2_ref, b2_ref, w3_ref, fr_ref, win_ref, o_ref):
    fr = fr_ref[...]
    h = jnp.sin(fr[0:1] * (_dot3(feat_ref[...], w1_ref[...]) + b1_ref[...]))
    h = jnp.sin(fr[1:2] * (_dot3(h, w2_ref[...]) + b2_ref[...]))
    o_ref[...] = _dot3(h, w3_ref[0]) * _tile_lanes(win_ref[...], HY_ORDER)


def _hy_filter(feats2, window2, w1p, b1, w2, b2, w3d, freq):
    l2 = feats2.shape[0]
    ln = l2 // 2
    tm = min(512, ln)
    full = lambda shape: pl.BlockSpec(shape, lambda i: (0,) * len(shape))
    return pl.pallas_call(
        _hy_filter_kernel,
        grid=(l2 // tm,),
        in_specs=[pl.BlockSpec((tm, LANE), lambda i: (i, 0)),
                  full((LANE, HY_FH)), full((1, HY_FH)), full((HY_FH, HY_FH)), full((1, HY_FH)),
                  pl.BlockSpec((1, HY_FH, HY_ORDER * HY_CH), lambda i: (i // (ln // tm), 0, 0)), full((2, HY_FH)),
                  pl.BlockSpec((tm, HY_CH), lambda i: (i, 0))],
        out_specs=pl.BlockSpec((tm, HY_ORDER * HY_CH), lambda i: (i, 0)),
        out_shape=jax.ShapeDtypeStruct((l2, HY_ORDER * HY_CH), F32),
        compiler_params=_cparams(("parallel",)),
        name="hy_filter",
    )(feats2, w1p, b1, w2, b2, w3d, freq, window2)


def _fft_blocking(nb, n1, n2, ch):
    per_batch = n1 * n2 * ch * 4
    if per_batch <= FFT_BLOCK_BYTES:
        bb = max(1, min(nb, FFT_BLOCK_BYTES // per_batch))
        while nb % bb:
            bb -= 1
        return bb, n2
    rt = n2
    while n1 * rt * ch * 4 > FFT_BLOCK_BYTES and rt > 8:
        rt //= 2
    return 1, rt


def _fft_a_kernel(f_ref, x_ref, o_ref):
    f = f_ref[...]
    for b in range(x_ref.shape[0]):
        x = x_ref[b].astype(BF16)
        o_ref[b] = jnp.einsum("kn,nrc->krc", f, x, preferred_element_type=F32).astype(o_ref.dtype)


def _fft_a(fa, x4, out_dtype):
    nb, n1, n2, ch = x4.shape
    r = fa.shape[0]
    bb, rt = _fft_blocking(nb, n1, n2, ch)
    return pl.pallas_call(
        _fft_a_kernel,
        grid=(nb // bb, n2 // rt),
        in_specs=[pl.BlockSpec((r, n1), lambda b, i: (0, 0)),
                  pl.BlockSpec((bb, n1, rt, ch), lambda b, i: (b, 0, i, 0))],
        out_specs=pl.BlockSpec((bb, r, rt, ch), lambda b, i: (b, 0, i, 0)),
        out_shape=jax.ShapeDtypeStruct((nb, r, n2, ch), out_dtype),
        compiler_params=_cparams(("parallel", "parallel")),
        name="fft_a",
    )(fa, x4)


def _fft_b_kernel(with_inverse, mf_ref, *rest):
    if with_inverse:
        mi_ref, a_ref, h_ref, o_ref = rest
    else:
        a_ref, o_ref = rest
    half = FFT_N2
    for kk in range(a_ref.shape[2]):
        for b in range(a_ref.shape[0]):
            a = jnp.concatenate([a_ref[b, 0, kk], a_ref[b, 1, kk]], axis=0)
            x = jnp.dot(mf_ref[kk], a.astype(BF16), preferred_element_type=F32)
            if with_inverse:
                xr, xi = x[:half], x[half:]
                hr, hi = h_ref[0, kk], h_ref[1, kk]
                y = jnp.concatenate([xr * hr - xi * hi, xr * hi + xi * hr], axis=0)
                x = jnp.dot(mi_ref[kk], y.astype(BF16), preferred_element_type=F32)
            o_ref[b, 0, kk] = x[:half].astype(o_ref.dtype)
            o_ref[b, 1, kk] = x[half:].astype(o_ref.dtype)


def _fft_b(mf, mi, a5, spec, order):
    nb, _, k1n, n2, ch = a5.shape
    ks = 3 if (k1n % 3 == 0 and nb * ch <= 1024) else 1
    mat = pl.BlockSpec((ks, 2 * n2, 2 * n2), lambda k: (k, 0, 0))
    blk = pl.BlockSpec((nb, 2, ks, n2, ch), lambda k: (0, 0, k, 0, 0))
    if spec is None:
        in_specs, args = [mat, blk], [mf, a5]
    else:
        in_specs = [mat, mat, blk, pl.BlockSpec((2, ks, n2, ch), lambda k: (0, k, 0, order))]
        args = [mf, mi, a5, spec]
    return pl.pallas_call(
        functools.partial(_fft_b_kernel, spec is not None),
        grid=(k1n // ks,),
        in_specs=in_specs,
        out_specs=blk,
        out_shape=jax.ShapeDtypeStruct(a5.shape, F32 if spec is None else BF16),
        compiler_params=_cparams(("parallel",)),
        name="fft_b",
    )(*args)


def _fft_a_inv_kernel(g_ref, p_ref, x_ref, z_ref, bias_ref, o_ref):
    g = g_ref[...]
    for b in range(p_ref.shape[0]):
        conv = jnp.einsum("nk,krc->nrc", g, p_ref[b], preferred_element_type=F32)
        o_ref[b] = x_ref[b] * (conv + z_ref[b] * bias_ref[...])


def _fft_a_inv(g, p4, xg4, z4, bias):
    nb, n1, n2, ch = z4.shape
    r = g.shape[1]
    bb, rt = _fft_blocking(nb, n1, n2, ch)
    blk = pl.BlockSpec((bb, n1, rt, ch), lambda b, i: (b, 0, i, 0))
    return pl.pallas_call(
        _fft_a_inv_kernel,
        grid=(nb // bb, n2 // rt),
        in_specs=[pl.BlockSpec((n1, r), lambda b, i: (0, 0)),
                  pl.BlockSpec((bb, r, rt, ch), lambda b, i: (b, 0, i, 0)), blk, blk,
                  pl.BlockSpec((1, 1, ch), lambda b, i: (0, 0, 0))],
        out_specs=blk,
        out_shape=jax.ShapeDtypeStruct(z4.shape, F32),
        compiler_params=_cparams(("parallel", "parallel")),
        name="fft_a_inv",
    )(g, p4, xg4, z4, bias.reshape(1, 1, ch))


def _fft_tables(ln):
    n = 2 * ln
    n1t = n // FFT_N2
    k1n = n1t // 2 + 1
    kk = np.arange(k1n)

    def stage_a(n1_in):
        ang = 2.0 * np.pi * ((kk[:, None] * np.arange(n1_in)[None, :]) % n1t) / n1t
        return jnp.asarray(np.concatenate([np.cos(ang), -np.sin(ang)], axis=0), BF16)

    n1o = n1t // 2
    ang = 2.0 * np.pi * ((np.arange(n1o)[:, None] * kk[None, :]) % n1t) / n1t
    edge = (kk == 0) | (kk == n1t // 2)
    ck = np.where(edge, 1.0, 2.0) / n
    g = jnp.asarray(np.concatenate([ck * np.cos(ang), -ck * np.where(edge, 0.0, np.sin(ang))], axis=1), BF16)

    k1 = jnp.arange(k1n, dtype=jnp.int32)[:, None, None]
    k2 = jnp.arange(FFT_N2, dtype=jnp.int32)[None, :, None]
    n2 = jnp.arange(FFT_N2, dtype=jnp.int32)[None, None, :]
    th = (2.0 * math.pi / n) * ((n2 * (k1 + n1t * k2)) % n).astype(F32)
    c, s = jnp.cos(th), jnp.sin(th)
    mf = jnp.concatenate([jnp.concatenate([c, s], axis=2), jnp.concatenate([-s, c], axis=2)], axis=1)
    return dict(fa_half=stage_a(n1o), fa_full=stage_a(n1t), g=g, mf=mf.astype(BF16),
                mi=mf.transpose(0, 2, 1).astype(BF16), k1n=k1n, n1o=n1o, n1t=n1t)


def _hy_static(ln):
    t = jnp.linspace(0.0, 1.0, ln, dtype=F32)[:, None]
    w = 2.0 * math.pi * jnp.arange(ln, dtype=F32) / ln
    f = jnp.linspace(1e-4, HY_BANDS - 1, HY_BANDS, dtype=F32)
    ang = w[:, None] * f[None, :]
    feats = jnp.concatenate([t, jnp.cos(ang), -jnp.sin(ang)], axis=-1)
    feats = jnp.pad(feats, ((0, 0), (0, LANE - HY_EMB)))
    min_decay = math.log(HY_DECAY_TARGET) / HY_SLOW_DECAY
    max_decay = math.log(HY_DECAY_TARGET) / HY_FAST_DECAY
    deltas = jnp.linspace(min_decay, max_decay, HY_CH, dtype=F32)
    window = jnp.exp(-t * jnp.abs(deltas))
    feats = jnp.concatenate([feats, feats[::-1]], axis=0)
    window = jnp.concatenate([window, window[::-1]], axis=0)
    return feats, window


def _outproj_kernel(x_ref, mod_ref, oa_ref, ob_ref, of_ref, obk_ref, od_ref, gate_ref, hg_ref, seg_ref, w_ref,
                    y_ref):
    gt = gate_ref[...]
    sg = gt * jax.nn.sigmoid(gt)
    oc = of_ref[...] + obk_ref[...]
    ss = _segsum(oc * oc, seg_ref[...]) * (1.0 / HGRN_DK)
    oc = oc * lax.rsqrt(ss + EPS) * hg_ref[...]
    acc = _bdot(oa_ref[0].T * sg[:, 0:256], w_ref[0:256, :])
    acc += _bdot(ob_ref[0].T * sg[:, 256:512], w_ref[256:512, :])
    acc += _bdot(oc * sg[:, 512:768], w_ref[512:768, :])
    acc += _bdot(od_ref[...] * sg[:, 768:1024], w_ref[768:1024, :])
    y_ref[...] = x_ref[...] + mod_ref[0, 2:3, :] * acc


def _outproj(x2, mod, ot_a, ot_b, o_f, o_b, out_d, p, hg, seg64, w_out, seq_len):
    n = x2.shape[0]
    tm = min(512, seq_len)
    per_batch = mod.shape[0] > 1
    tps = seq_len // tm
    mod_idx = (lambda i: (i // tps, 0, 0)) if per_batch else (lambda i: (0, 0, 0))
    b256 = pl.BlockSpec((tm, 256), lambda i: (i, 0))
    bt = pl.BlockSpec((1, 256, tm), lambda i: (i // tps, 0, i % tps))
    return pl.pallas_call(
        _outproj_kernel,
        grid=(n // tm,),
        in_specs=[pl.BlockSpec((tm, D_MODEL), lambda i: (i, 0)),
                  pl.BlockSpec((1, 3, D_MODEL), mod_idx),
                  bt, bt, b256, b256, b256,
                  pl.BlockSpec((tm, 1024), lambda i: (i, P_GATE // 1024)),
                  pl.BlockSpec((1, 256), lambda i: (0, 0)),
                  pl.BlockSpec((256, 256), lambda i: (0, 0)),
                  pl.BlockSpec((D_MODEL, D_MODEL), lambda i: (0, 0))],
        out_specs=pl.BlockSpec((tm, D_MODEL), lambda i: (i, 0)),
        out_shape=jax.ShapeDtypeStruct((n, D_MODEL), F32),
        compiler_params=_cparams(("parallel",)),
        name="outproj",
    )(x2, mod, ot_a, ot_b, o_f, o_b, out_d, p, hg, seg64, w_out)


def _layer(x2, mod, lw, consts, batch, seq_len, ctx, rope_tabs, hy):
    n = batch * seq_len
    p = _inproj(x2, mod, lw["norm_g"], lw["w_in"], seq_len)

    q_a, ckvn, krp = _mla_q(p, lw, consts, seq_len, rope_tabs)
    ckv3 = ckvn.reshape(batch, seq_len, MLA_KV_LORA)
    krp3 = krp.reshape(batch, seq_len, LANE)
    k_a, vt_a = _mla_kv(ckvn, krp, lw, consts, batch)
    ks_a, vts_a = [k_a.reshape(batch, seq_len, 512)], [vt_a]
    if ctx is not None:
        lc = ctx[0].shape[1]
        k_c, vt_c = _mla_kv(ctx[0].reshape(batch * lc, MLA_KV_LORA), ctx[1].reshape(batch * lc, LANE), lw, consts,
                            batch)
        ks_a.append(k_c.reshape(batch, lc, 512))
        vts_a.append(vt_c)
    ot_a = _mla_attn(q_a, ks_a, vts_a, batch, seq_len)

    q_b, k_b, kd, vt_b = _diff_prep(p, lw, consts, seq_len, rope_tabs)
    ks_b, vts_b = [k_b.reshape(batch, seq_len, BRANCH)], [vt_b]
    if ctx is not None:
        ks_b.append(ctx[2])
        vts_b.append(ctx[3])
    ot_b = _diff_attn(q_b, ks_b, vts_b, lw["diff_lambda"], lw["subln_col"], lw["lam_init"], batch, seq_len)

    if ctx is not None:
        s0 = ctx[4]
    else:
        s0 = jnp.zeros((batch, 2, HGRN_HEADS, HGRN_DK, HGRN_DK), F32)
    eye = jnp.eye(HGRN_HEADS, dtype=F32)
    st0 = jnp.einsum("bdhke,hg->bdhegk", s0, eye).reshape(batch, 2, BRANCH, BRANCH)
    o_f, o_b, st_out = _hgrn(p, lw["hgrn_lb"], consts["hgrn_dd"], consts["hgrn_right"], st0, batch, seq_len)
    st5 = st_out.reshape(batch, 2, HGRN_HEADS, HGRN_DK, HGRN_HEADS, HGRN_DK)
    states = jnp.stack([st5[:, :, h, :, h, :] for h in range(HGRN_HEADS)], axis=2).swapaxes(-1, -2)

    v_d, x1, x2g = _hy_conv3(p, lw["hy_conv_w"], lw["hy_conv_b"], seq_len)
    taps = _hy_filter(hy["feats"], hy["window"], lw["hy_w1"], lw["hy_b1"], lw["hy_w2"], lw["hy_b2"], lw["hy_w3"],
                      lw["hy_freq"])
    k1n, n1o, n1t = hy["k1n"], hy["n1o"], hy["n1t"]
    ta = _fft_a(hy["fa_full"], taps.reshape(1, n1t, FFT_N2, HY_ORDER * HY_CH), BF16)
    spec = _fft_b(hy["mf"], None, ta.reshape(1, 2, k1n, FFT_N2, HY_ORDER * HY_CH), None, 0)[0]
    z4 = v_d.reshape(batch, n1o, FFT_N2, HY_CH)
    for o, xg in enumerate((x1, x2g)):
        a = _fft_a(hy["fa_half"], z4, BF16).reshape(batch, 2, k1n, FFT_N2, HY_CH)
        pk = _fft_b(hy["mf"], hy["mi"], a, spec, o).reshape(batch, 2 * k1n, FFT_N2, HY_CH)
        z4 = _fft_a_inv(hy["g"], pk, xg.reshape(batch, n1o, FFT_N2, HY_CH), z4, lw["hy_bias"][o:o + 1])
    out_d = z4.reshape(n, HY_CH)

    y = _outproj(x2, mod, ot_a, ot_b, o_f, o_b, out_d, p, lw["hgrn_out_g"], consts["seg64"], lw["w_out"], seq_len)
    new = None
    if ctx is None:
        new = (ckv3, krp3[:, :, KR_OFF:KR_OFF + MLA_ROPE],
               kd.reshape(batch, seq_len, DIFF_HEADS, 2, DIFF_HD),
               p[:, P_DV:P_DV + BRANCH].reshape(batch, seq_len, DIFF_HEADS, 2 * DIFF_HD), states)
    return y, new


def _rope_tables(seq_len):
    half = MLA_ROPE // 2
    inv = ROPE_BASE ** (-jnp.arange(0, half, 2, dtype=F32) / half)
    rows = seq_len // GRID_W
    row = jnp.repeat(jnp.arange(rows, dtype=F32), GRID_W)
    col = (jnp.arange(rows * GRID_W) % GRID_W).astype(F32)
    ar, ac = row[:, None] * inv, col[:, None] * inv
    cos32 = jnp.concatenate([jnp.cos(ar), jnp.cos(ar), jnp.cos(ac), jnp.cos(ac)], axis=-1)
    sin32 = jnp.concatenate([-jnp.sin(ar), jnp.sin(ar), -jnp.sin(ac), jnp.sin(ac)], axis=-1)
    pad = ((0, 0), (KR_OFF, LANE - KR_OFF - MLA_ROPE))
    return dict(cos_mla=jnp.pad(cos32, pad, constant_values=1.0), sin_mla=jnp.pad(sin32, pad),
                cos_diff=jnp.tile(cos32, (1, 2 * DIFF_HEADS)), sin_diff=jnp.tile(sin32, (1, 2 * DIFF_HEADS)))


def _hy_tables(seq_len):
    feats, window = _hy_static(seq_len)
    return dict(feats=feats, window=window, **_fft_tables(seq_len))


def _layer_weights(l, w_in_p, lb, W):
    def head_pad(w, width, per):
        k = w.shape[0]
        w = w.reshape(k, MLA_HEADS, per)[:, :, :width]
        return jnp.pad(w, ((0, 0), (0, 0), (0, LANE - width))).reshape(k, MLA_HEADS * LANE)

    w_ukv = W["mla_w_ukv"][l].reshape(MLA_KV_LORA, MLA_HEADS, MLA_NOPE + MLA_V)
    nope_g, rope_g = W["mla_nope_g"][l], W["mla_rope_g"][l]
    zeros32 = jnp.zeros((MLA_ROPE,), F32)
    zeros64 = jnp.zeros((MLA_NOPE,), F32)
    gq = jnp.tile(jnp.concatenate([nope_g[0], rope_g[0], zeros32]), MLA_HEADS).reshape(1, 512)
    gk = jnp.tile(jnp.concatenate([nope_g[1], zeros64]), MLA_HEADS).reshape(1, 512)
    gkr = jnp.concatenate([zeros64, rope_g[1], zeros32]).reshape(1, LANE)
    return dict(
        norm_g=W["norm_g"][l], w_in=w_in_p[l], w_out=W["w_out"][l].astype(BF16),
        qn_g=W["mla_q_norm_g"][l].reshape(1, -1),
        w_uq=head_pad(W["mla_w_uq"][l], MLA_NOPE + MLA_ROPE, MLA_NOPE + MLA_ROPE).astype(BF16),
        kvn_g=W["mla_kv_norm_g"][l].reshape(1, -1),
        w_uk=jnp.pad(w_ukv[:, :, :MLA_NOPE], ((0, 0), (0, 0), (0, LANE - MLA_NOPE))).reshape(MLA_KV_LORA, 512)
        .astype(BF16),
        w_uv=w_ukv[:, :, MLA_NOPE:].reshape(MLA_KV_LORA, BRANCH).T.astype(BF16),
        gq=gq, gk=gk, gkr=gkr,
        dgq=jnp.tile(W["diff_qk_g"][l, 0], 2 * DIFF_HEADS).reshape(1, BRANCH),
        dgk=jnp.tile(W["diff_qk_g"][l, 1], 2 * DIFF_HEADS).reshape(1, BRANCH),
        diff_lambda=W["diff_lambda"][l], subln_col=W["diff_subln_g"][l].reshape(2 * DIFF_HD, 1),
        lam_init=0.8 - 0.6 * math.exp(-0.3 * l),
        hgrn_lb=lb[:, l].reshape(2, 1, BRANCH),
        hgrn_out_g=jnp.tile(W["hgrn_out_g"][l], HGRN_HEADS).reshape(1, BRANCH),
        hy_conv_w=W["hy_conv_w"][l], hy_conv_b=W["hy_conv_b"][l].reshape(1, -1),
        hy_w1=jnp.pad(W["hy_w1"][l], ((0, LANE - HY_EMB), (0, 0))), hy_b1=W["hy_b1"][l].reshape(1, -1),
        hy_w2=W["hy_w2"][l], hy_b2=W["hy_b2"][l].reshape(1, -1),
        hy_w3=W["hy_w3"][l].reshape(HY_FH, HY_ORDER, 2, HY_CH).transpose(2, 0, 1, 3)
        .reshape(2, HY_FH, HY_ORDER * HY_CH),
        hy_freq=W["hy_sin_freq"][l], hy_bias=W["hy_bias"][l],
    )


def kernel(x_prompt, x_sample, cache_mla_ckv, cache_mla_krope, cache_diff_k, cache_diff_v, state_hgrn, c, c_ctx,
           norm_g, w_mod, b_mod, w_in, w_out, mla_q_norm_g, mla_w_uq, mla_kv_norm_g, mla_w_ukv, mla_nope_g,
           mla_rope_g, diff_qk_g, diff_lambda, diff_subln_g, hgrn_lb_logits, hgrn_out_g, hy_conv_w, hy_conv_b,
           hy_w1, hy_b1, hy_w2, hy_b2, hy_w3, hy_sin_freq, hy_bias):
    W = dict(norm_g=norm_g, w_out=w_out, mla_q_norm_g=mla_q_norm_g, mla_w_uq=mla_w_uq,
             mla_kv_norm_g=mla_kv_norm_g, mla_w_ukv=mla_w_ukv, mla_nope_g=mla_nope_g, mla_rope_g=mla_rope_g,
             diff_qk_g=diff_qk_g, diff_lambda=diff_lambda, diff_subln_g=diff_subln_g, hgrn_out_g=hgrn_out_g,
             hy_conv_w=hy_conv_w, hy_conv_b=hy_conv_b, hy_w1=hy_w1, hy_b1=hy_b1, hy_w2=hy_w2, hy_b2=hy_b2,
             hy_w3=hy_w3, hy_sin_freq=hy_sin_freq, hy_bias=hy_bias)
    bp, lp, _ = x_prompt.shape
    bs, ls, _ = x_sample.shape

    w_in_p = _reorder_in_cols(w_in.astype(BF16))
    cvecs = jnp.concatenate([c_ctx[None, :], c, jnp.zeros((8 - 1 - bs, D_MODEL), F32)], axis=0)
    mods = _mod_all(cvecs, w_mod, b_mod)
    lb = _hgrn_lb(hgrn_lb_logits)
    seg512, cnt512 = _mla_seg()
    hgrn_dd, hgrn_right = _hgrn_consts()
    consts = dict(seg512=seg512, cnt512=cnt512, seg32=_seg_const(BRANCH, DIFF_HD), seg64=_seg_const(BRANCH, HGRN_DK),
                  hgrn_dd=hgrn_dd, hgrn_right=hgrn_right)
    lws = [_layer_weights(l, w_in_p, lb, W) for l in range(DEPTH)]

    hy_p = _hy_tables(lp)
    y = x_prompt.reshape(bp * lp, D_MODEL)
    per_layer = []
    for l in range(DEPTH):
        mod = mods[l, 0:1].reshape(1, 3, D_MODEL)
        y, new = _layer(y, mod, lws[l], consts, bp, lp, None, None, hy_p)
        per_layer.append(new)
    y_prompt = y.reshape(bp, lp, D_MODEL)
    news = [jnp.stack([s[i] for s in per_layer], axis=1) for i in range(5)]

    hy_s = _hy_tables(ls)
    rope_tabs = _rope_tables(ls)
    y = x_sample.reshape(bs * ls, D_MODEL)
    past = cache_mla_ckv.shape[2]
    cache_kr = jnp.pad(cache_mla_krope, ((0, 0), (0, 0), (0, 0), (KR_OFF, LANE - KR_OFF - MLA_ROPE)))
    cache_kb = cache_diff_k.reshape(bs, DEPTH, past, BRANCH).astype(BF16)
    cache_vtb = _vt_with_ones(cache_diff_v.reshape(bs * DEPTH, past, BRANCH).astype(BF16))
    cache_vtb = cache_vtb.reshape(bs, DEPTH, DIFF_HEADS, VT_ROWS, past)
    for l in range(DEPTH):
        mod = mods[l, 1:1 + bs].reshape(bs, 3, D_MODEL)
        ctx = (cache_mla_ckv[:, l], cache_kr[:, l], cache_kb[:, l], cache_vtb[:, l], state_hgrn[:, l])
        y, _ = _layer(y, mod, lws[l], consts, bs, ls, ctx, rope_tabs, hy_s)
    y_sample = y.reshape(bs, ls, D_MODEL)

    return (y_prompt, y_sample, news[0], news[1], news[2], news[3], news[4])
```

```python
import functools
import math

import numpy as np
import jax
import jax.numpy as jnp
from jax import lax
from jax.experimental import pallas as pl
from jax.experimental.pallas import tpu as pltpu

F32 = jnp.float32
BF16 = jnp.bfloat16

D_MODEL = 1024
DEPTH = 4
GRID_W = 64
ROPE_BASE = 10000.0
EPS = 1e-6
BRANCH = 256
MLA_HEADS = 4
MLA_NOPE = 64
MLA_ROPE = 32
MLA_V = 64
MLA_Q_LORA = 256
MLA_KV_LORA = 128
MLA_SCALE = (MLA_NOPE + MLA_ROPE) ** -0.5
DIFF_HEADS = 4
DIFF_HD = 32
DIFF_SCALE = DIFF_HD ** -0.5
HGRN_HEADS = 4
HGRN_DK = 64
HGRN_CHUNK = 128
HGRN_LEVELS = 7
HGRN_ROWS = 2
HGRN_MM_LEVELS = 3
HY_CH = 256
HY_ORDER = 2
HY_EMB = 33
HY_BANDS = 16
HY_FH = 64
HY_DECAY_TARGET = 0.01
HY_FAST_DECAY = 0.3
HY_SLOW_DECAY = 1.5
IN_COLS = 4000

LANE = 128
LOG2E = math.log2(math.e)
VT_ROWS = 80
FFT_N2 = 128
FFT_BLOCK_BYTES = 2 * 1024 * 1024
MLA_AHEAD = 8
MLA_CHUNK = 512
DIFF_AHEAD = 6
DIFF_CHUNK = 256
MLA_SUB = 256
DIFF_SUB = 512
VMEM_LIMIT = 52 * 1024 * 1024

P_CQ, P_CKV, P_KR, P_DQ, P_DK, P_DV = 0, 256, 384, 512, 768, 1024
P_HQ, P_HZF, P_HZB, P_HI, P_HU, P_GATE = 1280, 1536, 1792, 2048, 2304, 3072
P_COLS = 4096
KR_OFF = 64


def _in_col_perm():
    src = np.full((P_COLS,), IN_COLS, np.int32)

    def put(dst, lo, n):
        src[dst:dst + n] = np.arange(lo, lo + n)

    put(P_CQ, 0, 256)
    put(P_CKV, 256, 128)
    put(P_KR + KR_OFF, 384, 32)
    put(P_GATE, 416, 256)
    put(P_DQ, 672, 256)
    put(P_DK, 928, 256)
    put(P_DV, 1184, 256)
    put(P_GATE + 256, 1440, 256)
    put(P_HQ, 1696, 256)
    put(P_HZF, 1952, 256)
    put(P_HZB, 2208, 256)
    put(P_HI, 2464, 256)
    put(P_GATE + 512, 2720, 256)
    put(P_HU, 2976, 768)
    put(P_GATE + 768, 3744, 256)
    return src


def _reorder_in_cols(w):
    src = _in_col_perm()
    pieces, lo = [], 0
    while lo < P_COLS:
        hi = lo + 1
        if src[lo] == IN_COLS:
            while hi < P_COLS and src[hi] == IN_COLS:
                hi += 1
            pieces.append(jnp.zeros(w.shape[:-1] + (hi - lo,), w.dtype))
        else:
            while hi < P_COLS and src[hi] == src[hi - 1] + 1:
                hi += 1
            pieces.append(w[..., int(src[lo]):int(src[lo]) + hi - lo])
        lo = hi
    return jnp.concatenate(pieces, axis=-1)


def _cparams(sem):
    return pltpu.CompilerParams(dimension_semantics=sem, vmem_limit_bytes=VMEM_LIMIT)


def _bdot(a, b):
    return jnp.dot(a.astype(BF16), b.astype(BF16), preferred_element_type=F32)


def _nt(a, b):
    return lax.dot_general(a.astype(BF16), b.astype(BF16), (((1,), (1,)), ((), ())), preferred_element_type=F32)


def _split2(a):
    hi = a.astype(BF16)
    lo = (a - hi.astype(F32)).astype(BF16)
    return hi, lo


def _dot3(a, b):
    ah, al = _split2(a)
    bh, bl = _split2(b)
    d = functools.partial(jnp.dot, preferred_element_type=F32)
    return d(ah, bh) + d(ah, bl) + d(al, bh)


def _segsum(v, seg):
    return jnp.dot(v.astype(BF16), seg, preferred_element_type=F32)


def _rms(x, g):
    return x * lax.rsqrt(jnp.mean(x * x, axis=-1, keepdims=True) + EPS) * g


def _swap8(x):
    w = x.shape[-1]
    lane = lax.broadcasted_iota(jnp.int32, x.shape, x.ndim - 1)
    up = pltpu.roll(x, w - 8, x.ndim - 1)
    dn = pltpu.roll(x, 8, x.ndim - 1)
    return jnp.where((lane & 15) < 8, up, dn)


def _tile_lanes(x, n):
    return x if n == 1 else jnp.concatenate([x] * n, axis=-1)


def _mod_kernel(c_ref, w_ref, b_ref, o_ref):
    c = c_ref[...]
    o_ref[0] = _dot3(c * jax.nn.sigmoid(c), w_ref[0]) + b_ref[0]


def _mod_all(cvecs, w_mod, b_mod):
    nt = 3
    return pl.pallas_call(
        _mod_kernel,
        grid=(DEPTH, nt),
        in_specs=[pl.BlockSpec((8, D_MODEL), lambda l, j: (0, 0)),
                  pl.BlockSpec((1, D_MODEL, D_MODEL), lambda l, j: (l, 0, j)),
                  pl.BlockSpec((1, 1, D_MODEL), lambda l, j: (l, 0, j))],
        out_specs=pl.BlockSpec((1, 8, D_MODEL), lambda l, j: (l, 0, j)),
        out_shape=jax.ShapeDtypeStruct((DEPTH, 8, 3 * D_MODEL), F32),
        compiler_params=_cparams(("arbitrary", "arbitrary")),
        name="mod",
    )(cvecs, w_mod, b_mod.reshape(DEPTH, 1, 3 * D_MODEL))


def _lb_kernel(x_ref, o_ref):
    x = x_ref[...]
    rows = [x[l:l + 1, :] for l in range(DEPTH)]
    m = functools.reduce(jnp.maximum, rows)
    e = [jnp.exp(r - m) for r in rows]
    tot = functools.reduce(lambda a, b: a + b, e)
    acc = jnp.zeros_like(tot)
    o_ref[0:1, :] = acc
    for l in range(1, DEPTH):
        acc = acc + e[l] / tot
        o_ref[l:l + 1, :] = acc


def _hgrn_lb(logits):
    flat = logits.transpose(1, 0, 2).reshape(DEPTH, 2 * BRANCH)
    lb = pl.pallas_call(
        _lb_kernel,
        out_shape=jax.ShapeDtypeStruct(flat.shape, F32),
        name="hgrn_lb",
    )(flat)
    return lb.reshape(DEPTH, 2, BRANCH).transpose(1, 0, 2)


def _inproj_kernel(x_ref, mod_ref, g_ref, w_ref, p_ref):
    h = _rms(x_ref[...], g_ref[...]) * (1.0 + mod_ref[0, 1:2, :]) + mod_ref[0, 0:1, :]
    p_ref[...] = jnp.dot(h.astype(BF16), w_ref[...], preferred_element_type=F32)


def _inproj(x2, mod, norm_g, w_in_p, seq_len):
    n = x2.shape[0]
    tm = min(512, seq_len)
    per_batch = mod.shape[0] > 1
    tiles_per_seq = seq_len // tm
    mod_idx = (lambda i: (i // tiles_per_seq, 0, 0)) if per_batch else (lambda i: (0, 0, 0))
    return pl.pallas_call(
        _inproj_kernel,
        grid=(n // tm,),
        in_specs=[pl.BlockSpec((tm, D_MODEL), lambda i: (i, 0)),
                  pl.BlockSpec((1, 3, D_MODEL), mod_idx),
                  pl.BlockSpec((1, D_MODEL), lambda i: (0, 0)),
                  pl.BlockSpec((D_MODEL, P_COLS), lambda i: (0, 0))],
        out_specs=pl.BlockSpec((tm, P_COLS), lambda i: (i, 0)),
        out_shape=jax.ShapeDtypeStruct((n, P_COLS), F32),
        compiler_params=_cparams(("parallel",)),
        name="inproj",
    )(x2, mod, norm_g.reshape(1, D_MODEL), w_in_p)


def _mla_seg():
    sid = np.zeros((512,), np.int32)
    cnt = np.ones((512,), np.float32)
    for h in range(MLA_HEADS):
        b = 128 * h
        sid[b:b + 64] = 3 * h
        sid[b + 64:b + 96] = 3 * h + 1
        sid[b + 96:b + 128] = 3 * h + 2
        cnt[b:b + 64] = 1.0 / 64
        cnt[b + 64:b + 128] = 1.0 / 32
    seg = (sid[:, None] == sid[None, :]).astype(np.float32)
    return jnp.asarray(seg, BF16), jnp.asarray(cnt.reshape(1, 512))


def _mla_q_kernel(rope, cq_ref, ckv_ref, kr_ref, qng_ref, wuq_ref, kvg_ref, gq_ref, gkr_ref, seg_ref, cnt_ref,
                  *rest):
    if rope:
        cos_ref, sin_ref, q_ref, ckvn_ref, krp_ref = rest
    else:
        q_ref, ckvn_ref, krp_ref = rest
    cqn = _rms(cq_ref[...], qng_ref[...])
    q = _bdot(cqn, wuq_ref[...])
    ss = _segsum(q * q, seg_ref[...]) * cnt_ref[...]
    qn = q * lax.rsqrt(ss + EPS) * gq_ref[...]
    ckvn_ref[...] = _rms(ckv_ref[...], kvg_ref[...])
    kr = kr_ref[...]
    krn = kr * lax.rsqrt(jnp.sum(kr * kr, axis=-1, keepdims=True) * (1.0 / MLA_ROPE) + EPS) * gkr_ref[...]
    if rope:
        cos, sin = cos_ref[...], sin_ref[...]
        qn = qn * _tile_lanes(cos, MLA_HEADS) + _swap8(qn) * _tile_lanes(sin, MLA_HEADS)
        krn = krn * cos + _swap8(krn) * sin
    q_ref[...] = (qn * (MLA_SCALE * LOG2E)).astype(BF16)
    krp_ref[...] = krn


def _mla_q(p, lw, consts, seq_len, rope_tabs):
    n = p.shape[0]
    tm = min(512, seq_len)
    rope = rope_tabs is not None
    full = lambda shape: pl.BlockSpec(shape, lambda i: (0,) * len(shape))
    in_specs = [pl.BlockSpec((tm, 256), lambda i: (i, P_CQ // 256)),
                pl.BlockSpec((tm, 128), lambda i: (i, P_CKV // 128)),
                pl.BlockSpec((tm, 128), lambda i: (i, P_KR // 128)),
                full((1, 256)), full((256, 512)), full((1, 128)), full((1, 512)), full((1, 128)),
                full((512, 512)), full((1, 512))]
    args = [p, p, p, lw["qn_g"], lw["w_uq"], lw["kvn_g"], lw["gq"], lw["gkr"], consts["seg512"], consts["cnt512"]]
    if rope:
        tps = seq_len // tm
        in_specs += [pl.BlockSpec((tm, 128), lambda i: (i % tps, 0))] * 2
        args += [rope_tabs["cos_mla"], rope_tabs["sin_mla"]]
    return pl.pallas_call(
        functools.partial(_mla_q_kernel, rope),
        grid=(n // tm,),
        in_specs=in_specs,
        out_specs=[pl.BlockSpec((tm, 512), lambda i: (i, 0)),
                   pl.BlockSpec((tm, 128), lambda i: (i, 0)),
                   pl.BlockSpec((tm, 128), lambda i: (i, 0))],
        out_shape=[jax.ShapeDtypeStruct((n, 512), BF16),
                   jax.ShapeDtypeStruct((n, 128), F32),
                   jax.ShapeDtypeStruct((n, 128), F32)],
        compiler_params=_cparams(("parallel",)),
        name="mla_q",
    )(*args)


def _store_vt(o_ref, vt):
    tm = vt.shape[1]
    row = lax.broadcasted_iota(jnp.int32, (VT_ROWS - 64, tm), 0)
    extra = jnp.where(row == 0, 1.0, 0.0).astype(BF16)
    for h in range(4):
        o_ref[0, h, 0:64, :] = vt[64 * h:64 * (h + 1)].astype(BF16)
        o_ref[0, h, 64:VT_ROWS, :] = extra


def _mla_kv_kernel(ckvn_ref, krp_ref, wuk_ref, wuv_ref, gk_ref, seg_ref, cnt_ref, k_ref, vt_ref):
    c = ckvn_ref[...].astype(BF16)
    kn = jnp.dot(c, wuk_ref[...], preferred_element_type=F32)
    ss = _segsum(kn * kn, seg_ref[...]) * cnt_ref[...]
    k = kn * lax.rsqrt(ss + EPS) * gk_ref[...] + _tile_lanes(krp_ref[...], MLA_HEADS)
    k_ref[...] = k.astype(BF16)
    _store_vt(vt_ref, _nt(wuv_ref[...], c))


def _mla_kv(ckvn, krp, lw, consts, batch):
    n = ckvn.shape[0]
    lseg = n // batch
    tm = min(512, lseg)
    tpb = lseg // tm
    full = lambda shape: pl.BlockSpec(shape, lambda i: (0,) * len(shape))
    return pl.pallas_call(
        _mla_kv_kernel,
        grid=(n // tm,),
        in_specs=[pl.BlockSpec((tm, 128), lambda i: (i, 0)), pl.BlockSpec((tm, 128), lambda i: (i, 0)),
                  full((128, 512)), full((256, 128)), full((1, 512)), full((512, 512)), full((1, 512))],
        out_specs=[pl.BlockSpec((tm, 512), lambda i: (i, 0)),
                   pl.BlockSpec((1, 4, VT_ROWS, tm), lambda i: (i // tpb, 0, 0, i % tpb))],
        out_shape=[jax.ShapeDtypeStruct((n, 512), BF16), jax.ShapeDtypeStruct((batch, 4, VT_ROWS, lseg), BF16)],
        compiler_params=_cparams(("parallel",)),
        name="mla_kv",
    )(ckvn, krp, lw["w_uk"], lw["w_uv"], lw["gk"], consts["seg512"], consts["cnt512"])


def _softmax_pv(qs, k_refs, vt_refs, key_chunk, sub_rows, n_ahead, k_lane=0, v_head=0):
    where = [(i, lo) for i, r in enumerate(k_refs) for lo in range(0, r.shape[1], key_chunk)]
    nch = len(where)
    nq = len(qs)
    sub = min(sub_rows, key_chunk)
    nsub = key_chunk // sub

    def scores(c, u):
        seg, lo = where[c]
        ks = k_refs[seg][0, lo + u * sub:lo + (u + 1) * sub, k_lane:k_lane + LANE]
        return [_nt(ks, q) for q in qs]

    def chunk_max(s_chunk, j):
        mc = functools.reduce(jnp.maximum, [s_chunk[u][j] for u in range(nsub)])
        return jnp.max(mc, axis=0, keepdims=True)

    s_buf = {c: [scores(c, u) for u in range(nsub)] for c in range(min(n_ahead, nch))}
    m = [None] * nq
    acc = [None] * nq
    m_new = [chunk_max(s_buf[0], j) for j in range(nq)]
    for c in range(nch):
        s_cur = s_buf.pop(c)
        ahead = c + n_ahead
        if ahead < nch:
            s_buf[ahead] = []
        pv = [None] * nq
        for u in range(nsub):
            if ahead < nch:
                s_buf[ahead].append(scores(ahead, u))
            seg, lo = where[c]
            vs = vt_refs[seg][0, v_head, :, lo + u * sub:lo + (u + 1) * sub]
            for j in range(nq):
                part = jnp.dot(vs, jnp.exp2(s_cur[u][j] - m_new[j]).astype(BF16), preferred_element_type=F32)
                pv[j] = part if pv[j] is None else pv[j] + part
        for j in range(nq):
            acc[j] = pv[j] if c == 0 else acc[j] * jnp.exp2(m[j] - m_new[j]) + pv[j]
            m[j] = m_new[j]
        if c + 1 < nch:
            m_new = [jnp.maximum(m[j], chunk_max(s_buf[c + 1], j)) for j in range(nq)]
    return acc


def _mla_attn_kernel(key_chunk, nseg, heads, q_ref, *refs):
    k_refs, vt_refs, o_ref = refs[:nseg], refs[nseg:2 * nseg], refs[2 * nseg]
    for hh in range(heads):
        q = q_ref[:, LANE * hh:LANE * (hh + 1)]
        (acc,) = _softmax_pv([q], k_refs, vt_refs, key_chunk, MLA_SUB, MLA_AHEAD, LANE * hh, hh)
        o_ref[0, MLA_V * hh:MLA_V * (hh + 1)] = acc[0:MLA_V] / acc[MLA_V:MLA_V + 1]


def _key_chunk(lk, rows=512):
    return rows if lk % rows == 0 else lk


def _heads_per_step(ks):
    return 4 if sum(k.shape[1] for k in ks) <= 512 else 1


def _vt_kernel(v_ref, o_ref):
    _store_vt(o_ref, v_ref[0].astype(F32).T)


def _vt_with_ones(v3):
    b, lk, width = v3.shape
    tm = 1536 if lk % 1536 == 0 else lk
    return pl.pallas_call(
        _vt_kernel,
        grid=(b, lk // tm),
        in_specs=[pl.BlockSpec((1, tm, width), lambda i, j: (i, j, 0))],
        out_specs=pl.BlockSpec((1, 4, VT_ROWS, tm), lambda i, j: (i, 0, 0, j)),
        out_shape=jax.ShapeDtypeStruct((b, 4, VT_ROWS, lk), BF16),
        compiler_params=_cparams(("parallel", "parallel")),
        name="vt_ones",
    )(v3)


def _mla_attn(q, ks, vts, batch, seq_len):
    tq = 256
    nq = seq_len // tq
    chunk = _key_chunk(min(k.shape[1] for k in ks), MLA_CHUNK)
    hp = _heads_per_step(ks)
    k_specs = [pl.BlockSpec((1, k.shape[1], LANE * hp), lambda b, h, i: (b, 0, h)) for k in ks]
    v_specs = [pl.BlockSpec((1, hp, VT_ROWS, v.shape[3]), lambda b, h, i: (b, h, 0, 0)) for v in vts]
    return pl.pallas_call(
        functools.partial(_mla_attn_kernel, chunk, len(ks), hp),
        grid=(batch, MLA_HEADS // hp, nq),
        in_specs=[pl.BlockSpec((tq, LANE * hp), lambda b, h, i: (b * nq + i, h))] + k_specs + v_specs,
        out_specs=pl.BlockSpec((1, MLA_V * hp, tq), lambda b, h, i: (b, h, i)),
        out_shape=jax.ShapeDtypeStruct((batch, BRANCH, seq_len), F32),
        compiler_params=_cparams(("parallel", "parallel", "arbitrary")),
        name="mla_attn",
    )(q, *ks, *vts)


def _seg_const(width, seg):
    sid = np.arange(width) // seg
    return jnp.asarray((sid[:, None] == sid[None, :]).astype(np.float32), BF16)


def _diff_prep_kernel(rope, dq_ref, dk_ref, dv_ref, gq_ref, gk_ref, seg_ref, *rest):
    if rope:
        cos_ref, sin_ref, q_ref, k_ref, kf_ref, vt_ref = rest
    else:
        q_ref, k_ref, kf_ref, vt_ref = rest
    seg = seg_ref[...]
    _store_vt(vt_ref, dv_ref[...].T)

    def norm(x, g):
        ss = _segsum(x * x, seg) * (1.0 / DIFF_HD)
        return x * lax.rsqrt(ss + EPS) * g

    q = norm(dq_ref[...], gq_ref[...])
    k = norm(dk_ref[...], gk_ref[...])
    kf_ref[...] = k
    if rope:
        cos, sin = cos_ref[...], sin_ref[...]
        q = q * cos + _swap8(q) * sin
        k = k * cos + _swap8(k) * sin
    q_ref[...] = (q * (DIFF_SCALE * LOG2E)).astype(BF16)
    k_ref[...] = k.astype(BF16)


def _diff_prep(p, lw, consts, seq_len, rope_tabs):
    n = p.shape[0]
    tm = min(512, seq_len)
    rope = rope_tabs is not None
    full = lambda shape: pl.BlockSpec(shape, lambda i: (0,) * len(shape))
    in_specs = [pl.BlockSpec((tm, 256), lambda i: (i, P_DQ // 256)),
                pl.BlockSpec((tm, 256), lambda i: (i, P_DK // 256)),
                pl.BlockSpec((tm, 256), lambda i: (i, P_DV // 256)),
                full((1, 256)), full((1, 256)), full((256, 256))]
    args = [p, p, p, lw["dgq"], lw["dgk"], consts["seg32"]]
    tps = seq_len // tm
    if rope:
        in_specs += [pl.BlockSpec((tm, 256), lambda i: (i % tps, 0))] * 2
        args += [rope_tabs["cos_diff"], rope_tabs["sin_diff"]]
    blk = pl.BlockSpec((tm, 256), lambda i: (i, 0))
    return pl.pallas_call(
        functools.partial(_diff_prep_kernel, rope),
        grid=(n // tm,),
        in_specs=in_specs,
        out_specs=[blk, blk, blk, pl.BlockSpec((1, 4, VT_ROWS, tm), lambda i: (i // tps, 0, 0, i % tps))],
        out_shape=[jax.ShapeDtypeStruct((n, 256), BF16), jax.ShapeDtypeStruct((n, 256), BF16),
                   jax.ShapeDtypeStruct((n, 256), F32),
                   jax.ShapeDtypeStruct((n // seq_len, 4, VT_ROWS, seq_len), BF16)],
        compiler_params=_cparams(("parallel",)),
        name="diff_prep",
    )(*args)


def _diff_attn_kernel(lam_init, key_chunk, nseg, heads, q_ref, *refs):
    k_refs, vt_refs = refs[:nseg], refs[nseg:2 * nseg]
    lp_ref, g_ref, o_ref = refs[2 * nseg:]
    lp = lp_ref[...]
    lam = (jnp.exp(jnp.sum(lp[0:1] * lp[1:2], axis=1, keepdims=True))
           - jnp.exp(jnp.sum(lp[2:3] * lp[3:4], axis=1, keepdims=True)) + lam_init)
    for hh in range(heads):
        blk = 0 if heads == 1 else LANE * (hh // 2)
        base = (pl.program_id(1) % 2) * 64 if heads == 1 else (hh % 2) * 64
        q = q_ref[:, blk:blk + LANE]
        lane = lax.broadcasted_iota(jnp.int32, q.shape, 1)
        zero = jnp.zeros_like(q)

        def map_query(j):
            lo = base + 32 * j
            return jnp.where((lane >= lo) & (lane < lo + 32), q, zero)

        acc0, acc1 = _softmax_pv([map_query(0), map_query(1)], k_refs, vt_refs, key_chunk, DIFF_SUB, DIFF_AHEAD,
                                 blk, hh)
        o = acc0[0:64] / acc0[64:65] - lam * (acc1[0:64] / acc1[64:65])
        ms = jnp.mean(o * o, axis=0, keepdims=True)
        o_ref[0, 64 * hh:64 * (hh + 1)] = o * lax.rsqrt(ms + EPS) * g_ref[...] * (1.0 - lam_init)


def _diff_attn(q, ks, vts, lp, g_col, lam_init, batch, seq_len):
    tq = min(512, seq_len)
    nq = seq_len // tq
    chunk = _key_chunk(min(k.shape[1] for k in ks), DIFF_CHUNK)
    hp = _heads_per_step(ks)
    lanes = LANE if hp == 1 else BRANCH
    k_specs = [pl.BlockSpec((1, k.shape[1], lanes), lambda b, h, i: (b, 0, h // 2)) for k in ks]
    v_specs = [pl.BlockSpec((1, hp, VT_ROWS, v.shape[3]), lambda b, h, i: (b, h, 0, 0)) for v in vts]
    return pl.pallas_call(
        functools.partial(_diff_attn_kernel, lam_init, chunk, len(ks), hp),
        grid=(batch, DIFF_HEADS // hp, nq),
        in_specs=[pl.BlockSpec((tq, lanes), lambda b, h, i: (b * nq + i, h // 2))] + k_specs + v_specs
        + [pl.BlockSpec((4, DIFF_HD), lambda b, h, i: (0, 0)), pl.BlockSpec((64, 1), lambda b, h, i: (0, 0))],
        out_specs=pl.BlockSpec((1, 64 * hp, tq), lambda b, h, i: (b, h, i)),
        out_shape=jax.ShapeDtypeStruct((batch, BRANCH, seq_len), F32),
        compiler_params=_cparams(("parallel", "parallel", "arbitrary")),
        name="diff_attn",
    )(q, *ks, *vts, lp, g_col)


def _hgrn_consts():
    c = HGRN_CHUNK
    t = np.arange(c)
    low = (t[None, :] <= t[:, None]).astype(np.float32)
    blocks = []
    for j in range(HGRN_MM_LEVELS):
        m = 1 << j
        rho = (t // (2 * m)) * (2 * m) + m - 1
        sign = np.where((t // m) % 2 == 1, 1.0, -1.0)[:, None]
        blocks.append(sign * (low - (t[None, :] <= rho[:, None]).astype(np.float32)))
    blocks.append(low)
    fwd = np.concatenate(blocks, axis=0)
    bwd = np.concatenate([b[::-1, ::-1] for b in blocks], axis=0)
    right = np.stack([(t // (1 << j)) % 2 for j in range(HGRN_LEVELS)]).astype(np.float32)
    right = np.stack([right, right[:, ::-1]])
    right = np.broadcast_to(right[..., None], right.shape + (BRANCH,))
    return jnp.asarray(np.stack([fwd, bwd]), BF16), jnp.asarray(right, F32)


def _hgrn_kernel(nc, nb, qf_ref, zf_ref, vf_ref, qb_ref, zb_ref, vb_ref, lb_ref, dd_ref, rm_ref, s0_ref,
                 of_ref, ob_ref, sout_ref, st_ref):
    c = HGRN_CHUNK
    ci = pl.program_id(1)
    chains = [(bi, d) for bi in range(nb) for d in (0, 1)]
    ids = range(len(chains))

    @pl.when(ci == 0)
    def _():
        st_ref[...] = s0_ref[...]

    lane = lax.broadcasted_iota(jnp.int32, (1, BRANCH), 1)
    head_masks = [(lane >= HGRN_DK * h) & (lane < HGRN_DK * (h + 1)) for h in range(HGRN_HEADS)]
    t_idx = lax.broadcasted_iota(jnp.int32, (c, HGRN_HEADS * c), 0)
    s_idx = lax.broadcasted_iota(jnp.int32, (c, HGRN_HEADS * c), 1) & (c - 1)
    pair_xor = t_idx ^ s_idx

    def stack_heads(x):
        xb = x.astype(BF16)
        zero = jnp.zeros_like(xb)
        return jnp.concatenate([jnp.where(hm, xb, zero) for hm in head_masks], axis=0)

    q_refs, z_refs, v_refs = (qf_ref, qb_ref), (zf_ref, zb_ref), (vf_ref, vb_ref)
    q = [q_refs[d][bi] for bi, d in chains]
    v = [v_refs[d][bi] for bi, d in chains]
    z = [z_refs[d][bi] for bi, d in chains]
    lb = [lb_ref[d] for _, d in chains]
    g = [jnp.log(lb[i] + (1.0 - lb[i]) * jax.nn.sigmoid(z[i])) for i in ids]
    kk = [(1.0 - lb[i]) * jax.nn.sigmoid(-z[i]) for i in ids]
    sums = []
    for i in ids:
        gh, gl = _split2(g[i])
        dd = dd_ref[chains[i][1]]
        sums.append(jnp.dot(dd, gh, preferred_element_type=F32) + jnp.dot(dd, gl, preferred_element_type=F32))
    b = [sums[i][HGRN_MM_LEVELS * c:] for i in ids]
    b_tot = [b[i][c - 1:c] if chains[i][1] == 0 else b[i][0:1] for i in ids]

    def neg_abs_decay(i, j):
        if j < HGRN_MM_LEVELS:
            return sums[i][j * c:(j + 1) * c]
        m = 1 << j
        off = m - 1 if chains[i][1] == 0 else m
        ref = jnp.concatenate([jnp.broadcast_to(b[i][g0 + off:g0 + off + 1], (2 * m, BRANCH))
                               for g0 in range(0, c, 2 * m)], axis=0)
        return -jnp.abs(b[i] - ref)

    a = [None] * len(chains)
    for j in reversed(range(HGRN_LEVELS)):
        same_group = pair_xor < (2 << j)
        for i in ids:
            e = jnp.exp(neg_abs_decay(i, j))
            eq = e * rm_ref[chains[i][1], j]
            qt = q[i] * eq
            kt = kk[i] * (e - eq)
            lvl = _nt(qt, stack_heads(kt))
            a[i] = lvl if a[i] is None else jnp.where(same_group, lvl, a[i])
    diagonal = pair_xor == 0
    for i in ids:
        a[i] = jnp.where(diagonal, _nt(q[i], stack_heads(kk[i])), a[i])

    outs = (of_ref, ob_ref)
    for i in ids:
        bi, d = chains[i]
        o = jnp.dot(a[i].astype(BF16), stack_heads(v[i]), preferred_element_type=F32)
        outs[d][bi] = o + _nt(q[i] * jnp.exp(b[i]), st_ref[bi, d])

    r2 = lax.broadcasted_iota(jnp.int32, (BRANCH, BRANCH), 0) // HGRN_DK
    c2 = lax.broadcasted_iota(jnp.int32, (BRANCH, BRANCH), 1) // HGRN_DK
    for i in ids:
        bi, d = chains[i]
        kd = kk[i] * jnp.exp(b_tot[i] - b[i])
        upd = lax.dot_general(v[i].astype(BF16), kd.astype(BF16), (((0,), (0,)), ((), ())),
                              preferred_element_type=F32)
        st_new = st_ref[bi, d] * jnp.exp(b_tot[i]) + jnp.where(r2 == c2, upd, 0.0)
        st_ref[bi, d] = st_new

        @pl.when(ci == nc - 1)
        def _(bi=bi, d=d, st_new=st_new):
            sout_ref[bi, d] = st_new


def _hgrn(p, lb_l, dd, rm, st0, batch, seq_len):
    n = p.shape[0]
    c = HGRN_CHUNK
    nc = seq_len // c
    nb = HGRN_ROWS if batch % HGRN_ROWS == 0 else 1
    p3 = p.reshape(batch, seq_len, P_COLS)
    fwd = lambda col: pl.BlockSpec((nb, c, 256), lambda b, i: (b, i, col))
    bwd = lambda col: pl.BlockSpec((nb, c, 256), lambda b, i: (b, nc - 1 - i, col))
    whole = lambda shape: pl.BlockSpec(shape, lambda b, i: (0,) * len(shape))
    state = pl.BlockSpec((nb, 2, 256, 256), lambda b, i: (b, 0, 0, 0))
    o_f, o_b, st = pl.pallas_call(
        functools.partial(_hgrn_kernel, nc, nb),
        grid=(batch // nb, nc),
        in_specs=[fwd(P_HQ // 256), fwd(P_HZF // 256), fwd(P_HI // 256),
                  bwd(P_HQ // 256), bwd(P_HZB // 256), bwd(P_HI // 256),
                  whole((2, 1, 256)), whole((2, (HGRN_MM_LEVELS + 1) * c, c)),
                  whole((2, HGRN_LEVELS, c, 256)), state],
        out_specs=[pl.BlockSpec((nb, c, 256), lambda b, i: (b, i, 0)),
                   pl.BlockSpec((nb, c, 256), lambda b, i: (b, nc - 1 - i, 0)), state],
        out_shape=[jax.ShapeDtypeStruct((batch, seq_len, 256), F32),
                   jax.ShapeDtypeStruct((batch, seq_len, 256), F32),
                   jax.ShapeDtypeStruct((batch, 2, 256, 256), F32)],
        scratch_shapes=[pltpu.VMEM((nb, 2, 256, 256), F32)],
        compiler_params=_cparams(("parallel", "arbitrary")),
        name="hgrn",
    )(p3, p3, p3, p3, p3, p3, lb_l, dd, rm, st0)
    return o_f.reshape(n, 256), o_b.reshape(n, 256), st


def _hy_conv3_kernel(tiles_per_seq, above_ref, cur_ref, below_ref, w_ref, b_ref, v_ref, x1_ref, x2_ref):
    i = pl.program_id(0)
    cur = cur_ref[...]
    tm = cur.shape[0]
    first = (i % tiles_per_seq) == 0
    last = (i % tiles_per_seq) == tiles_per_seq - 1
    above = jnp.where(first, 0.0, above_ref[7:8, :])
    below = jnp.where(last, 0.0, below_ref[0:1, :])
    row = lax.broadcasted_iota(jnp.int32, (tm, 1), 0)
    prev = jnp.where(row == 0, above, pltpu.roll(cur, 1, 0))
    nxt = jnp.where(row == tm - 1, below, pltpu.roll(cur, tm - 1, 0))
    w = w_ref[...]
    u = prev * w[0:1] + cur * w[1:2] + nxt * w[2:3] + b_ref[...]
    v_ref[...] = u[:, 0:256]
    x1_ref[...] = u[:, 256:512]
    x2_ref[...] = u[:, 512:768]


def _hy_conv3(p, w, b, seq_len):
    n = p.shape[0]
    tm = min(512, seq_len)
    nt = n // tm
    g = tm // 8
    col = P_HU // 768
    oblk = pl.BlockSpec((tm, 256), lambda i: (i, 0))
    return pl.pallas_call(
        functools.partial(_hy_conv3_kernel, seq_len // tm),
        grid=(nt,),
        in_specs=[pl.BlockSpec((8, 768), lambda i: (jnp.maximum(i * g - 1, 0), col)),
                  pl.BlockSpec((tm, 768), lambda i: (i, col)),
                  pl.BlockSpec((8, 768), lambda i: (jnp.minimum((i + 1) * g, nt * g - 1), col)),
                  pl.BlockSpec((3, 768), lambda i: (0, 0)), pl.BlockSpec((1, 768), lambda i: (0, 0))],
        out_specs=[oblk, oblk, oblk],
        out_shape=[jax.ShapeDtypeStruct((n, 256), F32)] * 3,
        compiler_params=_cparams(("parallel",)),
        name="hy_conv3",
    )(p, p, p, w, b)


def _hy_filter_kernel(feat_ref, w1_ref, b1_ref, w2_ref, b2_ref, w3_ref, fr_ref, win_ref, o_ref):
    fr = fr_ref[...]
    h = jnp.sin(fr[0:1] * (_dot3(feat_ref[...], w1_ref[...]) + b1_ref[...]))
    h = jnp.sin(fr[1:2] * (_dot3(h, w2_ref[...]) + b2_ref[...]))
    o_ref[...] = _dot3(h, w3_ref[0]) * _tile_lanes(win_ref[...], HY_ORDER)


def _hy_filter(feats2, window2, w1p, b1, w2, b2, w3d, freq):
    l2 = feats2.shape[0]
    ln = l2 // 2
    tm = min(512, ln)
    full = lambda shape: pl.BlockSpec(shape, lambda i: (0,) * len(shape))
    return pl.pallas_call(
        _hy_filter_kernel,
        grid=(l2 // tm,),
        in_specs=[pl.BlockSpec((tm, LANE), lambda i: (i, 0)),
                  full((LANE, HY_FH)), full((1, HY_FH)), full((HY_FH, HY_FH)), full((1, HY_FH)),
                  pl.BlockSpec((1, HY_FH, HY_ORDER * HY_CH), lambda i: (i // (ln // tm), 0, 0)), full((2, HY_FH)),
                  pl.BlockSpec((tm, HY_CH), lambda i: (i, 0))],
        out_specs=pl.BlockSpec((tm, HY_ORDER * HY_CH), lambda i: (i, 0)),
        out_shape=jax.ShapeDtypeStruct((l2, HY_ORDER * HY_CH), F32),
        compiler_params=_cparams(("parallel",)),
        name="hy_filter",
    )(feats2, w1p, b1, w2, b2, w3d, freq, window2)


def _fft_blocking(nb, n1, n2, ch):
    per_batch = n1 * n2 * ch * 4
    if per_batch <= FFT_BLOCK_BYTES:
        bb = max(1, min(nb, FFT_BLOCK_BYTES // per_batch))
        while nb % bb:
            bb -= 1
        return bb, n2
    rt = n2
    while n1 * rt * ch * 4 > FFT_BLOCK_BYTES and rt > 8:
        rt //= 2
    return 1, rt


def _fft_a_kernel(f_ref, x_ref, o_ref):
    f = f_ref[...]
    for b in range(x_ref.shape[0]):
        x = x_ref[b].astype(BF16)
        o_ref[b] = jnp.einsum("kn,nrc->krc", f, x, preferred_element_type=F32).astype(o_ref.dtype)


def _fft_a(fa, x4, out_dtype):
    nb, n1, n2, ch = x4.shape
    r = fa.shape[0]
    bb, rt = _fft_blocking(nb, n1, n2, ch)
    return pl.pallas_call(
        _fft_a_kernel,
        grid=(nb // bb, n2 // rt),
        in_specs=[pl.BlockSpec((r, n1), lambda b, i: (0, 0)),
                  pl.BlockSpec((bb, n1, rt, ch), lambda b, i: (b, 0, i, 0))],
        out_specs=pl.BlockSpec((bb, r, rt, ch), lambda b, i: (b, 0, i, 0)),
        out_shape=jax.ShapeDtypeStruct((nb, r, n2, ch), out_dtype),
        compiler_params=_cparams(("parallel", "parallel")),
        name="fft_a",
    )(fa, x4)


def _fft_b_kernel(with_inverse, mf_ref, *rest):
    if with_inverse:
        mi_ref, a_ref, h_ref, o_ref = rest
    else:
        a_ref, o_ref = rest
    half = FFT_N2
    for kk in range(a_ref.shape[2]):
        for b in range(a_ref.shape[0]):
            a = jnp.concatenate([a_ref[b, 0, kk], a_ref[b, 1, kk]], axis=0)
            x = jnp.dot(mf_ref[kk], a.astype(BF16), preferred_element_type=F32)
            if with_inverse:
                xr, xi = x[:half], x[half:]
                hr, hi = h_ref[0, kk], h_ref[1, kk]
                y = jnp.concatenate([xr * hr - xi * hi, xr * hi + xi * hr], axis=0)
                x = jnp.dot(mi_ref[kk], y.astype(BF16), preferred_element_type=F32)
            o_ref[b, 0, kk] = x[:half].astype(o_ref.dtype)
            o_ref[b, 1, kk] = x[half:].astype(o_ref.dtype)


def _fft_b(mf, mi, a5, spec, order):
    nb, _, k1n, n2, ch = a5.shape
    ks = 3 if (k1n % 3 == 0 and nb * ch <= 1024) else 1
    mat = pl.BlockSpec((ks, 2 * n2, 2 * n2), lambda k: (k, 0, 0))
    blk = pl.BlockSpec((nb, 2, ks, n2, ch), lambda k: (0, 0, k, 0, 0))
    if spec is None:
        in_specs, args = [mat, blk], [mf, a5]
    else:
        in_specs = [mat, mat, blk, pl.BlockSpec((2, ks, n2, ch), lambda k: (0, k, 0, order))]
        args = [mf, mi, a5, spec]
    return pl.pallas_call(
        functools.partial(_fft_b_kernel, spec is not None),
        grid=(k1n // ks,),
        in_specs=in_specs,
        out_specs=blk,
        out_shape=jax.ShapeDtypeStruct(a5.shape, F32 if spec is None else BF16),
        compiler_params=_cparams(("parallel",)),
        name="fft_b",
    )(*args)


def _fft_a_inv_kernel(g_ref, p_ref, x_ref, z_ref, bias_ref, o_ref):
    g = g_ref[...]
    for b in range(p_ref.shape[0]):
        conv = jnp.einsum("nk,krc->nrc", g, p_ref[b], preferred_element_type=F32)
        o_ref[b] = x_ref[b] * (conv + z_ref[b] * bias_ref[...])


def _fft_a_inv(g, p4, xg4, z4, bias):
    nb, n1, n2, ch = z4.shape
    r = g.shape[1]
    bb, rt = _fft_blocking(nb, n1, n2, ch)
    blk = pl.BlockSpec((bb, n1, rt, ch), lambda b, i: (b, 0, i, 0))
    return pl.pallas_call(
        _fft_a_inv_kernel,
        grid=(nb // bb, n2 // rt),
        in_specs=[pl.BlockSpec((n1, r), lambda b, i: (0, 0)),
                  pl.BlockSpec((bb, r, rt, ch), lambda b, i: (b, 0, i, 0)), blk, blk,
                  pl.BlockSpec((1, 1, ch), lambda b, i: (0, 0, 0))],
        out_specs=blk,
        out_shape=jax.ShapeDtypeStruct(z4.shape, F32),
        compiler_params=_cparams(("parallel", "parallel")),
        name="fft_a_inv",
    )(g, p4, xg4, z4, bias.reshape(1, 1, ch))


def _fft_tables(ln):
    n = 2 * ln
    n1t = n // FFT_N2
    k1n = n1t // 2 + 1
    kk = np.arange(k1n)

    def stage_a(n1_in):
        ang = 2.0 * np.pi * ((kk[:, None] * np.arange(n1_in)[None, :]) % n1t) / n1t
        return jnp.asarray(np.concatenate([np.cos(ang), -np.sin(ang)], axis=0), BF16)

    n1o = n1t // 2
    ang = 2.0 * np.pi * ((np.arange(n1o)[:, None] * kk[None, :]) % n1t) / n1t
    edge = (kk == 0) | (kk == n1t // 2)
    ck = np.where(edge, 1.0, 2.0) / n
    g = jnp.asarray(np.concatenate([ck * np.cos(ang), -ck * np.where(edge, 0.0, np.sin(ang))], axis=1), BF16)

    k1 = jnp.arange(k1n, dtype=jnp.int32)[:, None, None]
    k2 = jnp.arange(FFT_N2, dtype=jnp.int32)[None, :, None]
    n2 = jnp.arange(FFT_N2, dtype=jnp.int32)[None, None, :]
    th = (2.0 * math.pi / n) * ((n2 * (k1 + n1t * k2)) % n).astype(F32)
    c, s = jnp.cos(th), jnp.sin(th)
    mf = jnp.concatenate([jnp.concatenate([c, s], axis=2), jnp.concatenate([-s, c], axis=2)], axis=1)
    return dict(fa_half=stage_a(n1o), fa_full=stage_a(n1t), g=g, mf=mf.astype(BF16),
                mi=mf.transpose(0, 2, 1).astype(BF16), k1n=k1n, n1o=n1o, n1t=n1t)


def _hy_static(ln):
    t = jnp.linspace(0.0, 1.0, ln, dtype=F32)[:, None]
    w = 2.0 * math.pi * jnp.arange(ln, dtype=F32) / ln
    f = jnp.linspace(1e-4, HY_BANDS - 1, HY_BANDS, dtype=F32)
    ang = w[:, None] * f[None, :]
    feats = jnp.concatenate([t, jnp.cos(ang), -jnp.sin(ang)], axis=-1)
    feats = jnp.pad(feats, ((0, 0), (0, LANE - HY_EMB)))
    min_decay = math.log(HY_DECAY_TARGET) / HY_SLOW_DECAY
    max_decay = math.log(HY_DECAY_TARGET) / HY_FAST_DECAY
    deltas = jnp.linspace(min_decay, max_decay, HY_CH, dtype=F32)
    window = jnp.exp(-t * jnp.abs(deltas))
    feats = jnp.concatenate([feats, feats[::-1]], axis=0)
    window = jnp.concatenate([window, window[::-1]], axis=0)
    return feats, window


def _outproj_kernel(x_ref, mod_ref, oa_ref, ob_ref, of_ref, obk_ref, od_ref, gate_ref, hg_ref, seg_ref, w_ref,
                    y_ref):
    gt = gate_ref[...]
    sg = gt * jax.nn.sigmoid(gt)
    oc = of_ref[...] + obk_ref[...]
    ss = _segsum(oc * oc, seg_ref[...]) * (1.0 / HGRN_DK)
    oc = oc * lax.rsqrt(ss + EPS) * hg_ref[...]
    acc = _bdot(oa_ref[0].T * sg[:, 0:256], w_ref[0:256, :])
    acc += _bdot(ob_ref[0].T * sg[:, 256:512], w_ref[256:512, :])
    acc += _bdot(oc * sg[:, 512:768], w_ref[512:768, :])
    acc += _bdot(od_ref[...] * sg[:, 768:1024], w_ref[768:1024, :])
    y_ref[...] = x_ref[...] + mod_ref[0, 2:3, :] * acc


def _outproj(x2, mod, ot_a, ot_b, o_f, o_b, out_d, p, hg, seg64, w_out, seq_len):
    n = x2.shape[0]
    tm = min(512, seq_len)
    per_batch = mod.shape[0] > 1
    tps = seq_len // tm
    mod_idx = (lambda i: (i // tps, 0, 0)) if per_batch else (lambda i: (0, 0, 0))
    b256 = pl.BlockSpec((tm, 256), lambda i: (i, 0))
    bt = pl.BlockSpec((1, 256, tm), lambda i: (i // tps, 0, i % tps))
    return pl.pallas_call(
        _outproj_kernel,
        grid=(n // tm,),
        in_specs=[pl.BlockSpec((tm, D_MODEL), lambda i: (i, 0)),
                  pl.BlockSpec((1, 3, D_MODEL), mod_idx),
                  bt, bt, b256, b256, b256,
                  pl.BlockSpec((tm, 1024), lambda i: (i, P_GATE // 1024)),
                  pl.BlockSpec((1, 256), lambda i: (0, 0)),
                  pl.BlockSpec((256, 256), lambda i: (0, 0)),
                  pl.BlockSpec((D_MODEL, D_MODEL), lambda i: (0, 0))],
        out_specs=pl.BlockSpec((tm, D_MODEL), lambda i: (i, 0)),
        out_shape=jax.ShapeDtypeStruct((n, D_MODEL), F32),
        compiler_params=_cparams(("parallel",)),
        name="outproj",
    )(x2, mod, ot_a, ot_b, o_f, o_b, out_d, p, hg, seg64, w_out)


def _layer(x2, mod, lw, consts, batch, seq_len, ctx, rope_tabs, hy):
    n = batch * seq_len
    p = _inproj(x2, mod, lw["norm_g"], lw["w_in"], seq_len)

    q_a, ckvn, krp = _mla_q(p, lw, consts, seq_len, rope_tabs)
    ckv3 = ckvn.reshape(batch, seq_len, MLA_KV_LORA)
    krp3 = krp.reshape(batch, seq_len, LANE)
    k_a, vt_a = _mla_kv(ckvn, krp, lw, consts, batch)
    ks_a, vts_a = [k_a.reshape(batch, seq_len, 512)], [vt_a]
    if ctx is not None:
        lc = ctx[0].shape[1]
        k_c, vt_c = _mla_kv(ctx[0].reshape(batch * lc, MLA_KV_LORA), ctx[1].reshape(batch * lc, LANE), lw, consts,
                            batch)
        ks_a.append(k_c.reshape(batch, lc, 512))
        vts_a.append(vt_c)
    ot_a = _mla_attn(q_a, ks_a, vts_a, batch, seq_len)

    q_b, k_b, kd, vt_b = _diff_prep(p, lw, consts, seq_len, rope_tabs)
    ks_b, vts_b = [k_b.reshape(batch, seq_len, BRANCH)], [vt_b]
    if ctx is not None:
        ks_b.append(ctx[2])
        vts_b.append(ctx[3])
    ot_b = _diff_attn(q_b, ks_b, vts_b, lw["diff_lambda"], lw["subln_col"], lw["lam_init"], batch, seq_len)

    if ctx is not None:
        s0 = ctx[4]
    else:
        s0 = jnp.zeros((batch, 2, HGRN_HEADS, HGRN_DK, HGRN_DK), F32)
    eye = jnp.eye(HGRN_HEADS, dtype=F32)
    st0 = jnp.einsum("bdhke,hg->bdhegk", s0, eye).reshape(batch, 2, BRANCH, BRANCH)
    o_f, o_b, st_out = _hgrn(p, lw["hgrn_lb"], consts["hgrn_dd"], consts["hgrn_right"], st0, batch, seq_len)
    st5 = st_out.reshape(batch, 2, HGRN_HEADS, HGRN_DK, HGRN_HEADS, HGRN_DK)
    states = jnp.stack([st5[:, :, h, :, h, :] for h in range(HGRN_HEADS)], axis=2).swapaxes(-1, -2)

    v_d, x1, x2g = _hy_conv3(p, lw["hy_conv_w"], lw["hy_conv_b"], seq_len)
    taps = _hy_filter(hy["feats"], hy["window"], lw["hy_w1"], lw["hy_b1"], lw["hy_w2"], lw["hy_b2"], lw["hy_w3"],
                      lw["hy_freq"])
    k1n, n1o, n1t = hy["k1n"], hy["n1o"], hy["n1t"]
    ta = _fft_a(hy["fa_full"], taps.reshape(1, n1t, FFT_N2, HY_ORDER * HY_CH), BF16)
    spec = _fft_b(hy["mf"], None, ta.reshape(1, 2, k1n, FFT_N2, HY_ORDER * HY_CH), None, 0)[0]
    z4 = v_d.reshape(batch, n1o, FFT_N2, HY_CH)
    for o, xg in enumerate((x1, x2g)):
        a = _fft_a(hy["fa_half"], z4, BF16).reshape(batch, 2, k1n, FFT_N2, HY_CH)
        pk = _fft_b(hy["mf"], hy["mi"], a, spec, o).reshape(batch, 2 * k1n, FFT_N2, HY_CH)
        z4 = _fft_a_inv(hy["g"], pk, xg.reshape(batch, n1o, FFT_N2, HY_CH), z4, lw["hy_bias"][o:o + 1])
    out_d = z4.reshape(n, HY_CH)

    y = _outproj(x2, mod, ot_a, ot_b, o_f, o_b, out_d, p, lw["hgrn_out_g"], consts["seg64"], lw["w_out"], seq_len)
    new = None
    if ctx is None:
        new = (ckv3, krp3[:, :, KR_OFF:KR_OFF + MLA_ROPE],
               kd.reshape(batch, seq_len, DIFF_HEADS, 2, DIFF_HD),
               p[:, P_DV:P_DV + BRANCH].reshape(batch, seq_len, DIFF_HEADS, 2 * DIFF_HD), states)
    return y, new


def _rope_tables(seq_len):
    half = MLA_ROPE // 2
    inv = ROPE_BASE ** (-jnp.arange(0, half, 2, dtype=F32) / half)
    rows = seq_len // GRID_W
    row = jnp.repeat(jnp.arange(rows, dtype=F32), GRID_W)
    col = (jnp.arange(rows * GRID_W) % GRID_W).astype(F32)
    ar, ac = row[:, None] * inv, col[:, None] * inv
    cos32 = jnp.concatenate([jnp.cos(ar), jnp.cos(ar), jnp.cos(ac), jnp.cos(ac)], axis=-1)
    sin32 = jnp.concatenate([-jnp.sin(ar), jnp.sin(ar), -jnp.sin(ac), jnp.sin(ac)], axis=-1)
    pad = ((0, 0), (KR_OFF, LANE - KR_OFF - MLA_ROPE))
    return dict(cos_mla=jnp.pad(cos32, pad, constant_values=1.0), sin_mla=jnp.pad(sin32, pad),
                cos_diff=jnp.tile(cos32, (1, 2 * DIFF_HEADS)), sin_diff=jnp.tile(sin32, (1, 2 * DIFF_HEADS)))


def _hy_tables(seq_len):
    feats, window = _hy_static(seq_len)
    return dict(feats=feats, window=window, **_fft_tables(seq_len))


def _layer_weights(l, w_in_p, lb, W):
    def head_pad(w, width, per):
        k = w.shape[0]
        w = w.reshape(k, MLA_HEADS, per)[:, :, :width]
        return jnp.pad(w, ((0, 0), (0, 0), (0, LANE - width))).reshape(k, MLA_HEADS * LANE)

    w_ukv = W["mla_w_ukv"][l].reshape(MLA_KV_LORA, MLA_HEADS, MLA_NOPE + MLA_V)
    nope_g, rope_g = W["mla_nope_g"][l], W["mla_rope_g"][l]
    zeros32 = jnp.zeros((MLA_ROPE,), F32)
    zeros64 = jnp.zeros((MLA_NOPE,), F32)
    gq = jnp.tile(jnp.concatenate([nope_g[0], rope_g[0], zeros32]), MLA_HEADS).reshape(1, 512)
    gk = jnp.tile(jnp.concatenate([nope_g[1], zeros64]), MLA_HEADS).reshape(1, 512)
    gkr = jnp.concatenate([zeros64, rope_g[1], zeros32]).reshape(1, LANE)
    return dict(
        norm_g=W["norm_g"][l], w_in=w_in_p[l], w_out=W["w_out"][l].astype(BF16),
        qn_g=W["mla_q_norm_g"][l].reshape(1, -1),
        w_uq=head_pad(W["mla_w_uq"][l], MLA_NOPE + MLA_ROPE, MLA_NOPE + MLA_ROPE).astype(BF16),
        kvn_g=W["mla_kv_norm_g"][l].reshape(1, -1),
        w_uk=jnp.pad(w_ukv[:, :, :MLA_NOPE], ((0, 0), (0, 0), (0, LANE - MLA_NOPE))).reshape(MLA_KV_LORA, 512)
        .astype(BF16),
        w_uv=w_ukv[:, :, MLA_NOPE:].reshape(MLA_KV_LORA, BRANCH).T.astype(BF16),
        gq=gq, gk=gk, gkr=gkr,
        dgq=jnp.tile(W["diff_qk_g"][l, 0], 2 * DIFF_HEADS).reshape(1, BRANCH),
        dgk=jnp.tile(W["diff_qk_g"][l, 1], 2 * DIFF_HEADS).reshape(1, BRANCH),
        diff_lambda=W["diff_lambda"][l], subln_col=W["diff_subln_g"][l].reshape(2 * DIFF_HD, 1),
        lam_init=0.8 - 0.6 * math.exp(-0.3 * l),
        hgrn_lb=lb[:, l].reshape(2, 1, BRANCH),
        hgrn_out_g=jnp.tile(W["hgrn_out_g"][l], HGRN_HEADS).reshape(1, BRANCH),
        hy_conv_w=W["hy_conv_w"][l], hy_conv_b=W["hy_conv_b"][l].reshape(1, -1),
        hy_w1=jnp.pad(W["hy_w1"][l], ((0, LANE - HY_EMB), (0, 0))), hy_b1=W["hy_b1"][l].reshape(1, -1),
        hy_w2=W["hy_w2"][l], hy_b2=W["hy_b2"][l].reshape(1, -1),
        hy_w3=W["hy_w3"][l].reshape(HY_FH, HY_ORDER, 2, HY_CH).transpose(2, 0, 1, 3)
        .reshape(2, HY_FH, HY_ORDER * HY_CH),
        hy_freq=W["hy_sin_freq"][l], hy_bias=W["hy_bias"][l],
    )


def kernel(x_prompt, x_sample, cache_mla_ckv, cache_mla_krope, cache_diff_k, cache_diff_v, state_hgrn, c, c_ctx,
           norm_g, w_mod, b_mod, w_in, w_out, mla_q_norm_g, mla_w_uq, mla_kv_norm_g, mla_w_ukv, mla_nope_g,
           mla_rope_g, diff_qk_g, diff_lambda, diff_subln_g, hgrn_lb_logits, hgrn_out_g, hy_conv_w, hy_conv_b,
           hy_w1, hy_b1, hy_w2, hy_b2, hy_w3, hy_sin_freq, hy_bias):
    W = dict(norm_g=norm_g, w_out=w_out, mla_q_norm_g=mla_q_norm_g, mla_w_uq=mla_w_uq,
             mla_kv_norm_g=mla_kv_norm_g, mla_w_ukv=mla_w_ukv, mla_nope_g=mla_nope_g, mla_rope_g=mla_rope_g,
             diff_qk_g=diff_qk_g, diff_lambda=diff_lambda, diff_subln_g=diff_subln_g, hgrn_out_g=hgrn_out_g,
             hy_conv_w=hy_conv_w, hy_conv_b=hy_conv_b, hy_w1=hy_w1, hy_b1=hy_b1, hy_w2=hy_w2, hy_b2=hy_b2,
             hy_w3=hy_w3, hy_sin_freq=hy_sin_freq, hy_bias=hy_bias)
    bp, lp, _ = x_prompt.shape
    bs, ls, _ = x_sample.shape

    w_in_p = _reorder_in_cols(w_in.astype(BF16))
    cvecs = jnp.concatenate([c_ctx[None, :], c, jnp.zeros((8 - 1 - bs, D_MODEL), F32)], axis=0)
    mods = _mod_all(cvecs, w_mod, b_mod)
    lb = _hgrn_lb(hgrn_lb_logits)
    seg512, cnt512 = _mla_seg()
    hgrn_dd, hgrn_right = _hgrn_consts()
    consts = dict(seg512=seg512, cnt512=cnt512, seg32=_seg_const(BRANCH, DIFF_HD), seg64=_seg_const(BRANCH, HGRN_DK),
                  hgrn_dd=hgrn_dd, hgrn_right=hgrn_right)
    lws = [_layer_weights(l, w_in_p, lb, W) for l in range(DEPTH)]

    hy_p = _hy_tables(lp)
    y = x_prompt.reshape(bp * lp, D_MODEL)
    per_layer = []
    for l in range(DEPTH):
        mod = mods[l, 0:1].reshape(1, 3, D_MODEL)
        y, new = _layer(y, mod, lws[l], consts, bp, lp, None, None, hy_p)
        per_layer.append(new)
    y_prompt = y.reshape(bp, lp, D_MODEL)
    news = [jnp.stack([s[i] for s in per_layer], axis=1) for i in range(5)]

    hy_s = _hy_tables(ls)
    rope_tabs = _rope_tables(ls)
    y = x_sample.reshape(bs * ls, D_MODEL)
    past = cache_mla_ckv.shape[2]
    cache_kr = jnp.pad(cache_mla_krope, ((0, 0), (0, 0), (0, 0), (KR_OFF, LANE - KR_OFF - MLA_ROPE)))
    cache_kb = cache_diff_k.reshape(bs, DEPTH, past, BRANCH).astype(BF16)
    cache_vtb = _vt_with_ones(cache_diff_v.reshape(bs * DEPTH, past, BRANCH).astype(BF16))
    cache_vtb = cache_vtb.reshape(bs, DEPTH, DIFF_HEADS, VT_ROWS, past)
    for l in range(DEPTH):
        mod = mods[l, 1:1 + bs].reshape(bs, 3, D_MODEL)
        ctx = (cache_mla_ckv[:, l], cache_kr[:, l], cache_kb[:, l], cache_vtb[:, l], state_hgrn[:, l])
        y, _ = _layer(y, mod, lws[l], consts, bs, ls, ctx, rope_tabs, hy_s)
    y_sample = y.reshape(bs, ls, D_MODEL)

    return (y_prompt, y_sample, news[0], news[1], news[2], news[3], news[4])
```

```python
import functools
import math

import numpy as np
import jax
import jax.numpy as jnp
from jax import lax
from jax.experimental import pallas as pl
from jax.experimental.pallas import tpu as pltpu

F32 = jnp.float32
BF16 = jnp.bfloat16

D_MODEL = 1024
DEPTH = 4
GRID_W = 64
ROPE_BASE = 10000.0
EPS = 1e-6
BRANCH = 256
MLA_HEADS = 4
MLA_NOPE = 64
MLA_ROPE = 32
MLA_V = 64
MLA_Q_LORA = 256
MLA_KV_LORA = 128
MLA_SCALE = (MLA_NOPE + MLA_ROPE) ** -0.5
DIFF_HEADS = 4
DIFF_HD = 32
DIFF_SCALE = DIFF_HD ** -0.5
HGRN_HEADS = 4
HGRN_DK = 64
HGRN_CHUNK = 128
HGRN_LEVELS = 7
HGRN_ROWS = 2
HGRN_MM_LEVELS = 3
HY_CH = 256
HY_ORDER = 2
HY_EMB = 33
HY_BANDS = 16
HY_FH = 64
HY_DECAY_TARGET = 0.01
HY_FAST_DECAY = 0.3
HY_SLOW_DECAY = 1.5
IN_COLS = 4000

LANE = 128
LOG2E = math.log2(math.e)
VT_ROWS = 80
FFT_N2 = 128
FFT_BLOCK_BYTES = 2 * 1024 * 1024
MLA_AHEAD = 8
MLA_CHUNK = 512
DIFF_AHEAD = 6
DIFF_CHUNK = 256
MLA_SUB = 256
DIFF_SUB = 512
VMEM_LIMIT = 52 * 1024 * 1024

P_CQ, P_CKV, P_KR, P_DQ, P_DK, P_DV = 0, 256, 384, 512, 768, 1024
P_HQ, P_HZF, P_HZB, P_HI, P_HU, P_GATE = 1280, 1536, 1792, 2048, 2304, 3072
P_COLS = 4096
KR_OFF = 64


def _in_col_perm():
    src = np.full((P_COLS,), IN_COLS, np.int32)

    def put(dst, lo, n):
        src[dst:dst + n] = np.arange(lo, lo + n)

    put(P_CQ, 0, 256)
    put(P_CKV, 256, 128)
    put(P_KR + KR_OFF, 384, 32)
    put(P_GATE, 416, 256)
    put(P_DQ, 672, 256)
    put(P_DK, 928, 256)
    put(P_DV, 1184, 256)
    put(P_GATE + 256, 1440, 256)
    put(P_HQ, 1696, 256)
    put(P_HZF, 1952, 256)
    put(P_HZB, 2208, 256)
    put(P_HI, 2464, 256)
    put(P_GATE + 512, 2720, 256)
    put(P_HU, 2976, 768)
    put(P_GATE + 768, 3744, 256)
    return src


def _reorder_in_cols(w):
    src = _in_col_perm()
    pieces, lo = [], 0
    while lo < P_COLS:
        hi = lo + 1
        if src[lo] == IN_COLS:
            while hi < P_COLS and src[hi] == IN_COLS:
                hi += 1
            pieces.append(jnp.zeros(w.shape[:-1] + (hi - lo,), w.dtype))
        else:
            while hi < P_COLS and src[hi] == src[hi - 1] + 1:
                hi += 1
            pieces.append(w[..., int(src[lo]):int(src[lo]) + hi - lo])
        lo = hi
    return jnp.concatenate(pieces, axis=-1)


def _cparams(sem):
    return pltpu.CompilerParams(dimension_semantics=sem, vmem_limit_bytes=VMEM_LIMIT)


def _bdot(a, b):
    return jnp.dot(a.astype(BF16), b.astype(BF16), preferred_element_type=F32)


def _nt(a, b):
    return lax.dot_general(a.astype(BF16), b.astype(BF16), (((1,), (1,)), ((), ())), preferred_element_type=F32)


def _split2(a):
    hi = a.astype(BF16)
    lo = (a - hi.astype(F32)).astype(BF16)
    return hi, lo


def _dot3(a, b):
    ah, al = _split2(a)
    bh, bl = _split2(b)
    d = functools.partial(jnp.dot, preferred_element_type=F32)
    return d(ah, bh) + d(ah, bl) + d(al, bh)


def _segsum(v, seg):
    return jnp.dot(v.astype(BF16), seg, preferred_element_type=F32)


def _rms(x, g):
    return x * lax.rsqrt(jnp.mean(x * x, axis=-1, keepdims=True) + EPS) * g


def _swap8(x):
    w = x.shape[-1]
    lane = lax.broadcasted_iota(jnp.int32, x.shape, x.ndim - 1)
    up = pltpu.roll(x, w - 8, x.ndim - 1)
    dn = pltpu.roll(x, 8, x.ndim - 1)
    return jnp.where((lane & 15) < 8, up, dn)


def _tile_lanes(x, n):
    return x if n == 1 else jnp.concatenate([x] * n, axis=-1)


def _mod_kernel(c_ref, w_ref, b_ref, o_ref):
    c = c_ref[...]
    o_ref[0] = _dot3(c * jax.nn.sigmoid(c), w_ref[0]) + b_ref[0]


def _mod_all(cvecs, w_mod, b_mod):
    nt = 3
    return pl.pallas_call(
        _mod_kernel,
        grid=(DEPTH, nt),
        in_specs=[pl.BlockSpec((8, D_MODEL), lambda l, j: (0, 0)),
                  pl.BlockSpec((1, D_MODEL, D_MODEL), lambda l, j: (l, 0, j)),
                  pl.BlockSpec((1, 1, D_MODEL), lambda l, j: (l, 0, j))],
        out_specs=pl.BlockSpec((1, 8, D_MODEL), lambda l, j: (l, 0, j)),
        out_shape=jax.ShapeDtypeStruct((DEPTH, 8, 3 * D_MODEL), F32),
        compiler_params=_cparams(("arbitrary", "arbitrary")),
        name="mod",
    )(cvecs, w_mod, b_mod.reshape(DEPTH, 1, 3 * D_MODEL))


def _lb_kernel(x_ref, o_ref):
    x = x_ref[...]
    rows = [x[l:l + 1, :] for l in range(DEPTH)]
    m = functools.reduce(jnp.maximum, rows)
    e = [jnp.exp(r - m) for r in rows]
    tot = functools.reduce(lambda a, b: a + b, e)
    acc = jnp.zeros_like(tot)
    o_ref[0:1, :] = acc
    for l in range(1, DEPTH):
        acc = acc + e[l] / tot
        o_ref[l:l + 1, :] = acc


def _hgrn_lb(logits):
    flat = logits.transpose(1, 0, 2).reshape(DEPTH, 2 * BRANCH)
    lb = pl.pallas_call(
        _lb_kernel,
        out_shape=jax.ShapeDtypeStruct(flat.shape, F32),
        name="hgrn_lb",
    )(flat)
    return lb.reshape(DEPTH, 2, BRANCH).transpose(1, 0, 2)


def _inproj_kernel(x_ref, mod_ref, g_ref, w_ref, p_ref):
    h = _rms(x_ref[...], g_ref[...]) * (1.0 + mod_ref[0, 1:2, :]) + mod_ref[0, 0:1, :]
    p_ref[...] = jnp.dot(h.astype(BF16), w_ref[...], preferred_element_type=F32)


def _inproj(x2, mod, norm_g, w_in_p, seq_len):
    n = x2.shape[0]
    tm = min(512, seq_len)
    per_batch = mod.shape[0] > 1
    tiles_per_seq = seq_len // tm
    mod_idx = (lambda i: (i // tiles_per_seq, 0, 0)) if per_batch else (lambda i: (0, 0, 0))
    return pl.pallas_call(
        _inproj_kernel,
        grid=(n // tm,),
        in_specs=[pl.BlockSpec((tm, D_MODEL), lambda i: (i, 0)),
                  pl.BlockSpec((1, 3, D_MODEL), mod_idx),
                  pl.BlockSpec((1, D_MODEL), lambda i: (0, 0)),
                  pl.BlockSpec((D_MODEL, P_COLS), lambda i: (0, 0))],
        out_specs=pl.BlockSpec((tm, P_COLS), lambda i: (i, 0)),
        out_shape=jax.ShapeDtypeStruct((n, P_COLS), F32),
        compiler_params=_cparams(("parallel",)),
        name="inproj",
    )(x2, mod, norm_g.reshape(1, D_MODEL), w_in_p)


def _mla_seg():
    sid = np.zeros((512,), np.int32)
    cnt = np.ones((512,), np.float32)
    for h in range(MLA_HEADS):
        b = 128 * h
        sid[b:b + 64] = 3 * h
        sid[b + 64:b + 96] = 3 * h + 1
        sid[b + 96:b + 128] = 3 * h + 2
        cnt[b:b + 64] = 1.0 / 64
        cnt[b + 64:b + 128] = 1.0 / 32
    seg = (sid[:, None] == sid[None, :]).astype(np.float32)
    return jnp.asarray(seg, BF16), jnp.asarray(cnt.reshape(1, 512))


def _mla_q_kernel(rope, cq_ref, ckv_ref, kr_ref, qng_ref, wuq_ref, kvg_ref, gq_ref, gkr_ref, seg_ref, cnt_ref,
                  *rest):
    if rope:
        cos_ref, sin_ref, q_ref, ckvn_ref, krp_ref = rest
    else:
        q_ref, ckvn_ref, krp_ref = rest
    cqn = _rms(cq_ref[...], qng_ref[...])
    q = _bdot(cqn, wuq_ref[...])
    ss = _segsum(q * q, seg_ref[...]) * cnt_ref[...]
    qn = q * lax.rsqrt(ss + EPS) * gq_ref[...]
    ckvn_ref[...] = _rms(ckv_ref[...], kvg_ref[...])
    kr = kr_ref[...]
    krn = kr * lax.rsqrt(jnp.sum(kr * kr, axis=-1, keepdims=True) * (1.0 / MLA_ROPE) + EPS) * gkr_ref[...]
    if rope:
        cos, sin = cos_ref[...], sin_ref[...]
        qn = qn * _tile_lanes(cos, MLA_HEADS) + _swap8(qn) * _tile_lanes(sin, MLA_HEADS)
        krn = krn * cos + _swap8(krn) * sin
    q_ref[...] = (qn * (MLA_SCALE * LOG2E)).astype(BF16)
    krp_ref[...] = krn


def _mla_q(p, lw, consts, seq_len, rope_tabs):
    n = p.shape[0]
    tm = min(512, seq_len)
    rope = rope_tabs is not None
    full = lambda shape: pl.BlockSpec(shape, lambda i: (0,) * len(shape))
    in_specs = [pl.BlockSpec((tm, 256), lambda i: (i, P_CQ // 256)),
                pl.BlockSpec((tm, 128), lambda i: (i, P_CKV // 128)),
                pl.BlockSpec((tm, 128), lambda i: (i, P_KR // 128)),
                full((1, 256)), full((256, 512)), full((1, 128)), full((1, 512)), full((1, 128)),
                full((512, 512)), full((1, 512))]
    args = [p, p, p, lw["qn_g"], lw["w_uq"], lw["kvn_g"], lw["gq"], lw["gkr"], consts["seg512"], consts["cnt512"]]
    if rope:
        tps = seq_len // tm
        in_specs += [pl.BlockSpec((tm, 128), lambda i: (i % tps, 0))] * 2
        args += [rope_tabs["cos_mla"], rope_tabs["sin_mla"]]
    return pl.pallas_call(
        functools.partial(_mla_q_kernel, rope),
        grid=(n // tm,),
        in_specs=in_specs,
        out_specs=[pl.BlockSpec((tm, 512), lambda i: (i, 0)),
                   pl.BlockSpec((tm, 128), lambda i: (i, 0)),
                   pl.BlockSpec((tm, 128), lambda i: (i, 0))],
        out_shape=[jax.ShapeDtypeStruct((n, 512), BF16),
                   jax.ShapeDtypeStruct((n, 128), F32),
                   jax.ShapeDtypeStruct((n, 128), F32)],
        compiler_params=_cparams(("parallel",)),
        name="mla_q",
    )(*args)


def _store_vt(o_ref, vt):
    tm = vt.shape[1]
    row = lax.broadcasted_iota(jnp.int32, (VT_ROWS - 64, tm), 0)
    extra = jnp.where(row == 0, 1.0, 0.0).astype(BF16)
    for h in range(4):
        o_ref[0, h, 0:64, :] = vt[64 * h:64 * (h + 1)].astype(BF16)
        o_ref[0, h, 64:VT_ROWS, :] = extra


def _mla_kv_kernel(ckvn_ref, krp_ref, wuk_ref, wuv_ref, gk_ref, seg_ref, cnt_ref, k_ref, vt_ref):
    c = ckvn_ref[...].astype(BF16)
    kn = jnp.dot(c, wuk_ref[...], preferred_element_type=F32)
    ss = _segsum(kn * kn, seg_ref[...]) * cnt_ref[...]
    k = kn * lax.rsqrt(ss + EPS) * gk_ref[...] + _tile_lanes(krp_ref[...], MLA_HEADS)
    k_ref[...] = k.astype(BF16)
    _store_vt(vt_ref, _nt(wuv_ref[...], c))


def _mla_kv(ckvn, krp, lw, consts, batch):
    n = ckvn.shape[0]
    lseg = n // batch
    tm = min(512, lseg)
    tpb = lseg // tm
    full = lambda shape: pl.BlockSpec(shape, lambda i: (0,) * len(shape))
    return pl.pallas_call(
        _mla_kv_kernel,
        grid=(n // tm,),
        in_specs=[pl.BlockSpec((tm, 128), lambda i: (i, 0)), pl.BlockSpec((tm, 128), lambda i: (i, 0)),
                  full((128, 512)), full((256, 128)), full((1, 512)), full((512, 512)), full((1, 512))],
        out_specs=[pl.BlockSpec((tm, 512), lambda i: (i, 0)),
                   pl.BlockSpec((1, 4, VT_ROWS, tm), lambda i: (i // tpb, 0, 0, i % tpb))],
        out_shape=[jax.ShapeDtypeStruct((n, 512), BF16), jax.ShapeDtypeStruct((batch, 4, VT_ROWS, lseg), BF16)],
        compiler_params=_cparams(("parallel",)),
        name="mla_kv",
    )(ckvn, krp, lw["w_uk"], lw["w_uv"], lw["gk"], consts["seg512"], consts["cnt512"])


def _softmax_pv(qs, k_refs, vt_refs, key_chunk, sub_rows, n_ahead, k_lane=0, v_head=0):
    where = [(i, lo) for i, r in enumerate(k_refs) for lo in range(0, r.shape[1], key_chunk)]
    nch = len(where)
    nq = len(qs)
    sub = min(sub_rows, key_chunk)
    nsub = key_chunk // sub

    def scores(c, u):
        seg, lo = where[c]
        ks = k_refs[seg][0, lo + u * sub:lo + (u + 1) * sub, k_lane:k_lane + LANE]
        return [_nt(ks, q) for q in qs]

    def chunk_max(s_chunk, j):
        mc = functools.reduce(jnp.maximum, [s_chunk[u][j] for u in range(nsub)])
        return jnp.max(mc, axis=0, keepdims=True)

    s_buf = {c: [scores(c, u) for u in range(nsub)] for c in range(min(n_ahead, nch))}
    m = [None] * nq
    acc = [None] * nq
    m_new = [chunk_max(s_buf[0], j) for j in range(nq)]
    for c in range(nch):
        s_cur = s_buf.pop(c)
        ahead = c + n_ahead
        if ahead < nch:
            s_buf[ahead] = []
        pv = [None] * nq
        for u in range(nsub):
            if ahead < nch:
                s_buf[ahead].append(scores(ahead, u))
            seg, lo = where[c]
            vs = vt_refs[seg][0, v_head, :, lo + u * sub:lo + (u + 1) * sub]
            for j in range(nq):
                part = jnp.dot(vs, jnp.exp2(s_cur[u][j] - m_new[j]).astype(BF16), preferred_element_type=F32)
                pv[j] = part if pv[j] is None else pv[j] + part
        for j in range(nq):
            acc[j] = pv[j] if c == 0 else acc[j] * jnp.exp2(m[j] - m_new[j]) + pv[j]
            m[j] = m_new[j]
        if c + 1 < nch:
            m_new = [jnp.maximum(m[j], chunk_max(s_buf[c + 1], j)) for j in range(nq)]
    return acc


def _mla_attn_kernel(key_chunk, nseg, heads, q_ref, *refs):
    k_refs, vt_refs, o_ref = refs[:nseg], refs[nseg:2 * nseg], refs[2 * nseg]
    for hh in range(heads):
        q = q_ref[:, LANE * hh:LANE * (hh + 1)]
        (acc,) = _softmax_pv([q], k_refs, vt_refs, key_chunk, MLA_SUB, MLA_AHEAD, LANE * hh, hh)
        o_ref[0, MLA_V * hh:MLA_V * (hh + 1)] = acc[0:MLA_V] / acc[MLA_V:MLA_V + 1]


def _key_chunk(lk, rows=512):
    return rows if lk % rows == 0 else lk


def _heads_per_step(ks):
    return 4 if sum(k.shape[1] for k in ks) <= 512 else 1


def _vt_kernel(v_ref, o_ref):
    _store_vt(o_ref, v_ref[0].astype(F32).T)


def _vt_with_ones(v3):
    b, lk, width = v3.shape
    tm = 1536 if lk % 1536 == 0 else lk
    return pl.pallas_call(
        _vt_kernel,
        grid=(b, lk // tm),
        in_specs=[pl.BlockSpec((1, tm, width), lambda i, j: (i, j, 0))],
        out_specs=pl.BlockSpec((1, 4, VT_ROWS, tm), lambda i, j: (i, 0, 0, j)),
        out_shape=jax.ShapeDtypeStruct((b, 4, VT_ROWS, lk), BF16),
        compiler_params=_cparams(("parallel", "parallel")),
        name="vt_ones",
    )(v3)


def _mla_attn(q, ks, vts, batch, seq_len):
    tq = min(512, seq_len)
    nq = seq_len // tq
    chunk = _key_chunk(min(k.shape[1] for k in ks), MLA_CHUNK)
    hp = _heads_per_step(ks)
    k_specs = [pl.BlockSpec((1, k.shape[1], LANE * hp), lambda b, h, i: (b, 0, h)) for k in ks]
    v_specs = [pl.BlockSpec((1, hp, VT_ROWS, v.shape[3]), lambda b, h, i: (b, h, 0, 0)) for v in vts]
    return pl.pallas_call(
        functools.partial(_mla_attn_kernel, chunk, len(ks), hp),
        grid=(batch, MLA_HEADS // hp, nq),
        in_specs=[pl.BlockSpec((tq, LANE * hp), lambda b, h, i: (b * nq + i, h))] + k_specs + v_specs,
        out_specs=pl.BlockSpec((1, MLA_V * hp, tq), lambda b, h, i: (b, h, i)),
        out_shape=jax.ShapeDtypeStruct((batch, BRANCH, seq_len), F32),
        compiler_params=_cparams(("parallel", "parallel", "arbitrary")),
        name="mla_attn",
    )(q, *ks, *vts)


def _seg_const(width, seg):
    sid = np.arange(width) // seg
    return jnp.asarray((sid[:, None] == sid[None, :]).astype(np.float32), BF16)


def _diff_prep_kernel(rope, dq_ref, dk_ref, dv_ref, gq_ref, gk_ref, seg_ref, *rest):
    if rope:
        cos_ref, sin_ref, q_ref, k_ref, kf_ref, vt_ref = rest
    else:
        q_ref, k_ref, kf_ref, vt_ref = rest
    seg = seg_ref[...]
    _store_vt(vt_ref, dv_ref[...].T)

    def norm(x, g):
        ss = _segsum(x * x, seg) * (1.0 / DIFF_HD)
        return x * lax.rsqrt(ss + EPS) * g

    q = norm(dq_ref[...], gq_ref[...])
    k = norm(dk_ref[...], gk_ref[...])
    kf_ref[...] = k
    if rope:
        cos, sin = cos_ref[...], sin_ref[...]
        q = q * cos + _swap8(q) * sin
        k = k * cos + _swap8(k) * sin
    q_ref[...] = (q * (DIFF_SCALE * LOG2E)).astype(BF16)
    k_ref[...] = k.astype(BF16)


def _diff_prep(p, lw, consts, seq_len, rope_tabs):
    n = p.shape[0]
    tm = min(512, seq_len)
    rope = rope_tabs is not None
    full = lambda shape: pl.BlockSpec(shape, lambda i: (0,) * len(shape))
    in_specs = [pl.BlockSpec((tm, 256), lambda i: (i, P_DQ // 256)),
                pl.BlockSpec((tm, 256), lambda i: (i, P_DK // 256)),
                pl.BlockSpec((tm, 256), lambda i: (i, P_DV // 256)),
                full((1, 256)), full((1, 256)), full((256, 256))]
    args = [p, p, p, lw["dgq"], lw["dgk"], consts["seg32"]]
    tps = seq_len // tm
    if rope:
        in_specs += [pl.BlockSpec((tm, 256), lambda i: (i % tps, 0))] * 2
        args += [rope_tabs["cos_diff"], rope_tabs["sin_diff"]]
    blk = pl.BlockSpec((tm, 256), lambda i: (i, 0))
    return pl.pallas_call(
        functools.partial(_diff_prep_kernel, rope),
        grid=(n // tm,),
        in_specs=in_specs,
        out_specs=[blk, blk, blk, pl.BlockSpec((1, 4, VT_ROWS, tm), lambda i: (i // tps, 0, 0, i % tps))],
        out_shape=[jax.ShapeDtypeStruct((n, 256), BF16), jax.ShapeDtypeStruct((n, 256), BF16),
                   jax.ShapeDtypeStruct((n, 256), F32),
                   jax.ShapeDtypeStruct((n // seq_len, 4, VT_ROWS, seq_len), BF16)],
        compiler_params=_cparams(("parallel",)),
        name="diff_prep",
    )(*args)


def _diff_attn_kernel(lam_init, key_chunk, nseg, heads, q_ref, *refs):
    k_refs, vt_refs = refs[:nseg], refs[nseg:2 * nseg]
    lp_ref, g_ref, o_ref = refs[2 * nseg:]
    lp = lp_ref[...]
    lam = (jnp.exp(jnp.sum(lp[0:1] * lp[1:2], axis=1, keepdims=True))
           - jnp.exp(jnp.sum(lp[2:3] * lp[3:4], axis=1, keepdims=True)) + lam_init)
    for hh in range(heads):
        blk = 0 if heads == 1 else LANE * (hh // 2)
        base = (pl.program_id(1) % 2) * 64 if heads == 1 else (hh % 2) * 64
        q = q_ref[:, blk:blk + LANE]
        lane = lax.broadcasted_iota(jnp.int32, q.shape, 1)
        zero = jnp.zeros_like(q)

        def map_query(j):
            lo = base + 32 * j
            return jnp.where((lane >= lo) & (lane < lo + 32), q, zero)

        acc0, acc1 = _softmax_pv([map_query(0), map_query(1)], k_refs, vt_refs, key_chunk, DIFF_SUB, DIFF_AHEAD,
                                 blk, hh)
        o = acc0[0:64] / acc0[64:65] - lam * (acc1[0:64] / acc1[64:65])
        ms = jnp.mean(o * o, axis=0, keepdims=True)
        o_ref[0, 64 * hh:64 * (hh + 1)] = o * lax.rsqrt(ms + EPS) * g_ref[...] * (1.0 - lam_init)


def _diff_attn(q, ks, vts, lp, g_col, lam_init, batch, seq_len):
    tq = min(512, seq_len)
    nq = seq_len // tq
    chunk = _key_chunk(min(k.shape[1] for k in ks), DIFF_CHUNK)
    hp = _heads_per_step(ks)
    lanes = LANE if hp == 1 else BRANCH
    k_specs = [pl.BlockSpec((1, k.shape[1], lanes), lambda b, h, i: (b, 0, h // 2)) for k in ks]
    v_specs = [pl.BlockSpec((1, hp, VT_ROWS, v.shape[3]), lambda b, h, i: (b, h, 0, 0)) for v in vts]
    return pl.pallas_call(
        functools.partial(_diff_attn_kernel, lam_init, chunk, len(ks), hp),
        grid=(batch, DIFF_HEADS // hp, nq),
        in_specs=[pl.BlockSpec((tq, lanes), lambda b, h, i: (b * nq + i, h // 2))] + k_specs + v_specs
        + [pl.BlockSpec((4, DIFF_HD), lambda b, h, i: (0, 0)), pl.BlockSpec((64, 1), lambda b, h, i: (0, 0))],
        out_specs=pl.BlockSpec((1, 64 * hp, tq), lambda b, h, i: (b, h, i)),
        out_shape=jax.ShapeDtypeStruct((batch, BRANCH, seq_len), F32),
        compiler_params=_cparams(("parallel", "parallel", "arbitrary")),
        name="diff_attn",
    )(q, *ks, *vts, lp, g_col)


def _hgrn_consts():
    c = HGRN_CHUNK
    t = np.arange(c)
    low = (t[None, :] <= t[:, None]).astype(np.float32)
    blocks = []
    for j in range(HGRN_MM_LEVELS):
        m = 1 << j
        rho = (t // (2 * m)) * (2 * m) + m - 1
        sign = np.where((t // m) % 2 == 1, 1.0, -1.0)[:, None]
        blocks.append(sign * (low - (t[None, :] <= rho[:, None]).astype(np.float32)))
    blocks.append(low)
    fwd = np.concatenate(blocks, axis=0)
    bwd = np.concatenate([b[::-1, ::-1] for b in blocks], axis=0)
    right = np.stack([(t // (1 << j)) % 2 for j in range(HGRN_LEVELS)]).astype(np.float32)
    right = np.stack([right, right[:, ::-1]])
    right = np.broadcast_to(right[..., None], right.shape + (BRANCH,))
    return jnp.asarray(np.stack([fwd, bwd]), BF16), jnp.asarray(right, F32)


def _hgrn_kernel(nc, nb, qf_ref, zf_ref, vf_ref, qb_ref, zb_ref, vb_ref, lb_ref, dd_ref, rm_ref, s0_ref,
                 of_ref, ob_ref, sout_ref, st_ref):
    c = HGRN_CHUNK
    ci = pl.program_id(1)
    chains = [(bi, d) for bi in range(nb) for d in (0, 1)]
    ids = range(len(chains))

    @pl.when(ci == 0)
    def _():
        st_ref[...] = s0_ref[...]

    lane = lax.broadcasted_iota(jnp.int32, (1, BRANCH), 1)
    head_masks = [(lane >= HGRN_DK * h) & (lane < HGRN_DK * (h + 1)) for h in range(HGRN_HEADS)]
    t_idx = lax.broadcasted_iota(jnp.int32, (c, HGRN_HEADS * c), 0)
    s_idx = lax.broadcasted_iota(jnp.int32, (c, HGRN_HEADS * c), 1) & (c - 1)
    pair_xor = t_idx ^ s_idx

    def stack_heads(x):
        xb = x.astype(BF16)
        zero = jnp.zeros_like(xb)
        return jnp.concatenate([jnp.where(hm, xb, zero) for hm in head_masks], axis=0)

    q_refs, z_refs, v_refs = (qf_ref, qb_ref), (zf_ref, zb_ref), (vf_ref, vb_ref)
    q = [q_refs[d][bi] for bi, d in chains]
    v = [v_refs[d][bi] for bi, d in chains]
    z = [z_refs[d][bi] for bi, d in chains]
    lb = [lb_ref[d] for _, d in chains]
    g = [jnp.log(lb[i] + (1.0 - lb[i]) * jax.nn.sigmoid(z[i])) for i in ids]
    kk = [(1.0 - lb[i]) * jax.nn.sigmoid(-z[i]) for i in ids]
    sums = []
    for i in ids:
        gh, gl = _split2(g[i])
        dd = dd_ref[chains[i][1]]
        sums.append(jnp.dot(dd, gh, preferred_element_type=F32) + jnp.dot(dd, gl, preferred_element_type=F32))
    b = [sums[i][HGRN_MM_LEVELS * c:] for i in ids]
    b_tot = [b[i][c - 1:c] if chains[i][1] == 0 else b[i][0:1] for i in ids]

    def neg_abs_decay(i, j):
        if j < HGRN_MM_LEVELS:
            return sums[i][j * c:(j + 1) * c]
        m = 1 << j
        off = m - 1 if chains[i][1] == 0 else m
        ref = jnp.concatenate([jnp.broadcast_to(b[i][g0 + off:g0 + off + 1], (2 * m, BRANCH))
                               for g0 in range(0, c, 2 * m)], axis=0)
        return -jnp.abs(b[i] - ref)

    a = [None] * len(chains)
    for j in reversed(range(HGRN_LEVELS)):
        same_group = pair_xor < (2 << j)
        for i in ids:
            e = jnp.exp(neg_abs_decay(i, j))
            eq = e * rm_ref[chains[i][1], j]
            qt = q[i] * eq
            kt = kk[i] * (e - eq)
            lvl = _nt(qt, stack_heads(kt))
            a[i] = lvl if a[i] is None else jnp.where(same_group, lvl, a[i])
    diagonal = pair_xor == 0
    for i in ids:
        a[i] = jnp.where(diagonal, _nt(q[i], stack_heads(kk[i])), a[i])

    outs = (of_ref, ob_ref)
    for i in ids:
        bi, d = chains[i]
        o = jnp.dot(a[i].astype(BF16), stack_heads(v[i]), preferred_element_type=F32)
        outs[d][bi] = o + _nt(q[i] * jnp.exp(b[i]), st_ref[bi, d])

    r2 = lax.broadcasted_iota(jnp.int32, (BRANCH, BRANCH), 0) // HGRN_DK
    c2 = lax.broadcasted_iota(jnp.int32, (BRANCH, BRANCH), 1) // HGRN_DK
    for i in ids:
        bi, d = chains[i]
        kd = kk[i] * jnp.exp(b_tot[i] - b[i])
        upd = lax.dot_general(v[i].astype(BF16), kd.astype(BF16), (((0,), (0,)), ((), ())),
                              preferred_element_type=F32)
        st_new = st_ref[bi, d] * jnp.exp(b_tot[i]) + jnp.where(r2 == c2, upd, 0.0)
        st_ref[bi, d] = st_new

        @pl.when(ci == nc - 1)
        def _(bi=bi, d=d, st_new=st_new):
            sout_ref[bi, d] = st_new


def _hgrn(p, lb_l, dd, rm, st0, batch, seq_len):
    n = p.shape[0]
    c = HGRN_CHUNK
    nc = seq_len // c
    nb = HGRN_ROWS if batch % HGRN_ROWS == 0 else 1
    p3 = p.reshape(batch, seq_len, P_COLS)
    fwd = lambda col: pl.BlockSpec((nb, c, 256), lambda b, i: (b, i, col))
    bwd = lambda col: pl.BlockSpec((nb, c, 256), lambda b, i: (b, nc - 1 - i, col))
    whole = lambda shape: pl.BlockSpec(shape, lambda b, i: (0,) * len(shape))
    state = pl.BlockSpec((nb, 2, 256, 256), lambda b, i: (b, 0, 0, 0))
    o_f, o_b, st = pl.pallas_call(
        functools.partial(_hgrn_kernel, nc, nb),
        grid=(batch // nb, nc),
        in_specs=[fwd(P_HQ // 256), fwd(P_HZF // 256), fwd(P_HI // 256),
                  bwd(P_HQ // 256), bwd(P_HZB // 256), bwd(P_HI // 256),
                  whole((2, 1, 256)), whole((2, (HGRN_MM_LEVELS + 1) * c, c)),
                  whole((2, HGRN_LEVELS, c, 256)), state],
        out_specs=[pl.BlockSpec((nb, c, 256), lambda b, i: (b, i, 0)),
                   pl.BlockSpec((nb, c, 256), lambda b, i: (b, nc - 1 - i, 0)), state],
        out_shape=[jax.ShapeDtypeStruct((batch, seq_len, 256), F32),
                   jax.ShapeDtypeStruct((batch, seq_len, 256), F32),
                   jax.ShapeDtypeStruct((batch, 2, 256, 256), F32)],
        scratch_shapes=[pltpu.VMEM((nb, 2, 256, 256), F32)],
        compiler_params=_cparams(("parallel", "arbitrary")),
        name="hgrn",
    )(p3, p3, p3, p3, p3, p3, lb_l, dd, rm, st0)
    return o_f.reshape(n, 256), o_b.reshape(n, 256), st


def _hy_conv3_kernel(tiles_per_seq, above_ref, cur_ref, below_ref, w_ref, b_ref, v_ref, x1_ref, x2_ref):
    i = pl.program_id(0)
    cur = cur_ref[...]
    tm = cur.shape[0]
    first = (i % tiles_per_seq) == 0
    last = (i % tiles_per_seq) == tiles_per_seq - 1
    above = jnp.where(first, 0.0, above_ref[7:8, :])
    below = jnp.where(last, 0.0, below_ref[0:1, :])
    row = lax.broadcasted_iota(jnp.int32, (tm, 1), 0)
    prev = jnp.where(row == 0, above, pltpu.roll(cur, 1, 0))
    nxt = jnp.where(row == tm - 1, below, pltpu.roll(cur, tm - 1, 0))
    w = w_ref[...]
    u = prev * w[0:1] + cur * w[1:2] + nxt * w[2:3] + b_ref[...]
    v_ref[...] = u[:, 0:256]
    x1_ref[...] = u[:, 256:512]
    x2_ref[...] = u[:, 512:768]


def _hy_conv3(p, w, b, seq_len):
    n = p.shape[0]
    tm = min(512, seq_len)
    nt = n // tm
    g = tm // 8
    col = P_HU // 768
    oblk = pl.BlockSpec((tm, 256), lambda i: (i, 0))
    return pl.pallas_call(
        functools.partial(_hy_conv3_kernel, seq_len // tm),
        grid=(nt,),
        in_specs=[pl.BlockSpec((8, 768), lambda i: (jnp.maximum(i * g - 1, 0), col)),
                  pl.BlockSpec((tm, 768), lambda i: (i, col)),
                  pl.BlockSpec((8, 768), lambda i: (jnp.minimum((i + 1) * g, nt * g - 1), col)),
                  pl.BlockSpec((3, 768), lambda i: (0, 0)), pl.BlockSpec((1, 768), lambda i: (0, 0))],
        out_specs=[oblk, oblk, oblk],
        out_shape=[jax.ShapeDtypeStruct((n, 256), F32)] * 3,
        compiler_params=_cparams(("parallel",)),
        name="hy_conv3",
    )(p, p, p, w, b)


def _hy_filter_kernel(feat_ref, w1_ref, b1_ref, w2_ref, b2_ref, w3_ref, fr_ref, win_ref, o_ref):
    fr = fr_ref[...]
    h = jnp.sin(fr[0:1] * (_dot3(feat_ref[...], w1_ref[...]) + b1_ref[...]))
    h = jnp.sin(fr[1:2] * (_dot3(h, w2_ref[...]) + b2_ref[...]))
    o_ref[...] = _dot3(h, w3_ref[0]) * _tile_lanes(win_ref[...], HY_ORDER)


def _hy_filter(feats2, window2, w1p, b1, w2, b2, w3d, freq):
    l2 = feats2.shape[0]
    ln = l2 // 2
    tm = min(512, ln)
    full = lambda shape: pl.BlockSpec(shape, lambda i: (0,) * len(shape))
    return pl.pallas_call(
        _hy_filter_kernel,
        grid=(l2 // tm,),
        in_specs=[pl.BlockSpec((tm, LANE), lambda i: (i, 0)),
                  full((LANE, HY_FH)), full((1, HY_FH)), full((HY_FH, HY_FH)), full((1, HY_FH)),
                  pl.BlockSpec((1, HY_FH, HY_ORDER * HY_CH), lambda i: (i // (ln // tm), 0, 0)), full((2, HY_FH)),
                  pl.BlockSpec((tm, HY_CH), lambda i: (i, 0))],
        out_specs=pl.BlockSpec((tm, HY_ORDER * HY_CH), lambda i: (i, 0)),
        out_shape=jax.ShapeDtypeStruct((l2, HY_ORDER * HY_CH), F32),
        compiler_params=_cparams(("parallel",)),
        name="hy_filter",
    )(feats2, w1p, b1, w2, b2, w3d, freq, window2)


def _fft_blocking(nb, n1, n2, ch):
    per_batch = n1 * n2 * ch * 4
    if per_batch <= FFT_BLOCK_BYTES:
        bb = max(1, min(nb, FFT_BLOCK_BYTES // per_batch))
        while nb % bb:
            bb -= 1
        return bb, n2
    rt = n2
    while n1 * rt * ch * 4 > FFT_BLOCK_BYTES and rt > 8:
        rt //= 2
    return 1, rt


def _fft_a_kernel(f_ref, x_ref, o_ref):
    f = f_ref[...]
    for b in range(x_ref.shape[0]):
        x = x_ref[b].astype(BF16)
        o_ref[b] = jnp.einsum("kn,nrc->krc", f, x, preferred_element_type=F32).astype(o_ref.dtype)


def _fft_a(fa, x4, out_dtype):
    nb, n1, n2, ch = x4.shape
    r = fa.shape[0]
    bb, rt = _fft_blocking(nb, n1, n2, ch)
    return pl.pallas_call(
        _fft_a_kernel,
        grid=(nb // bb, n2 // rt),
        in_specs=[pl.BlockSpec((r, n1), lambda b, i: (0, 0)),
                  pl.BlockSpec((bb, n1, rt, ch), lambda b, i: (b, 0, i, 0))],
        out_specs=pl.BlockSpec((bb, r, rt, ch), lambda b, i: (b, 0, i, 0)),
        out_shape=jax.ShapeDtypeStruct((nb, r, n2, ch), out_dtype),
        compiler_params=_cparams(("parallel", "parallel")),
        name="fft_a",
    )(fa, x4)


def _fft_b_kernel(with_inverse, mf_ref, *rest):
    if with_inverse:
        mi_ref, a_ref, h_ref, o_ref = rest
    else:
        a_ref, o_ref = rest
    half = FFT_N2
    for kk in range(a_ref.shape[2]):
        for b in range(a_ref.shape[0]):
            a = jnp.concatenate([a_ref[b, 0, kk], a_ref[b, 1, kk]], axis=0)
            x = jnp.dot(mf_ref[kk], a.astype(BF16), preferred_element_type=F32)
            if with_inverse:
                xr, xi = x[:half], x[half:]
                hr, hi = h_ref[0, kk], h_ref[1, kk]
                y = jnp.concatenate([xr * hr - xi * hi, xr * hi + xi * hr], axis=0)
                x = jnp.dot(mi_ref[kk], y.astype(BF16), preferred_element_type=F32)
            o_ref[b, 0, kk] = x[:half].astype(o_ref.dtype)
            o_ref[b, 1, kk] = x[half:].astype(o_ref.dtype)


def _fft_b(mf, mi, a5, spec, order):
    nb, _, k1n, n2, ch = a5.shape
    ks = 3 if (k1n % 3 == 0 and nb * ch <= 1024) else 1
    mat = pl.BlockSpec((ks, 2 * n2, 2 * n2), lambda k: (k, 0, 0))
    blk = pl.BlockSpec((nb, 2, ks, n2, ch), lambda k: (0, 0, k, 0, 0))
    if spec is None:
        in_specs, args = [mat, blk], [mf, a5]
    else:
        in_specs = [mat, mat, blk, pl.BlockSpec((2, ks, n2, ch), lambda k: (0, k, 0, order))]
        args = [mf, mi, a5, spec]
    return pl.pallas_call(
        functools.partial(_fft_b_kernel, spec is not None),
        grid=(k1n // ks,),
        in_specs=in_specs,
        out_specs=blk,
        out_shape=jax.ShapeDtypeStruct(a5.shape, F32 if spec is None else BF16),
        compiler_params=_cparams(("parallel",)),
        name="fft_b",
    )(*args)


def _fft_a_inv_kernel(g_ref, p_ref, x_ref, z_ref, bias_ref, o_ref):
    g = g_ref[...]
    for b in range(p_ref.shape[0]):
        conv = jnp.einsum("nk,krc->nrc", g, p_ref[b], preferred_element_type=F32)
        o_ref[b] = x_ref[b] * (conv + z_ref[b] * bias_ref[...])


def _fft_a_inv(g, p4, xg4, z4, bias):
    nb, n1, n2, ch = z4.shape
    r = g.shape[1]
    bb, rt = _fft_blocking(nb, n1, n2, ch)
    blk = pl.BlockSpec((bb, n1, rt, ch), lambda b, i: (b, 0, i, 0))
    return pl.pallas_call(
        _fft_a_inv_kernel,
        grid=(nb // bb, n2 // rt),
        in_specs=[pl.BlockSpec((n1, r), lambda b, i: (0, 0)),
                  pl.BlockSpec((bb, r, rt, ch), lambda b, i: (b, 0, i, 0)), blk, blk,
                  pl.BlockSpec((1, 1, ch), lambda b, i: (0, 0, 0))],
        out_specs=blk,
        out_shape=jax.ShapeDtypeStruct(z4.shape, F32),
        compiler_params=_cparams(("parallel", "parallel")),
        name="fft_a_inv",
    )(g, p4, xg4, z4, bias.reshape(1, 1, ch))


def _fft_tables(ln):
    n = 2 * ln
    n1t = n // FFT_N2
    k1n = n1t // 2 + 1
    kk = np.arange(k1n)

    def stage_a(n1_in):
        ang = 2.0 * np.pi * ((kk[:, None] * np.arange(n1_in)[None, :]) % n1t) / n1t
        return jnp.asarray(np.concatenate([np.cos(ang), -np.sin(ang)], axis=0), BF16)

    n1o = n1t // 2
    ang = 2.0 * np.pi * ((np.arange(n1o)[:, None] * kk[None, :]) % n1t) / n1t
    edge = (kk == 0) | (kk == n1t // 2)
    ck = np.where(edge, 1.0, 2.0) / n
    g = jnp.asarray(np.concatenate([ck * np.cos(ang), -ck * np.where(edge, 0.0, np.sin(ang))], axis=1), BF16)

    k1 = jnp.arange(k1n, dtype=jnp.int32)[:, None, None]
    k2 = jnp.arange(FFT_N2, dtype=jnp.int32)[None, :, None]
    n2 = jnp.arange(FFT_N2, dtype=jnp.int32)[None, None, :]
    th = (2.0 * math.pi / n) * ((n2 * (k1 + n1t * k2)) % n).astype(F32)
    c, s = jnp.cos(th), jnp.sin(th)
    mf = jnp.concatenate([jnp.concatenate([c, s], axis=2), jnp.concatenate([-s, c], axis=2)], axis=1)
    return dict(fa_half=stage_a(n1o), fa_full=stage_a(n1t), g=g, mf=mf.astype(BF16),
                mi=mf.transpose(0, 2, 1).astype(BF16), k1n=k1n, n1o=n1o, n1t=n1t)


def _hy_static(ln):
    t = jnp.linspace(0.0, 1.0, ln, dtype=F32)[:, None]
    w = 2.0 * math.pi * jnp.arange(ln, dtype=F32) / ln
    f = jnp.linspace(1e-4, HY_BANDS - 1, HY_BANDS, dtype=F32)
    ang = w[:, None] * f[None, :]
    feats = jnp.concatenate([t, jnp.cos(ang), -jnp.sin(ang)], axis=-1)
    feats = jnp.pad(feats, ((0, 0), (0, LANE - HY_EMB)))
    min_decay = math.log(HY_DECAY_TARGET) / HY_SLOW_DECAY
    max_decay = math.log(HY_DECAY_TARGET) / HY_FAST_DECAY
    deltas = jnp.linspace(min_decay, max_decay, HY_CH, dtype=F32)
    window = jnp.exp(-t * jnp.abs(deltas))
    feats = jnp.concatenate([feats, feats[::-1]], axis=0)
    window = jnp.concatenate([window, window[::-1]], axis=0)
    return feats, window


def _outproj_kernel(x_ref, mod_ref, oa_ref, ob_ref, of_ref, obk_ref, od_ref, gate_ref, hg_ref, seg_ref, w_ref,
                    y_ref):
    gt = gate_ref[...]
    sg = gt * jax.nn.sigmoid(gt)
    oc = of_ref[...] + obk_ref[...]
    ss = _segsum(oc * oc, seg_ref[...]) * (1.0 / HGRN_DK)
    oc = oc * lax.rsqrt(ss + EPS) * hg_ref[...]
    acc = _bdot(oa_ref[0].T * sg[:, 0:256], w_ref[0:256, :])
    acc += _bdot(ob_ref[0].T * sg[:, 256:512], w_ref[256:512, :])
    acc += _bdot(oc * sg[:, 512:768], w_ref[512:768, :])
    acc += _bdot(od_ref[...] * sg[:, 768:1024], w_ref[768:1024, :])
    y_ref[...] = x_ref[...] + mod_ref[0, 2:3, :] * acc


def _outproj(x2, mod, ot_a, ot_b, o_f, o_b, out_d, p, hg, seg64, w_out, seq_len):
    n = x2.shape[0]
    tm = min(512, seq_len)
    per_batch = mod.shape[0] > 1
    tps = seq_len // tm
    mod_idx = (lambda i: (i // tps, 0, 0)) if per_batch else (lambda i: (0, 0, 0))
    b256 = pl.BlockSpec((tm, 256), lambda i: (i, 0))
    bt = pl.BlockSpec((1, 256, tm), lambda i: (i // tps, 0, i % tps))
    return pl.pallas_call(
        _outproj_kernel,
        grid=(n // tm,),
        in_specs=[pl.BlockSpec((tm, D_MODEL), lambda i: (i, 0)),
                  pl.BlockSpec((1, 3, D_MODEL), mod_idx),
                  bt, bt, b256, b256, b256,
                  pl.BlockSpec((tm, 1024), lambda i: (i, P_GATE // 1024)),
                  pl.BlockSpec((1, 256), lambda i: (0, 0)),
                  pl.BlockSpec((256, 256), lambda i: (0, 0)),
                  pl.BlockSpec((D_MODEL, D_MODEL), lambda i: (0, 0))],
        out_specs=pl.BlockSpec((tm, D_MODEL), lambda i: (i, 0)),
        out_shape=jax.ShapeDtypeStruct((n, D_MODEL), F32),
        compiler_params=_cparams(("parallel",)),
        name="outproj",
    )(x2, mod, ot_a, ot_b, o_f, o_b, out_d, p, hg, seg64, w_out)


def _layer(x2, mod, lw, consts, batch, seq_len, ctx, rope_tabs, hy):
    n = batch * seq_len
    p = _inproj(x2, mod, lw["norm_g"], lw["w_in"], seq_len)

    q_a, ckvn, krp = _mla_q(p, lw, consts, seq_len, rope_tabs)
    ckv3 = ckvn.reshape(batch, seq_len, MLA_KV_LORA)
    krp3 = krp.reshape(batch, seq_len, LANE)
    k_a, vt_a = _mla_kv(ckvn, krp, lw, consts, batch)
    ks_a, vts_a = [k_a.reshape(batch, seq_len, 512)], [vt_a]
    if ctx is not None:
        lc = ctx[0].shape[1]
        k_c, vt_c = _mla_kv(ctx[0].reshape(batch * lc, MLA_KV_LORA), ctx[1].reshape(batch * lc, LANE), lw, consts,
                            batch)
        ks_a.append(k_c.reshape(batch, lc, 512))
        vts_a.append(vt_c)
    ot_a = _mla_attn(q_a, ks_a, vts_a, batch, seq_len)

    q_b, k_b, kd, vt_b = _diff_prep(p, lw, consts, seq_len, rope_tabs)
    ks_b, vts_b = [k_b.reshape(batch, seq_len, BRANCH)], [vt_b]
    if ctx is not None:
        ks_b.append(ctx[2])
        vts_b.append(ctx[3])
    ot_b = _diff_attn(q_b, ks_b, vts_b, lw["diff_lambda"], lw["subln_col"], lw["lam_init"], batch, seq_len)

    if ctx is not None:
        s0 = ctx[4]
    else:
        s0 = jnp.zeros((batch, 2, HGRN_HEADS, HGRN_DK, HGRN_DK), F32)
    eye = jnp.eye(HGRN_HEADS, dtype=F32)
    st0 = jnp.einsum("bdhke,hg->bdhegk", s0, eye).reshape(batch, 2, BRANCH, BRANCH)
    o_f, o_b, st_out = _hgrn(p, lw["hgrn_lb"], consts["hgrn_dd"], consts["hgrn_right"], st0, batch, seq_len)
    st5 = st_out.reshape(batch, 2, HGRN_HEADS, HGRN_DK, HGRN_HEADS, HGRN_DK)
    states = jnp.stack([st5[:, :, h, :, h, :] for h in range(HGRN_HEADS)], axis=2).swapaxes(-1, -2)

    v_d, x1, x2g = _hy_conv3(p, lw["hy_conv_w"], lw["hy_conv_b"], seq_len)
    taps = _hy_filter(hy["feats"], hy["window"], lw["hy_w1"], lw["hy_b1"], lw["hy_w2"], lw["hy_b2"], lw["hy_w3"],
                      lw["hy_freq"])
    k1n, n1o, n1t = hy["k1n"], hy["n1o"], hy["n1t"]
    ta = _fft_a(hy["fa_full"], taps.reshape(1, n1t, FFT_N2, HY_ORDER * HY_CH), BF16)
    spec = _fft_b(hy["mf"], None, ta.reshape(1, 2, k1n, FFT_N2, HY_ORDER * HY_CH), None, 0)[0]
    z4 = v_d.reshape(batch, n1o, FFT_N2, HY_CH)
    for o, xg in enumerate((x1, x2g)):
        a = _fft_a(hy["fa_half"], z4, BF16).reshape(batch, 2, k1n, FFT_N2, HY_CH)
        pk = _fft_b(hy["mf"], hy["mi"], a, spec, o).reshape(batch, 2 * k1n, FFT_N2, HY_CH)
        z4 = _fft_a_inv(hy["g"], pk, xg.reshape(batch, n1o, FFT_N2, HY_CH), z4, lw["hy_bias"][o:o + 1])
    out_d = z4.reshape(n, HY_CH)

    y = _outproj(x2, mod, ot_a, ot_b, o_f, o_b, out_d, p, lw["hgrn_out_g"], consts["seg64"], lw["w_out"], seq_len)
    new = None
    if ctx is None:
        new = (ckv3, krp3[:, :, KR_OFF:KR_OFF + MLA_ROPE],
               kd.reshape(batch, seq_len, DIFF_HEADS, 2, DIFF_HD),
               p[:, P_DV:P_DV + BRANCH].reshape(batch, seq_len, DIFF_HEADS, 2 * DIFF_HD), states)
    return y, new


def _rope_tables(seq_len):
    half = MLA_ROPE // 2
    inv = ROPE_BASE ** (-jnp.arange(0, half, 2, dtype=F32) / half)
    rows = seq_len // GRID_W
    row = jnp.repeat(jnp.arange(rows, dtype=F32), GRID_W)
    col = (jnp.arange(rows * GRID_W) % GRID_W).astype(F32)
    ar, ac = row[:, None] * inv, col[:, None] * inv
    cos32 = jnp.concatenate([jnp.cos(ar), jnp.cos(ar), jnp.cos(ac), jnp.cos(ac)], axis=-1)
    sin32 = jnp.concatenate([-jnp.sin(ar), jnp.sin(ar), -jnp.sin(ac), jnp.sin(ac)], axis=-1)
    pad = ((0, 0), (KR_OFF, LANE - KR_OFF - MLA_ROPE))
    return dict(cos_mla=jnp.pad(cos32, pad, constant_values=1.0), sin_mla=jnp.pad(sin32, pad),
                cos_diff=jnp.tile(cos32, (1, 2 * DIFF_HEADS)), sin_diff=jnp.tile(sin32, (1, 2 * DIFF_HEADS)))


def _hy_tables(seq_len):
    feats, window = _hy_static(seq_len)
    return dict(feats=feats, window=window, **_fft_tables(seq_len))


def _layer_weights(l, w_in_p, lb, W):
    def head_pad(w, width, per):
        k = w.shape[0]
        w = w.reshape(k, MLA_HEADS, per)[:, :, :width]
        return jnp.pad(w, ((0, 0), (0, 0), (0, LANE - width))).reshape(k, MLA_HEADS * LANE)

    w_ukv = W["mla_w_ukv"][l].reshape(MLA_KV_LORA, MLA_HEADS, MLA_NOPE + MLA_V)
    nope_g, rope_g = W["mla_nope_g"][l], W["mla_rope_g"][l]
    zeros32 = jnp.zeros((MLA_ROPE,), F32)
    zeros64 = jnp.zeros((MLA_NOPE,), F32)
    gq = jnp.tile(jnp.concatenate([nope_g[0], rope_g[0], zeros32]), MLA_HEADS).reshape(1, 512)
    gk = jnp.tile(jnp.concatenate([nope_g[1], zeros64]), MLA_HEADS).reshape(1, 512)
    gkr = jnp.concatenate([zeros64, rope_g[1], zeros32]).reshape(1, LANE)
    return dict(
        norm_g=W["norm_g"][l], w_in=w_in_p[l], w_out=W["w_out"][l].astype(BF16),
        qn_g=W["mla_q_norm_g"][l].reshape(1, -1),
        w_uq=head_pad(W["mla_w_uq"][l], MLA_NOPE + MLA_ROPE, MLA_NOPE + MLA_ROPE).astype(BF16),
        kvn_g=W["mla_kv_norm_g"][l].reshape(1, -1),
        w_uk=jnp.pad(w_ukv[:, :, :MLA_NOPE], ((0, 0), (0, 0), (0, LANE - MLA_NOPE))).reshape(MLA_KV_LORA, 512)
        .astype(BF16),
        w_uv=w_ukv[:, :, MLA_NOPE:].reshape(MLA_KV_LORA, BRANCH).T.astype(BF16),
        gq=gq, gk=gk, gkr=gkr,
        dgq=jnp.tile(W["diff_qk_g"][l, 0], 2 * DIFF_HEADS).reshape(1, BRANCH),
        dgk=jnp.tile(W["diff_qk_g"][l, 1], 2 * DIFF_HEADS).reshape(1, BRANCH),
        diff_lambda=W["diff_lambda"][l], subln_col=W["diff_subln_g"][l].reshape(2 * DIFF_HD, 1),
        lam_init=0.8 - 0.6 * math.exp(-0.3 * l),
        hgrn_lb=lb[:, l].reshape(2, 1, BRANCH),
        hgrn_out_g=jnp.tile(W["hgrn_out_g"][l], HGRN_HEADS).reshape(1, BRANCH),
        hy_conv_w=W["hy_conv_w"][l], hy_conv_b=W["hy_conv_b"][l].reshape(1, -1),
        hy_w1=jnp.pad(W["hy_w1"][l], ((0, LANE - HY_EMB), (0, 0))), hy_b1=W["hy_b1"][l].reshape(1, -1),
        hy_w2=W["hy_w2"][l], hy_b2=W["hy_b2"][l].reshape(1, -1),
        hy_w3=W["hy_w3"][l].reshape(HY_FH, HY_ORDER, 2, HY_CH).transpose(2, 0, 1, 3)
        .reshape(2, HY_FH, HY_ORDER * HY_CH),
        hy_freq=W["hy_sin_freq"][l], hy_bias=W["hy_bias"][l],
    )


def kernel(x_prompt, x_sample, cache_mla_ckv, cache_mla_krope, cache_diff_k, cache_diff_v, state_hgrn, c, c_ctx,
           norm_g, w_mod, b_mod, w_in, w_out, mla_q_norm_g, mla_w_uq, mla_kv_norm_g, mla_w_ukv, mla_nope_g,
           mla_rope_g, diff_qk_g, diff_lambda, diff_subln_g, hgrn_lb_logits, hgrn_out_g, hy_conv_w, hy_conv_b,
           hy_w1, hy_b1, hy_w2, hy_b2, hy_w3, hy_sin_freq, hy_bias):
    W = dict(norm_g=norm_g, w_out=w_out, mla_q_norm_g=mla_q_norm_g, mla_w_uq=mla_w_uq,
             mla_kv_norm_g=mla_kv_norm_g, mla_w_ukv=mla_w_ukv, mla_nope_g=mla_nope_g, mla_rope_g=mla_rope_g,
             diff_qk_g=diff_qk_g, diff_lambda=diff_lambda, diff_subln_g=diff_subln_g, hgrn_out_g=hgrn_out_g,
             hy_conv_w=hy_conv_w, hy_conv_b=hy_conv_b, hy_w1=hy_w1, hy_b1=hy_b1, hy_w2=hy_w2, hy_b2=hy_b2,
             hy_w3=hy_w3, hy_sin_freq=hy_sin_freq, hy_bias=hy_bias)
    bp, lp, _ = x_prompt.shape
    bs, ls, _ = x_sample.shape

    w_in_p = _reorder_in_cols(w_in.astype(BF16))
    cvecs = jnp.concatenate([c_ctx[None, :], c, jnp.zeros((8 - 1 - bs, D_MODEL), F32)], axis=0)
    mods = _mod_all(cvecs, w_mod, b_mod)
    lb = _hgrn_lb(hgrn_lb_logits)
    seg512, cnt512 = _mla_seg()
    hgrn_dd, hgrn_right = _hgrn_consts()
    consts = dict(seg512=seg512, cnt512=cnt512, seg32=_seg_const(BRANCH, DIFF_HD), seg64=_seg_const(BRANCH, HGRN_DK),
                  hgrn_dd=hgrn_dd, hgrn_right=hgrn_right)
    lws = [_layer_weights(l, w_in_p, lb, W) for l in range(DEPTH)]

    hy_p = _hy_tables(lp)
    y = x_prompt.reshape(bp * lp, D_MODEL)
    per_layer = []
    for l in range(DEPTH):
        mod = mods[l, 0:1].reshape(1, 3, D_MODEL)
        y, new = _layer(y, mod, lws[l], consts, bp, lp, None, None, hy_p)
        per_layer.append(new)
    y_prompt = y.reshape(bp, lp, D_MODEL)
    news = [jnp.stack([s[i] for s in per_layer], axis=1) for i in range(5)]

    hy_s = _hy_tables(ls)
    rope_tabs = _rope_tables(ls)
    y = x_sample.reshape(bs * ls, D_MODEL)
    past = cache_mla_ckv.shape[2]
    cache_kr = jnp.pad(cache_mla_krope, ((0, 0), (0, 0), (0, 0), (KR_OFF, LANE - KR_OFF - MLA_ROPE)))
    cache_kb = cache_diff_k.reshape(bs, DEPTH, past, BRANCH).astype(BF16)
    cache_vtb = _vt_with_ones(cache_diff_v.reshape(bs * DEPTH, past, BRANCH).astype(BF16))
    cache_vtb = cache_vtb.reshape(bs, DEPTH, DIFF_HEADS, VT_ROWS, past)
    for l in range(DEPTH):
        mod = mods[l, 1:1 + bs].reshape(bs, 3, D_MODEL)
        ctx = (cache_mla_ckv[:, l], cache_kr[:, l], cache_kb[:, l], cache_vtb[:, l], state_hgrn[:, l])
        y, _ = _layer(y, mod, lws[l], consts, bs, ls, ctx, rope_tabs, hy_s)
    y_sample = y.reshape(bs, ls, D_MODEL)

    return (y_prompt, y_sample, news[0], news[1], news[2], news[3], news[4])
```

```python
import functools
import math

import numpy as np
import jax
import jax.numpy as jnp
from jax import lax
from jax.experimental import pallas as pl
from jax.experimental.pallas import tpu as pltpu

F32 = jnp.float32
BF16 = jnp.bfloat16

D_MODEL = 1024
DEPTH = 4
GRID_W = 64
ROPE_BASE = 10000.0
EPS = 1e-6
BRANCH = 256
MLA_HEADS = 4
MLA_NOPE = 64
MLA_ROPE = 32
MLA_V = 64
MLA_Q_LORA = 256
MLA_KV_LORA = 128
MLA_SCALE = (MLA_NOPE + MLA_ROPE) ** -0.5
DIFF_HEADS = 4
DIFF_HD = 32
DIFF_SCALE = DIFF_HD ** -0.5
HGRN_HEADS = 4
HGRN_DK = 64
HGRN_CHUNK = 128
HGRN_LEVELS = 7
HGRN_ROWS = 4
HGRN_MM_LEVELS = 3
HY_CH = 256
HY_ORDER = 2
HY_EMB = 33
HY_BANDS = 16
HY_FH = 64
HY_DECAY_TARGET = 0.01
HY_FAST_DECAY = 0.3
HY_SLOW_DECAY = 1.5
IN_COLS = 4000

LANE = 128
LOG2E = math.log2(math.e)
VT_ROWS = 80
FFT_N2 = 128
FFT_BLOCK_BYTES = 2 * 1024 * 1024
MLA_AHEAD = 8
MLA_CHUNK = 512
DIFF_AHEAD = 6
DIFF_CHUNK = 256
MLA_SUB = 256
DIFF_SUB = 512
VMEM_LIMIT = 52 * 1024 * 1024

P_CQ, P_CKV, P_KR, P_DQ, P_DK, P_DV = 0, 256, 384, 512, 768, 1024
P_HQ, P_HZF, P_HZB, P_HI, P_HU, P_GATE = 1280, 1536, 1792, 2048, 2304, 3072
P_COLS = 4096
KR_OFF = 64


def _in_col_perm():
    src = np.full((P_COLS,), IN_COLS, np.int32)

    def put(dst, lo, n):
        src[dst:dst + n] = np.arange(lo, lo + n)

    put(P_CQ, 0, 256)
    put(P_CKV, 256, 128)
    put(P_KR + KR_OFF, 384, 32)
    put(P_GATE, 416, 256)
    put(P_DQ, 672, 256)
    put(P_DK, 928, 256)
    put(P_DV, 1184, 256)
    put(P_GATE + 256, 1440, 256)
    put(P_HQ, 1696, 256)
    put(P_HZF, 1952, 256)
    put(P_HZB, 2208, 256)
    put(P_HI, 2464, 256)
    put(P_GATE + 512, 2720, 256)
    put(P_HU, 2976, 768)
    put(P_GATE + 768, 3744, 256)
    return src


def _reorder_in_cols(w):
    src = _in_col_perm()
    pieces, lo = [], 0
    while lo < P_COLS:
        hi = lo + 1
        if src[lo] == IN_COLS:
            while hi < P_COLS and src[hi] == IN_COLS:
                hi += 1
            pieces.append(jnp.zeros(w.shape[:-1] + (hi - lo,), w.dtype))
        else:
            while hi < P_COLS and src[hi] == src[hi - 1] + 1:
                hi += 1
            pieces.append(w[..., int(src[lo]):int(src[lo]) + hi - lo])
        lo = hi
    return jnp.concatenate(pieces, axis=-1)


def _cparams(sem):
    return pltpu.CompilerParams(dimension_semantics=sem, vmem_limit_bytes=VMEM_LIMIT)


def _bdot(a, b):
    return jnp.dot(a.astype(BF16), b.astype(BF16), preferred_element_type=F32)


def _nt(a, b):
    return lax.dot_general(a.astype(BF16), b.astype(BF16), (((1,), (1,)), ((), ())), preferred_element_type=F32)


def _split2(a):
    hi = a.astype(BF16)
    lo = (a - hi.astype(F32)).astype(BF16)
    return hi, lo


def _dot3(a, b):
    ah, al = _split2(a)
    bh, bl = _split2(b)
    d = functools.partial(jnp.dot, preferred_element_type=F32)
    return d(ah, bh) + d(ah, bl) + d(al, bh)


def _segsum(v, seg):
    return jnp.dot(v.astype(BF16), seg, preferred_element_type=F32)


def _rms(x, g):
    return x * lax.rsqrt(jnp.mean(x * x, axis=-1, keepdims=True) + EPS) * g


def _swap8(x):
    w = x.shape[-1]
    lane = lax.broadcasted_iota(jnp.int32, x.shape, x.ndim - 1)
    up = pltpu.roll(x, w - 8, x.ndim - 1)
    dn = pltpu.roll(x, 8, x.ndim - 1)
    return jnp.where((lane & 15) < 8, up, dn)


def _tile_lanes(x, n):
    return x if n == 1 else jnp.concatenate([x] * n, axis=-1)


def _mod_kernel(c_ref, w_ref, b_ref, o_ref):
    c = c_ref[...]
    o_ref[0] = _dot3(c * jax.nn.sigmoid(c), w_ref[0]) + b_ref[0]


def _mod_all(cvecs, w_mod, b_mod):
    nt = 3
    return pl.pallas_call(
        _mod_kernel,
        grid=(DEPTH, nt),
        in_specs=[pl.BlockSpec((8, D_MODEL), lambda l, j: (0, 0)),
                  pl.BlockSpec((1, D_MODEL, D_MODEL), lambda l, j: (l, 0, j)),
                  pl.BlockSpec((1, 1, D_MODEL), lambda l, j: (l, 0, j))],
        out_specs=pl.BlockSpec((1, 8, D_MODEL), lambda l, j: (l, 0, j)),
        out_shape=jax.ShapeDtypeStruct((DEPTH, 8, 3 * D_MODEL), F32),
        compiler_params=_cparams(("arbitrary", "arbitrary")),
        name="mod",
    )(cvecs, w_mod, b_mod.reshape(DEPTH, 1, 3 * D_MODEL))


def _lb_kernel(x_ref, o_ref):
    x = x_ref[...]
    rows = [x[l:l + 1, :] for l in range(DEPTH)]
    m = functools.reduce(jnp.maximum, rows)
    e = [jnp.exp(r - m) for r in rows]
    tot = functools.reduce(lambda a, b: a + b, e)
    acc = jnp.zeros_like(tot)
    o_ref[0:1, :] = acc
    for l in range(1, DEPTH):
        acc = acc + e[l] / tot
        o_ref[l:l + 1, :] = acc


def _hgrn_lb(logits):
    flat = logits.transpose(1, 0, 2).reshape(DEPTH, 2 * BRANCH)
    lb = pl.pallas_call(
        _lb_kernel,
        out_shape=jax.ShapeDtypeStruct(flat.shape, F32),
        name="hgrn_lb",
    )(flat)
    return lb.reshape(DEPTH, 2, BRANCH).transpose(1, 0, 2)


def _inproj_kernel(x_ref, mod_ref, g_ref, w_ref, p_ref):
    h = _rms(x_ref[...], g_ref[...]) * (1.0 + mod_ref[0, 1:2, :]) + mod_ref[0, 0:1, :]
    p_ref[...] = jnp.dot(h.astype(BF16), w_ref[...], preferred_element_type=F32)


def _inproj(x2, mod, norm_g, w_in_p, seq_len):
    n = x2.shape[0]
    tm = min(512, seq_len)
    per_batch = mod.shape[0] > 1
    tiles_per_seq = seq_len // tm
    mod_idx = (lambda i: (i // tiles_per_seq, 0, 0)) if per_batch else (lambda i: (0, 0, 0))
    return pl.pallas_call(
        _inproj_kernel,
        grid=(n // tm,),
        in_specs=[pl.BlockSpec((tm, D_MODEL), lambda i: (i, 0)),
                  pl.BlockSpec((1, 3, D_MODEL), mod_idx),
                  pl.BlockSpec((1, D_MODEL), lambda i: (0, 0)),
                  pl.BlockSpec((D_MODEL, P_COLS), lambda i: (0, 0))],
        out_specs=pl.BlockSpec((tm, P_COLS), lambda i: (i, 0)),
        out_shape=jax.ShapeDtypeStruct((n, P_COLS), F32),
        compiler_params=_cparams(("parallel",)),
        name="inproj",
    )(x2, mod, norm_g.reshape(1, D_MODEL), w_in_p)


def _mla_seg():
    sid = np.zeros((512,), np.int32)
    cnt = np.ones((512,), np.float32)
    for h in range(MLA_HEADS):
        b = 128 * h
        sid[b:b + 64] = 3 * h
        sid[b + 64:b + 96] = 3 * h + 1
        sid[b + 96:b + 128] = 3 * h + 2
        cnt[b:b + 64] = 1.0 / 64
        cnt[b + 64:b + 128] = 1.0 / 32
    seg = (sid[:, None] == sid[None, :]).astype(np.float32)
    return jnp.asarray(seg, BF16), jnp.asarray(cnt.reshape(1, 512))


def _mla_q_kernel(rope, cq_ref, ckv_ref, kr_ref, qng_ref, wuq_ref, kvg_ref, gq_ref, gkr_ref, seg_ref, cnt_ref,
                  *rest):
    if rope:
        cos_ref, sin_ref, q_ref, ckvn_ref, krp_ref = rest
    else:
        q_ref, ckvn_ref, krp_ref = rest
    cqn = _rms(cq_ref[...], qng_ref[...])
    q = _bdot(cqn, wuq_ref[...])
    ss = _segsum(q * q, seg_ref[...]) * cnt_ref[...]
    qn = q * lax.rsqrt(ss + EPS) * gq_ref[...]
    ckvn_ref[...] = _rms(ckv_ref[...], kvg_ref[...])
    kr = kr_ref[...]
    krn = kr * lax.rsqrt(jnp.sum(kr * kr, axis=-1, keepdims=True) * (1.0 / MLA_ROPE) + EPS) * gkr_ref[...]
    if rope:
        cos, sin = cos_ref[...], sin_ref[...]
        qn = qn * _tile_lanes(cos, MLA_HEADS) + _swap8(qn) * _tile_lanes(sin, MLA_HEADS)
        krn = krn * cos + _swap8(krn) * sin
    q_ref[...] = (qn * (MLA_SCALE * LOG2E)).astype(BF16)
    krp_ref[...] = krn


def _mla_q(p, lw, consts, seq_len, rope_tabs):
    n = p.shape[0]
    tm = min(512, seq_len)
    rope = rope_tabs is not None
    full = lambda shape: pl.BlockSpec(shape, lambda i: (0,) * len(shape))
    in_specs = [pl.BlockSpec((tm, 256), lambda i: (i, P_CQ // 256)),
                pl.BlockSpec((tm, 128), lambda i: (i, P_CKV // 128)),
                pl.BlockSpec((tm, 128), lambda i: (i, P_KR // 128)),
                full((1, 256)), full((256, 512)), full((1, 128)), full((1, 512)), full((1, 128)),
                full((512, 512)), full((1, 512))]
    args = [p, p, p, lw["qn_g"], lw["w_uq"], lw["kvn_g"], lw["gq"], lw["gkr"], consts["seg512"], consts["cnt512"]]
    if rope:
        tps = seq_len // tm
        in_specs += [pl.BlockSpec((tm, 128), lambda i: (i % tps, 0))] * 2
        args += [rope_tabs["cos_mla"], rope_tabs["sin_mla"]]
    return pl.pallas_call(
        functools.partial(_mla_q_kernel, rope),
        grid=(n // tm,),
        in_specs=in_specs,
        out_specs=[pl.BlockSpec((tm, 512), lambda i: (i, 0)),
                   pl.BlockSpec((tm, 128), lambda i: (i, 0)),
                   pl.BlockSpec((tm, 128), lambda i: (i, 0))],
        out_shape=[jax.ShapeDtypeStruct((n, 512), BF16),
                   jax.ShapeDtypeStruct((n, 128), F32),
                   jax.ShapeDtypeStruct((n, 128), F32)],
        compiler_params=_cparams(("parallel",)),
        name="mla_q",
    )(*args)


def _store_vt(o_ref, vt):
    tm = vt.shape[1]
    row = lax.broadcasted_iota(jnp.int32, (VT_ROWS - 64, tm), 0)
    extra = jnp.where(row == 0, 1.0, 0.0).astype(BF16)
    for h in range(4):
        o_ref[0, h, 0:64, :] = vt[64 * h:64 * (h + 1)].astype(BF16)
        o_ref[0, h, 64:VT_ROWS, :] = extra


def _mla_kv_kernel(ckvn_ref, krp_ref, wuk_ref, wuv_ref, gk_ref, seg_ref, cnt_ref, k_ref, vt_ref):
    c = ckvn_ref[...].astype(BF16)
    kn = jnp.dot(c, wuk_ref[...], preferred_element_type=F32)
    ss = _segsum(kn * kn, seg_ref[...]) * cnt_ref[...]
    k = kn * lax.rsqrt(ss + EPS) * gk_ref[...] + _tile_lanes(krp_ref[...], MLA_HEADS)
    k_ref[...] = k.astype(BF16)
    _store_vt(vt_ref, _nt(wuv_ref[...], c))


def _mla_kv(ckvn, krp, lw, consts, batch):
    n = ckvn.shape[0]
    lseg = n // batch
    tm = min(512, lseg)
    tpb = lseg // tm
    full = lambda shape: pl.BlockSpec(shape, lambda i: (0,) * len(shape))
    return pl.pallas_call(
        _mla_kv_kernel,
        grid=(n // tm,),
        in_specs=[pl.BlockSpec((tm, 128), lambda i: (i, 0)), pl.BlockSpec((tm, 128), lambda i: (i, 0)),
                  full((128, 512)), full((256, 128)), full((1, 512)), full((512, 512)), full((1, 512))],
        out_specs=[pl.BlockSpec((tm, 512), lambda i: (i, 0)),
                   pl.BlockSpec((1, 4, VT_ROWS, tm), lambda i: (i // tpb, 0, 0, i % tpb))],
        out_shape=[jax.ShapeDtypeStruct((n, 512), BF16), jax.ShapeDtypeStruct((batch, 4, VT_ROWS, lseg), BF16)],
        compiler_params=_cparams(("parallel",)),
        name="mla_kv",
    )(ckvn, krp, lw["w_uk"], lw["w_uv"], lw["gk"], consts["seg512"], consts["cnt512"])


def _softmax_pv(qs, k_refs, vt_refs, key_chunk, sub_rows, n_ahead, k_lane=0, v_head=0):
    where = [(i, lo) for i, r in enumerate(k_refs) for lo in range(0, r.shape[1], key_chunk)]
    nch = len(where)
    nq = len(qs)
    sub = min(sub_rows, key_chunk)
    nsub = key_chunk // sub

    def scores(c, u):
        seg, lo = where[c]
        ks = k_refs[seg][0, lo + u * sub:lo + (u + 1) * sub, k_lane:k_lane + LANE]
        return [_nt(ks, q) for q in qs]

    def chunk_max(s_chunk, j):
        mc = functools.reduce(jnp.maximum, [s_chunk[u][j] for u in range(nsub)])
        return jnp.max(mc, axis=0, keepdims=True)

    s_buf = {c: [scores(c, u) for u in range(nsub)] for c in range(min(n_ahead, nch))}
    m = [None] * nq
    acc = [None] * nq
    m_new = [chunk_max(s_buf[0], j) for j in range(nq)]
    for c in range(nch):
        s_cur = s_buf.pop(c)
        ahead = c + n_ahead
        if ahead < nch:
            s_buf[ahead] = []
        pv = [None] * nq
        for u in range(nsub):
            if ahead < nch:
                s_buf[ahead].append(scores(ahead, u))
            seg, lo = where[c]
            vs = vt_refs[seg][0, v_head, :, lo + u * sub:lo + (u + 1) * sub]
            for j in range(nq):
                part = jnp.dot(vs, jnp.exp2(s_cur[u][j] - m_new[j]).astype(BF16), preferred_element_type=F32)
                pv[j] = part if pv[j] is None else pv[j] + part
        for j in range(nq):
            acc[j] = pv[j] if c == 0 else acc[j] * jnp.exp2(m[j] - m_new[j]) + pv[j]
            m[j] = m_new[j]
        if c + 1 < nch:
            m_new = [jnp.maximum(m[j], chunk_max(s_buf[c + 1], j)) for j in range(nq)]
    return acc


def _mla_attn_kernel(key_chunk, nseg, heads, q_ref, *refs):
    k_refs, vt_refs, o_ref = refs[:nseg], refs[nseg:2 * nseg], refs[2 * nseg]
    for hh in range(heads):
        q = q_ref[:, LANE * hh:LANE * (hh + 1)]
        (acc,) = _softmax_pv([q], k_refs, vt_refs, key_chunk, MLA_SUB, MLA_AHEAD, LANE * hh, hh)
        o_ref[0, MLA_V * hh:MLA_V * (hh + 1)] = acc[0:MLA_V] / acc[MLA_V:MLA_V + 1]


def _key_chunk(lk, rows=512):
    return rows if lk % rows == 0 else lk


def _heads_per_step(ks):
    return 4 if sum(k.shape[1] for k in ks) <= 512 else 1


def _vt_kernel(v_ref, o_ref):
    _store_vt(o_ref, v_ref[0].astype(F32).T)


def _vt_with_ones(v3):
    b, lk, width = v3.shape
    tm = 1536 if lk % 1536 == 0 else lk
    return pl.pallas_call(
        _vt_kernel,
        grid=(b, lk // tm),
        in_specs=[pl.BlockSpec((1, tm, width), lambda i, j: (i, j, 0))],
        out_specs=pl.BlockSpec((1, 4, VT_ROWS, tm), lambda i, j: (i, 0, 0, j)),
        out_shape=jax.ShapeDtypeStruct((b, 4, VT_ROWS, lk), BF16),
        compiler_params=_cparams(("parallel", "parallel")),
        name="vt_ones",
    )(v3)


def _mla_attn(q, ks, vts, batch, seq_len):
    tq = 256
    nq = seq_len // tq
    chunk = _key_chunk(min(k.shape[1] for k in ks), MLA_CHUNK)
    hp = _heads_per_step(ks)
    k_specs = [pl.BlockSpec((1, k.shape[1], LANE * hp), lambda b, h, i: (b, 0, h)) for k in ks]
    v_specs = [pl.BlockSpec((1, hp, VT_ROWS, v.shape[3]), lambda b, h, i: (b, h, 0, 0)) for v in vts]
    return pl.pallas_call(
        functools.partial(_mla_attn_kernel, chunk, len(ks), hp),
        grid=(batch, MLA_HEADS // hp, nq),
        in_specs=[pl.BlockSpec((tq, LANE * hp), lambda b, h, i: (b * nq + i, h))] + k_specs + v_specs,
        out_specs=pl.BlockSpec((1, MLA_V * hp, tq), lambda b, h, i: (b, h, i)),
        out_shape=jax.ShapeDtypeStruct((batch, BRANCH, seq_len), F32),
        compiler_params=_cparams(("parallel", "parallel", "arbitrary")),
        name="mla_attn",
    )(q, *ks, *vts)


def _seg_const(width, seg):
    sid = np.arange(width) // seg
    return jnp.asarray((sid[:, None] == sid[None, :]).astype(np.float32), BF16)


def _diff_prep_kernel(rope, dq_ref, dk_ref, dv_ref, gq_ref, gk_ref, seg_ref, *rest):
    if rope:
        cos_ref, sin_ref, q_ref, k_ref, kf_ref, vt_ref = rest
    else:
        q_ref, k_ref, kf_ref, vt_ref = rest
    seg = seg_ref[...]
    _store_vt(vt_ref, dv_ref[...].T)

    def norm(x, g):
        ss = _segsum(x * x, seg) * (1.0 / DIFF_HD)
        return x * lax.rsqrt(ss + EPS) * g

    q = norm(dq_ref[...], gq_ref[...])
    k = norm(dk_ref[...], gk_ref[...])
    kf_ref[...] = k
    if rope:
        cos, sin = cos_ref[...], sin_ref[...]
        q = q * cos + _swap8(q) * sin
        k = k * cos + _swap8(k) * sin
    q_ref[...] = (q * (DIFF_SCALE * LOG2E)).astype(BF16)
    k_ref[...] = k.astype(BF16)


def _diff_prep(p, lw, consts, seq_len, rope_tabs):
    n = p.shape[0]
    tm = min(512, seq_len)
    rope = rope_tabs is not None
    full = lambda shape: pl.BlockSpec(shape, lambda i: (0,) * len(shape))
    in_specs = [pl.BlockSpec((tm, 256), lambda i: (i, P_DQ // 256)),
                pl.BlockSpec((tm, 256), lambda i: (i, P_DK // 256)),
                pl.BlockSpec((tm, 256), lambda i: (i, P_DV // 256)),
                full((1, 256)), full((1, 256)), full((256, 256))]
    args = [p, p, p, lw["dgq"], lw["dgk"], consts["seg32"]]
    tps = seq_len // tm
    if rope:
        in_specs += [pl.BlockSpec((tm, 256), lambda i: (i % tps, 0))] * 2
        args += [rope_tabs["cos_diff"], rope_tabs["sin_diff"]]
    blk = pl.BlockSpec((tm, 256), lambda i: (i, 0))
    return pl.pallas_call(
        functools.partial(_diff_prep_kernel, rope),
        grid=(n // tm,),
        in_specs=in_specs,
        out_specs=[blk, blk, blk, pl.BlockSpec((1, 4, VT_ROWS, tm), lambda i: (i // tps, 0, 0, i % tps))],
        out_shape=[jax.ShapeDtypeStruct((n, 256), BF16), jax.ShapeDtypeStruct((n, 256), BF16),
                   jax.ShapeDtypeStruct((n, 256), F32),
                   jax.ShapeDtypeStruct((n // seq_len, 4, VT_ROWS, seq_len), BF16)],
        compiler_params=_cparams(("parallel",)),
        name="diff_prep",
    )(*args)


def _diff_attn_kernel(lam_init, key_chunk, nseg, heads, q_ref, *refs):
    k_refs, vt_refs = refs[:nseg], refs[nseg:2 * nseg]
    lp_ref, g_ref, o_ref = refs[2 * nseg:]
    lp = lp_ref[...]
    lam = (jnp.exp(jnp.sum(lp[0:1] * lp[1:2], axis=1, keepdims=True))
           - jnp.exp(jnp.sum(lp[2:3] * lp[3:4], axis=1, keepdims=True)) + lam_init)
    for hh in range(heads):
        blk = 0 if heads == 1 else LANE * (hh // 2)
        base = (pl.program_id(1) % 2) * 64 if heads == 1 else (hh % 2) * 64
        q = q_ref[:, blk:blk + LANE]
        lane = lax.broadcasted_iota(jnp.int32, q.shape, 1)
        zero = jnp.zeros_like(q)

        def map_query(j):
            lo = base + 32 * j
            return jnp.where((lane >= lo) & (lane < lo + 32), q, zero)

        acc0, acc1 = _softmax_pv([map_query(0), map_query(1)], k_refs, vt_refs, key_chunk, DIFF_SUB, DIFF_AHEAD,
                                 blk, hh)
        o = acc0[0:64] / acc0[64:65] - lam * (acc1[0:64] / acc1[64:65])
        ms = jnp.mean(o * o, axis=0, keepdims=True)
        o_ref[0, 64 * hh:64 * (hh + 1)] = o * lax.rsqrt(ms + EPS) * g_ref[...] * (1.0 - lam_init)


def _diff_attn(q, ks, vts, lp, g_col, lam_init, batch, seq_len):
    tq = min(512, seq_len)
    nq = seq_len // tq
    chunk = _key_chunk(min(k.shape[1] for k in ks), DIFF_CHUNK)
    hp = _heads_per_step(ks)
    lanes = LANE if hp == 1 else BRANCH
    k_specs = [pl.BlockSpec((1, k.shape[1], lanes), lambda b, h, i: (b, 0, h // 2)) for k in ks]
    v_specs = [pl.BlockSpec((1, hp, VT_ROWS, v.shape[3]), lambda b, h, i: (b, h, 0, 0)) for v in vts]
    return pl.pallas_call(
        functools.partial(_diff_attn_kernel, lam_init, chunk, len(ks), hp),
        grid=(batch, DIFF_HEADS // hp, nq),
        in_specs=[pl.BlockSpec((tq, lanes), lambda b, h, i: (b * nq + i, h // 2))] + k_specs + v_specs
        + [pl.BlockSpec((4, DIFF_HD), lambda b, h, i: (0, 0)), pl.BlockSpec((64, 1), lambda b, h, i: (0, 0))],
        out_specs=pl.BlockSpec((1, 64 * hp, tq), lambda b, h, i: (b, h, i)),
        out_shape=jax.ShapeDtypeStruct((batch, BRANCH, seq_len), F32),
        compiler_params=_cparams(("parallel", "parallel", "arbitrary")),
        name="diff_attn",
    )(q, *ks, *vts, lp, g_col)


def _hgrn_consts():
    c = HGRN_CHUNK
    t = np.arange(c)
    low = (t[None, :] <= t[:, None]).astype(np.float32)
    blocks = []
    for j in range(HGRN_MM_LEVELS):
        m = 1 << j
        rho = (t // (2 * m)) * (2 * m) + m - 1
        sign = np.where((t // m) % 2 == 1, 1.0, -1.0)[:, None]
        blocks.append(sign * (low - (t[None, :] <= rho[:, None]).astype(np.float32)))
    blocks.append(low)
    fwd = np.concatenate(blocks, axis=0)
    bwd = np.concatenate([b[::-1, ::-1] for b in blocks], axis=0)
    right = np.stack([(t // (1 << j)) % 2 for j in range(HGRN_LEVELS)]).astype(np.float32)
    right = np.stack([right, right[:, ::-1]])
    right = np.broadcast_to(right[..., None], right.shape + (BRANCH,))
    return jnp.asarray(np.stack([fwd, bwd]), BF16), jnp.asarray(right, F32)


def _hgrn_kernel(nc, nb, qf_ref, zf_ref, vf_ref, qb_ref, zb_ref, vb_ref, lb_ref, dd_ref, rm_ref, s0_ref,
                 of_ref, ob_ref, sout_ref, st_ref):
    c = HGRN_CHUNK
    ci = pl.program_id(1)
    chains = [(bi, d) for bi in range(nb) for d in (0, 1)]
    ids = range(len(chains))

    @pl.when(ci == 0)
    def _():
        st_ref[...] = s0_ref[...]

    lane = lax.broadcasted_iota(jnp.int32, (1, BRANCH), 1)
    head_masks = [(lane >= HGRN_DK * h) & (lane < HGRN_DK * (h + 1)) for h in range(HGRN_HEADS)]
    t_idx = lax.broadcasted_iota(jnp.int32, (c, HGRN_HEADS * c), 0)
    s_idx = lax.broadcasted_iota(jnp.int32, (c, HGRN_HEADS * c), 1) & (c - 1)
    pair_xor = t_idx ^ s_idx

    def stack_heads(x):
        xb = x.astype(BF16)
        zero = jnp.zeros_like(xb)
        return jnp.concatenate([jnp.where(hm, xb, zero) for hm in head_masks], axis=0)

    q_refs, z_refs, v_refs = (qf_ref, qb_ref), (zf_ref, zb_ref), (vf_ref, vb_ref)
    q = [q_refs[d][bi] for bi, d in chains]
    v = [v_refs[d][bi] for bi, d in chains]
    z = [z_refs[d][bi] for bi, d in chains]
    lb = [lb_ref[d] for _, d in chains]
    g = [jnp.log(lb[i] + (1.0 - lb[i]) * jax.nn.sigmoid(z[i])) for i in ids]
    kk = [(1.0 - lb[i]) * jax.nn.sigmoid(-z[i]) for i in ids]
    sums = []
    for i in ids:
        gh, gl = _split2(g[i])
        dd = dd_ref[chains[i][1]]
        sums.append(jnp.dot(dd, gh, preferred_element_type=F32) + jnp.dot(dd, gl, preferred_element_type=F32))
    b = [sums[i][HGRN_MM_LEVELS * c:] for i in ids]
    b_tot = [b[i][c - 1:c] if chains[i][1] == 0 else b[i][0:1] for i in ids]

    def neg_abs_decay(i, j):
        if j < HGRN_MM_LEVELS:
            return sums[i][j * c:(j + 1) * c]
        m = 1 << j
        off = m - 1 if chains[i][1] == 0 else m
        ref = jnp.concatenate([jnp.broadcast_to(b[i][g0 + off:g0 + off + 1], (2 * m, BRANCH))
                               for g0 in range(0, c, 2 * m)], axis=0)
        return -jnp.abs(b[i] - ref)

    a = [None] * len(chains)
    for j in reversed(range(HGRN_LEVELS)):
        same_group = pair_xor < (2 << j)
        for i in ids:
            e = jnp.exp(neg_abs_decay(i, j))
            eq = e * rm_ref[chains[i][1], j]
            qt = q[i] * eq
            kt = kk[i] * (e - eq)
            lvl = _nt(qt, stack_heads(kt))
            a[i] = lvl if a[i] is None else jnp.where(same_group, lvl, a[i])
    diagonal = pair_xor == 0
    for i in ids:
        a[i] = jnp.where(diagonal, _nt(q[i], stack_heads(kk[i])), a[i])

    outs = (of_ref, ob_ref)
    for i in ids:
        bi, d = chains[i]
        o = jnp.dot(a[i].astype(BF16), stack_heads(v[i]), preferred_element_type=F32)
        outs[d][bi] = o + _nt(q[i] * jnp.exp(b[i]), st_ref[bi, d])

    r2 = lax.broadcasted_iota(jnp.int32, (BRANCH, BRANCH), 0) // HGRN_DK
    c2 = lax.broadcasted_iota(jnp.int32, (BRANCH, BRANCH), 1) // HGRN_DK
    for i in ids:
        bi, d = chains[i]
        kd = kk[i] * jnp.exp(b_tot[i] - b[i])
        upd = lax.dot_general(v[i].astype(BF16), kd.astype(BF16), (((0,), (0,)), ((), ())),
                              preferred_element_type=F32)
        st_new = st_ref[bi, d] * jnp.exp(b_tot[i]) + jnp.where(r2 == c2, upd, 0.0)
        st_ref[bi, d] = st_new

        @pl.when(ci == nc - 1)
        def _(bi=bi, d=d, st_new=st_new):
            sout_ref[bi, d] = st_new


def _hgrn(p, lb_l, dd, rm, st0, batch, seq_len):
    n = p.shape[0]
    c = HGRN_CHUNK
    nc = seq_len // c
    nb = HGRN_ROWS if batch % HGRN_ROWS == 0 else 1
    p3 = p.reshape(batch, seq_len, P_COLS)
    fwd = lambda col: pl.BlockSpec((nb, c, 256), lambda b, i: (b, i, col))
    bwd = lambda col: pl.BlockSpec((nb, c, 256), lambda b, i: (b, nc - 1 - i, col))
    whole = lambda shape: pl.BlockSpec(shape, lambda b, i: (0,) * len(shape))
    state = pl.BlockSpec((nb, 2, 256, 256), lambda b, i: (b, 0, 0, 0))
    o_f, o_b, st = pl.pallas_call(
        functools.partial(_hgrn_kernel, nc, nb),
        grid=(batch // nb, nc),
        in_specs=[fwd(P_HQ // 256), fwd(P_HZF // 256), fwd(P_HI // 256),
                  bwd(P_HQ // 256), bwd(P_HZB // 256), bwd(P_HI // 256),
                  whole((2, 1, 256)), whole((2, (HGRN_MM_LEVELS + 1) * c, c)),
                  whole((2, HGRN_LEVELS, c, 256)), state],
        out_specs=[pl.BlockSpec((nb, c, 256), lambda b, i: (b, i, 0)),
                   pl.BlockSpec((nb, c, 256), lambda b, i: (b, nc - 1 - i, 0)), state],
        out_shape=[jax.ShapeDtypeStruct((batch, seq_len, 256), F32),
                   jax.ShapeDtypeStruct((batch, seq_len, 256), F32),
                   jax.ShapeDtypeStruct((batch, 2, 256, 256), F32)],
        scratch_shapes=[pltpu.VMEM((nb, 2, 256, 256), F32)],
        compiler_params=_cparams(("parallel", "arbitrary")),
        name="hgrn",
    )(p3, p3, p3, p3, p3, p3, lb_l, dd, rm, st0)
    return o_f.reshape(n, 256), o_b.reshape(n, 256), st


def _hy_conv3_kernel(tiles_per_seq, above_ref, cur_ref, below_ref, w_ref, b_ref, v_ref, x1_ref, x2_ref):
    i = pl.program_id(0)
    cur = cur_ref[...]
    tm = cur.shape[0]
    first = (i % tiles_per_seq) == 0
    last = (i % tiles_per_seq) == tiles_per_seq - 1
    above = jnp.where(first, 0.0, above_ref[7:8, :])
    below = jnp.where(last, 0.0, below_ref[0:1, :])
    row = lax.broadcasted_iota(jnp.int32, (tm, 1), 0)
    prev = jnp.where(row == 0, above, pltpu.roll(cur, 1, 0))
    nxt = jnp.where(row == tm - 1, below, pltpu.roll(cur, tm - 1, 0))
    w = w_ref[...]
    u = prev * w[0:1] + cur * w[1:2] + nxt * w[2:3] + b_ref[...]
    v_ref[...] = u[:, 0:256]
    x1_ref[...] = u[:, 256:512]
    x2_ref[...] = u[:, 512:768]


def _hy_conv3(p, w, b, seq_len):
    n = p.shape[0]
    tm = min(512, seq_len)
    nt = n // tm
    g = tm // 8
    col = P_HU // 768
    oblk = pl.BlockSpec((tm, 256), lambda i: (i, 0))
    return pl.pallas_call(
        functools.partial(_hy_conv3_kernel, seq_len // tm),
        grid=(nt,),
        in_specs=[pl.BlockSpec((8, 768), lambda i: (jnp.maximum(i * g - 1, 0), col)),
                  pl.BlockSpec((tm, 768), lambda i: (i, col)),
                  pl.BlockSpec((8, 768), lambda i: (jnp.minimum((i + 1) * g, nt * g - 1), col)),
                  pl.BlockSpec((3, 768), lambda i: (0, 0)), pl.BlockSpec((1, 768), lambda i: (0, 0))],
        out_specs=[oblk, oblk, oblk],
        out_shape=[jax.ShapeDtypeStruct((n, 256), F32)] * 3,
        compiler_params=_cparams(("parallel",)),
        name="hy_conv3",
    )(p, p, p, w, b)


def _hy_filter_kernel(feat_ref, w1_ref, b1_ref, w2_ref, b2_ref, w3_ref, fr_ref, win_ref, o_ref):
    fr = fr_ref[...]
    h = jnp.sin(fr[0:1] * (_dot3(feat_ref[...], w1_ref[...]) + b1_ref[...]))
    h = jnp.sin(fr[1:2] * (_dot3(h, w2_ref[...]) + b2_ref[...]))
    o_ref[...] = _dot3(h, w3_ref[0]) * _tile_lanes(win_ref[...], HY_ORDER)


def _hy_filter(feats2, window2, w1p, b1, w2, b2, w3d, freq):
    l2 = feats2.shape[0]
    ln = l2 // 2
    tm = min(512, ln)
    full = lambda shape: pl.BlockSpec(shape, lambda i: (0,) * len(shape))
    return pl.pallas_call(
        _hy_filter_kernel,
        grid=(l2 // tm,),
        in_specs=[pl.BlockSpec((tm, LANE), lambda i: (i, 0)),
                  full((LANE, HY_FH)), full((1, HY_FH)), full((HY_FH, HY_FH)), full((1, HY_FH)),
                  pl.BlockSpec((1, HY_FH, HY_ORDER * HY_CH), lambda i: (i // (ln // tm), 0, 0)), full((2, HY_FH)),
                  pl.BlockSpec((tm, HY_CH), lambda i: (i, 0))],
        out_specs=pl.BlockSpec((tm, HY_ORDER * HY_CH), lambda i: (i, 0)),
        out_shape=jax.ShapeDtypeStruct((l2, HY_ORDER * HY_CH), F32),
        compiler_params=_cparams(("parallel",)),
        name="hy_filter",
    )(feats2, w1p, b1, w2, b2, w3d, freq, window2)


def _fft_blocking(nb, n1, n2, ch):
    per_batch = n1 * n2 * ch * 4
    if per_batch <= FFT_BLOCK_BYTES:
        bb = max(1, min(nb, FFT_BLOCK_BYTES // per_batch))
        while nb % bb:
            bb -= 1
        return bb, n2
    rt = n2
    while n1 * rt * ch * 4 > FFT_BLOCK_BYTES and rt > 8:
        rt //= 2
    return 1, rt


def _fft_a_kernel(f_ref, x_ref, o_ref):
    f = f_ref[...]
    for b in range(x_ref.shape[0]):
        x = x_ref[b].astype(BF16)
        o_ref[b] = jnp.einsum("kn,nrc->krc", f, x, preferred_element_type=F32).astype(o_ref.dtype)


def _fft_a(fa, x4, out_dtype):
    nb, n1, n2, ch = x4.shape
    r = fa.shape[0]
    bb, rt = _fft_blocking(nb, n1, n2, ch)
    return pl.pallas_call(
        _fft_a_kernel,
        grid=(nb // bb, n2 // rt),
        in_specs=[pl.BlockSpec((r, n1), lambda b, i: (0, 0)),
                  pl.BlockSpec((bb, n1, rt, ch), lambda b, i: (b, 0, i, 0))],
        out_specs=pl.BlockSpec((bb, r, rt, ch), lambda b, i: (b, 0, i, 0)),
        out_shape=jax.ShapeDtypeStruct((nb, r, n2, ch), out_dtype),
        compiler_params=_cparams(("parallel", "parallel")),
        name="fft_a",
    )(fa, x4)


def _fft_b_kernel(with_inverse, mf_ref, *rest):
    if with_inverse:
        mi_ref, a_ref, h_ref, o_ref = rest
    else:
        a_ref, o_ref = rest
    half = FFT_N2
    for kk in range(a_ref.shape[2]):
        for b in range(a_ref.shape[0]):
            a = jnp.concatenate([a_ref[b, 0, kk], a_ref[b, 1, kk]], axis=0)
            x = jnp.dot(mf_ref[kk], a.astype(BF16), preferred_element_type=F32)
            if with_inverse:
                xr, xi = x[:half], x[half:]
                hr, hi = h_ref[0, kk], h_ref[1, kk]
                y = jnp.concatenate([xr * hr - xi * hi, xr * hi + xi * hr], axis=0)
                x = jnp.dot(mi_ref[kk], y.astype(BF16), preferred_element_type=F32)
            o_ref[b, 0, kk] = x[:half].astype(o_ref.dtype)
            o_ref[b, 1, kk] = x[half:].astype(o_ref.dtype)


def _fft_b(mf, mi, a5, spec, order):
    nb, _, k1n, n2, ch = a5.shape
    ks = 3 if (k1n % 3 == 0 and nb * ch <= 1024) else 1
    mat = pl.BlockSpec((ks, 2 * n2, 2 * n2), lambda k: (k, 0, 0))
    blk = pl.BlockSpec((nb, 2, ks, n2, ch), lambda k: (0, 0, k, 0, 0))
    if spec is None:
        in_specs, args = [mat, blk], [mf, a5]
    else:
        in_specs = [mat, mat, blk, pl.BlockSpec((2, ks, n2, ch), lambda k: (0, k, 0, order))]
        args = [mf, mi, a5, spec]
    return pl.pallas_call(
        functools.partial(_fft_b_kernel, spec is not None),
        grid=(k1n // ks,),
        in_specs=in_specs,
        out_specs=blk,
        out_shape=jax.ShapeDtypeStruct(a5.shape, F32 if spec is None else BF16),
        compiler_params=_cparams(("parallel",)),
        name="fft_b",
    )(*args)


def _fft_a_inv_kernel(g_ref, p_ref, x_ref, z_ref, bias_ref, o_ref):
    g = g_ref[...]
    for b in range(p_ref.shape[0]):
        conv = jnp.einsum("nk,krc->nrc", g, p_ref[b], preferred_element_type=F32)
        o_ref[b] = x_ref[b] * (conv + z_ref[b] * bias_ref[...])


def _fft_a_inv(g, p4, xg4, z4, bias):
    nb, n1, n2, ch = z4.shape
    r = g.shape[1]
    bb, rt = _fft_blocking(nb, n1, n2, ch)
    blk = pl.BlockSpec((bb, n1, rt, ch), lambda b, i: (b, 0, i, 0))
    return pl.pallas_call(
        _fft_a_inv_kernel,
        grid=(nb // bb, n2 // rt),
        in_specs=[pl.BlockSpec((n1, r), lambda b, i: (0, 0)),
                  pl.BlockSpec((bb, r, rt, ch), lambda b, i: (b, 0, i, 0)), blk, blk,
                  pl.BlockSpec((1, 1, ch), lambda b, i: (0, 0, 0))],
        out_specs=blk,
        out_shape=jax.ShapeDtypeStruct(z4.shape, F32),
        compiler_params=_cparams(("parallel", "parallel")),
        name="fft_a_inv",
    )(g, p4, xg4, z4, bias.reshape(1, 1, ch))


def _fft_tables(ln):
    n = 2 * ln
    n1t = n // FFT_N2
    k1n = n1t // 2 + 1
    kk = np.arange(k1n)

    def stage_a(n1_in):
        ang = 2.0 * np.pi * ((kk[:, None] * np.arange(n1_in)[None, :]) % n1t) / n1t
        return jnp.asarray(np.concatenate([np.cos(ang), -np.sin(ang)], axis=0), BF16)

    n1o = n1t // 2
    ang = 2.0 * np.pi * ((np.arange(n1o)[:, None] * kk[None, :]) % n1t) / n1t
    edge = (kk == 0) | (kk == n1t // 2)
    ck = np.where(edge, 1.0, 2.0) / n
    g = jnp.asarray(np.concatenate([ck * np.cos(ang), -ck * np.where(edge, 0.0, np.sin(ang))], axis=1), BF16)

    k1 = jnp.arange(k1n, dtype=jnp.int32)[:, None, None]
    k2 = jnp.arange(FFT_N2, dtype=jnp.int32)[None, :, None]
    n2 = jnp.arange(FFT_N2, dtype=jnp.int32)[None, None, :]
    th = (2.0 * math.pi / n) * ((n2 * (k1 + n1t * k2)) % n).astype(F32)
    c, s = jnp.cos(th), jnp.sin(th)
    mf = jnp.concatenate([jnp.concatenate([c, s], axis=2), jnp.concatenate([-s, c], axis=2)], axis=1)
    return dict(fa_half=stage_a(n1o), fa_full=stage_a(n1t), g=g, mf=mf.astype(BF16),
                mi=mf.transpose(0, 2, 1).astype(BF16), k1n=k1n, n1o=n1o, n1t=n1t)


def _hy_static(ln):
    t = jnp.linspace(0.0, 1.0, ln, dtype=F32)[:, None]
    w = 2.0 * math.pi * jnp.arange(ln, dtype=F32) / ln
    f = jnp.linspace(1e-4, HY_BANDS - 1, HY_BANDS, dtype=F32)
    ang = w[:, None] * f[None, :]
    feats = jnp.concatenate([t, jnp.cos(ang), -jnp.sin(ang)], axis=-1)
    feats = jnp.pad(feats, ((0, 0), (0, LANE - HY_EMB)))
    min_decay = math.log(HY_DECAY_TARGET) / HY_SLOW_DECAY
    max_decay = math.log(HY_DECAY_TARGET) / HY_FAST_DECAY
    deltas = jnp.linspace(min_decay, max_decay, HY_CH, dtype=F32)
    window = jnp.exp(-t * jnp.abs(deltas))
    feats = jnp.concatenate([feats, feats[::-1]], axis=0)
    window = jnp.concatenate([window, window[::-1]], axis=0)
    return feats, window


def _outproj_kernel(x_ref, mod_ref, oa_ref, ob_ref, of_ref, obk_ref, od_ref, gate_ref, hg_ref, seg_ref, w_ref,
                    y_ref):
    gt = gate_ref[...]
    sg = gt * jax.nn.sigmoid(gt)
    oc = of_ref[...] + obk_ref[...]
    ss = _segsum(oc * oc, seg_ref[...]) * (1.0 / HGRN_DK)
    oc = oc * lax.rsqrt(ss + EPS) * hg_ref[...]
    acc = _bdot(oa_ref[0].T * sg[:, 0:256], w_ref[0:256, :])
    acc += _bdot(ob_ref[0].T * sg[:, 256:512], w_ref[256:512, :])
    acc += _bdot(oc * sg[:, 512:768], w_ref[512:768, :])
    acc += _bdot(od_ref[...] * sg[:, 768:1024], w_ref[768:1024, :])
    y_ref[...] = x_ref[...] + mod_ref[0, 2:3, :] * acc


def _outproj(x2, mod, ot_a, ot_b, o_f, o_b, out_d, p, hg, seg64, w_out, seq_len):
    n = x2.shape[0]
    tm = min(512, seq_len)
    per_batch = mod.shape[0] > 1
    tps = seq_len // tm
    mod_idx = (lambda i: (i // tps, 0, 0)) if per_batch else (lambda i: (0, 0, 0))
    b256 = pl.BlockSpec((tm, 256), lambda i: (i, 0))
    bt = pl.BlockSpec((1, 256, tm), lambda i: (i // tps, 0, i % tps))
    return pl.pallas_call(
        _outproj_kernel,
        grid=(n // tm,),
        in_specs=[pl.BlockSpec((tm, D_MODEL), lambda i: (i, 0)),
                  pl.BlockSpec((1, 3, D_MODEL), mod_idx),
                  bt, bt, b256, b256, b256,
                  pl.BlockSpec((tm, 1024), lambda i: (i, P_GATE // 1024)),
                  pl.BlockSpec((1, 256), lambda i: (0, 0)),
                  pl.BlockSpec((256, 256), lambda i: (0, 0)),
                  pl.BlockSpec((D_MODEL, D_MODEL), lambda i: (0, 0))],
        out_specs=pl.BlockSpec((tm, D_MODEL), lambda i: (i, 0)),
        out_shape=jax.ShapeDtypeStruct((n, D_MODEL), F32),
        compiler_params=_cparams(("parallel",)),
        name="outproj",
    )(x2, mod, ot_a, ot_b, o_f, o_b, out_d, p, hg, seg64, w_out)


def _layer(x2, mod, lw, consts, batch, seq_len, ctx, rope_tabs, hy):
    n = batch * seq_len
    p = _inproj(x2, mod, lw["norm_g"], lw["w_in"], seq_len)

    q_a, ckvn, krp = _mla_q(p, lw, consts, seq_len, rope_tabs)
    ckv3 = ckvn.reshape(batch, seq_len, MLA_KV_LORA)
    krp3 = krp.reshape(batch, seq_len, LANE)
    k_a, vt_a = _mla_kv(ckvn, krp, lw, consts, batch)
    ks_a, vts_a = [k_a.reshape(batch, seq_len, 512)], [vt_a]
    if ctx is not None:
        lc = ctx[0].shape[1]
        k_c, vt_c = _mla_kv(ctx[0].reshape(batch * lc, MLA_KV_LORA), ctx[1].reshape(batch * lc, LANE), lw, consts,
                            batch)
        ks_a.append(k_c.reshape(batch, lc, 512))
        vts_a.append(vt_c)
    ot_a = _mla_attn(q_a, ks_a, vts_a, batch, seq_len)

    q_b, k_b, kd, vt_b = _diff_prep(p, lw, consts, seq_len, rope_tabs)
    ks_b, vts_b = [k_b.reshape(batch, seq_len, BRANCH)], [vt_b]
    if ctx is not None:
        ks_b.append(ctx[2])
        vts_b.append(ctx[3])
    ot_b = _diff_attn(q_b, ks_b, vts_b, lw["diff_lambda"], lw["subln_col"], lw["lam_init"], batch, seq_len)

    if ctx is not None:
        s0 = ctx[4]
    else:
        s0 = jnp.zeros((batch, 2, HGRN_HEADS, HGRN_DK, HGRN_DK), F32)
    eye = jnp.eye(HGRN_HEADS, dtype=F32)
    st0 = jnp.einsum("bdhke,hg->bdhegk", s0, eye).reshape(batch, 2, BRANCH, BRANCH)
    o_f, o_b, st_out = _hgrn(p, lw["hgrn_lb"], consts["hgrn_dd"], consts["hgrn_right"], st0, batch, seq_len)
    st5 = st_out.reshape(batch, 2, HGRN_HEADS, HGRN_DK, HGRN_HEADS, HGRN_DK)
    states = jnp.stack([st5[:, :, h, :, h, :] for h in range(HGRN_HEADS)], axis=2).swapaxes(-1, -2)

    v_d, x1, x2g = _hy_conv3(p, lw["hy_conv_w"], lw["hy_conv_b"], seq_len)
    taps = _hy_filter(hy["feats"], hy["window"], lw["hy_w1"], lw["hy_b1"], lw["hy_w2"], lw["hy_b2"], lw["hy_w3"],
                      lw["hy_freq"])
    k1n, n1o, n1t = hy["k1n"], hy["n1o"], hy["n1t"]
    ta = _fft_a(hy["fa_full"], taps.reshape(1, n1t, FFT_N2, HY_ORDER * HY_CH), BF16)
    spec = _fft_b(hy["mf"], None, ta.reshape(1, 2, k1n, FFT_N2, HY_ORDER * HY_CH), None, 0)[0]
    z4 = v_d.reshape(batch, n1o, FFT_N2, HY_CH)
    for o, xg in enumerate((x1, x2g)):
        a = _fft_a(hy["fa_half"], z4, BF16).reshape(batch, 2, k1n, FFT_N2, HY_CH)
        pk = _fft_b(hy["mf"], hy["mi"], a, spec, o).reshape(batch, 2 * k1n, FFT_N2, HY_CH)
        z4 = _fft_a_inv(hy["g"], pk, xg.reshape(batch, n1o, FFT_N2, HY_CH), z4, lw["hy_bias"][o:o + 1])
    out_d = z4.reshape(n, HY_CH)

    y = _outproj(x2, mod, ot_a, ot_b, o_f, o_b, out_d, p, lw["hgrn_out_g"], consts["seg64"], lw["w_out"], seq_len)
    new = None
    if ctx is None:
        new = (ckv3, krp3[:, :, KR_OFF:KR_OFF + MLA_ROPE],
               kd.reshape(batch, seq_len, DIFF_HEADS, 2, DIFF_HD),
               p[:, P_DV:P_DV + BRANCH].reshape(batch, seq_len, DIFF_HEADS, 2 * DIFF_HD), states)
    return y, new


def _rope_tables(seq_len):
    half = MLA_ROPE // 2
    inv = ROPE_BASE ** (-jnp.arange(0, half, 2, dtype=F32) / half)
    rows = seq_len // GRID_W
    row = jnp.repeat(jnp.arange(rows, dtype=F32), GRID_W)
    col = (jnp.arange(rows * GRID_W) % GRID_W).astype(F32)
    ar, ac = row[:, None] * inv, col[:, None] * inv
    cos32 = jnp.concatenate([jnp.cos(ar), jnp.cos(ar), jnp.cos(ac), jnp.cos(ac)], axis=-1)
    sin32 = jnp.concatenate([-jnp.sin(ar), jnp.sin(ar), -jnp.sin(ac), jnp.sin(ac)], axis=-1)
    pad = ((0, 0), (KR_OFF, LANE - KR_OFF - MLA_ROPE))
    return dict(cos_mla=jnp.pad(cos32, pad, constant_values=1.0), sin_mla=jnp.pad(sin32, pad),
                cos_diff=jnp.tile(cos32, (1, 2 * DIFF_HEADS)), sin_diff=jnp.tile(sin32, (1, 2 * DIFF_HEADS)))


def _hy_tables(seq_len):
    feats, window = _hy_static(seq_len)
    return dict(feats=feats, window=window, **_fft_tables(seq_len))


def _layer_weights(l, w_in_p, lb, W):
    def head_pad(w, width, per):
        k = w.shape[0]
        w = w.reshape(k, MLA_HEADS, per)[:, :, :width]
        return jnp.pad(w, ((0, 0), (0, 0), (0, LANE - width))).reshape(k, MLA_HEADS * LANE)

    w_ukv = W["mla_w_ukv"][l].reshape(MLA_KV_LORA, MLA_HEADS, MLA_NOPE + MLA_V)
    nope_g, rope_g = W["mla_nope_g"][l], W["mla_rope_g"][l]
    zeros32 = jnp.zeros((MLA_ROPE,), F32)
    zeros64 = jnp.zeros((MLA_NOPE,), F32)
    gq = jnp.tile(jnp.concatenate([nope_g[0], rope_g[0], zeros32]), MLA_HEADS).reshape(1, 512)
    gk = jnp.tile(jnp.concatenate([nope_g[1], zeros64]), MLA_HEADS).reshape(1, 512)
    gkr = jnp.concatenate([zeros64, rope_g[1], zeros32]).reshape(1, LANE)
    return dict(
        norm_g=W["norm_g"][l], w_in=w_in_p[l], w_out=W["w_out"][l].astype(BF16),
        qn_g=W["mla_q_norm_g"][l].reshape(1, -1),
        w_uq=head_pad(W["mla_w_uq"][l], MLA_NOPE + MLA_ROPE, MLA_NOPE + MLA_ROPE).astype(BF16),
        kvn_g=W["mla_kv_norm_g"][l].reshape(1, -1),
        w_uk=jnp.pad(w_ukv[:, :, :MLA_NOPE], ((0, 0), (0, 0), (0, LANE - MLA_NOPE))).reshape(MLA_KV_LORA, 512)
        .astype(BF16),
        w_uv=w_ukv[:, :, MLA_NOPE:].reshape(MLA_KV_LORA, BRANCH).T.astype(BF16),
        gq=gq, gk=gk, gkr=gkr,
        dgq=jnp.tile(W["diff_qk_g"][l, 0], 2 * DIFF_HEADS).reshape(1, BRANCH),
        dgk=jnp.tile(W["diff_qk_g"][l, 1], 2 * DIFF_HEADS).reshape(1, BRANCH),
        diff_lambda=W["diff_lambda"][l], subln_col=W["diff_subln_g"][l].reshape(2 * DIFF_HD, 1),
        lam_init=0.8 - 0.6 * math.exp(-0.3 * l),
        hgrn_lb=lb[:, l].reshape(2, 1, BRANCH),
        hgrn_out_g=jnp.tile(W["hgrn_out_g"][l], HGRN_HEADS).reshape(1, BRANCH),
        hy_conv_w=W["hy_conv_w"][l], hy_conv_b=W["hy_conv_b"][l].reshape(1, -1),
        hy_w1=jnp.pad(W["hy_w1"][l], ((0, LANE - HY_EMB), (0, 0))), hy_b1=W["hy_b1"][l].reshape(1, -1),
        hy_w2=W["hy_w2"][l], hy_b2=W["hy_b2"][l].reshape(1, -1),
        hy_w3=W["hy_w3"][l].reshape(HY_FH, HY_ORDER, 2, HY_CH).transpose(2, 0, 1, 3)
        .reshape(2, HY_FH, HY_ORDER * HY_CH),
        hy_freq=W["hy_sin_freq"][l], hy_bias=W["hy_bias"][l],
    )


def kernel(x_prompt, x_sample, cache_mla_ckv, cache_mla_krope, cache_diff_k, cache_diff_v, state_hgrn, c, c_ctx,
           norm_g, w_mod, b_mod, w_in, w_out, mla_q_norm_g, mla_w_uq, mla_kv_norm_g, mla_w_ukv, mla_nope_g,
           mla_rope_g, diff_qk_g, diff_lambda, diff_subln_g, hgrn_lb_logits, hgrn_out_g, hy_conv_w, hy_conv_b,
           hy_w1, hy_b1, hy_w2, hy_b2, hy_w3, hy_sin_freq, hy_bias):
    W = dict(norm_g=norm_g, w_out=w_out, mla_q_norm_g=mla_q_norm_g, mla_w_uq=mla_w_uq,
             mla_kv_norm_g=mla_kv_norm_g, mla_w_ukv=mla_w_ukv, mla_nope_g=mla_nope_g, mla_rope_g=mla_rope_g,
             diff_qk_g=diff_qk_g, diff_lambda=diff_lambda, diff_subln_g=diff_subln_g, hgrn_out_g=hgrn_out_g,
             hy_conv_w=hy_conv_w, hy_conv_b=hy_conv_b, hy_w1=hy_w1, hy_b1=hy_b1, hy_w2=hy_w2, hy_b2=hy_b2,
             hy_w3=hy_w3, hy_sin_freq=hy_sin_freq, hy_bias=hy_bias)
    bp, lp, _ = x_prompt.shape
    bs, ls, _ = x_sample.shape

    w_in_p = _reorder_in_cols(w_in.astype(BF16))
    cvecs = jnp.concatenate([c_ctx[None, :], c, jnp.zeros((8 - 1 - bs, D_MODEL), F32)], axis=0)
    mods = _mod_all(cvecs, w_mod, b_mod)
    lb = _hgrn_lb(hgrn_lb_logits)
    seg512, cnt512 = _mla_seg()
    hgrn_dd, hgrn_right = _hgrn_consts()
    consts = dict(seg512=seg512, cnt512=cnt512, seg32=_seg_const(BRANCH, DIFF_HD), seg64=_seg_const(BRANCH, HGRN_DK),
                  hgrn_dd=hgrn_dd, hgrn_right=hgrn_right)
    lws = [_layer_weights(l, w_in_p, lb, W) for l in range(DEPTH)]

    hy_p = _hy_tables(lp)
    y = x_prompt.reshape(bp * lp, D_MODEL)
    per_layer = []
    for l in range(DEPTH):
        mod = mods[l, 0:1].reshape(1, 3, D_MODEL)
        y, new = _layer(y, mod, lws[l], consts, bp, lp, None, None, hy_p)
        per_layer.append(new)
    y_prompt = y.reshape(bp, lp, D_MODEL)
    news = [jnp.stack([s[i] for s in per_layer], axis=1) for i in range(5)]

    hy_s = _hy_tables(ls)
    rope_tabs = _rope_tables(ls)
    y = x_sample.reshape(bs * ls, D_MODEL)
    past = cache_mla_ckv.shape[2]
    cache_kr = jnp.pad(cache_mla_krope, ((0, 0), (0, 0), (0, 0), (KR_OFF, LANE - KR_OFF - MLA_ROPE)))
    cache_kb = cache_diff_k.reshape(bs, DEPTH, past, BRANCH).astype(BF16)
    cache_vtb = _vt_with_ones(cache_diff_v.reshape(bs * DEPTH, past, BRANCH).astype(BF16))
    cache_vtb = cache_vtb.reshape(bs, DEPTH, DIFF_HEADS, VT_ROWS, past)
    for l in range(DEPTH):
        mod = mods[l, 1:1 + bs].reshape(bs, 3, D_MODEL)
        ctx = (cache_mla_ckv[:, l], cache_kr[:, l], cache_kb[:, l], cache_vtb[:, l], state_hgrn[:, l])
        y, _ = _layer(y, mod, lws[l], consts, bs, ls, ctx, rope_tabs, hy_s)
    y_sample = y.reshape(bs, ls, D_MODEL)

    return (y_prompt, y_sample, news[0], news[1], news[2], news[3], news[4])
```

```python
import functools
import math

import numpy as np
import jax
import jax.numpy as jnp
from jax import lax
from jax.experimental import pallas as pl
from jax.experimental.pallas import tpu as pltpu

F32 = jnp.float32
BF16 = jnp.bfloat16

D_MODEL = 1024
DEPTH = 4
GRID_W = 64
ROPE_BASE = 10000.0
EPS = 1e-6
BRANCH = 256
MLA_HEADS = 4
MLA_NOPE = 64
MLA_ROPE = 32
MLA_V = 64
MLA_Q_LORA = 256
MLA_KV_LORA = 128
MLA_SCALE = (MLA_NOPE + MLA_ROPE) ** -0.5
DIFF_HEADS = 4
DIFF_HD = 32
DIFF_SCALE = DIFF_HD ** -0.5
HGRN_HEADS = 4
HGRN_DK = 64
HGRN_CHUNK = 128
HGRN_LEVELS = 7
HGRN_ROWS = 2
HGRN_MM_LEVELS = 3
HY_CH = 256
HY_ORDER = 2
HY_EMB = 33
HY_BANDS = 16
HY_FH = 64
HY_DECAY_TARGET = 0.01
HY_FAST_DECAY = 0.3
HY_SLOW_DECAY = 1.5
IN_COLS = 4000

LANE = 128
LOG2E = math.log2(math.e)
VT_ROWS = 80
FFT_N2 = 128
FFT_BLOCK_BYTES = 2 * 1024 * 1024
MLA_AHEAD = 8
MLA_CHUNK = 512
DIFF_AHEAD = 6
DIFF_CHUNK = 256
MLA_SUB = 256
DIFF_SUB = 512
VMEM_LIMIT = 52 * 1024 * 1024

P_CQ, P_CKV, P_KR, P_DQ, P_DK, P_DV = 0, 256, 384, 512, 768, 1024
P_HQ, P_HZF, P_HZB, P_HI, P_HU, P_GATE = 1280, 1536, 1792, 2048, 2304, 3072
P_COLS = 4096
KR_OFF = 64


def _in_col_perm():
    src = np.full((P_COLS,), IN_COLS, np.int32)

    def put(dst, lo, n):
        src[dst:dst + n] = np.arange(lo, lo + n)

    put(P_CQ, 0, 256)
    put(P_CKV, 256, 128)
    put(P_KR + KR_OFF, 384, 32)
    put(P_GATE, 416, 256)
    put(P_DQ, 672, 256)
    put(P_DK, 928, 256)
    put(P_DV, 1184, 256)
    put(P_GATE + 256, 1440, 256)
    put(P_HQ, 1696, 256)
    put(P_HZF, 1952, 256)
    put(P_HZB, 2208, 256)
    put(P_HI, 2464, 256)
    put(P_GATE + 512, 2720, 256)
    put(P_HU, 2976, 768)
    put(P_GATE + 768, 3744, 256)
    return src


def _reorder_in_cols(w):
    src = _in_col_perm()
    pieces, lo = [], 0
    while lo < P_COLS:
        hi = lo + 1
        if src[lo] == IN_COLS:
            while hi < P_COLS and src[hi] == IN_COLS:
                hi += 1
            pieces.append(jnp.zeros(w.shape[:-1] + (hi - lo,), w.dtype))
        else:
            while hi < P_COLS and src[hi] == src[hi - 1] + 1:
                hi += 1
            pieces.append(w[..., int(src[lo]):int(src[lo]) + hi - lo])
        lo = hi
    return jnp.concatenate(pieces, axis=-1)


def _cparams(sem):
    return pltpu.CompilerParams(dimension_semantics=sem, vmem_limit_bytes=VMEM_LIMIT)


def _bdot(a, b):
    return jnp.dot(a.astype(BF16), b.astype(BF16), preferred_element_type=F32)


def _nt(a, b):
    return lax.dot_general(a.astype(BF16), b.astype(BF16), (((1,), (1,)), ((), ())), preferred_element_type=F32)


def _split2(a):
    hi = a.astype(BF16)
    lo = (a - hi.astype(F32)).astype(BF16)
    return hi, lo


def _dot3(a, b):
    ah, al = _split2(a)
    bh, bl = _split2(b)
    d = functools.partial(jnp.dot, preferred_element_type=F32)
    return d(ah, bh) + d(ah, bl) + d(al, bh)


def _segsum(v, seg):
    return jnp.dot(v.astype(BF16), seg, preferred_element_type=F32)


def _rms(x, g):
    return x * lax.rsqrt(jnp.mean(x * x, axis=-1, keepdims=True) + EPS) * g


def _swap8(x):
    w = x.shape[-1]
    lane = lax.broadcasted_iota(jnp.int32, x.shape, x.ndim - 1)
    up = pltpu.roll(x, w - 8, x.ndim - 1)
    dn = pltpu.roll(x, 8, x.ndim - 1)
    return jnp.where((lane & 15) < 8, up, dn)


def _tile_lanes(x, n):
    return x if n == 1 else jnp.concatenate([x] * n, axis=-1)


def _mod_kernel(c_ref, w_ref, b_ref, o_ref):
    c = c_ref[...]
    o_ref[0] = _dot3(c * jax.nn.sigmoid(c), w_ref[0]) + b_ref[0]


def _mod_all(cvecs, w_mod, b_mod):
    nt = 3
    return pl.pallas_call(
        _mod_kernel,
        grid=(DEPTH, nt),
        in_specs=[pl.BlockSpec((8, D_MODEL), lambda l, j: (0, 0)),
                  pl.BlockSpec((1, D_MODEL, D_MODEL), lambda l, j: (l, 0, j)),
                  pl.BlockSpec((1, 1, D_MODEL), lambda l, j: (l, 0, j))],
        out_specs=pl.BlockSpec((1, 8, D_MODEL), lambda l, j: (l, 0, j)),
        out_shape=jax.ShapeDtypeStruct((DEPTH, 8, 3 * D_MODEL), F32),
        compiler_params=_cparams(("arbitrary", "arbitrary")),
        name="mod",
    )(cvecs, w_mod, b_mod.reshape(DEPTH, 1, 3 * D_MODEL))


def _lb_kernel(x_ref, o_ref):
    x = x_ref[...]
    rows = [x[l:l + 1, :] for l in range(DEPTH)]
    m = functools.reduce(jnp.maximum, rows)
    e = [jnp.exp(r - m) for r in rows]
    tot = functools.reduce(lambda a, b: a + b, e)
    acc = jnp.zeros_like(tot)
    o_ref[0:1, :] = acc
    for l in range(1, DEPTH):
        acc = acc + e[l] / tot
        o_ref[l:l + 1, :] = acc


def _hgrn_lb(logits):
    flat = logits.transpose(1, 0, 2).reshape(DEPTH, 2 * BRANCH)
    lb = pl.pallas_call(
        _lb_kernel,
        out_shape=jax.ShapeDtypeStruct(flat.shape, F32),
        name="hgrn_lb",
    )(flat)
    return lb.reshape(DEPTH, 2, BRANCH).transpose(1, 0, 2)


def _inproj_kernel(x_ref, mod_ref, g_ref, w_ref, p_ref):
    h = _rms(x_ref[...], g_ref[...]) * (1.0 + mod_ref[0, 1:2, :]) + mod_ref[0, 0:1, :]
    p_ref[...] = jnp.dot(h.astype(BF16), w_ref[...], preferred_element_type=F32)


def _inproj(x2, mod, norm_g, w_in_p, seq_len):
    n = x2.shape[0]
    tm = min(512, seq_len)
    per_batch = mod.shape[0] > 1
    tiles_per_seq = seq_len // tm
    mod_idx = (lambda i: (i // tiles_per_seq, 0, 0)) if per_batch else (lambda i: (0, 0, 0))
    return pl.pallas_call(
        _inproj_kernel,
        grid=(n // tm,),
        in_specs=[pl.BlockSpec((tm, D_MODEL), lambda i: (i, 0)),
                  pl.BlockSpec((1, 3, D_MODEL), mod_idx),
                  pl.BlockSpec((1, D_MODEL), lambda i: (0, 0)),
                  pl.BlockSpec((D_MODEL, P_COLS), lambda i: (0, 0))],
        out_specs=pl.BlockSpec((tm, P_COLS), lambda i: (i, 0)),
        out_shape=jax.ShapeDtypeStruct((n, P_COLS), F32),
        compiler_params=_cparams(("parallel",)),
        name="inproj",
    )(x2, mod, norm_g.reshape(1, D_MODEL), w_in_p)


def _mla_seg():
    sid = np.zeros((512,), np.int32)
    cnt = np.ones((512,), np.float32)
    for h in range(MLA_HEADS):
        b = 128 * h
        sid[b:b + 64] = 3 * h
        sid[b + 64:b + 96] = 3 * h + 1
        sid[b + 96:b + 128] = 3 * h + 2
        cnt[b:b + 64] = 1.0 / 64
        cnt[b + 64:b + 128] = 1.0 / 32
    seg = (sid[:, None] == sid[None, :]).astype(np.float32)
    return jnp.asarray(seg, BF16), jnp.asarray(cnt.reshape(1, 512))


def _mla_q_kernel(rope, cq_ref, ckv_ref, kr_ref, qng_ref, wuq_ref, kvg_ref, gq_ref, gkr_ref, seg_ref, cnt_ref,
                  *rest):
    if rope:
        cos_ref, sin_ref, q_ref, ckvn_ref, krp_ref = rest
    else:
        q_ref, ckvn_ref, krp_ref = rest
    cqn = _rms(cq_ref[...], qng_ref[...])
    q = _bdot(cqn, wuq_ref[...])
    ss = _segsum(q * q, seg_ref[...]) * cnt_ref[...]
    qn = q * lax.rsqrt(ss + EPS) * gq_ref[...]
    ckvn_ref[...] = _rms(ckv_ref[...], kvg_ref[...])
    kr = kr_ref[...]
    krn = kr * lax.rsqrt(jnp.sum(kr * kr, axis=-1, keepdims=True) * (1.0 / MLA_ROPE) + EPS) * gkr_ref[...]
    if rope:
        cos, sin = cos_ref[...], sin_ref[...]
        qn = qn * _tile_lanes(cos, MLA_HEADS) + _swap8(qn) * _tile_lanes(sin, MLA_HEADS)
        krn = krn * cos + _swap8(krn) * sin
    q_ref[...] = (qn * (MLA_SCALE * LOG2E)).astype(BF16)
    krp_ref[...] = krn


def _mla_q(p, lw, consts, seq_len, rope_tabs):
    n = p.shape[0]
    tm = min(512, seq_len)
    rope = rope_tabs is not None
    full = lambda shape: pl.BlockSpec(shape, lambda i: (0,) * len(shape))
    in_specs = [pl.BlockSpec((tm, 256), lambda i: (i, P_CQ // 256)),
                pl.BlockSpec((tm, 128), lambda i: (i, P_CKV // 128)),
                pl.BlockSpec((tm, 128), lambda i: (i, P_KR // 128)),
                full((1, 256)), full((256, 512)), full((1, 128)), full((1, 512)), full((1, 128)),
                full((512, 512)), full((1, 512))]
    args = [p, p, p, lw["qn_g"], lw["w_uq"], lw["kvn_g"], lw["gq"], lw["gkr"], consts["seg512"], consts["cnt512"]]
    if rope:
        tps = seq_len // tm
        in_specs += [pl.BlockSpec((tm, 128), lambda i: (i % tps, 0))] * 2
        args += [rope_tabs["cos_mla"], rope_tabs["sin_mla"]]
    return pl.pallas_call(
        functools.partial(_mla_q_kernel, rope),
        grid=(n // tm,),
        in_specs=in_specs,
        out_specs=[pl.BlockSpec((tm, 512), lambda i: (i, 0)),
                   pl.BlockSpec((tm, 128), lambda i: (i, 0)),
                   pl.BlockSpec((tm, 128), lambda i: (i, 0))],
        out_shape=[jax.ShapeDtypeStruct((n, 512), BF16),
                   jax.ShapeDtypeStruct((n, 128), F32),
                   jax.ShapeDtypeStruct((n, 128), F32)],
        compiler_params=_cparams(("parallel",)),
        name="mla_q",
    )(*args)


def _store_vt(o_ref, vt):
    tm = vt.shape[1]
    row = lax.broadcasted_iota(jnp.int32, (VT_ROWS - 64, tm), 0)
    extra = jnp.where(row == 0, 1.0, 0.0).astype(BF16)
    for h in range(4):
        o_ref[0, h, 0:64, :] = vt[64 * h:64 * (h + 1)].astype(BF16)
        o_ref[0, h, 64:VT_ROWS, :] = extra


def _mla_kv_kernel(ckvn_ref, krp_ref, wuk_ref, wuv_ref, gk_ref, seg_ref, cnt_ref, k_ref, vt_ref):
    c = ckvn_ref[...].astype(BF16)
    kn = jnp.dot(c, wuk_ref[...], preferred_element_type=F32)
    ss = _segsum(kn * kn, seg_ref[...]) * cnt_ref[...]
    k = kn * lax.rsqrt(ss + EPS) * gk_ref[...] + _tile_lanes(krp_ref[...], MLA_HEADS)
    k_ref[...] = k.astype(BF16)
    _store_vt(vt_ref, _nt(wuv_ref[...], c))


def _mla_kv(ckvn, krp, lw, consts, batch):
    n = ckvn.shape[0]
    lseg = n // batch
    tm = min(512, lseg)
    tpb = lseg // tm
    full = lambda shape: pl.BlockSpec(shape, lambda i: (0,) * len(shape))
    return pl.pallas_call(
        _mla_kv_kernel,
        grid=(n // tm,),
        in_specs=[pl.BlockSpec((tm, 128), lambda i: (i, 0)), pl.BlockSpec((tm, 128), lambda i: (i, 0)),
                  full((128, 512)), full((256, 128)), full((1, 512)), full((512, 512)), full((1, 512))],
        out_specs=[pl.BlockSpec((tm, 512), lambda i: (i, 0)),
                   pl.BlockSpec((1, 4, VT_ROWS, tm), lambda i: (i // tpb, 0, 0, i % tpb))],
        out_shape=[jax.ShapeDtypeStruct((n, 512), BF16), jax.ShapeDtypeStruct((batch, 4, VT_ROWS, lseg), BF16)],
        compiler_params=_cparams(("parallel",)),
        name="mla_kv",
    )(ckvn, krp, lw["w_uk"], lw["w_uv"], lw["gk"], consts["seg512"], consts["cnt512"])


def _softmax_pv(qs, k_refs, vt_refs, key_chunk, sub_rows, n_ahead, k_lane=0, v_head=0):
    where = [(i, lo) for i, r in enumerate(k_refs) for lo in range(0, r.shape[1], key_chunk)]
    nch = len(where)
    nq = len(qs)
    sub = min(sub_rows, key_chunk)
    nsub = key_chunk // sub

    def scores(c, u):
        seg, lo = where[c]
        ks = k_refs[seg][0, lo + u * sub:lo + (u + 1) * sub, k_lane:k_lane + LANE]
        return [_nt(ks, q) for q in qs]

    def chunk_max(s_chunk, j):
        mc = functools.reduce(jnp.maximum, [s_chunk[u][j] for u in range(nsub)])
        return jnp.max(mc, axis=0, keepdims=True)

    s_buf = {c: [scores(c, u) for u in range(nsub)] for c in range(min(n_ahead, nch))}
    m = [None] * nq
    acc = [None] * nq
    m_new = [chunk_max(s_buf[0], j) for j in range(nq)]
    for c in range(nch):
        s_cur = s_buf.pop(c)
        ahead = c + n_ahead
        if ahead < nch:
            s_buf[ahead] = []
        pv = [None] * nq
        for u in range(nsub):
            if ahead < nch:
                s_buf[ahead].append(scores(ahead, u))
            seg, lo = where[c]
            vs = vt_refs[seg][0, v_head, :, lo + u * sub:lo + (u + 1) * sub]
            for j in range(nq):
                part = jnp.dot(vs, jnp.exp2(s_cur[u][j] - m_new[j]).astype(BF16), preferred_element_type=F32)
                pv[j] = part if pv[j] is None else pv[j] + part
        for j in range(nq):
            acc[j] = pv[j] if c == 0 else acc[j] * jnp.exp2(m[j] - m_new[j]) + pv[j]
            m[j] = m_new[j]
        if c + 1 < nch:
            m_new = [jnp.maximum(m[j], chunk_max(s_buf[c + 1], j)) for j in range(nq)]
    return acc


def _mla_attn_kernel(key_chunk, nseg, heads, q_ref, *refs):
    k_refs, vt_refs, o_ref = refs[:nseg], refs[nseg:2 * nseg], refs[2 * nseg]
    for hh in range(heads):
        q = q_ref[:, LANE * hh:LANE * (hh + 1)]
        (acc,) = _softmax_pv([q], k_refs, vt_refs, key_chunk, MLA_SUB, MLA_AHEAD, LANE * hh, hh)
        o_ref[0, MLA_V * hh:MLA_V * (hh + 1)] = acc[0:MLA_V] / acc[MLA_V:MLA_V + 1]


def _key_chunk(lk, rows=512):
    return rows if lk % rows == 0 else lk


def _heads_per_step(ks):
    return 4 if sum(k.shape[1] for k in ks) <= 512 else 1


def _vt_kernel(v_ref, o_ref):
    _store_vt(o_ref, v_ref[0].astype(F32).T)


def _vt_with_ones(v3):
    b, lk, width = v3.shape
    tm = 1536 if lk % 1536 == 0 else lk
    return pl.pallas_call(
        _vt_kernel,
        grid=(b, lk // tm),
        in_specs=[pl.BlockSpec((1, tm, width), lambda i, j: (i, j, 0))],
        out_specs=pl.BlockSpec((1, 4, VT_ROWS, tm), lambda i, j: (i, 0, 0, j)),
        out_shape=jax.ShapeDtypeStruct((b, 4, VT_ROWS, lk), BF16),
        compiler_params=_cparams(("parallel", "parallel")),
        name="vt_ones",
    )(v3)


def _mla_attn(q, ks, vts, batch, seq_len):
    tq = 256
    nq = seq_len // tq
    chunk = _key_chunk(min(k.shape[1] for k in ks), MLA_CHUNK)
    hp = _heads_per_step(ks)
    k_specs = [pl.BlockSpec((1, k.shape[1], LANE * hp), lambda b, h, i: (b, 0, h)) for k in ks]
    v_specs = [pl.BlockSpec((1, hp, VT_ROWS, v.shape[3]), lambda b, h, i: (b, h, 0, 0)) for v in vts]
    return pl.pallas_call(
        functools.partial(_mla_attn_kernel, chunk, len(ks), hp),
        grid=(batch, MLA_HEADS // hp, nq),
        in_specs=[pl.BlockSpec((tq, LANE * hp), lambda b, h, i: (b * nq + i, h))] + k_specs + v_specs,
        out_specs=pl.BlockSpec((1, MLA_V * hp, tq), lambda b, h, i: (b, h, i)),
        out_shape=jax.ShapeDtypeStruct((batch, BRANCH, seq_len), F32),
        compiler_params=_cparams(("parallel", "parallel", "arbitrary")),
        name="mla_attn",
    )(q, *ks, *vts)


def _seg_const(width, seg):
    sid = np.arange(width) // seg
    return jnp.asarray((sid[:, None] == sid[None, :]).astype(np.float32), BF16)


def _diff_prep_kernel(rope, dq_ref, dk_ref, dv_ref, gq_ref, gk_ref, seg_ref, *rest):
    if rope:
        cos_ref, sin_ref, q_ref, k_ref, kf_ref, vt_ref = rest
    else:
        q_ref, k_ref, kf_ref, vt_ref = rest
    seg = seg_ref[...]
    _store_vt(vt_ref, dv_ref[...].T)

    def norm(x, g):
        ss = _segsum(x * x, seg) * (1.0 / DIFF_HD)
        return x * lax.rsqrt(ss + EPS) * g

    q = norm(dq_ref[...], gq_ref[...])
    k = norm(dk_ref[...], gk_ref[...])
    kf_ref[...] = k
    if rope:
        cos, sin = cos_ref[...], sin_ref[...]
        q = q * cos + _swap8(q) * sin
        k = k * cos + _swap8(k) * sin
    q_ref[...] = (q * (DIFF_SCALE * LOG2E)).astype(BF16)
    k_ref[...] = k.astype(BF16)


def _diff_prep(p, lw, consts, seq_len, rope_tabs):
    n = p.shape[0]
    tm = min(512, seq_len)
    rope = rope_tabs is not None
    full = lambda shape: pl.BlockSpec(shape, lambda i: (0,) * len(shape))
    in_specs = [pl.BlockSpec((tm, 256), lambda i: (i, P_DQ // 256)),
                pl.BlockSpec((tm, 256), lambda i: (i, P_DK // 256)),
                pl.BlockSpec((tm, 256), lambda i: (i, P_DV // 256)),
                full((1, 256)), full((1, 256)), full((256, 256))]
    args = [p, p, p, lw["dgq"], lw["dgk"], consts["seg32"]]
    tps = seq_len // tm
    if rope:
        in_specs += [pl.BlockSpec((tm, 256), lambda i: (i % tps, 0))] * 2
        args += [rope_tabs["cos_diff"], rope_tabs["sin_diff"]]
    blk = pl.BlockSpec((tm, 256), lambda i: (i, 0))
    return pl.pallas_call(
        functools.partial(_diff_prep_kernel, rope),
        grid=(n // tm,),
        in_specs=in_specs,
        out_specs=[blk, blk, blk, pl.BlockSpec((1, 4, VT_ROWS, tm), lambda i: (i // tps, 0, 0, i % tps))],
        out_shape=[jax.ShapeDtypeStruct((n, 256), BF16), jax.ShapeDtypeStruct((n, 256), BF16),
                   jax.ShapeDtypeStruct((n, 256), F32),
                   jax.ShapeDtypeStruct((n // seq_len, 4, VT_ROWS, seq_len), BF16)],
        compiler_params=_cparams(("parallel",)),
        name="diff_prep",
    )(*args)


def _attn_prep_kernel(rope, *refs):
    n_mla, n_diff = (12, 8) if rope else (10, 6)
    outs = refs[n_mla + n_diff:]
    _mla_q_kernel(rope, *refs[:n_mla], *outs[:3])
    _diff_prep_kernel(rope, *refs[n_mla:n_mla + n_diff], *outs[3:])


def _attn_prep(p, lw, consts, seq_len, rope_tabs):
    n = p.shape[0]
    tm = min(512, seq_len)
    tps = seq_len // tm
    rope = rope_tabs is not None
    full = lambda shape: pl.BlockSpec(shape, lambda i: (0,) * len(shape))
    col = lambda width, off: pl.BlockSpec((tm, width), lambda i: (i, off // width))
    row = lambda width: pl.BlockSpec((tm, width), lambda i: (i, 0))
    tab = lambda width: pl.BlockSpec((tm, width), lambda i: (i % tps, 0))
    mla_specs = [col(256, P_CQ), col(128, P_CKV), col(128, P_KR),
                 full((1, 256)), full((256, 512)), full((1, 128)), full((1, 512)), full((1, 128)),
                 full((512, 512)), full((1, 512))]
    mla_args = [p, p, p, lw["qn_g"], lw["w_uq"], lw["kvn_g"], lw["gq"], lw["gkr"], consts["seg512"],
                consts["cnt512"]]
    diff_specs = [col(256, P_DQ), col(256, P_DK), col(256, P_DV), full((1, 256)), full((1, 256)), full((256, 256))]
    diff_args = [p, p, p, lw["dgq"], lw["dgk"], consts["seg32"]]
    if rope:
        mla_specs += [tab(128), tab(128)]
        mla_args += [rope_tabs["cos_mla"], rope_tabs["sin_mla"]]
        diff_specs += [tab(256), tab(256)]
        diff_args += [rope_tabs["cos_diff"], rope_tabs["sin_diff"]]
    return pl.pallas_call(
        functools.partial(_attn_prep_kernel, rope),
        grid=(n // tm,),
        in_specs=mla_specs + diff_specs,
        out_specs=[row(512), row(128), row(128), row(256), row(256), row(256),
                   pl.BlockSpec((1, 4, VT_ROWS, tm), lambda i: (i // tps, 0, 0, i % tps))],
        out_shape=[jax.ShapeDtypeStruct((n, 512), BF16), jax.ShapeDtypeStruct((n, 128), F32),
                   jax.ShapeDtypeStruct((n, 128), F32),
                   jax.ShapeDtypeStruct((n, 256), BF16), jax.ShapeDtypeStruct((n, 256), BF16),
                   jax.ShapeDtypeStruct((n, 256), F32),
                   jax.ShapeDtypeStruct((n // seq_len, 4, VT_ROWS, seq_len), BF16)],
        compiler_params=_cparams(("parallel",)),
        name="attn_prep",
    )(*mla_args, *diff_args)


def _diff_attn_kernel(lam_init, key_chunk, nseg, heads, q_ref, *refs):
    k_refs, vt_refs = refs[:nseg], refs[nseg:2 * nseg]
    lp_ref, g_ref, o_ref = refs[2 * nseg:]
    lp = lp_ref[...]
    lam = (jnp.exp(jnp.sum(lp[0:1] * lp[1:2], axis=1, keepdims=True))
           - jnp.exp(jnp.sum(lp[2:3] * lp[3:4], axis=1, keepdims=True)) + lam_init)
    for hh in range(heads):
        blk = 0 if heads == 1 else LANE * (hh // 2)
        base = (pl.program_id(1) % 2) * 64 if heads == 1 else (hh % 2) * 64
        q = q_ref[:, blk:blk + LANE]
        lane = lax.broadcasted_iota(jnp.int32, q.shape, 1)
        zero = jnp.zeros_like(q)

        def map_query(j):
            lo = base + 32 * j
            return jnp.where((lane >= lo) & (lane < lo + 32), q, zero)

        acc0, acc1 = _softmax_pv([map_query(0), map_query(1)], k_refs, vt_refs, key_chunk, DIFF_SUB, DIFF_AHEAD,
                                 blk, hh)
        o = acc0[0:64] / acc0[64:65] - lam * (acc1[0:64] / acc1[64:65])
        ms = jnp.mean(o * o, axis=0, keepdims=True)
        o_ref[0, 64 * hh:64 * (hh + 1)] = o * lax.rsqrt(ms + EPS) * g_ref[...] * (1.0 - lam_init)


def _diff_attn(q, ks, vts, lp, g_col, lam_init, batch, seq_len):
    tq = min(512, seq_len)
    nq = seq_len // tq
    chunk = _key_chunk(min(k.shape[1] for k in ks), DIFF_CHUNK)
    hp = _heads_per_step(ks)
    lanes = LANE if hp == 1 else BRANCH
    k_specs = [pl.BlockSpec((1, k.shape[1], lanes), lambda b, h, i: (b, 0, h // 2)) for k in ks]
    v_specs = [pl.BlockSpec((1, hp, VT_ROWS, v.shape[3]), lambda b, h, i: (b, h, 0, 0)) for v in vts]
    return pl.pallas_call(
        functools.partial(_diff_attn_kernel, lam_init, chunk, len(ks), hp),
        grid=(batch, DIFF_HEADS // hp, nq),
        in_specs=[pl.BlockSpec((tq, lanes), lambda b, h, i: (b * nq + i, h // 2))] + k_specs + v_specs
        + [pl.BlockSpec((4, DIFF_HD), lambda b, h, i: (0, 0)), pl.BlockSpec((64, 1), lambda b, h, i: (0, 0))],
        out_specs=pl.BlockSpec((1, 64 * hp, tq), lambda b, h, i: (b, h, i)),
        out_shape=jax.ShapeDtypeStruct((batch, BRANCH, seq_len), F32),
        compiler_params=_cparams(("parallel", "parallel", "arbitrary")),
        name="diff_attn",
    )(q, *ks, *vts, lp, g_col)


def _hgrn_consts():
    c = HGRN_CHUNK
    t = np.arange(c)
    low = (t[None, :] <= t[:, None]).astype(np.float32)
    blocks = []
    for j in range(HGRN_MM_LEVELS):
        m = 1 << j
        rho = (t // (2 * m)) * (2 * m) + m - 1
        sign = np.where((t // m) % 2 == 1, 1.0, -1.0)[:, None]
        blocks.append(sign * (low - (t[None, :] <= rho[:, None]).astype(np.float32)))
    blocks.append(low)
    fwd = np.concatenate(blocks, axis=0)
    bwd = np.concatenate([b[::-1, ::-1] for b in blocks], axis=0)
    right = np.stack([(t // (1 << j)) % 2 for j in range(HGRN_LEVELS)]).astype(np.float32)
    right = np.stack([right, right[:, ::-1]])
    right = np.broadcast_to(right[..., None], right.shape + (BRANCH,))
    return jnp.asarray(np.stack([fwd, bwd]), BF16), jnp.asarray(right, F32)


def _hgrn_kernel(nc, nb, qf_ref, zf_ref, vf_ref, qb_ref, zb_ref, vb_ref, lb_ref, dd_ref, rm_ref, s0_ref,
                 of_ref, ob_ref, sout_ref, st_ref):
    c = HGRN_CHUNK
    ci = pl.program_id(1)
    chains = [(bi, d) for bi in range(nb) for d in (0, 1)]
    ids = range(len(chains))

    @pl.when(ci == 0)
    def _():
        st_ref[...] = s0_ref[...]

    lane = lax.broadcasted_iota(jnp.int32, (1, BRANCH), 1)
    head_masks = [(lane >= HGRN_DK * h) & (lane < HGRN_DK * (h + 1)) for h in range(HGRN_HEADS)]
    t_idx = lax.broadcasted_iota(jnp.int32, (c, HGRN_HEADS * c), 0)
    s_idx = lax.broadcasted_iota(jnp.int32, (c, HGRN_HEADS * c), 1) & (c - 1)
    pair_xor = t_idx ^ s_idx

    def stack_heads(x):
        xb = x.astype(BF16)
        zero = jnp.zeros_like(xb)
        return jnp.concatenate([jnp.where(hm, xb, zero) for hm in head_masks], axis=0)

    q_refs, z_refs, v_refs = (qf_ref, qb_ref), (zf_ref, zb_ref), (vf_ref, vb_ref)
    q = [q_refs[d][bi] for bi, d in chains]
    v = [v_refs[d][bi] for bi, d in chains]
    z = [z_refs[d][bi] for bi, d in chains]
    lb = [lb_ref[d] for _, d in chains]
    g = [jnp.log(lb[i] + (1.0 - lb[i]) * jax.nn.sigmoid(z[i])) for i in ids]
    kk = [(1.0 - lb[i]) * jax.nn.sigmoid(-z[i]) for i in ids]
    sums = []
    for i in ids:
        gh, gl = _split2(g[i])
        dd = dd_ref[chains[i][1]]
        sums.append(jnp.dot(dd, gh, preferred_element_type=F32) + jnp.dot(dd, gl, preferred_element_type=F32))
    b = [sums[i][HGRN_MM_LEVELS * c:] for i in ids]
    b_tot = [b[i][c - 1:c] if chains[i][1] == 0 else b[i][0:1] for i in ids]

    def neg_abs_decay(i, j):
        if j < HGRN_MM_LEVELS:
            return sums[i][j * c:(j + 1) * c]
        m = 1 << j
        off = m - 1 if chains[i][1] == 0 else m
        ref = jnp.concatenate([jnp.broadcast_to(b[i][g0 + off:g0 + off + 1], (2 * m, BRANCH))
                               for g0 in range(0, c, 2 * m)], axis=0)
        return -jnp.abs(b[i] - ref)

    a = [None] * len(chains)
    for j in reversed(range(HGRN_LEVELS)):
        same_group = pair_xor < (2 << j)
        for i in ids:
            e = jnp.exp(neg_abs_decay(i, j))
            eq = e * rm_ref[chains[i][1], j]
            qt = q[i] * eq
            kt = kk[i] * (e - eq)
            lvl = _nt(qt, stack_heads(kt))
            a[i] = lvl if a[i] is None else jnp.where(same_group, lvl, a[i])
    diagonal = pair_xor == 0
    for i in ids:
        a[i] = jnp.where(diagonal, _nt(q[i], stack_heads(kk[i])), a[i])

    outs = (of_ref, ob_ref)
    for i in ids:
        bi, d = chains[i]
        o = jnp.dot(a[i].astype(BF16), stack_heads(v[i]), preferred_element_type=F32)
        outs[d][bi] = o + _nt(q[i] * jnp.exp(b[i]), st_ref[bi, d])

    r2 = lax.broadcasted_iota(jnp.int32, (BRANCH, BRANCH), 0) // HGRN_DK
    c2 = lax.broadcasted_iota(jnp.int32, (BRANCH, BRANCH), 1) // HGRN_DK
    for i in ids:
        bi, d = chains[i]
        kd = kk[i] * jnp.exp(b_tot[i] - b[i])
        upd = lax.dot_general(v[i].astype(BF16), kd.astype(BF16), (((0,), (0,)), ((), ())),
                              preferred_element_type=F32)
        st_new = st_ref[bi, d] * jnp.exp(b_tot[i]) + jnp.where(r2 == c2, upd, 0.0)
        st_ref[bi, d] = st_new

        @pl.when(ci == nc - 1)
        def _(bi=bi, d=d, st_new=st_new):
            sout_ref[bi, d] = st_new


def _hgrn(p, lb_l, dd, rm, st0, batch, seq_len):
    n = p.shape[0]
    c = HGRN_CHUNK
    nc = seq_len // c
    nb = HGRN_ROWS if batch % HGRN_ROWS == 0 else 1
    p3 = p.reshape(batch, seq_len, P_COLS)
    fwd = lambda col: pl.BlockSpec((nb, c, 256), lambda b, i: (b, i, col))
    bwd = lambda col: pl.BlockSpec((nb, c, 256), lambda b, i: (b, nc - 1 - i, col))
    whole = lambda shape: pl.BlockSpec(shape, lambda b, i: (0,) * len(shape))
    state = pl.BlockSpec((nb, 2, 256, 256), lambda b, i: (b, 0, 0, 0))
    o_f, o_b, st = pl.pallas_call(
        functools.partial(_hgrn_kernel, nc, nb),
        grid=(batch // nb, nc),
        in_specs=[fwd(P_HQ // 256), fwd(P_HZF // 256), fwd(P_HI // 256),
                  bwd(P_HQ // 256), bwd(P_HZB // 256), bwd(P_HI // 256),
                  whole((2, 1, 256)), whole((2, (HGRN_MM_LEVELS + 1) * c, c)),
                  whole((2, HGRN_LEVELS, c, 256)), state],
        out_specs=[pl.BlockSpec((nb, c, 256), lambda b, i: (b, i, 0)),
                   pl.BlockSpec((nb, c, 256), lambda b, i: (b, nc - 1 - i, 0)), state],
        out_shape=[jax.ShapeDtypeStruct((batch, seq_len, 256), F32),
                   jax.ShapeDtypeStruct((batch, seq_len, 256), F32),
                   jax.ShapeDtypeStruct((batch, 2, 256, 256), F32)],
        scratch_shapes=[pltpu.VMEM((nb, 2, 256, 256), F32)],
        compiler_params=_cparams(("parallel", "arbitrary")),
        name="hgrn",
    )(p3, p3, p3, p3, p3, p3, lb_l, dd, rm, st0)
    return o_f.reshape(n, 256), o_b.reshape(n, 256), st


def _hy_conv3_kernel(tiles_per_seq, above_ref, cur_ref, below_ref, w_ref, b_ref, v_ref, x1_ref, x2_ref):
    i = pl.program_id(0)
    cur = cur_ref[...]
    tm = cur.shape[0]
    first = (i % tiles_per_seq) == 0
    last = (i % tiles_per_seq) == tiles_per_seq - 1
    above = jnp.where(first, 0.0, above_ref[7:8, :])
    below = jnp.where(last, 0.0, below_ref[0:1, :])
    row = lax.broadcasted_iota(jnp.int32, (tm, 1), 0)
    prev = jnp.where(row == 0, above, pltpu.roll(cur, 1, 0))
    nxt = jnp.where(row == tm - 1, below, pltpu.roll(cur, tm - 1, 0))
    w = w_ref[...]
    u = prev * w[0:1] + cur * w[1:2] + nxt * w[2:3] + b_ref[...]
    v_ref[...] = u[:, 0:256]
    x1_ref[...] = u[:, 256:512]
    x2_ref[...] = u[:, 512:768]


def _hy_conv3(p, w, b, seq_len):
    n = p.shape[0]
    tm = min(512, seq_len)
    nt = n // tm
    g = tm // 8
    col = P_HU // 768
    oblk = pl.BlockSpec((tm, 256), lambda i: (i, 0))
    return pl.pallas_call(
        functools.partial(_hy_conv3_kernel, seq_len // tm),
        grid=(nt,),
        in_specs=[pl.BlockSpec((8, 768), lambda i: (jnp.maximum(i * g - 1, 0), col)),
                  pl.BlockSpec((tm, 768), lambda i: (i, col)),
                  pl.BlockSpec((8, 768), lambda i: (jnp.minimum((i + 1) * g, nt * g - 1), col)),
                  pl.BlockSpec((3, 768), lambda i: (0, 0)), pl.BlockSpec((1, 768), lambda i: (0, 0))],
        out_specs=[oblk, oblk, oblk],
        out_shape=[jax.ShapeDtypeStruct((n, 256), F32)] * 3,
        compiler_params=_cparams(("parallel",)),
        name="hy_conv3",
    )(p, p, p, w, b)


def _hy_filter_kernel(feat_ref, w1_ref, b1_ref, w2_ref, b2_ref, w3_ref, fr_ref, win_ref, o_ref):
    fr = fr_ref[...]
    h = jnp.sin(fr[0:1] * (_dot3(feat_ref[...], w1_ref[...]) + b1_ref[...]))
    h = jnp.sin(fr[1:2] * (_dot3(h, w2_ref[...]) + b2_ref[...]))
    o_ref[...] = _dot3(h, w3_ref[0]) * _tile_lanes(win_ref[...], HY_ORDER)


def _hy_filter(feats2, window2, w1p, b1, w2, b2, w3d, freq):
    l2 = feats2.shape[0]
    ln = l2 // 2
    tm = min(512, ln)
    full = lambda shape: pl.BlockSpec(shape, lambda i: (0,) * len(shape))
    return pl.pallas_call(
        _hy_filter_kernel,
        grid=(l2 // tm,),
        in_specs=[pl.BlockSpec((tm, LANE), lambda i: (i, 0)),
                  full((LANE, HY_FH)), full((1, HY_FH)), full((HY_FH, HY_FH)), full((1, HY_FH)),
                  pl.BlockSpec((1, HY_FH, HY_ORDER * HY_CH), lambda i: (i // (ln // tm), 0, 0)), full((2, HY_FH)),
                  pl.BlockSpec((tm, HY_CH), lambda i: (i, 0))],
        out_specs=pl.BlockSpec((tm, HY_ORDER * HY_CH), lambda i: (i, 0)),
        out_shape=jax.ShapeDtypeStruct((l2, HY_ORDER * HY_CH), F32),
        compiler_params=_cparams(("parallel",)),
        name="hy_filter",
    )(feats2, w1p, b1, w2, b2, w3d, freq, window2)


def _fft_blocking(nb, n1, n2, ch):
    per_batch = n1 * n2 * ch * 4
    if per_batch <= FFT_BLOCK_BYTES:
        bb = max(1, min(nb, FFT_BLOCK_BYTES // per_batch))
        while nb % bb:
            bb -= 1
        return bb, n2
    rt = n2
    while n1 * rt * ch * 4 > FFT_BLOCK_BYTES and rt > 8:
        rt //= 2
    return 1, rt


def _fft_a_kernel(f_ref, x_ref, o_ref):
    f = f_ref[...]
    for b in range(x_ref.shape[0]):
        x = x_ref[b].astype(BF16)
        o_ref[b] = jnp.einsum("kn,nrc->krc", f, x, preferred_element_type=F32).astype(o_ref.dtype)


def _fft_a(fa, x4, out_dtype):
    nb, n1, n2, ch = x4.shape
    r = fa.shape[0]
    bb, rt = _fft_blocking(nb, n1, n2, ch)
    return pl.pallas_call(
        _fft_a_kernel,
        grid=(nb // bb, n2 // rt),
        in_specs=[pl.BlockSpec((r, n1), lambda b, i: (0, 0)),
                  pl.BlockSpec((bb, n1, rt, ch), lambda b, i: (b, 0, i, 0))],
        out_specs=pl.BlockSpec((bb, r, rt, ch), lambda b, i: (b, 0, i, 0)),
        out_shape=jax.ShapeDtypeStruct((nb, r, n2, ch), out_dtype),
        compiler_params=_cparams(("parallel", "parallel")),
        name="fft_a",
    )(fa, x4)


def _fft_b_kernel(with_inverse, mf_ref, *rest):
    if with_inverse:
        mi_ref, a_ref, h_ref, o_ref = rest
    else:
        a_ref, o_ref = rest
    half = FFT_N2
    for kk in range(a_ref.shape[2]):
        for b in range(a_ref.shape[0]):
            a = jnp.concatenate([a_ref[b, 0, kk], a_ref[b, 1, kk]], axis=0)
            x = jnp.dot(mf_ref[kk], a.astype(BF16), preferred_element_type=F32)
            if with_inverse:
                xr, xi = x[:half], x[half:]
                hr, hi = h_ref[0, kk], h_ref[1, kk]
                y = jnp.concatenate([xr * hr - xi * hi, xr * hi + xi * hr], axis=0)
                x = jnp.dot(mi_ref[kk], y.astype(BF16), preferred_element_type=F32)
            o_ref[b, 0, kk] = x[:half].astype(o_ref.dtype)
            o_ref[b, 1, kk] = x[half:].astype(o_ref.dtype)


def _fft_b(mf, mi, a5, spec, order):
    nb, _, k1n, n2, ch = a5.shape
    ks = 3 if (k1n % 3 == 0 and nb * ch <= 1024) else 1
    mat = pl.BlockSpec((ks, 2 * n2, 2 * n2), lambda k: (k, 0, 0))
    blk = pl.BlockSpec((nb, 2, ks, n2, ch), lambda k: (0, 0, k, 0, 0))
    if spec is None:
        in_specs, args = [mat, blk], [mf, a5]
    else:
        in_specs = [mat, mat, blk, pl.BlockSpec((2, ks, n2, ch), lambda k: (0, k, 0, order))]
        args = [mf, mi, a5, spec]
    return pl.pallas_call(
        functools.partial(_fft_b_kernel, spec is not None),
        grid=(k1n // ks,),
        in_specs=in_specs,
        out_specs=blk,
        out_shape=jax.ShapeDtypeStruct(a5.shape, F32 if spec is None else BF16),
        compiler_params=_cparams(("parallel",)),
        name="fft_b",
    )(*args)


def _fft_a_inv_kernel(g_ref, p_ref, x_ref, z_ref, bias_ref, o_ref):
    g = g_ref[...]
    for b in range(p_ref.shape[0]):
        conv = jnp.einsum("nk,krc->nrc", g, p_ref[b], preferred_element_type=F32)
        o_ref[b] = x_ref[b] * (conv + z_ref[b] * bias_ref[...])


def _fft_a_inv(g, p4, xg4, z4, bias):
    nb, n1, n2, ch = z4.shape
    r = g.shape[1]
    bb, rt = _fft_blocking(nb, n1, n2, ch)
    blk = pl.BlockSpec((bb, n1, rt, ch), lambda b, i: (b, 0, i, 0))
    return pl.pallas_call(
        _fft_a_inv_kernel,
        grid=(nb // bb, n2 // rt),
        in_specs=[pl.BlockSpec((n1, r), lambda b, i: (0, 0)),
                  pl.BlockSpec((bb, r, rt, ch), lambda b, i: (b, 0, i, 0)), blk, blk,
                  pl.BlockSpec((1, 1, ch), lambda b, i: (0, 0, 0))],
        out_specs=blk,
        out_shape=jax.ShapeDtypeStruct(z4.shape, F32),
        compiler_params=_cparams(("parallel", "parallel")),
        name="fft_a_inv",
    )(g, p4, xg4, z4, bias.reshape(1, 1, ch))


def _fft_tables(ln):
    n = 2 * ln
    n1t = n // FFT_N2
    k1n = n1t // 2 + 1
    kk = np.arange(k1n)

    def stage_a(n1_in):
        ang = 2.0 * np.pi * ((kk[:, None] * np.arange(n1_in)[None, :]) % n1t) / n1t
        return jnp.asarray(np.concatenate([np.cos(ang), -np.sin(ang)], axis=0), BF16)

    n1o = n1t // 2
    ang = 2.0 * np.pi * ((np.arange(n1o)[:, None] * kk[None, :]) % n1t) / n1t
    edge = (kk == 0) | (kk == n1t // 2)
    ck = np.where(edge, 1.0, 2.0) / n
    g = jnp.asarray(np.concatenate([ck * np.cos(ang), -ck * np.where(edge, 0.0, np.sin(ang))], axis=1), BF16)

    k1 = jnp.arange(k1n, dtype=jnp.int32)[:, None, None]
    k2 = jnp.arange(FFT_N2, dtype=jnp.int32)[None, :, None]
    n2 = jnp.arange(FFT_N2, dtype=jnp.int32)[None, None, :]
    th = (2.0 * math.pi / n) * ((n2 * (k1 + n1t * k2)) % n).astype(F32)
    c, s = jnp.cos(th), jnp.sin(th)
    mf = jnp.concatenate([jnp.concatenate([c, s], axis=2), jnp.concatenate([-s, c], axis=2)], axis=1)
    return dict(fa_half=stage_a(n1o), fa_full=stage_a(n1t), g=g, mf=mf.astype(BF16),
                mi=mf.transpose(0, 2, 1).astype(BF16), k1n=k1n, n1o=n1o, n1t=n1t)


def _hy_static(ln):
    t = jnp.linspace(0.0, 1.0, ln, dtype=F32)[:, None]
    w = 2.0 * math.pi * jnp.arange(ln, dtype=F32) / ln
    f = jnp.linspace(1e-4, HY_BANDS - 1, HY_BANDS, dtype=F32)
    ang = w[:, None] * f[None, :]
    feats = jnp.concatenate([t, jnp.cos(ang), -jnp.sin(ang)], axis=-1)
    feats = jnp.pad(feats, ((0, 0), (0, LANE - HY_EMB)))
    min_decay = math.log(HY_DECAY_TARGET) / HY_SLOW_DECAY
    max_decay = math.log(HY_DECAY_TARGET) / HY_FAST_DECAY
    deltas = jnp.linspace(min_decay, max_decay, HY_CH, dtype=F32)
    window = jnp.exp(-t * jnp.abs(deltas))
    feats = jnp.concatenate([feats, feats[::-1]], axis=0)
    window = jnp.concatenate([window, window[::-1]], axis=0)
    return feats, window


def _outproj_kernel(x_ref, mod_ref, oa_ref, ob_ref, of_ref, obk_ref, od_ref, gate_ref, hg_ref, seg_ref, w_ref,
                    y_ref):
    gt = gate_ref[...]
    sg = gt * jax.nn.sigmoid(gt)
    oc = of_ref[...] + obk_ref[...]
    ss = _segsum(oc * oc, seg_ref[...]) * (1.0 / HGRN_DK)
    oc = oc * lax.rsqrt(ss + EPS) * hg_ref[...]
    acc = _bdot(oa_ref[0].T * sg[:, 0:256], w_ref[0:256, :])
    acc += _bdot(ob_ref[0].T * sg[:, 256:512], w_ref[256:512, :])
    acc += _bdot(oc * sg[:, 512:768], w_ref[512:768, :])
    acc += _bdot(od_ref[...] * sg[:, 768:1024], w_ref[768:1024, :])
    y_ref[...] = x_ref[...] + mod_ref[0, 2:3, :] * acc


def _outproj(x2, mod, ot_a, ot_b, o_f, o_b, out_d, p, hg, seg64, w_out, seq_len):
    n = x2.shape[0]
    tm = min(512, seq_len)
    per_batch = mod.shape[0] > 1
    tps = seq_len // tm
    mod_idx = (lambda i: (i // tps, 0, 0)) if per_batch else (lambda i: (0, 0, 0))
    b256 = pl.BlockSpec((tm, 256), lambda i: (i, 0))
    bt = pl.BlockSpec((1, 256, tm), lambda i: (i // tps, 0, i % tps))
    return pl.pallas_call(
        _outproj_kernel,
        grid=(n // tm,),
        in_specs=[pl.BlockSpec((tm, D_MODEL), lambda i: (i, 0)),
                  pl.BlockSpec((1, 3, D_MODEL), mod_idx),
                  bt, bt, b256, b256, b256,
                  pl.BlockSpec((tm, 1024), lambda i: (i, P_GATE // 1024)),
                  pl.BlockSpec((1, 256), lambda i: (0, 0)),
                  pl.BlockSpec((256, 256), lambda i: (0, 0)),
                  pl.BlockSpec((D_MODEL, D_MODEL), lambda i: (0, 0))],
        out_specs=pl.BlockSpec((tm, D_MODEL), lambda i: (i, 0)),
        out_shape=jax.ShapeDtypeStruct((n, D_MODEL), F32),
        compiler_params=_cparams(("parallel",)),
        name="outproj",
    )(x2, mod, ot_a, ot_b, o_f, o_b, out_d, p, hg, seg64, w_out)


def _layer(x2, mod, lw, consts, batch, seq_len, ctx, rope_tabs, hy):
    n = batch * seq_len
    p = _inproj(x2, mod, lw["norm_g"], lw["w_in"], seq_len)

    q_a, ckvn, krp, q_b, k_b, kd, vt_b = _attn_prep(p, lw, consts, seq_len, rope_tabs)
    ckv3 = ckvn.reshape(batch, seq_len, MLA_KV_LORA)
    krp3 = krp.reshape(batch, seq_len, LANE)
    k_a, vt_a = _mla_kv(ckvn, krp, lw, consts, batch)
    ks_a, vts_a = [k_a.reshape(batch, seq_len, 512)], [vt_a]
    if ctx is not None:
        lc = ctx[0].shape[1]
        k_c, vt_c = _mla_kv(ctx[0].reshape(batch * lc, MLA_KV_LORA), ctx[1].reshape(batch * lc, LANE), lw, consts,
                            batch)
        ks_a.append(k_c.reshape(batch, lc, 512))
        vts_a.append(vt_c)
    ot_a = _mla_attn(q_a, ks_a, vts_a, batch, seq_len)

    ks_b, vts_b = [k_b.reshape(batch, seq_len, BRANCH)], [vt_b]
    if ctx is not None:
        ks_b.append(ctx[2])
        vts_b.append(ctx[3])
    ot_b = _diff_attn(q_b, ks_b, vts_b, lw["diff_lambda"], lw["subln_col"], lw["lam_init"], batch, seq_len)

    if ctx is not None:
        s0 = ctx[4]
    else:
        s0 = jnp.zeros((batch, 2, HGRN_HEADS, HGRN_DK, HGRN_DK), F32)
    eye = jnp.eye(HGRN_HEADS, dtype=F32)
    st0 = jnp.einsum("bdhke,hg->bdhegk", s0, eye).reshape(batch, 2, BRANCH, BRANCH)
    o_f, o_b, st_out = _hgrn(p, lw["hgrn_lb"], consts["hgrn_dd"], consts["hgrn_right"], st0, batch, seq_len)
    st5 = st_out.reshape(batch, 2, HGRN_HEADS, HGRN_DK, HGRN_HEADS, HGRN_DK)
    states = jnp.stack([st5[:, :, h, :, h, :] for h in range(HGRN_HEADS)], axis=2).swapaxes(-1, -2)

    v_d, x1, x2g = _hy_conv3(p, lw["hy_conv_w"], lw["hy_conv_b"], seq_len)
    taps = _hy_filter(hy["feats"], hy["window"], lw["hy_w1"], lw["hy_b1"], lw["hy_w2"], lw["hy_b2"], lw["hy_w3"],
                      lw["hy_freq"])
    k1n, n1o, n1t = hy["k1n"], hy["n1o"], hy["n1t"]
    ta = _fft_a(hy["fa_full"], taps.reshape(1, n1t, FFT_N2, HY_ORDER * HY_CH), BF16)
    spec = _fft_b(hy["mf"], None, ta.reshape(1, 2, k1n, FFT_N2, HY_ORDER * HY_CH), None, 0)[0]
    z4 = v_d.reshape(batch, n1o, FFT_N2, HY_CH)
    for o, xg in enumerate((x1, x2g)):
        a = _fft_a(hy["fa_half"], z4, BF16).reshape(batch, 2, k1n, FFT_N2, HY_CH)
        pk = _fft_b(hy["mf"], hy["mi"], a, spec, o).reshape(batch, 2 * k1n, FFT_N2, HY_CH)
        z4 = _fft_a_inv(hy["g"], pk, xg.reshape(batch, n1o, FFT_N2, HY_CH), z4, lw["hy_bias"][o:o + 1])
    out_d = z4.reshape(n, HY_CH)

    y = _outproj(x2, mod, ot_a, ot_b, o_f, o_b, out_d, p, lw["hgrn_out_g"], consts["seg64"], lw["w_out"], seq_len)
    new = None
    if ctx is None:
        new = (ckv3, krp3[:, :, KR_OFF:KR_OFF + MLA_ROPE],
               kd.reshape(batch, seq_len, DIFF_HEADS, 2, DIFF_HD),
               p[:, P_DV:P_DV + BRANCH].reshape(batch, seq_len, DIFF_HEADS, 2 * DIFF_HD), states)
    return y, new


def _rope_tables(seq_len):
    half = MLA_ROPE // 2
    inv = ROPE_BASE ** (-jnp.arange(0, half, 2, dtype=F32) / half)
    rows = seq_len // GRID_W
    row = jnp.repeat(jnp.arange(rows, dtype=F32), GRID_W)
    col = (jnp.arange(rows * GRID_W) % GRID_W).astype(F32)
    ar, ac = row[:, None] * inv, col[:, None] * inv
    cos32 = jnp.concatenate([jnp.cos(ar), jnp.cos(ar), jnp.cos(ac), jnp.cos(ac)], axis=-1)
    sin32 = jnp.concatenate([-jnp.sin(ar), jnp.sin(ar), -jnp.sin(ac), jnp.sin(ac)], axis=-1)
    pad = ((0, 0), (KR_OFF, LANE - KR_OFF - MLA_ROPE))
    return dict(cos_mla=jnp.pad(cos32, pad, constant_values=1.0), sin_mla=jnp.pad(sin32, pad),
                cos_diff=jnp.tile(cos32, (1, 2 * DIFF_HEADS)), sin_diff=jnp.tile(sin32, (1, 2 * DIFF_HEADS)))


def _hy_tables(seq_len):
    feats, window = _hy_static(seq_len)
    return dict(feats=feats, window=window, **_fft_tables(seq_len))


def _layer_weights(l, w_in_p, lb, W):
    def head_pad(w, width, per):
        k = w.shape[0]
        w = w.reshape(k, MLA_HEADS, per)[:, :, :width]
        return jnp.pad(w, ((0, 0), (0, 0), (0, LANE - width))).reshape(k, MLA_HEADS * LANE)

    w_ukv = W["mla_w_ukv"][l].reshape(MLA_KV_LORA, MLA_HEADS, MLA_NOPE + MLA_V)
    nope_g, rope_g = W["mla_nope_g"][l], W["mla_rope_g"][l]
    zeros32 = jnp.zeros((MLA_ROPE,), F32)
    zeros64 = jnp.zeros((MLA_NOPE,), F32)
    gq = jnp.tile(jnp.concatenate([nope_g[0], rope_g[0], zeros32]), MLA_HEADS).reshape(1, 512)
    gk = jnp.tile(jnp.concatenate([nope_g[1], zeros64]), MLA_HEADS).reshape(1, 512)
    gkr = jnp.concatenate([zeros64, rope_g[1], zeros32]).reshape(1, LANE)
    return dict(
        norm_g=W["norm_g"][l], w_in=w_in_p[l], w_out=W["w_out"][l].astype(BF16),
        qn_g=W["mla_q_norm_g"][l].reshape(1, -1),
        w_uq=head_pad(W["mla_w_uq"][l], MLA_NOPE + MLA_ROPE, MLA_NOPE + MLA_ROPE).astype(BF16),
        kvn_g=W["mla_kv_norm_g"][l].reshape(1, -1),
        w_uk=jnp.pad(w_ukv[:, :, :MLA_NOPE], ((0, 0), (0, 0), (0, LANE - MLA_NOPE))).reshape(MLA_KV_LORA, 512)
        .astype(BF16),
        w_uv=w_ukv[:, :, MLA_NOPE:].reshape(MLA_KV_LORA, BRANCH).T.astype(BF16),
        gq=gq, gk=gk, gkr=gkr,
        dgq=jnp.tile(W["diff_qk_g"][l, 0], 2 * DIFF_HEADS).reshape(1, BRANCH),
        dgk=jnp.tile(W["diff_qk_g"][l, 1], 2 * DIFF_HEADS).reshape(1, BRANCH),
        diff_lambda=W["diff_lambda"][l], subln_col=W["diff_subln_g"][l].reshape(2 * DIFF_HD, 1),
        lam_init=0.8 - 0.6 * math.exp(-0.3 * l),
        hgrn_lb=lb[:, l].reshape(2, 1, BRANCH),
        hgrn_out_g=jnp.tile(W["hgrn_out_g"][l], HGRN_HEADS).reshape(1, BRANCH),
        hy_conv_w=W["hy_conv_w"][l], hy_conv_b=W["hy_conv_b"][l].reshape(1, -1),
        hy_w1=jnp.pad(W["hy_w1"][l], ((0, LANE - HY_EMB), (0, 0))), hy_b1=W["hy_b1"][l].reshape(1, -1),
        hy_w2=W["hy_w2"][l], hy_b2=W["hy_b2"][l].reshape(1, -1),
        hy_w3=W["hy_w3"][l].reshape(HY_FH, HY_ORDER, 2, HY_CH).transpose(2, 0, 1, 3)
        .reshape(2, HY_FH, HY_ORDER * HY_CH),
        hy_freq=W["hy_sin_freq"][l], hy_bias=W["hy_bias"][l],
    )


def kernel(x_prompt, x_sample, cache_mla_ckv, cache_mla_krope, cache_diff_k, cache_diff_v, state_hgrn, c, c_ctx,
           norm_g, w_mod, b_mod, w_in, w_out, mla_q_norm_g, mla_w_uq, mla_kv_norm_g, mla_w_ukv, mla_nope_g,
           mla_rope_g, diff_qk_g, diff_lambda, diff_subln_g, hgrn_lb_logits, hgrn_out_g, hy_conv_w, hy_conv_b,
           hy_w1, hy_b1, hy_w2, hy_b2, hy_w3, hy_sin_freq, hy_bias):
    W = dict(norm_g=norm_g, w_out=w_out, mla_q_norm_g=mla_q_norm_g, mla_w_uq=mla_w_uq,
             mla_kv_norm_g=mla_kv_norm_g, mla_w_ukv=mla_w_ukv, mla_nope_g=mla_nope_g, mla_rope_g=mla_rope_g,
             diff_qk_g=diff_qk_g, diff_lambda=diff_lambda, diff_subln_g=diff_subln_g, hgrn_out_g=hgrn_out_g,
             hy_conv_w=hy_conv_w, hy_conv_b=hy_conv_b, hy_w1=hy_w1, hy_b1=hy_b1, hy_w2=hy_w2, hy_b2=hy_b2,
             hy_w3=hy_w3, hy_sin_freq=hy_sin_freq, hy_bias=hy_bias)
    bp, lp, _ = x_prompt.shape
    bs, ls, _ = x_sample.shape

    w_in_p = _reorder_in_cols(w_in.astype(BF16))
    cvecs = jnp.concatenate([c_ctx[None, :], c, jnp.zeros((8 - 1 - bs, D_MODEL), F32)], axis=0)
    mods = _mod_all(cvecs, w_mod, b_mod)
    lb = _hgrn_lb(hgrn_lb_logits)
    seg512, cnt512 = _mla_seg()
    hgrn_dd, hgrn_right = _hgrn_consts()
    consts = dict(seg512=seg512, cnt512=cnt512, seg32=_seg_const(BRANCH, DIFF_HD), seg64=_seg_const(BRANCH, HGRN_DK),
                  hgrn_dd=hgrn_dd, hgrn_right=hgrn_right)
    lws = [_layer_weights(l, w_in_p, lb, W) for l in range(DEPTH)]

    hy_p = _hy_tables(lp)
    y = x_prompt.reshape(bp * lp, D_MODEL)
    per_layer = []
    for l in range(DEPTH):
        mod = mods[l, 0:1].reshape(1, 3, D_MODEL)
        y, new = _layer(y, mod, lws[l], consts, bp, lp, None, None, hy_p)
        per_layer.append(new)
    y_prompt = y.reshape(bp, lp, D_MODEL)
    news = [jnp.stack([s[i] for s in per_layer], axis=1) for i in range(5)]

    hy_s = _hy_tables(ls)
    rope_tabs = _rope_tables(ls)
    y = x_sample.reshape(bs * ls, D_MODEL)
    past = cache_mla_ckv.shape[2]
    cache_kr = jnp.pad(cache_mla_krope, ((0, 0), (0, 0), (0, 0), (KR_OFF, LANE - KR_OFF - MLA_ROPE)))
    cache_kb = cache_diff_k.reshape(bs, DEPTH, past, BRANCH).astype(BF16)
    cache_vtb = _vt_with_ones(cache_diff_v.reshape(bs * DEPTH, past, BRANCH).astype(BF16))
    cache_vtb = cache_vtb.reshape(bs, DEPTH, DIFF_HEADS, VT_ROWS, past)
    for l in range(DEPTH):
        mod = mods[l, 1:1 + bs].reshape(bs, 3, D_MODEL)
        ctx = (cache_mla_ckv[:, l], cache_kr[:, l], cache_kb[:, l], cache_vtb[:, l], state_hgrn[:, l])
        y, _ = _layer(y, mod, lws[l], consts, bs, ls, ctx, rope_tabs, hy_s)
    y_sample = y.reshape(bs, ls, D_MODEL)

    return (y_prompt, y_sample, news[0], news[1], news[2], news[3], news[4])
```

```python
import functools
import math

import numpy as np
import jax
import jax.numpy as jnp
from jax import lax
from jax.experimental import pallas as pl
from jax.experimental.pallas import tpu as pltpu

F32 = jnp.float32
BF16 = jnp.bfloat16

D_MODEL = 1024
DEPTH = 4
GRID_W = 64
ROPE_BASE = 10000.0
EPS = 1e-6
BRANCH = 256
MLA_HEADS = 4
MLA_NOPE = 64
MLA_ROPE = 32
MLA_V = 64
MLA_Q_LORA = 256
MLA_KV_LORA = 128
MLA_SCALE = (MLA_NOPE + MLA_ROPE) ** -0.5
DIFF_HEADS = 4
DIFF_HD = 32
DIFF_SCALE = DIFF_HD ** -0.5
HGRN_HEADS = 4
HGRN_DK = 64
HGRN_CHUNK = 128
HGRN_LEVELS = 7
HGRN_ROWS = 2
HGRN_MM_LEVELS = 3
HY_CH = 256
HY_ORDER = 2
HY_EMB = 33
HY_BANDS = 16
HY_FH = 64
HY_DECAY_TARGET = 0.01
HY_FAST_DECAY = 0.3
HY_SLOW_DECAY = 1.5
IN_COLS = 4000

LANE = 128
LOG2E = math.log2(math.e)
VT_ROWS = 80
FFT_N2 = 128
FFT_BLOCK_BYTES = 2 * 1024 * 1024
MLA_AHEAD = 8
MLA_CHUNK = 512
DIFF_AHEAD = 6
DIFF_CHUNK = 256
MLA_SUB = 256
DIFF_SUB = 512
VMEM_LIMIT = 52 * 1024 * 1024

P_CQ, P_CKV, P_KR, P_DQ, P_DK, P_DV = 0, 256, 384, 512, 768, 1024
P_HQ, P_HZF, P_HZB, P_HI, P_HU, P_GATE = 1280, 1536, 1792, 2048, 2304, 3072
P_COLS = 4096
KR_OFF = 64


def _in_col_perm():
    src = np.full((P_COLS,), IN_COLS, np.int32)

    def put(dst, lo, n):
        src[dst:dst + n] = np.arange(lo, lo + n)

    put(P_CQ, 0, 256)
    put(P_CKV, 256, 128)
    put(P_KR + KR_OFF, 384, 32)
    put(P_GATE, 416, 256)
    put(P_DQ, 672, 256)
    put(P_DK, 928, 256)
    put(P_DV, 1184, 256)
    put(P_GATE + 256, 1440, 256)
    put(P_HQ, 1696, 256)
    put(P_HZF, 1952, 256)
    put(P_HZB, 2208, 256)
    put(P_HI, 2464, 256)
    put(P_GATE + 512, 2720, 256)
    put(P_HU, 2976, 768)
    put(P_GATE + 768, 3744, 256)
    return src


def _reorder_in_cols(w):
    src = _in_col_perm()
    pieces, lo = [], 0
    while lo < P_COLS:
        hi = lo + 1
        if src[lo] == IN_COLS:
            while hi < P_COLS and src[hi] == IN_COLS:
                hi += 1
            pieces.append(jnp.zeros(w.shape[:-1] + (hi - lo,), w.dtype))
        else:
            while hi < P_COLS and src[hi] == src[hi - 1] + 1:
                hi += 1
            pieces.append(w[..., int(src[lo]):int(src[lo]) + hi - lo])
        lo = hi
    return jnp.concatenate(pieces, axis=-1)


def _cparams(sem):
    return pltpu.CompilerParams(dimension_semantics=sem, vmem_limit_bytes=VMEM_LIMIT)


def _bdot(a, b):
    return jnp.dot(a.astype(BF16), b.astype(BF16), preferred_element_type=F32)


def _nt(a, b):
    return lax.dot_general(a.astype(BF16), b.astype(BF16), (((1,), (1,)), ((), ())), preferred_element_type=F32)


def _split2(a):
    hi = a.astype(BF16)
    lo = (a - hi.astype(F32)).astype(BF16)
    return hi, lo


def _dot3(a, b):
    ah, al = _split2(a)
    bh, bl = _split2(b)
    d = functools.partial(jnp.dot, preferred_element_type=F32)
    return d(ah, bh) + d(ah, bl) + d(al, bh)


def _segsum(v, seg):
    return jnp.dot(v.astype(BF16), seg, preferred_element_type=F32)


def _rms(x, g):
    return x * lax.rsqrt(jnp.mean(x * x, axis=-1, keepdims=True) + EPS) * g


def _swap8(x):
    w = x.shape[-1]
    lane = lax.broadcasted_iota(jnp.int32, x.shape, x.ndim - 1)
    up = pltpu.roll(x, w - 8, x.ndim - 1)
    dn = pltpu.roll(x, 8, x.ndim - 1)
    return jnp.where((lane & 15) < 8, up, dn)


def _tile_lanes(x, n):
    return x if n == 1 else jnp.concatenate([x] * n, axis=-1)


def _mod_kernel(c_ref, w_ref, b_ref, o_ref):
    c = c_ref[...]
    o_ref[0] = _dot3(c * jax.nn.sigmoid(c), w_ref[0]) + b_ref[0]


def _mod_all(cvecs, w_mod, b_mod):
    nt = 3
    return pl.pallas_call(
        _mod_kernel,
        grid=(DEPTH, nt),
        in_specs=[pl.BlockSpec((8, D_MODEL), lambda l, j: (0, 0)),
                  pl.BlockSpec((1, D_MODEL, D_MODEL), lambda l, j: (l, 0, j)),
                  pl.BlockSpec((1, 1, D_MODEL), lambda l, j: (l, 0, j))],
        out_specs=pl.BlockSpec((1, 8, D_MODEL), lambda l, j: (l, 0, j)),
        out_shape=jax.ShapeDtypeStruct((DEPTH, 8, 3 * D_MODEL), F32),
        compiler_params=_cparams(("arbitrary", "arbitrary")),
        name="mod",
    )(cvecs, w_mod, b_mod.reshape(DEPTH, 1, 3 * D_MODEL))


def _lb_kernel(x_ref, o_ref):
    x = x_ref[...]
    rows = [x[l:l + 1, :] for l in range(DEPTH)]
    m = functools.reduce(jnp.maximum, rows)
    e = [jnp.exp(r - m) for r in rows]
    tot = functools.reduce(lambda a, b: a + b, e)
    acc = jnp.zeros_like(tot)
    o_ref[0:1, :] = acc
    for l in range(1, DEPTH):
        acc = acc + e[l] / tot
        o_ref[l:l + 1, :] = acc


def _hgrn_lb(logits):
    flat = logits.transpose(1, 0, 2).reshape(DEPTH, 2 * BRANCH)
    lb = pl.pallas_call(
        _lb_kernel,
        out_shape=jax.ShapeDtypeStruct(flat.shape, F32),
        name="hgrn_lb",
    )(flat)
    return lb.reshape(DEPTH, 2, BRANCH).transpose(1, 0, 2)


def _inproj_kernel(x_ref, mod_ref, g_ref, w_ref, p_ref):
    h = _rms(x_ref[...], g_ref[...]) * (1.0 + mod_ref[0, 1:2, :]) + mod_ref[0, 0:1, :]
    p_ref[...] = jnp.dot(h.astype(BF16), w_ref[...], preferred_element_type=F32)


def _inproj(x2, mod, norm_g, w_in_p, seq_len):
    n = x2.shape[0]
    tm = min(512, seq_len)
    per_batch = mod.shape[0] > 1
    tiles_per_seq = seq_len // tm
    mod_idx = (lambda i: (i // tiles_per_seq, 0, 0)) if per_batch else (lambda i: (0, 0, 0))
    return pl.pallas_call(
        _inproj_kernel,
        grid=(n // tm,),
        in_specs=[pl.BlockSpec((tm, D_MODEL), lambda i: (i, 0)),
                  pl.BlockSpec((1, 3, D_MODEL), mod_idx),
                  pl.BlockSpec((1, D_MODEL), lambda i: (0, 0)),
                  pl.BlockSpec((D_MODEL, P_COLS), lambda i: (0, 0))],
        out_specs=pl.BlockSpec((tm, P_COLS), lambda i: (i, 0)),
        out_shape=jax.ShapeDtypeStruct((n, P_COLS), F32),
        compiler_params=_cparams(("parallel",)),
        name="inproj",
    )(x2, mod, norm_g.reshape(1, D_MODEL), w_in_p)


def _mla_seg():
    sid = np.zeros((512,), np.int32)
    cnt = np.ones((512,), np.float32)
    for h in range(MLA_HEADS):
        b = 128 * h
        sid[b:b + 64] = 3 * h
        sid[b + 64:b + 96] = 3 * h + 1
        sid[b + 96:b + 128] = 3 * h + 2
        cnt[b:b + 64] = 1.0 / 64
        cnt[b + 64:b + 128] = 1.0 / 32
    seg = (sid[:, None] == sid[None, :]).astype(np.float32)
    return jnp.asarray(seg, BF16), jnp.asarray(cnt.reshape(1, 512))


def _mla_q_kernel(rope, cq_ref, ckv_ref, kr_ref, qng_ref, wuq_ref, kvg_ref, gq_ref, gkr_ref, seg_ref, cnt_ref,
                  *rest):
    if rope:
        cos_ref, sin_ref, q_ref, ckvn_ref, krp_ref = rest
    else:
        q_ref, ckvn_ref, krp_ref = rest
    cqn = _rms(cq_ref[...], qng_ref[...])
    q = _bdot(cqn, wuq_ref[...])
    ss = _segsum(q * q, seg_ref[...]) * cnt_ref[...]
    qn = q * lax.rsqrt(ss + EPS) * gq_ref[...]
    ckvn_ref[...] = _rms(ckv_ref[...], kvg_ref[...])
    kr = kr_ref[...]
    krn = kr * lax.rsqrt(jnp.sum(kr * kr, axis=-1, keepdims=True) * (1.0 / MLA_ROPE) + EPS) * gkr_ref[...]
    if rope:
        cos, sin = cos_ref[...], sin_ref[...]
        qn = qn * _tile_lanes(cos, MLA_HEADS) + _swap8(qn) * _tile_lanes(sin, MLA_HEADS)
        krn = krn * cos + _swap8(krn) * sin
    q_ref[...] = (qn * (MLA_SCALE * LOG2E)).astype(BF16)
    krp_ref[...] = krn


def _mla_q(p, lw, consts, seq_len, rope_tabs):
    n = p.shape[0]
    tm = min(512, seq_len)
    rope = rope_tabs is not None
    full = lambda shape: pl.BlockSpec(shape, lambda i: (0,) * len(shape))
    in_specs = [pl.BlockSpec((tm, 256), lambda i: (i, P_CQ // 256)),
                pl.BlockSpec((tm, 128), lambda i: (i, P_CKV // 128)),
                pl.BlockSpec((tm, 128), lambda i: (i, P_KR // 128)),
                full((1, 256)), full((256, 512)), full((1, 128)), full((1, 512)), full((1, 128)),
                full((512, 512)), full((1, 512))]
    args = [p, p, p, lw["qn_g"], lw["w_uq"], lw["kvn_g"], lw["gq"], lw["gkr"], consts["seg512"], consts["cnt512"]]
    if rope:
        tps = seq_len // tm
        in_specs += [pl.BlockSpec((tm, 128), lambda i: (i % tps, 0))] * 2
        args += [rope_tabs["cos_mla"], rope_tabs["sin_mla"]]
    return pl.pallas_call(
        functools.partial(_mla_q_kernel, rope),
        grid=(n // tm,),
        in_specs=in_specs,
        out_specs=[pl.BlockSpec((tm, 512), lambda i: (i, 0)),
                   pl.BlockSpec((tm, 128), lambda i: (i, 0)),
                   pl.BlockSpec((tm, 128), lambda i: (i, 0))],
        out_shape=[jax.ShapeDtypeStruct((n, 512), BF16),
                   jax.ShapeDtypeStruct((n, 128), F32),
                   jax.ShapeDtypeStruct((n, 128), F32)],
        compiler_params=_cparams(("parallel",)),
        name="mla_q",
    )(*args)


def _store_vt(o_ref, vt):
    tm = vt.shape[1]
    row = lax.broadcasted_iota(jnp.int32, (VT_ROWS - 64, tm), 0)
    extra = jnp.where(row == 0, 1.0, 0.0).astype(BF16)
    for h in range(4):
        o_ref[0, h, 0:64, :] = vt[64 * h:64 * (h + 1)].astype(BF16)
        o_ref[0, h, 64:VT_ROWS, :] = extra


def _mla_kv_kernel(ckvn_ref, krp_ref, wuk_ref, wuv_ref, gk_ref, seg_ref, cnt_ref, k_ref, vt_ref):
    c = ckvn_ref[...].astype(BF16)
    kn = jnp.dot(c, wuk_ref[...], preferred_element_type=F32)
    ss = _segsum(kn * kn, seg_ref[...]) * cnt_ref[...]
    k = kn * lax.rsqrt(ss + EPS) * gk_ref[...] + _tile_lanes(krp_ref[...], MLA_HEADS)
    k_ref[...] = k.astype(BF16)
    _store_vt(vt_ref, _nt(wuv_ref[...], c))


def _mla_kv(ckvn, krp, lw, consts, batch):
    n = ckvn.shape[0]
    lseg = n // batch
    tm = min(512, lseg)
    tpb = lseg // tm
    full = lambda shape: pl.BlockSpec(shape, lambda i: (0,) * len(shape))
    return pl.pallas_call(
        _mla_kv_kernel,
        grid=(n // tm,),
        in_specs=[pl.BlockSpec((tm, 128), lambda i: (i, 0)), pl.BlockSpec((tm, 128), lambda i: (i, 0)),
                  full((128, 512)), full((256, 128)), full((1, 512)), full((512, 512)), full((1, 512))],
        out_specs=[pl.BlockSpec((tm, 512), lambda i: (i, 0)),
                   pl.BlockSpec((1, 4, VT_ROWS, tm), lambda i: (i // tpb, 0, 0, i % tpb))],
        out_shape=[jax.ShapeDtypeStruct((n, 512), BF16), jax.ShapeDtypeStruct((batch, 4, VT_ROWS, lseg), BF16)],
        compiler_params=_cparams(("parallel",)),
        name="mla_kv",
    )(ckvn, krp, lw["w_uk"], lw["w_uv"], lw["gk"], consts["seg512"], consts["cnt512"])


def _softmax_pv(qs, k_refs, vt_refs, key_chunk, sub_rows, n_ahead, k_lane=0, v_head=0):
    where = [(i, lo) for i, r in enumerate(k_refs) for lo in range(0, r.shape[1], key_chunk)]
    nch = len(where)
    nq = len(qs)
    sub = min(sub_rows, key_chunk)
    nsub = key_chunk // sub

    def scores(c, u):
        seg, lo = where[c]
        ks = k_refs[seg][0, lo + u * sub:lo + (u + 1) * sub, k_lane:k_lane + LANE]
        return [_nt(ks, q) for q in qs]

    def chunk_max(s_chunk, j):
        mc = functools.reduce(jnp.maximum, [s_chunk[u][j] for u in range(nsub)])
        return jnp.max(mc, axis=0, keepdims=True)

    s_buf = {c: [scores(c, u) for u in range(nsub)] for c in range(min(n_ahead, nch))}
    m = [None] * nq
    acc = [None] * nq
    m_new = [chunk_max(s_buf[0], j) for j in range(nq)]
    for c in range(nch):
        s_cur = s_buf.pop(c)
        ahead = c + n_ahead
        if ahead < nch:
            s_buf[ahead] = []
        pv = [None] * nq
        for u in range(nsub):
            if ahead < nch:
                s_buf[ahead].append(scores(ahead, u))
            seg, lo = where[c]
            vs = vt_refs[seg][0, v_head, :, lo + u * sub:lo + (u + 1) * sub]
            for j in range(nq):
                part = jnp.dot(vs, jnp.exp2(s_cur[u][j] - m_new[j]).astype(BF16), preferred_element_type=F32)
                pv[j] = part if pv[j] is None else pv[j] + part
        for j in range(nq):
            acc[j] = pv[j] if c == 0 else acc[j] * jnp.exp2(m[j] - m_new[j]) + pv[j]
            m[j] = m_new[j]
        if c + 1 < nch:
            m_new = [jnp.maximum(m[j], chunk_max(s_buf[c + 1], j)) for j in range(nq)]
    return acc


def _mla_attn_kernel(key_chunk, nseg, heads, q_ref, *refs):
    k_refs, vt_refs, o_ref = refs[:nseg], refs[nseg:2 * nseg], refs[2 * nseg]
    for hh in range(heads):
        q = q_ref[:, LANE * hh:LANE * (hh + 1)]
        (acc,) = _softmax_pv([q], k_refs, vt_refs, key_chunk, MLA_SUB, MLA_AHEAD, LANE * hh, hh)
        o_ref[0, MLA_V * hh:MLA_V * (hh + 1)] = acc[0:MLA_V] / acc[MLA_V:MLA_V + 1]


def _key_chunk(lk, rows=512):
    return rows if lk % rows == 0 else lk


def _heads_per_step(ks):
    return 4 if sum(k.shape[1] for k in ks) <= 512 else 1


def _vt_kernel(v_ref, o_ref):
    _store_vt(o_ref, v_ref[0].astype(F32).T)


def _vt_with_ones(v3):
    b, lk, width = v3.shape
    tm = 1536 if lk % 1536 == 0 else lk
    return pl.pallas_call(
        _vt_kernel,
        grid=(b, lk // tm),
        in_specs=[pl.BlockSpec((1, tm, width), lambda i, j: (i, j, 0))],
        out_specs=pl.BlockSpec((1, 4, VT_ROWS, tm), lambda i, j: (i, 0, 0, j)),
        out_shape=jax.ShapeDtypeStruct((b, 4, VT_ROWS, lk), BF16),
        compiler_params=_cparams(("parallel", "parallel")),
        name="vt_ones",
    )(v3)


def _mla_attn(q, ks, vts, batch, seq_len):
    tq = 256
    nq = seq_len // tq
    chunk = _key_chunk(min(k.shape[1] for k in ks), MLA_CHUNK)
    hp = _heads_per_step(ks)
    k_specs = [pl.BlockSpec((1, k.shape[1], LANE * hp), lambda b, h, i: (b, 0, h)) for k in ks]
    v_specs = [pl.BlockSpec((1, hp, VT_ROWS, v.shape[3]), lambda b, h, i: (b, h, 0, 0)) for v in vts]
    return pl.pallas_call(
        functools.partial(_mla_attn_kernel, chunk, len(ks), hp),
        grid=(batch, MLA_HEADS // hp, nq),
        in_specs=[pl.BlockSpec((tq, LANE * hp), lambda b, h, i: (b * nq + i, h))] + k_specs + v_specs,
        out_specs=pl.BlockSpec((1, MLA_V * hp, tq), lambda b, h, i: (b, h, i)),
        out_shape=jax.ShapeDtypeStruct((batch, BRANCH, seq_len), F32),
        compiler_params=_cparams(("parallel", "parallel", "arbitrary")),
        name="mla_attn",
    )(q, *ks, *vts)


def _seg_const(width, seg):
    sid = np.arange(width) // seg
    return jnp.asarray((sid[:, None] == sid[None, :]).astype(np.float32), BF16)


def _diff_prep_kernel(rope, dq_ref, dk_ref, dv_ref, gq_ref, gk_ref, seg_ref, *rest):
    if rope:
        cos_ref, sin_ref, q_ref, k_ref, kf_ref, vt_ref = rest
    else:
        q_ref, k_ref, kf_ref, vt_ref = rest
    seg = seg_ref[...]
    _store_vt(vt_ref, dv_ref[...].T)

    def norm(x, g):
        ss = _segsum(x * x, seg) * (1.0 / DIFF_HD)
        return x * lax.rsqrt(ss + EPS) * g

    q = norm(dq_ref[...], gq_ref[...])
    k = norm(dk_ref[...], gk_ref[...])
    kf_ref[...] = k
    if rope:
        cos, sin = cos_ref[...], sin_ref[...]
        q = q * cos + _swap8(q) * sin
        k = k * cos + _swap8(k) * sin
    q_ref[...] = (q * (DIFF_SCALE * LOG2E)).astype(BF16)
    k_ref[...] = k.astype(BF16)


def _diff_prep(p, lw, consts, seq_len, rope_tabs):
    n = p.shape[0]
    tm = min(512, seq_len)
    rope = rope_tabs is not None
    full = lambda shape: pl.BlockSpec(shape, lambda i: (0,) * len(shape))
    in_specs = [pl.BlockSpec((tm, 256), lambda i: (i, P_DQ // 256)),
                pl.BlockSpec((tm, 256), lambda i: (i, P_DK // 256)),
                pl.BlockSpec((tm, 256), lambda i: (i, P_DV // 256)),
                full((1, 256)), full((1, 256)), full((256, 256))]
    args = [p, p, p, lw["dgq"], lw["dgk"], consts["seg32"]]
    tps = seq_len // tm
    if rope:
        in_specs += [pl.BlockSpec((tm, 256), lambda i: (i % tps, 0))] * 2
        args += [rope_tabs["cos_diff"], rope_tabs["sin_diff"]]
    blk = pl.BlockSpec((tm, 256), lambda i: (i, 0))
    return pl.pallas_call(
        functools.partial(_diff_prep_kernel, rope),
        grid=(n // tm,),
        in_specs=in_specs,
        out_specs=[blk, blk, blk, pl.BlockSpec((1, 4, VT_ROWS, tm), lambda i: (i // tps, 0, 0, i % tps))],
        out_shape=[jax.ShapeDtypeStruct((n, 256), BF16), jax.ShapeDtypeStruct((n, 256), BF16),
                   jax.ShapeDtypeStruct((n, 256), F32),
                   jax.ShapeDtypeStruct((n // seq_len, 4, VT_ROWS, seq_len), BF16)],
        compiler_params=_cparams(("parallel",)),
        name="diff_prep",
    )(*args)


def _attn_prep_kernel(rope, tiles_per_seq, *refs):
    n_mla, n_diff, n_hy = (12, 8, 5) if rope else (10, 6, 5)
    outs = refs[n_mla + n_diff + n_hy:]
    _mla_q_kernel(rope, *refs[:n_mla], *outs[:3])
    _diff_prep_kernel(rope, *refs[n_mla:n_mla + n_diff], *outs[3:7])
    _hy_conv3_kernel(tiles_per_seq, *refs[n_mla + n_diff:n_mla + n_diff + n_hy], *outs[7:])


def _attn_prep(p, lw, consts, seq_len, rope_tabs):
    n = p.shape[0]
    tm = min(512, seq_len)
    tps = seq_len // tm
    rope = rope_tabs is not None
    full = lambda shape: pl.BlockSpec(shape, lambda i: (0,) * len(shape))
    col = lambda width, off: pl.BlockSpec((tm, width), lambda i: (i, off // width))
    row = lambda width: pl.BlockSpec((tm, width), lambda i: (i, 0))
    tab = lambda width: pl.BlockSpec((tm, width), lambda i: (i % tps, 0))
    mla_specs = [col(256, P_CQ), col(128, P_CKV), col(128, P_KR),
                 full((1, 256)), full((256, 512)), full((1, 128)), full((1, 512)), full((1, 128)),
                 full((512, 512)), full((1, 512))]
    mla_args = [p, p, p, lw["qn_g"], lw["w_uq"], lw["kvn_g"], lw["gq"], lw["gkr"], consts["seg512"],
                consts["cnt512"]]
    diff_specs = [col(256, P_DQ), col(256, P_DK), col(256, P_DV), full((1, 256)), full((1, 256)), full((256, 256))]
    diff_args = [p, p, p, lw["dgq"], lw["dgk"], consts["seg32"]]
    if rope:
        mla_specs += [tab(128), tab(128)]
        mla_args += [rope_tabs["cos_mla"], rope_tabs["sin_mla"]]
        diff_specs += [tab(256), tab(256)]
        diff_args += [rope_tabs["cos_diff"], rope_tabs["sin_diff"]]
    nt, g8, hcol = n // tm, tm // 8, P_HU // 768
    hy_specs = [pl.BlockSpec((8, 768), lambda i: (jnp.maximum(i * g8 - 1, 0), hcol)),
                pl.BlockSpec((tm, 768), lambda i: (i, hcol)),
                pl.BlockSpec((8, 768), lambda i: (jnp.minimum((i + 1) * g8, nt * g8 - 1), hcol)),
                full((3, 768)), full((1, 768))]
    hy_args = [p, p, p, lw["hy_conv_w"], lw["hy_conv_b"]]
    return pl.pallas_call(
        functools.partial(_attn_prep_kernel, rope, tps),
        grid=(nt,),
        in_specs=mla_specs + diff_specs + hy_specs,
        out_specs=[row(512), row(128), row(128), row(256), row(256), row(256),
                   pl.BlockSpec((1, 4, VT_ROWS, tm), lambda i: (i // tps, 0, 0, i % tps)),
                   row(256), row(256), row(256)],
        out_shape=[jax.ShapeDtypeStruct((n, 512), BF16), jax.ShapeDtypeStruct((n, 128), F32),
                   jax.ShapeDtypeStruct((n, 128), F32),
                   jax.ShapeDtypeStruct((n, 256), BF16), jax.ShapeDtypeStruct((n, 256), BF16),
                   jax.ShapeDtypeStruct((n, 256), F32),
                   jax.ShapeDtypeStruct((n // seq_len, 4, VT_ROWS, seq_len), BF16)]
        + [jax.ShapeDtypeStruct((n, 256), F32)] * 3,
        compiler_params=_cparams(("parallel",)),
        name="attn_prep",
    )(*mla_args, *diff_args, *hy_args)


def _diff_attn_kernel(lam_init, key_chunk, nseg, heads, q_ref, *refs):
    k_refs, vt_refs = refs[:nseg], refs[nseg:2 * nseg]
    lp_ref, g_ref, o_ref = refs[2 * nseg:]
    lp = lp_ref[...]
    lam = (jnp.exp(jnp.sum(lp[0:1] * lp[1:2], axis=1, keepdims=True))
           - jnp.exp(jnp.sum(lp[2:3] * lp[3:4], axis=1, keepdims=True)) + lam_init)
    for hh in range(heads):
        blk = 0 if heads == 1 else LANE * (hh // 2)
        base = (pl.program_id(1) % 2) * 64 if heads == 1 else (hh % 2) * 64
        q = q_ref[:, blk:blk + LANE]
        lane = lax.broadcasted_iota(jnp.int32, q.shape, 1)
        zero = jnp.zeros_like(q)

        def map_query(j):
            lo = base + 32 * j
            return jnp.where((lane >= lo) & (lane < lo + 32), q, zero)

        acc0, acc1 = _softmax_pv([map_query(0), map_query(1)], k_refs, vt_refs, key_chunk, DIFF_SUB, DIFF_AHEAD,
                                 blk, hh)
        o = acc0[0:64] / acc0[64:65] - lam * (acc1[0:64] / acc1[64:65])
        ms = jnp.mean(o * o, axis=0, keepdims=True)
        o_ref[0, 64 * hh:64 * (hh + 1)] = o * lax.rsqrt(ms + EPS) * g_ref[...] * (1.0 - lam_init)


def _diff_attn(q, ks, vts, lp, g_col, lam_init, batch, seq_len):
    tq = min(512, seq_len)
    nq = seq_len // tq
    chunk = _key_chunk(min(k.shape[1] for k in ks), DIFF_CHUNK)
    hp = _heads_per_step(ks)
    lanes = LANE if hp == 1 else BRANCH
    k_specs = [pl.BlockSpec((1, k.shape[1], lanes), lambda b, h, i: (b, 0, h // 2)) for k in ks]
    v_specs = [pl.BlockSpec((1, hp, VT_ROWS, v.shape[3]), lambda b, h, i: (b, h, 0, 0)) for v in vts]
    return pl.pallas_call(
        functools.partial(_diff_attn_kernel, lam_init, chunk, len(ks), hp),
        grid=(batch, DIFF_HEADS // hp, nq),
        in_specs=[pl.BlockSpec((tq, lanes), lambda b, h, i: (b * nq + i, h // 2))] + k_specs + v_specs
        + [pl.BlockSpec((4, DIFF_HD), lambda b, h, i: (0, 0)), pl.BlockSpec((64, 1), lambda b, h, i: (0, 0))],
        out_specs=pl.BlockSpec((1, 64 * hp, tq), lambda b, h, i: (b, h, i)),
        out_shape=jax.ShapeDtypeStruct((batch, BRANCH, seq_len), F32),
        compiler_params=_cparams(("parallel", "parallel", "arbitrary")),
        name="diff_attn",
    )(q, *ks, *vts, lp, g_col)


def _hgrn_consts():
    c = HGRN_CHUNK
    t = np.arange(c)
    low = (t[None, :] <= t[:, None]).astype(np.float32)
    blocks = []
    for j in range(HGRN_MM_LEVELS):
        m = 1 << j
        rho = (t // (2 * m)) * (2 * m) + m - 1
        sign = np.where((t // m) % 2 == 1, 1.0, -1.0)[:, None]
        blocks.append(sign * (low - (t[None, :] <= rho[:, None]).astype(np.float32)))
    blocks.append(low)
    fwd = np.concatenate(blocks, axis=0)
    bwd = np.concatenate([b[::-1, ::-1] for b in blocks], axis=0)
    right = np.stack([(t // (1 << j)) % 2 for j in range(HGRN_LEVELS)]).astype(np.float32)
    right = np.stack([right, right[:, ::-1]])
    right = np.broadcast_to(right[..., None], right.shape + (BRANCH,))
    return jnp.asarray(np.stack([fwd, bwd]), BF16), jnp.asarray(right, F32)


def _hgrn_kernel(nc, nb, qf_ref, zf_ref, vf_ref, qb_ref, zb_ref, vb_ref, lb_ref, dd_ref, rm_ref, s0_ref,
                 of_ref, ob_ref, sout_ref, st_ref):
    c = HGRN_CHUNK
    ci = pl.program_id(1)
    chains = [(bi, d) for bi in range(nb) for d in (0, 1)]
    ids = range(len(chains))

    @pl.when(ci == 0)
    def _():
        st_ref[...] = s0_ref[...]

    lane = lax.broadcasted_iota(jnp.int32, (1, BRANCH), 1)
    head_masks = [(lane >= HGRN_DK * h) & (lane < HGRN_DK * (h + 1)) for h in range(HGRN_HEADS)]
    t_idx = lax.broadcasted_iota(jnp.int32, (c, HGRN_HEADS * c), 0)
    s_idx = lax.broadcasted_iota(jnp.int32, (c, HGRN_HEADS * c), 1) & (c - 1)
    pair_xor = t_idx ^ s_idx

    def stack_heads(x):
        xb = x.astype(BF16)
        zero = jnp.zeros_like(xb)
        return jnp.concatenate([jnp.where(hm, xb, zero) for hm in head_masks], axis=0)

    q_refs, z_refs, v_refs = (qf_ref, qb_ref), (zf_ref, zb_ref), (vf_ref, vb_ref)
    q = [q_refs[d][bi] for bi, d in chains]
    v = [v_refs[d][bi] for bi, d in chains]
    z = [z_refs[d][bi] for bi, d in chains]
    lb = [lb_ref[d] for _, d in chains]
    g = [jnp.log(lb[i] + (1.0 - lb[i]) * jax.nn.sigmoid(z[i])) for i in ids]
    kk = [(1.0 - lb[i]) * jax.nn.sigmoid(-z[i]) for i in ids]
    sums = []
    for i in ids:
        gh, gl = _split2(g[i])
        dd = dd_ref[chains[i][1]]
        sums.append(jnp.dot(dd, gh, preferred_element_type=F32) + jnp.dot(dd, gl, preferred_element_type=F32))
    b = [sums[i][HGRN_MM_LEVELS * c:] for i in ids]
    b_tot = [b[i][c - 1:c] if chains[i][1] == 0 else b[i][0:1] for i in ids]

    def neg_abs_decay(i, j):
        if j < HGRN_MM_LEVELS:
            return sums[i][j * c:(j + 1) * c]
        m = 1 << j
        off = m - 1 if chains[i][1] == 0 else m
        ref = jnp.concatenate([jnp.broadcast_to(b[i][g0 + off:g0 + off + 1], (2 * m, BRANCH))
                               for g0 in range(0, c, 2 * m)], axis=0)
        return -jnp.abs(b[i] - ref)

    a = [None] * len(chains)
    for j in reversed(range(HGRN_LEVELS)):
        same_group = pair_xor < (2 << j)
        for i in ids:
            e = jnp.exp(neg_abs_decay(i, j))
            eq = e * rm_ref[chains[i][1], j]
            qt = q[i] * eq
            kt = kk[i] * (e - eq)
            lvl = _nt(qt, stack_heads(kt))
            a[i] = lvl if a[i] is None else jnp.where(same_group, lvl, a[i])
    diagonal = pair_xor == 0
    for i in ids:
        a[i] = jnp.where(diagonal, _nt(q[i], stack_heads(kk[i])), a[i])

    outs = (of_ref, ob_ref)
    for i in ids:
        bi, d = chains[i]
        o = jnp.dot(a[i].astype(BF16), stack_heads(v[i]), preferred_element_type=F32)
        outs[d][bi] = o + _nt(q[i] * jnp.exp(b[i]), st_ref[bi, d])

    r2 = lax.broadcasted_iota(jnp.int32, (BRANCH, BRANCH), 0) // HGRN_DK
    c2 = lax.broadcasted_iota(jnp.int32, (BRANCH, BRANCH), 1) // HGRN_DK
    for i in ids:
        bi, d = chains[i]
        kd = kk[i] * jnp.exp(b_tot[i] - b[i])
        upd = lax.dot_general(v[i].astype(BF16), kd.astype(BF16), (((0,), (0,)), ((), ())),
                              preferred_element_type=F32)
        st_new = st_ref[bi, d] * jnp.exp(b_tot[i]) + jnp.where(r2 == c2, upd, 0.0)
        st_ref[bi, d] = st_new

        @pl.when(ci == nc - 1)
        def _(bi=bi, d=d, st_new=st_new):
            sout_ref[bi, d] = st_new


def _hgrn(p, lb_l, dd, rm, st0, batch, seq_len):
    n = p.shape[0]
    c = HGRN_CHUNK
    nc = seq_len // c
    nb = HGRN_ROWS if batch % HGRN_ROWS == 0 else 1
    p3 = p.reshape(batch, seq_len, P_COLS)
    fwd = lambda col: pl.BlockSpec((nb, c, 256), lambda b, i: (b, i, col))
    bwd = lambda col: pl.BlockSpec((nb, c, 256), lambda b, i: (b, nc - 1 - i, col))
    whole = lambda shape: pl.BlockSpec(shape, lambda b, i: (0,) * len(shape))
    state = pl.BlockSpec((nb, 2, 256, 256), lambda b, i: (b, 0, 0, 0))
    o_f, o_b, st = pl.pallas_call(
        functools.partial(_hgrn_kernel, nc, nb),
        grid=(batch // nb, nc),
        in_specs=[fwd(P_HQ // 256), fwd(P_HZF // 256), fwd(P_HI // 256),
                  bwd(P_HQ // 256), bwd(P_HZB // 256), bwd(P_HI // 256),
                  whole((2, 1, 256)), whole((2, (HGRN_MM_LEVELS + 1) * c, c)),
                  whole((2, HGRN_LEVELS, c, 256)), state],
        out_specs=[pl.BlockSpec((nb, c, 256), lambda b, i: (b, i, 0)),
                   pl.BlockSpec((nb, c, 256), lambda b, i: (b, nc - 1 - i, 0)), state],
        out_shape=[jax.ShapeDtypeStruct((batch, seq_len, 256), F32),
                   jax.ShapeDtypeStruct((batch, seq_len, 256), F32),
                   jax.ShapeDtypeStruct((batch, 2, 256, 256), F32)],
        scratch_shapes=[pltpu.VMEM((nb, 2, 256, 256), F32)],
        compiler_params=_cparams(("parallel", "arbitrary")),
        name="hgrn",
    )(p3, p3, p3, p3, p3, p3, lb_l, dd, rm, st0)
    return o_f.reshape(n, 256), o_b.reshape(n, 256), st


def _hy_conv3_kernel(tiles_per_seq, above_ref, cur_ref, below_ref, w_ref, b_ref, v_ref, x1_ref, x2_ref):
    i = pl.program_id(0)
    cur = cur_ref[...]
    tm = cur.shape[0]
    first = (i % tiles_per_seq) == 0
    last = (i % tiles_per_seq) == tiles_per_seq - 1
    above = jnp.where(first, 0.0, above_ref[7:8, :])
    below = jnp.where(last, 0.0, below_ref[0:1, :])
    row = lax.broadcasted_iota(jnp.int32, (tm, 1), 0)
    prev = jnp.where(row == 0, above, pltpu.roll(cur, 1, 0))
    nxt = jnp.where(row == tm - 1, below, pltpu.roll(cur, tm - 1, 0))
    w = w_ref[...]
    u = prev * w[0:1] + cur * w[1:2] + nxt * w[2:3] + b_ref[...]
    v_ref[...] = u[:, 0:256]
    x1_ref[...] = u[:, 256:512]
    x2_ref[...] = u[:, 512:768]


def _hy_conv3(p, w, b, seq_len):
    n = p.shape[0]
    tm = min(512, seq_len)
    nt = n // tm
    g = tm // 8
    col = P_HU // 768
    oblk = pl.BlockSpec((tm, 256), lambda i: (i, 0))
    return pl.pallas_call(
        functools.partial(_hy_conv3_kernel, seq_len // tm),
        grid=(nt,),
        in_specs=[pl.BlockSpec((8, 768), lambda i: (jnp.maximum(i * g - 1, 0), col)),
                  pl.BlockSpec((tm, 768), lambda i: (i, col)),
                  pl.BlockSpec((8, 768), lambda i: (jnp.minimum((i + 1) * g, nt * g - 1), col)),
                  pl.BlockSpec((3, 768), lambda i: (0, 0)), pl.BlockSpec((1, 768), lambda i: (0, 0))],
        out_specs=[oblk, oblk, oblk],
        out_shape=[jax.ShapeDtypeStruct((n, 256), F32)] * 3,
        compiler_params=_cparams(("parallel",)),
        name="hy_conv3",
    )(p, p, p, w, b)


def _hy_filter_kernel(feat_ref, w1_ref, b1_ref, w2_ref, b2_ref, w3_ref, fr_ref, win_ref, o_ref):
    fr = fr_ref[...]
    h = jnp.sin(fr[0:1] * (_dot3(feat_ref[...], w1_ref[...]) + b1_ref[...]))
    h = jnp.sin(fr[1:2] * (_dot3(h, w2_ref[...]) + b2_ref[...]))
    o_ref[...] = _dot3(h, w3_ref[0]) * _tile_lanes(win_ref[...], HY_ORDER)


def _hy_filter(feats2, window2, w1p, b1, w2, b2, w3d, freq):
    l2 = feats2.shape[0]
    ln = l2 // 2
    tm = min(512, ln)
    full = lambda shape: pl.BlockSpec(shape, lambda i: (0,) * len(shape))
    return pl.pallas_call(
        _hy_filter_kernel,
        grid=(l2 // tm,),
        in_specs=[pl.BlockSpec((tm, LANE), lambda i: (i, 0)),
                  full((LANE, HY_FH)), full((1, HY_FH)), full((HY_FH, HY_FH)), full((1, HY_FH)),
                  pl.BlockSpec((1, HY_FH, HY_ORDER * HY_CH), lambda i: (i // (ln // tm), 0, 0)), full((2, HY_FH)),
                  pl.BlockSpec((tm, HY_CH), lambda i: (i, 0))],
        out_specs=pl.BlockSpec((tm, HY_ORDER * HY_CH), lambda i: (i, 0)),
        out_shape=jax.ShapeDtypeStruct((l2, HY_ORDER * HY_CH), F32),
        compiler_params=_cparams(("parallel",)),
        name="hy_filter",
    )(feats2, w1p, b1, w2, b2, w3d, freq, window2)


def _fft_blocking(nb, n1, n2, ch):
    per_batch = n1 * n2 * ch * 4
    if per_batch <= FFT_BLOCK_BYTES:
        bb = max(1, min(nb, FFT_BLOCK_BYTES // per_batch))
        while nb % bb:
            bb -= 1
        return bb, n2
    rt = n2
    while n1 * rt * ch * 4 > FFT_BLOCK_BYTES and rt > 8:
        rt //= 2
    return 1, rt


def _fft_a_kernel(f_ref, x_ref, o_ref):
    f = f_ref[...]
    for b in range(x_ref.shape[0]):
        x = x_ref[b].astype(BF16)
        o_ref[b] = jnp.einsum("kn,nrc->krc", f, x, preferred_element_type=F32).astype(o_ref.dtype)


def _fft_a(fa, x4, out_dtype):
    nb, n1, n2, ch = x4.shape
    r = fa.shape[0]
    bb, rt = _fft_blocking(nb, n1, n2, ch)
    return pl.pallas_call(
        _fft_a_kernel,
        grid=(nb // bb, n2 // rt),
        in_specs=[pl.BlockSpec((r, n1), lambda b, i: (0, 0)),
                  pl.BlockSpec((bb, n1, rt, ch), lambda b, i: (b, 0, i, 0))],
        out_specs=pl.BlockSpec((bb, r, rt, ch), lambda b, i: (b, 0, i, 0)),
        out_shape=jax.ShapeDtypeStruct((nb, r, n2, ch), out_dtype),
        compiler_params=_cparams(("parallel", "parallel")),
        name="fft_a",
    )(fa, x4)


def _fft_b_kernel(with_inverse, mf_ref, *rest):
    if with_inverse:
        mi_ref, a_ref, h_ref, o_ref = rest
    else:
        a_ref, o_ref = rest
    half = FFT_N2
    for kk in range(a_ref.shape[2]):
        for b in range(a_ref.shape[0]):
            a = jnp.concatenate([a_ref[b, 0, kk], a_ref[b, 1, kk]], axis=0)
            x = jnp.dot(mf_ref[kk], a.astype(BF16), preferred_element_type=F32)
            if with_inverse:
                xr, xi = x[:half], x[half:]
                hr, hi = h_ref[0, kk], h_ref[1, kk]
                y = jnp.concatenate([xr * hr - xi * hi, xr * hi + xi * hr], axis=0)
                x = jnp.dot(mi_ref[kk], y.astype(BF16), preferred_element_type=F32)
            o_ref[b, 0, kk] = x[:half].astype(o_ref.dtype)
            o_ref[b, 1, kk] = x[half:].astype(o_ref.dtype)


def _fft_b(mf, mi, a5, spec, order):
    nb, _, k1n, n2, ch = a5.shape
    ks = 3 if (k1n % 3 == 0 and nb * ch <= 1024) else 1
    mat = pl.BlockSpec((ks, 2 * n2, 2 * n2), lambda k: (k, 0, 0))
    blk = pl.BlockSpec((nb, 2, ks, n2, ch), lambda k: (0, 0, k, 0, 0))
    if spec is None:
        in_specs, args = [mat, blk], [mf, a5]
    else:
        in_specs = [mat, mat, blk, pl.BlockSpec((2, ks, n2, ch), lambda k: (0, k, 0, order))]
        args = [mf, mi, a5, spec]
    return pl.pallas_call(
        functools.partial(_fft_b_kernel, spec is not None),
        grid=(k1n // ks,),
        in_specs=in_specs,
        out_specs=blk,
        out_shape=jax.ShapeDtypeStruct(a5.shape, F32 if spec is None else BF16),
        compiler_params=_cparams(("parallel",)),
        name="fft_b",
    )(*args)


def _fft_a_inv_kernel(g_ref, p_ref, x_ref, z_ref, bias_ref, o_ref):
    g = g_ref[...]
    for b in range(p_ref.shape[0]):
        conv = jnp.einsum("nk,krc->nrc", g, p_ref[b], preferred_element_type=F32)
        o_ref[b] = x_ref[b] * (conv + z_ref[b] * bias_ref[...])


def _fft_a_inv(g, p4, xg4, z4, bias):
    nb, n1, n2, ch = z4.shape
    r = g.shape[1]
    bb, rt = _fft_blocking(nb, n1, n2, ch)
    blk = pl.BlockSpec((bb, n1, rt, ch), lambda b, i: (b, 0, i, 0))
    return pl.pallas_call(
        _fft_a_inv_kernel,
        grid=(nb // bb, n2 // rt),
        in_specs=[pl.BlockSpec((n1, r), lambda b, i: (0, 0)),
                  pl.BlockSpec((bb, r, rt, ch), lambda b, i: (b, 0, i, 0)), blk, blk,
                  pl.BlockSpec((1, 1, ch), lambda b, i: (0, 0, 0))],
        out_specs=blk,
        out_shape=jax.ShapeDtypeStruct(z4.shape, F32),
        compiler_params=_cparams(("parallel", "parallel")),
        name="fft_a_inv",
    )(g, p4, xg4, z4, bias.reshape(1, 1, ch))


def _fft_tables(ln):
    n = 2 * ln
    n1t = n // FFT_N2
    k1n = n1t // 2 + 1
    kk = np.arange(k1n)

    def stage_a(n1_in):
        ang = 2.0 * np.pi * ((kk[:, None] * np.arange(n1_in)[None, :]) % n1t) / n1t
        return jnp.asarray(np.concatenate([np.cos(ang), -np.sin(ang)], axis=0), BF16)

    n1o = n1t // 2
    ang = 2.0 * np.pi * ((np.arange(n1o)[:, None] * kk[None, :]) % n1t) / n1t
    edge = (kk == 0) | (kk == n1t // 2)
    ck = np.where(edge, 1.0, 2.0) / n
    g = jnp.asarray(np.concatenate([ck * np.cos(ang), -ck * np.where(edge, 0.0, np.sin(ang))], axis=1), BF16)

    k1 = jnp.arange(k1n, dtype=jnp.int32)[:, None, None]
    k2 = jnp.arange(FFT_N2, dtype=jnp.int32)[None, :, None]
    n2 = jnp.arange(FFT_N2, dtype=jnp.int32)[None, None, :]
    th = (2.0 * math.pi / n) * ((n2 * (k1 + n1t * k2)) % n).astype(F32)
    c, s = jnp.cos(th), jnp.sin(th)
    mf = jnp.concatenate([jnp.concatenate([c, s], axis=2), jnp.concatenate([-s, c], axis=2)], axis=1)
    return dict(fa_half=stage_a(n1o), fa_full=stage_a(n1t), g=g, mf=mf.astype(BF16),
                mi=mf.transpose(0, 2, 1).astype(BF16), k1n=k1n, n1o=n1o, n1t=n1t)


def _hy_static(ln):
    t = jnp.linspace(0.0, 1.0, ln, dtype=F32)[:, None]
    w = 2.0 * math.pi * jnp.arange(ln, dtype=F32) / ln
    f = jnp.linspace(1e-4, HY_BANDS - 1, HY_BANDS, dtype=F32)
    ang = w[:, None] * f[None, :]
    feats = jnp.concatenate([t, jnp.cos(ang), -jnp.sin(ang)], axis=-1)
    feats = jnp.pad(feats, ((0, 0), (0, LANE - HY_EMB)))
    min_decay = math.log(HY_DECAY_TARGET) / HY_SLOW_DECAY
    max_decay = math.log(HY_DECAY_TARGET) / HY_FAST_DECAY
    deltas = jnp.linspace(min_decay, max_decay, HY_CH, dtype=F32)
    window = jnp.exp(-t * jnp.abs(deltas))
    feats = jnp.concatenate([feats, feats[::-1]], axis=0)
    window = jnp.concatenate([window, window[::-1]], axis=0)
    return feats, window


def _outproj_kernel(x_ref, mod_ref, oa_ref, ob_ref, of_ref, obk_ref, od_ref, gate_ref, hg_ref, seg_ref, w_ref,
                    y_ref):
    gt = gate_ref[...]
    sg = gt * jax.nn.sigmoid(gt)
    oc = of_ref[...] + obk_ref[...]
    ss = _segsum(oc * oc, seg_ref[...]) * (1.0 / HGRN_DK)
    oc = oc * lax.rsqrt(ss + EPS) * hg_ref[...]
    acc = _bdot(oa_ref[0].T * sg[:, 0:256], w_ref[0:256, :])
    acc += _bdot(ob_ref[0].T * sg[:, 256:512], w_ref[256:512, :])
    acc += _bdot(oc * sg[:, 512:768], w_ref[512:768, :])
    acc += _bdot(od_ref[...] * sg[:, 768:1024], w_ref[768:1024, :])
    y_ref[...] = x_ref[...] + mod_ref[0, 2:3, :] * acc


def _outproj(x2, mod, ot_a, ot_b, o_f, o_b, out_d, p, hg, seg64, w_out, seq_len):
    n = x2.shape[0]
    tm = min(512, seq_len)
    per_batch = mod.shape[0] > 1
    tps = seq_len // tm
    mod_idx = (lambda i: (i // tps, 0, 0)) if per_batch else (lambda i: (0, 0, 0))
    b256 = pl.BlockSpec((tm, 256), lambda i: (i, 0))
    bt = pl.BlockSpec((1, 256, tm), lambda i: (i // tps, 0, i % tps))
    return pl.pallas_call(
        _outproj_kernel,
        grid=(n // tm,),
        in_specs=[pl.BlockSpec((tm, D_MODEL), lambda i: (i, 0)),
                  pl.BlockSpec((1, 3, D_MODEL), mod_idx),
                  bt, bt, b256, b256, b256,
                  pl.BlockSpec((tm, 1024), lambda i: (i, P_GATE // 1024)),
                  pl.BlockSpec((1, 256), lambda i: (0, 0)),
                  pl.BlockSpec((256, 256), lambda i: (0, 0)),
                  pl.BlockSpec((D_MODEL, D_MODEL), lambda i: (0, 0))],
        out_specs=pl.BlockSpec((tm, D_MODEL), lambda i: (i, 0)),
        out_shape=jax.ShapeDtypeStruct((n, D_MODEL), F32),
        compiler_params=_cparams(("parallel",)),
        name="outproj",
    )(x2, mod, ot_a, ot_b, o_f, o_b, out_d, p, hg, seg64, w_out)


def _layer(x2, mod, lw, consts, batch, seq_len, ctx, rope_tabs, hy):
    n = batch * seq_len
    p = _inproj(x2, mod, lw["norm_g"], lw["w_in"], seq_len)

    q_a, ckvn, krp, q_b, k_b, kd, vt_b, v_d, x1, x2g = _attn_prep(p, lw, consts, seq_len, rope_tabs)
    ckv3 = ckvn.reshape(batch, seq_len, MLA_KV_LORA)
    krp3 = krp.reshape(batch, seq_len, LANE)
    k_a, vt_a = _mla_kv(ckvn, krp, lw, consts, batch)
    ks_a, vts_a = [k_a.reshape(batch, seq_len, 512)], [vt_a]
    if ctx is not None:
        lc = ctx[0].shape[1]
        k_c, vt_c = _mla_kv(ctx[0].reshape(batch * lc, MLA_KV_LORA), ctx[1].reshape(batch * lc, LANE), lw, consts,
                            batch)
        ks_a.append(k_c.reshape(batch, lc, 512))
        vts_a.append(vt_c)
    ot_a = _mla_attn(q_a, ks_a, vts_a, batch, seq_len)

    ks_b, vts_b = [k_b.reshape(batch, seq_len, BRANCH)], [vt_b]
    if ctx is not None:
        ks_b.append(ctx[2])
        vts_b.append(ctx[3])
    ot_b = _diff_attn(q_b, ks_b, vts_b, lw["diff_lambda"], lw["subln_col"], lw["lam_init"], batch, seq_len)

    if ctx is not None:
        s0 = ctx[4]
    else:
        s0 = jnp.zeros((batch, 2, HGRN_HEADS, HGRN_DK, HGRN_DK), F32)
    eye = jnp.eye(HGRN_HEADS, dtype=F32)
    st0 = jnp.einsum("bdhke,hg->bdhegk", s0, eye).reshape(batch, 2, BRANCH, BRANCH)
    o_f, o_b, st_out = _hgrn(p, lw["hgrn_lb"], consts["hgrn_dd"], consts["hgrn_right"], st0, batch, seq_len)
    st5 = st_out.reshape(batch, 2, HGRN_HEADS, HGRN_DK, HGRN_HEADS, HGRN_DK)
    states = jnp.stack([st5[:, :, h, :, h, :] for h in range(HGRN_HEADS)], axis=2).swapaxes(-1, -2)

    taps = _hy_filter(hy["feats"], hy["window"], lw["hy_w1"], lw["hy_b1"], lw["hy_w2"], lw["hy_b2"], lw["hy_w3"],
                      lw["hy_freq"])
    k1n, n1o, n1t = hy["k1n"], hy["n1o"], hy["n1t"]
    ta = _fft_a(hy["fa_full"], taps.reshape(1, n1t, FFT_N2, HY_ORDER * HY_CH), BF16)
    spec = _fft_b(hy["mf"], None, ta.reshape(1, 2, k1n, FFT_N2, HY_ORDER * HY_CH), None, 0)[0]
    z4 = v_d.reshape(batch, n1o, FFT_N2, HY_CH)
    for o, xg in enumerate((x1, x2g)):
        a = _fft_a(hy["fa_half"], z4, BF16).reshape(batch, 2, k1n, FFT_N2, HY_CH)
        pk = _fft_b(hy["mf"], hy["mi"], a, spec, o).reshape(batch, 2 * k1n, FFT_N2, HY_CH)
        z4 = _fft_a_inv(hy["g"], pk, xg.reshape(batch, n1o, FFT_N2, HY_CH), z4, lw["hy_bias"][o:o + 1])
    out_d = z4.reshape(n, HY_CH)

    y = _outproj(x2, mod, ot_a, ot_b, o_f, o_b, out_d, p, lw["hgrn_out_g"], consts["seg64"], lw["w_out"], seq_len)
    new = None
    if ctx is None:
        new = (ckv3, krp3[:, :, KR_OFF:KR_OFF + MLA_ROPE],
               kd.reshape(batch, seq_len, DIFF_HEADS, 2, DIFF_HD),
               p[:, P_DV:P_DV + BRANCH].reshape(batch, seq_len, DIFF_HEADS, 2 * DIFF_HD), states)
    return y, new


def _rope_tables(seq_len):
    half = MLA_ROPE // 2
    inv = ROPE_BASE ** (-jnp.arange(0, half, 2, dtype=F32) / half)
    rows = seq_len // GRID_W
    row = jnp.repeat(jnp.arange(rows, dtype=F32), GRID_W)
    col = (jnp.arange(rows * GRID_W) % GRID_W).astype(F32)
    ar, ac = row[:, None] * inv, col[:, None] * inv
    cos32 = jnp.concatenate([jnp.cos(ar), jnp.cos(ar), jnp.cos(ac), jnp.cos(ac)], axis=-1)
    sin32 = jnp.concatenate([-jnp.sin(ar), jnp.sin(ar), -jnp.sin(ac), jnp.sin(ac)], axis=-1)
    pad = ((0, 0), (KR_OFF, LANE - KR_OFF - MLA_ROPE))
    return dict(cos_mla=jnp.pad(cos32, pad, constant_values=1.0), sin_mla=jnp.pad(sin32, pad),
                cos_diff=jnp.tile(cos32, (1, 2 * DIFF_HEADS)), sin_diff=jnp.tile(sin32, (1, 2 * DIFF_HEADS)))


def _hy_tables(seq_len):
    feats, window = _hy_static(seq_len)
    return dict(feats=feats, window=window, **_fft_tables(seq_len))


def _layer_weights(l, w_in_p, lb, W):
    def head_pad(w, width, per):
        k = w.shape[0]
        w = w.reshape(k, MLA_HEADS, per)[:, :, :width]
        return jnp.pad(w, ((0, 0), (0, 0), (0, LANE - width))).reshape(k, MLA_HEADS * LANE)

    w_ukv = W["mla_w_ukv"][l].reshape(MLA_KV_LORA, MLA_HEADS, MLA_NOPE + MLA_V)
    nope_g, rope_g = W["mla_nope_g"][l], W["mla_rope_g"][l]
    zeros32 = jnp.zeros((MLA_ROPE,), F32)
    zeros64 = jnp.zeros((MLA_NOPE,), F32)
    gq = jnp.tile(jnp.concatenate([nope_g[0], rope_g[0], zeros32]), MLA_HEADS).reshape(1, 512)
    gk = jnp.tile(jnp.concatenate([nope_g[1], zeros64]), MLA_HEADS).reshape(1, 512)
    gkr = jnp.concatenate([zeros64, rope_g[1], zeros32]).reshape(1, LANE)
    return dict(
        norm_g=W["norm_g"][l], w_in=w_in_p[l], w_out=W["w_out"][l].astype(BF16),
        qn_g=W["mla_q_norm_g"][l].reshape(1, -1),
        w_uq=head_pad(W["mla_w_uq"][l], MLA_NOPE + MLA_ROPE, MLA_NOPE + MLA_ROPE).astype(BF16),
        kvn_g=W["mla_kv_norm_g"][l].reshape(1, -1),
        w_uk=jnp.pad(w_ukv[:, :, :MLA_NOPE], ((0, 0), (0, 0), (0, LANE - MLA_NOPE))).reshape(MLA_KV_LORA, 512)
        .astype(BF16),
        w_uv=w_ukv[:, :, MLA_NOPE:].reshape(MLA_KV_LORA, BRANCH).T.astype(BF16),
        gq=gq, gk=gk, gkr=gkr,
        dgq=jnp.tile(W["diff_qk_g"][l, 0], 2 * DIFF_HEADS).reshape(1, BRANCH),
        dgk=jnp.tile(W["diff_qk_g"][l, 1], 2 * DIFF_HEADS).reshape(1, BRANCH),
        diff_lambda=W["diff_lambda"][l], subln_col=W["diff_subln_g"][l].reshape(2 * DIFF_HD, 1),
        lam_init=0.8 - 0.6 * math.exp(-0.3 * l),
        hgrn_lb=lb[:, l].reshape(2, 1, BRANCH),
        hgrn_out_g=jnp.tile(W["hgrn_out_g"][l], HGRN_HEADS).reshape(1, BRANCH),
        hy_conv_w=W["hy_conv_w"][l], hy_conv_b=W["hy_conv_b"][l].reshape(1, -1),
        hy_w1=jnp.pad(W["hy_w1"][l], ((0, LANE - HY_EMB), (0, 0))), hy_b1=W["hy_b1"][l].reshape(1, -1),
        hy_w2=W["hy_w2"][l], hy_b2=W["hy_b2"][l].reshape(1, -1),
        hy_w3=W["hy_w3"][l].reshape(HY_FH, HY_ORDER, 2, HY_CH).transpose(2, 0, 1, 3)
        .reshape(2, HY_FH, HY_ORDER * HY_CH),
        hy_freq=W["hy_sin_freq"][l], hy_bias=W["hy_bias"][l],
    )


def kernel(x_prompt, x_sample, cache_mla_ckv, cache_mla_krope, cache_diff_k, cache_diff_v, state_hgrn, c, c_ctx,
           norm_g, w_mod, b_mod, w_in, w_out, mla_q_norm_g, mla_w_uq, mla_kv_norm_g, mla_w_ukv, mla_nope_g,
           mla_rope_g, diff_qk_g, diff_lambda, diff_subln_g, hgrn_lb_logits, hgrn_out_g, hy_conv_w, hy_conv_b,
           hy_w1, hy_b1, hy_w2, hy_b2, hy_w3, hy_sin_freq, hy_bias):
    W = dict(norm_g=norm_g, w_out=w_out, mla_q_norm_g=mla_q_norm_g, mla_w_uq=mla_w_uq,
             mla_kv_norm_g=mla_kv_norm_g, mla_w_ukv=mla_w_ukv, mla_nope_g=mla_nope_g, mla_rope_g=mla_rope_g,
             diff_qk_g=diff_qk_g, diff_lambda=diff_lambda, diff_subln_g=diff_subln_g, hgrn_out_g=hgrn_out_g,
             hy_conv_w=hy_conv_w, hy_conv_b=hy_conv_b, hy_w1=hy_w1, hy_b1=hy_b1, hy_w2=hy_w2, hy_b2=hy_b2,
             hy_w3=hy_w3, hy_sin_freq=hy_sin_freq, hy_bias=hy_bias)
    bp, lp, _ = x_prompt.shape
    bs, ls, _ = x_sample.shape

    w_in_p = _reorder_in_cols(w_in.astype(BF16))
    cvecs = jnp.concatenate([c_ctx[None, :], c, jnp.zeros((8 - 1 - bs, D_MODEL), F32)], axis=0)
    mods = _mod_all(cvecs, w_mod, b_mod)
    lb = _hgrn_lb(hgrn_lb_logits)
    seg512, cnt512 = _mla_seg()
    hgrn_dd, hgrn_right = _hgrn_consts()
    consts = dict(seg512=seg512, cnt512=cnt512, seg32=_seg_const(BRANCH, DIFF_HD), seg64=_seg_const(BRANCH, HGRN_DK),
                  hgrn_dd=hgrn_dd, hgrn_right=hgrn_right)
    lws = [_layer_weights(l, w_in_p, lb, W) for l in range(DEPTH)]

    hy_p = _hy_tables(lp)
    y = x_prompt.reshape(bp * lp, D_MODEL)
    per_layer = []
    for l in range(DEPTH):
        mod = mods[l, 0:1].reshape(1, 3, D_MODEL)
        y, new = _layer(y, mod, lws[l], consts, bp, lp, None, None, hy_p)
        per_layer.append(new)
    y_prompt = y.reshape(bp, lp, D_MODEL)
    news = [jnp.stack([s[i] for s in per_layer], axis=1) for i in range(5)]

    hy_s = _hy_tables(ls)
    rope_tabs = _rope_tables(ls)
    y = x_sample.reshape(bs * ls, D_MODEL)
    past = cache_mla_ckv.shape[2]
    cache_kr = jnp.pad(cache_mla_krope, ((0, 0), (0, 0), (0, 0), (KR_OFF, LANE - KR_OFF - MLA_ROPE)))
    cache_kb = cache_diff_k.reshape(bs, DEPTH, past, BRANCH).astype(BF16)
    cache_vtb = _vt_with_ones(cache_diff_v.reshape(bs * DEPTH, past, BRANCH).astype(BF16))
    cache_vtb = cache_vtb.reshape(bs, DEPTH, DIFF_HEADS, VT_ROWS, past)
    for l in range(DEPTH):
        mod = mods[l, 1:1 + bs].reshape(bs, 3, D_MODEL)
        ctx = (cache_mla_ckv[:, l], cache_kr[:, l], cache_kb[:, l], cache_vtb[:, l], state_hgrn[:, l])
        y, _ = _layer(y, mod, lws[l], consts, bs, ls, ctx, rope_tabs, hy_s)
    y_sample = y.reshape(bs, ls, D_MODEL)

    return (y_prompt, y_sample, news[0], news[1], news[2], news[3], news[4])
```
